```python
import math
import jax, jax.numpy as jnp
from jax import lax
import numpy as np


D_MODEL = 1024
BATCH = 16
SEQ = 4096
DEPTH = 1

MIX_WIDTH = D_MODEL
SSD_WIDTH = MIX_WIDTH // 2
SSD_HEAD_DIM = 64
SSD_HEADS = SSD_WIDTH // SSD_HEAD_DIM
SSD_GROUPS = 2
SSD_STATE = 128
SSD_CONV = 4
SSD_CHUNK = 128
SSD_BC = SSD_GROUPS * SSD_STATE
CONV_CH = SSD_WIDTH + 2 * SSD_BC
S5_WIDTH = MIX_WIDTH - SSD_WIDTH
S5_GROUP_CH = 16
S5_GROUPS = S5_WIDTH // S5_GROUP_CH
S5_STATE = 64
IN_COLS = SSD_WIDTH + CONV_CH + SSD_HEADS + S5_WIDTH
D_FF = ((8 * D_MODEL // 3 + 255) // 256) * 256
N_MOD = 9
ALPHA = (2 * DEPTH) ** 0.25
BETA = (8 * DEPTH) ** -0.25
LN_EPS = 1e-5

kernel_name = 'hymba_ssd_s5_macaron_deepnorm'


def layer_norm(x, g, b):
    xf = x.astype(jnp.float32)
    mu = jnp.mean(xf, axis=-1, keepdims=True)
    var = jnp.mean(jnp.square(xf - mu), axis=-1, keepdims=True)
    return ((xf - mu) * lax.rsqrt(var + LN_EPS) * g + b).astype(x.dtype)


def modulate(x, shift, scale):
    return x * (1 + scale[:, None, :]) + shift[:, None, :]


def swiglu(u, w1, w3, w2):
    return (jax.nn.silu(u @ w1) * (u @ w3)) @ w2


def causal_dwconv(x, w, b):
    k = w.shape[0]
    y = lax.conv_general_dilated(x, w[:, None, :], window_strides=(1,), padding=[(k - 1, 0)],
                                 dimension_numbers=('NWC', 'WIO', 'NWC'),
                                 feature_group_count=x.shape[-1])
    return y + b


def ssd_chunked(xs, dt, a, bm, cm):
    bsz, s_len, n_h, p = xs.shape
    n_g, n_s = bm.shape[-2:]
    n_z = n_h // n_g
    l = SSD_CHUNK
    nc = s_len // l
    x = (xs * dt[..., None]).reshape(bsz, nc, l, n_g, n_z, p)
    a_dt = (dt * a).reshape(bsz, nc, l, n_g, n_z).transpose(0, 3, 4, 1, 2)
    bm = bm.reshape(bsz, nc, l, n_g, n_s)
    cm = cm.reshape(bsz, nc, l, n_g, n_s)
    a_cs = jnp.cumsum(a_dt, axis=-1)
    causal = jnp.tril(jnp.ones((l, l), dtype=bool))
    seg = a_cs[..., :, None] - a_cs[..., None, :]
    lmat = jnp.exp(jnp.where(causal, seg, -jnp.inf))
    cb = jnp.einsum('bclgn,bcsgn->bcgls', cm, bm)
    y_diag = jnp.einsum('bcgls,bgzcls,bcsgzp->bclgzp', cb, lmat, x)
    decay = jnp.exp(a_cs[..., -1:] - a_cs)
    states = jnp.einsum('bclgn,bgzcl,bclgzp->bcgzpn', bm, decay, x)
    chunk_decay = jnp.exp(a_cs[..., -1])

    def step(h, inp):
        s_c, d_c = inp
        return d_c[..., None, None] * h + s_c, h

    h0 = jnp.zeros((bsz, n_g, n_z, p, n_s), dtype=states.dtype)
    _, prev = lax.scan(step, h0, (jnp.moveaxis(states, 1, 0), jnp.moveaxis(chunk_decay, 3, 0)))
    prev = jnp.moveaxis(prev, 0, 1)
    y_off = jnp.einsum('bclgn,bcgzpn,bgzcl->bclgzp', cm, prev, jnp.exp(a_cs))
    return (y_diag + y_off).reshape(bsz, s_len, n_h, p)


def s5_mixer(u, a_re, a_im, log_dt, b_re, b_im, c_re, c_im, d, w_glu, b_glu):
    f32 = jnp.float32
    bsz, s_len, _ = u.shape
    uf = u.astype(f32).reshape(bsz, s_len, S5_GROUPS, S5_GROUP_CH)
    ar, ai = a_re.astype(f32), a_im.astype(f32)
    dt = jnp.exp(log_dt.astype(f32))[:, None]
    mag = jnp.exp(dt * ar)
    ab_re, ab_im = mag * jnp.cos(dt * ai), mag * jnp.sin(dt * ai)
    den = ar * ar + ai * ai
    nr, ni = ab_re - 1.0, ab_im
    f_re, f_im = (nr * ar + ni * ai) / den, (ni * ar - nr * ai) / den
    br, bi = b_re.astype(f32), b_im.astype(f32)
    bb_re = f_re[..., None] * br - f_im[..., None] * bi
    bb_im = f_re[..., None] * bi + f_im[..., None] * br
    bu_re = jnp.einsum('bsgh,gph->bsgp', uf, bb_re)
    bu_im = jnp.einsum('bsgh,gph->bsgp', uf, bb_im)
    a_seq_re = jnp.broadcast_to(ab_re, (1, s_len, S5_GROUPS, S5_STATE))
    a_seq_im = jnp.broadcast_to(ab_im, (1, s_len, S5_GROUPS, S5_STATE))

    def combine(e1, e2):
        a1r, a1i, b1r, b1i = e1
        a2r, a2i, b2r, b2i = e2
        return (a2r * a1r - a2i * a1i, a2r * a1i + a2i * a1r,
                a2r * b1r - a2i * b1i + b2r, a2r * b1i + a2i * b1r + b2i)

    _, _, xr, xi = lax.associative_scan(combine, (a_seq_re, a_seq_im, bu_re, bu_im), axis=1)
    y = (jnp.einsum('bsgp,ghp->bsgh', xr, c_re.astype(f32))
         - jnp.einsum('bsgp,ghp->bsgh', xi, c_im.astype(f32))
         + uf * d.astype(f32).reshape(S5_GROUPS, S5_GROUP_CH))
    y = y.reshape(bsz, s_len, S5_WIDTH)
    g = jax.nn.gelu(y)
    out = g * jax.nn.sigmoid(g @ w_glu.astype(f32) + b_glu.astype(f32))
    return out.astype(u.dtype)


def hybrid_mixer(h, w_in, conv_w, conv_b, dt_bias, a_log, d_ssd, ssd_norm_w,
                 s5_a_re, s5_a_im, s5_log_dt, s5_b_re, s5_b_im, s5_c_re, s5_c_im, s5_d,
                 w_glu, b_glu, w_out):
    f32 = jnp.float32
    bsz, s_len, _ = h.shape
    proj = h @ w_in
    z, xbc, dt_raw, u = jnp.split(proj, [SSD_WIDTH, SSD_WIDTH + CONV_CH,
                                         SSD_WIDTH + CONV_CH + SSD_HEADS], axis=-1)
    xbc = jax.nn.silu(causal_dwconv(xbc, conv_w, conv_b))
    xs, bm, cm = jnp.split(xbc.astype(f32), [SSD_WIDTH, SSD_WIDTH + SSD_BC], axis=-1)
    dt = jax.nn.softplus(dt_raw.astype(f32) + dt_bias.astype(f32))
    a = -jnp.exp(a_log.astype(f32))
    xs = xs.reshape(bsz, s_len, SSD_HEADS, SSD_HEAD_DIM)
    y = ssd_chunked(xs, dt, a,
                    bm.reshape(bsz, s_len, SSD_GROUPS, SSD_STATE),
                    cm.reshape(bsz, s_len, SSD_GROUPS, SSD_STATE))
    y = y + d_ssd.astype(f32)[:, None] * xs
    y = y.reshape(bsz, s_len, SSD_WIDTH) * jax.nn.silu(z.astype(f32))
    yg = y.reshape(bsz, s_len, SSD_GROUPS, SSD_WIDTH // SSD_GROUPS)
    yg = yg * lax.rsqrt(jnp.mean(jnp.square(yg), axis=-1, keepdims=True) + LN_EPS)
    y_ssd = (yg.reshape(bsz, s_len, SSD_WIDTH) * ssd_norm_w.astype(f32)).astype(h.dtype)
    y_s5 = s5_mixer(u, s5_a_re, s5_a_im, s5_log_dt, s5_b_re, s5_b_im, s5_c_re, s5_c_im,
                    s5_d, w_glu, b_glu)
    return jnp.concatenate([y_ssd, y_s5], axis=-1) @ w_out


def _fwd_setup_inputs(seed: int = 0) -> dict:
    key = jax.random.key(seed)
    ks = iter(jax.random.split(key, 48))
    f32 = jnp.float32
    nl = DEPTH

    def nrm(shape, std):
        return std * jax.random.normal(next(ks), shape, f32)

    def unif(shape, lo, hi):
        return jax.random.uniform(next(ks), shape, f32, minval=lo, maxval=hi)

    x = nrm((BATCH, SEQ, D_MODEL), 1.0)
    c = nrm((BATCH, D_MODEL), 1.0)
    w_ada = nrm((nl, D_MODEL, N_MOD * D_MODEL), 0.5 * D_MODEL ** -0.5)
    b_ada = nrm((nl, N_MOD * D_MODEL), 0.02)
    ffn1_w1 = nrm((nl, D_MODEL, D_FF), D_MODEL ** -0.5)
    ffn1_w3 = nrm((nl, D_MODEL, D_FF), D_MODEL ** -0.5)
    ffn1_w2 = nrm((nl, D_FF, D_MODEL), BETA * D_FF ** -0.5)
    ln1_g = 1.0 + nrm((nl, D_MODEL), 0.02)
    ln1_b = nrm((nl, D_MODEL), 0.02)
    w_in = nrm((nl, D_MODEL, IN_COLS), D_MODEL ** -0.5)
    conv_w = nrm((nl, SSD_CONV, CONV_CH), SSD_CONV ** -0.5)
    conv_b = nrm((nl, CONV_CH), 0.01)
    dt0 = jnp.exp(unif((nl, SSD_HEADS), math.log(1e-3), math.log(1e-1)))
    dt_bias = dt0 + jnp.log(-jnp.expm1(-dt0))
    a_log = jnp.log(unif((nl, SSD_HEADS), 1.0, 16.0))
    d_ssd = 1.0 + nrm((nl, SSD_HEADS), 0.1)
    ssd_norm_w = 1.0 + nrm((nl, SSD_WIDTH), 0.02)
    s5_a_re = -0.5 + nrm((nl, S5_GROUPS, S5_STATE), 0.01)
    s5_a_im = math.pi * jnp.arange(S5_STATE, dtype=f32)[None, None, :] + nrm((nl, S5_GROUPS, S5_STATE), 0.01)
    s5_log_dt = unif((nl, S5_GROUPS), math.log(1e-3), math.log(1e-1))
    s5_b_re = nrm((nl, S5_GROUPS, S5_STATE, S5_GROUP_CH), (2 * S5_GROUP_CH) ** -0.5)
    s5_b_im = nrm((nl, S5_GROUPS, S5_STATE, S5_GROUP_CH), (2 * S5_GROUP_CH) ** -0.5)
    s5_c_re = nrm((nl, S5_GROUPS, S5_GROUP_CH, S5_STATE), S5_STATE ** -0.5)
    s5_c_im = nrm((nl, S5_GROUPS, S5_GROUP_CH, S5_STATE), S5_STATE ** -0.5)
    s5_d = nrm((nl, S5_WIDTH), 1.0)
    w_glu = nrm((nl, S5_WIDTH, S5_WIDTH), S5_WIDTH ** -0.5)
    b_glu = nrm((nl, S5_WIDTH), 0.01)
    w_out = nrm((nl, MIX_WIDTH, D_MODEL), BETA * MIX_WIDTH ** -0.5)
    ln2_g = 1.0 + nrm((nl, D_MODEL), 0.02)
    ln2_b = nrm((nl, D_MODEL), 0.02)
    ffn2_w1 = nrm((nl, D_MODEL, D_FF), D_MODEL ** -0.5)
    ffn2_w3 = nrm((nl, D_MODEL, D_FF), D_MODEL ** -0.5)
    ffn2_w2 = nrm((nl, D_FF, D_MODEL), BETA * D_FF ** -0.5)
    ln3_g = 1.0 + nrm((nl, D_MODEL), 0.02)
    ln3_b = nrm((nl, D_MODEL), 0.02)
    return {'x': x, 'c': c, 'w_ada': w_ada, 'b_ada': b_ada,
            'ffn1_w1': ffn1_w1, 'ffn1_w3': ffn1_w3, 'ffn1_w2': ffn1_w2, 'ln1_g': ln1_g, 'ln1_b': ln1_b,
            'w_in': w_in, 'conv_w': conv_w, 'conv_b': conv_b, 'dt_bias': dt_bias, 'a_log': a_log,
            'd_ssd': d_ssd, 'ssd_norm_w': ssd_norm_w, 's5_a_re': s5_a_re, 's5_a_im': s5_a_im,
            's5_log_dt': s5_log_dt, 's5_b_re': s5_b_re, 's5_b_im': s5_b_im, 's5_c_re': s5_c_re,
            's5_c_im': s5_c_im, 's5_d': s5_d, 'w_glu': w_glu, 'b_glu': b_glu, 'w_out': w_out,
            'ln2_g': ln2_g, 'ln2_b': ln2_b, 'ffn2_w1': ffn2_w1, 'ffn2_w3': ffn2_w3, 'ffn2_w2': ffn2_w2,
            'ln3_g': ln3_g, 'ln3_b': ln3_b}


def _fwd_reference(x, c, w_ada, b_ada, ffn1_w1, ffn1_w3, ffn1_w2, ln1_g, ln1_b,
              w_in, conv_w, conv_b, dt_bias, a_log, d_ssd, ssd_norm_w, s5_a_re, s5_a_im,
              s5_log_dt, s5_b_re, s5_b_im, s5_c_re, s5_c_im, s5_d, w_glu, b_glu, w_out,
              ln2_g, ln2_b, ffn2_w1, ffn2_w3, ffn2_w2, ln3_g, ln3_b):
    bsz = x.shape[0]
    cs = jax.nn.silu(c)
    for l in range(DEPTH):
        mod = (cs @ w_ada[l] + b_ada[l]).reshape(bsz, N_MOD, D_MODEL)
        sh1, sc1, g1 = mod[:, 0], mod[:, 1], mod[:, 2]
        sh2, sc2, g2 = mod[:, 3], mod[:, 4], mod[:, 5]
        sh3, sc3, g3 = mod[:, 6], mod[:, 7], mod[:, 8]
        h = modulate(x, sh1, sc1)
        x = layer_norm(ALPHA * x + 0.5 * g1[:, None, :] * swiglu(h, ffn1_w1[l], ffn1_w3[l], ffn1_w2[l]),
                       ln1_g[l], ln1_b[l])
        h = modulate(x, sh2, sc2)
        m = hybrid_mixer(h, w_in[l], conv_w[l], conv_b[l], dt_bias[l], a_log[l], d_ssd[l],
                         ssd_norm_w[l], s5_a_re[l], s5_a_im[l], s5_log_dt[l], s5_b_re[l],
                         s5_b_im[l], s5_c_re[l], s5_c_im[l], s5_d[l], w_glu[l], b_glu[l], w_out[l])
        x = layer_norm(ALPHA * x + g2[:, None, :] * m, ln2_g[l], ln2_b[l])
        h = modulate(x, sh3, sc3)
        x = layer_norm(ALPHA * x + 0.5 * g3[:, None, :] * swiglu(h, ffn2_w1[l], ffn2_w3[l], ffn2_w2[l]),
                       ln3_g[l], ln3_b[l])
    return x


import jax as _jax
import jax.numpy as _jnp

TWIN_FORMAT = 'train_step'
FWD_PARAMS = ['x', 'c', 'w_ada', 'b_ada', 'ffn1_w1', 'ffn1_w3', 'ffn1_w2', 'ln1_g', 'ln1_b', 'w_in', 'conv_w', 'conv_b', 'dt_bias', 'a_log', 'd_ssd', 'ssd_norm_w', 's5_a_re', 's5_a_im', 's5_log_dt', 's5_b_re', 's5_b_im', 's5_c_re', 's5_c_im', 's5_d', 'w_glu', 'b_glu', 'w_out', 'ln2_g', 'ln2_b', 'ffn2_w1', 'ffn2_w3', 'ffn2_w2', 'ln3_g', 'ln3_b']
TWIN_WEIGHTS = ['w_ada', 'b_ada', 'ffn1_w1', 'ffn1_w3', 'ffn1_w2', 'ln1_g', 'ln1_b', 'w_in', 'conv_w', 'conv_b', 'dt_bias', 'a_log', 'd_ssd', 'ssd_norm_w', 's5_a_re', 's5_a_im', 's5_log_dt', 's5_b_re', 's5_b_im', 's5_c_re', 's5_c_im', 's5_d', 'w_glu', 'b_glu', 'w_out', 'ln2_g', 'ln2_b', 'ffn2_w1', 'ffn2_w3', 'ffn2_w2', 'ln3_g', 'ln3_b']
TWIN_DIFF_INPUT = 'x'
TWIN_INPUTS = ['x', 'c', 'w_ada', 'b_ada', 'ffn1_w1', 'ffn1_w3', 'ffn1_w2', 'ln1_g', 'ln1_b', 'w_in', 'conv_w', 'conv_b', 'dt_bias', 'a_log', 'd_ssd', 'ssd_norm_w', 's5_a_re', 's5_a_im', 's5_log_dt', 's5_b_re', 's5_b_im', 's5_c_re', 's5_c_im', 's5_d', 'w_glu', 'b_glu', 'w_out', 'ln2_g', 'ln2_b', 'ffn2_w1', 'ffn2_w3', 'ffn2_w2', 'ln3_g', 'ln3_b', 'loss_target', 'm_w_ada', 'm_b_ada', 'm_ffn1_w1', 'm_ffn1_w3', 'm_ffn1_w2', 'm_ln1_g', 'm_ln1_b', 'm_w_in', 'm_conv_w', 'm_conv_b', 'm_dt_bias', 'm_a_log', 'm_d_ssd', 'm_ssd_norm_w', 'm_s5_a_re', 'm_s5_a_im', 'm_s5_log_dt', 'm_s5_b_re', 'm_s5_b_im', 'm_s5_c_re', 'm_s5_c_im', 'm_s5_d', 'm_w_glu', 'm_b_glu', 'm_w_out', 'm_ln2_g', 'm_ln2_b', 'm_ffn2_w1', 'm_ffn2_w3', 'm_ffn2_w2', 'm_ln3_g', 'm_ln3_b', 'v_w_ada', 'v_b_ada', 'v_ffn1_w1', 'v_ffn1_w3', 'v_ffn1_w2', 'v_ln1_g', 'v_ln1_b', 'v_w_in', 'v_conv_w', 'v_conv_b', 'v_dt_bias', 'v_a_log', 'v_d_ssd', 'v_ssd_norm_w', 'v_s5_a_re', 'v_s5_a_im', 'v_s5_log_dt', 'v_s5_b_re', 'v_s5_b_im', 'v_s5_c_re', 'v_s5_c_im', 'v_s5_d', 'v_w_glu', 'v_b_glu', 'v_w_out', 'v_ln2_g', 'v_ln2_b', 'v_ffn2_w1', 'v_ffn2_w3', 'v_ffn2_w2', 'v_ln3_g', 'v_ln3_b']
TWIN_OUTPUTS = ['loss', 'grad_x', 'grad_w_ada', 'grad_b_ada', 'grad_ffn1_w1', 'grad_ffn1_w3', 'grad_ffn1_w2', 'grad_ln1_g', 'grad_ln1_b', 'grad_w_in', 'grad_conv_w', 'grad_conv_b', 'grad_dt_bias', 'grad_a_log', 'grad_d_ssd', 'grad_ssd_norm_w', 'grad_s5_a_re', 'grad_s5_a_im', 'grad_s5_log_dt', 'grad_s5_b_re', 'grad_s5_b_im', 'grad_s5_c_re', 'grad_s5_c_im', 'grad_s5_d', 'grad_w_glu', 'grad_b_glu', 'grad_w_out', 'grad_ln2_g', 'grad_ln2_b', 'grad_ffn2_w1', 'grad_ffn2_w3', 'grad_ffn2_w2', 'grad_ln3_g', 'grad_ln3_b', 'delta_w_ada', 'delta_b_ada', 'delta_ffn1_w1', 'delta_ffn1_w3', 'delta_ffn1_w2', 'delta_ln1_g', 'delta_ln1_b', 'delta_w_in', 'delta_conv_w', 'delta_conv_b', 'delta_dt_bias', 'delta_a_log', 'delta_d_ssd', 'delta_ssd_norm_w', 'delta_s5_a_re', 'delta_s5_a_im', 'delta_s5_log_dt', 'delta_s5_b_re', 'delta_s5_b_im', 'delta_s5_c_re', 'delta_s5_c_im', 'delta_s5_d', 'delta_w_glu', 'delta_b_glu', 'delta_w_out', 'delta_ln2_g', 'delta_ln2_b', 'delta_ffn2_w1', 'delta_ffn2_w3', 'delta_ffn2_w2', 'delta_ln3_g', 'delta_ln3_b', 'new_m_w_ada', 'new_m_b_ada', 'new_m_ffn1_w1', 'new_m_ffn1_w3', 'new_m_ffn1_w2', 'new_m_ln1_g', 'new_m_ln1_b', 'new_m_w_in', 'new_m_conv_w', 'new_m_conv_b', 'new_m_dt_bias', 'new_m_a_log', 'new_m_d_ssd', 'new_m_ssd_norm_w', 'new_m_s5_a_re', 'new_m_s5_a_im', 'new_m_s5_log_dt', 'new_m_s5_b_re', 'new_m_s5_b_im', 'new_m_s5_c_re', 'new_m_s5_c_im', 'new_m_s5_d', 'new_m_w_glu', 'new_m_b_glu', 'new_m_w_out', 'new_m_ln2_g', 'new_m_ln2_b', 'new_m_ffn2_w1', 'new_m_ffn2_w3', 'new_m_ffn2_w2', 'new_m_ln3_g', 'new_m_ln3_b', 'new_v_w_ada', 'new_v_b_ada', 'new_v_ffn1_w1', 'new_v_ffn1_w3', 'new_v_ffn1_w2', 'new_v_ln1_g', 'new_v_ln1_b', 'new_v_w_in', 'new_v_conv_w', 'new_v_conv_b', 'new_v_dt_bias', 'new_v_a_log', 'new_v_d_ssd', 'new_v_ssd_norm_w', 'new_v_s5_a_re', 'new_v_s5_a_im', 'new_v_s5_log_dt', 'new_v_s5_b_re', 'new_v_s5_b_im', 'new_v_s5_c_re', 'new_v_s5_c_im', 'new_v_s5_d', 'new_v_w_glu', 'new_v_b_glu', 'new_v_w_out', 'new_v_ln2_g', 'new_v_ln2_b', 'new_v_ffn2_w1', 'new_v_ffn2_w3', 'new_v_ffn2_w2', 'new_v_ln3_g', 'new_v_ln3_b']
TWIN_LEAF_KINDS = {'loss': 'loss', 'grad_x': 'grad_x', 'grad_w_ada': 'grad_w', 'grad_b_ada': 'grad_w', 'grad_ffn1_w1': 'grad_w', 'grad_ffn1_w3': 'grad_w', 'grad_ffn1_w2': 'grad_w', 'grad_ln1_g': 'grad_w', 'grad_ln1_b': 'grad_w', 'grad_w_in': 'grad_w', 'grad_conv_w': 'grad_w', 'grad_conv_b': 'grad_w', 'grad_dt_bias': 'grad_w', 'grad_a_log': 'grad_w', 'grad_d_ssd': 'grad_w', 'grad_ssd_norm_w': 'grad_w', 'grad_s5_a_re': 'grad_w', 'grad_s5_a_im': 'grad_w', 'grad_s5_log_dt': 'grad_w', 'grad_s5_b_re': 'grad_w', 'grad_s5_b_im': 'grad_w', 'grad_s5_c_re': 'grad_w', 'grad_s5_c_im': 'grad_w', 'grad_s5_d': 'grad_w', 'grad_w_glu': 'grad_w', 'grad_b_glu': 'grad_w', 'grad_w_out': 'grad_w', 'grad_ln2_g': 'grad_w', 'grad_ln2_b': 'grad_w', 'grad_ffn2_w1': 'grad_w', 'grad_ffn2_w3': 'grad_w', 'grad_ffn2_w2': 'grad_w', 'grad_ln3_g': 'grad_w', 'grad_ln3_b': 'grad_w', 'delta_w_ada': 'delta_w', 'delta_b_ada': 'delta_w', 'delta_ffn1_w1': 'delta_w', 'delta_ffn1_w3': 'delta_w', 'delta_ffn1_w2': 'delta_w', 'delta_ln1_g': 'delta_w', 'delta_ln1_b': 'delta_w', 'delta_w_in': 'delta_w', 'delta_conv_w': 'delta_w', 'delta_conv_b': 'delta_w', 'delta_dt_bias': 'delta_w', 'delta_a_log': 'delta_w', 'delta_d_ssd': 'delta_w', 'delta_ssd_norm_w': 'delta_w', 'delta_s5_a_re': 'delta_w', 'delta_s5_a_im': 'delta_w', 'delta_s5_log_dt': 'delta_w', 'delta_s5_b_re': 'delta_w', 'delta_s5_b_im': 'delta_w', 'delta_s5_c_re': 'delta_w', 'delta_s5_c_im': 'delta_w', 'delta_s5_d': 'delta_w', 'delta_w_glu': 'delta_w', 'delta_b_glu': 'delta_w', 'delta_w_out': 'delta_w', 'delta_ln2_g': 'delta_w', 'delta_ln2_b': 'delta_w', 'delta_ffn2_w1': 'delta_w', 'delta_ffn2_w3': 'delta_w', 'delta_ffn2_w2': 'delta_w', 'delta_ln3_g': 'delta_w', 'delta_ln3_b': 'delta_w', 'new_m_w_ada': 'new_m', 'new_m_b_ada': 'new_m', 'new_m_ffn1_w1': 'new_m', 'new_m_ffn1_w3': 'new_m', 'new_m_ffn1_w2': 'new_m', 'new_m_ln1_g': 'new_m', 'new_m_ln1_b': 'new_m', 'new_m_w_in': 'new_m', 'new_m_conv_w': 'new_m', 'new_m_conv_b': 'new_m', 'new_m_dt_bias': 'new_m', 'new_m_a_log': 'new_m', 'new_m_d_ssd': 'new_m', 'new_m_ssd_norm_w': 'new_m', 'new_m_s5_a_re': 'new_m', 'new_m_s5_a_im': 'new_m', 'new_m_s5_log_dt': 'new_m', 'new_m_s5_b_re': 'new_m', 'new_m_s5_b_im': 'new_m', 'new_m_s5_c_re': 'new_m', 'new_m_s5_c_im': 'new_m', 'new_m_s5_d': 'new_m', 'new_m_w_glu': 'new_m', 'new_m_b_glu': 'new_m', 'new_m_w_out': 'new_m', 'new_m_ln2_g': 'new_m', 'new_m_ln2_b': 'new_m', 'new_m_ffn2_w1': 'new_m', 'new_m_ffn2_w3': 'new_m', 'new_m_ffn2_w2': 'new_m', 'new_m_ln3_g': 'new_m', 'new_m_ln3_b': 'new_m', 'new_v_w_ada': 'new_v', 'new_v_b_ada': 'new_v', 'new_v_ffn1_w1': 'new_v', 'new_v_ffn1_w3': 'new_v', 'new_v_ffn1_w2': 'new_v', 'new_v_ln1_g': 'new_v', 'new_v_ln1_b': 'new_v', 'new_v_w_in': 'new_v', 'new_v_conv_w': 'new_v', 'new_v_conv_b': 'new_v', 'new_v_dt_bias': 'new_v', 'new_v_a_log': 'new_v', 'new_v_d_ssd': 'new_v', 'new_v_ssd_norm_w': 'new_v', 'new_v_s5_a_re': 'new_v', 'new_v_s5_a_im': 'new_v', 'new_v_s5_log_dt': 'new_v', 'new_v_s5_b_re': 'new_v', 'new_v_s5_b_im': 'new_v', 'new_v_s5_c_re': 'new_v', 'new_v_s5_c_im': 'new_v', 'new_v_s5_d': 'new_v', 'new_v_w_glu': 'new_v', 'new_v_b_glu': 'new_v', 'new_v_w_out': 'new_v', 'new_v_ln2_g': 'new_v', 'new_v_ln2_b': 'new_v', 'new_v_ffn2_w1': 'new_v', 'new_v_ffn2_w3': 'new_v', 'new_v_ffn2_w2': 'new_v', 'new_v_ln3_g': 'new_v', 'new_v_ln3_b': 'new_v'}


def _forward(args):
    return _fwd_reference(*[args[k] for k in FWD_PARAMS])


def _output_shape():
    out = _jax.eval_shape(lambda: _forward(_fwd_setup_inputs(0)))
    return out.shape, out.dtype

N_MICROBATCH = 1
ADAM_LR = 0.001
ADAM_B1 = 0.9
ADAM_B2 = 0.999
ADAM_EPS = 1e-08
ADAM_WD = 0.01
ADAM_STEP = 10
PER_EXAMPLE_BATCH_AXIS = {'x': 0, 'c': 0, 'loss_target': 0}
SHARED_INPUTS = []
_WEIGHT_DTYPES = {'w_ada': _jnp.float32, 'b_ada': _jnp.float32, 'ffn1_w1': _jnp.float32, 'ffn1_w3': _jnp.float32, 'ffn1_w2': _jnp.float32, 'ln1_g': _jnp.float32, 'ln1_b': _jnp.float32, 'w_in': _jnp.float32, 'conv_w': _jnp.float32, 'conv_b': _jnp.float32, 'dt_bias': _jnp.float32, 'a_log': _jnp.float32, 'd_ssd': _jnp.float32, 'ssd_norm_w': _jnp.float32, 's5_a_re': _jnp.float32, 's5_a_im': _jnp.float32, 's5_log_dt': _jnp.float32, 's5_b_re': _jnp.float32, 's5_b_im': _jnp.float32, 's5_c_re': _jnp.float32, 's5_c_im': _jnp.float32, 's5_d': _jnp.float32, 'w_glu': _jnp.float32, 'b_glu': _jnp.float32, 'w_out': _jnp.float32, 'ln2_g': _jnp.float32, 'ln2_b': _jnp.float32, 'ffn2_w1': _jnp.float32, 'ffn2_w3': _jnp.float32, 'ffn2_w2': _jnp.float32, 'ln3_g': _jnp.float32, 'ln3_b': _jnp.float32}
MOMENT_SCALE = {'w_ada': 2.904595e-02, 'b_ada': 4.840176e-02, 'ffn1_w1': 8.622200e-03, 'ffn1_w3': 8.363124e-03, 'ffn1_w2': 2.331802e-02, 'ln1_g': 1.994451e+00, 'ln1_b': 6.809682e-01, 'w_in': 3.006413e-02, 'conv_w': 2.999926e-02, 'conv_b': 4.254740e-02, 'dt_bias': 5.498699e-02, 'a_log': 4.156427e-01, 'd_ssd': 1.838553e-01, 'ssd_norm_w': 4.331624e-02, 's5_a_re': 1.531883e-03, 's5_a_im': 2.265346e-03, 's5_log_dt': 5.627277e-01, 's5_b_re': 1.150672e-03, 's5_b_im': 1.089316e-03, 's5_c_re': 1.683215e-03, 's5_c_im': 1.587836e-03, 's5_d': 1.907546e-02, 'w_glu': 5.039796e-03, 'b_glu': 7.109532e-03, 'w_out': 5.256794e-02, 'ln2_g': 1.986862e+00, 'ln2_b': 6.865391e-01, 'ffn2_w1': 8.653817e-03, 'ffn2_w3': 8.401182e-03, 'ffn2_w2': 2.350638e-02, 'ln3_g': 6.395302e+01, 'ln3_b': 1.574099e+00}


def _to_microbatches(a, axis):
    t = _jnp.moveaxis(a, axis, 0)
    t = t.reshape((N_MICROBATCH, t.shape[0] // N_MICROBATCH) + t.shape[1:])
    return _jnp.moveaxis(t, 1, axis + 1)


def setup_inputs(seed: int = 0) -> dict:
    inp = _fwd_setup_inputs(seed)
    key = _jax.random.fold_in(_jax.random.key(seed), 7919)
    shape, _ = _output_shape()
    out = dict(inp)
    out["loss_target"] = _jax.random.normal(_jax.random.fold_in(key, 0), shape, _jnp.float32)
    for i, name in enumerate(TWIN_WEIGHTS):
        w = inp[name].astype(_jnp.float32)
        if MOMENT_SCALE is None:
            s = _jnp.sqrt(_jnp.mean(_jnp.square(w)) + 1e-30)
        else:
            s = MOMENT_SCALE[name]
        km, kv = _jax.random.split(_jax.random.fold_in(key, i + 1))
        out[name] = w
        out["m_" + name] = s * _jax.random.normal(km, w.shape, _jnp.float32)
        out["v_" + name] = (s * s) * _jax.random.uniform(kv, w.shape, _jnp.float32, 0.5, 1.5)
    if N_MICROBATCH > 1:
        for name, axis in PER_EXAMPLE_BATCH_AXIS.items():
            out[name] = _to_microbatches(out[name], axis)
    return {'x': out['x'], 'c': out['c'], 'w_ada': out['w_ada'], 'b_ada': out['b_ada'], 'ffn1_w1': out['ffn1_w1'], 'ffn1_w3': out['ffn1_w3'], 'ffn1_w2': out['ffn1_w2'], 'ln1_g': out['ln1_g'], 'ln1_b': out['ln1_b'], 'w_in': out['w_in'], 'conv_w': out['conv_w'], 'conv_b': out['conv_b'], 'dt_bias': out['dt_bias'], 'a_log': out['a_log'], 'd_ssd': out['d_ssd'], 'ssd_norm_w': out['ssd_norm_w'], 's5_a_re': out['s5_a_re'], 's5_a_im': out['s5_a_im'], 's5_log_dt': out['s5_log_dt'], 's5_b_re': out['s5_b_re'], 's5_b_im': out['s5_b_im'], 's5_c_re': out['s5_c_re'], 's5_c_im': out['s5_c_im'], 's5_d': out['s5_d'], 'w_glu': out['w_glu'], 'b_glu': out['b_glu'], 'w_out': out['w_out'], 'ln2_g': out['ln2_g'], 'ln2_b': out['ln2_b'], 'ffn2_w1': out['ffn2_w1'], 'ffn2_w3': out['ffn2_w3'], 'ffn2_w2': out['ffn2_w2'], 'ln3_g': out['ln3_g'], 'ln3_b': out['ln3_b'], 'loss_target': out['loss_target'], 'm_w_ada': out['m_w_ada'], 'm_b_ada': out['m_b_ada'], 'm_ffn1_w1': out['m_ffn1_w1'], 'm_ffn1_w3': out['m_ffn1_w3'], 'm_ffn1_w2': out['m_ffn1_w2'], 'm_ln1_g': out['m_ln1_g'], 'm_ln1_b': out['m_ln1_b'], 'm_w_in': out['m_w_in'], 'm_conv_w': out['m_conv_w'], 'm_conv_b': out['m_conv_b'], 'm_dt_bias': out['m_dt_bias'], 'm_a_log': out['m_a_log'], 'm_d_ssd': out['m_d_ssd'], 'm_ssd_norm_w': out['m_ssd_norm_w'], 'm_s5_a_re': out['m_s5_a_re'], 'm_s5_a_im': out['m_s5_a_im'], 'm_s5_log_dt': out['m_s5_log_dt'], 'm_s5_b_re': out['m_s5_b_re'], 'm_s5_b_im': out['m_s5_b_im'], 'm_s5_c_re': out['m_s5_c_re'], 'm_s5_c_im': out['m_s5_c_im'], 'm_s5_d': out['m_s5_d'], 'm_w_glu': out['m_w_glu'], 'm_b_glu': out['m_b_glu'], 'm_w_out': out['m_w_out'], 'm_ln2_g': out['m_ln2_g'], 'm_ln2_b': out['m_ln2_b'], 'm_ffn2_w1': out['m_ffn2_w1'], 'm_ffn2_w3': out['m_ffn2_w3'], 'm_ffn2_w2': out['m_ffn2_w2'], 'm_ln3_g': out['m_ln3_g'], 'm_ln3_b': out['m_ln3_b'], 'v_w_ada': out['v_w_ada'], 'v_b_ada': out['v_b_ada'], 'v_ffn1_w1': out['v_ffn1_w1'], 'v_ffn1_w3': out['v_ffn1_w3'], 'v_ffn1_w2': out['v_ffn1_w2'], 'v_ln1_g': out['v_ln1_g'], 'v_ln1_b': out['v_ln1_b'], 'v_w_in': out['v_w_in'], 'v_conv_w': out['v_conv_w'], 'v_conv_b': out['v_conv_b'], 'v_dt_bias': out['v_dt_bias'], 'v_a_log': out['v_a_log'], 'v_d_ssd': out['v_d_ssd'], 'v_ssd_norm_w': out['v_ssd_norm_w'], 'v_s5_a_re': out['v_s5_a_re'], 'v_s5_a_im': out['v_s5_a_im'], 'v_s5_log_dt': out['v_s5_log_dt'], 'v_s5_b_re': out['v_s5_b_re'], 'v_s5_b_im': out['v_s5_b_im'], 'v_s5_c_re': out['v_s5_c_re'], 'v_s5_c_im': out['v_s5_c_im'], 'v_s5_d': out['v_s5_d'], 'v_w_glu': out['v_w_glu'], 'v_b_glu': out['v_b_glu'], 'v_w_out': out['v_w_out'], 'v_ln2_g': out['v_ln2_g'], 'v_ln2_b': out['v_ln2_b'], 'v_ffn2_w1': out['v_ffn2_w1'], 'v_ffn2_w3': out['v_ffn2_w3'], 'v_ffn2_w2': out['v_ffn2_w2'], 'v_ln3_g': out['v_ln3_g'], 'v_ln3_b': out['v_ln3_b']}


def _loss(weights, diff, rest, loss_target):
    with _jax.named_scope("forward"):
        args = {**rest, TWIN_DIFF_INPUT: diff, **{k: w.astype(_WEIGHT_DTYPES[k]) for k, w in weights.items()}}
        y = _forward(args)
    with _jax.named_scope("loss_head"):
        err = _jnp.square(y.astype(_jnp.float32) - loss_target)
        return 0.5 * _jnp.sum(_jnp.mean(err, axis=-1)) if err.ndim else 0.5 * err


def _adamw(w, g, m, v):
    m = ADAM_B1 * m + (1.0 - ADAM_B1) * g
    v = ADAM_B2 * v + (1.0 - ADAM_B2) * _jnp.square(g)
    m_hat = m / (1.0 - ADAM_B1 ** ADAM_STEP)
    v_hat = v / (1.0 - ADAM_B2 ** ADAM_STEP)
    delta = -ADAM_LR * (m_hat / (_jnp.sqrt(v_hat) + ADAM_EPS) + ADAM_WD * w)
    return delta, m, v


def reference(x, c, w_ada, b_ada, ffn1_w1, ffn1_w3, ffn1_w2, ln1_g, ln1_b, w_in, conv_w, conv_b, dt_bias, a_log, d_ssd, ssd_norm_w, s5_a_re, s5_a_im, s5_log_dt, s5_b_re, s5_b_im, s5_c_re, s5_c_im, s5_d, w_glu, b_glu, w_out, ln2_g, ln2_b, ffn2_w1, ffn2_w3, ffn2_w2, ln3_g, ln3_b, loss_target, m_w_ada, m_b_ada, m_ffn1_w1, m_ffn1_w3, m_ffn1_w2, m_ln1_g, m_ln1_b, m_w_in, m_conv_w, m_conv_b, m_dt_bias, m_a_log, m_d_ssd, m_ssd_norm_w, m_s5_a_re, m_s5_a_im, m_s5_log_dt, m_s5_b_re, m_s5_b_im, m_s5_c_re, m_s5_c_im, m_s5_d, m_w_glu, m_b_glu, m_w_out, m_ln2_g, m_ln2_b, m_ffn2_w1, m_ffn2_w3, m_ffn2_w2, m_ln3_g, m_ln3_b, v_w_ada, v_b_ada, v_ffn1_w1, v_ffn1_w3, v_ffn1_w2, v_ln1_g, v_ln1_b, v_w_in, v_conv_w, v_conv_b, v_dt_bias, v_a_log, v_d_ssd, v_ssd_norm_w, v_s5_a_re, v_s5_a_im, v_s5_log_dt, v_s5_b_re, v_s5_b_im, v_s5_c_re, v_s5_c_im, v_s5_d, v_w_glu, v_b_glu, v_w_out, v_ln2_g, v_ln2_b, v_ffn2_w1, v_ffn2_w3, v_ffn2_w2, v_ln3_g, v_ln3_b):
    given = dict(x=x, c=c, w_ada=w_ada, b_ada=b_ada, ffn1_w1=ffn1_w1, ffn1_w3=ffn1_w3, ffn1_w2=ffn1_w2, ln1_g=ln1_g, ln1_b=ln1_b, w_in=w_in, conv_w=conv_w, conv_b=conv_b, dt_bias=dt_bias, a_log=a_log, d_ssd=d_ssd, ssd_norm_w=ssd_norm_w, s5_a_re=s5_a_re, s5_a_im=s5_a_im, s5_log_dt=s5_log_dt, s5_b_re=s5_b_re, s5_b_im=s5_b_im, s5_c_re=s5_c_re, s5_c_im=s5_c_im, s5_d=s5_d, w_glu=w_glu, b_glu=b_glu, w_out=w_out, ln2_g=ln2_g, ln2_b=ln2_b, ffn2_w1=ffn2_w1, ffn2_w3=ffn2_w3, ffn2_w2=ffn2_w2, ln3_g=ln3_g, ln3_b=ln3_b, loss_target=loss_target, m_w_ada=m_w_ada, m_b_ada=m_b_ada, m_ffn1_w1=m_ffn1_w1, m_ffn1_w3=m_ffn1_w3, m_ffn1_w2=m_ffn1_w2, m_ln1_g=m_ln1_g, m_ln1_b=m_ln1_b, m_w_in=m_w_in, m_conv_w=m_conv_w, m_conv_b=m_conv_b, m_dt_bias=m_dt_bias, m_a_log=m_a_log, m_d_ssd=m_d_ssd, m_ssd_norm_w=m_ssd_norm_w, m_s5_a_re=m_s5_a_re, m_s5_a_im=m_s5_a_im, m_s5_log_dt=m_s5_log_dt, m_s5_b_re=m_s5_b_re, m_s5_b_im=m_s5_b_im, m_s5_c_re=m_s5_c_re, m_s5_c_im=m_s5_c_im, m_s5_d=m_s5_d, m_w_glu=m_w_glu, m_b_glu=m_b_glu, m_w_out=m_w_out, m_ln2_g=m_ln2_g, m_ln2_b=m_ln2_b, m_ffn2_w1=m_ffn2_w1, m_ffn2_w3=m_ffn2_w3, m_ffn2_w2=m_ffn2_w2, m_ln3_g=m_ln3_g, m_ln3_b=m_ln3_b, v_w_ada=v_w_ada, v_b_ada=v_b_ada, v_ffn1_w1=v_ffn1_w1, v_ffn1_w3=v_ffn1_w3, v_ffn1_w2=v_ffn1_w2, v_ln1_g=v_ln1_g, v_ln1_b=v_ln1_b, v_w_in=v_w_in, v_conv_w=v_conv_w, v_conv_b=v_conv_b, v_dt_bias=v_dt_bias, v_a_log=v_a_log, v_d_ssd=v_d_ssd, v_ssd_norm_w=v_ssd_norm_w, v_s5_a_re=v_s5_a_re, v_s5_a_im=v_s5_a_im, v_s5_log_dt=v_s5_log_dt, v_s5_b_re=v_s5_b_re, v_s5_b_im=v_s5_b_im, v_s5_c_re=v_s5_c_re, v_s5_c_im=v_s5_c_im, v_s5_d=v_s5_d, v_w_glu=v_w_glu, v_b_glu=v_b_glu, v_w_out=v_w_out, v_ln2_g=v_ln2_g, v_ln2_b=v_ln2_b, v_ffn2_w1=v_ffn2_w1, v_ffn2_w3=v_ffn2_w3, v_ffn2_w2=v_ffn2_w2, v_ln3_g=v_ln3_g, v_ln3_b=v_ln3_b)
    weights = {n: given[n] for n in TWIN_WEIGHTS}
    shared = {n: given[n] for n in SHARED_INPUTS}
    per_example = {n: given[n] for n in ['x', 'c']}
    grad_fn = _jax.value_and_grad(_loss, argnums=(0, 1))

    def one_microbatch(ex, loss_target):
        ex = dict(ex)
        diff = ex.pop(TWIN_DIFF_INPUT)
        return grad_fn(weights, diff, {**shared, **ex}, loss_target)

    if N_MICROBATCH == 1:
        loss, (grad_w, grad_x) = one_microbatch(per_example, given["loss_target"])
    else:
        def body(carry, xs):
            loss_sum, grad_sum = carry
            l_k, (gw_k, gx_k) = one_microbatch(xs[0], xs[1])
            with _jax.named_scope("update"):
                return (loss_sum + l_k, _jax.tree.map(_jnp.add, grad_sum, gw_k)), gx_k

        init = (_jnp.zeros((), _jnp.float32), _jax.tree.map(_jnp.zeros_like, weights))
        (loss, grad_w), grad_x = _jax.lax.scan(body, init, (per_example, given["loss_target"]))
    with _jax.named_scope("update"):
        delta_w, new_m, new_v = {}, {}, {}
        for n in TWIN_WEIGHTS:
            delta_w[n], new_m[n], new_v[n] = _adamw(weights[n], grad_w[n], given["m_" + n], given["v_" + n])
    return (loss, grad_x, *[grad_w[n] for n in TWIN_WEIGHTS], *[delta_w[n] for n in TWIN_WEIGHTS],
            *[new_m[n] for n in TWIN_WEIGHTS], *[new_v[n] for n in TWIN_WEIGHTS])
```

```python
import functools
import math

import jax
import jax.numpy as jnp
from jax import lax
from jax.experimental import pallas as pl
from jax.experimental.pallas import tpu as pltpu

F32 = jnp.float32
BF16 = jnp.bfloat16
MXU_DTYPE = jnp.bfloat16

D = 1024
FF = 2816
FB = 1408
NH, HP, NS, NG = 8, 64, 128, 2
CH = 128
SW = 512
S5G, S5P, S5H = 32, 64, 16
S5L = S5G * S5P
PW = 2176
ALPHA = 2.0 ** 0.25
LN_EPS = 1e-5
ADAM_LR, ADAM_B1, ADAM_B2, ADAM_EPS, ADAM_WD, ADAM_STEP = 0.001, 0.9, 0.999, 1e-08, 0.01, 10
VMEM_LIMIT = 56 * 1024 * 1024
MESH_T = pl.DeviceIdType.MESH


def _pcall(body, **kw):
    return pl.pallas_call(body, **kw)


def _cparams(sem=None, **kw):
    return pltpu.CompilerParams(dimension_semantics=sem, vmem_limit_bytes=VMEM_LIMIT, **kw)


def _dot(a, b):
    return jnp.dot(a, b, preferred_element_type=F32)


def _dot_nt(a, b):
    return lax.dot_general(a, b, (((1,), (1,)), ((), ())), preferred_element_type=F32)


def _dot_hi(a, b):
    return jnp.dot(a, b, preferred_element_type=F32, precision=lax.Precision.HIGHEST)


def _mx(a):
    return a.astype(MXU_DTYPE)


def _sigmoid(x):
    return 1.0 / (1.0 + jnp.exp(-x))


def _iota(shape, axis):
    return lax.broadcasted_iota(jnp.int32, shape, axis)


def _rowcall(name, fn, n_rows, tm, tpe, *, tiled=(), halos=(), exs=(), res=(), out_tiled=(), out_acc=(),
             out_exacc=(), scratch=(), reverse=False):
    nt = n_rows // tm

    def blk(i):
        return (nt - 1 - i) if reverse else i

    in_specs, args = [], []
    for a in tiled:
        in_specs.append(pl.BlockSpec((tm, a.shape[1]), lambda i: (blk(i), 0)))
        args.append(a)
    for a, rows in halos:
        r = tm // rows
        in_specs.append(pl.BlockSpec((rows, a.shape[1]), lambda i, r=r: (jnp.maximum(blk(i) * r - 1, 0), 0)))
        args.append(a)
    for a in exs:
        in_specs.append(pl.BlockSpec((1,) + a.shape[1:], lambda i: (blk(i) // tpe, 0, 0)))
        args.append(a)
    for a in res:
        nd = a.ndim
        in_specs.append(pl.BlockSpec(a.shape, lambda i, nd=nd: (0,) * nd, pipeline_mode=pl.Buffered(1)))
        args.append(a)
    out_specs, out_shape = [], []
    for s in out_tiled:
        out_specs.append(pl.BlockSpec((tm, s.shape[1]), lambda i: (blk(i), 0)))
        out_shape.append(s)
    for s in out_acc:
        nd = len(s.shape)
        out_specs.append(pl.BlockSpec(s.shape, lambda i, nd=nd: (0,) * nd))
        out_shape.append(s)
    for s in out_exacc:
        out_specs.append(pl.BlockSpec((1,) + s.shape[1:], lambda i: (blk(i) // tpe, 0, 0)))
        out_shape.append(s)
    n = [len(tiled), len(halos), len(exs), len(res), len(out_tiled), len(out_acc), len(out_exacc), len(scratch)]

    def body(*refs):
        groups, k = [], 0
        for m in n:
            groups.append(refs[k:k + m])
            k += m
        i = pl.program_id(0)
        b = blk(i)

        class ctx:
            first = i == 0
            pos = b % tpe
            seq_first = (b % tpe) == 0
            seq_last = (b % tpe) == tpe - 1
            ex_enter = (i % tpe) == 0

        fn(ctx, *groups)

    return _pcall(body, name=name, grid=(nt,), in_specs=in_specs, out_specs=out_specs, out_shape=out_shape,
                  scratch_shapes=list(scratch), compiler_params=_cparams(("arbitrary",)))(*args)


def _acc(ref, val, first):
    @pl.when(first)
    def _():
        ref[...] = jnp.zeros(ref.shape, ref.dtype)
    ref[...] += val


def _sds(shape, dtype=F32):
    return jax.ShapeDtypeStruct(shape, dtype)


def _ln_fwd(r, g, b):
    mu = jnp.mean(r, axis=-1, keepdims=True)
    rc = r - mu
    var = jnp.mean(rc * rc, axis=-1, keepdims=True)
    return rc * lax.rsqrt(var + LN_EPS) * g + b


def _ln_bwd(r, g, dy):
    mu = jnp.mean(r, axis=-1, keepdims=True)
    rc = r - mu
    var = jnp.mean(rc * rc, axis=-1, keepdims=True)
    rstd = lax.rsqrt(var + LN_EPS)
    xhat = rc * rstd
    dxh = dy * g
    dr = rstd * (dxh - jnp.mean(dxh, axis=-1, keepdims=True) - xhat * jnp.mean(dxh * xhat, axis=-1, keepdims=True))
    return dr, jnp.sum(dy * xhat, axis=0, keepdims=True), jnp.sum(dy, axis=0, keepdims=True)


def _ffn_fwd(name, x, mod, k0, w13, w2, lng, lnb, seq, tgt=None):
    t = x.shape[0]
    tm = 256
    with_loss = tgt is not None

    def fn(ctx, tiled, halos, exs, res, outs, accs, exaccs, scr):
        x_ref = tiled[0]
        mod_ref, = exs
        w13_ref, w2_ref, g_ref, b_ref = res
        xo_ref, r_ref, h_ref, ab_ref, f_ref = outs[:5]
        xv = x_ref[...]
        sh, sc, g = mod_ref[0, k0:k0 + 1, :], mod_ref[0, k0 + 1:k0 + 2, :], mod_ref[0, k0 + 2:k0 + 3, :]
        h = _mx(xv * (1.0 + sc) + sh)
        h_ref[...] = h
        f = jnp.zeros((tm, D), F32)
        for j in range(2):
            a = _dot(h, w13_ref[:, j * FB:(j + 1) * FB])
            b = _dot(h, w13_ref[:, FF + j * FB:FF + (j + 1) * FB])
            ab_ref[:, j * FB:(j + 1) * FB] = a
            ab_ref[:, FF + j * FB:FF + (j + 1) * FB] = b
            s = a * _sigmoid(a) * b
            f = f + _dot(_mx(s), w2_ref[j * FB:(j + 1) * FB, :])
        f_ref[...] = f
        r = ALPHA * xv + 0.5 * g * f
        r_ref[...] = r
        xo = _ln_fwd(r, g_ref[...], b_ref[...])
        xo_ref[...] = xo
        if with_loss:
            e = xo - tiled[1][...]
            outs[5][...] = e * (1.0 / D)
            _acc(accs[0], jnp.sum(e * e) * jnp.ones((8, 128), F32), ctx.first)

    out_tiled = [_sds((t, D)), _sds((t, D)), _sds((t, D), MXU_DTYPE), _sds((t, 2 * FF)), _sds((t, D))]
    if with_loss:
        out_tiled.append(_sds((t, D)))
    return _rowcall(name, fn, t, tm, seq // tm, tiled=[x] + ([tgt] if with_loss else []), exs=[mod],
                    res=[w13, w2, lng, lnb], out_tiled=out_tiled, out_acc=[_sds((8, 128))] if with_loss else [])


def _ffn_bwd1(name, dxo, r, ab, f, mod, k0, lng, w2t, seq):
    t = dxo.shape[0]
    tm = 256

    def fn(ctx, tiled, halos, exs, res, outs, accs, exaccs, scr):
        dxo_ref, r_ref, ab_ref, f_ref = tiled
        mod_ref, = exs
        g_ref, w2t_ref = res
        dr_ref, df_ref, s_ref, dab_ref = outs
        g = mod_ref[0, k0 + 2:k0 + 3, :]
        dr, dgam, dbet = _ln_bwd(r_ref[...], g_ref[...], dxo_ref[...])
        dr_ref[...] = dr
        _acc(accs[0], dgam, ctx.first)
        _acc(accs[1], dbet, ctx.first)
        _acc(exaccs[0].at[0], jnp.sum(0.5 * f_ref[...] * dr, axis=0, keepdims=True), ctx.ex_enter)
        df = _mx(0.5 * g * dr)
        df_ref[...] = df
        for j in range(2):
            ds = _dot(df, w2t_ref[:, j * FB:(j + 1) * FB])
            a = ab_ref[:, j * FB:(j + 1) * FB]
            b = ab_ref[:, FF + j * FB:FF + (j + 1) * FB]
            sig = _sigmoid(a)
            silu = a * sig
            s_ref[:, j * FB:(j + 1) * FB] = _mx(silu * b)
            dab_ref[:, j * FB:(j + 1) * FB] = _mx(ds * b * (sig * (1.0 + a * (1.0 - sig))))
            dab_ref[:, FF + j * FB:FF + (j + 1) * FB] = _mx(ds * silu)

    b = mod.shape[0]
    return _rowcall(name, fn, t, tm, seq // tm, tiled=[dxo, r, ab, f], exs=[mod], res=[lng, w2t],
                    out_tiled=[_sds((t, D)), _sds((t, D), MXU_DTYPE), _sds((t, FF), MXU_DTYPE),
                               _sds((t, 2 * FF), MXU_DTYPE)],
                    out_acc=[_sds((1, D)), _sds((1, D))], out_exacc=[_sds((b, 1, D))])


def _mod_bwd(name, dab, dr, x, mod, k0, wt, seq, extra=()):
    t = dr.shape[0]
    tm = 256
    nin = 1 + len(extra)
    width = dab.shape[1] + sum(e.shape[1] for e in extra)

    def fn(ctx, tiled, halos, exs, res, outs, accs, exaccs, scr):
        parts = tiled[:nin]
        dr_ref, x_ref = tiled[nin:]
        mod_ref, = exs
        wt_ref, = res
        sc = mod_ref[0, k0 + 1:k0 + 2, :]
        if nin == 1:
            dp = parts[0][...]
        else:
            dp = jnp.concatenate([_mx(p[...]) for p in parts], axis=1)
            outs[1][...] = dp
        dh = _dot(dp, wt_ref[...])
        outs[0][...] = ALPHA * dr_ref[...] + dh * (1.0 + sc)
        _acc(exaccs[0].at[0], jnp.sum(dh, axis=0, keepdims=True), ctx.ex_enter)
        _acc(exaccs[1].at[0], jnp.sum(dh * x_ref[...], axis=0, keepdims=True), ctx.ex_enter)

    b = mod.shape[0]
    out_tiled = [_sds((t, D))] + ([_sds((t, width), MXU_DTYPE)] if nin > 1 else [])
    return _rowcall(name, fn, t, tm, seq // tm, tiled=[dab, *extra, dr, x], exs=[mod], res=[wt],
                    out_tiled=out_tiled, out_exacc=[_sds((b, 1, D)), _sds((b, 1, D))])


def _tn_matmul(name, a, b, bm, bn, bt=512):
    t, m = a.shape
    n = b.shape[1]

    def body(a_ref, b_ref, o_ref):
        @pl.when(pl.program_id(2) == 0)
        def _():
            o_ref[...] = jnp.zeros(o_ref.shape, F32)
        o_ref[...] += lax.dot_general(a_ref[...], b_ref[...], (((0,), (0,)), ((), ())), preferred_element_type=F32)

    return _pcall(body, name=name, grid=(m // bm, n // bn, t // bt),
                  in_specs=[pl.BlockSpec((bt, bm), lambda i, j, k: (k, i)), pl.BlockSpec((bt, bn), lambda i, j, k: (k, j))],
                  out_specs=pl.BlockSpec((bm, bn), lambda i, j, k: (i, j)), out_shape=_sds((m, n)),
                  compiler_params=_cparams(("parallel", "parallel", "arbitrary")))(a, b)


def _proj_fwd(name, x, mod, w_in, seq):
    t = x.shape[0]
    tm = 256

    def fn(ctx, tiled, halos, exs, res, outs, accs, exaccs, scr):
        mod_ref, = exs
        sh, sc = mod_ref[0, 3:4, :], mod_ref[0, 4:5, :]
        h = _mx(tiled[0][...] * (1.0 + sc) + sh)
        outs[0][...] = h
        outs[1][...] = _dot(h, res[0][...])

    return _rowcall(name, fn, t, tm, seq // tm, tiled=[x], exs=[mod], res=[w_in],
                    out_tiled=[_sds((t, D), MXU_DTYPE), _sds((t, PW))])


def _shift_rows(cur, prev8, j):
    if j == 0:
        return cur
    rolled = pltpu.roll(cur, j, 0)
    top = jnp.where(_iota((8, cur.shape[1]), 0) < j, pltpu.roll(prev8, j, 0), rolled[0:8])
    return jnp.concatenate([top, rolled[8:]], axis=0)


def _shift_rows_up(cur, next8, j):
    if j == 0:
        return cur
    n = cur.shape[0]
    rolled = pltpu.roll(cur, n - j, 0)
    bot = jnp.where(_iota((8, cur.shape[1]), 0) >= 8 - j, pltpu.roll(next8, 8 - j, 0), rolled[n - 8:n])
    return jnp.concatenate([rolled[:n - 8], bot], axis=0)


def _softplus(x):
    return jnp.maximum(x, 0.0) + jnp.log(1.0 + jnp.exp(-jnp.abs(x)))


def _ssd_common(proj_ref, xpre, dtb_ref, alog_ref):
    xbc = xpre * _sigmoid(xpre)
    xs, bm, cm = xbc[:, 0:SW], xbc[:, SW:SW + 256], xbc[:, SW + 256:SW + 512]
    dtraw = proj_ref[:, PW - 128:PW] + dtb_ref[...]
    dt = _softplus(dtraw)
    a = -jnp.exp(alog_ref[...])
    tril = (_iota((CH, CH), 0) >= _iota((CH, CH), 1)).astype(F32)
    acs = _dot_hi(tril, dt * a)
    return xs, bm, cm, dtraw, dt, a, acs, acs.T


def _pair_lane(lo, hi):
    r = lo.shape[0]
    return jnp.where(_iota((r, 128), 1) < HP, lo, hi)


def _ssd_fwd(name, proj, conv_w, conv_b, dt_bias, a_log, d_rep, norm_w, seq):
    t = proj.shape[0]

    def fn(ctx, tiled, halos, exs, res, outs, accs, exaccs, scr):
        proj_ref, = tiled
        halo_ref, = halos
        cw_ref, cb_ref, dtb_ref, alog_ref, d_ref, nw_ref = res
        yo_ref, xpre_ref, y_ref, sprev_ref = outs
        state_ref, = scr

        @pl.when(ctx.seq_first)
        def _():
            state_ref[...] = jnp.zeros(state_ref.shape, F32)

        raw = proj_ref[:, SW:SW + D]
        prev8 = halo_ref[:, SW:SW + D] * jnp.where(ctx.seq_first, 0.0, 1.0)
        xpre = cb_ref[...] + cw_ref[3:4, :] * raw
        for j in (1, 2, 3):
            xpre = xpre + cw_ref[3 - j:4 - j, :] * _shift_rows(raw, prev8, j)
        xpre_ref[...] = xpre
        xs, bm, cm, dtraw, dt, a, acs, acst = _ssd_common(proj_ref, xpre, dtb_ref, alog_ref)
        causal = _iota((CH, CH), 0) >= _iota((CH, CH), 1)
        lane_lo = _iota((CH, 128), 1) < HP
        sprev = state_ref[...]
        sprev_ref[...] = sprev
        ys = []
        for g in range(NG):
            bmg, cmg = bm[:, g * NS:(g + 1) * NS], cm[:, g * NS:(g + 1) * NS]
            bmt = bmg.T
            cb = _dot(_mx(cmg), _mx(bmt))
            for q in (2 * g, 2 * g + 1):
                xsq = xs[:, 128 * q:128 * q + 128]
                xd = xsq * _pair_lane(dt[:, 2 * q:2 * q + 1], dt[:, 2 * q + 1:2 * q + 2])
                sp = sprev[:, 128 * q:128 * q + 128]
                ydiag = jnp.zeros((CH, 128), F32)
                snew = jnp.zeros((NS, 128), F32)
                for jj in range(2):
                    h = 2 * q + jj
                    col, row = acs[:, h:h + 1], acst[h:h + 1, :]
                    lm = jnp.where(causal, jnp.exp(jnp.minimum(col - row, 0.0)), 0.0)
                    xm = _mx(jnp.where(lane_lo if jj == 0 else ~lane_lo, xd, 0.0))
                    ydiag = ydiag + _dot(_mx(cb * lm), xm)
                    dec_row = jnp.exp(acst[h:h + 1, CH - 1:CH] - row)
                    snew = snew + _dot(_mx(bmt * dec_row), xm)
                e_pair = jnp.exp(_pair_lane(acs[:, 2 * q:2 * q + 1], acs[:, 2 * q + 1:2 * q + 2]))
                yoff = _dot(_mx(cmg), _mx(sp)) * e_pair
                cd = jnp.exp(_pair_lane(acst[2 * q:2 * q + 1, CH - 1:CH], acst[2 * q + 1:2 * q + 2, CH - 1:CH]))
                state_ref[:, 128 * q:128 * q + 128] = cd * sp + snew
                ys.append(ydiag + yoff + d_ref[:, 128 * q:128 * q + 128] * xsq)
        y = jnp.concatenate(ys, axis=1)
        y_ref[...] = y
        z = proj_ref[:, 0:SW]
        yz = y * (z * _sigmoid(z))
        outp = []
        for g in range(NG):
            seg = yz[:, 256 * g:256 * g + 256]
            rinv = lax.rsqrt(jnp.mean(seg * seg, axis=-1, keepdims=True) + LN_EPS)
            outp.append(seg * rinv * nw_ref[:, 256 * g:256 * g + 256])
        yo_ref[...] = _mx(jnp.concatenate(outp, axis=1))

    return _rowcall(name, fn, t, CH, seq // CH, tiled=[proj], halos=[(proj, 8)],
                    res=[conv_w, conv_b, dt_bias, a_log, d_rep, norm_w],
                    out_tiled=[_sds((t, SW), MXU_DTYPE), _sds((t, D)), _sds((t, SW)), _sds((t, SW))],
                    scratch=[pltpu.VMEM((NS, SW), F32)])


def _ssd_bwd(name, dyo, proj, xpre_all, y_all, sprev_all, conv_w, dt_bias, a_log, d_rep, norm_w, seq):
    t = proj.shape[0]

    def fn(ctx, tiled, halos, exs, res, outs, accs, exaccs, scr):
        dyo_ref, proj_ref, xpre_ref, y_ref, sprev_ref = tiled
        halo_ref, = halos
        cw_ref, dtb_ref, alog_ref, d_ref, nw_ref = res
        dzx_ref, ddt_ref = outs
        dnw_acc, dd_acc, dcw_acc, dcb_acc, ddtb_acc, dalog_acc = accs
        ds_ref, nxt_ref = scr

        @pl.when(ctx.seq_last)
        def _():
            ds_ref[...] = jnp.zeros(ds_ref.shape, F32)
            nxt_ref[...] = jnp.zeros(nxt_ref.shape, F32)

        xpre = xpre_ref[...]
        xs, bm, cm, dtraw, dt, a, acs, acst = _ssd_common(proj_ref, xpre, dtb_ref, alog_ref)
        y = y_ref[...]
        z = proj_ref[:, 0:SW]
        sz = _sigmoid(z)
        siluz = z * sz
        yz = y * siluz
        dyo = dyo_ref[...]
        dyz_parts, dnw_parts = [], []
        for g in range(NG):
            seg = yz[:, 256 * g:256 * g + 256]
            rinv = lax.rsqrt(jnp.mean(seg * seg, axis=-1, keepdims=True) + LN_EPS)
            yn = seg * rinv
            dseg = dyo[:, 256 * g:256 * g + 256]
            dnw_parts.append(jnp.sum(dseg * yn, axis=0, keepdims=True))
            dyn = dseg * nw_ref[:, 256 * g:256 * g + 256]
            dyz_parts.append(rinv * (dyn - yn * jnp.mean(dyn * yn, axis=-1, keepdims=True)))
        dyz = jnp.concatenate(dyz_parts, axis=1)
        _acc(dnw_acc, jnp.concatenate(dnw_parts, axis=1), ctx.first)
        dy = dyz * siluz
        dz = dyz * y * (sz * (1.0 + z * (1.0 - sz)))
        _acc(dd_acc, jnp.sum(dy * xs, axis=0, keepdims=True), ctx.first)

        causal = _iota((CH, CH), 0) >= _iota((CH, CH), 1)
        anti = _iota((CH, CH), 0) <= _iota((CH, CH), 1)
        lane_lo = _iota((CH, 128), 1) < HP
        lane_id = _iota((CH, 128), 1)
        last_row = _iota((CH, 128), 0) == CH - 1
        sprev = sprev_ref[...]
        dacs = jnp.zeros((CH, 128), F32)
        ddt_x = jnp.zeros((CH, 128), F32)
        dxs_parts, dbm_parts, dcm_parts = [], [], []
        for g in range(NG):
            bmg, cmg = bm[:, g * NS:(g + 1) * NS], cm[:, g * NS:(g + 1) * NS]
            bmt, cmt = bmg.T, cmg.T
            cb = _dot(_mx(cmg), _mx(bmt))
            cbt = _dot(_mx(bmg), _mx(cmt))
            dcb = jnp.zeros((CH, CH), F32)
            dcbt = jnp.zeros((CH, CH), F32)
            dbmg = jnp.zeros((CH, NS), F32)
            dcmg = jnp.zeros((CH, NS), F32)
            for q in (2 * g, 2 * g + 1):
                sl = slice(128 * q, 128 * q + 128)
                xsq = xs[:, sl]
                dtp = _pair_lane(dt[:, 2 * q:2 * q + 1], dt[:, 2 * q + 1:2 * q + 2])
                xd = xsq * dtp
                dyq = dy[:, sl]
                sp = sprev[:, sl]
                dsn = ds_ref[:, sl]
                e_pair = jnp.exp(_pair_lane(acs[:, 2 * q:2 * q + 1], acs[:, 2 * q + 1:2 * q + 2]))
                cd = jnp.exp(_pair_lane(acst[2 * q:2 * q + 1, CH - 1:CH], acst[2 * q + 1:2 * q + 2, CH - 1:CH]))
                dye = dyq * e_pair
                dcmg = dcmg + _dot(_mx(dye), _mx(sp.T))
                dsp = _dot(_mx(cmt), _mx(dye)) + cd * dsn
                yoff = _dot(_mx(cmg), _mx(sp)) * e_pair
                dacs_lane = dyq * yoff
                dxd = jnp.zeros((CH, 128), F32)
                sds = jnp.sum(dsn * sp, axis=0, keepdims=True) * cd
                for jj in range(2):
                    h = 2 * q + jj
                    hm = lane_lo if jj == 0 else ~lane_lo
                    col, row = acs[:, h:h + 1], acst[h:h + 1, :]
                    lm = jnp.where(causal, jnp.exp(jnp.minimum(col - row, 0.0)), 0.0)
                    lmt = jnp.where(anti, jnp.exp(jnp.minimum(row - col, 0.0)), 0.0)
                    xm = _mx(jnp.where(hm, xd, 0.0))
                    dym = _mx(jnp.where(hm, dyq, 0.0))
                    gm = _dot_nt(dym, xm)
                    gmt = _dot_nt(xm, dym)
                    dcb = dcb + gm * lm
                    dcbt = dcbt + gmt * lmt
                    dxd = dxd + _dot(_mx(cbt * lmt), dym)
                    w = gm * cb * lm
                    wt = gmt * cbt * lmt
                    dacs_h = jnp.sum(w, axis=1, keepdims=True) - jnp.sum(wt, axis=1, keepdims=True)
                    alast = acst[h:h + 1, CH - 1:CH]
                    dec_col = jnp.exp(alast - col)
                    dsm = _mx(jnp.where(hm[0:NS], dsn, 0.0))
                    qh = _dot_nt(xm, dsm)
                    dbmg = dbmg + qh * dec_col
                    ddec = jnp.sum(qh * bmg, axis=1, keepdims=True)
                    dxd = dxd + _dot(_mx(bmg * dec_col), dsm)
                    dacs_h = dacs_h - ddec * dec_col
                    dacs_h = dacs_h + jnp.sum(jnp.where(hm, dacs_lane, 0.0), axis=1, keepdims=True)
                    tail = jnp.sum(ddec * dec_col, axis=0, keepdims=True) + jnp.sum(
                        jnp.where(hm[0:1], sds, 0.0), axis=1, keepdims=True)
                    dacs = dacs + jnp.where(lane_id == h, dacs_h, 0.0) + jnp.where(
                        last_row & (lane_id == h), tail, 0.0)
                ds_ref[:, sl] = dsp
                for jj in range(2):
                    h = 2 * q + jj
                    hm = lane_lo if jj == 0 else ~lane_lo
                    ddt_x = ddt_x + jnp.where(lane_id == h, jnp.sum(jnp.where(hm, dxd * xsq, 0.0), axis=1, keepdims=True), 0.0)
                dxs_parts.append(dxd * dtp + d_ref[:, sl] * dyq)
            dcmg = dcmg + _dot(_mx(dcb), _mx(bmg))
            dbmg = dbmg + _dot(_mx(dcbt), _mx(cmg))
            dbm_parts.append(dbmg)
            dcm_parts.append(dcmg)
        triu = (_iota((CH, CH), 0) <= _iota((CH, CH), 1)).astype(F32)
        dadt = _dot_hi(triu, dacs)
        ddt = dadt * a + ddt_x
        _acc(dalog_acc, jnp.sum(dadt * dt, axis=0, keepdims=True) * a, ctx.first)
        ddtraw = ddt * _sigmoid(dtraw)
        ddt_ref[...] = ddtraw
        _acc(ddtb_acc, jnp.sum(ddtraw, axis=0, keepdims=True), ctx.first)
        dxbc = jnp.concatenate(dxs_parts + dbm_parts + dcm_parts, axis=1)
        sx = _sigmoid(xpre)
        dpre = dxbc * (sx * (1.0 + xpre * (1.0 - sx)))
        _acc(dcb_acc, jnp.sum(dpre, axis=0, keepdims=True), ctx.first)
        raw = proj_ref[:, SW:SW + D]
        prev8 = halo_ref[:, SW:SW + D] * jnp.where(ctx.seq_first, 0.0, 1.0)
        next8 = nxt_ref[...]
        draw = cw_ref[3:4, :] * dpre
        dcw = [None] * 4
        dcw[3] = jnp.sum(dpre * raw, axis=0, keepdims=True)
        for j in (1, 2, 3):
            dcw[3 - j] = jnp.sum(dpre * _shift_rows(raw, prev8, j), axis=0, keepdims=True)
            draw = draw + cw_ref[3 - j:4 - j, :] * _shift_rows_up(dpre, next8, j)
        _acc(dcw_acc, jnp.concatenate(dcw + [jnp.zeros((4, D), F32)], axis=0), ctx.first)
        nxt_ref[...] = dpre[0:8]
        dzx_ref[:, 0:SW] = dz
        dzx_ref[:, SW:SW + D] = draw

    return _rowcall(name, fn, t, CH, seq // CH, tiled=[dyo, proj, xpre_all, y_all, sprev_all], halos=[(proj, 8)],
                    res=[conv_w, dt_bias, a_log, d_rep, norm_w],
                    out_tiled=[_sds((t, SW + D)), _sds((t, 128))],
                    out_acc=[_sds((1, SW)), _sds((1, SW)), _sds((8, D)), _sds((1, D)), _sds((1, 128)), _sds((1, 128))],
                    scratch=[pltpu.VMEM((NS, SW), F32), pltpu.VMEM((8, D), F32)], reverse=True)


def _gelu(y):
    k = math.sqrt(2.0 / math.pi)
    return 0.5 * y * (1.0 + jnp.tanh(k * (y + 0.044715 * y * y * y)))


def _gelu_grad(y):
    k = math.sqrt(2.0 / math.pi)
    th = jnp.tanh(k * (y + 0.044715 * y * y * y))
    return 0.5 * (1.0 + th) + 0.5 * y * (1.0 - th * th) * k * (1.0 + 3.0 * 0.044715 * y * y)


S5T = 256


def _cmul_add(xr, xi, ar, ai, sr, si):
    return xr + ar * sr - ai * si, xi + ar * si + ai * sr


def _s5_fwd(name, proj, bbd, cbd, pw, tab, d5, w_glu, b_glu, seq):
    t = proj.shape[0]
    tm = S5T

    def fn(ctx, tiled, halos, exs, res, outs, accs, exaccs, scr):
        proj_ref, = tiled
        bbd_ref, cbd_ref, pw_ref, tab_ref, d_ref, wg_ref, bg_ref = res
        out_ref, xst_ref, y_ref = outs
        carry_ref, = scr

        @pl.when(ctx.seq_first)
        def _():
            carry_ref[...] = jnp.zeros(carry_ref.shape, F32)

        u = proj_ref[:, 1536:2048]
        bu = _dot(_mx(u), bbd_ref[...])
        xr, xi = bu[:, :S5L], bu[:, S5L:]
        rowi = _iota((tm, S5L), 0) & 7
        for k, sh in enumerate((1, 2, 4)):
            keep = rowi >= sh
            sr = jnp.where(keep, pltpu.roll(xr, sh, 0), 0.0)
            si = jnp.where(keep, pltpu.roll(xi, sh, 0), 0.0)
            xr, xi = _cmul_add(xr, xi, pw_ref[k:k + 1, :S5L], pw_ref[k:k + 1, S5L:], sr, si)
        xst_ref[:, :S5L] = xr
        xst_ref[:, S5L:] = xi

        def tile_fix(i, c):
            cr, ci = c
            rows = pl.ds(pl.multiple_of(i * 8, 8), 8)
            tr, ti = _cmul_add(xst_ref[rows, :S5L], xst_ref[rows, S5L:], tab_ref[:, :S5L], tab_ref[:, S5L:], cr, ci)
            xst_ref[rows, :S5L] = tr
            xst_ref[rows, S5L:] = ti
            return tr[7:8], ti[7:8]

        cr, ci = lax.fori_loop(0, tm // 8, tile_fix, (carry_ref[0:1, :S5L], carry_ref[0:1, S5L:]))
        carry_ref[0:1, :S5L] = cr
        carry_ref[0:1, S5L:] = ci
        y = _dot(_mx(xst_ref[...]), cbd_ref[...]) + u * d_ref[...]
        y_ref[...] = y
        g = _gelu(y)
        v = _dot(_mx(g), wg_ref[...]) + bg_ref[...]
        out_ref[...] = _mx(g * _sigmoid(v))

    return _rowcall(name, fn, t, tm, seq // tm, tiled=[proj], res=[bbd, cbd, pw, tab, d5, w_glu, b_glu],
                    out_tiled=[_sds((t, SW), MXU_DTYPE), _sds((t, 2 * S5L)), _sds((t, SW))],
                    scratch=[pltpu.VMEM((8, 2 * S5L), F32)])


def _s5_bwd(name, dout, proj, xst, y_all, bbdt, cbdt, pwc, tabc, d5, w_glu, w_glut, b_glu, seq):
    t = proj.shape[0]
    tm = S5T

    def fn(ctx, tiled, halos, exs, res, outs, accs, exaccs, scr):
        dout_ref, proj_ref, xst_ref, y_ref = tiled
        halo_ref, = halos
        bbdt_ref, cbdt_ref, pw_ref, tab_ref, d_ref, wg_ref, wgt_ref, bg_ref = res
        du_ref, lam_ref, dyb_ref, gb_ref, dvb_ref = outs
        da_acc, dd_acc, dbg_acc = accs
        carry_ref, lamf_ref = scr

        @pl.when(ctx.seq_last)
        def _():
            carry_ref[...] = jnp.zeros(carry_ref.shape, F32)

        u = proj_ref[:, 1536:2048]
        y = y_ref[...]
        g = _gelu(y)
        v = _dot(_mx(g), wg_ref[...]) + bg_ref[...]
        sg = _sigmoid(v)
        dout = dout_ref[...]
        dv = dout * g * sg * (1.0 - sg)
        dvb = _mx(dv)
        dvb_ref[...] = dvb
        gb_ref[...] = _mx(g)
        _acc(dbg_acc, jnp.sum(dv, axis=0, keepdims=True), ctx.first)
        dg = dout * sg + _dot(dvb, wgt_ref[...])
        dy = dg * _gelu_grad(y)
        dyb = _mx(dy)
        dyb_ref[...] = dyb
        _acc(dd_acc, jnp.sum(dy * u, axis=0, keepdims=True), ctx.first)
        dx = _dot(dyb, cbdt_ref[...])
        xr, xi = dx[:, :S5L], dx[:, S5L:]
        rowi = _iota((tm, S5L), 0) & 7
        for k, sh in enumerate((1, 2, 4)):
            keep = rowi < 8 - sh
            sr = jnp.where(keep, pltpu.roll(xr, tm - sh, 0), 0.0)
            si = jnp.where(keep, pltpu.roll(xi, tm - sh, 0), 0.0)
            xr, xi = _cmul_add(xr, xi, pw_ref[k:k + 1, :S5L], pw_ref[k:k + 1, S5L:], sr, si)
        lamf_ref[:, :S5L] = xr
        lamf_ref[:, S5L:] = xi

        def tile_fix(i, c):
            cr, ci = c
            rows = pl.ds(pl.multiple_of((tm // 8 - 1 - i) * 8, 8), 8)
            tr, ti = _cmul_add(lamf_ref[rows, :S5L], lamf_ref[rows, S5L:], tab_ref[:, :S5L], tab_ref[:, S5L:], cr, ci)
            lamf_ref[rows, :S5L] = tr
            lamf_ref[rows, S5L:] = ti
            return tr[0:1], ti[0:1]

        cr, ci = lax.fori_loop(0, tm // 8, tile_fix, (carry_ref[0:1, :S5L], carry_ref[0:1, S5L:]))
        carry_ref[0:1, :S5L] = cr
        carry_ref[0:1, S5L:] = ci
        lam = lamf_ref[...]
        lamb = _mx(lam)
        lam_ref[...] = lamb
        du_ref[...] = dy * d_ref[...] + _dot(lamb, bbdt_ref[...])
        prev8 = halo_ref[...] * jnp.where(ctx.seq_first, 0.0, 1.0)
        xprev = _shift_rows(xst_ref[...], prev8, 1)
        lr, li = lam[:, :S5L], lam[:, S5L:]
        pr, pi = xprev[:, :S5L], xprev[:, S5L:]
        dar = jnp.sum(lr * pr + li * pi, axis=0, keepdims=True)
        dai = jnp.sum(li * pr - lr * pi, axis=0, keepdims=True)
        _acc(da_acc, jnp.concatenate([dar, dai], axis=1), ctx.first)

    return _rowcall(name, fn, t, tm, seq // tm, tiled=[dout, proj, xst, y_all], halos=[(xst, 8)],
                    res=[bbdt, cbdt, pwc, tabc, d5, w_glu, w_glut, b_glu],
                    out_tiled=[_sds((t, SW)), _sds((t, 2 * S5L), MXU_DTYPE), _sds((t, SW), MXU_DTYPE),
                               _sds((t, SW), MXU_DTYPE), _sds((t, SW), MXU_DTYPE)],
                    out_acc=[_sds((1, 2 * S5L)), _sds((1, SW)), _sds((1, SW))],
                    scratch=[pltpu.VMEM((8, 2 * S5L), F32), pltpu.VMEM((tm, 2 * S5L), F32)], reverse=True)


def _out_fwd(name, yssd, ys5, x1, mod, w_out, lng, lnb, seq):
    t = x1.shape[0]
    tm = 256

    def fn(ctx, tiled, halos, exs, res, outs, accs, exaccs, scr):
        ya_ref, yb_ref, x_ref = tiled
        mod_ref, = exs
        w_ref, g_ref, b_ref = res
        m = _dot(ya_ref[...], w_ref[0:SW, :]) + _dot(yb_ref[...], w_ref[SW:2 * SW, :])
        r = ALPHA * x_ref[...] + mod_ref[0, 5:6, :] * m
        outs[0][...] = _ln_fwd(r, g_ref[...], b_ref[...])
        outs[1][...] = r
        outs[2][...] = m

    return _rowcall(name, fn, t, tm, seq // tm, tiled=[yssd, ys5, x1], exs=[mod], res=[w_out, lng, lnb],
                    out_tiled=[_sds((t, D)), _sds((t, D)), _sds((t, D))])


def _out_bwd(name, dxo, r, m, mod, lng, w_outt, seq):
    t = dxo.shape[0]
    tm = 256

    def fn(ctx, tiled, halos, exs, res, outs, accs, exaccs, scr):
        dxo_ref, r_ref, m_ref = tiled
        mod_ref, = exs
        g_ref, wt_ref = res
        dr, dgam, dbet = _ln_bwd(r_ref[...], g_ref[...], dxo_ref[...])
        outs[0][...] = dr
        _acc(accs[0], dgam, ctx.first)
        _acc(accs[1], dbet, ctx.first)
        _acc(exaccs[0].at[0], jnp.sum(dr * m_ref[...], axis=0, keepdims=True), ctx.ex_enter)
        dm = _mx(mod_ref[0, 5:6, :] * dr)
        outs[1][...] = dm
        dyc = _dot(dm, wt_ref[...])
        outs[2][...] = dyc[:, 0:SW]
        outs[3][...] = dyc[:, SW:2 * SW]

    b = mod.shape[0]
    return _rowcall(name, fn, t, tm, seq // tm, tiled=[dxo, r, m], exs=[mod], res=[lng, w_outt],
                    out_tiled=[_sds((t, D)), _sds((t, D), MXU_DTYPE), _sds((t, SW)), _sds((t, SW))],
                    out_acc=[_sds((1, D)), _sds((1, D))], out_exacc=[_sds((b, 1, D))])


def _s5_discretise(a_re, a_im, log_dt, b_re, b_im):
    dt = jnp.exp(log_dt)[:, None]
    mag = jnp.exp(dt * a_re)
    ab_re, ab_im = mag * jnp.cos(dt * a_im), mag * jnp.sin(dt * a_im)
    den = a_re * a_re + a_im * a_im
    nr, ni = ab_re - 1.0, ab_im
    f_re, f_im = (nr * a_re + ni * a_im) / den, (ni * a_re - nr * a_im) / den
    bb_re = f_re[..., None] * b_re - f_im[..., None] * b_im
    bb_im = f_re[..., None] * b_im + f_im[..., None] * b_re
    return ab_re, ab_im, bb_re, bb_im


def _s5_tables(ab_re, ab_im):
    ar, ai = ab_re.reshape(1, S5L), ab_im.reshape(1, S5L)
    pows = [(ar, ai)]
    for _ in range(7):
        pr, pi = pows[-1]
        pows.append((pr * ar - pi * ai, pr * ai + pi * ar))
    z = jnp.zeros((1, S5L), F32)

    def pack(rows, sign):
        return jnp.concatenate([jnp.concatenate([r for r, _ in rows], axis=0),
                                jnp.concatenate([sign * i for _, i in rows], axis=0)], axis=1)

    sel = [pows[0], pows[1], pows[3]] + [(z, z)] * 5
    pw, pwc = pack(sel, 1.0), pack(sel, -1.0)
    tab = pack(pows, 1.0)
    tabc = pack(pows[::-1], -1.0)
    return pw, tab, pwc, tabc


def _local_step(x, tgt, mod, w, sp, seq):
    t = x.shape[0]
    mxu = MXU_DTYPE
    x1, r1, h1, ab1, f1 = _ffn_fwd("ffn1_fwd", x, mod, 0, w["ffn1_w13"], w["ffn1_w2"], sp["ln1_g"], sp["ln1_b"], seq)
    h2, proj = _proj_fwd("proj_fwd", x1, mod, w["w_in"], seq)
    yssd, xpre, yraw, sprev = _ssd_fwd("ssd_fwd", proj, sp["conv_w"], sp["conv_b"], sp["dt_bias"], sp["a_log"],
                                       sp["d_rep"], sp["ssd_norm_w"], seq)
    (ab_re, ab_im, bb_re, bb_im), disc_vjp = jax.vjp(_s5_discretise, sp["s5_a_re"], sp["s5_a_im"], sp["s5_log_dt"],
                                                     sp["s5_b_re"], sp["s5_b_im"])
    eye = jnp.eye(S5G, dtype=F32)
    bbd = jnp.concatenate([jnp.einsum("gk,gph->ghkp", eye, bb_re).reshape(SW, S5L),
                           jnp.einsum("gk,gph->ghkp", eye, bb_im).reshape(SW, S5L)], axis=1).astype(mxu)
    cbd = jnp.concatenate([jnp.einsum("gk,ghp->gpkh", eye, sp["s5_c_re"]).reshape(S5L, SW),
                           -jnp.einsum("gk,ghp->gpkh", eye, sp["s5_c_im"]).reshape(S5L, SW)], axis=0).astype(mxu)
    pw, tab, pwc, tabc = _s5_tables(lax.stop_gradient(ab_re), lax.stop_gradient(ab_im))
    ys5, xst, y5 = _s5_fwd("s5_fwd", proj, bbd, cbd, pw, tab, sp["s5_d"], w["w_glu"], sp["b_glu"], seq)
    x2, r2, m2 = _out_fwd("out_fwd", yssd, ys5, x1, mod, w["w_out"], sp["ln2_g"], sp["ln2_b"], seq)
    x3, r3, h3, ab3, f3, dy3, loss_acc = _ffn_fwd("ffn2_fwd", x2, mod, 6, w["ffn2_w13"], w["ffn2_w2"], sp["ln3_g"],
                                                  sp["ln3_b"], seq, tgt=tgt)
    dr3, df3, s3, dab3, dg3g, dg3b, dgate3 = _ffn_bwd1("ffn2_bwd1", dy3, r3, ab3, f3, mod, 6, sp["ln3_g"],
                                                        w["ffn2_w2t"], seq)
    dx2, dsh3, dsc3 = _mod_bwd("ffn2_bwd2", dab3, dr3, x2, mod, 6, w["ffn2_w13t"], seq)
    g_ffn2_w13 = _tn_matmul("ffn2_dw13", h3, dab3, D, FB)
    g_ffn2_w2 = _tn_matmul("ffn2_dw2", s3, df3, FB, D)
    dr2, dm2, dyssd, dys5, dg2g, dg2b, dgate2 = _out_bwd("out_bwd", dx2, r2, m2, mod, sp["ln2_g"], w["w_outt"], seq)
    g_w_out = jnp.concatenate([_tn_matmul("dw_out_a", yssd, dm2, SW, D), _tn_matmul("dw_out_b", ys5, dm2, SW, D)], axis=0)
    du, lam, dy5b, g5b, dv5b, da5, dd5, dbglu = _s5_bwd("s5_bwd", dys5, proj, xst, y5, bbd.T, cbd.T, pwc, tabc,
                                                       sp["s5_d"], w["w_glu"], w["w_glut"], sp["b_glu"], seq)
    g_w_glu = _tn_matmul("dw_glu", g5b, dv5b, SW, SW)
    dbfull = _tn_matmul("s5_db", lam, _mx(proj[:, 1536:2048]), D, SW)
    dcfull = _tn_matmul("s5_dc", _mx(xst), dy5b, D, SW)
    dzx, ddt, dnw, ddl, dcw, dcb, ddtb, dalog = _ssd_bwd("ssd_bwd", dyssd, proj, xpre, yraw, sprev, sp["conv_w"],
                                                         sp["dt_bias"], sp["a_log"], sp["d_rep"], sp["ssd_norm_w"], seq)
    dx1, dproj, dsh2, dsc2 = _mod_bwd("proj_bwd", dzx, dr2, x1, mod, 3, w["w_int"], seq, extra=(du, ddt))
    g_w_in = _tn_matmul("dw_in", h2, dproj, D, PW)
    dr1, df1, s1, dab1, dg1g, dg1b, dgate1 = _ffn_bwd1("ffn1_bwd1", dx1, r1, ab1, f1, mod, 0, sp["ln1_g"],
                                                        w["ffn1_w2t"], seq)
    dx0, dsh1, dsc1 = _mod_bwd("ffn1_bwd2", dab1, dr1, x, mod, 0, w["ffn1_w13t"], seq)
    g_ffn1_w13 = _tn_matmul("ffn1_dw13", h1, dab1, D, FB)
    g_ffn1_w2 = _tn_matmul("ffn1_dw2", s1, df1, FB, D)
    dmod = jnp.concatenate([dsh1, dsc1, dgate1, dsh2, dsc2, dgate2, dsh3, dsc3, dgate3], axis=1)
    d4 = lambda a: a.reshape(S5G, S5P, S5G, S5H)
    dbb_re = jnp.einsum("gpgh->gph", d4(dbfull[:S5L]))
    dbb_im = jnp.einsum("gpgh->gph", d4(dbfull[S5L:]))
    dc_re = jnp.einsum("gpgh->ghp", d4(dcfull[:S5L]))
    dc_im = -jnp.einsum("gpgh->ghp", d4(dcfull[S5L:]))
    g_a_re, g_a_im, g_log_dt, g_b_re, g_b_im = disc_vjp(
        (da5[:, :S5L].reshape(S5G, S5P), da5[:, S5L:].reshape(S5G, S5P), dbb_re, dbb_im))
    small = dict(ln1_g=dg1g, ln1_b=dg1b, ln2_g=dg2g, ln2_b=dg2b, ln3_g=dg3g, ln3_b=dg3b, conv_w=dcw[0:4], conv_b=dcb,
                 dt_bias=ddtb[:, :NH], a_log=dalog[:, :NH], d_ssd=jnp.sum(ddl.reshape(NH, HP), axis=1).reshape(1, NH),
                 ssd_norm_w=dnw, s5_a_re=g_a_re, s5_a_im=g_a_im, s5_log_dt=g_log_dt, s5_b_re=g_b_re, s5_b_im=g_b_im,
                 s5_c_re=dc_re, s5_c_im=dc_im, s5_d=dd5, w_glu_b=dbglu)
    big = dict(ffn1_w13=g_ffn1_w13, ffn1_w2=g_ffn1_w2, w_in=g_w_in, w_glu=g_w_glu, w_out=g_w_out,
               ffn2_w13=g_ffn2_w13, ffn2_w2=g_ffn2_w2)
    return loss_acc[0, 0], dx0, dmod, big, small


def _place():
    return lax.axis_index("x"), lax.axis_index("y"), lax.axis_index("c")


def _other_chips(x, y):
    return [(1 - x, y), (x, 1 - y), (1 - x, 1 - y)]


def _allgather8(name, a):
    r, n = a.shape

    def body(x_ref, out_ref, send_sems, recv_sems, local_sem):
        x, y, c = _place()
        me, sibling = (x, y, c), (x, y, 1 - c)
        chips = _other_chips(x, y)

        def rows(px, py, pc):
            return out_ref.at[pl.ds(pl.multiple_of((4 * px + 2 * py + pc) * r, 8), r), :]

        def copy(k, block, to, src=None):
            return pltpu.make_async_remote_copy(src_ref=rows(*block) if src is None else src, dst_ref=rows(*block),
                                                send_sem=send_sems.at[k], recv_sem=recv_sems.at[k], device_id=to,
                                                device_id_type=MESH_T)

        mine = pltpu.make_async_copy(x_ref, rows(*me), local_sem)
        mine.start()
        first = [copy(0, me, sibling, src=x_ref)]
        first += [copy(1 + j, me, (*chip, c), src=x_ref) for j, chip in enumerate(chips)]
        for cp in first:
            cp.start()
        passed = [copy(4 + j, (*chip, c), sibling) for j, chip in enumerate(chips)]
        for j, chip in enumerate(chips):
            copy(1 + j, (*chip, c), me).wait_recv()
            passed[j].start()
        copy(0, sibling, me).wait_recv()
        for j, chip in enumerate(chips):
            copy(4 + j, (*chip, 1 - c), me).wait_recv()
        for cp in first + passed:
            cp.wait_send()
        mine.wait()

    out = _pcall(body, name=name, out_shape=_sds((8 * r, n), a.dtype),
                 in_specs=[pl.BlockSpec(memory_space=pltpu.VMEM)], out_specs=pl.BlockSpec(memory_space=pltpu.VMEM),
                 scratch_shapes=[pltpu.SemaphoreType.DMA((7,)), pltpu.SemaphoreType.DMA((7,)), pltpu.SemaphoreType.DMA],
                 compiler_params=_cparams())(a)
    return out.reshape(8, r, n)


def _gather_weights(name, shard):
    _, rh, n = shard.shape

    def body(s_ref, out_ref, send_sems, recv_sems, local_sem):
        x, y, c = _place()
        me, sibling = (x, y, c), (x, y, 1 - c)
        chips = _other_chips(x, y)

        def copy(k, chip, half, to, src=None):
            dst = out_ref.at[2 * chip[0] + chip[1], half]
            return pltpu.make_async_remote_copy(src_ref=dst if src is None else src, dst_ref=dst,
                                                send_sem=send_sems.at[k], recv_sem=recv_sems.at[k], device_id=to,
                                                device_id_type=MESH_T)

        mine = pltpu.make_async_copy(s_ref, out_ref.at[2 * x + y], local_sem)
        mine.start()
        first = [copy(j, (x, y), c, (*chip, c), src=s_ref.at[c]) for j, chip in enumerate(chips)]
        for cp in first:
            cp.start()
        passed = [copy(3 + j, chip, c, sibling) for j, chip in enumerate(chips)]
        for j, chip in enumerate(chips):
            copy(j, chip, c, me).wait_recv()
            passed[j].start()
        for j, chip in enumerate(chips):
            copy(3 + j, chip, 1 - c, me).wait_recv()
        for cp in first + passed:
            cp.wait_send()
        mine.wait()

    return _pcall(body, name=name, out_shape=_sds((4, 2, rh, n), shard.dtype),
                  in_specs=[pl.BlockSpec(memory_space=pl.ANY)], out_specs=pl.BlockSpec(memory_space=pl.ANY),
                  scratch_shapes=[pltpu.SemaphoreType.DMA((6,)), pltpu.SemaphoreType.DMA((6,)), pltpu.SemaphoreType.DMA],
                  compiler_params=_cparams())(shard)


def _send_sibling_half(name, g):
    def body(g_ref, out_ref, send_sem, recv_sem):
        x, y, c = _place()
        cp = pltpu.make_async_remote_copy(src_ref=g_ref.at[1 - c], dst_ref=out_ref, send_sem=send_sem, recv_sem=recv_sem,
                                          device_id=(x, y, 1 - c), device_id_type=MESH_T)
        cp.start()
        cp.wait()

    return _pcall(body, name=name, out_shape=_sds(g.shape[1:], g.dtype),
                  in_specs=[pl.BlockSpec(memory_space=pl.ANY)], out_specs=pl.BlockSpec(memory_space=pl.ANY),
                  scratch_shapes=[pltpu.SemaphoreType.DMA, pltpu.SemaphoreType.DMA], compiler_params=_cparams())(g)


def _scatter_chips(name, h):
    _, rh, n = h.shape

    def body(h_ref, out_ref, send_sems, recv_sems):
        x, y, c = _place()
        chips = _other_chips(x, y)
        cps = [pltpu.make_async_remote_copy(src_ref=h_ref.at[2 * chip[0] + chip[1]], dst_ref=out_ref.at[j],
                                            send_sem=send_sems.at[j], recv_sem=recv_sems.at[j], device_id=(*chip, c),
                                            device_id_type=MESH_T) for j, chip in enumerate(chips)]
        for cp in cps:
            cp.start()
        for cp in cps:
            cp.wait()

    return _pcall(body, name=name, out_shape=_sds((3, rh, n), h.dtype),
                  in_specs=[pl.BlockSpec(memory_space=pl.ANY)], out_specs=pl.BlockSpec(memory_space=pl.ANY),
                  scratch_shapes=[pltpu.SemaphoreType.DMA((3,)), pltpu.SemaphoreType.DMA((3,))],
                  compiler_params=_cparams())(h)


def _join_halves(name, f):
    rh, n = f.shape

    def body(f_ref, out_ref, send_sem, recv_sem, local_sem):
        x, y, c = _place()
        mine = pltpu.make_async_copy(f_ref, out_ref.at[c], local_sem)
        mine.start()
        cp = pltpu.make_async_remote_copy(src_ref=f_ref, dst_ref=out_ref.at[c], send_sem=send_sem, recv_sem=recv_sem,
                                          device_id=(x, y, 1 - c), device_id_type=MESH_T)
        cp.start()
        cp.wait_send()
        pltpu.make_async_remote_copy(src_ref=f_ref, dst_ref=out_ref.at[1 - c], send_sem=send_sem, recv_sem=recv_sem,
                                     device_id=(x, y, 1 - c), device_id_type=MESH_T).wait_recv()
        mine.wait()

    return _pcall(body, name=name, out_shape=_sds((2, rh, n), f.dtype),
                  in_specs=[pl.BlockSpec(memory_space=pl.ANY)], out_specs=pl.BlockSpec(memory_space=pl.ANY),
                  scratch_shapes=[pltpu.SemaphoreType.DMA, pltpu.SemaphoreType.DMA, pltpu.SemaphoreType.DMA],
                  compiler_params=_cparams())(f)


def _row_block(r, cap=2048):
    b = min(r, cap)
    while r % b or b % 8:
        b -= 8
    return b


def _add_selected(name, g, r1, sel):
    _, _, rh, n = g.shape
    br = _row_block(rh)

    def body(sel_ref, g_ref, r_ref, o_ref):
        o_ref[...] = g_ref[0] + r_ref[...]

    return _pcall(body, name=name, out_shape=_sds((4, rh, n)),
                  grid_spec=pltpu.PrefetchScalarGridSpec(
                      num_scalar_prefetch=1, grid=(4, rh // br),
                      in_specs=[pl.BlockSpec((1, 1, br, n), lambda i, j, s: (s[0], i, j, 0)),
                                pl.BlockSpec((1, br, n), lambda i, j, s: (i, j, 0))],
                      out_specs=pl.BlockSpec((1, br, n), lambda i, j, s: (i, j, 0))),
                  compiler_params=_cparams(("parallel", "parallel")))(sel.reshape(1).astype(jnp.int32), g, r1)


def _sum_selected(name, h, r2, sel):
    _, rh, n = h.shape
    br = _row_block(rh)

    def body(sel_ref, h_ref, r_ref, o_ref):
        o_ref[...] = ((h_ref[0] + r_ref[0]) + r_ref[1]) + r_ref[2]

    return _pcall(body, name=name, out_shape=_sds((rh, n)),
                  grid_spec=pltpu.PrefetchScalarGridSpec(
                      num_scalar_prefetch=1, grid=(rh // br,),
                      in_specs=[pl.BlockSpec((1, br, n), lambda j, s: (s[0], j, 0)),
                                pl.BlockSpec((3, br, n), lambda j, s: (0, j, 0))],
                      out_specs=pl.BlockSpec((br, n), lambda j, s: (j, 0))),
                  compiler_params=_cparams(("parallel",)))(sel.reshape(1).astype(jnp.int32), h, r2)


def _sum8(name, a):
    _, r, n = a.shape
    br = _row_block(r)

    def body(a_ref, o_ref):
        acc = a_ref[0]
        for k in range(1, 8):
            acc = acc + a_ref[k]
        o_ref[...] = acc

    return _pcall(body, name=name, out_shape=_sds((r, n)), grid=(r // br,),
                  in_specs=[pl.BlockSpec((8, br, n), lambda j: (0, j, 0))], out_specs=pl.BlockSpec((br, n), lambda j: (j, 0)),
                  compiler_params=_cparams(("parallel",)))(a)


def _adamw(name, w, g, m, v):
    r, n = w.shape
    br = _row_block(r)

    def body(w_ref, g_ref, m_ref, v_ref, d_ref, nm_ref, nv_ref):
        gv = g_ref[...]
        nm = ADAM_B1 * m_ref[...] + (1.0 - ADAM_B1) * gv
        nv = ADAM_B2 * v_ref[...] + (1.0 - ADAM_B2) * (gv * gv)
        nm_ref[...] = nm
        nv_ref[...] = nv
        m_hat = nm / (1.0 - ADAM_B1 ** ADAM_STEP)
        v_hat = nv / (1.0 - ADAM_B2 ** ADAM_STEP)
        d_ref[...] = -ADAM_LR * (m_hat / (jnp.sqrt(v_hat) + ADAM_EPS) + ADAM_WD * w_ref[...])

    spec = pl.BlockSpec((br, n), lambda j: (j, 0))
    return _pcall(body, name=name, out_shape=[_sds((r, n))] * 3, grid=(r // br,), in_specs=[spec] * 4,
                  out_specs=[spec] * 3, compiler_params=_cparams(("parallel",)))(w, g, m, v)


ADA_COLS = 2304
ADA_BLK = 768


def _ada_fwd(name, c_all, w_shard, b_cols):
    nb = c_all.shape[0]

    def body(c_ref, w_ref, b_ref, o_ref):
        cv = c_ref[...]
        cs = _mx(cv * _sigmoid(cv))
        o_ref[...] = _dot(cs, _mx(w_ref[...])) + b_ref[...]

    return _pcall(body, name=name, out_shape=_sds((nb, ADA_COLS)), grid=(ADA_COLS // ADA_BLK,),
                  in_specs=[pl.BlockSpec((nb, D), lambda j: (0, 0)), pl.BlockSpec((D, ADA_BLK), lambda j: (0, j)),
                            pl.BlockSpec((1, ADA_BLK), lambda j: (0, j))],
                  out_specs=pl.BlockSpec((nb, ADA_BLK), lambda j: (0, j)),
                  compiler_params=_cparams(("parallel",)))(c_all, w_shard, b_cols)


def _ada_bwd(name, c_all, dmod_cols, dmod_all):
    nb = c_all.shape[0]

    def body(c_ref, dc_ref, da_ref, gw_ref, gb_ref):
        cv = c_ref[...]
        cs = _mx(cv * _sigmoid(cv))
        gw_ref[...] = lax.dot_general(cs, _mx(dc_ref[...]), (((0,), (0,)), ((), ())), preferred_element_type=F32)

        @pl.when(pl.program_id(0) == 0)
        def _():
            gb_ref[...] = jnp.sum(da_ref[...], axis=0, keepdims=True)

    return _pcall(body, name=name, out_shape=[_sds((D, ADA_COLS)), _sds((1, 9 * D))], grid=(ADA_COLS // ADA_BLK,),
                  in_specs=[pl.BlockSpec((nb, D), lambda j: (0, 0)), pl.BlockSpec((nb, ADA_BLK), lambda j: (0, j)),
                            pl.BlockSpec((nb, 9 * D), lambda j: (0, 0))],
                  out_specs=[pl.BlockSpec((D, ADA_BLK), lambda j: (0, j)), pl.BlockSpec((1, 9 * D), lambda j: (0, 0))],
                  compiler_params=_cparams(("arbitrary",)))(c_all, dmod_cols, dmod_all)


BIG = ("ffn1_w1", "ffn1_w3", "ffn1_w2", "w_in", "w_glu", "w_out", "ffn2_w1", "ffn2_w3", "ffn2_w2")
COL_SHARDED = ("ffn1_w1", "ffn1_w3", "w_in", "ffn2_w1", "ffn2_w3")
SMALL = ("b_ada", "ln1_g", "ln1_b", "conv_w", "conv_b", "dt_bias", "a_log", "d_ssd", "ssd_norm_w", "s5_a_re", "s5_a_im",
         "s5_log_dt", "s5_b_re", "s5_b_im", "s5_c_re", "s5_c_im", "s5_d", "b_glu", "ln2_g", "ln2_b", "ln3_g", "ln3_b")
WEIGHTS = ("w_ada", "b_ada", "ffn1_w1", "ffn1_w3", "ffn1_w2", "ln1_g", "ln1_b", "w_in", "conv_w", "conv_b", "dt_bias",
           "a_log", "d_ssd", "ssd_norm_w", "s5_a_re", "s5_a_im", "s5_log_dt", "s5_b_re", "s5_b_im", "s5_c_re", "s5_c_im",
           "s5_d", "w_glu", "b_glu", "w_out", "ln2_g", "ln2_b", "ffn2_w1", "ffn2_w3", "ffn2_w2", "ln3_g", "ln3_b")
BIG_PAD = 2 * 1024 * 128


def _pack(arrs, mult, axis_keep=0):
    lead = arrs[0].shape[:axis_keep]
    flat = jnp.concatenate([a.reshape(lead + (-1,)) for a in arrs], axis=axis_keep)
    pad = (-flat.shape[-1]) % mult
    if pad:
        flat = jnp.concatenate([flat, jnp.zeros(lead + (pad,), flat.dtype)], axis=axis_keep)
    return flat


def _unpack(flat, shapes):
    out, off = [], 0
    for s in shapes:
        size = math.prod(s)
        out.append(flat[..., off:off + size].reshape(flat.shape[:-1] + tuple(s)))
        off += size
    return out


def _shard_major(a):
    rows, cols = a.shape
    return a.reshape(rows, 4, cols // 4).transpose(1, 0, 2)


def _from_shard_major(a):
    _, rows, w = a.shape
    return a.transpose(1, 0, 2).reshape(rows, 4 * w)


def kernel(x, c, w_ada, b_ada, ffn1_w1, ffn1_w3, ffn1_w2, ln1_g, ln1_b, w_in, conv_w, conv_b, dt_bias, a_log, d_ssd, ssd_norm_w, s5_a_re, s5_a_im, s5_log_dt, s5_b_re, s5_b_im, s5_c_re, s5_c_im, s5_d, w_glu, b_glu, w_out, ln2_g, ln2_b, ffn2_w1, ffn2_w3, ffn2_w2, ln3_g, ln3_b, loss_target, m_w_ada, m_b_ada, m_ffn1_w1, m_ffn1_w3, m_ffn1_w2, m_ln1_g, m_ln1_b, m_w_in, m_conv_w, m_conv_b, m_dt_bias, m_a_log, m_d_ssd, m_ssd_norm_w, m_s5_a_re, m_s5_a_im, m_s5_log_dt, m_s5_b_re, m_s5_b_im, m_s5_c_re, m_s5_c_im, m_s5_d, m_w_glu, m_b_glu, m_w_out, m_ln2_g, m_ln2_b, m_ffn2_w1, m_ffn2_w3, m_ffn2_w2, m_ln3_g, m_ln3_b, v_w_ada, v_b_ada, v_ffn1_w1, v_ffn1_w3, v_ffn1_w2, v_ln1_g, v_ln1_b, v_w_in, v_conv_w, v_conv_b, v_dt_bias, v_a_log, v_d_ssd, v_ssd_norm_w, v_s5_a_re, v_s5_a_im, v_s5_log_dt, v_s5_b_re, v_s5_b_im, v_s5_c_re, v_s5_c_im, v_s5_d, v_w_glu, v_b_glu, v_w_out, v_ln2_g, v_ln2_b, v_ffn2_w1, v_ffn2_w3, v_ffn2_w2, v_ln3_g, v_ln3_b):
    a = dict(locals())
    xi, yi, ci = _place()
    chip = 2 * xi + yi
    dev = 2 * chip + ci
    nb, seq, _ = x.shape
    t = nb * seq
    ndev = 8

    c_all = _allgather8("gather_c", c.reshape(-1, 128)).reshape(ndev * nb, D)
    b_cols = lax.dynamic_slice(b_ada, (0, chip * ADA_COLS), (1, ADA_COLS))
    mod_part = _ada_fwd("ada_fwd", c_all, w_ada[0], b_cols)
    mod_parts = _allgather8("gather_mod", mod_part.reshape(-1, 128)).reshape(ndev, ndev * nb, ADA_COLS)
    mod_all = mod_parts[0::2].transpose(1, 0, 2).reshape(ndev * nb, 9 * D)
    mod = lax.dynamic_slice(mod_all, (nb * dev, 0), (nb, 9 * D)).reshape(nb, 9, D)

    big_shapes = [a[n].shape[1:] for n in BIG]
    wflat = _pack([a[n][0].astype(MXU_DTYPE) for n in BIG], BIG_PAD)
    rh = wflat.shape[0] // 256
    wg = _gather_weights("gather_w", wflat.reshape(2, rh, 128)).reshape(4, -1)
    full = {}
    for n, piece in zip(BIG, _unpack(wg, big_shapes)):
        full[n] = _from_shard_major(piece) if n in COL_SHARDED else piece.reshape(-1, piece.shape[-1])
    wi = full["w_in"]
    w_inp = jnp.concatenate([wi[:, :1536], wi[:, 1544:2056], wi[:, 1536:1544], jnp.zeros((D, 120), wi.dtype)], axis=1)
    w = dict(ffn1_w13=jnp.concatenate([full["ffn1_w1"], full["ffn1_w3"]], axis=1), ffn1_w2=full["ffn1_w2"],
             ffn2_w13=jnp.concatenate([full["ffn2_w1"], full["ffn2_w3"]], axis=1), ffn2_w2=full["ffn2_w2"],
             w_in=w_inp, w_glu=full["w_glu"], w_out=full["w_out"])
    for n in ("ffn1_w13", "ffn1_w2", "ffn2_w13", "ffn2_w2", "w_in", "w_out"):
        w[n + "t"] = w[n].T
    w["w_glut"] = w["w_glu"].T

    cw_all = _allgather8("gather_conv_w", _pack([conv_w[0]], 1024).reshape(-1, 128)).reshape(ndev, -1)[0::2, :4 * 256]
    conv_full = _from_shard_major(cw_all.reshape(4, 4, 256))
    pad8 = lambda v: jnp.concatenate([v.reshape(1, NH), jnp.zeros((1, 128 - NH), F32)], axis=1)
    sp = dict(ln1_g=ln1_g, ln1_b=ln1_b, ln2_g=ln2_g, ln2_b=ln2_b, ln3_g=ln3_g, ln3_b=ln3_b, conv_w=conv_full,
              conv_b=conv_b, dt_bias=pad8(dt_bias), a_log=pad8(a_log), d_rep=jnp.repeat(d_ssd[0], HP)[None],
              ssd_norm_w=ssd_norm_w, s5_a_re=s5_a_re[0], s5_a_im=s5_a_im[0], s5_log_dt=s5_log_dt[0], s5_b_re=s5_b_re[0],
              s5_b_im=s5_b_im[0], s5_c_re=s5_c_re[0], s5_c_im=s5_c_im[0], s5_d=s5_d, b_glu=b_glu)

    lsum, dx0, dmod, big, small = _local_step(x.reshape(t, D), loss_target.reshape(t, D), mod, w, sp, seq)
    loss = lax.psum(lsum * (0.5 / D), ("x", "y", "c"))

    dmod_all = _allgather8("gather_dmod", dmod.reshape(-1, 128)).reshape(ndev * nb, 9 * D)
    dmod_cols = lax.dynamic_slice(dmod_all, (0, chip * ADA_COLS), (ndev * nb, ADA_COLS))
    g_w_ada, g_b_ada = _ada_bwd("ada_bwd", c_all, dmod_cols, dmod_all)

    gwi = big["w_in"]
    gfull = dict(ffn1_w1=big["ffn1_w13"][:, :FF], ffn1_w3=big["ffn1_w13"][:, FF:], ffn1_w2=big["ffn1_w2"],
                 w_in=jnp.concatenate([gwi[:, :1536], gwi[:, 2048:2056], gwi[:, 1536:2048]], axis=1),
                 w_glu=big["w_glu"], w_out=big["w_out"],
                 ffn2_w1=big["ffn2_w13"][:, :FF], ffn2_w3=big["ffn2_w13"][:, FF:], ffn2_w2=big["ffn2_w2"])
    pieces = [_shard_major(gfull[n]) if n in COL_SHARDED else gfull[n].reshape(4, -1) for n in BIG]
    gpack = _pack(pieces, BIG_PAD, axis_keep=1).reshape(4, 2, rh, 128).transpose(1, 0, 2, 3)
    r1 = _send_sibling_half("rs_sibling", gpack)
    hsum = _add_selected("rs_add", gpack, r1, ci)
    r2 = _scatter_chips("rs_chips", hsum)
    fhalf = _sum_selected("rs_sum", hsum, r2, chip)
    gbig = _join_halves("rs_join", fhalf).reshape(2 * rh, 128)

    outs = {}
    pk = lambda pre: _pack([a[pre + n][0] for n in BIG], BIG_PAD).reshape(2 * rh, 128)
    d_big, m_big, v_big = _adamw("adamw_big", pk(""), gbig, pk("m_"), pk("v_"))
    for kind, vec in (("grad", gbig), ("delta", d_big), ("new_m", m_big), ("new_v", v_big)):
        for n, arr in zip(BIG, _unpack(vec.reshape(-1), big_shapes)):
            outs[kind, n] = arr[None]

    flat2 = lambda u: u.reshape(-1, 128)
    d_ada, m_ada, v_ada = _adamw("adamw_ada", flat2(w_ada[0]), flat2(g_w_ada), flat2(m_w_ada[0]), flat2(v_w_ada[0]))
    for kind, vec in (("grad", g_w_ada), ("delta", d_ada), ("new_m", m_ada), ("new_v", v_ada)):
        outs[kind, "w_ada"] = vec.reshape(w_ada.shape)

    snames = [n for n in SMALL if n != "b_ada"]
    sgrad = dict(small)
    sgrad["b_glu"] = small["w_glu_b"]
    svec = _pack([sgrad[n] for n in snames], 1024).reshape(-1, 128)
    ssum = _sum8("small_sum", _allgather8("gather_small", svec)).reshape(-1)
    full_shapes = {n: (a[n].shape[1:] if n != "conv_w" else (4, D)) for n in snames}
    gsm = dict(zip(snames, _unpack(ssum, [full_shapes[n] for n in snames])))
    gsm["conv_w"] = lax.dynamic_slice(gsm["conv_w"], (0, chip * 256), (4, 256))
    gsm["b_ada"] = g_b_ada.reshape(-1)
    sshapes = [a[n].shape[1:] for n in SMALL]
    spk = lambda d: _pack([d[n].reshape(-1) for n in SMALL], 1024).reshape(-1, 128)
    gs_vec = spk(gsm)
    d_s, m_s, v_s = _adamw("adamw_small", spk({n: a[n] for n in SMALL}), gs_vec, spk({n: a["m_" + n] for n in SMALL}),
                           spk({n: a["v_" + n] for n in SMALL}))
    for kind, vec in (("grad", gs_vec), ("delta", d_s), ("new_m", m_s), ("new_v", v_s)):
        for n, arr in zip(SMALL, _unpack(vec.reshape(-1), sshapes)):
            outs[kind, n] = arr[None]

    res = [loss, dx0.reshape(nb, seq, D)]
    for kind in ("grad", "delta", "new_m", "new_v"):
        res += [outs[kind, n] for n in WEIGHTS]
    return tuple(res)
```

```python
import functools
import math

import jax
import jax.numpy as jnp
from jax import lax
from jax.experimental import pallas as pl
from jax.experimental.pallas import tpu as pltpu

F32 = jnp.float32
BF16 = jnp.bfloat16
MXU_DTYPE = jnp.bfloat16

D = 1024
FF = 2816
FB = 1408
NH, HP, NS, NG = 8, 64, 128, 2
CH = 128
SW = 512
S5G, S5P, S5H = 32, 64, 16
S5L = S5G * S5P
PW = 2176
ALPHA = 2.0 ** 0.25
LN_EPS = 1e-5
ADAM_LR, ADAM_B1, ADAM_B2, ADAM_EPS, ADAM_WD, ADAM_STEP = 0.001, 0.9, 0.999, 1e-08, 0.01, 10
VMEM_LIMIT = 56 * 1024 * 1024
MESH_T = pl.DeviceIdType.MESH


def _pcall(body, **kw):
    return pl.pallas_call(body, **kw)


def _cparams(sem=None, **kw):
    return pltpu.CompilerParams(dimension_semantics=sem, vmem_limit_bytes=VMEM_LIMIT, **kw)


def _dot(a, b):
    return jnp.dot(a, b, preferred_element_type=F32)


def _dot_nt(a, b):
    return lax.dot_general(a, b, (((1,), (1,)), ((), ())), preferred_element_type=F32)


def _dot_hi(a, b):
    return jnp.dot(a, b, preferred_element_type=F32, precision=lax.Precision.HIGHEST)


def _mx(a):
    return a.astype(MXU_DTYPE)


def _sigmoid(x):
    return 1.0 / (1.0 + jnp.exp(-x))


def _iota(shape, axis):
    return lax.broadcasted_iota(jnp.int32, shape, axis)


def _rowcall(name, fn, n_rows, tm, tpe, *, tiled=(), halos=(), exs=(), res=(), out_tiled=(), out_acc=(),
             out_exacc=(), scratch=(), reverse=False):
    nt = n_rows // tm

    def blk(i):
        return (nt - 1 - i) if reverse else i

    in_specs, args = [], []
    for a in tiled:
        in_specs.append(pl.BlockSpec((tm, a.shape[1]), lambda i: (blk(i), 0)))
        args.append(a)
    for a, rows in halos:
        r = tm // rows
        in_specs.append(pl.BlockSpec((rows, a.shape[1]), lambda i, r=r: (jnp.maximum(blk(i) * r - 1, 0), 0)))
        args.append(a)
    for a in exs:
        in_specs.append(pl.BlockSpec((1,) + a.shape[1:], lambda i: (blk(i) // tpe, 0, 0)))
        args.append(a)
    for a in res:
        nd = a.ndim
        in_specs.append(pl.BlockSpec(a.shape, lambda i, nd=nd: (0,) * nd, pipeline_mode=pl.Buffered(1)))
        args.append(a)
    out_specs, out_shape = [], []
    for s in out_tiled:
        out_specs.append(pl.BlockSpec((tm, s.shape[1]), lambda i: (blk(i), 0)))
        out_shape.append(s)
    for s in out_acc:
        nd = len(s.shape)
        out_specs.append(pl.BlockSpec(s.shape, lambda i, nd=nd: (0,) * nd))
        out_shape.append(s)
    for s in out_exacc:
        out_specs.append(pl.BlockSpec((1,) + s.shape[1:], lambda i: (blk(i) // tpe, 0, 0)))
        out_shape.append(s)
    n = [len(tiled), len(halos), len(exs), len(res), len(out_tiled), len(out_acc), len(out_exacc), len(scratch)]

    def body(*refs):
        groups, k = [], 0
        for m in n:
            groups.append(refs[k:k + m])
            k += m
        i = pl.program_id(0)
        b = blk(i)

        class ctx:
            first = i == 0
            pos = b % tpe
            seq_first = (b % tpe) == 0
            seq_last = (b % tpe) == tpe - 1
            ex_enter = (i % tpe) == 0

        fn(ctx, *groups)

    return _pcall(body, name=name, grid=(nt,), in_specs=in_specs, out_specs=out_specs, out_shape=out_shape,
                  scratch_shapes=list(scratch), compiler_params=_cparams(("arbitrary",)))(*args)


def _acc(ref, val, first):
    @pl.when(first)
    def _():
        ref[...] = jnp.zeros(ref.shape, ref.dtype)
    ref[...] += val


def _sds(shape, dtype=F32):
    return jax.ShapeDtypeStruct(shape, dtype)


def _ln_fwd(r, g, b):
    mu = jnp.mean(r, axis=-1, keepdims=True)
    rc = r - mu
    var = jnp.mean(rc * rc, axis=-1, keepdims=True)
    return rc * lax.rsqrt(var + LN_EPS) * g + b


def _ln_bwd(r, g, dy):
    mu = jnp.mean(r, axis=-1, keepdims=True)
    rc = r - mu
    var = jnp.mean(rc * rc, axis=-1, keepdims=True)
    rstd = lax.rsqrt(var + LN_EPS)
    xhat = rc * rstd
    dxh = dy * g
    dr = rstd * (dxh - jnp.mean(dxh, axis=-1, keepdims=True) - xhat * jnp.mean(dxh * xhat, axis=-1, keepdims=True))
    return dr, jnp.sum(dy * xhat, axis=0, keepdims=True), jnp.sum(dy, axis=0, keepdims=True)


def _ffn_fwd(name, x, mod, k0, w13, w2, lng, lnb, seq, tgt=None):
    t = x.shape[0]
    tm = 256
    with_loss = tgt is not None

    def fn(ctx, tiled, halos, exs, res, outs, accs, exaccs, scr):
        x_ref = tiled[0]
        mod_ref, = exs
        w13_ref, w2_ref, g_ref, b_ref = res
        xo_ref, r_ref, h_ref, ab_ref, f_ref = outs[:5]
        xv = x_ref[...]
        sh, sc, g = mod_ref[0, k0:k0 + 1, :], mod_ref[0, k0 + 1:k0 + 2, :], mod_ref[0, k0 + 2:k0 + 3, :]
        h = _mx(xv * (1.0 + sc) + sh)
        h_ref[...] = h
        f = jnp.zeros((tm, D), F32)
        for j in range(2):
            a = _dot(h, w13_ref[:, j * FB:(j + 1) * FB])
            b = _dot(h, w13_ref[:, FF + j * FB:FF + (j + 1) * FB])
            ab_ref[:, j * FB:(j + 1) * FB] = a
            ab_ref[:, FF + j * FB:FF + (j + 1) * FB] = b
            s = a * _sigmoid(a) * b
            f = f + _dot(_mx(s), w2_ref[j * FB:(j + 1) * FB, :])
        f_ref[...] = f
        r = ALPHA * xv + 0.5 * g * f
        r_ref[...] = r
        xo = _ln_fwd(r, g_ref[...], b_ref[...])
        xo_ref[...] = xo
        if with_loss:
            e = xo - tiled[1][...]
            outs[5][...] = e * (1.0 / D)
            _acc(accs[0], jnp.sum(e * e) * jnp.ones((8, 128), F32), ctx.first)

    out_tiled = [_sds((t, D)), _sds((t, D)), _sds((t, D), MXU_DTYPE), _sds((t, 2 * FF)), _sds((t, D))]
    if with_loss:
        out_tiled.append(_sds((t, D)))
    return _rowcall(name, fn, t, tm, seq // tm, tiled=[x] + ([tgt] if with_loss else []), exs=[mod],
                    res=[w13, w2, lng, lnb], out_tiled=out_tiled, out_acc=[_sds((8, 128))] if with_loss else [])


def _ffn_bwd1(name, dxo, r, ab, f, mod, k0, lng, w2t, seq):
    t = dxo.shape[0]
    tm = 256

    def fn(ctx, tiled, halos, exs, res, outs, accs, exaccs, scr):
        dxo_ref, r_ref, ab_ref, f_ref = tiled
        mod_ref, = exs
        g_ref, w2t_ref = res
        dr_ref, df_ref, s_ref, dab_ref = outs
        g = mod_ref[0, k0 + 2:k0 + 3, :]
        dr, dgam, dbet = _ln_bwd(r_ref[...], g_ref[...], dxo_ref[...])
        dr_ref[...] = dr
        _acc(accs[0], dgam, ctx.first)
        _acc(accs[1], dbet, ctx.first)
        _acc(exaccs[0].at[0], jnp.sum(0.5 * f_ref[...] * dr, axis=0, keepdims=True), ctx.ex_enter)
        df = _mx(0.5 * g * dr)
        df_ref[...] = df
        for j in range(2):
            ds = _dot(df, w2t_ref[:, j * FB:(j + 1) * FB])
            a = ab_ref[:, j * FB:(j + 1) * FB]
            b = ab_ref[:, FF + j * FB:FF + (j + 1) * FB]
            sig = _sigmoid(a)
            silu = a * sig
            s_ref[:, j * FB:(j + 1) * FB] = _mx(silu * b)
            dab_ref[:, j * FB:(j + 1) * FB] = _mx(ds * b * (sig * (1.0 + a * (1.0 - sig))))
            dab_ref[:, FF + j * FB:FF + (j + 1) * FB] = _mx(ds * silu)

    b = mod.shape[0]
    return _rowcall(name, fn, t, tm, seq // tm, tiled=[dxo, r, ab, f], exs=[mod], res=[lng, w2t],
                    out_tiled=[_sds((t, D)), _sds((t, D), MXU_DTYPE), _sds((t, FF), MXU_DTYPE),
                               _sds((t, 2 * FF), MXU_DTYPE)],
                    out_acc=[_sds((1, D)), _sds((1, D))], out_exacc=[_sds((b, 1, D))])


def _mod_bwd(name, dab, dr, x, mod, k0, wt, seq, extra=()):
    t = dr.shape[0]
    tm = 256
    nin = 1 + len(extra)
    width = dab.shape[1] + sum(e.shape[1] for e in extra)

    def fn(ctx, tiled, halos, exs, res, outs, accs, exaccs, scr):
        parts = tiled[:nin]
        dr_ref, x_ref = tiled[nin:]
        mod_ref, = exs
        wt_ref, = res
        sc = mod_ref[0, k0 + 1:k0 + 2, :]
        if nin == 1:
            dp = parts[0][...]
        else:
            dp = jnp.concatenate([_mx(p[...]) for p in parts], axis=1)
            outs[1][...] = dp
        dh = _dot(dp, wt_ref[...])
        outs[0][...] = ALPHA * dr_ref[...] + dh * (1.0 + sc)
        _acc(exaccs[0].at[0], jnp.sum(dh, axis=0, keepdims=True), ctx.ex_enter)
        _acc(exaccs[1].at[0], jnp.sum(dh * x_ref[...], axis=0, keepdims=True), ctx.ex_enter)

    b = mod.shape[0]
    out_tiled = [_sds((t, D))] + ([_sds((t, width), MXU_DTYPE)] if nin > 1 else [])
    return _rowcall(name, fn, t, tm, seq // tm, tiled=[dab, *extra, dr, x], exs=[mod], res=[wt],
                    out_tiled=out_tiled, out_exacc=[_sds((b, 1, D)), _sds((b, 1, D))])


def _tn_matmul(name, a, b, bm, bn, bt=512):
    t, m = a.shape
    n = b.shape[1]

    def body(a_ref, b_ref, o_ref):
        @pl.when(pl.program_id(2) == 0)
        def _():
            o_ref[...] = jnp.zeros(o_ref.shape, F32)
        o_ref[...] += lax.dot_general(a_ref[...], b_ref[...], (((0,), (0,)), ((), ())), preferred_element_type=F32)

    return _pcall(body, name=name, grid=(m // bm, n // bn, t // bt),
                  in_specs=[pl.BlockSpec((bt, bm), lambda i, j, k: (k, i)), pl.BlockSpec((bt, bn), lambda i, j, k: (k, j))],
                  out_specs=pl.BlockSpec((bm, bn), lambda i, j, k: (i, j)), out_shape=_sds((m, n)),
                  compiler_params=_cparams(("parallel", "parallel", "arbitrary")))(a, b)


def _proj_fwd(name, x, mod, w_in, seq):
    t = x.shape[0]
    tm = 256

    def fn(ctx, tiled, halos, exs, res, outs, accs, exaccs, scr):
        mod_ref, = exs
        sh, sc = mod_ref[0, 3:4, :], mod_ref[0, 4:5, :]
        h = _mx(tiled[0][...] * (1.0 + sc) + sh)
        outs[0][...] = h
        outs[1][...] = _dot(h, res[0][...])

    return _rowcall(name, fn, t, tm, seq // tm, tiled=[x], exs=[mod], res=[w_in],
                    out_tiled=[_sds((t, D), MXU_DTYPE), _sds((t, PW))])


def _shift_rows(cur, prev8, j):
    if j == 0:
        return cur
    rolled = pltpu.roll(cur, j, 0)
    top = jnp.where(_iota((8, cur.shape[1]), 0) < j, pltpu.roll(prev8, j, 0), rolled[0:8])
    return jnp.concatenate([top, rolled[8:]], axis=0)


def _shift_rows_up(cur, next8, j):
    if j == 0:
        return cur
    n = cur.shape[0]
    rolled = pltpu.roll(cur, n - j, 0)
    bot = jnp.where(_iota((8, cur.shape[1]), 0) >= 8 - j, pltpu.roll(next8, 8 - j, 0), rolled[n - 8:n])
    return jnp.concatenate([rolled[:n - 8], bot], axis=0)


def _softplus(x):
    return jnp.maximum(x, 0.0) + jnp.log(1.0 + jnp.exp(-jnp.abs(x)))


def _ssd_common(proj_ref, xpre, dtb_ref, alog_ref):
    xbc = xpre * _sigmoid(xpre)
    xs, bm, cm = xbc[:, 0:SW], xbc[:, SW:SW + 256], xbc[:, SW + 256:SW + 512]
    dtraw = proj_ref[:, PW - 128:PW] + dtb_ref[...]
    dt = _softplus(dtraw)
    a = -jnp.exp(alog_ref[...])
    tril = (_iota((CH, CH), 0) >= _iota((CH, CH), 1)).astype(F32)
    acs = _dot_hi(tril, dt * a)
    return xs, bm, cm, dtraw, dt, a, acs, acs.T


def _pair_lane(lo, hi):
    r = lo.shape[0]
    return jnp.where(_iota((r, 128), 1) < HP, lo, hi)


def _ssd_fwd(name, proj, conv_w, conv_b, dt_bias, a_log, d_rep, norm_w, seq):
    t = proj.shape[0]

    def fn(ctx, tiled, halos, exs, res, outs, accs, exaccs, scr):
        proj_ref, = tiled
        halo_ref, = halos
        cw_ref, cb_ref, dtb_ref, alog_ref, d_ref, nw_ref = res
        yo_ref, xpre_ref, y_ref, sprev_ref = outs
        state_ref, = scr

        @pl.when(ctx.seq_first)
        def _():
            state_ref[...] = jnp.zeros(state_ref.shape, F32)

        raw = proj_ref[:, SW:SW + D]
        prev8 = halo_ref[:, SW:SW + D] * jnp.where(ctx.seq_first, 0.0, 1.0)
        xpre = cb_ref[...] + cw_ref[3:4, :] * raw
        for j in (1, 2, 3):
            xpre = xpre + cw_ref[3 - j:4 - j, :] * _shift_rows(raw, prev8, j)
        xpre_ref[...] = xpre
        xs, bm, cm, dtraw, dt, a, acs, acst = _ssd_common(proj_ref, xpre, dtb_ref, alog_ref)
        causal = _iota((CH, CH), 0) >= _iota((CH, CH), 1)
        lane_lo = _iota((CH, 128), 1) < HP
        sprev = state_ref[...]
        sprev_ref[...] = sprev
        ys = []
        for g in range(NG):
            bmg, cmg = bm[:, g * NS:(g + 1) * NS], cm[:, g * NS:(g + 1) * NS]
            bmt = bmg.T
            cb = _dot(_mx(cmg), _mx(bmt))
            for q in (2 * g, 2 * g + 1):
                xsq = xs[:, 128 * q:128 * q + 128]
                xd = xsq * _pair_lane(dt[:, 2 * q:2 * q + 1], dt[:, 2 * q + 1:2 * q + 2])
                sp = sprev[:, 128 * q:128 * q + 128]
                ydiag = jnp.zeros((CH, 128), F32)
                snew = jnp.zeros((NS, 128), F32)
                for jj in range(2):
                    h = 2 * q + jj
                    col, row = acs[:, h:h + 1], acst[h:h + 1, :]
                    lm = jnp.where(causal, jnp.exp(jnp.minimum(col - row, 0.0)), 0.0)
                    xm = _mx(jnp.where(lane_lo if jj == 0 else ~lane_lo, xd, 0.0))
                    ydiag = ydiag + _dot(_mx(cb * lm), xm)
                    dec_row = jnp.exp(acst[h:h + 1, CH - 1:CH] - row)
                    snew = snew + _dot(_mx(bmt * dec_row), xm)
                e_pair = jnp.exp(_pair_lane(acs[:, 2 * q:2 * q + 1], acs[:, 2 * q + 1:2 * q + 2]))
                yoff = _dot(_mx(cmg), _mx(sp)) * e_pair
                cd = jnp.exp(_pair_lane(acst[2 * q:2 * q + 1, CH - 1:CH], acst[2 * q + 1:2 * q + 2, CH - 1:CH]))
                state_ref[:, 128 * q:128 * q + 128] = cd * sp + snew
                ys.append(ydiag + yoff + d_ref[:, 128 * q:128 * q + 128] * xsq)
        y = jnp.concatenate(ys, axis=1)
        y_ref[...] = y
        z = proj_ref[:, 0:SW]
        yz = y * (z * _sigmoid(z))
        outp = []
        for g in range(NG):
            seg = yz[:, 256 * g:256 * g + 256]
            rinv = lax.rsqrt(jnp.mean(seg * seg, axis=-1, keepdims=True) + LN_EPS)
            outp.append(seg * rinv * nw_ref[:, 256 * g:256 * g + 256])
        yo_ref[...] = _mx(jnp.concatenate(outp, axis=1))

    return _rowcall(name, fn, t, CH, seq // CH, tiled=[proj], halos=[(proj, 8)],
                    res=[conv_w, conv_b, dt_bias, a_log, d_rep, norm_w],
                    out_tiled=[_sds((t, SW), MXU_DTYPE), _sds((t, D)), _sds((t, SW)), _sds((t, SW))],
                    scratch=[pltpu.VMEM((NS, SW), F32)])


def _ssd_bwd(name, dyo, proj, xpre_all, y_all, sprev_all, conv_w, dt_bias, a_log, d_rep, norm_w, seq):
    t = proj.shape[0]

    def fn(ctx, tiled, halos, exs, res, outs, accs, exaccs, scr):
        dyo_ref, proj_ref, xpre_ref, y_ref, sprev_ref = tiled
        halo_ref, = halos
        cw_ref, dtb_ref, alog_ref, d_ref, nw_ref = res
        dzx_ref, ddt_ref = outs
        dnw_acc, dd_acc, dcw_acc, dcb_acc, ddtb_acc, dalog_acc = accs
        ds_ref, nxt_ref = scr

        @pl.when(ctx.seq_last)
        def _():
            ds_ref[...] = jnp.zeros(ds_ref.shape, F32)
            nxt_ref[...] = jnp.zeros(nxt_ref.shape, F32)

        xpre = xpre_ref[...]
        xs, bm, cm, dtraw, dt, a, acs, acst = _ssd_common(proj_ref, xpre, dtb_ref, alog_ref)
        y = y_ref[...]
        z = proj_ref[:, 0:SW]
        sz = _sigmoid(z)
        siluz = z * sz
        yz = y * siluz
        dyo = dyo_ref[...]
        dyz_parts, dnw_parts = [], []
        for g in range(NG):
            seg = yz[:, 256 * g:256 * g + 256]
            rinv = lax.rsqrt(jnp.mean(seg * seg, axis=-1, keepdims=True) + LN_EPS)
            yn = seg * rinv
            dseg = dyo[:, 256 * g:256 * g + 256]
            dnw_parts.append(jnp.sum(dseg * yn, axis=0, keepdims=True))
            dyn = dseg * nw_ref[:, 256 * g:256 * g + 256]
            dyz_parts.append(rinv * (dyn - yn * jnp.mean(dyn * yn, axis=-1, keepdims=True)))
        dyz = jnp.concatenate(dyz_parts, axis=1)
        _acc(dnw_acc, jnp.concatenate(dnw_parts, axis=1), ctx.first)
        dy = dyz * siluz
        dz = dyz * y * (sz * (1.0 + z * (1.0 - sz)))
        _acc(dd_acc, jnp.sum(dy * xs, axis=0, keepdims=True), ctx.first)

        causal = _iota((CH, CH), 0) >= _iota((CH, CH), 1)
        anti = _iota((CH, CH), 0) <= _iota((CH, CH), 1)
        lane_lo = _iota((CH, 128), 1) < HP
        lane_id = _iota((CH, 128), 1)
        last_row = _iota((CH, 128), 0) == CH - 1
        sprev = sprev_ref[...]
        dacs = jnp.zeros((CH, 128), F32)
        ddt_x = jnp.zeros((CH, 128), F32)
        dxs_parts, dbm_parts, dcm_parts = [], [], []
        for g in range(NG):
            bmg, cmg = bm[:, g * NS:(g + 1) * NS], cm[:, g * NS:(g + 1) * NS]
            bmt, cmt = bmg.T, cmg.T
            cb = _dot(_mx(cmg), _mx(bmt))
            cbt = _dot(_mx(bmg), _mx(cmt))
            dcb = jnp.zeros((CH, CH), F32)
            dcbt = jnp.zeros((CH, CH), F32)
            dbmg = jnp.zeros((CH, NS), F32)
            dcmg = jnp.zeros((CH, NS), F32)
            for q in (2 * g, 2 * g + 1):
                sl = slice(128 * q, 128 * q + 128)
                xsq = xs[:, sl]
                dtp = _pair_lane(dt[:, 2 * q:2 * q + 1], dt[:, 2 * q + 1:2 * q + 2])
                xd = xsq * dtp
                dyq = dy[:, sl]
                sp = sprev[:, sl]
                dsn = ds_ref[:, sl]
                e_pair = jnp.exp(_pair_lane(acs[:, 2 * q:2 * q + 1], acs[:, 2 * q + 1:2 * q + 2]))
                cd = jnp.exp(_pair_lane(acst[2 * q:2 * q + 1, CH - 1:CH], acst[2 * q + 1:2 * q + 2, CH - 1:CH]))
                dye = dyq * e_pair
                dcmg = dcmg + _dot(_mx(dye), _mx(sp.T))
                dsp = _dot(_mx(cmt), _mx(dye)) + cd * dsn
                yoff = _dot(_mx(cmg), _mx(sp)) * e_pair
                dacs_lane = dyq * yoff
                dxd = jnp.zeros((CH, 128), F32)
                sds = jnp.sum(dsn * sp, axis=0, keepdims=True) * cd
                for jj in range(2):
                    h = 2 * q + jj
                    hm = lane_lo if jj == 0 else ~lane_lo
                    col, row = acs[:, h:h + 1], acst[h:h + 1, :]
                    lm = jnp.where(causal, jnp.exp(jnp.minimum(col - row, 0.0)), 0.0)
                    lmt = jnp.where(anti, jnp.exp(jnp.minimum(row - col, 0.0)), 0.0)
                    xm = _mx(jnp.where(hm, xd, 0.0))
                    dym = _mx(jnp.where(hm, dyq, 0.0))
                    gm = _dot_nt(dym, xm)
                    gmt = _dot_nt(xm, dym)
                    dcb = dcb + gm * lm
                    dcbt = dcbt + gmt * lmt
                    dxd = dxd + _dot(_mx(cbt * lmt), dym)
                    w = gm * cb * lm
                    wt = gmt * cbt * lmt
                    dacs_h = jnp.sum(w, axis=1, keepdims=True) - jnp.sum(wt, axis=1, keepdims=True)
                    alast = acst[h:h + 1, CH - 1:CH]
                    dec_col = jnp.exp(alast - col)
                    dsm = _mx(jnp.where(hm[0:NS], dsn, 0.0))
                    qh = _dot_nt(xm, dsm)
                    dbmg = dbmg + qh * dec_col
                    ddec = jnp.sum(qh * bmg, axis=1, keepdims=True)
                    dxd = dxd + _dot(_mx(bmg * dec_col), dsm)
                    dacs_h = dacs_h - ddec * dec_col
                    dacs_h = dacs_h + jnp.sum(jnp.where(hm, dacs_lane, 0.0), axis=1, keepdims=True)
                    tail = jnp.sum(ddec * dec_col, axis=0, keepdims=True) + jnp.sum(
                        jnp.where(hm[0:1], sds, 0.0), axis=1, keepdims=True)
                    dacs = dacs + jnp.where(lane_id == h, dacs_h, 0.0) + jnp.where(
                        last_row & (lane_id == h), tail, 0.0)
                ds_ref[:, sl] = dsp
                for jj in range(2):
                    h = 2 * q + jj
                    hm = lane_lo if jj == 0 else ~lane_lo
                    ddt_x = ddt_x + jnp.where(lane_id == h, jnp.sum(jnp.where(hm, dxd * xsq, 0.0), axis=1, keepdims=True), 0.0)
                dxs_parts.append(dxd * dtp + d_ref[:, sl] * dyq)
            dcmg = dcmg + _dot(_mx(dcb), _mx(bmg))
            dbmg = dbmg + _dot(_mx(dcbt), _mx(cmg))
            dbm_parts.append(dbmg)
            dcm_parts.append(dcmg)
        triu = (_iota((CH, CH), 0) <= _iota((CH, CH), 1)).astype(F32)
        dadt = _dot_hi(triu, dacs)
        ddt = dadt * a + ddt_x
        _acc(dalog_acc, jnp.sum(dadt * dt, axis=0, keepdims=True) * a, ctx.first)
        ddtraw = ddt * _sigmoid(dtraw)
        ddt_ref[...] = ddtraw
        _acc(ddtb_acc, jnp.sum(ddtraw, axis=0, keepdims=True), ctx.first)
        dxbc = jnp.concatenate(dxs_parts + dbm_parts + dcm_parts, axis=1)
        sx = _sigmoid(xpre)
        dpre = dxbc * (sx * (1.0 + xpre * (1.0 - sx)))
        _acc(dcb_acc, jnp.sum(dpre, axis=0, keepdims=True), ctx.first)
        raw = proj_ref[:, SW:SW + D]
        prev8 = halo_ref[:, SW:SW + D] * jnp.where(ctx.seq_first, 0.0, 1.0)
        next8 = nxt_ref[...]
        draw = cw_ref[3:4, :] * dpre
        dcw = [None] * 4
        dcw[3] = jnp.sum(dpre * raw, axis=0, keepdims=True)
        for j in (1, 2, 3):
            dcw[3 - j] = jnp.sum(dpre * _shift_rows(raw, prev8, j), axis=0, keepdims=True)
            draw = draw + cw_ref[3 - j:4 - j, :] * _shift_rows_up(dpre, next8, j)
        _acc(dcw_acc, jnp.concatenate(dcw + [jnp.zeros((4, D), F32)], axis=0), ctx.first)
        nxt_ref[...] = dpre[0:8]
        dzx_ref[:, 0:SW] = dz
        dzx_ref[:, SW:SW + D] = draw

    return _rowcall(name, fn, t, CH, seq // CH, tiled=[dyo, proj, xpre_all, y_all, sprev_all], halos=[(proj, 8)],
                    res=[conv_w, dt_bias, a_log, d_rep, norm_w],
                    out_tiled=[_sds((t, SW + D)), _sds((t, 128))],
                    out_acc=[_sds((1, SW)), _sds((1, SW)), _sds((8, D)), _sds((1, D)), _sds((1, 128)), _sds((1, 128))],
                    scratch=[pltpu.VMEM((NS, SW), F32), pltpu.VMEM((8, D), F32)], reverse=True)


def _gelu(y):
    k = math.sqrt(2.0 / math.pi)
    return 0.5 * y * (1.0 + jnp.tanh(k * (y + 0.044715 * y * y * y)))


def _gelu_grad(y):
    k = math.sqrt(2.0 / math.pi)
    th = jnp.tanh(k * (y + 0.044715 * y * y * y))
    return 0.5 * (1.0 + th) + 0.5 * y * (1.0 - th * th) * k * (1.0 + 3.0 * 0.044715 * y * y)


S5T = 256


def _cmul_add(xr, xi, ar, ai, sr, si):
    return xr + ar * sr - ai * si, xi + ar * si + ai * sr


def _s5_fwd(name, proj, bbd, cbd, pw, tab, d5, w_glu, b_glu, seq):
    t = proj.shape[0]
    tm = S5T

    def fn(ctx, tiled, halos, exs, res, outs, accs, exaccs, scr):
        proj_ref, = tiled
        bbd_ref, cbd_ref, pw_ref, tab_ref, d_ref, wg_ref, bg_ref = res
        out_ref, xst_ref, y_ref, xb_ref, ub_ref = outs
        carry_ref, = scr

        @pl.when(ctx.seq_first)
        def _():
            carry_ref[...] = jnp.zeros(carry_ref.shape, F32)

        u = proj_ref[:, 1536:2048]
        bu = _dot(_mx(u), bbd_ref[...])
        xr, xi = bu[:, :S5L], bu[:, S5L:]
        rowi = _iota((tm, S5L), 0) & 7
        for k, sh in enumerate((1, 2, 4)):
            keep = rowi >= sh
            sr = jnp.where(keep, pltpu.roll(xr, sh, 0), 0.0)
            si = jnp.where(keep, pltpu.roll(xi, sh, 0), 0.0)
            xr, xi = _cmul_add(xr, xi, pw_ref[k:k + 1, :S5L], pw_ref[k:k + 1, S5L:], sr, si)
        xst_ref[:, :S5L] = xr
        xst_ref[:, S5L:] = xi

        def tile_fix(i, c):
            cr, ci = c
            rows = pl.ds(pl.multiple_of(i * 8, 8), 8)
            tr, ti = _cmul_add(xst_ref[rows, :S5L], xst_ref[rows, S5L:], tab_ref[:, :S5L], tab_ref[:, S5L:], cr, ci)
            xst_ref[rows, :S5L] = tr
            xst_ref[rows, S5L:] = ti
            return tr[7:8], ti[7:8]

        cr, ci = lax.fori_loop(0, tm // 8, tile_fix, (carry_ref[0:1, :S5L], carry_ref[0:1, S5L:]))
        carry_ref[0:1, :S5L] = cr
        carry_ref[0:1, S5L:] = ci
        xb = _mx(xst_ref[...])
        xb_ref[...] = xb
        ub_ref[...] = _mx(u)
        y = _dot(xb, cbd_ref[...]) + u * d_ref[...]
        y_ref[...] = y
        g = _gelu(y)
        v = _dot(_mx(g), wg_ref[...]) + bg_ref[...]
        out_ref[...] = _mx(g * _sigmoid(v))

    return _rowcall(name, fn, t, tm, seq // tm, tiled=[proj], res=[bbd, cbd, pw, tab, d5, w_glu, b_glu],
                    out_tiled=[_sds((t, SW), MXU_DTYPE), _sds((t, 2 * S5L)), _sds((t, SW)),
                               _sds((t, 2 * S5L), MXU_DTYPE), _sds((t, SW), MXU_DTYPE)],
                    scratch=[pltpu.VMEM((8, 2 * S5L), F32)])


def _s5_bwd(name, dout, proj, xst, y_all, bbdt, cbdt, pwc, tabc, d5, w_glu, w_glut, b_glu, seq):
    t = proj.shape[0]
    tm = S5T

    def fn(ctx, tiled, halos, exs, res, outs, accs, exaccs, scr):
        dout_ref, proj_ref, xst_ref, y_ref = tiled
        halo_ref, = halos
        bbdt_ref, cbdt_ref, pw_ref, tab_ref, d_ref, wg_ref, wgt_ref, bg_ref = res
        du_ref, lam_ref, dyb_ref, gb_ref, dvb_ref = outs
        da_acc, dd_acc, dbg_acc = accs
        carry_ref, lamf_ref = scr

        @pl.when(ctx.seq_last)
        def _():
            carry_ref[...] = jnp.zeros(carry_ref.shape, F32)

        u = proj_ref[:, 1536:2048]
        y = y_ref[...]
        g = _gelu(y)
        v = _dot(_mx(g), wg_ref[...]) + bg_ref[...]
        sg = _sigmoid(v)
        dout = dout_ref[...]
        dv = dout * g * sg * (1.0 - sg)
        dvb = _mx(dv)
        dvb_ref[...] = dvb
        gb_ref[...] = _mx(g)
        _acc(dbg_acc, jnp.sum(dv, axis=0, keepdims=True), ctx.first)
        dg = dout * sg + _dot(dvb, wgt_ref[...])
        dy = dg * _gelu_grad(y)
        dyb = _mx(dy)
        dyb_ref[...] = dyb
        _acc(dd_acc, jnp.sum(dy * u, axis=0, keepdims=True), ctx.first)
        dx = _dot(dyb, cbdt_ref[...])
        xr, xi = dx[:, :S5L], dx[:, S5L:]
        rowi = _iota((tm, S5L), 0) & 7
        for k, sh in enumerate((1, 2, 4)):
            keep = rowi < 8 - sh
            sr = jnp.where(keep, pltpu.roll(xr, tm - sh, 0), 0.0)
            si = jnp.where(keep, pltpu.roll(xi, tm - sh, 0), 0.0)
            xr, xi = _cmul_add(xr, xi, pw_ref[k:k + 1, :S5L], pw_ref[k:k + 1, S5L:], sr, si)
        lamf_ref[:, :S5L] = xr
        lamf_ref[:, S5L:] = xi

        def tile_fix(i, c):
            cr, ci = c
            rows = pl.ds(pl.multiple_of((tm // 8 - 1 - i) * 8, 8), 8)
            tr, ti = _cmul_add(lamf_ref[rows, :S5L], lamf_ref[rows, S5L:], tab_ref[:, :S5L], tab_ref[:, S5L:], cr, ci)
            lamf_ref[rows, :S5L] = tr
            lamf_ref[rows, S5L:] = ti
            return tr[0:1], ti[0:1]

        cr, ci = lax.fori_loop(0, tm // 8, tile_fix, (carry_ref[0:1, :S5L], carry_ref[0:1, S5L:]))
        carry_ref[0:1, :S5L] = cr
        carry_ref[0:1, S5L:] = ci
        lam = lamf_ref[...]
        lamb = _mx(lam)
        lam_ref[...] = lamb
        du_ref[...] = dy * d_ref[...] + _dot(lamb, bbdt_ref[...])
        prev8 = halo_ref[...] * jnp.where(ctx.seq_first, 0.0, 1.0)
        xprev = _shift_rows(xst_ref[...], prev8, 1)
        lr, li = lam[:, :S5L], lam[:, S5L:]
        pr, pi = xprev[:, :S5L], xprev[:, S5L:]
        dar = jnp.sum(lr * pr + li * pi, axis=0, keepdims=True)
        dai = jnp.sum(li * pr - lr * pi, axis=0, keepdims=True)
        _acc(da_acc, jnp.concatenate([dar, dai], axis=1), ctx.first)

    return _rowcall(name, fn, t, tm, seq // tm, tiled=[dout, proj, xst, y_all], halos=[(xst, 8)],
                    res=[bbdt, cbdt, pwc, tabc, d5, w_glu, w_glut, b_glu],
                    out_tiled=[_sds((t, SW)), _sds((t, 2 * S5L), MXU_DTYPE), _sds((t, SW), MXU_DTYPE),
                               _sds((t, SW), MXU_DTYPE), _sds((t, SW), MXU_DTYPE)],
                    out_acc=[_sds((1, 2 * S5L)), _sds((1, SW)), _sds((1, SW))],
                    scratch=[pltpu.VMEM((8, 2 * S5L), F32), pltpu.VMEM((tm, 2 * S5L), F32)], reverse=True)


def _out_fwd(name, yssd, ys5, x1, mod, w_out, lng, lnb, seq):
    t = x1.shape[0]
    tm = 256

    def fn(ctx, tiled, halos, exs, res, outs, accs, exaccs, scr):
        ya_ref, yb_ref, x_ref = tiled
        mod_ref, = exs
        w_ref, g_ref, b_ref = res
        m = _dot(ya_ref[...], w_ref[0:SW, :]) + _dot(yb_ref[...], w_ref[SW:2 * SW, :])
        r = ALPHA * x_ref[...] + mod_ref[0, 5:6, :] * m
        outs[0][...] = _ln_fwd(r, g_ref[...], b_ref[...])
        outs[1][...] = r
        outs[2][...] = m

    return _rowcall(name, fn, t, tm, seq // tm, tiled=[yssd, ys5, x1], exs=[mod], res=[w_out, lng, lnb],
                    out_tiled=[_sds((t, D)), _sds((t, D)), _sds((t, D))])


def _out_bwd(name, dxo, r, m, mod, lng, w_outt, seq):
    t = dxo.shape[0]
    tm = 256

    def fn(ctx, tiled, halos, exs, res, outs, accs, exaccs, scr):
        dxo_ref, r_ref, m_ref = tiled
        mod_ref, = exs
        g_ref, wt_ref = res
        dr, dgam, dbet = _ln_bwd(r_ref[...], g_ref[...], dxo_ref[...])
        outs[0][...] = dr
        _acc(accs[0], dgam, ctx.first)
        _acc(accs[1], dbet, ctx.first)
        _acc(exaccs[0].at[0], jnp.sum(dr * m_ref[...], axis=0, keepdims=True), ctx.ex_enter)
        dm = _mx(mod_ref[0, 5:6, :] * dr)
        outs[1][...] = dm
        dyc = _dot(dm, wt_ref[...])
        outs[2][...] = dyc[:, 0:SW]
        outs[3][...] = dyc[:, SW:2 * SW]

    b = mod.shape[0]
    return _rowcall(name, fn, t, tm, seq // tm, tiled=[dxo, r, m], exs=[mod], res=[lng, w_outt],
                    out_tiled=[_sds((t, D)), _sds((t, D), MXU_DTYPE), _sds((t, SW)), _sds((t, SW))],
                    out_acc=[_sds((1, D)), _sds((1, D))], out_exacc=[_sds((b, 1, D))])


def _s5_discretise(a_re, a_im, log_dt, b_re, b_im):
    dt = jnp.exp(log_dt)[:, None]
    mag = jnp.exp(dt * a_re)
    ab_re, ab_im = mag * jnp.cos(dt * a_im), mag * jnp.sin(dt * a_im)
    den = a_re * a_re + a_im * a_im
    nr, ni = ab_re - 1.0, ab_im
    f_re, f_im = (nr * a_re + ni * a_im) / den, (ni * a_re - nr * a_im) / den
    bb_re = f_re[..., None] * b_re - f_im[..., None] * b_im
    bb_im = f_re[..., None] * b_im + f_im[..., None] * b_re
    return ab_re, ab_im, bb_re, bb_im


def _s5_tables(ab_re, ab_im):
    ar, ai = ab_re.reshape(1, S5L), ab_im.reshape(1, S5L)
    pows = [(ar, ai)]
    for _ in range(7):
        pr, pi = pows[-1]
        pows.append((pr * ar - pi * ai, pr * ai + pi * ar))
    z = jnp.zeros((1, S5L), F32)

    def pack(rows, sign):
        return jnp.concatenate([jnp.concatenate([r for r, _ in rows], axis=0),
                                jnp.concatenate([sign * i for _, i in rows], axis=0)], axis=1)

    sel = [pows[0], pows[1], pows[3]] + [(z, z)] * 5
    pw, pwc = pack(sel, 1.0), pack(sel, -1.0)
    tab = pack(pows, 1.0)
    tabc = pack(pows[::-1], -1.0)
    return pw, tab, pwc, tabc


def _local_step(x, tgt, mod, w, sp, seq):
    t = x.shape[0]
    mxu = MXU_DTYPE
    x1, r1, h1, ab1, f1 = _ffn_fwd("ffn1_fwd", x, mod, 0, w["ffn1_w13"], w["ffn1_w2"], sp["ln1_g"], sp["ln1_b"], seq)
    h2, proj = _proj_fwd("proj_fwd", x1, mod, w["w_in"], seq)
    yssd, xpre, yraw, sprev = _ssd_fwd("ssd_fwd", proj, sp["conv_w"], sp["conv_b"], sp["dt_bias"], sp["a_log"],
                                       sp["d_rep"], sp["ssd_norm_w"], seq)
    (ab_re, ab_im, bb_re, bb_im), disc_vjp = jax.vjp(_s5_discretise, sp["s5_a_re"], sp["s5_a_im"], sp["s5_log_dt"],
                                                     sp["s5_b_re"], sp["s5_b_im"])
    eye = jnp.eye(S5G, dtype=F32)
    bbd = jnp.concatenate([jnp.einsum("gk,gph->ghkp", eye, bb_re).reshape(SW, S5L),
                           jnp.einsum("gk,gph->ghkp", eye, bb_im).reshape(SW, S5L)], axis=1).astype(mxu)
    cbd = jnp.concatenate([jnp.einsum("gk,ghp->gpkh", eye, sp["s5_c_re"]).reshape(S5L, SW),
                           -jnp.einsum("gk,ghp->gpkh", eye, sp["s5_c_im"]).reshape(S5L, SW)], axis=0).astype(mxu)
    pw, tab, pwc, tabc = _s5_tables(lax.stop_gradient(ab_re), lax.stop_gradient(ab_im))
    ys5, xst, y5, xstb, ub = _s5_fwd("s5_fwd", proj, bbd, cbd, pw, tab, sp["s5_d"], w["w_glu"], sp["b_glu"], seq)
    x2, r2, m2 = _out_fwd("out_fwd", yssd, ys5, x1, mod, w["w_out"], sp["ln2_g"], sp["ln2_b"], seq)
    x3, r3, h3, ab3, f3, dy3, loss_acc = _ffn_fwd("ffn2_fwd", x2, mod, 6, w["ffn2_w13"], w["ffn2_w2"], sp["ln3_g"],
                                                  sp["ln3_b"], seq, tgt=tgt)
    dr3, df3, s3, dab3, dg3g, dg3b, dgate3 = _ffn_bwd1("ffn2_bwd1", dy3, r3, ab3, f3, mod, 6, sp["ln3_g"],
                                                        w["ffn2_w2t"], seq)
    dx2, dsh3, dsc3 = _mod_bwd("ffn2_bwd2", dab3, dr3, x2, mod, 6, w["ffn2_w13t"], seq)
    g_ffn2_w13 = _tn_matmul("ffn2_dw13", h3, dab3, D, FB)
    g_ffn2_w2 = _tn_matmul("ffn2_dw2", s3, df3, FB, D)
    dr2, dm2, dyssd, dys5, dg2g, dg2b, dgate2 = _out_bwd("out_bwd", dx2, r2, m2, mod, sp["ln2_g"], w["w_outt"], seq)
    g_w_out = jnp.concatenate([_tn_matmul("dw_out_a", yssd, dm2, SW, D), _tn_matmul("dw_out_b", ys5, dm2, SW, D)], axis=0)
    du, lam, dy5b, g5b, dv5b, da5, dd5, dbglu = _s5_bwd("s5_bwd", dys5, proj, xst, y5, bbd.T, cbd.T, pwc, tabc,
                                                       sp["s5_d"], w["w_glu"], w["w_glut"], sp["b_glu"], seq)
    g_w_glu = _tn_matmul("dw_glu", g5b, dv5b, SW, SW)
    dbfull = _tn_matmul("s5_db", lam, ub, D, SW)
    dcfull = _tn_matmul("s5_dc", xstb, dy5b, D, SW)
    dzx, ddt, dnw, ddl, dcw, dcb, ddtb, dalog = _ssd_bwd("ssd_bwd", dyssd, proj, xpre, yraw, sprev, sp["conv_w"],
                                                         sp["dt_bias"], sp["a_log"], sp["d_rep"], sp["ssd_norm_w"], seq)
    dx1, dproj, dsh2, dsc2 = _mod_bwd("proj_bwd", dzx, dr2, x1, mod, 3, w["w_int"], seq, extra=(du, ddt))
    g_w_in = _tn_matmul("dw_in", h2, dproj, D, PW)
    dr1, df1, s1, dab1, dg1g, dg1b, dgate1 = _ffn_bwd1("ffn1_bwd1", dx1, r1, ab1, f1, mod, 0, sp["ln1_g"],
                                                        w["ffn1_w2t"], seq)
    dx0, dsh1, dsc1 = _mod_bwd("ffn1_bwd2", dab1, dr1, x, mod, 0, w["ffn1_w13t"], seq)
    g_ffn1_w13 = _tn_matmul("ffn1_dw13", h1, dab1, D, FB)
    g_ffn1_w2 = _tn_matmul("ffn1_dw2", s1, df1, FB, D)
    dmod = jnp.concatenate([dsh1, dsc1, dgate1, dsh2, dsc2, dgate2, dsh3, dsc3, dgate3], axis=1)
    d4 = lambda a: a.reshape(S5G, S5P, S5G, S5H)
    dbb_re = jnp.einsum("gpgh->gph", d4(dbfull[:S5L]))
    dbb_im = jnp.einsum("gpgh->gph", d4(dbfull[S5L:]))
    dc_re = jnp.einsum("gpgh->ghp", d4(dcfull[:S5L]))
    dc_im = -jnp.einsum("gpgh->ghp", d4(dcfull[S5L:]))
    g_a_re, g_a_im, g_log_dt, g_b_re, g_b_im = disc_vjp(
        (da5[:, :S5L].reshape(S5G, S5P), da5[:, S5L:].reshape(S5G, S5P), dbb_re, dbb_im))
    small = dict(ln1_g=dg1g, ln1_b=dg1b, ln2_g=dg2g, ln2_b=dg2b, ln3_g=dg3g, ln3_b=dg3b, conv_w=dcw[0:4], conv_b=dcb,
                 dt_bias=ddtb[:, :NH], a_log=dalog[:, :NH], d_ssd=jnp.sum(ddl.reshape(NH, HP), axis=1).reshape(1, NH),
                 ssd_norm_w=dnw, s5_a_re=g_a_re, s5_a_im=g_a_im, s5_log_dt=g_log_dt, s5_b_re=g_b_re, s5_b_im=g_b_im,
                 s5_c_re=dc_re, s5_c_im=dc_im, s5_d=dd5, w_glu_b=dbglu)
    big = dict(ffn1_w13=g_ffn1_w13, ffn1_w2=g_ffn1_w2, w_in=g_w_in, w_glu=g_w_glu, w_out=g_w_out,
               ffn2_w13=g_ffn2_w13, ffn2_w2=g_ffn2_w2)
    return loss_acc[0, 0], dx0, dmod, big, small


def _place():
    return lax.axis_index("x"), lax.axis_index("y"), lax.axis_index("c")


def _other_chips(x, y):
    return [(1 - x, y), (x, 1 - y), (1 - x, 1 - y)]


def _allgather8(name, a):
    r, n = a.shape

    def body(x_ref, out_ref, send_sems, recv_sems, local_sem):
        x, y, c = _place()
        me, sibling = (x, y, c), (x, y, 1 - c)
        chips = _other_chips(x, y)

        def rows(px, py, pc):
            return out_ref.at[pl.ds(pl.multiple_of((4 * px + 2 * py + pc) * r, 8), r), :]

        def copy(k, block, to, src=None):
            return pltpu.make_async_remote_copy(src_ref=rows(*block) if src is None else src, dst_ref=rows(*block),
                                                send_sem=send_sems.at[k], recv_sem=recv_sems.at[k], device_id=to,
                                                device_id_type=MESH_T)

        mine = pltpu.make_async_copy(x_ref, rows(*me), local_sem)
        mine.start()
        first = [copy(0, me, sibling, src=x_ref)]
        first += [copy(1 + j, me, (*chip, c), src=x_ref) for j, chip in enumerate(chips)]
        for cp in first:
            cp.start()
        passed = [copy(4 + j, (*chip, c), sibling) for j, chip in enumerate(chips)]
        for j, chip in enumerate(chips):
            copy(1 + j, (*chip, c), me).wait_recv()
            passed[j].start()
        copy(0, sibling, me).wait_recv()
        for j, chip in enumerate(chips):
            copy(4 + j, (*chip, 1 - c), me).wait_recv()
        for cp in first + passed:
            cp.wait_send()
        mine.wait()

    out = _pcall(body, name=name, out_shape=_sds((8 * r, n), a.dtype),
                 in_specs=[pl.BlockSpec(memory_space=pltpu.VMEM)], out_specs=pl.BlockSpec(memory_space=pltpu.VMEM),
                 scratch_shapes=[pltpu.SemaphoreType.DMA((7,)), pltpu.SemaphoreType.DMA((7,)), pltpu.SemaphoreType.DMA],
                 compiler_params=_cparams())(a)
    return out.reshape(8, r, n)


def _hbm_call(body, name, ins, out_shape, n_sems):
    any_spec = pl.BlockSpec(memory_space=pl.ANY)
    scratch = [pltpu.SemaphoreType.DMA((n,)) for n in n_sems]
    return _pcall(body, name=name, out_shape=out_shape, in_specs=[any_spec] * len(ins), out_specs=[any_spec] * len(out_shape),
                  scratch_shapes=scratch, compiler_params=_cparams())(*ins)


def _rows(ref_rows, half, align):
    hr = ref_rows // 2
    return pl.ds(pl.multiple_of(half * hr, align), hr)


def _gather_weights(name, shards):
    n = len(shards)

    def body(*refs):
        s_refs, o_refs = refs[:n], refs[n:2 * n]
        send_sems, recv_sems, local_sems = refs[2 * n:]
        x, y, c = _place()
        sibling = (x, y, 1 - c)
        chips = _other_chips(x, y)

        def copy(i, k, chip, half, to, src=None):
            dst = o_refs[i].at[2 * chip[0] + chip[1], _rows(shards[i].shape[0], half, 16)]
            return pltpu.make_async_remote_copy(src_ref=dst if src is None else src, dst_ref=dst,
                                                send_sem=send_sems.at[6 * i + k], recv_sem=recv_sems.at[6 * i + k],
                                                device_id=to, device_id_type=MESH_T)

        local = [pltpu.make_async_copy(s_refs[i], o_refs[i].at[2 * x + y], local_sems.at[i]) for i in range(n)]
        for cp in local:
            cp.start()
        first = [copy(i, j, (x, y), c, (*chip, c), src=s_refs[i].at[_rows(shards[i].shape[0], c, 16)])
                 for i in range(n) for j, chip in enumerate(chips)]
        for cp in first:
            cp.start()
        passed = []
        for i in range(n):
            for j, chip in enumerate(chips):
                copy(i, j, chip, c, sibling).wait_recv()
                passed.append(copy(i, 3 + j, chip, c, sibling))
                passed[-1].start()
        for i in range(n):
            for j, chip in enumerate(chips):
                copy(i, 3 + j, chip, 1 - c, sibling).wait_recv()
        for cp in first + passed:
            cp.wait_send()
        for cp in local:
            cp.wait()

    return _hbm_call(body, name, shards, [_sds((4,) + s.shape, s.dtype) for s in shards], (6 * n, 6 * n, n))


def _rs_sibling(name, gs):
    n = len(gs)

    def body(*refs):
        g_refs, o_refs = refs[:n], refs[n:2 * n]
        send_sems, recv_sems = refs[2 * n:]
        x, y, c = _place()
        cps = [pltpu.make_async_remote_copy(src_ref=g_refs[i].at[:, _rows(gs[i].shape[1], 1 - c, 8)], dst_ref=o_refs[i],
                                            send_sem=send_sems.at[i], recv_sem=recv_sems.at[i], device_id=(x, y, 1 - c),
                                            device_id_type=MESH_T) for i in range(n)]
        for cp in cps:
            cp.start()
        for cp in cps:
            cp.wait()

    return _hbm_call(body, name, gs, [_sds((4, g.shape[1] // 2, g.shape[2]), g.dtype) for g in gs], (n, n))


def _rs_chips(name, hs):
    n = len(hs)

    def body(*refs):
        h_refs, o_refs = refs[:n], refs[n:2 * n]
        send_sems, recv_sems = refs[2 * n:]
        x, y, c = _place()
        cps = [pltpu.make_async_remote_copy(src_ref=h_refs[i].at[2 * chip[0] + chip[1]], dst_ref=o_refs[i].at[j],
                                            send_sem=send_sems.at[3 * i + j], recv_sem=recv_sems.at[3 * i + j],
                                            device_id=(*chip, c), device_id_type=MESH_T)
               for i in range(n) for j, chip in enumerate(_other_chips(x, y))]
        for cp in cps:
            cp.start()
        for cp in cps:
            cp.wait()

    return _hbm_call(body, name, hs, [_sds((3,) + h.shape[1:], h.dtype) for h in hs], (3 * n, 3 * n))


def _rs_join(name, fs):
    n = len(fs)

    def body(*refs):
        f_refs, o_refs = refs[:n], refs[n:2 * n]
        send_sems, recv_sems, local_sems = refs[2 * n:]
        x, y, c = _place()

        def part(i, half):
            return o_refs[i].at[_rows(2 * fs[i].shape[0], half, 8)]

        local = [pltpu.make_async_copy(f_refs[i], part(i, c), local_sems.at[i]) for i in range(n)]
        for cp in local:
            cp.start()

        def copy(i, half):
            return pltpu.make_async_remote_copy(src_ref=f_refs[i], dst_ref=part(i, half), send_sem=send_sems.at[i],
                                                recv_sem=recv_sems.at[i], device_id=(x, y, 1 - c), device_id_type=MESH_T)

        sends = [copy(i, c) for i in range(n)]
        for cp in sends:
            cp.start()
        for i in range(n):
            copy(i, 1 - c).wait_recv()
        for cp in sends:
            cp.wait_send()
        for cp in local:
            cp.wait()

    return _hbm_call(body, name, fs, [_sds((2 * f.shape[0], f.shape[1]), f.dtype) for f in fs], (n, n, n))


def _row_block(r, cap=2048):
    b = min(r, cap)
    while r % b or b % 8:
        b -= 8
    return b


RS_SPLIT = 2


def _rs_add(name, gs, r1s, sel):
    n = len(gs)

    def body(sel_ref, *refs):
        g_refs, r_refs, h_refs, b_refs = (refs[k * n:(k + 1) * n] for k in range(4))
        for i in range(n):
            h = g_refs[i][...] + r_refs[i][...]
            h_refs[i][...] = h
            b_refs[i][...] = h.astype(BF16)

    def blk(g):
        return (1, g.shape[1] // 2 // RS_SPLIT, g.shape[2])

    here = lambda k, j, s: (k, j, 0)
    in_specs = [pl.BlockSpec(blk(g), lambda k, j, s: (k, s[0] * RS_SPLIT + j, 0)) for g in gs]
    in_specs += [pl.BlockSpec(blk(g), here) for g in gs]
    outs = _pcall(body, name=name,
                  out_shape=[_sds((4, g.shape[1] // 2, g.shape[2])) for g in gs]
                  + [_sds((4, g.shape[1] // 2, g.shape[2]), BF16) for g in gs],
                  grid_spec=pltpu.PrefetchScalarGridSpec(num_scalar_prefetch=1, grid=(4, RS_SPLIT), in_specs=in_specs,
                                                         out_specs=[pl.BlockSpec(blk(g), here) for g in gs] * 2),
                  compiler_params=_cparams(("parallel", "parallel")))(sel.reshape(1).astype(jnp.int32), *gs, *r1s)
    return outs[:n], outs[n:]


def _rs_sum(name, hs, r2s, sel):
    n = len(hs)

    def body(sel_ref, *refs):
        h_refs, r_refs, o_refs = (refs[k * n:(k + 1) * n] for k in range(3))
        for i in range(n):
            r = r_refs[i]
            o_refs[i][...] = ((h_refs[i][0] + r[0].astype(F32)) + r[1].astype(F32)) + r[2].astype(F32)

    def rows(h):
        return h.shape[1] // RS_SPLIT

    in_specs = [pl.BlockSpec((1, rows(h), h.shape[2]), lambda j, s: (s[0], j, 0)) for h in hs]
    in_specs += [pl.BlockSpec((3, rows(h), h.shape[2]), lambda j, s: (0, j, 0)) for h in hs]
    return _pcall(body, name=name, out_shape=[_sds(h.shape[1:]) for h in hs],
                  grid_spec=pltpu.PrefetchScalarGridSpec(
                      num_scalar_prefetch=1, grid=(RS_SPLIT,), in_specs=in_specs,
                      out_specs=[pl.BlockSpec((rows(h), h.shape[2]), lambda j, s: (j, 0)) for h in hs]),
                  compiler_params=_cparams(("parallel",)))(sel.reshape(1).astype(jnp.int32), *hs, *r2s)


def _sum8(name, a):
    _, r, n = a.shape
    br = _row_block(r)

    def body(a_ref, o_ref):
        acc = a_ref[0]
        for k in range(1, 8):
            acc = acc + a_ref[k]
        o_ref[...] = acc

    return _pcall(body, name=name, out_shape=_sds((r, n)), grid=(r // br,),
                  in_specs=[pl.BlockSpec((8, br, n), lambda j: (0, j, 0))], out_specs=pl.BlockSpec((br, n), lambda j: (j, 0)),
                  compiler_params=_cparams(("parallel",)))(a)


def _adamw(name, ws, gs, ms, vs, nblk):
    n = len(ws)

    def body(*refs):
        w_refs, g_refs, m_refs, v_refs, d_refs, nm_refs, nv_refs = (refs[k * n:(k + 1) * n] for k in range(7))
        for i in range(n):
            gv = g_refs[i][...]
            nm = ADAM_B1 * m_refs[i][...] + (1.0 - ADAM_B1) * gv
            nv = ADAM_B2 * v_refs[i][...] + (1.0 - ADAM_B2) * (gv * gv)
            nm_refs[i][...] = nm
            nv_refs[i][...] = nv
            m_hat = nm / (1.0 - ADAM_B1 ** ADAM_STEP)
            v_hat = nv / (1.0 - ADAM_B2 ** ADAM_STEP)
            d_refs[i][...] = -ADAM_LR * (m_hat / (jnp.sqrt(v_hat) + ADAM_EPS) + ADAM_WD * w_refs[i][...])

    specs = [pl.BlockSpec((w.shape[0] // nblk, w.shape[1]), lambda j: (j, 0)) for w in ws]
    outs = _pcall(body, name=name, out_shape=[_sds(w.shape) for w in ws] * 3, grid=(nblk,), in_specs=specs * 4,
                  out_specs=specs * 3, compiler_params=_cparams(("parallel",)))(*ws, *gs, *ms, *vs)
    return outs[:n], outs[n:2 * n], outs[2 * n:]


ADA_COLS = 2304
ADA_BLK = 768


def _ada_fwd(name, c_all, w_shard, b_cols):
    nb = c_all.shape[0]

    def body(c_ref, w_ref, b_ref, o_ref):
        cv = c_ref[...]
        cs = _mx(cv * _sigmoid(cv))
        o_ref[...] = _dot(cs, _mx(w_ref[...])) + b_ref[...]

    return _pcall(body, name=name, out_shape=_sds((nb, ADA_COLS)), grid=(ADA_COLS // ADA_BLK,),
                  in_specs=[pl.BlockSpec((nb, D), lambda j: (0, 0)), pl.BlockSpec((D, ADA_BLK), lambda j: (0, j)),
                            pl.BlockSpec((1, ADA_BLK), lambda j: (0, j))],
                  out_specs=pl.BlockSpec((nb, ADA_BLK), lambda j: (0, j)),
                  compiler_params=_cparams(("parallel",)))(c_all, w_shard, b_cols)


def _ada_bwd(name, c_all, dmod_cols, dmod_all):
    nb = c_all.shape[0]

    def body(c_ref, dc_ref, da_ref, gw_ref, gb_ref):
        cv = c_ref[...]
        cs = _mx(cv * _sigmoid(cv))
        gw_ref[...] = lax.dot_general(cs, _mx(dc_ref[...]), (((0,), (0,)), ((), ())), preferred_element_type=F32)

        @pl.when(pl.program_id(0) == 0)
        def _():
            gb_ref[...] = jnp.sum(da_ref[...], axis=0, keepdims=True)

    return _pcall(body, name=name, out_shape=[_sds((D, ADA_COLS)), _sds((1, 9 * D))], grid=(ADA_COLS // ADA_BLK,),
                  in_specs=[pl.BlockSpec((nb, D), lambda j: (0, 0)), pl.BlockSpec((nb, ADA_BLK), lambda j: (0, j)),
                            pl.BlockSpec((nb, 9 * D), lambda j: (0, 0))],
                  out_specs=[pl.BlockSpec((D, ADA_BLK), lambda j: (0, j)), pl.BlockSpec((1, 9 * D), lambda j: (0, 0))],
                  compiler_params=_cparams(("arbitrary",)))(c_all, dmod_cols, dmod_all)


BIG = ("ffn1_w1", "ffn1_w3", "ffn1_w2", "w_in", "w_glu", "w_out", "ffn2_w1", "ffn2_w3", "ffn2_w2")
COL_SHARDED = ("ffn1_w1", "ffn1_w3", "w_in", "ffn2_w1", "ffn2_w3")
SMALL = ("b_ada", "ln1_g", "ln1_b", "conv_w", "conv_b", "dt_bias", "a_log", "d_ssd", "ssd_norm_w", "s5_a_re", "s5_a_im",
         "s5_log_dt", "s5_b_re", "s5_b_im", "s5_c_re", "s5_c_im", "s5_d", "b_glu", "ln2_g", "ln2_b", "ln3_g", "ln3_b")
WEIGHTS = ("w_ada", "b_ada", "ffn1_w1", "ffn1_w3", "ffn1_w2", "ln1_g", "ln1_b", "w_in", "conv_w", "conv_b", "dt_bias",
           "a_log", "d_ssd", "ssd_norm_w", "s5_a_re", "s5_a_im", "s5_log_dt", "s5_b_re", "s5_b_im", "s5_c_re", "s5_c_im",
           "s5_d", "w_glu", "b_glu", "w_out", "ln2_g", "ln2_b", "ffn2_w1", "ffn2_w3", "ffn2_w2", "ln3_g", "ln3_b")
BIG_PAD = 2 * 1024 * 128


def _pack(arrs, mult, axis_keep=0):
    lead = arrs[0].shape[:axis_keep]
    flat = jnp.concatenate([a.reshape(lead + (-1,)) for a in arrs], axis=axis_keep)
    pad = (-flat.shape[-1]) % mult
    if pad:
        flat = jnp.concatenate([flat, jnp.zeros(lead + (pad,), flat.dtype)], axis=axis_keep)
    return flat


def _unpack(flat, shapes):
    out, off = [], 0
    for s in shapes:
        size = math.prod(s)
        out.append(flat[..., off:off + size].reshape(flat.shape[:-1] + tuple(s)))
        off += size
    return out


def _shard_major(a):
    rows, cols = a.shape
    return a.reshape(rows, 4, cols // 4).transpose(1, 0, 2)


def _from_shard_major(a):
    _, rows, w = a.shape
    return a.transpose(1, 0, 2).reshape(rows, 4 * w)


def kernel(x, c, w_ada, b_ada, ffn1_w1, ffn1_w3, ffn1_w2, ln1_g, ln1_b, w_in, conv_w, conv_b, dt_bias, a_log, d_ssd, ssd_norm_w, s5_a_re, s5_a_im, s5_log_dt, s5_b_re, s5_b_im, s5_c_re, s5_c_im, s5_d, w_glu, b_glu, w_out, ln2_g, ln2_b, ffn2_w1, ffn2_w3, ffn2_w2, ln3_g, ln3_b, loss_target, m_w_ada, m_b_ada, m_ffn1_w1, m_ffn1_w3, m_ffn1_w2, m_ln1_g, m_ln1_b, m_w_in, m_conv_w, m_conv_b, m_dt_bias, m_a_log, m_d_ssd, m_ssd_norm_w, m_s5_a_re, m_s5_a_im, m_s5_log_dt, m_s5_b_re, m_s5_b_im, m_s5_c_re, m_s5_c_im, m_s5_d, m_w_glu, m_b_glu, m_w_out, m_ln2_g, m_ln2_b, m_ffn2_w1, m_ffn2_w3, m_ffn2_w2, m_ln3_g, m_ln3_b, v_w_ada, v_b_ada, v_ffn1_w1, v_ffn1_w3, v_ffn1_w2, v_ln1_g, v_ln1_b, v_w_in, v_conv_w, v_conv_b, v_dt_bias, v_a_log, v_d_ssd, v_ssd_norm_w, v_s5_a_re, v_s5_a_im, v_s5_log_dt, v_s5_b_re, v_s5_b_im, v_s5_c_re, v_s5_c_im, v_s5_d, v_w_glu, v_b_glu, v_w_out, v_ln2_g, v_ln2_b, v_ffn2_w1, v_ffn2_w3, v_ffn2_w2, v_ln3_g, v_ln3_b):
    a = dict(locals())
    xi, yi, ci = _place()
    chip = 2 * xi + yi
    dev = 2 * chip + ci
    nb, seq, _ = x.shape
    t = nb * seq
    ndev = 8

    c_all = _allgather8("gather_c", c.reshape(-1, 128)).reshape(ndev * nb, D)
    b_cols = lax.dynamic_slice(b_ada, (0, chip * ADA_COLS), (1, ADA_COLS))
    mod_part = _ada_fwd("ada_fwd", c_all, w_ada[0], b_cols)
    mod_parts = _allgather8("gather_mod", mod_part.reshape(-1, 128)).reshape(ndev, ndev * nb, ADA_COLS)
    mod_all = mod_parts[0::2].transpose(1, 0, 2).reshape(ndev * nb, 9 * D)
    mod = lax.dynamic_slice(mod_all, (nb * dev, 0), (nb, 9 * D)).reshape(nb, 9, D)

    gathered = _gather_weights("gather_w", [a[n][0].astype(MXU_DTYPE) for n in BIG])
    full = {}
    for n, piece in zip(BIG, gathered):
        full[n] = _from_shard_major(piece) if n in COL_SHARDED else piece.reshape(-1, piece.shape[-1])
    wi = full["w_in"]
    w_inp = jnp.concatenate([wi[:, :1536], wi[:, 1544:2056], wi[:, 1536:1544], jnp.zeros((D, 120), wi.dtype)], axis=1)
    w = dict(ffn1_w13=jnp.concatenate([full["ffn1_w1"], full["ffn1_w3"]], axis=1), ffn1_w2=full["ffn1_w2"],
             ffn2_w13=jnp.concatenate([full["ffn2_w1"], full["ffn2_w3"]], axis=1), ffn2_w2=full["ffn2_w2"],
             w_in=w_inp, w_glu=full["w_glu"], w_out=full["w_out"])
    for n in ("ffn1_w13", "ffn1_w2", "ffn2_w13", "ffn2_w2", "w_in", "w_out"):
        w[n + "t"] = w[n].T
    w["w_glut"] = w["w_glu"].T

    cw_all = _allgather8("gather_conv_w", _pack([conv_w[0]], 1024).reshape(-1, 128)).reshape(ndev, -1)[0::2, :4 * 256]
    conv_full = _from_shard_major(cw_all.reshape(4, 4, 256))
    pad8 = lambda v: jnp.concatenate([v.reshape(1, NH), jnp.zeros((1, 128 - NH), F32)], axis=1)
    sp = dict(ln1_g=ln1_g, ln1_b=ln1_b, ln2_g=ln2_g, ln2_b=ln2_b, ln3_g=ln3_g, ln3_b=ln3_b, conv_w=conv_full,
              conv_b=conv_b, dt_bias=pad8(dt_bias), a_log=pad8(a_log), d_rep=jnp.repeat(d_ssd[0], HP)[None],
              ssd_norm_w=ssd_norm_w, s5_a_re=s5_a_re[0], s5_a_im=s5_a_im[0], s5_log_dt=s5_log_dt[0], s5_b_re=s5_b_re[0],
              s5_b_im=s5_b_im[0], s5_c_re=s5_c_re[0], s5_c_im=s5_c_im[0], s5_d=s5_d, b_glu=b_glu)

    lsum, dx0, dmod, big, small = _local_step(x.reshape(t, D), loss_target.reshape(t, D), mod, w, sp, seq)
    loss = lax.psum(lsum * (0.5 / D), ("x", "y", "c"))

    dmod_all = _allgather8("gather_dmod", dmod.reshape(-1, 128)).reshape(ndev * nb, 9 * D)
    dmod_cols = lax.dynamic_slice(dmod_all, (0, chip * ADA_COLS), (ndev * nb, ADA_COLS))
    g_w_ada, g_b_ada = _ada_bwd("ada_bwd", c_all, dmod_cols, dmod_all)

    gwi = big["w_in"]
    gfull = dict(ffn1_w1=big["ffn1_w13"][:, :FF], ffn1_w3=big["ffn1_w13"][:, FF:], ffn1_w2=big["ffn1_w2"],
                 w_in=jnp.concatenate([gwi[:, :1536], gwi[:, 2048:2056], gwi[:, 1536:2048]], axis=1),
                 w_glu=big["w_glu"], w_out=big["w_out"],
                 ffn2_w1=big["ffn2_w13"][:, :FF], ffn2_w3=big["ffn2_w13"][:, FF:], ffn2_w2=big["ffn2_w2"])
    gsh = [_shard_major(gfull[n]) if n in COL_SHARDED else gfull[n].reshape((4,) + a[n].shape[1:]) for n in BIG]
    r1 = _rs_sibling("rs_sibling", gsh)
    hsum, hsum_bf = _rs_add("rs_add", gsh, r1, ci)
    r2 = _rs_chips("rs_chips", hsum_bf)
    fhalf = _rs_sum("rs_sum", hsum, r2, chip)
    gbig = dict(zip(BIG, _rs_join("rs_join", fhalf)))
    gbig["w_ada"] = g_w_ada

    outs = {}
    for call, names in (("adamw_a", ("ffn1_w1", "ffn1_w3", "ffn1_w2", "w_in", "w_glu", "w_out")),
                        ("adamw_b", ("ffn2_w1", "ffn2_w3", "ffn2_w2", "w_ada"))):
        res = _adamw(call, [a[n][0] for n in names], [gbig[n] for n in names], [a["m_" + n][0] for n in names],
                     [a["v_" + n][0] for n in names], 8)
        for kind, arrs in zip(("grad", "delta", "new_m", "new_v"), ([gbig[n] for n in names],) + tuple(res)):
            for n, arr in zip(names, arrs):
                outs[kind, n] = arr[None]

    snames = [n for n in SMALL if n != "b_ada"]
    sgrad = dict(small)
    sgrad["b_glu"] = small["w_glu_b"]
    svec = _pack([sgrad[n] for n in snames], 1024).reshape(-1, 128)
    ssum = _sum8("small_sum", _allgather8("gather_small", svec)).reshape(-1)

    def view2d(u):
        s = u.shape[1:]
        return u.reshape((1, s[0]) if len(s) == 1 else (-1, s[-1]))

    vshape = {n: view2d(a[n]).shape for n in SMALL}
    gsm = dict(zip(snames, _unpack(ssum, [vshape[n] if n != "conv_w" else (4, D) for n in snames])))
    gsm["conv_w"] = lax.dynamic_slice(gsm["conv_w"], (0, chip * 256), (4, 256))
    gsm["b_ada"] = g_b_ada
    res = _adamw("adamw_small", [view2d(a[n]) for n in SMALL], [gsm[n] for n in SMALL],
                 [view2d(a["m_" + n]) for n in SMALL], [view2d(a["v_" + n]) for n in SMALL], 1)
    for kind, arrs in zip(("grad", "delta", "new_m", "new_v"), ([gsm[n] for n in SMALL],) + tuple(res)):
        for n, arr in zip(SMALL, arrs):
            outs[kind, n] = arr.reshape(a[n].shape)

    res = [loss, dx0.reshape(nb, seq, D)]
    for kind in ("grad", "delta", "new_m", "new_v"):
        res += [outs[kind, n] for n in WEIGHTS]
    return tuple(res)
```

```python
import functools
import math

import jax
import jax.numpy as jnp
from jax import lax
from jax.experimental import pallas as pl
from jax.experimental.pallas import tpu as pltpu

F32 = jnp.float32
BF16 = jnp.bfloat16
MXU_DTYPE = jnp.bfloat16

D = 1024
FF = 2816
FB = 1408
NH, HP, NS, NG = 8, 64, 128, 2
CH = 128
SW = 512
S5G, S5P, S5H = 32, 64, 16
S5L = S5G * S5P
PW = 2176
ALPHA = 2.0 ** 0.25
LN_EPS = 1e-5
ADAM_LR, ADAM_B1, ADAM_B2, ADAM_EPS, ADAM_WD, ADAM_STEP = 0.001, 0.9, 0.999, 1e-08, 0.01, 10
VMEM_LIMIT = 56 * 1024 * 1024
MESH_T = pl.DeviceIdType.MESH


def _pcall(body, **kw):
    return pl.pallas_call(body, **kw)


def _cparams(sem=None, **kw):
    return pltpu.CompilerParams(dimension_semantics=sem, vmem_limit_bytes=VMEM_LIMIT, **kw)


def _dot(a, b):
    return jnp.dot(a, b, preferred_element_type=F32)


def _dot_nt(a, b):
    return lax.dot_general(a, b, (((1,), (1,)), ((), ())), preferred_element_type=F32)


def _dot_hi(a, b):
    return jnp.dot(a, b, preferred_element_type=F32, precision=lax.Precision.HIGHEST)


def _mx(a):
    return a.astype(MXU_DTYPE)


def _sigmoid(x):
    return 1.0 / (1.0 + jnp.exp(-x))


def _iota(shape, axis):
    return lax.broadcasted_iota(jnp.int32, shape, axis)


def _rowcall(name, fn, n_rows, tm, tpe, *, tiled=(), halos=(), exs=(), res=(), out_tiled=(), out_acc=(),
             out_exacc=(), scratch=(), reverse=False):
    nt = n_rows // tm

    def blk(i):
        return (nt - 1 - i) if reverse else i

    in_specs, args = [], []
    for a in tiled:
        in_specs.append(pl.BlockSpec((tm, a.shape[1]), lambda i: (blk(i), 0)))
        args.append(a)
    for a, rows in halos:
        r = tm // rows
        in_specs.append(pl.BlockSpec((rows, a.shape[1]), lambda i, r=r: (jnp.maximum(blk(i) * r - 1, 0), 0)))
        args.append(a)
    for a in exs:
        in_specs.append(pl.BlockSpec((1,) + a.shape[1:], lambda i: (blk(i) // tpe, 0, 0)))
        args.append(a)
    for a in res:
        nd = a.ndim
        in_specs.append(pl.BlockSpec(a.shape, lambda i, nd=nd: (0,) * nd, pipeline_mode=pl.Buffered(1)))
        args.append(a)
    out_specs, out_shape = [], []
    for s in out_tiled:
        out_specs.append(pl.BlockSpec((tm, s.shape[1]), lambda i: (blk(i), 0)))
        out_shape.append(s)
    for s in out_acc:
        nd = len(s.shape)
        out_specs.append(pl.BlockSpec(s.shape, lambda i, nd=nd: (0,) * nd))
        out_shape.append(s)
    for s in out_exacc:
        out_specs.append(pl.BlockSpec((1,) + s.shape[1:], lambda i: (blk(i) // tpe, 0, 0)))
        out_shape.append(s)
    n = [len(tiled), len(halos), len(exs), len(res), len(out_tiled), len(out_acc), len(out_exacc), len(scratch)]

    def body(*refs):
        groups, k = [], 0
        for m in n:
            groups.append(refs[k:k + m])
            k += m
        i = pl.program_id(0)
        b = blk(i)

        class ctx:
            first = i == 0
            pos = b % tpe
            seq_first = (b % tpe) == 0
            seq_last = (b % tpe) == tpe - 1
            ex_enter = (i % tpe) == 0

        fn(ctx, *groups)

    return _pcall(body, name=name, grid=(nt,), in_specs=in_specs, out_specs=out_specs, out_shape=out_shape,
                  scratch_shapes=list(scratch), compiler_params=_cparams(("arbitrary",)))(*args)


def _acc(ref, val, first):
    @pl.when(first)
    def _():
        ref[...] = jnp.zeros(ref.shape, ref.dtype)
    ref[...] += val


def _sds(shape, dtype=F32):
    return jax.ShapeDtypeStruct(shape, dtype)


def _ln_fwd(r, g, b):
    mu = jnp.mean(r, axis=-1, keepdims=True)
    rc = r - mu
    var = jnp.mean(rc * rc, axis=-1, keepdims=True)
    return rc * lax.rsqrt(var + LN_EPS) * g + b


def _ln_bwd(r, g, dy):
    mu = jnp.mean(r, axis=-1, keepdims=True)
    rc = r - mu
    var = jnp.mean(rc * rc, axis=-1, keepdims=True)
    rstd = lax.rsqrt(var + LN_EPS)
    xhat = rc * rstd
    dxh = dy * g
    dr = rstd * (dxh - jnp.mean(dxh, axis=-1, keepdims=True) - xhat * jnp.mean(dxh * xhat, axis=-1, keepdims=True))
    return dr, jnp.sum(dy * xhat, axis=0, keepdims=True), jnp.sum(dy, axis=0, keepdims=True)


def _ffn_fwd(name, x, mod, k0, w13, w2, lng, lnb, seq, tgt=None):
    t = x.shape[0]
    tm = 256
    with_loss = tgt is not None

    def fn(ctx, tiled, halos, exs, res, outs, accs, exaccs, scr):
        x_ref = tiled[0]
        mod_ref, = exs
        w13_ref, w2_ref, g_ref, b_ref = res
        xo_ref, r_ref, h_ref, ab_ref, f_ref = outs[:5]
        xv = x_ref[...]
        sh, sc, g = mod_ref[0, k0:k0 + 1, :], mod_ref[0, k0 + 1:k0 + 2, :], mod_ref[0, k0 + 2:k0 + 3, :]
        h = _mx(xv * (1.0 + sc) + sh)
        h_ref[...] = h
        f = jnp.zeros((tm, D), F32)
        for j in range(2):
            a = _dot(h, w13_ref[:, j * FB:(j + 1) * FB])
            b = _dot(h, w13_ref[:, FF + j * FB:FF + (j + 1) * FB])
            ab_ref[:, j * FB:(j + 1) * FB] = a
            ab_ref[:, FF + j * FB:FF + (j + 1) * FB] = b
            s = a * _sigmoid(a) * b
            f = f + _dot(_mx(s), w2_ref[j * FB:(j + 1) * FB, :])
        f_ref[...] = f
        r = ALPHA * xv + 0.5 * g * f
        r_ref[...] = r
        xo = _ln_fwd(r, g_ref[...], b_ref[...])
        xo_ref[...] = xo
        if with_loss:
            e = xo - tiled[1][...]
            outs[5][...] = e * (1.0 / D)
            _acc(accs[0], jnp.sum(e * e) * jnp.ones((8, 128), F32), ctx.first)

    out_tiled = [_sds((t, D)), _sds((t, D)), _sds((t, D), MXU_DTYPE), _sds((t, 2 * FF)), _sds((t, D))]
    if with_loss:
        out_tiled.append(_sds((t, D)))
    return _rowcall(name, fn, t, tm, seq // tm, tiled=[x] + ([tgt] if with_loss else []), exs=[mod],
                    res=[w13, w2, lng, lnb], out_tiled=out_tiled, out_acc=[_sds((8, 128))] if with_loss else [])


def _ffn_bwd1(name, dxo, r, ab, f, mod, k0, lng, w2t, seq):
    t = dxo.shape[0]
    tm = 256

    def fn(ctx, tiled, halos, exs, res, outs, accs, exaccs, scr):
        dxo_ref, r_ref, ab_ref, f_ref = tiled
        mod_ref, = exs
        g_ref, w2t_ref = res
        dr_ref, df_ref, s_ref, dab_ref = outs
        g = mod_ref[0, k0 + 2:k0 + 3, :]
        dr, dgam, dbet = _ln_bwd(r_ref[...], g_ref[...], dxo_ref[...])
        dr_ref[...] = dr
        _acc(accs[0], dgam, ctx.first)
        _acc(accs[1], dbet, ctx.first)
        _acc(exaccs[0].at[0], jnp.sum(0.5 * f_ref[...] * dr, axis=0, keepdims=True), ctx.ex_enter)
        df = _mx(0.5 * g * dr)
        df_ref[...] = df
        for j in range(2):
            ds = _dot(df, w2t_ref[:, j * FB:(j + 1) * FB])
            a = ab_ref[:, j * FB:(j + 1) * FB]
            b = ab_ref[:, FF + j * FB:FF + (j + 1) * FB]
            sig = _sigmoid(a)
            silu = a * sig
            s_ref[:, j * FB:(j + 1) * FB] = _mx(silu * b)
            dab_ref[:, j * FB:(j + 1) * FB] = _mx(ds * b * (sig * (1.0 + a * (1.0 - sig))))
            dab_ref[:, FF + j * FB:FF + (j + 1) * FB] = _mx(ds * silu)

    b = mod.shape[0]
    return _rowcall(name, fn, t, tm, seq // tm, tiled=[dxo, r, ab, f], exs=[mod], res=[lng, w2t],
                    out_tiled=[_sds((t, D)), _sds((t, D), MXU_DTYPE), _sds((t, FF), MXU_DTYPE),
                               _sds((t, 2 * FF), MXU_DTYPE)],
                    out_acc=[_sds((1, D)), _sds((1, D))], out_exacc=[_sds((b, 1, D))])


def _mod_bwd(name, dab, dr, x, mod, k0, wt, seq, extra=()):
    t = dr.shape[0]
    tm = 256
    nin = 1 + len(extra)
    width = dab.shape[1] + sum(e.shape[1] for e in extra)

    def fn(ctx, tiled, halos, exs, res, outs, accs, exaccs, scr):
        parts = tiled[:nin]
        dr_ref, x_ref = tiled[nin:]
        mod_ref, = exs
        wt_ref, = res
        sc = mod_ref[0, k0 + 1:k0 + 2, :]
        if nin == 1:
            dp = parts[0][...]
        else:
            dp = jnp.concatenate([_mx(p[...]) for p in parts], axis=1)
            outs[1][...] = dp
        dh = _dot(dp, wt_ref[...])
        outs[0][...] = ALPHA * dr_ref[...] + dh * (1.0 + sc)
        _acc(exaccs[0].at[0], jnp.sum(dh, axis=0, keepdims=True), ctx.ex_enter)
        _acc(exaccs[1].at[0], jnp.sum(dh * x_ref[...], axis=0, keepdims=True), ctx.ex_enter)

    b = mod.shape[0]
    out_tiled = [_sds((t, D))] + ([_sds((t, width), MXU_DTYPE)] if nin > 1 else [])
    return _rowcall(name, fn, t, tm, seq // tm, tiled=[dab, *extra, dr, x], exs=[mod], res=[wt],
                    out_tiled=out_tiled, out_exacc=[_sds((b, 1, D)), _sds((b, 1, D))])


def _tn_matmul(name, a, b, bm, bn, bt=512):
    t, m = a.shape
    n = b.shape[1]

    def body(a_ref, b_ref, o_ref):
        @pl.when(pl.program_id(2) == 0)
        def _():
            o_ref[...] = jnp.zeros(o_ref.shape, F32)
        o_ref[...] += lax.dot_general(a_ref[...], b_ref[...], (((0,), (0,)), ((), ())), preferred_element_type=F32)

    return _pcall(body, name=name, grid=(m // bm, n // bn, t // bt),
                  in_specs=[pl.BlockSpec((bt, bm), lambda i, j, k: (k, i)), pl.BlockSpec((bt, bn), lambda i, j, k: (k, j))],
                  out_specs=pl.BlockSpec((bm, bn), lambda i, j, k: (i, j)), out_shape=_sds((m, n)),
                  compiler_params=_cparams(("parallel", "parallel", "arbitrary")))(a, b)


def _proj_fwd(name, x, mod, w_in, seq):
    t = x.shape[0]
    tm = 256

    def fn(ctx, tiled, halos, exs, res, outs, accs, exaccs, scr):
        mod_ref, = exs
        sh, sc = mod_ref[0, 3:4, :], mod_ref[0, 4:5, :]
        h = _mx(tiled[0][...] * (1.0 + sc) + sh)
        outs[0][...] = h
        outs[1][...] = _dot(h, res[0][...])

    return _rowcall(name, fn, t, tm, seq // tm, tiled=[x], exs=[mod], res=[w_in],
                    out_tiled=[_sds((t, D), MXU_DTYPE), _sds((t, PW))])


def _shift_rows(cur, prev8, j):
    if j == 0:
        return cur
    rolled = pltpu.roll(cur, j, 0)
    top = jnp.where(_iota((8, cur.shape[1]), 0) < j, pltpu.roll(prev8, j, 0), rolled[0:8])
    return jnp.concatenate([top, rolled[8:]], axis=0)


def _shift_rows_up(cur, next8, j):
    if j == 0:
        return cur
    n = cur.shape[0]
    rolled = pltpu.roll(cur, n - j, 0)
    bot = jnp.where(_iota((8, cur.shape[1]), 0) >= 8 - j, pltpu.roll(next8, 8 - j, 0), rolled[n - 8:n])
    return jnp.concatenate([rolled[:n - 8], bot], axis=0)


def _softplus(x):
    return jnp.maximum(x, 0.0) + jnp.log(1.0 + jnp.exp(-jnp.abs(x)))


def _ssd_common(proj_ref, xpre, dtb_ref, alog_ref):
    xbc = xpre * _sigmoid(xpre)
    xs, bm, cm = xbc[:, 0:SW], xbc[:, SW:SW + 256], xbc[:, SW + 256:SW + 512]
    dtraw = proj_ref[:, PW - 128:PW] + dtb_ref[...]
    dt = _softplus(dtraw)
    a = -jnp.exp(alog_ref[...])
    tril = (_iota((CH, CH), 0) >= _iota((CH, CH), 1)).astype(F32)
    acs = _dot_hi(tril, dt * a)
    return xs, bm, cm, dtraw, dt, a, acs, acs.T


def _pair_lane(lo, hi):
    r = lo.shape[0]
    return jnp.where(_iota((r, 128), 1) < HP, lo, hi)


def _ssd_fwd(name, proj, conv_w, conv_b, dt_bias, a_log, d_rep, norm_w, seq):
    t = proj.shape[0]

    def fn(ctx, tiled, halos, exs, res, outs, accs, exaccs, scr):
        proj_ref, = tiled
        halo_ref, = halos
        cw_ref, cb_ref, dtb_ref, alog_ref, d_ref, nw_ref = res
        yo_ref, xpre_ref, y_ref, sprev_ref = outs
        state_ref, = scr

        @pl.when(ctx.seq_first)
        def _():
            state_ref[...] = jnp.zeros(state_ref.shape, F32)

        raw = proj_ref[:, SW:SW + D]
        prev8 = halo_ref[:, SW:SW + D] * jnp.where(ctx.seq_first, 0.0, 1.0)
        xpre = cb_ref[...] + cw_ref[3:4, :] * raw
        for j in (1, 2, 3):
            xpre = xpre + cw_ref[3 - j:4 - j, :] * _shift_rows(raw, prev8, j)
        xpre_ref[...] = xpre
        xs, bm, cm, dtraw, dt, a, acs, acst = _ssd_common(proj_ref, xpre, dtb_ref, alog_ref)
        causal = _iota((CH, CH), 0) >= _iota((CH, CH), 1)
        lane_lo = _iota((CH, 128), 1) < HP
        sprev = state_ref[...]
        sprev_ref[...] = sprev
        ys = []
        for g in range(NG):
            bmg, cmg = bm[:, g * NS:(g + 1) * NS], cm[:, g * NS:(g + 1) * NS]
            bmt = bmg.T
            cb = _dot(_mx(cmg), _mx(bmt))
            for q in (2 * g, 2 * g + 1):
                xsq = xs[:, 128 * q:128 * q + 128]
                xd = xsq * _pair_lane(dt[:, 2 * q:2 * q + 1], dt[:, 2 * q + 1:2 * q + 2])
                sp = sprev[:, 128 * q:128 * q + 128]
                ydiag = jnp.zeros((CH, 128), F32)
                snew = jnp.zeros((NS, 128), F32)
                for jj in range(2):
                    h = 2 * q + jj
                    col, row = acs[:, h:h + 1], acst[h:h + 1, :]
                    lm = jnp.where(causal, jnp.exp(jnp.minimum(col - row, 0.0)), 0.0)
                    xm = _mx(jnp.where(lane_lo if jj == 0 else ~lane_lo, xd, 0.0))
                    ydiag = ydiag + _dot(_mx(cb * lm), xm)
                    dec_row = jnp.exp(acst[h:h + 1, CH - 1:CH] - row)
                    snew = snew + _dot(_mx(bmt * dec_row), xm)
                e_pair = jnp.exp(_pair_lane(acs[:, 2 * q:2 * q + 1], acs[:, 2 * q + 1:2 * q + 2]))
                yoff = _dot(_mx(cmg), _mx(sp)) * e_pair
                cd = jnp.exp(_pair_lane(acst[2 * q:2 * q + 1, CH - 1:CH], acst[2 * q + 1:2 * q + 2, CH - 1:CH]))
                state_ref[:, 128 * q:128 * q + 128] = cd * sp + snew
                ys.append(ydiag + yoff + d_ref[:, 128 * q:128 * q + 128] * xsq)
        y = jnp.concatenate(ys, axis=1)
        y_ref[...] = y
        z = proj_ref[:, 0:SW]
        yz = y * (z * _sigmoid(z))
        outp = []
        for g in range(NG):
            seg = yz[:, 256 * g:256 * g + 256]
            rinv = lax.rsqrt(jnp.mean(seg * seg, axis=-1, keepdims=True) + LN_EPS)
            outp.append(seg * rinv * nw_ref[:, 256 * g:256 * g + 256])
        yo_ref[...] = _mx(jnp.concatenate(outp, axis=1))

    return _rowcall(name, fn, t, CH, seq // CH, tiled=[proj], halos=[(proj, 8)],
                    res=[conv_w, conv_b, dt_bias, a_log, d_rep, norm_w],
                    out_tiled=[_sds((t, SW), MXU_DTYPE), _sds((t, D)), _sds((t, SW)), _sds((t, SW))],
                    scratch=[pltpu.VMEM((NS, SW), F32)])


def _ssd_bwd(name, dyo, proj, xpre_all, y_all, sprev_all, conv_w, dt_bias, a_log, d_rep, norm_w, seq):
    t = proj.shape[0]

    def fn(ctx, tiled, halos, exs, res, outs, accs, exaccs, scr):
        dyo_ref, proj_ref, xpre_ref, y_ref, sprev_ref = tiled
        halo_ref, = halos
        cw_ref, dtb_ref, alog_ref, d_ref, nw_ref = res
        dzx_ref, ddt_ref = outs
        dnw_acc, dd_acc, dcw_acc, dcb_acc, ddtb_acc, dalog_acc = accs
        ds_ref, nxt_ref = scr

        @pl.when(ctx.seq_last)
        def _():
            ds_ref[...] = jnp.zeros(ds_ref.shape, F32)
            nxt_ref[...] = jnp.zeros(nxt_ref.shape, F32)

        xpre = xpre_ref[...]
        xs, bm, cm, dtraw, dt, a, acs, acst = _ssd_common(proj_ref, xpre, dtb_ref, alog_ref)
        y = y_ref[...]
        z = proj_ref[:, 0:SW]
        sz = _sigmoid(z)
        siluz = z * sz
        yz = y * siluz
        dyo = dyo_ref[...]
        dyz_parts, dnw_parts = [], []
        for g in range(NG):
            seg = yz[:, 256 * g:256 * g + 256]
            rinv = lax.rsqrt(jnp.mean(seg * seg, axis=-1, keepdims=True) + LN_EPS)
            yn = seg * rinv
            dseg = dyo[:, 256 * g:256 * g + 256]
            dnw_parts.append(jnp.sum(dseg * yn, axis=0, keepdims=True))
            dyn = dseg * nw_ref[:, 256 * g:256 * g + 256]
            dyz_parts.append(rinv * (dyn - yn * jnp.mean(dyn * yn, axis=-1, keepdims=True)))
        dyz = jnp.concatenate(dyz_parts, axis=1)
        _acc(dnw_acc, jnp.concatenate(dnw_parts, axis=1), ctx.first)
        dy = dyz * siluz
        dz = dyz * y * (sz * (1.0 + z * (1.0 - sz)))
        _acc(dd_acc, jnp.sum(dy * xs, axis=0, keepdims=True), ctx.first)

        causal = _iota((CH, CH), 0) >= _iota((CH, CH), 1)
        anti = _iota((CH, CH), 0) <= _iota((CH, CH), 1)
        lane_lo = _iota((CH, 128), 1) < HP
        lane_id = _iota((CH, 128), 1)
        last_row = _iota((CH, 128), 0) == CH - 1
        sprev = sprev_ref[...]
        dacs = jnp.zeros((CH, 128), F32)
        ddt_x = jnp.zeros((CH, 128), F32)
        dxs_parts, dbm_parts, dcm_parts = [], [], []
        for g in range(NG):
            bmg, cmg = bm[:, g * NS:(g + 1) * NS], cm[:, g * NS:(g + 1) * NS]
            bmt, cmt = bmg.T, cmg.T
            cb = _dot(_mx(cmg), _mx(bmt))
            cbt = _dot(_mx(bmg), _mx(cmt))
            dcb = jnp.zeros((CH, CH), F32)
            dcbt = jnp.zeros((CH, CH), F32)
            dbmg = jnp.zeros((CH, NS), F32)
            dcmg = jnp.zeros((CH, NS), F32)
            for q in (2 * g, 2 * g + 1):
                sl = slice(128 * q, 128 * q + 128)
                xsq = xs[:, sl]
                dtp = _pair_lane(dt[:, 2 * q:2 * q + 1], dt[:, 2 * q + 1:2 * q + 2])
                xd = xsq * dtp
                dyq = dy[:, sl]
                sp = sprev[:, sl]
                dsn = ds_ref[:, sl]
                e_pair = jnp.exp(_pair_lane(acs[:, 2 * q:2 * q + 1], acs[:, 2 * q + 1:2 * q + 2]))
                cd = jnp.exp(_pair_lane(acst[2 * q:2 * q + 1, CH - 1:CH], acst[2 * q + 1:2 * q + 2, CH - 1:CH]))
                dye = dyq * e_pair
                dcmg = dcmg + _dot(_mx(dye), _mx(sp.T))
                dsp = _dot(_mx(cmt), _mx(dye)) + cd * dsn
                yoff = _dot(_mx(cmg), _mx(sp)) * e_pair
                dacs_lane = dyq * yoff
                dxd = jnp.zeros((CH, 128), F32)
                sds = jnp.sum(dsn * sp, axis=0, keepdims=True) * cd
                for jj in range(2):
                    h = 2 * q + jj
                    hm = lane_lo if jj == 0 else ~lane_lo
                    col, row = acs[:, h:h + 1], acst[h:h + 1, :]
                    lm = jnp.where(causal, jnp.exp(jnp.minimum(col - row, 0.0)), 0.0)
                    lmt = jnp.where(anti, jnp.exp(jnp.minimum(row - col, 0.0)), 0.0)
                    xm = _mx(jnp.where(hm, xd, 0.0))
                    dym = _mx(jnp.where(hm, dyq, 0.0))
                    gm = _dot_nt(dym, xm)
                    gmt = _dot_nt(xm, dym)
                    dcb = dcb + gm * lm
                    dcbt = dcbt + gmt * lmt
                    dxd = dxd + _dot(_mx(cbt * lmt), dym)
                    w = gm * cb * lm
                    wt = gmt * cbt * lmt
                    dacs_h = jnp.sum(w, axis=1, keepdims=True) - jnp.sum(wt, axis=1, keepdims=True)
                    alast = acst[h:h + 1, CH - 1:CH]
                    dec_col = jnp.exp(alast - col)
                    dsm = _mx(jnp.where(hm[0:NS], dsn, 0.0))
                    qh = _dot_nt(xm, dsm)
                    dbmg = dbmg + qh * dec_col
                    ddec = jnp.sum(qh * bmg, axis=1, keepdims=True)
                    dxd = dxd + _dot(_mx(bmg * dec_col), dsm)
                    dacs_h = dacs_h - ddec * dec_col
                    dacs_h = dacs_h + jnp.sum(jnp.where(hm, dacs_lane, 0.0), axis=1, keepdims=True)
                    tail = jnp.sum(ddec * dec_col, axis=0, keepdims=True) + jnp.sum(
                        jnp.where(hm[0:1], sds, 0.0), axis=1, keepdims=True)
                    dacs = dacs + jnp.where(lane_id == h, dacs_h, 0.0) + jnp.where(
                        last_row & (lane_id == h), tail, 0.0)
                ds_ref[:, sl] = dsp
                for jj in range(2):
                    h = 2 * q + jj
                    hm = lane_lo if jj == 0 else ~lane_lo
                    ddt_x = ddt_x + jnp.where(lane_id == h, jnp.sum(jnp.where(hm, dxd * xsq, 0.0), axis=1, keepdims=True), 0.0)
                dxs_parts.append(dxd * dtp + d_ref[:, sl] * dyq)
            dcmg = dcmg + _dot(_mx(dcb), _mx(bmg))
            dbmg = dbmg + _dot(_mx(dcbt), _mx(cmg))
            dbm_parts.append(dbmg)
            dcm_parts.append(dcmg)
        triu = (_iota((CH, CH), 0) <= _iota((CH, CH), 1)).astype(F32)
        dadt = _dot_hi(triu, dacs)
        ddt = dadt * a + ddt_x
        _acc(dalog_acc, jnp.sum(dadt * dt, axis=0, keepdims=True) * a, ctx.first)
        ddtraw = ddt * _sigmoid(dtraw)
        ddt_ref[...] = ddtraw
        _acc(ddtb_acc, jnp.sum(ddtraw, axis=0, keepdims=True), ctx.first)
        dxbc = jnp.concatenate(dxs_parts + dbm_parts + dcm_parts, axis=1)
        sx = _sigmoid(xpre)
        dpre = dxbc * (sx * (1.0 + xpre * (1.0 - sx)))
        _acc(dcb_acc, jnp.sum(dpre, axis=0, keepdims=True), ctx.first)
        raw = proj_ref[:, SW:SW + D]
        prev8 = halo_ref[:, SW:SW + D] * jnp.where(ctx.seq_first, 0.0, 1.0)
        next8 = nxt_ref[...]
        draw = cw_ref[3:4, :] * dpre
        dcw = [None] * 4
        dcw[3] = jnp.sum(dpre * raw, axis=0, keepdims=True)
        for j in (1, 2, 3):
            dcw[3 - j] = jnp.sum(dpre * _shift_rows(raw, prev8, j), axis=0, keepdims=True)
            draw = draw + cw_ref[3 - j:4 - j, :] * _shift_rows_up(dpre, next8, j)
        _acc(dcw_acc, jnp.concatenate(dcw + [jnp.zeros((4, D), F32)], axis=0), ctx.first)
        nxt_ref[...] = dpre[0:8]
        dzx_ref[:, 0:SW] = dz
        dzx_ref[:, SW:SW + D] = draw

    return _rowcall(name, fn, t, CH, seq // CH, tiled=[dyo, proj, xpre_all, y_all, sprev_all], halos=[(proj, 8)],
                    res=[conv_w, dt_bias, a_log, d_rep, norm_w],
                    out_tiled=[_sds((t, SW + D)), _sds((t, 128))],
                    out_acc=[_sds((1, SW)), _sds((1, SW)), _sds((8, D)), _sds((1, D)), _sds((1, 128)), _sds((1, 128))],
                    scratch=[pltpu.VMEM((NS, SW), F32), pltpu.VMEM((8, D), F32)], reverse=True)


def _gelu(y):
    k = math.sqrt(2.0 / math.pi)
    return 0.5 * y * (1.0 + jnp.tanh(k * (y + 0.044715 * y * y * y)))


def _gelu_grad(y):
    k = math.sqrt(2.0 / math.pi)
    th = jnp.tanh(k * (y + 0.044715 * y * y * y))
    return 0.5 * (1.0 + th) + 0.5 * y * (1.0 - th * th) * k * (1.0 + 3.0 * 0.044715 * y * y)


S5T = 256


def _cmul_add(xr, xi, ar, ai, sr, si):
    return xr + ar * sr - ai * si, xi + ar * si + ai * sr


def _s5_fwd(name, proj, bbd, cbd, pw, tab, d5, w_glu, b_glu, seq):
    t = proj.shape[0]
    tm = S5T

    def fn(ctx, tiled, halos, exs, res, outs, accs, exaccs, scr):
        proj_ref, = tiled
        bbd_ref, cbd_ref, pw_ref, tab_ref, d_ref, wg_ref, bg_ref = res
        out_ref, xst_ref, y_ref, xb_ref, ub_ref = outs
        carry_ref, = scr

        @pl.when(ctx.seq_first)
        def _():
            carry_ref[...] = jnp.zeros(carry_ref.shape, F32)

        u = proj_ref[:, 1536:2048]
        bu = _dot(_mx(u), bbd_ref[...])
        xr, xi = bu[:, :S5L], bu[:, S5L:]
        rowi = _iota((tm, S5L), 0) & 7
        for k, sh in enumerate((1, 2, 4)):
            keep = rowi >= sh
            sr = jnp.where(keep, pltpu.roll(xr, sh, 0), 0.0)
            si = jnp.where(keep, pltpu.roll(xi, sh, 0), 0.0)
            xr, xi = _cmul_add(xr, xi, pw_ref[k:k + 1, :S5L], pw_ref[k:k + 1, S5L:], sr, si)
        xst_ref[:, :S5L] = xr
        xst_ref[:, S5L:] = xi

        def tile_fix(i, c):
            cr, ci = c
            rows = pl.ds(pl.multiple_of(i * 8, 8), 8)
            tr, ti = _cmul_add(xst_ref[rows, :S5L], xst_ref[rows, S5L:], tab_ref[:, :S5L], tab_ref[:, S5L:], cr, ci)
            xst_ref[rows, :S5L] = tr
            xst_ref[rows, S5L:] = ti
            return tr[7:8], ti[7:8]

        cr, ci = lax.fori_loop(0, tm // 8, tile_fix, (carry_ref[0:1, :S5L], carry_ref[0:1, S5L:]))
        carry_ref[0:1, :S5L] = cr
        carry_ref[0:1, S5L:] = ci
        xb = _mx(xst_ref[...])
        xb_ref[...] = xb
        ub_ref[...] = _mx(u)
        y = _dot(xb, cbd_ref[...]) + u * d_ref[...]
        y_ref[...] = y
        g = _gelu(y)
        v = _dot(_mx(g), wg_ref[...]) + bg_ref[...]
        out_ref[...] = _mx(g * _sigmoid(v))

    return _rowcall(name, fn, t, tm, seq // tm, tiled=[proj], res=[bbd, cbd, pw, tab, d5, w_glu, b_glu],
                    out_tiled=[_sds((t, SW), MXU_DTYPE), _sds((t, 2 * S5L)), _sds((t, SW)),
                               _sds((t, 2 * S5L), MXU_DTYPE), _sds((t, SW), MXU_DTYPE)],
                    scratch=[pltpu.VMEM((8, 2 * S5L), F32)])


def _s5_bwd(name, dout, proj, xst, y_all, bbdt, cbdt, pwc, tabc, d5, w_glu, w_glut, b_glu, seq):
    t = proj.shape[0]
    tm = S5T

    def fn(ctx, tiled, halos, exs, res, outs, accs, exaccs, scr):
        dout_ref, proj_ref, xst_ref, y_ref = tiled
        halo_ref, = halos
        bbdt_ref, cbdt_ref, pw_ref, tab_ref, d_ref, wg_ref, wgt_ref, bg_ref = res
        du_ref, lam_ref, dyb_ref, gb_ref, dvb_ref = outs
        da_acc, dd_acc, dbg_acc = accs
        carry_ref, lamf_ref = scr

        @pl.when(ctx.seq_last)
        def _():
            carry_ref[...] = jnp.zeros(carry_ref.shape, F32)

        u = proj_ref[:, 1536:2048]
        y = y_ref[...]
        g = _gelu(y)
        v = _dot(_mx(g), wg_ref[...]) + bg_ref[...]
        sg = _sigmoid(v)
        dout = dout_ref[...]
        dv = dout * g * sg * (1.0 - sg)
        dvb = _mx(dv)
        dvb_ref[...] = dvb
        gb_ref[...] = _mx(g)
        _acc(dbg_acc, jnp.sum(dv, axis=0, keepdims=True), ctx.first)
        dg = dout * sg + _dot(dvb, wgt_ref[...])
        dy = dg * _gelu_grad(y)
        dyb = _mx(dy)
        dyb_ref[...] = dyb
        _acc(dd_acc, jnp.sum(dy * u, axis=0, keepdims=True), ctx.first)
        dx = _dot(dyb, cbdt_ref[...])
        xr, xi = dx[:, :S5L], dx[:, S5L:]
        rowi = _iota((tm, S5L), 0) & 7
        for k, sh in enumerate((1, 2, 4)):
            keep = rowi < 8 - sh
            sr = jnp.where(keep, pltpu.roll(xr, tm - sh, 0), 0.0)
            si = jnp.where(keep, pltpu.roll(xi, tm - sh, 0), 0.0)
            xr, xi = _cmul_add(xr, xi, pw_ref[k:k + 1, :S5L], pw_ref[k:k + 1, S5L:], sr, si)
        lamf_ref[:, :S5L] = xr
        lamf_ref[:, S5L:] = xi

        def tile_fix(i, c):
            cr, ci = c
            rows = pl.ds(pl.multiple_of((tm // 8 - 1 - i) * 8, 8), 8)
            tr, ti = _cmul_add(lamf_ref[rows, :S5L], lamf_ref[rows, S5L:], tab_ref[:, :S5L], tab_ref[:, S5L:], cr, ci)
            lamf_ref[rows, :S5L] = tr
            lamf_ref[rows, S5L:] = ti
            return tr[0:1], ti[0:1]

        cr, ci = lax.fori_loop(0, tm // 8, tile_fix, (carry_ref[0:1, :S5L], carry_ref[0:1, S5L:]))
        carry_ref[0:1, :S5L] = cr
        carry_ref[0:1, S5L:] = ci
        lam = lamf_ref[...]
        lamb = _mx(lam)
        lam_ref[...] = lamb
        du_ref[...] = dy * d_ref[...] + _dot(lamb, bbdt_ref[...])
        prev8 = halo_ref[...] * jnp.where(ctx.seq_first, 0.0, 1.0)
        xprev = _shift_rows(xst_ref[...], prev8, 1)
        lr, li = lam[:, :S5L], lam[:, S5L:]
        pr, pi = xprev[:, :S5L], xprev[:, S5L:]
        dar = jnp.sum(lr * pr + li * pi, axis=0, keepdims=True)
        dai = jnp.sum(li * pr - lr * pi, axis=0, keepdims=True)
        _acc(da_acc, jnp.concatenate([dar, dai], axis=1), ctx.first)

    return _rowcall(name, fn, t, tm, seq // tm, tiled=[dout, proj, xst, y_all], halos=[(xst, 8)],
                    res=[bbdt, cbdt, pwc, tabc, d5, w_glu, w_glut, b_glu],
                    out_tiled=[_sds((t, SW)), _sds((t, 2 * S5L), MXU_DTYPE), _sds((t, SW), MXU_DTYPE),
                               _sds((t, SW), MXU_DTYPE), _sds((t, SW), MXU_DTYPE)],
                    out_acc=[_sds((1, 2 * S5L)), _sds((1, SW)), _sds((1, SW))],
                    scratch=[pltpu.VMEM((8, 2 * S5L), F32), pltpu.VMEM((tm, 2 * S5L), F32)], reverse=True)


def _out_fwd(name, yssd, ys5, x1, mod, w_out, lng, lnb, seq):
    t = x1.shape[0]
    tm = 256

    def fn(ctx, tiled, halos, exs, res, outs, accs, exaccs, scr):
        ya_ref, yb_ref, x_ref = tiled
        mod_ref, = exs
        w_ref, g_ref, b_ref = res
        m = _dot(ya_ref[...], w_ref[0:SW, :]) + _dot(yb_ref[...], w_ref[SW:2 * SW, :])
        r = ALPHA * x_ref[...] + mod_ref[0, 5:6, :] * m
        outs[0][...] = _ln_fwd(r, g_ref[...], b_ref[...])
        outs[1][...] = r
        outs[2][...] = m

    return _rowcall(name, fn, t, tm, seq // tm, tiled=[yssd, ys5, x1], exs=[mod], res=[w_out, lng, lnb],
                    out_tiled=[_sds((t, D)), _sds((t, D)), _sds((t, D))])


def _out_bwd(name, dxo, r, m, mod, lng, w_outt, seq):
    t = dxo.shape[0]
    tm = 256

    def fn(ctx, tiled, halos, exs, res, outs, accs, exaccs, scr):
        dxo_ref, r_ref, m_ref = tiled
        mod_ref, = exs
        g_ref, wt_ref = res
        dr, dgam, dbet = _ln_bwd(r_ref[...], g_ref[...], dxo_ref[...])
        outs[0][...] = dr
        _acc(accs[0], dgam, ctx.first)
        _acc(accs[1], dbet, ctx.first)
        _acc(exaccs[0].at[0], jnp.sum(dr * m_ref[...], axis=0, keepdims=True), ctx.ex_enter)
        dm = _mx(mod_ref[0, 5:6, :] * dr)
        outs[1][...] = dm
        dyc = _dot(dm, wt_ref[...])
        outs[2][...] = dyc[:, 0:SW]
        outs[3][...] = dyc[:, SW:2 * SW]

    b = mod.shape[0]
    return _rowcall(name, fn, t, tm, seq // tm, tiled=[dxo, r, m], exs=[mod], res=[lng, w_outt],
                    out_tiled=[_sds((t, D)), _sds((t, D), MXU_DTYPE), _sds((t, SW)), _sds((t, SW))],
                    out_acc=[_sds((1, D)), _sds((1, D))], out_exacc=[_sds((b, 1, D))])


def _s5_discretise(a_re, a_im, log_dt, b_re, b_im):
    dt = jnp.exp(log_dt)[:, None]
    mag = jnp.exp(dt * a_re)
    ab_re, ab_im = mag * jnp.cos(dt * a_im), mag * jnp.sin(dt * a_im)
    den = a_re * a_re + a_im * a_im
    nr, ni = ab_re - 1.0, ab_im
    f_re, f_im = (nr * a_re + ni * a_im) / den, (ni * a_re - nr * a_im) / den
    bb_re = f_re[..., None] * b_re - f_im[..., None] * b_im
    bb_im = f_re[..., None] * b_im + f_im[..., None] * b_re
    return ab_re, ab_im, bb_re, bb_im


def _s5_tables(ab_re, ab_im):
    ar, ai = ab_re.reshape(1, S5L), ab_im.reshape(1, S5L)
    pows = [(ar, ai)]
    for _ in range(7):
        pr, pi = pows[-1]
        pows.append((pr * ar - pi * ai, pr * ai + pi * ar))
    z = jnp.zeros((1, S5L), F32)

    def pack(rows, sign):
        return jnp.concatenate([jnp.concatenate([r for r, _ in rows], axis=0),
                                jnp.concatenate([sign * i for _, i in rows], axis=0)], axis=1)

    sel = [pows[0], pows[1], pows[3]] + [(z, z)] * 5
    pw, pwc = pack(sel, 1.0), pack(sel, -1.0)
    tab = pack(pows, 1.0)
    tabc = pack(pows[::-1], -1.0)
    return pw, tab, pwc, tabc


def _local_step(x, tgt, mod, w, sp, seq):
    t = x.shape[0]
    mxu = MXU_DTYPE
    x1, r1, h1, ab1, f1 = _ffn_fwd("ffn1_fwd", x, mod, 0, w["ffn1_w13"], w["ffn1_w2"], sp["ln1_g"], sp["ln1_b"], seq)
    h2, proj = _proj_fwd("proj_fwd", x1, mod, w["w_in"], seq)
    yssd, xpre, yraw, sprev = _ssd_fwd("ssd_fwd", proj, sp["conv_w"], sp["conv_b"], sp["dt_bias"], sp["a_log"],
                                       sp["d_rep"], sp["ssd_norm_w"], seq)
    (ab_re, ab_im, bb_re, bb_im), disc_vjp = jax.vjp(_s5_discretise, sp["s5_a_re"], sp["s5_a_im"], sp["s5_log_dt"],
                                                     sp["s5_b_re"], sp["s5_b_im"])
    eye = jnp.eye(S5G, dtype=F32)
    bbd = jnp.concatenate([jnp.einsum("gk,gph->ghkp", eye, bb_re).reshape(SW, S5L),
                           jnp.einsum("gk,gph->ghkp", eye, bb_im).reshape(SW, S5L)], axis=1).astype(mxu)
    cbd = jnp.concatenate([jnp.einsum("gk,ghp->gpkh", eye, sp["s5_c_re"]).reshape(S5L, SW),
                           -jnp.einsum("gk,ghp->gpkh", eye, sp["s5_c_im"]).reshape(S5L, SW)], axis=0).astype(mxu)
    pw, tab, pwc, tabc = _s5_tables(lax.stop_gradient(ab_re), lax.stop_gradient(ab_im))
    ys5, xst, y5, xstb, ub = _s5_fwd("s5_fwd", proj, bbd, cbd, pw, tab, sp["s5_d"], w["w_glu"], sp["b_glu"], seq)
    x2, r2, m2 = _out_fwd("out_fwd", yssd, ys5, x1, mod, w["w_out"], sp["ln2_g"], sp["ln2_b"], seq)
    x3, r3, h3, ab3, f3, dy3, loss_acc = _ffn_fwd("ffn2_fwd", x2, mod, 6, w["ffn2_w13"], w["ffn2_w2"], sp["ln3_g"],
                                                  sp["ln3_b"], seq, tgt=tgt)
    dr3, df3, s3, dab3, dg3g, dg3b, dgate3 = _ffn_bwd1("ffn2_bwd1", dy3, r3, ab3, f3, mod, 6, sp["ln3_g"],
                                                        w["ffn2_w2t"], seq)
    dx2, dsh3, dsc3 = _mod_bwd("ffn2_bwd2", dab3, dr3, x2, mod, 6, w["ffn2_w13t"], seq)
    g_ffn2_w13 = _tn_matmul("ffn2_dw13", h3, dab3, D, FB)
    g_ffn2_w2 = _tn_matmul("ffn2_dw2", s3, df3, FB, D)
    dr2, dm2, dyssd, dys5, dg2g, dg2b, dgate2 = _out_bwd("out_bwd", dx2, r2, m2, mod, sp["ln2_g"], w["w_outt"], seq)
    g_w_out = jnp.concatenate([_tn_matmul("dw_out_a", yssd, dm2, SW, D), _tn_matmul("dw_out_b", ys5, dm2, SW, D)], axis=0)
    du, lam, dy5b, g5b, dv5b, da5, dd5, dbglu = _s5_bwd("s5_bwd", dys5, proj, xst, y5, bbd.T, cbd.T, pwc, tabc,
                                                       sp["s5_d"], w["w_glu"], w["w_glut"], sp["b_glu"], seq)
    g_w_glu = _tn_matmul("dw_glu", g5b, dv5b, SW, SW)
    dbfull = _tn_matmul("s5_db", lam, ub, D, SW)
    dcfull = _tn_matmul("s5_dc", xstb, dy5b, D, SW)
    dzx, ddt, dnw, ddl, dcw, dcb, ddtb, dalog = _ssd_bwd("ssd_bwd", dyssd, proj, xpre, yraw, sprev, sp["conv_w"],
                                                         sp["dt_bias"], sp["a_log"], sp["d_rep"], sp["ssd_norm_w"], seq)
    dx1, dproj, dsh2, dsc2 = _mod_bwd("proj_bwd", dzx, dr2, x1, mod, 3, w["w_int"], seq, extra=(du, ddt))
    g_w_in = _tn_matmul("dw_in", h2, dproj, D, PW)
    dr1, df1, s1, dab1, dg1g, dg1b, dgate1 = _ffn_bwd1("ffn1_bwd1", dx1, r1, ab1, f1, mod, 0, sp["ln1_g"],
                                                        w["ffn1_w2t"], seq)
    dx0, dsh1, dsc1 = _mod_bwd("ffn1_bwd2", dab1, dr1, x, mod, 0, w["ffn1_w13t"], seq)
    g_ffn1_w13 = _tn_matmul("ffn1_dw13", h1, dab1, D, FB)
    g_ffn1_w2 = _tn_matmul("ffn1_dw2", s1, df1, FB, D)
    dmod = jnp.concatenate([dsh1, dsc1, dgate1, dsh2, dsc2, dgate2, dsh3, dsc3, dgate3], axis=1)
    d4 = lambda a: a.reshape(S5G, S5P, S5G, S5H)
    dbb_re = jnp.einsum("gpgh->gph", d4(dbfull[:S5L]))
    dbb_im = jnp.einsum("gpgh->gph", d4(dbfull[S5L:]))
    dc_re = jnp.einsum("gpgh->ghp", d4(dcfull[:S5L]))
    dc_im = -jnp.einsum("gpgh->ghp", d4(dcfull[S5L:]))
    g_a_re, g_a_im, g_log_dt, g_b_re, g_b_im = disc_vjp(
        (da5[:, :S5L].reshape(S5G, S5P), da5[:, S5L:].reshape(S5G, S5P), dbb_re, dbb_im))
    small = dict(ln1_g=dg1g, ln1_b=dg1b, ln2_g=dg2g, ln2_b=dg2b, ln3_g=dg3g, ln3_b=dg3b, conv_w=dcw[0:4], conv_b=dcb,
                 dt_bias=ddtb[:, :NH], a_log=dalog[:, :NH], d_ssd=jnp.sum(ddl.reshape(NH, HP), axis=1).reshape(1, NH),
                 ssd_norm_w=dnw, s5_a_re=g_a_re, s5_a_im=g_a_im, s5_log_dt=g_log_dt, s5_b_re=g_b_re, s5_b_im=g_b_im,
                 s5_c_re=dc_re, s5_c_im=dc_im, s5_d=dd5, w_glu_b=dbglu)
    big = dict(ffn1_w13=g_ffn1_w13, ffn1_w2=g_ffn1_w2, w_in=g_w_in, w_glu=g_w_glu, w_out=g_w_out,
               ffn2_w13=g_ffn2_w13, ffn2_w2=g_ffn2_w2)
    return loss_acc[0, 0], dx0, dmod, big, small


def _place():
    return lax.axis_index("x"), lax.axis_index("y"), lax.axis_index("c")


def _other_chips(x, y):
    return [(1 - x, y), (x, 1 - y), (1 - x, 1 - y)]


def _allgather8(name, a):
    r, n = a.shape

    def body(x_ref, out_ref, send_sems, recv_sems, local_sem):
        x, y, c = _place()
        me, sibling = (x, y, c), (x, y, 1 - c)
        chips = _other_chips(x, y)

        def rows(px, py, pc):
            return out_ref.at[pl.ds(pl.multiple_of((4 * px + 2 * py + pc) * r, 8), r), :]

        def copy(k, block, to, src=None):
            return pltpu.make_async_remote_copy(src_ref=rows(*block) if src is None else src, dst_ref=rows(*block),
                                                send_sem=send_sems.at[k], recv_sem=recv_sems.at[k], device_id=to,
                                                device_id_type=MESH_T)

        mine = pltpu.make_async_copy(x_ref, rows(*me), local_sem)
        mine.start()
        first = [copy(0, me, sibling, src=x_ref)]
        first += [copy(1 + j, me, (*chip, c), src=x_ref) for j, chip in enumerate(chips)]
        for cp in first:
            cp.start()
        passed = [copy(4 + j, (*chip, c), sibling) for j, chip in enumerate(chips)]
        for j, chip in enumerate(chips):
            copy(1 + j, (*chip, c), me).wait_recv()
            passed[j].start()
        copy(0, sibling, me).wait_recv()
        for j, chip in enumerate(chips):
            copy(4 + j, (*chip, 1 - c), me).wait_recv()
        for cp in first + passed:
            cp.wait_send()
        mine.wait()

    out = _pcall(body, name=name, out_shape=_sds((8 * r, n), a.dtype),
                 in_specs=[pl.BlockSpec(memory_space=pltpu.VMEM)], out_specs=pl.BlockSpec(memory_space=pltpu.VMEM),
                 scratch_shapes=[pltpu.SemaphoreType.DMA((7,)), pltpu.SemaphoreType.DMA((7,)), pltpu.SemaphoreType.DMA],
                 compiler_params=_cparams())(a)
    return out.reshape(8, r, n)


def _hbm_call(body, name, ins, out_shape, n_sems):
    any_spec = pl.BlockSpec(memory_space=pl.ANY)
    scratch = [pltpu.SemaphoreType.DMA((n,)) for n in n_sems]
    return _pcall(body, name=name, out_shape=out_shape, in_specs=[any_spec] * len(ins), out_specs=[any_spec] * len(out_shape),
                  scratch_shapes=scratch, compiler_params=_cparams())(*ins)


def _rows(ref_rows, half, align):
    hr = ref_rows // 2
    return pl.ds(pl.multiple_of(half * hr, align), hr)


def _gather_weights(name, shards):
    n = len(shards)

    def body(*refs):
        s_refs, o_refs = refs[:n], refs[n:2 * n]
        send_sems, recv_sems = refs[2 * n:]
        x, y, c = _place()
        sibling = (x, y, 1 - c)
        chips = _other_chips(x, y)

        def copy(i, k, chip, half, to, src=None):
            dst = o_refs[i].at[2 * chip[0] + chip[1], _rows(shards[i].shape[0], half, 16)]
            return pltpu.make_async_remote_copy(src_ref=dst if src is None else src, dst_ref=dst,
                                                send_sem=send_sems.at[6 * i + k], recv_sem=recv_sems.at[6 * i + k],
                                                device_id=to, device_id_type=MESH_T)

        first = [copy(i, j, (x, y), c, (*chip, c), src=s_refs[i].at[_rows(shards[i].shape[0], c, 16)])
                 for i in range(n) for j, chip in enumerate(chips)]
        for cp in first:
            cp.start()
        passed = []
        for i in range(n):
            for j, chip in enumerate(chips):
                copy(i, j, chip, c, sibling).wait_recv()
                passed.append(copy(i, 3 + j, chip, c, sibling))
                passed[-1].start()
        for i in range(n):
            for j, chip in enumerate(chips):
                copy(i, 3 + j, chip, 1 - c, sibling).wait_recv()
        for cp in first + passed:
            cp.wait_send()

    return _hbm_call(body, name, shards, [_sds((4,) + s.shape, s.dtype) for s in shards], (6 * n, 6 * n))


def _rs_sibling(name, gs):
    n = len(gs)

    def body(*refs):
        g_refs, o_refs = refs[:n], refs[n:2 * n]
        send_sems, recv_sems = refs[2 * n:]
        x, y, c = _place()
        cps = [pltpu.make_async_remote_copy(src_ref=g_refs[i].at[:, _rows(gs[i].shape[1], 1 - c, 8)], dst_ref=o_refs[i],
                                            send_sem=send_sems.at[i], recv_sem=recv_sems.at[i], device_id=(x, y, 1 - c),
                                            device_id_type=MESH_T) for i in range(n)]
        for cp in cps:
            cp.start()
        for cp in cps:
            cp.wait()

    return _hbm_call(body, name, gs, [_sds((4, g.shape[1] // 2, g.shape[2]), g.dtype) for g in gs], (n, n))


def _rs_chips(name, hs):
    n = len(hs)

    def body(*refs):
        h_refs, o_refs = refs[:n], refs[n:2 * n]
        send_sems, recv_sems = refs[2 * n:]
        x, y, c = _place()
        cps = [pltpu.make_async_remote_copy(src_ref=h_refs[i].at[2 * chip[0] + chip[1]], dst_ref=o_refs[i].at[j],
                                            send_sem=send_sems.at[3 * i + j], recv_sem=recv_sems.at[3 * i + j],
                                            device_id=(*chip, c), device_id_type=MESH_T)
               for i in range(n) for j, chip in enumerate(_other_chips(x, y))]
        for cp in cps:
            cp.start()
        for cp in cps:
            cp.wait()

    return _hbm_call(body, name, hs, [_sds((3,) + h.shape[1:], h.dtype) for h in hs], (3 * n, 3 * n))


def _rs_join(name, fs):
    n = len(fs)

    def body(*refs):
        o_refs = refs[n:2 * n]
        send_sems, recv_sems = refs[2 * n:]
        x, y, c = _place()

        def copy(i, half):
            part = o_refs[i].at[_rows(fs[i].shape[0], half, 8)]
            return pltpu.make_async_remote_copy(src_ref=part, dst_ref=part, send_sem=send_sems.at[i],
                                                recv_sem=recv_sems.at[i], device_id=(x, y, 1 - c), device_id_type=MESH_T)

        sends = [copy(i, c) for i in range(n)]
        for cp in sends:
            cp.start()
        for i in range(n):
            copy(i, 1 - c).wait_recv()
        for cp in sends:
            cp.wait_send()

    any_spec = pl.BlockSpec(memory_space=pl.ANY)
    return _pcall(body, name=name, out_shape=[_sds(f.shape, f.dtype) for f in fs], in_specs=[any_spec] * n,
                  out_specs=[any_spec] * n, input_output_aliases={i: i for i in range(n)},
                  scratch_shapes=[pltpu.SemaphoreType.DMA((n,)), pltpu.SemaphoreType.DMA((n,))],
                  compiler_params=_cparams())(*fs)


def _row_block(r, cap=2048):
    b = min(r, cap)
    while r % b or b % 8:
        b -= 8
    return b


RS_SPLIT = 2


def _rs_add(name, gs, r1s, sel):
    n = len(gs)

    def body(sel_ref, *refs):
        g_refs, r_refs, h_refs, b_refs = (refs[k * n:(k + 1) * n] for k in range(4))
        for i in range(n):
            h = g_refs[i][...] + r_refs[i][...]
            h_refs[i][...] = h
            b_refs[i][...] = h.astype(BF16)

    def blk(g):
        return (1, g.shape[1] // 2 // RS_SPLIT, g.shape[2])

    here = lambda k, j, s: (k, j, 0)
    in_specs = [pl.BlockSpec(blk(g), lambda k, j, s: (k, s[0] * RS_SPLIT + j, 0)) for g in gs]
    in_specs += [pl.BlockSpec(blk(g), here) for g in gs]
    outs = _pcall(body, name=name,
                  out_shape=[_sds((4, g.shape[1] // 2, g.shape[2])) for g in gs]
                  + [_sds((4, g.shape[1] // 2, g.shape[2]), BF16) for g in gs],
                  grid_spec=pltpu.PrefetchScalarGridSpec(num_scalar_prefetch=1, grid=(4, RS_SPLIT), in_specs=in_specs,
                                                         out_specs=[pl.BlockSpec(blk(g), here) for g in gs] * 2),
                  compiler_params=_cparams(("parallel", "parallel")))(sel.reshape(1).astype(jnp.int32), *gs, *r1s)
    return outs[:n], outs[n:]


def _rs_sum(name, hs, r2s, chip, half):
    n = len(hs)

    def body(sel_ref, *refs):
        h_refs, r_refs, o_refs = (refs[k * n:(k + 1) * n] for k in range(3))
        for i in range(n):
            r = r_refs[i]
            o_refs[i][...] = ((h_refs[i][0] + r[0].astype(F32)) + r[1].astype(F32)) + r[2].astype(F32)

    def rows(h):
        return h.shape[1] // RS_SPLIT

    in_specs = [pl.BlockSpec((1, rows(h), h.shape[2]), lambda j, s: (s[0], j, 0)) for h in hs]
    in_specs += [pl.BlockSpec((3, rows(h), h.shape[2]), lambda j, s: (0, j, 0)) for h in hs]
    sel = jnp.stack([chip, half]).astype(jnp.int32)
    return _pcall(body, name=name, out_shape=[_sds((2 * h.shape[1], h.shape[2])) for h in hs],
                  grid_spec=pltpu.PrefetchScalarGridSpec(
                      num_scalar_prefetch=1, grid=(RS_SPLIT,), in_specs=in_specs,
                      out_specs=[pl.BlockSpec((rows(h), h.shape[2]), lambda j, s: (s[1] * RS_SPLIT + j, 0)) for h in hs]),
                  compiler_params=_cparams(("parallel",)))(sel, *hs, *r2s)


def _sum8(name, a):
    _, r, n = a.shape
    br = _row_block(r)

    def body(a_ref, o_ref):
        acc = a_ref[0]
        for k in range(1, 8):
            acc = acc + a_ref[k]
        o_ref[...] = acc

    return _pcall(body, name=name, out_shape=_sds((r, n)), grid=(r // br,),
                  in_specs=[pl.BlockSpec((8, br, n), lambda j: (0, j, 0))], out_specs=pl.BlockSpec((br, n), lambda j: (j, 0)),
                  compiler_params=_cparams(("parallel",)))(a)


def _adamw(name, ws, gs, ms, vs, nblk):
    n = len(ws)

    def body(*refs):
        w_refs, g_refs, m_refs, v_refs, d_refs, nm_refs, nv_refs = (refs[k * n:(k + 1) * n] for k in range(7))
        for i in range(n):
            gv = g_refs[i][...]
            nm = ADAM_B1 * m_refs[i][...] + (1.0 - ADAM_B1) * gv
            nv = ADAM_B2 * v_refs[i][...] + (1.0 - ADAM_B2) * (gv * gv)
            nm_refs[i][...] = nm
            nv_refs[i][...] = nv
            m_hat = nm / (1.0 - ADAM_B1 ** ADAM_STEP)
            v_hat = nv / (1.0 - ADAM_B2 ** ADAM_STEP)
            d_refs[i][...] = -ADAM_LR * (m_hat / (jnp.sqrt(v_hat) + ADAM_EPS) + ADAM_WD * w_refs[i][...])

    specs = [pl.BlockSpec((w.shape[0] // nblk, w.shape[1]), lambda j: (j, 0)) for w in ws]
    outs = _pcall(body, name=name, out_shape=[_sds(w.shape) for w in ws] * 3, grid=(nblk,), in_specs=specs * 4,
                  out_specs=specs * 3, compiler_params=_cparams(("parallel",)))(*ws, *gs, *ms, *vs)
    return outs[:n], outs[n:2 * n], outs[2 * n:]


ADA_COLS = 2304
ADA_BLK = 768


def _ada_fwd(name, c_all, w_shard, b_cols):
    nb = c_all.shape[0]

    def body(c_ref, w_ref, b_ref, o_ref):
        cv = c_ref[...]
        cs = _mx(cv * _sigmoid(cv))
        o_ref[...] = _dot(cs, _mx(w_ref[...])) + b_ref[...]

    return _pcall(body, name=name, out_shape=_sds((nb, ADA_COLS)), grid=(ADA_COLS // ADA_BLK,),
                  in_specs=[pl.BlockSpec((nb, D), lambda j: (0, 0)), pl.BlockSpec((D, ADA_BLK), lambda j: (0, j)),
                            pl.BlockSpec((1, ADA_BLK), lambda j: (0, j))],
                  out_specs=pl.BlockSpec((nb, ADA_BLK), lambda j: (0, j)),
                  compiler_params=_cparams(("parallel",)))(c_all, w_shard, b_cols)


def _ada_bwd(name, c_all, dmod_cols, dmod_all):
    nb = c_all.shape[0]

    def body(c_ref, dc_ref, da_ref, gw_ref, gb_ref):
        cv = c_ref[...]
        cs = _mx(cv * _sigmoid(cv))
        gw_ref[...] = lax.dot_general(cs, _mx(dc_ref[...]), (((0,), (0,)), ((), ())), preferred_element_type=F32)

        @pl.when(pl.program_id(0) == 0)
        def _():
            gb_ref[...] = jnp.sum(da_ref[...], axis=0, keepdims=True)

    return _pcall(body, name=name, out_shape=[_sds((D, ADA_COLS)), _sds((1, 9 * D))], grid=(ADA_COLS // ADA_BLK,),
                  in_specs=[pl.BlockSpec((nb, D), lambda j: (0, 0)), pl.BlockSpec((nb, ADA_BLK), lambda j: (0, j)),
                            pl.BlockSpec((nb, 9 * D), lambda j: (0, 0))],
                  out_specs=[pl.BlockSpec((D, ADA_BLK), lambda j: (0, j)), pl.BlockSpec((1, 9 * D), lambda j: (0, 0))],
                  compiler_params=_cparams(("arbitrary",)))(c_all, dmod_cols, dmod_all)


BIG = ("ffn1_w1", "ffn1_w3", "ffn1_w2", "w_in", "w_glu", "w_out", "ffn2_w1", "ffn2_w3", "ffn2_w2")
COL_SHARDED = ("ffn1_w1", "ffn1_w3", "w_in", "ffn2_w1", "ffn2_w3")
SMALL = ("b_ada", "ln1_g", "ln1_b", "conv_w", "conv_b", "dt_bias", "a_log", "d_ssd", "ssd_norm_w", "s5_a_re", "s5_a_im",
         "s5_log_dt", "s5_b_re", "s5_b_im", "s5_c_re", "s5_c_im", "s5_d", "b_glu", "ln2_g", "ln2_b", "ln3_g", "ln3_b")
WEIGHTS = ("w_ada", "b_ada", "ffn1_w1", "ffn1_w3", "ffn1_w2", "ln1_g", "ln1_b", "w_in", "conv_w", "conv_b", "dt_bias",
           "a_log", "d_ssd", "ssd_norm_w", "s5_a_re", "s5_a_im", "s5_log_dt", "s5_b_re", "s5_b_im", "s5_c_re", "s5_c_im",
           "s5_d", "w_glu", "b_glu", "w_out", "ln2_g", "ln2_b", "ffn2_w1", "ffn2_w3", "ffn2_w2", "ln3_g", "ln3_b")
BIG_PAD = 2 * 1024 * 128


def _pack(arrs, mult, axis_keep=0):
    lead = arrs[0].shape[:axis_keep]
    flat = jnp.concatenate([a.reshape(lead + (-1,)) for a in arrs], axis=axis_keep)
    pad = (-flat.shape[-1]) % mult
    if pad:
        flat = jnp.concatenate([flat, jnp.zeros(lead + (pad,), flat.dtype)], axis=axis_keep)
    return flat


def _unpack(flat, shapes):
    out, off = [], 0
    for s in shapes:
        size = math.prod(s)
        out.append(flat[..., off:off + size].reshape(flat.shape[:-1] + tuple(s)))
        off += size
    return out


def _shard_major(a):
    rows, cols = a.shape
    return a.reshape(rows, 4, cols // 4).transpose(1, 0, 2)


def _from_shard_major(a):
    _, rows, w = a.shape
    return a.transpose(1, 0, 2).reshape(rows, 4 * w)


def kernel(x, c, w_ada, b_ada, ffn1_w1, ffn1_w3, ffn1_w2, ln1_g, ln1_b, w_in, conv_w, conv_b, dt_bias, a_log, d_ssd, ssd_norm_w, s5_a_re, s5_a_im, s5_log_dt, s5_b_re, s5_b_im, s5_c_re, s5_c_im, s5_d, w_glu, b_glu, w_out, ln2_g, ln2_b, ffn2_w1, ffn2_w3, ffn2_w2, ln3_g, ln3_b, loss_target, m_w_ada, m_b_ada, m_ffn1_w1, m_ffn1_w3, m_ffn1_w2, m_ln1_g, m_ln1_b, m_w_in, m_conv_w, m_conv_b, m_dt_bias, m_a_log, m_d_ssd, m_ssd_norm_w, m_s5_a_re, m_s5_a_im, m_s5_log_dt, m_s5_b_re, m_s5_b_im, m_s5_c_re, m_s5_c_im, m_s5_d, m_w_glu, m_b_glu, m_w_out, m_ln2_g, m_ln2_b, m_ffn2_w1, m_ffn2_w3, m_ffn2_w2, m_ln3_g, m_ln3_b, v_w_ada, v_b_ada, v_ffn1_w1, v_ffn1_w3, v_ffn1_w2, v_ln1_g, v_ln1_b, v_w_in, v_conv_w, v_conv_b, v_dt_bias, v_a_log, v_d_ssd, v_ssd_norm_w, v_s5_a_re, v_s5_a_im, v_s5_log_dt, v_s5_b_re, v_s5_b_im, v_s5_c_re, v_s5_c_im, v_s5_d, v_w_glu, v_b_glu, v_w_out, v_ln2_g, v_ln2_b, v_ffn2_w1, v_ffn2_w3, v_ffn2_w2, v_ln3_g, v_ln3_b):
    a = dict(locals())
    xi, yi, ci = _place()
    chip = 2 * xi + yi
    dev = 2 * chip + ci
    nb, seq, _ = x.shape
    t = nb * seq
    ndev = 8

    c_all = _allgather8("gather_c", c.reshape(-1, 128)).reshape(ndev * nb, D)
    b_cols = lax.dynamic_slice(b_ada, (0, chip * ADA_COLS), (1, ADA_COLS))
    mod_part = _ada_fwd("ada_fwd", c_all, w_ada[0], b_cols)
    mod_parts = _allgather8("gather_mod", mod_part.reshape(-1, 128)).reshape(ndev, ndev * nb, ADA_COLS)
    mod_all = mod_parts[0::2].transpose(1, 0, 2).reshape(ndev * nb, 9 * D)
    mod = lax.dynamic_slice(mod_all, (nb * dev, 0), (nb, 9 * D)).reshape(nb, 9, D)

    own = [a[n][0].astype(MXU_DTYPE) for n in BIG]
    full = {}
    for n, mine, piece in zip(BIG, own, _gather_weights("gather_w", own)):
        piece = lax.dynamic_update_slice(piece, mine[None], (chip, 0, 0))
        full[n] = _from_shard_major(piece) if n in COL_SHARDED else piece.reshape(-1, piece.shape[-1])
    wi = full["w_in"]
    w_inp = jnp.concatenate([wi[:, :1536], wi[:, 1544:2056], wi[:, 1536:1544], jnp.zeros((D, 120), wi.dtype)], axis=1)
    w = dict(ffn1_w13=jnp.concatenate([full["ffn1_w1"], full["ffn1_w3"]], axis=1), ffn1_w2=full["ffn1_w2"],
             ffn2_w13=jnp.concatenate([full["ffn2_w1"], full["ffn2_w3"]], axis=1), ffn2_w2=full["ffn2_w2"],
             w_in=w_inp, w_glu=full["w_glu"], w_out=full["w_out"])
    for n in ("ffn1_w13", "ffn1_w2", "ffn2_w13", "ffn2_w2", "w_in", "w_out"):
        w[n + "t"] = w[n].T
    w["w_glut"] = w["w_glu"].T

    cw_all = _allgather8("gather_conv_w", _pack([conv_w[0]], 1024).reshape(-1, 128)).reshape(ndev, -1)[0::2, :4 * 256]
    conv_full = _from_shard_major(cw_all.reshape(4, 4, 256))
    pad8 = lambda v: jnp.concatenate([v.reshape(1, NH), jnp.zeros((1, 128 - NH), F32)], axis=1)
    sp = dict(ln1_g=ln1_g, ln1_b=ln1_b, ln2_g=ln2_g, ln2_b=ln2_b, ln3_g=ln3_g, ln3_b=ln3_b, conv_w=conv_full,
              conv_b=conv_b, dt_bias=pad8(dt_bias), a_log=pad8(a_log), d_rep=jnp.repeat(d_ssd[0], HP)[None],
              ssd_norm_w=ssd_norm_w, s5_a_re=s5_a_re[0], s5_a_im=s5_a_im[0], s5_log_dt=s5_log_dt[0], s5_b_re=s5_b_re[0],
              s5_b_im=s5_b_im[0], s5_c_re=s5_c_re[0], s5_c_im=s5_c_im[0], s5_d=s5_d, b_glu=b_glu)

    lsum, dx0, dmod, big, small = _local_step(x.reshape(t, D), loss_target.reshape(t, D), mod, w, sp, seq)
    loss = lax.psum(lsum * (0.5 / D), ("x", "y", "c"))

    dmod_all = _allgather8("gather_dmod", dmod.reshape(-1, 128)).reshape(ndev * nb, 9 * D)
    dmod_cols = lax.dynamic_slice(dmod_all, (0, chip * ADA_COLS), (ndev * nb, ADA_COLS))
    g_w_ada, g_b_ada = _ada_bwd("ada_bwd", c_all, dmod_cols, dmod_all)

    gwi = big["w_in"]
    gfull = dict(ffn1_w1=big["ffn1_w13"][:, :FF], ffn1_w3=big["ffn1_w13"][:, FF:], ffn1_w2=big["ffn1_w2"],
                 w_in=jnp.concatenate([gwi[:, :1536], gwi[:, 2048:2056], gwi[:, 1536:2048]], axis=1),
                 w_glu=big["w_glu"], w_out=big["w_out"],
                 ffn2_w1=big["ffn2_w13"][:, :FF], ffn2_w3=big["ffn2_w13"][:, FF:], ffn2_w2=big["ffn2_w2"])
    gsh = [_shard_major(gfull[n]) if n in COL_SHARDED else gfull[n].reshape((4,) + a[n].shape[1:]) for n in BIG]
    r1 = _rs_sibling("rs_sibling", gsh)
    hsum, hsum_bf = _rs_add("rs_add", gsh, r1, ci)
    r2 = _rs_chips("rs_chips", hsum_bf)
    fhalf = _rs_sum("rs_sum", hsum, r2, chip, ci)
    gbig = dict(zip(BIG, _rs_join("rs_join", fhalf)))
    gbig["w_ada"] = g_w_ada

    outs = {}
    for call, names in (("adamw_a", ("ffn1_w1", "ffn1_w3", "ffn1_w2", "w_in", "w_glu", "w_out")),
                        ("adamw_b", ("ffn2_w1", "ffn2_w3", "ffn2_w2", "w_ada"))):
        res = _adamw(call, [a[n][0] for n in names], [gbig[n] for n in names], [a["m_" + n][0] for n in names],
                     [a["v_" + n][0] for n in names], 8)
        for kind, arrs in zip(("grad", "delta", "new_m", "new_v"), ([gbig[n] for n in names],) + tuple(res)):
            for n, arr in zip(names, arrs):
                outs[kind, n] = arr[None]

    snames = [n for n in SMALL if n != "b_ada"]
    sgrad = dict(small)
    sgrad["b_glu"] = small["w_glu_b"]
    svec = _pack([sgrad[n] for n in snames], 1024).reshape(-1, 128)
    ssum = _sum8("small_sum", _allgather8("gather_small", svec)).reshape(-1)

    def view2d(u):
        s = u.shape[1:]
        return u.reshape((1, s[0]) if len(s) == 1 else (-1, s[-1]))

    vshape = {n: view2d(a[n]).shape for n in SMALL}
    gsm = dict(zip(snames, _unpack(ssum, [vshape[n] if n != "conv_w" else (4, D) for n in snames])))
    gsm["conv_w"] = lax.dynamic_slice(gsm["conv_w"], (0, chip * 256), (4, 256))
    gsm["b_ada"] = g_b_ada
    res = _adamw("adamw_small", [view2d(a[n]) for n in SMALL], [gsm[n] for n in SMALL],
                 [view2d(a["m_" + n]) for n in SMALL], [view2d(a["v_" + n]) for n in SMALL], 1)
    for kind, arrs in zip(("grad", "delta", "new_m", "new_v"), ([gsm[n] for n in SMALL],) + tuple(res)):
        for n, arr in zip(SMALL, arrs):
            outs[kind, n] = arr.reshape(a[n].shape)

    res = [loss, dx0.reshape(nb, seq, D)]
    for kind in ("grad", "delta", "new_m", "new_v"):
        res += [outs[kind, n] for n in WEIGHTS]
    return tuple(res)
```

```python
import functools
import math

import jax
import jax.numpy as jnp
from jax import lax
from jax.experimental import pallas as pl
from jax.experimental.pallas import tpu as pltpu

F32 = jnp.float32
BF16 = jnp.bfloat16
MXU_DTYPE = jnp.bfloat16

D = 1024
FF = 2816
FB = 1408
NH, HP, NS, NG = 8, 64, 128, 2
CH = 128
SW = 512
S5G, S5P, S5H = 32, 64, 16
S5L = S5G * S5P
PW = 2176
ALPHA = 2.0 ** 0.25
LN_EPS = 1e-5
ADAM_LR, ADAM_B1, ADAM_B2, ADAM_EPS, ADAM_WD, ADAM_STEP = 0.001, 0.9, 0.999, 1e-08, 0.01, 10
VMEM_LIMIT = 56 * 1024 * 1024
MESH_T = pl.DeviceIdType.MESH


def _pcall(body, **kw):
    return pl.pallas_call(body, **kw)


def _cparams(sem=None, **kw):
    return pltpu.CompilerParams(dimension_semantics=sem, vmem_limit_bytes=VMEM_LIMIT, **kw)


def _dot(a, b):
    return jnp.dot(a, b, preferred_element_type=F32)


def _dot_nt(a, b):
    return lax.dot_general(a, b, (((1,), (1,)), ((), ())), preferred_element_type=F32)


def _dot_hi(a, b):
    return jnp.dot(a, b, preferred_element_type=F32, precision=lax.Precision.HIGHEST)


def _mx(a):
    return a.astype(MXU_DTYPE)


def _sigmoid(x):
    return 1.0 / (1.0 + jnp.exp(-x))


def _iota(shape, axis):
    return lax.broadcasted_iota(jnp.int32, shape, axis)


def _rowcall(name, fn, n_rows, tm, tpe, *, tiled=(), halos=(), exs=(), res=(), out_tiled=(), out_acc=(),
             out_exacc=(), scratch=(), reverse=False, batch=None):
    if batch:
        n_rows //= batch
    nt = n_rows // tm

    def blk(i):
        return (nt - 1 - i) if reverse else i

    in_specs, args = [], []
    for a in tiled:
        if batch:
            in_specs.append(pl.BlockSpec((batch, tm, a.shape[1]), lambda i: (0, blk(i), 0)))
            args.append(a.reshape(batch, n_rows, a.shape[1]))
            continue
        in_specs.append(pl.BlockSpec((tm, a.shape[1]), lambda i: (blk(i), 0)))
        args.append(a)
    for a, rows in halos:
        r = tm // rows
        if batch:
            in_specs.append(pl.BlockSpec((batch, rows, a.shape[1]), lambda i, r=r: (0, jnp.maximum(blk(i) * r - 1, 0), 0)))
            args.append(a.reshape(batch, n_rows, a.shape[1]))
            continue
        in_specs.append(pl.BlockSpec((rows, a.shape[1]), lambda i, r=r: (jnp.maximum(blk(i) * r - 1, 0), 0)))
        args.append(a)
    for a in exs:
        in_specs.append(pl.BlockSpec((1,) + a.shape[1:], lambda i: (blk(i) // tpe, 0, 0)))
        args.append(a)
    for a in res:
        nd = a.ndim
        in_specs.append(pl.BlockSpec(a.shape, lambda i, nd=nd: (0,) * nd, pipeline_mode=pl.Buffered(1)))
        args.append(a)
    out_specs, out_shape = [], []
    for s in out_tiled:
        if batch:
            out_specs.append(pl.BlockSpec((batch, tm, s.shape[1]), lambda i: (0, blk(i), 0)))
            out_shape.append(_sds((batch, n_rows, s.shape[1]), s.dtype))
            continue
        out_specs.append(pl.BlockSpec((tm, s.shape[1]), lambda i: (blk(i), 0)))
        out_shape.append(s)
    for s in out_acc:
        nd = len(s.shape)
        out_specs.append(pl.BlockSpec(s.shape, lambda i, nd=nd: (0,) * nd))
        out_shape.append(s)
    for s in out_exacc:
        out_specs.append(pl.BlockSpec((1,) + s.shape[1:], lambda i: (blk(i) // tpe, 0, 0)))
        out_shape.append(s)
    n = [len(tiled), len(halos), len(exs), len(res), len(out_tiled), len(out_acc), len(out_exacc), len(scratch)]

    def body(*refs):
        groups, k = [], 0
        for m in n:
            groups.append(refs[k:k + m])
            k += m
        i = pl.program_id(0)
        b = blk(i)

        class ctx:
            first = i == 0
            pos = b % tpe
            seq_first = (b % tpe) == 0
            seq_last = (b % tpe) == tpe - 1
            ex_enter = (i % tpe) == 0

        fn(ctx, *groups)

    outs = _pcall(body, name=name, grid=(nt,), in_specs=in_specs, out_specs=out_specs, out_shape=out_shape,
                  scratch_shapes=list(scratch), compiler_params=_cparams(("arbitrary",)))(*args)
    if batch:
        outs = [o.reshape(batch * n_rows, o.shape[2]) if k < len(out_tiled) else o for k, o in enumerate(outs)]
    return outs


def _acc(ref, val, first):
    if first is not None:
        @pl.when(first)
        def _():
            ref[...] = jnp.zeros(ref.shape, ref.dtype)
    ref[...] += val


def _sds(shape, dtype=F32):
    return jax.ShapeDtypeStruct(shape, dtype)


def _ln_fwd(r, g, b):
    mu = jnp.mean(r, axis=-1, keepdims=True)
    rc = r - mu
    var = jnp.mean(rc * rc, axis=-1, keepdims=True)
    return rc * lax.rsqrt(var + LN_EPS) * g + b


def _ln_bwd(r, g, dy):
    mu = jnp.mean(r, axis=-1, keepdims=True)
    rc = r - mu
    var = jnp.mean(rc * rc, axis=-1, keepdims=True)
    rstd = lax.rsqrt(var + LN_EPS)
    xhat = rc * rstd
    dxh = dy * g
    dr = rstd * (dxh - jnp.mean(dxh, axis=-1, keepdims=True) - xhat * jnp.mean(dxh * xhat, axis=-1, keepdims=True))
    return dr, jnp.sum(dy * xhat, axis=0, keepdims=True), jnp.sum(dy, axis=0, keepdims=True)


def _ffn_fwd(name, x, mod, k0, w13, w2, lng, lnb, seq, tgt=None):
    t = x.shape[0]
    tm = 256
    with_loss = tgt is not None

    def fn(ctx, tiled, halos, exs, res, outs, accs, exaccs, scr):
        x_ref = tiled[0]
        mod_ref, = exs
        w13_ref, w2_ref, g_ref, b_ref = res
        xo_ref, r_ref, h_ref, ab_ref, f_ref = outs[:5]
        xv = x_ref[...]
        sh, sc, g = mod_ref[0, k0:k0 + 1, :], mod_ref[0, k0 + 1:k0 + 2, :], mod_ref[0, k0 + 2:k0 + 3, :]
        h = _mx(xv * (1.0 + sc) + sh)
        h_ref[...] = h
        f = jnp.zeros((tm, D), F32)
        for j in range(2):
            a = _dot(h, w13_ref[:, j * FB:(j + 1) * FB])
            b = _dot(h, w13_ref[:, FF + j * FB:FF + (j + 1) * FB])
            ab_ref[:, j * FB:(j + 1) * FB] = a
            ab_ref[:, FF + j * FB:FF + (j + 1) * FB] = b
            s = a * _sigmoid(a) * b
            f = f + _dot(_mx(s), w2_ref[j * FB:(j + 1) * FB, :])
        f_ref[...] = f
        r = ALPHA * xv + 0.5 * g * f
        r_ref[...] = r
        xo = _ln_fwd(r, g_ref[...], b_ref[...])
        xo_ref[...] = xo
        if with_loss:
            e = xo - tiled[1][...]
            outs[5][...] = e * (1.0 / D)
            _acc(accs[0], jnp.sum(e * e) * jnp.ones((8, 128), F32), ctx.first)

    out_tiled = [_sds((t, D)), _sds((t, D)), _sds((t, D), MXU_DTYPE), _sds((t, 2 * FF)), _sds((t, D))]
    if with_loss:
        out_tiled.append(_sds((t, D)))
    return _rowcall(name, fn, t, tm, seq // tm, tiled=[x] + ([tgt] if with_loss else []), exs=[mod],
                    res=[w13, w2, lng, lnb], out_tiled=out_tiled, out_acc=[_sds((8, 128))] if with_loss else [])


def _ffn_bwd1(name, dxo, r, ab, f, mod, k0, lng, w2t, seq):
    t = dxo.shape[0]
    tm = 256

    def fn(ctx, tiled, halos, exs, res, outs, accs, exaccs, scr):
        dxo_ref, r_ref, ab_ref, f_ref = tiled
        mod_ref, = exs
        g_ref, w2t_ref = res
        dr_ref, df_ref, s_ref, dab_ref = outs
        g = mod_ref[0, k0 + 2:k0 + 3, :]
        dr, dgam, dbet = _ln_bwd(r_ref[...], g_ref[...], dxo_ref[...])
        dr_ref[...] = dr
        _acc(accs[0], dgam, ctx.first)
        _acc(accs[1], dbet, ctx.first)
        _acc(exaccs[0].at[0], jnp.sum(0.5 * f_ref[...] * dr, axis=0, keepdims=True), ctx.ex_enter)
        df = _mx(0.5 * g * dr)
        df_ref[...] = df
        for j in range(2):
            ds = _dot(df, w2t_ref[:, j * FB:(j + 1) * FB])
            a = ab_ref[:, j * FB:(j + 1) * FB]
            b = ab_ref[:, FF + j * FB:FF + (j + 1) * FB]
            sig = _sigmoid(a)
            silu = a * sig
            s_ref[:, j * FB:(j + 1) * FB] = _mx(silu * b)
            dab_ref[:, j * FB:(j + 1) * FB] = _mx(ds * b * (sig * (1.0 + a * (1.0 - sig))))
            dab_ref[:, FF + j * FB:FF + (j + 1) * FB] = _mx(ds * silu)

    b = mod.shape[0]
    return _rowcall(name, fn, t, tm, seq // tm, tiled=[dxo, r, ab, f], exs=[mod], res=[lng, w2t],
                    out_tiled=[_sds((t, D)), _sds((t, D), MXU_DTYPE), _sds((t, FF), MXU_DTYPE),
                               _sds((t, 2 * FF), MXU_DTYPE)],
                    out_acc=[_sds((1, D)), _sds((1, D))], out_exacc=[_sds((b, 1, D))])


def _mod_bwd(name, dab, dr, x, mod, k0, wt, seq, extra=()):
    t = dr.shape[0]
    tm = 256
    nin = 1 + len(extra)
    width = dab.shape[1] + sum(e.shape[1] for e in extra)

    def fn(ctx, tiled, halos, exs, res, outs, accs, exaccs, scr):
        parts = tiled[:nin]
        dr_ref, x_ref = tiled[nin:]
        mod_ref, = exs
        wt_ref, = res
        sc = mod_ref[0, k0 + 1:k0 + 2, :]
        if nin == 1:
            dp = parts[0][...]
        else:
            dp = jnp.concatenate([_mx(p[...]) for p in parts], axis=1)
            outs[1][...] = dp
        dh = _dot(dp, wt_ref[...])
        outs[0][...] = ALPHA * dr_ref[...] + dh * (1.0 + sc)
        _acc(exaccs[0].at[0], jnp.sum(dh, axis=0, keepdims=True), ctx.ex_enter)
        _acc(exaccs[1].at[0], jnp.sum(dh * x_ref[...], axis=0, keepdims=True), ctx.ex_enter)

    b = mod.shape[0]
    out_tiled = [_sds((t, D))] + ([_sds((t, width), MXU_DTYPE)] if nin > 1 else [])
    return _rowcall(name, fn, t, tm, seq // tm, tiled=[dab, *extra, dr, x], exs=[mod], res=[wt],
                    out_tiled=out_tiled, out_exacc=[_sds((b, 1, D)), _sds((b, 1, D))])


def _tn_matmul(name, a, b, bm, bn, bt=512):
    t, m = a.shape
    n = b.shape[1]

    def body(a_ref, b_ref, o_ref):
        @pl.when(pl.program_id(2) == 0)
        def _():
            o_ref[...] = jnp.zeros(o_ref.shape, F32)
        o_ref[...] += lax.dot_general(a_ref[...], b_ref[...], (((0,), (0,)), ((), ())), preferred_element_type=F32)

    return _pcall(body, name=name, grid=(m // bm, n // bn, t // bt),
                  in_specs=[pl.BlockSpec((bt, bm), lambda i, j, k: (k, i)), pl.BlockSpec((bt, bn), lambda i, j, k: (k, j))],
                  out_specs=pl.BlockSpec((bm, bn), lambda i, j, k: (i, j)), out_shape=_sds((m, n)),
                  compiler_params=_cparams(("parallel", "parallel", "arbitrary")))(a, b)


def _proj_fwd(name, x, mod, w_in, seq):
    t = x.shape[0]
    tm = 256

    def fn(ctx, tiled, halos, exs, res, outs, accs, exaccs, scr):
        mod_ref, = exs
        sh, sc = mod_ref[0, 3:4, :], mod_ref[0, 4:5, :]
        h = _mx(tiled[0][...] * (1.0 + sc) + sh)
        outs[0][...] = h
        outs[1][...] = _dot(h, res[0][...])

    return _rowcall(name, fn, t, tm, seq // tm, tiled=[x], exs=[mod], res=[w_in],
                    out_tiled=[_sds((t, D), MXU_DTYPE), _sds((t, PW))])


def _shift_rows(cur, prev8, j):
    if j == 0:
        return cur
    rolled = pltpu.roll(cur, j, 0)
    top = jnp.where(_iota((8, cur.shape[1]), 0) < j, pltpu.roll(prev8, j, 0), rolled[0:8])
    return jnp.concatenate([top, rolled[8:]], axis=0)


def _shift_rows_up(cur, next8, j):
    if j == 0:
        return cur
    n = cur.shape[0]
    rolled = pltpu.roll(cur, n - j, 0)
    bot = jnp.where(_iota((8, cur.shape[1]), 0) >= 8 - j, pltpu.roll(next8, 8 - j, 0), rolled[n - 8:n])
    return jnp.concatenate([rolled[:n - 8], bot], axis=0)


def _softplus(x):
    return jnp.maximum(x, 0.0) + jnp.log(1.0 + jnp.exp(-jnp.abs(x)))


def _ssd_common(proj_ref, xpre, dtb_ref, alog_ref):
    xbc = xpre * _sigmoid(xpre)
    xs, bm, cm = xbc[:, 0:SW], xbc[:, SW:SW + 256], xbc[:, SW + 256:SW + 512]
    dtraw = proj_ref[:, PW - 128:PW] + dtb_ref[...]
    dt = _softplus(dtraw)
    a = -jnp.exp(alog_ref[...])
    tril = (_iota((CH, CH), 0) >= _iota((CH, CH), 1)).astype(F32)
    acs = _dot_hi(tril, dt * a)
    return xs, bm, cm, dtraw, dt, a, acs, acs.T


def _pair_lane(lo, hi):
    r = lo.shape[0]
    return jnp.where(_iota((r, 128), 1) < HP, lo, hi)


def _ssd_fwd(name, proj, conv_w, conv_b, dt_bias, a_log, d_rep, norm_w, seq):
    t = proj.shape[0]
    nb = t // seq

    def fn(ctx, tiled, halos, exs, res, outs, accs, exaccs, scr):
        @pl.when(ctx.seq_first)
        def _():
            scr[0][...] = jnp.zeros(scr[0].shape, F32)

        for b in range(nb):
            one(ctx, res, [r.at[b] for r in tiled + halos + outs + scr])

    def one(ctx, res, refs):
        proj_ref, halo_ref, yo_ref, xpre_ref, y_ref, sprev_ref, state_ref = refs
        cw_ref, cb_ref, dtb_ref, alog_ref, d_ref, nw_ref = res
        raw = proj_ref[:, SW:SW + D]
        prev8 = halo_ref[:, SW:SW + D] * jnp.where(ctx.seq_first, 0.0, 1.0)
        xpre = cb_ref[...] + cw_ref[3:4, :] * raw
        for j in (1, 2, 3):
            xpre = xpre + cw_ref[3 - j:4 - j, :] * _shift_rows(raw, prev8, j)
        xpre_ref[...] = xpre
        xs, bm, cm, dtraw, dt, a, acs, acst = _ssd_common(proj_ref, xpre, dtb_ref, alog_ref)
        causal = _iota((CH, CH), 0) >= _iota((CH, CH), 1)
        lane_lo = _iota((CH, 128), 1) < HP
        sprev = state_ref[...]
        sprev_ref[...] = sprev
        ys = []
        for g in range(NG):
            bmg, cmg = bm[:, g * NS:(g + 1) * NS], cm[:, g * NS:(g + 1) * NS]
            bmt = bmg.T
            cb = _dot(_mx(cmg), _mx(bmt))
            for q in (2 * g, 2 * g + 1):
                xsq = xs[:, 128 * q:128 * q + 128]
                xd = xsq * _pair_lane(dt[:, 2 * q:2 * q + 1], dt[:, 2 * q + 1:2 * q + 2])
                sp = sprev[:, 128 * q:128 * q + 128]
                ydiag = jnp.zeros((CH, 128), F32)
                snew = jnp.zeros((NS, 128), F32)
                for jj in range(2):
                    h = 2 * q + jj
                    col, row = acs[:, h:h + 1], acst[h:h + 1, :]
                    lm = jnp.where(causal, jnp.exp(jnp.minimum(col - row, 0.0)), 0.0)
                    xm = _mx(jnp.where(lane_lo if jj == 0 else ~lane_lo, xd, 0.0))
                    ydiag = ydiag + _dot(_mx(cb * lm), xm)
                    dec_row = jnp.exp(acst[h:h + 1, CH - 1:CH] - row)
                    snew = snew + _dot(_mx(bmt * dec_row), xm)
                e_pair = jnp.exp(_pair_lane(acs[:, 2 * q:2 * q + 1], acs[:, 2 * q + 1:2 * q + 2]))
                yoff = _dot(_mx(cmg), _mx(sp)) * e_pair
                cd = jnp.exp(_pair_lane(acst[2 * q:2 * q + 1, CH - 1:CH], acst[2 * q + 1:2 * q + 2, CH - 1:CH]))
                state_ref[:, 128 * q:128 * q + 128] = cd * sp + snew
                ys.append(ydiag + yoff + d_ref[:, 128 * q:128 * q + 128] * xsq)
        y = jnp.concatenate(ys, axis=1)
        y_ref[...] = y
        z = proj_ref[:, 0:SW]
        yz = y * (z * _sigmoid(z))
        outp = []
        for g in range(NG):
            seg = yz[:, 256 * g:256 * g + 256]
            rinv = lax.rsqrt(jnp.mean(seg * seg, axis=-1, keepdims=True) + LN_EPS)
            outp.append(seg * rinv * nw_ref[:, 256 * g:256 * g + 256])
        yo_ref[...] = _mx(jnp.concatenate(outp, axis=1))

    return _rowcall(name, fn, t, CH, seq // CH, tiled=[proj], halos=[(proj, 8)],
                    res=[conv_w, conv_b, dt_bias, a_log, d_rep, norm_w],
                    out_tiled=[_sds((t, SW), MXU_DTYPE), _sds((t, D)), _sds((t, SW)), _sds((t, SW))],
                    scratch=[pltpu.VMEM((nb, NS, SW), F32)], batch=nb)


def _ssd_bwd(name, dyo, proj, xpre_all, y_all, sprev_all, conv_w, dt_bias, a_log, d_rep, norm_w, seq):
    t = proj.shape[0]
    nb = t // seq

    def fn(ctx, tiled, halos, exs, res, outs, accs, exaccs, scr):
        @pl.when(ctx.seq_last)
        def _():
            for r in scr:
                r[...] = jnp.zeros(r.shape, F32)

        @pl.when(ctx.first)
        def _():
            for r in accs:
                r[...] = jnp.zeros(r.shape, F32)

        for b in range(nb):
            one(ctx, None, res, accs, [r.at[b] for r in tiled + halos + outs + scr])

    def one(ctx, first, res, accs, refs):
        dyo_ref, proj_ref, xpre_ref, y_ref, sprev_ref, halo_ref, dzx_ref, ddt_ref, ds_ref, nxt_ref = refs
        cw_ref, dtb_ref, alog_ref, d_ref, nw_ref = res
        dnw_acc, dd_acc, dcw_acc, dcb_acc, ddtb_acc, dalog_acc = accs
        xpre = xpre_ref[...]
        xs, bm, cm, dtraw, dt, a, acs, acst = _ssd_common(proj_ref, xpre, dtb_ref, alog_ref)
        y = y_ref[...]
        z = proj_ref[:, 0:SW]
        sz = _sigmoid(z)
        siluz = z * sz
        yz = y * siluz
        dyo = dyo_ref[...]
        dyz_parts, dnw_parts = [], []
        for g in range(NG):
            seg = yz[:, 256 * g:256 * g + 256]
            rinv = lax.rsqrt(jnp.mean(seg * seg, axis=-1, keepdims=True) + LN_EPS)
            yn = seg * rinv
            dseg = dyo[:, 256 * g:256 * g + 256]
            dnw_parts.append(jnp.sum(dseg * yn, axis=0, keepdims=True))
            dyn = dseg * nw_ref[:, 256 * g:256 * g + 256]
            dyz_parts.append(rinv * (dyn - yn * jnp.mean(dyn * yn, axis=-1, keepdims=True)))
        dyz = jnp.concatenate(dyz_parts, axis=1)
        _acc(dnw_acc, jnp.concatenate(dnw_parts, axis=1), first)
        dy = dyz * siluz
        dz = dyz * y * (sz * (1.0 + z * (1.0 - sz)))
        _acc(dd_acc, jnp.sum(dy * xs, axis=0, keepdims=True), first)

        causal = _iota((CH, CH), 0) >= _iota((CH, CH), 1)
        anti = _iota((CH, CH), 0) <= _iota((CH, CH), 1)
        lane_lo = _iota((CH, 128), 1) < HP
        lane_id = _iota((CH, 128), 1)
        last_row = _iota((CH, 128), 0) == CH - 1
        sprev = sprev_ref[...]
        dacs = jnp.zeros((CH, 128), F32)
        ddt_x = jnp.zeros((CH, 128), F32)
        dxs_parts, dbm_parts, dcm_parts = [], [], []
        for g in range(NG):
            bmg, cmg = bm[:, g * NS:(g + 1) * NS], cm[:, g * NS:(g + 1) * NS]
            bmt, cmt = bmg.T, cmg.T
            cb = _dot(_mx(cmg), _mx(bmt))
            cbt = _dot(_mx(bmg), _mx(cmt))
            dcb = jnp.zeros((CH, CH), F32)
            dcbt = jnp.zeros((CH, CH), F32)
            dbmg = jnp.zeros((CH, NS), F32)
            dcmg = jnp.zeros((CH, NS), F32)
            for q in (2 * g, 2 * g + 1):
                sl = slice(128 * q, 128 * q + 128)
                xsq = xs[:, sl]
                dtp = _pair_lane(dt[:, 2 * q:2 * q + 1], dt[:, 2 * q + 1:2 * q + 2])
                xd = xsq * dtp
                dyq = dy[:, sl]
                sp = sprev[:, sl]
                dsn = ds_ref[:, sl]
                e_pair = jnp.exp(_pair_lane(acs[:, 2 * q:2 * q + 1], acs[:, 2 * q + 1:2 * q + 2]))
                cd = jnp.exp(_pair_lane(acst[2 * q:2 * q + 1, CH - 1:CH], acst[2 * q + 1:2 * q + 2, CH - 1:CH]))
                dye = dyq * e_pair
                dcmg = dcmg + _dot(_mx(dye), _mx(sp.T))
                dsp = _dot(_mx(cmt), _mx(dye)) + cd * dsn
                yoff = _dot(_mx(cmg), _mx(sp)) * e_pair
                dacs_lane = dyq * yoff
                dxd = jnp.zeros((CH, 128), F32)
                sds = jnp.sum(dsn * sp, axis=0, keepdims=True) * cd
                for jj in range(2):
                    h = 2 * q + jj
                    hm = lane_lo if jj == 0 else ~lane_lo
                    col, row = acs[:, h:h + 1], acst[h:h + 1, :]
                    lm = jnp.where(causal, jnp.exp(jnp.minimum(col - row, 0.0)), 0.0)
                    lmt = jnp.where(anti, jnp.exp(jnp.minimum(row - col, 0.0)), 0.0)
                    xm = _mx(jnp.where(hm, xd, 0.0))
                    dym = _mx(jnp.where(hm, dyq, 0.0))
                    gm = _dot_nt(dym, xm)
                    gmt = _dot_nt(xm, dym)
                    dcb = dcb + gm * lm
                    dcbt = dcbt + gmt * lmt
                    dxd = dxd + _dot(_mx(cbt * lmt), dym)
                    w = gm * cb * lm
                    wt = gmt * cbt * lmt
                    dacs_h = jnp.sum(w, axis=1, keepdims=True) - jnp.sum(wt, axis=1, keepdims=True)
                    alast = acst[h:h + 1, CH - 1:CH]
                    dec_col = jnp.exp(alast - col)
                    dsm = _mx(jnp.where(hm[0:NS], dsn, 0.0))
                    qh = _dot_nt(xm, dsm)
                    dbmg = dbmg + qh * dec_col
                    ddec = jnp.sum(qh * bmg, axis=1, keepdims=True)
                    dxd = dxd + _dot(_mx(bmg * dec_col), dsm)
                    dacs_h = dacs_h - ddec * dec_col
                    dacs_h = dacs_h + jnp.sum(jnp.where(hm, dacs_lane, 0.0), axis=1, keepdims=True)
                    tail = jnp.sum(ddec * dec_col, axis=0, keepdims=True) + jnp.sum(
                        jnp.where(hm[0:1], sds, 0.0), axis=1, keepdims=True)
                    dacs = dacs + jnp.where(lane_id == h, dacs_h, 0.0) + jnp.where(
                        last_row & (lane_id == h), tail, 0.0)
                ds_ref[:, sl] = dsp
                for jj in range(2):
                    h = 2 * q + jj
                    hm = lane_lo if jj == 0 else ~lane_lo
                    ddt_x = ddt_x + jnp.where(lane_id == h, jnp.sum(jnp.where(hm, dxd * xsq, 0.0), axis=1, keepdims=True), 0.0)
                dxs_parts.append(dxd * dtp + d_ref[:, sl] * dyq)
            dcmg = dcmg + _dot(_mx(dcb), _mx(bmg))
            dbmg = dbmg + _dot(_mx(dcbt), _mx(cmg))
            dbm_parts.append(dbmg)
            dcm_parts.append(dcmg)
        triu = (_iota((CH, CH), 0) <= _iota((CH, CH), 1)).astype(F32)
        dadt = _dot_hi(triu, dacs)
        ddt = dadt * a + ddt_x
        _acc(dalog_acc, jnp.sum(dadt * dt, axis=0, keepdims=True) * a, first)
        ddtraw = ddt * _sigmoid(dtraw)
        ddt_ref[...] = ddtraw
        _acc(ddtb_acc, jnp.sum(ddtraw, axis=0, keepdims=True), first)
        dxbc = jnp.concatenate(dxs_parts + dbm_parts + dcm_parts, axis=1)
        sx = _sigmoid(xpre)
        dpre = dxbc * (sx * (1.0 + xpre * (1.0 - sx)))
        _acc(dcb_acc, jnp.sum(dpre, axis=0, keepdims=True), first)
        raw = proj_ref[:, SW:SW + D]
        prev8 = halo_ref[:, SW:SW + D] * jnp.where(ctx.seq_first, 0.0, 1.0)
        next8 = nxt_ref[...]
        draw = cw_ref[3:4, :] * dpre
        dcw = [None] * 4
        dcw[3] = jnp.sum(dpre * raw, axis=0, keepdims=True)
        for j in (1, 2, 3):
            dcw[3 - j] = jnp.sum(dpre * _shift_rows(raw, prev8, j), axis=0, keepdims=True)
            draw = draw + cw_ref[3 - j:4 - j, :] * _shift_rows_up(dpre, next8, j)
        _acc(dcw_acc, jnp.concatenate(dcw + [jnp.zeros((4, D), F32)], axis=0), first)
        nxt_ref[...] = dpre[0:8]
        dzx_ref[:, 0:SW] = dz
        dzx_ref[:, SW:SW + D] = draw

    return _rowcall(name, fn, t, CH, seq // CH, tiled=[dyo, proj, xpre_all, y_all, sprev_all], halos=[(proj, 8)],
                    res=[conv_w, dt_bias, a_log, d_rep, norm_w],
                    out_tiled=[_sds((t, SW + D)), _sds((t, 128))],
                    out_acc=[_sds((1, SW)), _sds((1, SW)), _sds((8, D)), _sds((1, D)), _sds((1, 128)), _sds((1, 128))],
                    scratch=[pltpu.VMEM((nb, NS, SW), F32), pltpu.VMEM((nb, 8, D), F32)], reverse=True, batch=nb)


def _gelu(y):
    k = math.sqrt(2.0 / math.pi)
    return 0.5 * y * (1.0 + jnp.tanh(k * (y + 0.044715 * y * y * y)))


def _gelu_grad(y):
    k = math.sqrt(2.0 / math.pi)
    th = jnp.tanh(k * (y + 0.044715 * y * y * y))
    return 0.5 * (1.0 + th) + 0.5 * y * (1.0 - th * th) * k * (1.0 + 3.0 * 0.044715 * y * y)


S5T = 256


def _cmul_add(xr, xi, ar, ai, sr, si):
    return xr + ar * sr - ai * si, xi + ar * si + ai * sr


def _s5_fwd(name, proj, bbd, cbd, pw, tab, d5, w_glu, b_glu, seq):
    t = proj.shape[0]
    tm = S5T

    def fn(ctx, tiled, halos, exs, res, outs, accs, exaccs, scr):
        proj_ref, = tiled
        bbd_ref, cbd_ref, pw_ref, tab_ref, d_ref, wg_ref, bg_ref = res
        out_ref, xst_ref, y_ref, xb_ref, ub_ref = outs
        carry_ref, = scr

        @pl.when(ctx.seq_first)
        def _():
            carry_ref[...] = jnp.zeros(carry_ref.shape, F32)

        u = proj_ref[:, 1536:2048]
        bu = _dot(_mx(u), bbd_ref[...])
        xr, xi = bu[:, :S5L].reshape(tm // 8, 8, S5L), bu[:, S5L:].reshape(tm // 8, 8, S5L)
        for k, sh in enumerate((1, 2, 4)):
            xr, xi = _cmul_add(xr, xi, pw_ref[k, :, :S5L], pw_ref[k, :, S5L:], pltpu.roll(xr, sh, 1), pltpu.roll(xi, sh, 1))
        xst_ref[:, :S5L] = xr.reshape(tm, S5L)
        xst_ref[:, S5L:] = xi.reshape(tm, S5L)

        def tile_fix(i, c):
            cr, ci = c
            rows = pl.ds(pl.multiple_of(i * 8, 8), 8)
            tr, ti = _cmul_add(xst_ref[rows, :S5L], xst_ref[rows, S5L:], tab_ref[:, :S5L], tab_ref[:, S5L:], cr, ci)
            xst_ref[rows, :S5L] = tr
            xst_ref[rows, S5L:] = ti
            return tr[7:8], ti[7:8]

        cr, ci = lax.fori_loop(0, tm // 8, tile_fix, (carry_ref[0:1, :S5L], carry_ref[0:1, S5L:]))
        carry_ref[0:1, :S5L] = cr
        carry_ref[0:1, S5L:] = ci
        xb = _mx(xst_ref[...])
        xb_ref[...] = xb
        ub_ref[...] = _mx(u)
        y = _dot(xb, cbd_ref[...]) + u * d_ref[...]
        y_ref[...] = y
        g = _gelu(y)
        v = _dot(_mx(g), wg_ref[...]) + bg_ref[...]
        out_ref[...] = _mx(g * _sigmoid(v))

    return _rowcall(name, fn, t, tm, seq // tm, tiled=[proj], res=[bbd, cbd, pw, tab, d5, w_glu, b_glu],
                    out_tiled=[_sds((t, SW), MXU_DTYPE), _sds((t, 2 * S5L)), _sds((t, SW)),
                               _sds((t, 2 * S5L), MXU_DTYPE), _sds((t, SW), MXU_DTYPE)],
                    scratch=[pltpu.VMEM((8, 2 * S5L), F32)])


def _s5_bwd(name, dout, proj, xst, y_all, bbdt, cbdt, pwc, tabc, d5, w_glu, w_glut, b_glu, seq):
    t = proj.shape[0]
    tm = S5T

    def fn(ctx, tiled, halos, exs, res, outs, accs, exaccs, scr):
        dout_ref, proj_ref, xst_ref, y_ref = tiled
        halo_ref, = halos
        bbdt_ref, cbdt_ref, pw_ref, tab_ref, d_ref, wg_ref, wgt_ref, bg_ref = res
        du_ref, lam_ref, dyb_ref, gb_ref, dvb_ref = outs
        da_acc, dd_acc, dbg_acc = accs
        carry_ref, lamf_ref = scr

        @pl.when(ctx.seq_last)
        def _():
            carry_ref[...] = jnp.zeros(carry_ref.shape, F32)

        u = proj_ref[:, 1536:2048]
        y = y_ref[...]
        g = _gelu(y)
        v = _dot(_mx(g), wg_ref[...]) + bg_ref[...]
        sg = _sigmoid(v)
        dout = dout_ref[...]
        dv = dout * g * sg * (1.0 - sg)
        dvb = _mx(dv)
        dvb_ref[...] = dvb
        gb_ref[...] = _mx(g)
        _acc(dbg_acc, jnp.sum(dv, axis=0, keepdims=True), ctx.first)
        dg = dout * sg + _dot(dvb, wgt_ref[...])
        dy = dg * _gelu_grad(y)
        dyb = _mx(dy)
        dyb_ref[...] = dyb
        _acc(dd_acc, jnp.sum(dy * u, axis=0, keepdims=True), ctx.first)
        dx = _dot(dyb, cbdt_ref[...])
        xr, xi = dx[:, :S5L].reshape(tm // 8, 8, S5L), dx[:, S5L:].reshape(tm // 8, 8, S5L)
        for k, sh in enumerate((1, 2, 4)):
            xr, xi = _cmul_add(xr, xi, pw_ref[k, :, :S5L], pw_ref[k, :, S5L:], pltpu.roll(xr, 8 - sh, 1),
                               pltpu.roll(xi, 8 - sh, 1))
        lamf_ref[:, :S5L] = xr.reshape(tm, S5L)
        lamf_ref[:, S5L:] = xi.reshape(tm, S5L)

        def tile_fix(i, c):
            cr, ci = c
            rows = pl.ds(pl.multiple_of((tm // 8 - 1 - i) * 8, 8), 8)
            tr, ti = _cmul_add(lamf_ref[rows, :S5L], lamf_ref[rows, S5L:], tab_ref[:, :S5L], tab_ref[:, S5L:], cr, ci)
            lamf_ref[rows, :S5L] = tr
            lamf_ref[rows, S5L:] = ti
            return tr[0:1], ti[0:1]

        cr, ci = lax.fori_loop(0, tm // 8, tile_fix, (carry_ref[0:1, :S5L], carry_ref[0:1, S5L:]))
        carry_ref[0:1, :S5L] = cr
        carry_ref[0:1, S5L:] = ci
        lam = lamf_ref[...]
        lamb = _mx(lam)
        lam_ref[...] = lamb
        du_ref[...] = dy * d_ref[...] + _dot(lamb, bbdt_ref[...])
        prev8 = halo_ref[...] * jnp.where(ctx.seq_first, 0.0, 1.0)
        xprev = _shift_rows(xst_ref[...], prev8, 1)
        lr, li = lam[:, :S5L], lam[:, S5L:]
        pr, pi = xprev[:, :S5L], xprev[:, S5L:]
        dar = jnp.sum(lr * pr + li * pi, axis=0, keepdims=True)
        dai = jnp.sum(li * pr - lr * pi, axis=0, keepdims=True)
        _acc(da_acc, jnp.concatenate([dar, dai], axis=1), ctx.first)

    return _rowcall(name, fn, t, tm, seq // tm, tiled=[dout, proj, xst, y_all], halos=[(xst, 8)],
                    res=[bbdt, cbdt, pwc, tabc, d5, w_glu, w_glut, b_glu],
                    out_tiled=[_sds((t, SW)), _sds((t, 2 * S5L), MXU_DTYPE), _sds((t, SW), MXU_DTYPE),
                               _sds((t, SW), MXU_DTYPE), _sds((t, SW), MXU_DTYPE)],
                    out_acc=[_sds((1, 2 * S5L)), _sds((1, SW)), _sds((1, SW))],
                    scratch=[pltpu.VMEM((8, 2 * S5L), F32), pltpu.VMEM((tm, 2 * S5L), F32)], reverse=True)


def _out_fwd(name, yssd, ys5, x1, mod, w_out, lng, lnb, seq):
    t = x1.shape[0]
    tm = 256

    def fn(ctx, tiled, halos, exs, res, outs, accs, exaccs, scr):
        ya_ref, yb_ref, x_ref = tiled
        mod_ref, = exs
        w_ref, g_ref, b_ref = res
        m = _dot(ya_ref[...], w_ref[0:SW, :]) + _dot(yb_ref[...], w_ref[SW:2 * SW, :])
        r = ALPHA * x_ref[...] + mod_ref[0, 5:6, :] * m
        outs[0][...] = _ln_fwd(r, g_ref[...], b_ref[...])
        outs[1][...] = r
        outs[2][...] = m

    return _rowcall(name, fn, t, tm, seq // tm, tiled=[yssd, ys5, x1], exs=[mod], res=[w_out, lng, lnb],
                    out_tiled=[_sds((t, D)), _sds((t, D)), _sds((t, D))])


def _out_bwd(name, dxo, r, m, mod, lng, w_outt, seq):
    t = dxo.shape[0]
    tm = 256

    def fn(ctx, tiled, halos, exs, res, outs, accs, exaccs, scr):
        dxo_ref, r_ref, m_ref = tiled
        mod_ref, = exs
        g_ref, wt_ref = res
        dr, dgam, dbet = _ln_bwd(r_ref[...], g_ref[...], dxo_ref[...])
        outs[0][...] = dr
        _acc(accs[0], dgam, ctx.first)
        _acc(accs[1], dbet, ctx.first)
        _acc(exaccs[0].at[0], jnp.sum(dr * m_ref[...], axis=0, keepdims=True), ctx.ex_enter)
        dm = _mx(mod_ref[0, 5:6, :] * dr)
        outs[1][...] = dm
        dyc = _dot(dm, wt_ref[...])
        outs[2][...] = dyc[:, 0:SW]
        outs[3][...] = dyc[:, SW:2 * SW]

    b = mod.shape[0]
    return _rowcall(name, fn, t, tm, seq // tm, tiled=[dxo, r, m], exs=[mod], res=[lng, w_outt],
                    out_tiled=[_sds((t, D)), _sds((t, D), MXU_DTYPE), _sds((t, SW)), _sds((t, SW))],
                    out_acc=[_sds((1, D)), _sds((1, D))], out_exacc=[_sds((b, 1, D))])


def _s5_discretise(a_re, a_im, log_dt, b_re, b_im):
    dt = jnp.exp(log_dt)[:, None]
    mag = jnp.exp(dt * a_re)
    ab_re, ab_im = mag * jnp.cos(dt * a_im), mag * jnp.sin(dt * a_im)
    den = a_re * a_re + a_im * a_im
    nr, ni = ab_re - 1.0, ab_im
    f_re, f_im = (nr * a_re + ni * a_im) / den, (ni * a_re - nr * a_im) / den
    bb_re = f_re[..., None] * b_re - f_im[..., None] * b_im
    bb_im = f_re[..., None] * b_im + f_im[..., None] * b_re
    return ab_re, ab_im, bb_re, bb_im


def _s5_tables(ab_re, ab_im):
    ar, ai = ab_re.reshape(1, S5L), ab_im.reshape(1, S5L)
    pows = [(ar, ai)]
    for _ in range(7):
        pr, pi = pows[-1]
        pows.append((pr * ar - pi * ai, pr * ai + pi * ar))

    def pack(rows, sign):
        return jnp.concatenate([jnp.concatenate([r for r, _ in rows], axis=0),
                                jnp.concatenate([sign * i for _, i in rows], axis=0)], axis=1)

    row = jnp.arange(8)[:, None]
    pw = jnp.stack([jnp.where(row >= sh, pack([pows[sh - 1]], 1.0), 0.0) for sh in (1, 2, 4)])
    pwc = jnp.stack([jnp.where(row < 8 - sh, pack([pows[sh - 1]], -1.0), 0.0) for sh in (1, 2, 4)])
    tab = pack(pows, 1.0)
    tabc = pack(pows[::-1], -1.0)
    return pw, tab, pwc, tabc


def _local_step(x, tgt, mod, w, sp, seq):
    t = x.shape[0]
    mxu = MXU_DTYPE
    x1, r1, h1, ab1, f1 = _ffn_fwd("ffn1_fwd", x, mod, 0, w["ffn1_w13"], w["ffn1_w2"], sp["ln1_g"], sp["ln1_b"], seq)
    h2, proj = _proj_fwd("proj_fwd", x1, mod, w["w_in"], seq)
    yssd, xpre, yraw, sprev = _ssd_fwd("ssd_fwd", proj, sp["conv_w"], sp["conv_b"], sp["dt_bias"], sp["a_log"],
                                       sp["d_rep"], sp["ssd_norm_w"], seq)
    (ab_re, ab_im, bb_re, bb_im), disc_vjp = jax.vjp(_s5_discretise, sp["s5_a_re"], sp["s5_a_im"], sp["s5_log_dt"],
                                                     sp["s5_b_re"], sp["s5_b_im"])
    eye = jnp.eye(S5G, dtype=F32)
    bbd = jnp.concatenate([jnp.einsum("gk,gph->ghkp", eye, bb_re).reshape(SW, S5L),
                           jnp.einsum("gk,gph->ghkp", eye, bb_im).reshape(SW, S5L)], axis=1).astype(mxu)
    cbd = jnp.concatenate([jnp.einsum("gk,ghp->gpkh", eye, sp["s5_c_re"]).reshape(S5L, SW),
                           -jnp.einsum("gk,ghp->gpkh", eye, sp["s5_c_im"]).reshape(S5L, SW)], axis=0).astype(mxu)
    pw, tab, pwc, tabc = _s5_tables(lax.stop_gradient(ab_re), lax.stop_gradient(ab_im))
    ys5, xst, y5, xstb, ub = _s5_fwd("s5_fwd", proj, bbd, cbd, pw, tab, sp["s5_d"], w["w_glu"], sp["b_glu"], seq)
    x2, r2, m2 = _out_fwd("out_fwd", yssd, ys5, x1, mod, w["w_out"], sp["ln2_g"], sp["ln2_b"], seq)
    x3, r3, h3, ab3, f3, dy3, loss_acc = _ffn_fwd("ffn2_fwd", x2, mod, 6, w["ffn2_w13"], w["ffn2_w2"], sp["ln3_g"],
                                                  sp["ln3_b"], seq, tgt=tgt)
    dr3, df3, s3, dab3, dg3g, dg3b, dgate3 = _ffn_bwd1("ffn2_bwd1", dy3, r3, ab3, f3, mod, 6, sp["ln3_g"],
                                                        w["ffn2_w2t"], seq)
    dx2, dsh3, dsc3 = _mod_bwd("ffn2_bwd2", dab3, dr3, x2, mod, 6, w["ffn2_w13t"], seq)
    g_ffn2_w13 = _tn_matmul("ffn2_dw13", h3, dab3, D, FB)
    g_ffn2_w2 = _tn_matmul("ffn2_dw2", s3, df3, FB, D)
    dr2, dm2, dyssd, dys5, dg2g, dg2b, dgate2 = _out_bwd("out_bwd", dx2, r2, m2, mod, sp["ln2_g"], w["w_outt"], seq)
    g_w_out = jnp.concatenate([_tn_matmul("dw_out_a", yssd, dm2, SW, D), _tn_matmul("dw_out_b", ys5, dm2, SW, D)], axis=0)
    du, lam, dy5b, g5b, dv5b, da5, dd5, dbglu = _s5_bwd("s5_bwd", dys5, proj, xst, y5, bbd.T, cbd.T, pwc, tabc,
                                                       sp["s5_d"], w["w_glu"], w["w_glut"], sp["b_glu"], seq)
    g_w_glu = _tn_matmul("dw_glu", g5b, dv5b, SW, SW)
    dbfull = _tn_matmul("s5_db", lam, ub, D, SW)
    dcfull = _tn_matmul("s5_dc", xstb, dy5b, D, SW)
    dzx, ddt, dnw, ddl, dcw, dcb, ddtb, dalog = _ssd_bwd("ssd_bwd", dyssd, proj, xpre, yraw, sprev, sp["conv_w"],
                                                         sp["dt_bias"], sp["a_log"], sp["d_rep"], sp["ssd_norm_w"], seq)
    dx1, dproj, dsh2, dsc2 = _mod_bwd("proj_bwd", dzx, dr2, x1, mod, 3, w["w_int"], seq, extra=(du, ddt))
    g_w_in = _tn_matmul("dw_in", h2, dproj, D, PW)
    dr1, df1, s1, dab1, dg1g, dg1b, dgate1 = _ffn_bwd1("ffn1_bwd1", dx1, r1, ab1, f1, mod, 0, sp["ln1_g"],
                                                        w["ffn1_w2t"], seq)
    dx0, dsh1, dsc1 = _mod_bwd("ffn1_bwd2", dab1, dr1, x, mod, 0, w["ffn1_w13t"], seq)
    g_ffn1_w13 = _tn_matmul("ffn1_dw13", h1, dab1, D, FB)
    g_ffn1_w2 = _tn_matmul("ffn1_dw2", s1, df1, FB, D)
    dmod = jnp.concatenate([dsh1, dsc1, dgate1, dsh2, dsc2, dgate2, dsh3, dsc3, dgate3], axis=1)
    d4 = lambda a: a.reshape(S5G, S5P, S5G, S5H)
    dbb_re = jnp.einsum("gpgh->gph", d4(dbfull[:S5L]))
    dbb_im = jnp.einsum("gpgh->gph", d4(dbfull[S5L:]))
    dc_re = jnp.einsum("gpgh->ghp", d4(dcfull[:S5L]))
    dc_im = -jnp.einsum("gpgh->ghp", d4(dcfull[S5L:]))
    g_a_re, g_a_im, g_log_dt, g_b_re, g_b_im = disc_vjp(
        (da5[:, :S5L].reshape(S5G, S5P), da5[:, S5L:].reshape(S5G, S5P), dbb_re, dbb_im))
    small = dict(ln1_g=dg1g, ln1_b=dg1b, ln2_g=dg2g, ln2_b=dg2b, ln3_g=dg3g, ln3_b=dg3b, conv_w=dcw[0:4], conv_b=dcb,
                 dt_bias=ddtb[:, :NH], a_log=dalog[:, :NH], d_ssd=jnp.sum(ddl.reshape(NH, HP), axis=1).reshape(1, NH),
                 ssd_norm_w=dnw, s5_a_re=g_a_re, s5_a_im=g_a_im, s5_log_dt=g_log_dt, s5_b_re=g_b_re, s5_b_im=g_b_im,
                 s5_c_re=dc_re, s5_c_im=dc_im, s5_d=dd5, w_glu_b=dbglu)
    big = dict(ffn1_w13=g_ffn1_w13, ffn1_w2=g_ffn1_w2, w_in=g_w_in, w_glu=g_w_glu, w_out=g_w_out,
               ffn2_w13=g_ffn2_w13, ffn2_w2=g_ffn2_w2)
    return loss_acc[0, 0], dx0, dmod, big, small


def _place():
    return lax.axis_index("x"), lax.axis_index("y"), lax.axis_index("c")


def _other_chips(x, y):
    return [(1 - x, y), (x, 1 - y), (1 - x, 1 - y)]


def _allgather8(name, a):
    r, n = a.shape

    def body(x_ref, out_ref, send_sems, recv_sems, local_sem):
        x, y, c = _place()
        me, sibling = (x, y, c), (x, y, 1 - c)
        chips = _other_chips(x, y)

        def rows(px, py, pc):
            return out_ref.at[pl.ds(pl.multiple_of((4 * px + 2 * py + pc) * r, 8), r), :]

        def copy(k, block, to, src=None):
            return pltpu.make_async_remote_copy(src_ref=rows(*block) if src is None else src, dst_ref=rows(*block),
                                                send_sem=send_sems.at[k], recv_sem=recv_sems.at[k], device_id=to,
                                                device_id_type=MESH_T)

        mine = pltpu.make_async_copy(x_ref, rows(*me), local_sem)
        mine.start()
        first = [copy(0, me, sibling, src=x_ref)]
        first += [copy(1 + j, me, (*chip, c), src=x_ref) for j, chip in enumerate(chips)]
        for cp in first:
            cp.start()
        passed = [copy(4 + j, (*chip, c), sibling) for j, chip in enumerate(chips)]
        for j, chip in enumerate(chips):
            copy(1 + j, (*chip, c), me).wait_recv()
            passed[j].start()
        copy(0, sibling, me).wait_recv()
        for j, chip in enumerate(chips):
            copy(4 + j, (*chip, 1 - c), me).wait_recv()
        for cp in first + passed:
            cp.wait_send()
        mine.wait()

    out = _pcall(body, name=name, out_shape=_sds((8 * r, n), a.dtype),
                 in_specs=[pl.BlockSpec(memory_space=pltpu.VMEM)], out_specs=pl.BlockSpec(memory_space=pltpu.VMEM),
                 scratch_shapes=[pltpu.SemaphoreType.DMA((7,)), pltpu.SemaphoreType.DMA((7,)), pltpu.SemaphoreType.DMA],
                 compiler_params=_cparams())(a)
    return out.reshape(8, r, n)


def _hbm_call(body, name, ins, out_shape, n_sems):
    any_spec = pl.BlockSpec(memory_space=pl.ANY)
    scratch = [pltpu.SemaphoreType.DMA((n,)) for n in n_sems]
    return _pcall(body, name=name, out_shape=out_shape, in_specs=[any_spec] * len(ins), out_specs=[any_spec] * len(out_shape),
                  scratch_shapes=scratch, compiler_params=_cparams())(*ins)


def _rows(ref_rows, half, align):
    hr = ref_rows // 2
    return pl.ds(pl.multiple_of(half * hr, align), hr)


def _gather_weights(name, shards):
    n = len(shards)

    def body(*refs):
        s_refs, o_refs = refs[:n], refs[n:2 * n]
        send_sems, recv_sems = refs[2 * n:]
        x, y, c = _place()
        sibling = (x, y, 1 - c)
        chips = _other_chips(x, y)

        def copy(i, k, chip, half, to, src=None):
            dst = o_refs[i].at[2 * chip[0] + chip[1], _rows(shards[i].shape[0], half, 16)]
            return pltpu.make_async_remote_copy(src_ref=dst if src is None else src, dst_ref=dst,
                                                send_sem=send_sems.at[6 * i + k], recv_sem=recv_sems.at[6 * i + k],
                                                device_id=to, device_id_type=MESH_T)

        first = [copy(i, j, (x, y), c, (*chip, c), src=s_refs[i].at[_rows(shards[i].shape[0], c, 16)])
                 for i in range(n) for j, chip in enumerate(chips)]
        for cp in first:
            cp.start()
        passed = []
        for i in range(n):
            for j, chip in enumerate(chips):
                copy(i, j, chip, c, sibling).wait_recv()
                passed.append(copy(i, 3 + j, chip, c, sibling))
                passed[-1].start()
        for i in range(n):
            for j, chip in enumerate(chips):
                copy(i, 3 + j, chip, 1 - c, sibling).wait_recv()
        for cp in first + passed:
            cp.wait_send()

    return _hbm_call(body, name, shards, [_sds((4,) + s.shape, s.dtype) for s in shards], (6 * n, 6 * n))


def _rs_sibling(name, gs):
    n = len(gs)

    def body(*refs):
        g_refs, o_refs = refs[:n], refs[n:2 * n]
        send_sems, recv_sems = refs[2 * n:]
        x, y, c = _place()
        cps = [pltpu.make_async_remote_copy(src_ref=g_refs[i].at[:, _rows(gs[i].shape[1], 1 - c, 8)], dst_ref=o_refs[i],
                                            send_sem=send_sems.at[i], recv_sem=recv_sems.at[i], device_id=(x, y, 1 - c),
                                            device_id_type=MESH_T) for i in range(n)]
        for cp in cps:
            cp.start()
        for cp in cps:
            cp.wait()

    return _hbm_call(body, name, gs, [_sds((4, g.shape[1] // 2, g.shape[2]), g.dtype) for g in gs], (n, n))


def _rs_chips(name, hs):
    n = len(hs)

    def body(*refs):
        h_refs, o_refs = refs[:n], refs[n:2 * n]
        send_sems, recv_sems = refs[2 * n:]
        x, y, c = _place()
        cps = [pltpu.make_async_remote_copy(src_ref=h_refs[i].at[2 * chip[0] + chip[1]], dst_ref=o_refs[i].at[j],
                                            send_sem=send_sems.at[3 * i + j], recv_sem=recv_sems.at[3 * i + j],
                                            device_id=(*chip, c), device_id_type=MESH_T)
               for i in range(n) for j, chip in enumerate(_other_chips(x, y))]
        for cp in cps:
            cp.start()
        for cp in cps:
            cp.wait()

    return _hbm_call(body, name, hs, [_sds((3,) + h.shape[1:], h.dtype) for h in hs], (3 * n, 3 * n))


def _rs_join(name, fs):
    n = len(fs)

    def body(*refs):
        o_refs = refs[n:2 * n]
        send_sems, recv_sems = refs[2 * n:]
        x, y, c = _place()

        def copy(i, half):
            part = o_refs[i].at[_rows(fs[i].shape[0], half, 8)]
            return pltpu.make_async_remote_copy(src_ref=part, dst_ref=part, send_sem=send_sems.at[i],
                                                recv_sem=recv_sems.at[i], device_id=(x, y, 1 - c), device_id_type=MESH_T)

        sends = [copy(i, c) for i in range(n)]
        for cp in sends:
            cp.start()
        for i in range(n):
            copy(i, 1 - c).wait_recv()
        for cp in sends:
            cp.wait_send()

    any_spec = pl.BlockSpec(memory_space=pl.ANY)
    return _pcall(body, name=name, out_shape=[_sds(f.shape, f.dtype) for f in fs], in_specs=[any_spec] * n,
                  out_specs=[any_spec] * n, input_output_aliases={i: i for i in range(n)},
                  scratch_shapes=[pltpu.SemaphoreType.DMA((n,)), pltpu.SemaphoreType.DMA((n,))],
                  compiler_params=_cparams())(*fs)


def _row_block(r, cap=2048):
    b = min(r, cap)
    while r % b or b % 8:
        b -= 8
    return b


RS_SPLIT = 2


def _rs_add(name, gs, r1s, sel):
    n = len(gs)

    def body(sel_ref, *refs):
        g_refs, r_refs, h_refs, b_refs = (refs[k * n:(k + 1) * n] for k in range(4))
        for i in range(n):
            h = g_refs[i][...] + r_refs[i][...]
            h_refs[i][...] = h
            b_refs[i][...] = h.astype(BF16)

    def blk(g):
        return (1, g.shape[1] // 2 // RS_SPLIT, g.shape[2])

    here = lambda k, j, s: (k, j, 0)
    in_specs = [pl.BlockSpec(blk(g), lambda k, j, s: (k, s[0] * RS_SPLIT + j, 0)) for g in gs]
    in_specs += [pl.BlockSpec(blk(g), here) for g in gs]
    outs = _pcall(body, name=name,
                  out_shape=[_sds((4, g.shape[1] // 2, g.shape[2])) for g in gs]
                  + [_sds((4, g.shape[1] // 2, g.shape[2]), BF16) for g in gs],
                  grid_spec=pltpu.PrefetchScalarGridSpec(num_scalar_prefetch=1, grid=(4, RS_SPLIT), in_specs=in_specs,
                                                         out_specs=[pl.BlockSpec(blk(g), here) for g in gs] * 2),
                  compiler_params=_cparams(("parallel", "parallel")))(sel.reshape(1).astype(jnp.int32), *gs, *r1s)
    return outs[:n], outs[n:]


def _rs_sum(name, hs, r2s, chip, half):
    n = len(hs)

    def body(sel_ref, *refs):
        h_refs, r_refs, o_refs = (refs[k * n:(k + 1) * n] for k in range(3))
        for i in range(n):
            r = r_refs[i]
            o_refs[i][...] = ((h_refs[i][0] + r[0].astype(F32)) + r[1].astype(F32)) + r[2].astype(F32)

    def rows(h):
        return h.shape[1] // RS_SPLIT

    in_specs = [pl.BlockSpec((1, rows(h), h.shape[2]), lambda j, s: (s[0], j, 0)) for h in hs]
    in_specs += [pl.BlockSpec((3, rows(h), h.shape[2]), lambda j, s: (0, j, 0)) for h in hs]
    sel = jnp.stack([chip, half]).astype(jnp.int32)
    return _pcall(body, name=name, out_shape=[_sds((2 * h.shape[1], h.shape[2])) for h in hs],
                  grid_spec=pltpu.PrefetchScalarGridSpec(
                      num_scalar_prefetch=1, grid=(RS_SPLIT,), in_specs=in_specs,
                      out_specs=[pl.BlockSpec((rows(h), h.shape[2]), lambda j, s: (s[1] * RS_SPLIT + j, 0)) for h in hs]),
                  compiler_params=_cparams(("parallel",)))(sel, *hs, *r2s)


def _sum8(name, a):
    _, r, n = a.shape
    br = _row_block(r)

    def body(a_ref, o_ref):
        acc = a_ref[0]
        for k in range(1, 8):
            acc = acc + a_ref[k]
        o_ref[...] = acc

    return _pcall(body, name=name, out_shape=_sds((r, n)), grid=(r // br,),
                  in_specs=[pl.BlockSpec((8, br, n), lambda j: (0, j, 0))], out_specs=pl.BlockSpec((br, n), lambda j: (j, 0)),
                  compiler_params=_cparams(("parallel",)))(a)


def _adamw(name, ws, gs, ms, vs, nblk):
    n = len(ws)

    def body(*refs):
        w_refs, g_refs, m_refs, v_refs, d_refs, nm_refs, nv_refs = (refs[k * n:(k + 1) * n] for k in range(7))
        for i in range(n):
            gv = g_refs[i][...]
            nm = ADAM_B1 * m_refs[i][...] + (1.0 - ADAM_B1) * gv
            nv = ADAM_B2 * v_refs[i][...] + (1.0 - ADAM_B2) * (gv * gv)
            nm_refs[i][...] = nm
            nv_refs[i][...] = nv
            m_hat = nm / (1.0 - ADAM_B1 ** ADAM_STEP)
            v_hat = nv / (1.0 - ADAM_B2 ** ADAM_STEP)
            d_refs[i][...] = -ADAM_LR * (m_hat / (jnp.sqrt(v_hat) + ADAM_EPS) + ADAM_WD * w_refs[i][...])

    specs = [pl.BlockSpec((w.shape[0] // nblk, w.shape[1]), lambda j: (j, 0)) for w in ws]
    outs = _pcall(body, name=name, out_shape=[_sds(w.shape) for w in ws] * 3, grid=(nblk,), in_specs=specs * 4,
                  out_specs=specs * 3, compiler_params=_cparams(("parallel",)))(*ws, *gs, *ms, *vs)
    return outs[:n], outs[n:2 * n], outs[2 * n:]


ADA_COLS = 2304
ADA_BLK = 768


def _ada_fwd(name, c_all, w_shard, b_cols):
    nb = c_all.shape[0]

    def body(c_ref, w_ref, b_ref, o_ref):
        cv = c_ref[...]
        cs = _mx(cv * _sigmoid(cv))
        o_ref[...] = _dot(cs, _mx(w_ref[...])) + b_ref[...]

    return _pcall(body, name=name, out_shape=_sds((nb, ADA_COLS)), grid=(ADA_COLS // ADA_BLK,),
                  in_specs=[pl.BlockSpec((nb, D), lambda j: (0, 0)), pl.BlockSpec((D, ADA_BLK), lambda j: (0, j)),
                            pl.BlockSpec((1, ADA_BLK), lambda j: (0, j))],
                  out_specs=pl.BlockSpec((nb, ADA_BLK), lambda j: (0, j)),
                  compiler_params=_cparams(("parallel",)))(c_all, w_shard, b_cols)


def _ada_bwd(name, c_all, dmod_cols, dmod_all):
    nb = c_all.shape[0]

    def body(c_ref, dc_ref, da_ref, gw_ref, gb_ref):
        cv = c_ref[...]
        cs = _mx(cv * _sigmoid(cv))
        gw_ref[...] = lax.dot_general(cs, _mx(dc_ref[...]), (((0,), (0,)), ((), ())), preferred_element_type=F32)

        @pl.when(pl.program_id(0) == 0)
        def _():
            gb_ref[...] = jnp.sum(da_ref[...], axis=0, keepdims=True)

    return _pcall(body, name=name, out_shape=[_sds((D, ADA_COLS)), _sds((1, 9 * D))], grid=(ADA_COLS // ADA_BLK,),
                  in_specs=[pl.BlockSpec((nb, D), lambda j: (0, 0)), pl.BlockSpec((nb, ADA_BLK), lambda j: (0, j)),
                            pl.BlockSpec((nb, 9 * D), lambda j: (0, 0))],
                  out_specs=[pl.BlockSpec((D, ADA_BLK), lambda j: (0, j)), pl.BlockSpec((1, 9 * D), lambda j: (0, 0))],
                  compiler_params=_cparams(("arbitrary",)))(c_all, dmod_cols, dmod_all)


BIG = ("ffn1_w1", "ffn1_w3", "ffn1_w2", "w_in", "w_glu", "w_out", "ffn2_w1", "ffn2_w3", "ffn2_w2")
COL_SHARDED = ("ffn1_w1", "ffn1_w3", "w_in", "ffn2_w1", "ffn2_w3")
SMALL = ("b_ada", "ln1_g", "ln1_b", "conv_w", "conv_b", "dt_bias", "a_log", "d_ssd", "ssd_norm_w", "s5_a_re", "s5_a_im",
         "s5_log_dt", "s5_b_re", "s5_b_im", "s5_c_re", "s5_c_im", "s5_d", "b_glu", "ln2_g", "ln2_b", "ln3_g", "ln3_b")
WEIGHTS = ("w_ada", "b_ada", "ffn1_w1", "ffn1_w3", "ffn1_w2", "ln1_g", "ln1_b", "w_in", "conv_w", "conv_b", "dt_bias",
           "a_log", "d_ssd", "ssd_norm_w", "s5_a_re", "s5_a_im", "s5_log_dt", "s5_b_re", "s5_b_im", "s5_c_re", "s5_c_im",
           "s5_d", "w_glu", "b_glu", "w_out", "ln2_g", "ln2_b", "ffn2_w1", "ffn2_w3", "ffn2_w2", "ln3_g", "ln3_b")
BIG_PAD = 2 * 1024 * 128


def _pack(arrs, mult, axis_keep=0):
    lead = arrs[0].shape[:axis_keep]
    flat = jnp.concatenate([a.reshape(lead + (-1,)) for a in arrs], axis=axis_keep)
    pad = (-flat.shape[-1]) % mult
    if pad:
        flat = jnp.concatenate([flat, jnp.zeros(lead + (pad,), flat.dtype)], axis=axis_keep)
    return flat


def _unpack(flat, shapes):
    out, off = [], 0
    for s in shapes:
        size = math.prod(s)
        out.append(flat[..., off:off + size].reshape(flat.shape[:-1] + tuple(s)))
        off += size
    return out


def _shard_major(a):
    rows, cols = a.shape
    return a.reshape(rows, 4, cols // 4).transpose(1, 0, 2)


def _from_shard_major(a):
    _, rows, w = a.shape
    return a.transpose(1, 0, 2).reshape(rows, 4 * w)


def kernel(x, c, w_ada, b_ada, ffn1_w1, ffn1_w3, ffn1_w2, ln1_g, ln1_b, w_in, conv_w, conv_b, dt_bias, a_log, d_ssd, ssd_norm_w, s5_a_re, s5_a_im, s5_log_dt, s5_b_re, s5_b_im, s5_c_re, s5_c_im, s5_d, w_glu, b_glu, w_out, ln2_g, ln2_b, ffn2_w1, ffn2_w3, ffn2_w2, ln3_g, ln3_b, loss_target, m_w_ada, m_b_ada, m_ffn1_w1, m_ffn1_w3, m_ffn1_w2, m_ln1_g, m_ln1_b, m_w_in, m_conv_w, m_conv_b, m_dt_bias, m_a_log, m_d_ssd, m_ssd_norm_w, m_s5_a_re, m_s5_a_im, m_s5_log_dt, m_s5_b_re, m_s5_b_im, m_s5_c_re, m_s5_c_im, m_s5_d, m_w_glu, m_b_glu, m_w_out, m_ln2_g, m_ln2_b, m_ffn2_w1, m_ffn2_w3, m_ffn2_w2, m_ln3_g, m_ln3_b, v_w_ada, v_b_ada, v_ffn1_w1, v_ffn1_w3, v_ffn1_w2, v_ln1_g, v_ln1_b, v_w_in, v_conv_w, v_conv_b, v_dt_bias, v_a_log, v_d_ssd, v_ssd_norm_w, v_s5_a_re, v_s5_a_im, v_s5_log_dt, v_s5_b_re, v_s5_b_im, v_s5_c_re, v_s5_c_im, v_s5_d, v_w_glu, v_b_glu, v_w_out, v_ln2_g, v_ln2_b, v_ffn2_w1, v_ffn2_w3, v_ffn2_w2, v_ln3_g, v_ln3_b):
    a = dict(locals())
    xi, yi, ci = _place()
    chip = 2 * xi + yi
    dev = 2 * chip + ci
    nb, seq, _ = x.shape
    t = nb * seq
    ndev = 8

    c_all = _allgather8("gather_c", c.reshape(-1, 128)).reshape(ndev * nb, D)
    b_cols = lax.dynamic_slice(b_ada, (0, chip * ADA_COLS), (1, ADA_COLS))
    mod_part = _ada_fwd("ada_fwd", c_all, w_ada[0], b_cols)
    mod_parts = _allgather8("gather_mod", mod_part.reshape(-1, 128)).reshape(ndev, ndev * nb, ADA_COLS)
    mod_all = mod_parts[0::2].transpose(1, 0, 2).reshape(ndev * nb, 9 * D)
    mod = lax.dynamic_slice(mod_all, (nb * dev, 0), (nb, 9 * D)).reshape(nb, 9, D)

    own = [a[n][0].astype(MXU_DTYPE) for n in BIG]
    full = {}
    for n, mine, piece in zip(BIG, own, _gather_weights("gather_w", own)):
        piece = lax.dynamic_update_slice(piece, mine[None], (chip, 0, 0))
        full[n] = _from_shard_major(piece) if n in COL_SHARDED else piece.reshape(-1, piece.shape[-1])
    wi = full["w_in"]
    w_inp = jnp.concatenate([wi[:, :1536], wi[:, 1544:2056], wi[:, 1536:1544], jnp.zeros((D, 120), wi.dtype)], axis=1)
    w = dict(ffn1_w13=jnp.concatenate([full["ffn1_w1"], full["ffn1_w3"]], axis=1), ffn1_w2=full["ffn1_w2"],
             ffn2_w13=jnp.concatenate([full["ffn2_w1"], full["ffn2_w3"]], axis=1), ffn2_w2=full["ffn2_w2"],
             w_in=w_inp, w_glu=full["w_glu"], w_out=full["w_out"])
    for n in ("ffn1_w13", "ffn1_w2", "ffn2_w13", "ffn2_w2", "w_in", "w_out"):
        w[n + "t"] = w[n].T
    w["w_glut"] = w["w_glu"].T

    cw_all = _allgather8("gather_conv_w", _pack([conv_w[0]], 1024).reshape(-1, 128)).reshape(ndev, -1)[0::2, :4 * 256]
    conv_full = _from_shard_major(cw_all.reshape(4, 4, 256))
    pad8 = lambda v: jnp.concatenate([v.reshape(1, NH), jnp.zeros((1, 128 - NH), F32)], axis=1)
    sp = dict(ln1_g=ln1_g, ln1_b=ln1_b, ln2_g=ln2_g, ln2_b=ln2_b, ln3_g=ln3_g, ln3_b=ln3_b, conv_w=conv_full,
              conv_b=conv_b, dt_bias=pad8(dt_bias), a_log=pad8(a_log), d_rep=jnp.repeat(d_ssd[0], HP)[None],
              ssd_norm_w=ssd_norm_w, s5_a_re=s5_a_re[0], s5_a_im=s5_a_im[0], s5_log_dt=s5_log_dt[0], s5_b_re=s5_b_re[0],
              s5_b_im=s5_b_im[0], s5_c_re=s5_c_re[0], s5_c_im=s5_c_im[0], s5_d=s5_d, b_glu=b_glu)

    lsum, dx0, dmod, big, small = _local_step(x.reshape(t, D), loss_target.reshape(t, D), mod, w, sp, seq)
    loss = lax.psum(lsum * (0.5 / D), ("x", "y", "c"))

    dmod_all = _allgather8("gather_dmod", dmod.reshape(-1, 128)).reshape(ndev * nb, 9 * D)
    dmod_cols = lax.dynamic_slice(dmod_all, (0, chip * ADA_COLS), (ndev * nb, ADA_COLS))
    g_w_ada, g_b_ada = _ada_bwd("ada_bwd", c_all, dmod_cols, dmod_all)

    gwi = big["w_in"]
    gfull = dict(ffn1_w1=big["ffn1_w13"][:, :FF], ffn1_w3=big["ffn1_w13"][:, FF:], ffn1_w2=big["ffn1_w2"],
                 w_in=jnp.concatenate([gwi[:, :1536], gwi[:, 2048:2056], gwi[:, 1536:2048]], axis=1),
                 w_glu=big["w_glu"], w_out=big["w_out"],
                 ffn2_w1=big["ffn2_w13"][:, :FF], ffn2_w3=big["ffn2_w13"][:, FF:], ffn2_w2=big["ffn2_w2"])
    gsh = [_shard_major(gfull[n]) if n in COL_SHARDED else gfull[n].reshape((4,) + a[n].shape[1:]) for n in BIG]
    r1 = _rs_sibling("rs_sibling", gsh)
    hsum, hsum_bf = _rs_add("rs_add", gsh, r1, ci)
    r2 = _rs_chips("rs_chips", hsum_bf)
    fhalf = _rs_sum("rs_sum", hsum, r2, chip, ci)
    gbig = dict(zip(BIG, _rs_join("rs_join", fhalf)))
    gbig["w_ada"] = g_w_ada

    outs = {}
    for call, names in (("adamw_a", ("ffn1_w1", "ffn1_w3", "ffn1_w2", "w_in", "w_glu", "w_out")),
                        ("adamw_b", ("ffn2_w1", "ffn2_w3", "ffn2_w2", "w_ada"))):
        res = _adamw(call, [a[n][0] for n in names], [gbig[n] for n in names], [a["m_" + n][0] for n in names],
                     [a["v_" + n][0] for n in names], 8)
        for kind, arrs in zip(("grad", "delta", "new_m", "new_v"), ([gbig[n] for n in names],) + tuple(res)):
            for n, arr in zip(names, arrs):
                outs[kind, n] = arr[None]

    snames = [n for n in SMALL if n != "b_ada"]
    sgrad = dict(small)
    sgrad["b_glu"] = small["w_glu_b"]
    svec = _pack([sgrad[n] for n in snames], 1024).reshape(-1, 128)
    ssum = _sum8("small_sum", _allgather8("gather_small", svec)).reshape(-1)

    def view2d(u):
        s = u.shape[1:]
        return u.reshape((1, s[0]) if len(s) == 1 else (-1, s[-1]))

    vshape = {n: view2d(a[n]).shape for n in SMALL}
    gsm = dict(zip(snames, _unpack(ssum, [vshape[n] if n != "conv_w" else (4, D) for n in snames])))
    gsm["conv_w"] = lax.dynamic_slice(gsm["conv_w"], (0, chip * 256), (4, 256))
    gsm["b_ada"] = g_b_ada
    res = _adamw("adamw_small", [view2d(a[n]) for n in SMALL], [gsm[n] for n in SMALL],
                 [view2d(a["m_" + n]) for n in SMALL], [view2d(a["v_" + n]) for n in SMALL], 1)
    for kind, arrs in zip(("grad", "delta", "new_m", "new_v"), ([gsm[n] for n in SMALL],) + tuple(res)):
        for n, arr in zip(SMALL, arrs):
            outs[kind, n] = arr.reshape(a[n].shape)

    res = [loss, dx0.reshape(nb, seq, D)]
    for kind in ("grad", "delta", "new_m", "new_v"):
        res += [outs[kind, n] for n in WEIGHTS]
    return tuple(res)
```

```python
import functools
import math

import jax
import jax.numpy as jnp
from jax import lax
from jax.experimental import pallas as pl
from jax.experimental.pallas import tpu as pltpu

F32 = jnp.float32
BF16 = jnp.bfloat16
MXU_DTYPE = jnp.bfloat16

D = 1024
FF = 2816
FB = 1408
NH, HP, NS, NG = 8, 64, 128, 2
CH = 128
SW = 512
S5G, S5P, S5H = 32, 64, 16
S5L = S5G * S5P
PW = 2176
ALPHA = 2.0 ** 0.25
LN_EPS = 1e-5
ADAM_LR, ADAM_B1, ADAM_B2, ADAM_EPS, ADAM_WD, ADAM_STEP = 0.001, 0.9, 0.999, 1e-08, 0.01, 10
VMEM_LIMIT = 56 * 1024 * 1024
MESH_T = pl.DeviceIdType.MESH


def _pcall(body, **kw):
    return pl.pallas_call(body, **kw)


def _cparams(sem=None, **kw):
    return pltpu.CompilerParams(dimension_semantics=sem, vmem_limit_bytes=VMEM_LIMIT, **kw)


def _dot(a, b):
    return jnp.dot(a, b, preferred_element_type=F32)


def _dot_nt(a, b):
    return lax.dot_general(a, b, (((1,), (1,)), ((), ())), preferred_element_type=F32)


def _dot_hi(a, b):
    return jnp.dot(a, b, preferred_element_type=F32, precision=lax.Precision.HIGHEST)


def _mx(a):
    return a.astype(MXU_DTYPE)


def _sigmoid(x):
    return 1.0 / (1.0 + jnp.exp(-x))


def _iota(shape, axis):
    return lax.broadcasted_iota(jnp.int32, shape, axis)


def _rowcall(name, fn, n_rows, tm, tpe, *, tiled=(), halos=(), exs=(), res=(), out_tiled=(), out_acc=(),
             out_exacc=(), scratch=(), reverse=False, batch=None, carry=None, carry_mid=0.0):
    if batch:
        n_rows //= batch
    nt = n_rows // tm

    def blk(i):
        return (nt - 1 - i) if reverse else i

    in_specs, args = [], []
    for a in tiled:
        if batch:
            in_specs.append(pl.BlockSpec((batch, tm, a.shape[1]), lambda i: (0, blk(i), 0)))
            args.append(a.reshape(batch, n_rows, a.shape[1]))
            continue
        in_specs.append(pl.BlockSpec((tm, a.shape[1]), lambda i: (blk(i), 0)))
        args.append(a)
    for a, rows in halos:
        r = tm // rows
        if batch:
            in_specs.append(pl.BlockSpec((batch, rows, a.shape[1]), lambda i, r=r: (0, jnp.maximum(blk(i) * r - 1, 0), 0)))
            args.append(a.reshape(batch, n_rows, a.shape[1]))
            continue
        in_specs.append(pl.BlockSpec((rows, a.shape[1]), lambda i, r=r: (jnp.maximum(blk(i) * r - 1, 0), 0)))
        args.append(a)
    for a in exs:
        in_specs.append(pl.BlockSpec((1,) + a.shape[1:], lambda i: (blk(i) // tpe, 0, 0)))
        args.append(a)
    for a in res:
        nd = a.ndim
        in_specs.append(pl.BlockSpec(a.shape, lambda i, nd=nd: (0,) * nd, pipeline_mode=pl.Buffered(1)))
        args.append(a)
    any_spec = pl.BlockSpec(memory_space=pl.ANY)
    st_ins = carry.ins if carry else []
    st_outs = carry.out_shapes if carry else []
    st_sems = carry.sem_shapes() if carry else []
    base_in = len(args)
    in_specs += [any_spec] * len(st_ins)
    args += st_ins
    out_specs, out_shape = [], []
    for s in out_tiled:
        if batch:
            out_specs.append(pl.BlockSpec((batch, tm, s.shape[1]), lambda i: (0, blk(i), 0)))
            out_shape.append(_sds((batch, n_rows, s.shape[1]), s.dtype))
            continue
        out_specs.append(pl.BlockSpec((tm, s.shape[1]), lambda i: (blk(i), 0)))
        out_shape.append(s)
    for s in out_acc:
        nd = len(s.shape)
        out_specs.append(pl.BlockSpec(s.shape, lambda i, nd=nd: (0,) * nd))
        out_shape.append(s)
    for s in out_exacc:
        out_specs.append(pl.BlockSpec((1,) + s.shape[1:], lambda i: (blk(i) // tpe, 0, 0)))
        out_shape.append(s)
    base_out = len(out_shape)
    out_specs += [any_spec] * len(st_outs)
    out_shape += st_outs
    aliases = {base_in + k: base_out + v for k, v in carry.aliases.items()} if carry else {}
    n = [len(tiled), len(halos), len(exs), len(res), len(st_ins), len(out_tiled), len(out_acc), len(out_exacc),
         len(st_outs), len(scratch), len(st_sems)]

    def body(*refs):
        groups, k = [], 0
        for m in n:
            groups.append(refs[k:k + m])
            k += m
        i = pl.program_id(0)
        b = blk(i)

        class ctx:
            first = i == 0
            pos = b % tpe
            seq_first = (b % tpe) == 0
            seq_last = (b % tpe) == tpe - 1
            ex_enter = (i % tpe) == 0

        work = functools.partial(fn, ctx, *groups[0:4], *groups[5:8], groups[9])
        if carry:
            carry.run((groups[4], groups[8], groups[10]), i == 0, i == min(nt - 1, int(carry_mid * nt)), i == nt - 1, work)
        else:
            work()

    outs = _pcall(body, name=name, grid=(nt,), in_specs=in_specs, out_specs=out_specs, out_shape=out_shape,
                  input_output_aliases=aliases, scratch_shapes=list(scratch) + st_sems,
                  compiler_params=_cparams(("arbitrary",)))(*args)
    if batch:
        outs = [o.reshape(batch * n_rows, o.shape[2]) if k < len(out_tiled) else o for k, o in enumerate(outs)]
    return (outs[:base_out], outs[base_out:]) if carry else outs


def _acc(ref, val, first):
    if first is not None:
        @pl.when(first)
        def _():
            ref[...] = jnp.zeros(ref.shape, ref.dtype)
    ref[...] += val


def _sds(shape, dtype=F32):
    return jax.ShapeDtypeStruct(shape, dtype)


def _ln_fwd(r, g, b):
    mu = jnp.mean(r, axis=-1, keepdims=True)
    rc = r - mu
    var = jnp.mean(rc * rc, axis=-1, keepdims=True)
    return rc * lax.rsqrt(var + LN_EPS) * g + b


def _ln_bwd(r, g, dy):
    mu = jnp.mean(r, axis=-1, keepdims=True)
    rc = r - mu
    var = jnp.mean(rc * rc, axis=-1, keepdims=True)
    rstd = lax.rsqrt(var + LN_EPS)
    xhat = rc * rstd
    dxh = dy * g
    dr = rstd * (dxh - jnp.mean(dxh, axis=-1, keepdims=True) - xhat * jnp.mean(dxh * xhat, axis=-1, keepdims=True))
    return dr, jnp.sum(dy * xhat, axis=0, keepdims=True), jnp.sum(dy, axis=0, keepdims=True)


def _ffn_fwd(name, x, mod, k0, w13, w2, lng, lnb, seq, tgt=None, **carry):
    t = x.shape[0]
    tm = 256
    with_loss = tgt is not None

    def fn(ctx, tiled, halos, exs, res, outs, accs, exaccs, scr):
        x_ref = tiled[0]
        mod_ref, = exs
        w13_ref, w2_ref, g_ref, b_ref = res
        xo_ref, r_ref, h_ref, ab_ref, f_ref = outs[:5]
        xv = x_ref[...]
        sh, sc, g = mod_ref[0, k0:k0 + 1, :], mod_ref[0, k0 + 1:k0 + 2, :], mod_ref[0, k0 + 2:k0 + 3, :]
        h = _mx(xv * (1.0 + sc) + sh)
        h_ref[...] = h
        f = jnp.zeros((tm, D), F32)
        for j in range(2):
            a = _dot(h, w13_ref[:, j * FB:(j + 1) * FB])
            b = _dot(h, w13_ref[:, FF + j * FB:FF + (j + 1) * FB])
            ab_ref[:, j * FB:(j + 1) * FB] = a
            ab_ref[:, FF + j * FB:FF + (j + 1) * FB] = b
            s = a * _sigmoid(a) * b
            f = f + _dot(_mx(s), w2_ref[j * FB:(j + 1) * FB, :])
        f_ref[...] = f
        r = ALPHA * xv + 0.5 * g * f
        r_ref[...] = r
        xo = _ln_fwd(r, g_ref[...], b_ref[...])
        xo_ref[...] = xo
        if with_loss:
            e = xo - tiled[1][...]
            outs[5][...] = e * (1.0 / D)
            _acc(accs[0], jnp.sum(e * e) * jnp.ones((8, 128), F32), ctx.first)

    out_tiled = [_sds((t, D)), _sds((t, D)), _sds((t, D), MXU_DTYPE), _sds((t, 2 * FF)), _sds((t, D))]
    if with_loss:
        out_tiled.append(_sds((t, D)))
    return _rowcall(name, fn, t, tm, seq // tm, tiled=[x] + ([tgt] if with_loss else []), exs=[mod],
                    res=[w13, w2, lng, lnb], out_tiled=out_tiled, out_acc=[_sds((8, 128))] if with_loss else [], **carry)


def _ffn_bwd1(name, dxo, r, ab, f, mod, k0, lng, w2t, seq, **carry):
    t = dxo.shape[0]
    tm = 256

    def fn(ctx, tiled, halos, exs, res, outs, accs, exaccs, scr):
        dxo_ref, r_ref, ab_ref, f_ref = tiled
        mod_ref, = exs
        g_ref, w2t_ref = res
        dr_ref, df_ref, s_ref, dab_ref = outs
        g = mod_ref[0, k0 + 2:k0 + 3, :]
        dr, dgam, dbet = _ln_bwd(r_ref[...], g_ref[...], dxo_ref[...])
        dr_ref[...] = dr
        _acc(accs[0], dgam, ctx.first)
        _acc(accs[1], dbet, ctx.first)
        _acc(exaccs[0].at[0], jnp.sum(0.5 * f_ref[...] * dr, axis=0, keepdims=True), ctx.ex_enter)
        df = _mx(0.5 * g * dr)
        df_ref[...] = df
        for j in range(2):
            ds = _dot(df, w2t_ref[:, j * FB:(j + 1) * FB])
            a = ab_ref[:, j * FB:(j + 1) * FB]
            b = ab_ref[:, FF + j * FB:FF + (j + 1) * FB]
            sig = _sigmoid(a)
            silu = a * sig
            s_ref[:, j * FB:(j + 1) * FB] = _mx(silu * b)
            dab_ref[:, j * FB:(j + 1) * FB] = _mx(ds * b * (sig * (1.0 + a * (1.0 - sig))))
            dab_ref[:, FF + j * FB:FF + (j + 1) * FB] = _mx(ds * silu)

    b = mod.shape[0]
    return _rowcall(name, fn, t, tm, seq // tm, tiled=[dxo, r, ab, f], exs=[mod], res=[lng, w2t],
                    out_tiled=[_sds((t, D)), _sds((t, D), MXU_DTYPE), _sds((t, FF), MXU_DTYPE),
                               _sds((t, 2 * FF), MXU_DTYPE)],
                    out_acc=[_sds((1, D)), _sds((1, D))], out_exacc=[_sds((b, 1, D))], **carry)


def _mod_bwd(name, dab, dr, x, mod, k0, wt, seq, extra=(), **carry):
    t = dr.shape[0]
    tm = 256
    nin = 1 + len(extra)
    width = dab.shape[1] + sum(e.shape[1] for e in extra)

    def fn(ctx, tiled, halos, exs, res, outs, accs, exaccs, scr):
        parts = tiled[:nin]
        dr_ref, x_ref = tiled[nin:]
        mod_ref, = exs
        wt_ref, = res
        sc = mod_ref[0, k0 + 1:k0 + 2, :]
        if nin == 1:
            dp = parts[0][...]
        else:
            dp = jnp.concatenate([_mx(p[...]) for p in parts], axis=1)
            outs[1][...] = dp
        dh = _dot(dp, wt_ref[...])
        outs[0][...] = ALPHA * dr_ref[...] + dh * (1.0 + sc)
        _acc(exaccs[0].at[0], jnp.sum(dh, axis=0, keepdims=True), ctx.ex_enter)
        _acc(exaccs[1].at[0], jnp.sum(dh * x_ref[...], axis=0, keepdims=True), ctx.ex_enter)

    b = mod.shape[0]
    out_tiled = [_sds((t, D))] + ([_sds((t, width), MXU_DTYPE)] if nin > 1 else [])
    return _rowcall(name, fn, t, tm, seq // tm, tiled=[dab, *extra, dr, x], exs=[mod], res=[wt],
                    out_tiled=out_tiled, out_exacc=[_sds((b, 1, D)), _sds((b, 1, D))], **carry)


def _tn_matmul(name, a, b, bm, bn, bt=512, carry=None):
    t, m = a.shape
    n = b.shape[1]
    grid = (m // bm, n // bn, t // bt)
    n_in, n_out = (len(carry.ins), len(carry.out_shapes)) if carry else (0, 0)

    def body(a_ref, b_ref, *refs):
        o_ref = refs[n_in]

        def work():
            @pl.when(pl.program_id(2) == 0)
            def _():
                o_ref[...] = jnp.zeros(o_ref.shape, F32)
            o_ref[...] += lax.dot_general(a_ref[...], b_ref[...], (((0,), (0,)), ((), ())), preferred_element_type=F32)

        if not carry:
            return work()
        step = (pl.program_id(0) * grid[1] + pl.program_id(1)) * grid[2] + pl.program_id(2)
        carry.run((refs[:n_in], refs[n_in + 1:n_in + 1 + n_out], refs[n_in + 1 + n_out:]), step == 0, step == 0,
                  step == grid[0] * grid[1] * grid[2] - 1, work)

    any_spec = pl.BlockSpec(memory_space=pl.ANY)
    outs = _pcall(body, name=name, grid=grid,
                  in_specs=[pl.BlockSpec((bt, bm), lambda i, j, k: (k, i)), pl.BlockSpec((bt, bn), lambda i, j, k: (k, j))]
                  + [any_spec] * n_in,
                  out_specs=[pl.BlockSpec((bm, bn), lambda i, j, k: (i, j))] + [any_spec] * n_out,
                  out_shape=[_sds((m, n))] + (carry.out_shapes if carry else []),
                  input_output_aliases={2 + k: 1 + v for k, v in carry.aliases.items()} if carry else {},
                  scratch_shapes=carry.sem_shapes() if carry else [],
                  compiler_params=_cparams(("arbitrary",) * 3 if carry else ("parallel", "parallel", "arbitrary")))(
                      a, b, *(carry.ins if carry else []))
    return (outs[0], outs[1:]) if carry else outs[0]


def _proj_fwd(name, x, mod, w_in, seq):
    t = x.shape[0]
    tm = 256

    def fn(ctx, tiled, halos, exs, res, outs, accs, exaccs, scr):
        mod_ref, = exs
        sh, sc = mod_ref[0, 3:4, :], mod_ref[0, 4:5, :]
        h = _mx(tiled[0][...] * (1.0 + sc) + sh)
        outs[0][...] = h
        outs[1][...] = _dot(h, res[0][...])

    return _rowcall(name, fn, t, tm, seq // tm, tiled=[x], exs=[mod], res=[w_in],
                    out_tiled=[_sds((t, D), MXU_DTYPE), _sds((t, PW))])


def _shift_rows(cur, prev8, j):
    if j == 0:
        return cur
    rolled = pltpu.roll(cur, j, 0)
    top = jnp.where(_iota((8, cur.shape[1]), 0) < j, pltpu.roll(prev8, j, 0), rolled[0:8])
    return jnp.concatenate([top, rolled[8:]], axis=0)


def _shift_rows_up(cur, next8, j):
    if j == 0:
        return cur
    n = cur.shape[0]
    rolled = pltpu.roll(cur, n - j, 0)
    bot = jnp.where(_iota((8, cur.shape[1]), 0) >= 8 - j, pltpu.roll(next8, 8 - j, 0), rolled[n - 8:n])
    return jnp.concatenate([rolled[:n - 8], bot], axis=0)


def _softplus(x):
    return jnp.maximum(x, 0.0) + jnp.log(1.0 + jnp.exp(-jnp.abs(x)))


def _ssd_common(proj_ref, xpre, dtb_ref, alog_ref):
    xbc = xpre * _sigmoid(xpre)
    xs, bm, cm = xbc[:, 0:SW], xbc[:, SW:SW + 256], xbc[:, SW + 256:SW + 512]
    dtraw = proj_ref[:, PW - 128:PW] + dtb_ref[...]
    dt = _softplus(dtraw)
    a = -jnp.exp(alog_ref[...])
    tril = (_iota((CH, CH), 0) >= _iota((CH, CH), 1)).astype(F32)
    acs = _dot_hi(tril, dt * a)
    return xs, bm, cm, dtraw, dt, a, acs, acs.T


def _pair_lane(lo, hi):
    r = lo.shape[0]
    return jnp.where(_iota((r, 128), 1) < HP, lo, hi)


def _ssd_fwd(name, proj, conv_w, conv_b, dt_bias, a_log, d_rep, norm_w, seq):
    t = proj.shape[0]
    nb = t // seq

    def fn(ctx, tiled, halos, exs, res, outs, accs, exaccs, scr):
        @pl.when(ctx.seq_first)
        def _():
            scr[0][...] = jnp.zeros(scr[0].shape, F32)

        for b in range(nb):
            one(ctx, res, [r.at[b] for r in tiled + halos + outs + scr])

    def one(ctx, res, refs):
        proj_ref, halo_ref, yo_ref, xpre_ref, y_ref, sprev_ref, state_ref = refs
        cw_ref, cb_ref, dtb_ref, alog_ref, d_ref, nw_ref = res
        raw = proj_ref[:, SW:SW + D]
        prev8 = halo_ref[:, SW:SW + D] * jnp.where(ctx.seq_first, 0.0, 1.0)
        xpre = cb_ref[...] + cw_ref[3:4, :] * raw
        for j in (1, 2, 3):
            xpre = xpre + cw_ref[3 - j:4 - j, :] * _shift_rows(raw, prev8, j)
        xpre_ref[...] = xpre
        xs, bm, cm, dtraw, dt, a, acs, acst = _ssd_common(proj_ref, xpre, dtb_ref, alog_ref)
        causal = _iota((CH, CH), 0) >= _iota((CH, CH), 1)
        lane_lo = _iota((CH, 128), 1) < HP
        sprev = state_ref[...]
        sprev_ref[...] = sprev
        ys = []
        for g in range(NG):
            bmg, cmg = bm[:, g * NS:(g + 1) * NS], cm[:, g * NS:(g + 1) * NS]
            bmt = bmg.T
            cb = _dot(_mx(cmg), _mx(bmt))
            for q in (2 * g, 2 * g + 1):
                xsq = xs[:, 128 * q:128 * q + 128]
                xd = xsq * _pair_lane(dt[:, 2 * q:2 * q + 1], dt[:, 2 * q + 1:2 * q + 2])
                sp = sprev[:, 128 * q:128 * q + 128]
                ydiag = jnp.zeros((CH, 128), F32)
                snew = jnp.zeros((NS, 128), F32)
                for jj in range(2):
                    h = 2 * q + jj
                    col, row = acs[:, h:h + 1], acst[h:h + 1, :]
                    lm = jnp.where(causal, jnp.exp(jnp.minimum(col - row, 0.0)), 0.0)
                    xm = _mx(jnp.where(lane_lo if jj == 0 else ~lane_lo, xd, 0.0))
                    ydiag = ydiag + _dot(_mx(cb * lm), xm)
                    dec_row = jnp.exp(acst[h:h + 1, CH - 1:CH] - row)
                    snew = snew + _dot(_mx(bmt * dec_row), xm)
                e_pair = jnp.exp(_pair_lane(acs[:, 2 * q:2 * q + 1], acs[:, 2 * q + 1:2 * q + 2]))
                yoff = _dot(_mx(cmg), _mx(sp)) * e_pair
                cd = jnp.exp(_pair_lane(acst[2 * q:2 * q + 1, CH - 1:CH], acst[2 * q + 1:2 * q + 2, CH - 1:CH]))
                state_ref[:, 128 * q:128 * q + 128] = cd * sp + snew
                ys.append(ydiag + yoff + d_ref[:, 128 * q:128 * q + 128] * xsq)
        y = jnp.concatenate(ys, axis=1)
        y_ref[...] = y
        z = proj_ref[:, 0:SW]
        yz = y * (z * _sigmoid(z))
        outp = []
        for g in range(NG):
            seg = yz[:, 256 * g:256 * g + 256]
            rinv = lax.rsqrt(jnp.mean(seg * seg, axis=-1, keepdims=True) + LN_EPS)
            outp.append(seg * rinv * nw_ref[:, 256 * g:256 * g + 256])
        yo_ref[...] = _mx(jnp.concatenate(outp, axis=1))

    return _rowcall(name, fn, t, CH, seq // CH, tiled=[proj], halos=[(proj, 8)],
                    res=[conv_w, conv_b, dt_bias, a_log, d_rep, norm_w],
                    out_tiled=[_sds((t, SW), MXU_DTYPE), _sds((t, D)), _sds((t, SW)), _sds((t, SW))],
                    scratch=[pltpu.VMEM((nb, NS, SW), F32)], batch=nb)


def _ssd_bwd(name, dyo, proj, xpre_all, y_all, sprev_all, conv_w, dt_bias, a_log, d_rep, norm_w, seq, **carry):
    t = proj.shape[0]
    nb = t // seq

    def fn(ctx, tiled, halos, exs, res, outs, accs, exaccs, scr):
        @pl.when(ctx.seq_last)
        def _():
            for r in scr:
                r[...] = jnp.zeros(r.shape, F32)

        @pl.when(ctx.first)
        def _():
            for r in accs:
                r[...] = jnp.zeros(r.shape, F32)

        for b in range(nb):
            one(ctx, None, res, accs, [r.at[b] for r in tiled + halos + outs + scr])

    def one(ctx, first, res, accs, refs):
        dyo_ref, proj_ref, xpre_ref, y_ref, sprev_ref, halo_ref, dzx_ref, ddt_ref, ds_ref, nxt_ref = refs
        cw_ref, dtb_ref, alog_ref, d_ref, nw_ref = res
        dnw_acc, dd_acc, dcw_acc, dcb_acc, ddtb_acc, dalog_acc = accs
        xpre = xpre_ref[...]
        xs, bm, cm, dtraw, dt, a, acs, acst = _ssd_common(proj_ref, xpre, dtb_ref, alog_ref)
        y = y_ref[...]
        z = proj_ref[:, 0:SW]
        sz = _sigmoid(z)
        siluz = z * sz
        yz = y * siluz
        dyo = dyo_ref[...]
        dyz_parts, dnw_parts = [], []
        for g in range(NG):
            seg = yz[:, 256 * g:256 * g + 256]
            rinv = lax.rsqrt(jnp.mean(seg * seg, axis=-1, keepdims=True) + LN_EPS)
            yn = seg * rinv
            dseg = dyo[:, 256 * g:256 * g + 256]
            dnw_parts.append(jnp.sum(dseg * yn, axis=0, keepdims=True))
            dyn = dseg * nw_ref[:, 256 * g:256 * g + 256]
            dyz_parts.append(rinv * (dyn - yn * jnp.mean(dyn * yn, axis=-1, keepdims=True)))
        dyz = jnp.concatenate(dyz_parts, axis=1)
        _acc(dnw_acc, jnp.concatenate(dnw_parts, axis=1), first)
        dy = dyz * siluz
        dz = dyz * y * (sz * (1.0 + z * (1.0 - sz)))
        _acc(dd_acc, jnp.sum(dy * xs, axis=0, keepdims=True), first)

        causal = _iota((CH, CH), 0) >= _iota((CH, CH), 1)
        anti = _iota((CH, CH), 0) <= _iota((CH, CH), 1)
        lane_lo = _iota((CH, 128), 1) < HP
        lane_id = _iota((CH, 128), 1)
        last_row = _iota((CH, 128), 0) == CH - 1
        sprev = sprev_ref[...]
        dacs = jnp.zeros((CH, 128), F32)
        ddt_x = jnp.zeros((CH, 128), F32)
        dxs_parts, dbm_parts, dcm_parts = [], [], []
        for g in range(NG):
            bmg, cmg = bm[:, g * NS:(g + 1) * NS], cm[:, g * NS:(g + 1) * NS]
            bmt, cmt = bmg.T, cmg.T
            cb = _dot(_mx(cmg), _mx(bmt))
            cbt = _dot(_mx(bmg), _mx(cmt))
            dcb = jnp.zeros((CH, CH), F32)
            dcbt = jnp.zeros((CH, CH), F32)
            dbmg = jnp.zeros((CH, NS), F32)
            dcmg = jnp.zeros((CH, NS), F32)
            for q in (2 * g, 2 * g + 1):
                sl = slice(128 * q, 128 * q + 128)
                xsq = xs[:, sl]
                dtp = _pair_lane(dt[:, 2 * q:2 * q + 1], dt[:, 2 * q + 1:2 * q + 2])
                xd = xsq * dtp
                dyq = dy[:, sl]
                sp = sprev[:, sl]
                dsn = ds_ref[:, sl]
                e_pair = jnp.exp(_pair_lane(acs[:, 2 * q:2 * q + 1], acs[:, 2 * q + 1:2 * q + 2]))
                cd = jnp.exp(_pair_lane(acst[2 * q:2 * q + 1, CH - 1:CH], acst[2 * q + 1:2 * q + 2, CH - 1:CH]))
                dye = dyq * e_pair
                dcmg = dcmg + _dot(_mx(dye), _mx(sp.T))
                dsp = _dot(_mx(cmt), _mx(dye)) + cd * dsn
                yoff = _dot(_mx(cmg), _mx(sp)) * e_pair
                dacs_lane = dyq * yoff
                dxd = jnp.zeros((CH, 128), F32)
                sds = jnp.sum(dsn * sp, axis=0, keepdims=True) * cd
                for jj in range(2):
                    h = 2 * q + jj
                    hm = lane_lo if jj == 0 else ~lane_lo
                    col, row = acs[:, h:h + 1], acst[h:h + 1, :]
                    lm = jnp.where(causal, jnp.exp(jnp.minimum(col - row, 0.0)), 0.0)
                    lmt = jnp.where(anti, jnp.exp(jnp.minimum(row - col, 0.0)), 0.0)
                    xm = _mx(jnp.where(hm, xd, 0.0))
                    dym = _mx(jnp.where(hm, dyq, 0.0))
                    gm = _dot_nt(dym, xm)
                    gmt = _dot_nt(xm, dym)
                    dcb = dcb + gm * lm
                    dcbt = dcbt + gmt * lmt
                    dxd = dxd + _dot(_mx(cbt * lmt), dym)
                    w = gm * cb * lm
                    wt = gmt * cbt * lmt
                    dacs_h = jnp.sum(w, axis=1, keepdims=True) - jnp.sum(wt, axis=1, keepdims=True)
                    alast = acst[h:h + 1, CH - 1:CH]
                    dec_col = jnp.exp(alast - col)
                    dsm = _mx(jnp.where(hm[0:NS], dsn, 0.0))
                    qh = _dot_nt(xm, dsm)
                    dbmg = dbmg + qh * dec_col
                    ddec = jnp.sum(qh * bmg, axis=1, keepdims=True)
                    dxd = dxd + _dot(_mx(bmg * dec_col), dsm)
                    dacs_h = dacs_h - ddec * dec_col
                    dacs_h = dacs_h + jnp.sum(jnp.where(hm, dacs_lane, 0.0), axis=1, keepdims=True)
                    tail = jnp.sum(ddec * dec_col, axis=0, keepdims=True) + jnp.sum(
                        jnp.where(hm[0:1], sds, 0.0), axis=1, keepdims=True)
                    dacs = dacs + jnp.where(lane_id == h, dacs_h, 0.0) + jnp.where(
                        last_row & (lane_id == h), tail, 0.0)
                ds_ref[:, sl] = dsp
                for jj in range(2):
                    h = 2 * q + jj
                    hm = lane_lo if jj == 0 else ~lane_lo
                    ddt_x = ddt_x + jnp.where(lane_id == h, jnp.sum(jnp.where(hm, dxd * xsq, 0.0), axis=1, keepdims=True), 0.0)
                dxs_parts.append(dxd * dtp + d_ref[:, sl] * dyq)
            dcmg = dcmg + _dot(_mx(dcb), _mx(bmg))
            dbmg = dbmg + _dot(_mx(dcbt), _mx(cmg))
            dbm_parts.append(dbmg)
            dcm_parts.append(dcmg)
        triu = (_iota((CH, CH), 0) <= _iota((CH, CH), 1)).astype(F32)
        dadt = _dot_hi(triu, dacs)
        ddt = dadt * a + ddt_x
        _acc(dalog_acc, jnp.sum(dadt * dt, axis=0, keepdims=True) * a, first)
        ddtraw = ddt * _sigmoid(dtraw)
        ddt_ref[...] = ddtraw
        _acc(ddtb_acc, jnp.sum(ddtraw, axis=0, keepdims=True), first)
        dxbc = jnp.concatenate(dxs_parts + dbm_parts + dcm_parts, axis=1)
        sx = _sigmoid(xpre)
        dpre = dxbc * (sx * (1.0 + xpre * (1.0 - sx)))
        _acc(dcb_acc, jnp.sum(dpre, axis=0, keepdims=True), first)
        raw = proj_ref[:, SW:SW + D]
        prev8 = halo_ref[:, SW:SW + D] * jnp.where(ctx.seq_first, 0.0, 1.0)
        next8 = nxt_ref[...]
        draw = cw_ref[3:4, :] * dpre
        dcw = [None] * 4
        dcw[3] = jnp.sum(dpre * raw, axis=0, keepdims=True)
        for j in (1, 2, 3):
            dcw[3 - j] = jnp.sum(dpre * _shift_rows(raw, prev8, j), axis=0, keepdims=True)
            draw = draw + cw_ref[3 - j:4 - j, :] * _shift_rows_up(dpre, next8, j)
        _acc(dcw_acc, jnp.concatenate(dcw + [jnp.zeros((4, D), F32)], axis=0), first)
        nxt_ref[...] = dpre[0:8]
        dzx_ref[:, 0:SW] = dz
        dzx_ref[:, SW:SW + D] = draw

    return _rowcall(name, fn, t, CH, seq // CH, tiled=[dyo, proj, xpre_all, y_all, sprev_all], halos=[(proj, 8)],
                    res=[conv_w, dt_bias, a_log, d_rep, norm_w],
                    out_tiled=[_sds((t, SW + D)), _sds((t, 128))],
                    out_acc=[_sds((1, SW)), _sds((1, SW)), _sds((8, D)), _sds((1, D)), _sds((1, 128)), _sds((1, 128))],
                    scratch=[pltpu.VMEM((nb, NS, SW), F32), pltpu.VMEM((nb, 8, D), F32)], reverse=True, batch=nb, **carry)


def _gelu(y):
    k = math.sqrt(2.0 / math.pi)
    return 0.5 * y * (1.0 + jnp.tanh(k * (y + 0.044715 * y * y * y)))


def _gelu_grad(y):
    k = math.sqrt(2.0 / math.pi)
    th = jnp.tanh(k * (y + 0.044715 * y * y * y))
    return 0.5 * (1.0 + th) + 0.5 * y * (1.0 - th * th) * k * (1.0 + 3.0 * 0.044715 * y * y)


S5T = 256


def _cmul_add(xr, xi, ar, ai, sr, si):
    return xr + ar * sr - ai * si, xi + ar * si + ai * sr


def _s5_fwd(name, proj, bbd, cbd, pw, tab, d5, w_glu, b_glu, seq):
    t = proj.shape[0]
    tm = S5T

    def fn(ctx, tiled, halos, exs, res, outs, accs, exaccs, scr):
        proj_ref, = tiled
        bbd_ref, cbd_ref, pw_ref, tab_ref, d_ref, wg_ref, bg_ref = res
        out_ref, xst_ref, y_ref, xb_ref, ub_ref = outs
        carry_ref, = scr

        @pl.when(ctx.seq_first)
        def _():
            carry_ref[...] = jnp.zeros(carry_ref.shape, F32)

        u = proj_ref[:, 1536:2048]
        bu = _dot(_mx(u), bbd_ref[...])
        xr, xi = bu[:, :S5L].reshape(tm // 8, 8, S5L), bu[:, S5L:].reshape(tm // 8, 8, S5L)
        for k, sh in enumerate((1, 2, 4)):
            xr, xi = _cmul_add(xr, xi, pw_ref[k, :, :S5L], pw_ref[k, :, S5L:], pltpu.roll(xr, sh, 1), pltpu.roll(xi, sh, 1))
        xst_ref[:, :S5L] = xr.reshape(tm, S5L)
        xst_ref[:, S5L:] = xi.reshape(tm, S5L)

        def tile_fix(i, c):
            cr, ci = c
            rows = pl.ds(pl.multiple_of(i * 8, 8), 8)
            tr, ti = _cmul_add(xst_ref[rows, :S5L], xst_ref[rows, S5L:], tab_ref[:, :S5L], tab_ref[:, S5L:], cr, ci)
            xst_ref[rows, :S5L] = tr
            xst_ref[rows, S5L:] = ti
            return tr[7:8], ti[7:8]

        cr, ci = lax.fori_loop(0, tm // 8, tile_fix, (carry_ref[0:1, :S5L], carry_ref[0:1, S5L:]))
        carry_ref[0:1, :S5L] = cr
        carry_ref[0:1, S5L:] = ci
        xb = _mx(xst_ref[...])
        xb_ref[...] = xb
        ub_ref[...] = _mx(u)
        y = _dot(xb, cbd_ref[...]) + u * d_ref[...]
        y_ref[...] = y
        g = _gelu(y)
        v = _dot(_mx(g), wg_ref[...]) + bg_ref[...]
        out_ref[...] = _mx(g * _sigmoid(v))

    return _rowcall(name, fn, t, tm, seq // tm, tiled=[proj], res=[bbd, cbd, pw, tab, d5, w_glu, b_glu],
                    out_tiled=[_sds((t, SW), MXU_DTYPE), _sds((t, 2 * S5L)), _sds((t, SW)),
                               _sds((t, 2 * S5L), MXU_DTYPE), _sds((t, SW), MXU_DTYPE)],
                    scratch=[pltpu.VMEM((8, 2 * S5L), F32)])


def _s5_bwd(name, dout, proj, xst, y_all, bbdt, cbdt, pwc, tabc, d5, w_glu, w_glut, b_glu, seq, **carry):
    t = proj.shape[0]
    tm = S5T

    def fn(ctx, tiled, halos, exs, res, outs, accs, exaccs, scr):
        dout_ref, proj_ref, xst_ref, y_ref = tiled
        halo_ref, = halos
        bbdt_ref, cbdt_ref, pw_ref, tab_ref, d_ref, wg_ref, wgt_ref, bg_ref = res
        du_ref, lam_ref, dyb_ref, gb_ref, dvb_ref = outs
        da_acc, dd_acc, dbg_acc = accs
        carry_ref, lamf_ref = scr

        @pl.when(ctx.seq_last)
        def _():
            carry_ref[...] = jnp.zeros(carry_ref.shape, F32)

        u = proj_ref[:, 1536:2048]
        y = y_ref[...]
        g = _gelu(y)
        v = _dot(_mx(g), wg_ref[...]) + bg_ref[...]
        sg = _sigmoid(v)
        dout = dout_ref[...]
        dv = dout * g * sg * (1.0 - sg)
        dvb = _mx(dv)
        dvb_ref[...] = dvb
        gb_ref[...] = _mx(g)
        _acc(dbg_acc, jnp.sum(dv, axis=0, keepdims=True), ctx.first)
        dg = dout * sg + _dot(dvb, wgt_ref[...])
        dy = dg * _gelu_grad(y)
        dyb = _mx(dy)
        dyb_ref[...] = dyb
        _acc(dd_acc, jnp.sum(dy * u, axis=0, keepdims=True), ctx.first)
        dx = _dot(dyb, cbdt_ref[...])
        xr, xi = dx[:, :S5L].reshape(tm // 8, 8, S5L), dx[:, S5L:].reshape(tm // 8, 8, S5L)
        for k, sh in enumerate((1, 2, 4)):
            xr, xi = _cmul_add(xr, xi, pw_ref[k, :, :S5L], pw_ref[k, :, S5L:], pltpu.roll(xr, 8 - sh, 1),
                               pltpu.roll(xi, 8 - sh, 1))
        lamf_ref[:, :S5L] = xr.reshape(tm, S5L)
        lamf_ref[:, S5L:] = xi.reshape(tm, S5L)

        def tile_fix(i, c):
            cr, ci = c
            rows = pl.ds(pl.multiple_of((tm // 8 - 1 - i) * 8, 8), 8)
            tr, ti = _cmul_add(lamf_ref[rows, :S5L], lamf_ref[rows, S5L:], tab_ref[:, :S5L], tab_ref[:, S5L:], cr, ci)
            lamf_ref[rows, :S5L] = tr
            lamf_ref[rows, S5L:] = ti
            return tr[0:1], ti[0:1]

        cr, ci = lax.fori_loop(0, tm // 8, tile_fix, (carry_ref[0:1, :S5L], carry_ref[0:1, S5L:]))
        carry_ref[0:1, :S5L] = cr
        carry_ref[0:1, S5L:] = ci
        lam = lamf_ref[...]
        lamb = _mx(lam)
        lam_ref[...] = lamb
        du_ref[...] = dy * d_ref[...] + _dot(lamb, bbdt_ref[...])
        prev8 = halo_ref[...] * jnp.where(ctx.seq_first, 0.0, 1.0)
        xprev = _shift_rows(xst_ref[...], prev8, 1)
        lr, li = lam[:, :S5L], lam[:, S5L:]
        pr, pi = xprev[:, :S5L], xprev[:, S5L:]
        dar = jnp.sum(lr * pr + li * pi, axis=0, keepdims=True)
        dai = jnp.sum(li * pr - lr * pi, axis=0, keepdims=True)
        _acc(da_acc, jnp.concatenate([dar, dai], axis=1), ctx.first)

    return _rowcall(name, fn, t, tm, seq // tm, tiled=[dout, proj, xst, y_all], halos=[(xst, 8)],
                    res=[bbdt, cbdt, pwc, tabc, d5, w_glu, w_glut, b_glu],
                    out_tiled=[_sds((t, SW)), _sds((t, 2 * S5L), MXU_DTYPE), _sds((t, SW), MXU_DTYPE),
                               _sds((t, SW), MXU_DTYPE), _sds((t, SW), MXU_DTYPE)],
                    out_acc=[_sds((1, 2 * S5L)), _sds((1, SW)), _sds((1, SW))],
                    scratch=[pltpu.VMEM((8, 2 * S5L), F32), pltpu.VMEM((tm, 2 * S5L), F32)], reverse=True, **carry)


def _out_fwd(name, yssd, ys5, x1, mod, w_out, lng, lnb, seq):
    t = x1.shape[0]
    tm = 256

    def fn(ctx, tiled, halos, exs, res, outs, accs, exaccs, scr):
        ya_ref, yb_ref, x_ref = tiled
        mod_ref, = exs
        w_ref, g_ref, b_ref = res
        m = _dot(ya_ref[...], w_ref[0:SW, :]) + _dot(yb_ref[...], w_ref[SW:2 * SW, :])
        r = ALPHA * x_ref[...] + mod_ref[0, 5:6, :] * m
        outs[0][...] = _ln_fwd(r, g_ref[...], b_ref[...])
        outs[1][...] = r
        outs[2][...] = m

    return _rowcall(name, fn, t, tm, seq // tm, tiled=[yssd, ys5, x1], exs=[mod], res=[w_out, lng, lnb],
                    out_tiled=[_sds((t, D)), _sds((t, D)), _sds((t, D))])


def _out_bwd(name, dxo, r, m, mod, lng, w_outt, seq, **carry):
    t = dxo.shape[0]
    tm = 256

    def fn(ctx, tiled, halos, exs, res, outs, accs, exaccs, scr):
        dxo_ref, r_ref, m_ref = tiled
        mod_ref, = exs
        g_ref, wt_ref = res
        dr, dgam, dbet = _ln_bwd(r_ref[...], g_ref[...], dxo_ref[...])
        outs[0][...] = dr
        _acc(accs[0], dgam, ctx.first)
        _acc(accs[1], dbet, ctx.first)
        _acc(exaccs[0].at[0], jnp.sum(dr * m_ref[...], axis=0, keepdims=True), ctx.ex_enter)
        dm = _mx(mod_ref[0, 5:6, :] * dr)
        outs[1][...] = dm
        dyc = _dot(dm, wt_ref[...])
        outs[2][...] = dyc[:, 0:SW]
        outs[3][...] = dyc[:, SW:2 * SW]

    b = mod.shape[0]
    return _rowcall(name, fn, t, tm, seq // tm, tiled=[dxo, r, m], exs=[mod], res=[lng, w_outt],
                    out_tiled=[_sds((t, D)), _sds((t, D), MXU_DTYPE), _sds((t, SW)), _sds((t, SW))],
                    out_acc=[_sds((1, D)), _sds((1, D))], out_exacc=[_sds((b, 1, D))], **carry)


def _s5_discretise(a_re, a_im, log_dt, b_re, b_im):
    dt = jnp.exp(log_dt)[:, None]
    mag = jnp.exp(dt * a_re)
    ab_re, ab_im = mag * jnp.cos(dt * a_im), mag * jnp.sin(dt * a_im)
    den = a_re * a_re + a_im * a_im
    nr, ni = ab_re - 1.0, ab_im
    f_re, f_im = (nr * a_re + ni * a_im) / den, (ni * a_re - nr * a_im) / den
    bb_re = f_re[..., None] * b_re - f_im[..., None] * b_im
    bb_im = f_re[..., None] * b_im + f_im[..., None] * b_re
    return ab_re, ab_im, bb_re, bb_im


def _s5_tables(ab_re, ab_im):
    ar, ai = ab_re.reshape(1, S5L), ab_im.reshape(1, S5L)
    pows = [(ar, ai)]
    for _ in range(7):
        pr, pi = pows[-1]
        pows.append((pr * ar - pi * ai, pr * ai + pi * ar))

    def pack(rows, sign):
        return jnp.concatenate([jnp.concatenate([r for r, _ in rows], axis=0),
                                jnp.concatenate([sign * i for _, i in rows], axis=0)], axis=1)

    row = jnp.arange(8)[:, None]
    pw = jnp.stack([jnp.where(row >= sh, pack([pows[sh - 1]], 1.0), 0.0) for sh in (1, 2, 4)])
    pwc = jnp.stack([jnp.where(row < 8 - sh, pack([pows[sh - 1]], -1.0), 0.0) for sh in (1, 2, 4)])
    tab = pack(pows, 1.0)
    tabc = pack(pows[::-1], -1.0)
    return pw, tab, pwc, tabc


class _GradGroup:
    def __init__(self, tag, grads, place):
        self.tag, self.names, (self.half, self.chip) = tag, list(grads), place
        self.gsh = [_shard_major(g) if n in COL_SHARDED else g.reshape(4, g.shape[0] // 4, g.shape[1])
                    for n, g in grads.items()]

    def sibling(self):
        return _rs_sibling_stage(self.gsh)

    def chips(self, received):
        self.sums, sums_bf = _rs_add(self.tag + "_add", self.gsh, received, self.half)
        return _rs_chips_stage(sums_bf)

    def join(self, received):
        return _rs_join_stage(_rs_sum(self.tag + "_sum", self.sums, received, self.chip, self.half))

    def result(self, joined):
        return dict(zip(self.names, joined))


def _hid(fn, *args, stage=None, **kw):
    if stage is None:
        return fn(*args, **kw), None
    return fn(*args, carry=stage, **kw)


def _local_step(x, tgt, mod, w, sp, seq, dist=None):
    t = x.shape[0]
    mxu = MXU_DTYPE
    big = {}

    def group(tag, grads):
        if dist is None:
            big.update(grads)
            return None
        return _GradGroup(tag, grads, dist[2])

    (x1, r1, h1, ab1, f1), late = _hid(_ffn_fwd, "ffn1_fwd", x, mod, 0, w["ffn1_w13"], w["ffn1_w2"], sp["ln1_g"],
                                       sp["ln1_b"], seq, stage=dist and dist[0], **({"carry_mid": 0.75} if dist else {}))
    if dist:
        w = {**w, **dist[1](late)}
    h2, proj = _proj_fwd("proj_fwd", x1, mod, w["w_in"], seq)
    yssd, xpre, yraw, sprev = _ssd_fwd("ssd_fwd", proj, sp["conv_w"], sp["conv_b"], sp["dt_bias"], sp["a_log"],
                                       sp["d_rep"], sp["ssd_norm_w"], seq)
    (ab_re, ab_im, bb_re, bb_im), disc_vjp = jax.vjp(_s5_discretise, sp["s5_a_re"], sp["s5_a_im"], sp["s5_log_dt"],
                                                     sp["s5_b_re"], sp["s5_b_im"])
    eye = jnp.eye(S5G, dtype=F32)
    bbd = jnp.concatenate([jnp.einsum("gk,gph->ghkp", eye, bb_re).reshape(SW, S5L),
                           jnp.einsum("gk,gph->ghkp", eye, bb_im).reshape(SW, S5L)], axis=1).astype(mxu)
    cbd = jnp.concatenate([jnp.einsum("gk,ghp->gpkh", eye, sp["s5_c_re"]).reshape(S5L, SW),
                           -jnp.einsum("gk,ghp->gpkh", eye, sp["s5_c_im"]).reshape(S5L, SW)], axis=0).astype(mxu)
    pw, tab, pwc, tabc = _s5_tables(lax.stop_gradient(ab_re), lax.stop_gradient(ab_im))
    ys5, xst, y5, xstb, ub = _s5_fwd("s5_fwd", proj, bbd, cbd, pw, tab, sp["s5_d"], w["w_glu"], sp["b_glu"], seq)
    x2, r2, m2 = _out_fwd("out_fwd", yssd, ys5, x1, mod, w["w_out"], sp["ln2_g"], sp["ln2_b"], seq)
    x3, r3, h3, ab3, f3, dy3, loss_acc = _ffn_fwd("ffn2_fwd", x2, mod, 6, w["ffn2_w13"], w["ffn2_w2"], sp["ln3_g"],
                                                  sp["ln3_b"], seq, tgt=tgt)
    dr3, df3, s3, dab3, dg3g, dg3b, dgate3 = _ffn_bwd1("ffn2_bwd1", dy3, r3, ab3, f3, mod, 6, sp["ln3_g"],
                                                        w["ffn2_w2t"], seq)
    dx2, dsh3, dsc3 = _mod_bwd("ffn2_bwd2", dab3, dr3, x2, mod, 6, w["ffn2_w13t"], seq)
    g13 = _tn_matmul("ffn2_dw13", h3, dab3, D, FB)
    grp = group("rs_ffn2", dict(ffn2_w1=g13[:, :FF], ffn2_w3=g13[:, FF:], ffn2_w2=_tn_matmul("ffn2_dw2", s3, df3, FB, D)))
    (dr2, dm2, dyssd, dys5, dg2g, dg2b, dgate2), got = _hid(_out_bwd, "out_bwd", dx2, r2, m2, mod, sp["ln2_g"],
                                                          w["w_outt"], seq, stage=grp and grp.sibling())
    g_w_out = jnp.concatenate([_tn_matmul("dw_out_a", yssd, dm2, SW, D), _tn_matmul("dw_out_b", ys5, dm2, SW, D)], axis=0)
    (du, lam, dy5b, g5b, dv5b, da5, dd5, dbglu), got = _hid(
        _s5_bwd, "s5_bwd", dys5, proj, xst, y5, bbd.T, cbd.T, pwc, tabc, sp["s5_d"], w["w_glu"], w["w_glut"], sp["b_glu"],
        seq, stage=grp and grp.chips(got))
    g_w_glu = _tn_matmul("dw_glu", g5b, dv5b, SW, SW)
    dbfull = _tn_matmul("s5_db", lam, ub, D, SW)
    dcfull = _tn_matmul("s5_dc", xstb, dy5b, D, SW)
    (dzx, ddt, dnw, ddl, dcw, dcb, ddtb, dalog), got = _hid(
        _ssd_bwd, "ssd_bwd", dyssd, proj, xpre, yraw, sprev, sp["conv_w"], sp["dt_bias"], sp["a_log"], sp["d_rep"],
        sp["ssd_norm_w"], seq, stage=grp and grp.join(got))
    if grp:
        big.update(grp.result(got))
    dx1, dproj, dsh2, dsc2 = _mod_bwd("proj_bwd", dzx, dr2, x1, mod, 3, w["w_int"], seq, extra=(du, ddt))
    gwi = _tn_matmul("dw_in", h2, dproj, D, PW)
    grp = group("rs_mix", dict(w_in=jnp.concatenate([gwi[:, :1536], gwi[:, 2048:2056], gwi[:, 1536:2048]], axis=1),
                               w_glu=g_w_glu, w_out=g_w_out))
    (dr1, df1, s1, dab1, dg1g, dg1b, dgate1), got = _hid(_ffn_bwd1, "ffn1_bwd1", dx1, r1, ab1, f1, mod, 0, sp["ln1_g"],
                                                          w["ffn1_w2t"], seq, stage=grp and grp.sibling())
    g13, got = _hid(_tn_matmul, "ffn1_dw13", h1, dab1, D, FB, stage=grp and grp.chips(got))
    g2, got = _hid(_tn_matmul, "ffn1_dw2", s1, df1, FB, D, stage=grp and grp.join(got))
    if grp:
        big.update(grp.result(got))
    grp = group("rs_ffn1", dict(ffn1_w1=g13[:, :FF], ffn1_w3=g13[:, FF:], ffn1_w2=g2))
    (dx0, dsh1, dsc1), got = _hid(_mod_bwd, "ffn1_bwd2", dab1, dr1, x, mod, 0, w["ffn1_w13t"], seq,
                                  stage=grp and grp.sibling())
    if grp:
        got = _run_stage("rs_ffn1_chips", grp.chips(got))
        big.update(grp.result(_run_stage("rs_ffn1_join", grp.join(got))))
    dmod = jnp.concatenate([dsh1, dsc1, dgate1, dsh2, dsc2, dgate2, dsh3, dsc3, dgate3], axis=1)
    d4 = lambda a: a.reshape(S5G, S5P, S5G, S5H)
    dbb_re = jnp.einsum("gpgh->gph", d4(dbfull[:S5L]))
    dbb_im = jnp.einsum("gpgh->gph", d4(dbfull[S5L:]))
    dc_re = jnp.einsum("gpgh->ghp", d4(dcfull[:S5L]))
    dc_im = -jnp.einsum("gpgh->ghp", d4(dcfull[S5L:]))
    g_a_re, g_a_im, g_log_dt, g_b_re, g_b_im = disc_vjp(
        (da5[:, :S5L].reshape(S5G, S5P), da5[:, S5L:].reshape(S5G, S5P), dbb_re, dbb_im))
    small = dict(ln1_g=dg1g, ln1_b=dg1b, ln2_g=dg2g, ln2_b=dg2b, ln3_g=dg3g, ln3_b=dg3b, conv_w=dcw[0:4], conv_b=dcb,
                 dt_bias=ddtb[:, :NH], a_log=dalog[:, :NH], d_ssd=jnp.sum(ddl.reshape(NH, HP), axis=1).reshape(1, NH),
                 ssd_norm_w=dnw, s5_a_re=g_a_re, s5_a_im=g_a_im, s5_log_dt=g_log_dt, s5_b_re=g_b_re, s5_b_im=g_b_im,
                 s5_c_re=dc_re, s5_c_im=dc_im, s5_d=dd5, w_glu_b=dbglu)
    return loss_acc[0, 0], dx0, dmod, big, small


def _place():
    return lax.axis_index("x"), lax.axis_index("y"), lax.axis_index("c")


def _other_chips(x, y):
    return [(1 - x, y), (x, 1 - y), (1 - x, 1 - y)]


def _allgather8(name, a):
    r, n = a.shape

    def body(x_ref, out_ref, send_sems, recv_sems, local_sem):
        x, y, c = _place()
        me, sibling = (x, y, c), (x, y, 1 - c)
        chips = _other_chips(x, y)

        def rows(px, py, pc):
            return out_ref.at[pl.ds(pl.multiple_of((4 * px + 2 * py + pc) * r, 8), r), :]

        def copy(k, block, to, src=None):
            return pltpu.make_async_remote_copy(src_ref=rows(*block) if src is None else src, dst_ref=rows(*block),
                                                send_sem=send_sems.at[k], recv_sem=recv_sems.at[k], device_id=to,
                                                device_id_type=MESH_T)

        mine = pltpu.make_async_copy(x_ref, rows(*me), local_sem)
        mine.start()
        first = [copy(0, me, sibling, src=x_ref)]
        first += [copy(1 + j, me, (*chip, c), src=x_ref) for j, chip in enumerate(chips)]
        for cp in first:
            cp.start()
        passed = [copy(4 + j, (*chip, c), sibling) for j, chip in enumerate(chips)]
        for j, chip in enumerate(chips):
            copy(1 + j, (*chip, c), me).wait_recv()
            passed[j].start()
        copy(0, sibling, me).wait_recv()
        for j, chip in enumerate(chips):
            copy(4 + j, (*chip, 1 - c), me).wait_recv()
        for cp in first + passed:
            cp.wait_send()
        mine.wait()

    out = _pcall(body, name=name, out_shape=_sds((8 * r, n), a.dtype),
                 in_specs=[pl.BlockSpec(memory_space=pltpu.VMEM)], out_specs=pl.BlockSpec(memory_space=pltpu.VMEM),
                 scratch_shapes=[pltpu.SemaphoreType.DMA((7,)), pltpu.SemaphoreType.DMA((7,)), pltpu.SemaphoreType.DMA],
                 compiler_params=_cparams())(a)
    return out.reshape(8, r, n)


class _Stage:
    def __init__(self, ins, out_shapes, n_sems, start, finish, mid=None, aliases=None):
        self.ins, self.out_shapes, self.n_sems = list(ins), list(out_shapes), tuple(n_sems)
        self.start, self.mid, self.finish = start, mid, finish
        self.aliases = dict(aliases or {})

    def sem_shapes(self):
        return [pltpu.SemaphoreType.DMA((n,)) for n in self.n_sems]

    def run(self, refs, at_start=None, at_mid=None, at_finish=None, between=None):
        if between is None:
            for part in (self.start, self.mid, self.finish):
                if part is not None:
                    part(*refs)
            return
        pl.when(at_start)(functools.partial(self.start, *refs))
        if self.mid is not None:
            pl.when(at_mid)(functools.partial(self.mid, *refs))
        between()
        pl.when(at_finish)(functools.partial(self.finish, *refs))


def _run_stage(name, st):
    n_in, n_out = len(st.ins), len(st.out_shapes)

    def body(*refs):
        st.run((refs[:n_in], refs[n_in:n_in + n_out], refs[n_in + n_out:]))

    any_spec = pl.BlockSpec(memory_space=pl.ANY)
    return _pcall(body, name=name, out_shape=st.out_shapes, in_specs=[any_spec] * n_in, out_specs=[any_spec] * n_out,
                  input_output_aliases=st.aliases, scratch_shapes=st.sem_shapes(), compiler_params=_cparams())(*st.ins)


def _rows(ref_rows, half, align):
    hr = ref_rows // 2
    return pl.ds(pl.multiple_of(half * hr, align), hr)


def _gather_stage(shards):
    n = len(shards)
    pairs = [(i, j) for i in range(n) for j in range(3)]

    def env(ins, outs, sems):
        x, y, c = _place()
        chips = _other_chips(x, y)

        def copy(i, k, chip, half, to, src=None):
            dst = outs[i].at[2 * chip[0] + chip[1], _rows(shards[i].shape[0], half, 16)]
            return pltpu.make_async_remote_copy(src_ref=dst if src is None else src, dst_ref=dst,
                                                send_sem=sems[0].at[6 * i + k], recv_sem=sems[1].at[6 * i + k],
                                                device_id=to, device_id_type=MESH_T)

        def first(i, j):
            return copy(i, j, (x, y), c, (*chips[j], c), src=ins[i].at[_rows(shards[i].shape[0], c, 16)])

        def passed(i, j, half):
            return copy(i, 3 + j, chips[j], half, (x, y, 1 - c))

        def landed(i, j):
            return copy(i, j, chips[j], c, (x, y, 1 - c))

        return c, first, passed, landed

    def start(ins, outs, sems):
        c, first, passed, landed = env(ins, outs, sems)
        for i, j in pairs:
            first(i, j).start()

    def mid(ins, outs, sems):
        c, first, passed, landed = env(ins, outs, sems)
        for i, j in pairs:
            landed(i, j).wait_recv()
            passed(i, j, c).start()

    def finish(ins, outs, sems):
        c, first, passed, landed = env(ins, outs, sems)
        for i, j in pairs:
            passed(i, j, 1 - c).wait_recv()
        for i, j in pairs:
            first(i, j).wait_send()
            passed(i, j, c).wait_send()

    return _Stage(shards, [_sds((4,) + s.shape, s.dtype) for s in shards], (6 * n, 6 * n), start, finish, mid)


def _rs_sibling_stage(gs):
    n = len(gs)

    def copies(ins, outs, sems):
        x, y, c = _place()
        return [pltpu.make_async_remote_copy(src_ref=ins[i].at[:, _rows(gs[i].shape[1], 1 - c, 8)], dst_ref=outs[i],
                                             send_sem=sems[0].at[i], recv_sem=sems[1].at[i], device_id=(x, y, 1 - c),
                                             device_id_type=MESH_T) for i in range(n)]

    def start(*refs):
        for cp in copies(*refs):
            cp.start()

    def finish(*refs):
        for cp in copies(*refs):
            cp.wait()

    return _Stage(gs, [_sds((4, g.shape[1] // 2, g.shape[2]), g.dtype) for g in gs], (n, n), start, finish)


def _rs_chips_stage(hs):
    n = len(hs)

    def copies(ins, outs, sems):
        x, y, c = _place()
        return [pltpu.make_async_remote_copy(src_ref=ins[i].at[2 * chip[0] + chip[1]], dst_ref=outs[i].at[j],
                                             send_sem=sems[0].at[3 * i + j], recv_sem=sems[1].at[3 * i + j],
                                             device_id=(*chip, c), device_id_type=MESH_T)
                for i in range(n) for j, chip in enumerate(_other_chips(x, y))]

    def start(*refs):
        for cp in copies(*refs):
            cp.start()

    def finish(*refs):
        for cp in copies(*refs):
            cp.wait()

    return _Stage(hs, [_sds((3,) + h.shape[1:], h.dtype) for h in hs], (3 * n, 3 * n), start, finish)


def _rs_join_stage(fs):
    n = len(fs)

    def copy(outs, sems, i, half):
        x, y, c = _place()
        part = outs[i].at[_rows(fs[i].shape[0], c if half == "mine" else 1 - c, 8)]
        return pltpu.make_async_remote_copy(src_ref=part, dst_ref=part, send_sem=sems[0].at[i], recv_sem=sems[1].at[i],
                                            device_id=(x, y, 1 - c), device_id_type=MESH_T)

    def start(ins, outs, sems):
        for i in range(n):
            copy(outs, sems, i, "mine").start()

    def finish(ins, outs, sems):
        for i in range(n):
            copy(outs, sems, i, "theirs").wait_recv()
        for i in range(n):
            copy(outs, sems, i, "mine").wait_send()

    return _Stage(fs, [_sds(f.shape, f.dtype) for f in fs], (n, n), start, finish, aliases={i: i for i in range(n)})


def _row_block(r, cap=2048):
    b = min(r, cap)
    while r % b or b % 8:
        b -= 8
    return b


RS_SPLIT = 2


def _rs_add(name, gs, r1s, sel):
    n = len(gs)

    def body(sel_ref, *refs):
        g_refs, r_refs, h_refs, b_refs = (refs[k * n:(k + 1) * n] for k in range(4))
        for i in range(n):
            h = g_refs[i][...] + r_refs[i][...]
            h_refs[i][...] = h
            b_refs[i][...] = h.astype(BF16)

    def blk(g):
        return (1, g.shape[1] // 2 // RS_SPLIT, g.shape[2])

    here = lambda k, j, s: (k, j, 0)
    in_specs = [pl.BlockSpec(blk(g), lambda k, j, s: (k, s[0] * RS_SPLIT + j, 0)) for g in gs]
    in_specs += [pl.BlockSpec(blk(g), here) for g in gs]
    outs = _pcall(body, name=name,
                  out_shape=[_sds((4, g.shape[1] // 2, g.shape[2])) for g in gs]
                  + [_sds((4, g.shape[1] // 2, g.shape[2]), BF16) for g in gs],
                  grid_spec=pltpu.PrefetchScalarGridSpec(num_scalar_prefetch=1, grid=(4, RS_SPLIT), in_specs=in_specs,
                                                         out_specs=[pl.BlockSpec(blk(g), here) for g in gs] * 2),
                  compiler_params=_cparams(("parallel", "parallel")))(sel.reshape(1).astype(jnp.int32), *gs, *r1s)
    return outs[:n], outs[n:]


def _rs_sum(name, hs, r2s, chip, half):
    n = len(hs)

    def body(sel_ref, *refs):
        h_refs, r_refs, o_refs = (refs[k * n:(k + 1) * n] for k in range(3))
        for i in range(n):
            r = r_refs[i]
            o_refs[i][...] = ((h_refs[i][0] + r[0].astype(F32)) + r[1].astype(F32)) + r[2].astype(F32)

    def rows(h):
        return h.shape[1] // RS_SPLIT

    in_specs = [pl.BlockSpec((1, rows(h), h.shape[2]), lambda j, s: (s[0], j, 0)) for h in hs]
    in_specs += [pl.BlockSpec((3, rows(h), h.shape[2]), lambda j, s: (0, j, 0)) for h in hs]
    sel = jnp.stack([chip, half]).astype(jnp.int32)
    return _pcall(body, name=name, out_shape=[_sds((2 * h.shape[1], h.shape[2])) for h in hs],
                  grid_spec=pltpu.PrefetchScalarGridSpec(
                      num_scalar_prefetch=1, grid=(RS_SPLIT,), in_specs=in_specs,
                      out_specs=[pl.BlockSpec((rows(h), h.shape[2]), lambda j, s: (s[1] * RS_SPLIT + j, 0)) for h in hs]),
                  compiler_params=_cparams(("parallel",)))(sel, *hs, *r2s)


def _sum8(name, a):
    _, r, n = a.shape
    br = _row_block(r)

    def body(a_ref, o_ref):
        acc = a_ref[0]
        for k in range(1, 8):
            acc = acc + a_ref[k]
        o_ref[...] = acc

    return _pcall(body, name=name, out_shape=_sds((r, n)), grid=(r // br,),
                  in_specs=[pl.BlockSpec((8, br, n), lambda j: (0, j, 0))], out_specs=pl.BlockSpec((br, n), lambda j: (j, 0)),
                  compiler_params=_cparams(("parallel",)))(a)


def _adamw(name, ws, gs, ms, vs, nblk):
    n = len(ws)

    def body(*refs):
        w_refs, g_refs, m_refs, v_refs, d_refs, nm_refs, nv_refs = (refs[k * n:(k + 1) * n] for k in range(7))
        for i in range(n):
            gv = g_refs[i][...]
            nm = ADAM_B1 * m_refs[i][...] + (1.0 - ADAM_B1) * gv
            nv = ADAM_B2 * v_refs[i][...] + (1.0 - ADAM_B2) * (gv * gv)
            nm_refs[i][...] = nm
            nv_refs[i][...] = nv
            m_hat = nm / (1.0 - ADAM_B1 ** ADAM_STEP)
            v_hat = nv / (1.0 - ADAM_B2 ** ADAM_STEP)
            d_refs[i][...] = -ADAM_LR * (m_hat / (jnp.sqrt(v_hat) + ADAM_EPS) + ADAM_WD * w_refs[i][...])

    specs = [pl.BlockSpec((w.shape[0] // nblk, w.shape[1]), lambda j: (j, 0)) for w in ws]
    outs = _pcall(body, name=name, out_shape=[_sds(w.shape) for w in ws] * 3, grid=(nblk,), in_specs=specs * 4,
                  out_specs=specs * 3, compiler_params=_cparams(("parallel",)))(*ws, *gs, *ms, *vs)
    return outs[:n], outs[n:2 * n], outs[2 * n:]


ADA_COLS = 2304
ADA_BLK = 768


def _ada_fwd(name, c_all, w_shard, b_cols):
    nb = c_all.shape[0]

    def body(c_ref, w_ref, b_ref, o_ref):
        cv = c_ref[...]
        cs = _mx(cv * _sigmoid(cv))
        o_ref[...] = _dot(cs, _mx(w_ref[...])) + b_ref[...]

    return _pcall(body, name=name, out_shape=_sds((nb, ADA_COLS)), grid=(ADA_COLS // ADA_BLK,),
                  in_specs=[pl.BlockSpec((nb, D), lambda j: (0, 0)), pl.BlockSpec((D, ADA_BLK), lambda j: (0, j)),
                            pl.BlockSpec((1, ADA_BLK), lambda j: (0, j))],
                  out_specs=pl.BlockSpec((nb, ADA_BLK), lambda j: (0, j)),
                  compiler_params=_cparams(("parallel",)))(c_all, w_shard, b_cols)


def _ada_bwd(name, c_all, dmod_cols, dmod_all):
    nb = c_all.shape[0]

    def body(c_ref, dc_ref, da_ref, gw_ref, gb_ref):
        cv = c_ref[...]
        cs = _mx(cv * _sigmoid(cv))
        gw_ref[...] = lax.dot_general(cs, _mx(dc_ref[...]), (((0,), (0,)), ((), ())), preferred_element_type=F32)

        @pl.when(pl.program_id(0) == 0)
        def _():
            gb_ref[...] = jnp.sum(da_ref[...], axis=0, keepdims=True)

    return _pcall(body, name=name, out_shape=[_sds((D, ADA_COLS)), _sds((1, 9 * D))], grid=(ADA_COLS // ADA_BLK,),
                  in_specs=[pl.BlockSpec((nb, D), lambda j: (0, 0)), pl.BlockSpec((nb, ADA_BLK), lambda j: (0, j)),
                            pl.BlockSpec((nb, 9 * D), lambda j: (0, 0))],
                  out_specs=[pl.BlockSpec((D, ADA_BLK), lambda j: (0, j)), pl.BlockSpec((1, 9 * D), lambda j: (0, 0))],
                  compiler_params=_cparams(("arbitrary",)))(c_all, dmod_cols, dmod_all)


BIG = ("ffn1_w1", "ffn1_w3", "ffn1_w2", "w_in", "w_glu", "w_out", "ffn2_w1", "ffn2_w3", "ffn2_w2")
COL_SHARDED = ("ffn1_w1", "ffn1_w3", "w_in", "ffn2_w1", "ffn2_w3")
SMALL = ("b_ada", "ln1_g", "ln1_b", "conv_w", "conv_b", "dt_bias", "a_log", "d_ssd", "ssd_norm_w", "s5_a_re", "s5_a_im",
         "s5_log_dt", "s5_b_re", "s5_b_im", "s5_c_re", "s5_c_im", "s5_d", "b_glu", "ln2_g", "ln2_b", "ln3_g", "ln3_b")
WEIGHTS = ("w_ada", "b_ada", "ffn1_w1", "ffn1_w3", "ffn1_w2", "ln1_g", "ln1_b", "w_in", "conv_w", "conv_b", "dt_bias",
           "a_log", "d_ssd", "ssd_norm_w", "s5_a_re", "s5_a_im", "s5_log_dt", "s5_b_re", "s5_b_im", "s5_c_re", "s5_c_im",
           "s5_d", "w_glu", "b_glu", "w_out", "ln2_g", "ln2_b", "ffn2_w1", "ffn2_w3", "ffn2_w2", "ln3_g", "ln3_b")
BIG_PAD = 2 * 1024 * 128


def _pack(arrs, mult, axis_keep=0):
    lead = arrs[0].shape[:axis_keep]
    flat = jnp.concatenate([a.reshape(lead + (-1,)) for a in arrs], axis=axis_keep)
    pad = (-flat.shape[-1]) % mult
    if pad:
        flat = jnp.concatenate([flat, jnp.zeros(lead + (pad,), flat.dtype)], axis=axis_keep)
    return flat


def _unpack(flat, shapes):
    out, off = [], 0
    for s in shapes:
        size = math.prod(s)
        out.append(flat[..., off:off + size].reshape(flat.shape[:-1] + tuple(s)))
        off += size
    return out


def _shard_major(a):
    rows, cols = a.shape
    return a.reshape(rows, 4, cols // 4).transpose(1, 0, 2)


def _from_shard_major(a):
    _, rows, w = a.shape
    return a.transpose(1, 0, 2).reshape(rows, 4 * w)


def kernel(x, c, w_ada, b_ada, ffn1_w1, ffn1_w3, ffn1_w2, ln1_g, ln1_b, w_in, conv_w, conv_b, dt_bias, a_log, d_ssd, ssd_norm_w, s5_a_re, s5_a_im, s5_log_dt, s5_b_re, s5_b_im, s5_c_re, s5_c_im, s5_d, w_glu, b_glu, w_out, ln2_g, ln2_b, ffn2_w1, ffn2_w3, ffn2_w2, ln3_g, ln3_b, loss_target, m_w_ada, m_b_ada, m_ffn1_w1, m_ffn1_w3, m_ffn1_w2, m_ln1_g, m_ln1_b, m_w_in, m_conv_w, m_conv_b, m_dt_bias, m_a_log, m_d_ssd, m_ssd_norm_w, m_s5_a_re, m_s5_a_im, m_s5_log_dt, m_s5_b_re, m_s5_b_im, m_s5_c_re, m_s5_c_im, m_s5_d, m_w_glu, m_b_glu, m_w_out, m_ln2_g, m_ln2_b, m_ffn2_w1, m_ffn2_w3, m_ffn2_w2, m_ln3_g, m_ln3_b, v_w_ada, v_b_ada, v_ffn1_w1, v_ffn1_w3, v_ffn1_w2, v_ln1_g, v_ln1_b, v_w_in, v_conv_w, v_conv_b, v_dt_bias, v_a_log, v_d_ssd, v_ssd_norm_w, v_s5_a_re, v_s5_a_im, v_s5_log_dt, v_s5_b_re, v_s5_b_im, v_s5_c_re, v_s5_c_im, v_s5_d, v_w_glu, v_b_glu, v_w_out, v_ln2_g, v_ln2_b, v_ffn2_w1, v_ffn2_w3, v_ffn2_w2, v_ln3_g, v_ln3_b):
    a = dict(locals())
    xi, yi, ci = _place()
    chip = 2 * xi + yi
    dev = 2 * chip + ci
    nb, seq, _ = x.shape
    t = nb * seq
    ndev = 8

    c_all = _allgather8("gather_c", c.reshape(-1, 128)).reshape(ndev * nb, D)
    b_cols = lax.dynamic_slice(b_ada, (0, chip * ADA_COLS), (1, ADA_COLS))
    mod_part = _ada_fwd("ada_fwd", c_all, w_ada[0], b_cols)
    mod_parts = _allgather8("gather_mod", mod_part.reshape(-1, 128)).reshape(ndev, ndev * nb, ADA_COLS)
    mod_all = mod_parts[0::2].transpose(1, 0, 2).reshape(ndev * nb, 9 * D)
    mod = lax.dynamic_slice(mod_all, (nb * dev, 0), (nb, 9 * D)).reshape(nb, 9, D)

    def gather(names):
        own = [a[n][0].astype(MXU_DTYPE) for n in names]

        def weights(pieces):
            full = {}
            for n, mine, piece in zip(names, own, pieces):
                piece = lax.dynamic_update_slice(piece, mine[None], (chip, 0, 0))
                full[n] = _from_shard_major(piece) if n in COL_SHARDED else piece.reshape(-1, piece.shape[-1])
            w = {}
            for p in ("ffn1", "ffn2"):
                if p + "_w1" in full:
                    w[p + "_w13"] = jnp.concatenate([full[p + "_w1"], full[p + "_w3"]], axis=1)
                    w[p + "_w2"] = full[p + "_w2"]
            if "w_in" in full:
                wi = full["w_in"]
                w["w_in"] = jnp.concatenate([wi[:, :1536], wi[:, 1544:2056], wi[:, 1536:1544],
                                             jnp.zeros((D, 120), wi.dtype)], axis=1)
                w["w_glu"], w["w_out"] = full["w_glu"], full["w_out"]
            w.update({n + "t": v.T for n, v in w.items()})
            return w

        return _gather_stage(own), weights

    first_stage, first_weights = gather(BIG[:3])
    w = first_weights(_run_stage("gather_w_ffn1", first_stage))
    late_stage, late_weights = gather(BIG[3:])

    cw_all = _allgather8("gather_conv_w", _pack([conv_w[0]], 1024).reshape(-1, 128)).reshape(ndev, -1)[0::2, :4 * 256]
    conv_full = _from_shard_major(cw_all.reshape(4, 4, 256))
    pad8 = lambda v: jnp.concatenate([v.reshape(1, NH), jnp.zeros((1, 128 - NH), F32)], axis=1)
    sp = dict(ln1_g=ln1_g, ln1_b=ln1_b, ln2_g=ln2_g, ln2_b=ln2_b, ln3_g=ln3_g, ln3_b=ln3_b, conv_w=conv_full,
              conv_b=conv_b, dt_bias=pad8(dt_bias), a_log=pad8(a_log), d_rep=jnp.repeat(d_ssd[0], HP)[None],
              ssd_norm_w=ssd_norm_w, s5_a_re=s5_a_re[0], s5_a_im=s5_a_im[0], s5_log_dt=s5_log_dt[0], s5_b_re=s5_b_re[0],
              s5_b_im=s5_b_im[0], s5_c_re=s5_c_re[0], s5_c_im=s5_c_im[0], s5_d=s5_d, b_glu=b_glu)

    lsum, dx0, dmod, gbig, small = _local_step(x.reshape(t, D), loss_target.reshape(t, D), mod, w, sp, seq,
                                               dist=(late_stage, late_weights, (ci, chip)))
    loss = lax.psum(lsum * (0.5 / D), ("x", "y", "c"))

    dmod_all = _allgather8("gather_dmod", dmod.reshape(-1, 128)).reshape(ndev * nb, 9 * D)
    dmod_cols = lax.dynamic_slice(dmod_all, (0, chip * ADA_COLS), (ndev * nb, ADA_COLS))
    g_w_ada, g_b_ada = _ada_bwd("ada_bwd", c_all, dmod_cols, dmod_all)

    gbig["w_ada"] = g_w_ada

    outs = {}
    for call, names in (("adamw_a", ("ffn1_w1", "ffn1_w3", "ffn1_w2", "w_in", "w_glu", "w_out")),
                        ("adamw_b", ("ffn2_w1", "ffn2_w3", "ffn2_w2", "w_ada"))):
        res = _adamw(call, [a[n][0] for n in names], [gbig[n] for n in names], [a["m_" + n][0] for n in names],
                     [a["v_" + n][0] for n in names], 8)
        for kind, arrs in zip(("grad", "delta", "new_m", "new_v"), ([gbig[n] for n in names],) + tuple(res)):
            for n, arr in zip(names, arrs):
                outs[kind, n] = arr[None]

    snames = [n for n in SMALL if n != "b_ada"]
    sgrad = dict(small)
    sgrad["b_glu"] = small["w_glu_b"]
    svec = _pack([sgrad[n] for n in snames], 1024).reshape(-1, 128)
    ssum = _sum8("small_sum", _allgather8("gather_small", svec)).reshape(-1)

    def view2d(u):
        s = u.shape[1:]
        return u.reshape((1, s[0]) if len(s) == 1 else (-1, s[-1]))

    vshape = {n: view2d(a[n]).shape for n in SMALL}
    gsm = dict(zip(snames, _unpack(ssum, [vshape[n] if n != "conv_w" else (4, D) for n in snames])))
    gsm["conv_w"] = lax.dynamic_slice(gsm["conv_w"], (0, chip * 256), (4, 256))
    gsm["b_ada"] = g_b_ada
    res = _adamw("adamw_small", [view2d(a[n]) for n in SMALL], [gsm[n] for n in SMALL],
                 [view2d(a["m_" + n]) for n in SMALL], [view2d(a["v_" + n]) for n in SMALL], 1)
    for kind, arrs in zip(("grad", "delta", "new_m", "new_v"), ([gsm[n] for n in SMALL],) + tuple(res)):
        for n, arr in zip(SMALL, arrs):
            outs[kind, n] = arr.reshape(a[n].shape)

    res = [loss, dx0.reshape(nb, seq, D)]
    for kind in ("grad", "delta", "new_m", "new_v"):
        res += [outs[kind, n] for n in WEIGHTS]
    return tuple(res)
```

```python
import functools
import math

import jax
import jax.numpy as jnp
from jax import lax
from jax.experimental import pallas as pl
from jax.experimental.pallas import tpu as pltpu

F32 = jnp.float32
BF16 = jnp.bfloat16
MXU_DTYPE = jnp.bfloat16

D = 1024
FF = 2816
FB = 1408
NH, HP, NS, NG = 8, 64, 128, 2
CH = 128
SW = 512
S5G, S5P, S5H = 32, 64, 16
S5L = S5G * S5P
PW = 2176
ALPHA = 2.0 ** 0.25
LN_EPS = 1e-5
ADAM_LR, ADAM_B1, ADAM_B2, ADAM_EPS, ADAM_WD, ADAM_STEP = 0.001, 0.9, 0.999, 1e-08, 0.01, 10
VMEM_LIMIT = 56 * 1024 * 1024
MESH_T = pl.DeviceIdType.MESH


def _pcall(body, **kw):
    return pl.pallas_call(body, **kw)


def _cparams(sem=None, **kw):
    return pltpu.CompilerParams(dimension_semantics=sem, vmem_limit_bytes=VMEM_LIMIT, **kw)


def _dot(a, b):
    return jnp.dot(a, b, preferred_element_type=F32)


def _dot_nt(a, b):
    return lax.dot_general(a, b, (((1,), (1,)), ((), ())), preferred_element_type=F32)


def _dot_hi(a, b):
    return jnp.dot(a, b, preferred_element_type=F32, precision=lax.Precision.HIGHEST)


def _mx(a):
    return a.astype(MXU_DTYPE)


def _sigmoid(x):
    return 1.0 / (1.0 + jnp.exp(-x))


def _iota(shape, axis):
    return lax.broadcasted_iota(jnp.int32, shape, axis)


def _rowcall(name, fn, n_rows, tm, tpe, *, tiled=(), halos=(), exs=(), res=(), out_tiled=(), out_acc=(),
             out_exacc=(), scratch=(), reverse=False, batch=None, carry=None, carry_mid=0.0):
    if batch:
        n_rows //= batch
    nt = n_rows // tm

    def blk(i):
        return (nt - 1 - i) if reverse else i

    in_specs, args = [], []
    for a in tiled:
        if batch:
            in_specs.append(pl.BlockSpec((batch, tm, a.shape[1]), lambda i: (0, blk(i), 0)))
            args.append(a.reshape(batch, n_rows, a.shape[1]))
            continue
        in_specs.append(pl.BlockSpec((tm, a.shape[1]), lambda i: (blk(i), 0)))
        args.append(a)
    for a, rows in halos:
        r = tm // rows
        if batch:
            in_specs.append(pl.BlockSpec((batch, rows, a.shape[1]), lambda i, r=r: (0, jnp.maximum(blk(i) * r - 1, 0), 0)))
            args.append(a.reshape(batch, n_rows, a.shape[1]))
            continue
        in_specs.append(pl.BlockSpec((rows, a.shape[1]), lambda i, r=r: (jnp.maximum(blk(i) * r - 1, 0), 0)))
        args.append(a)
    for a in exs:
        in_specs.append(pl.BlockSpec((1,) + a.shape[1:], lambda i: (blk(i) // tpe, 0, 0)))
        args.append(a)
    for a in res:
        nd = a.ndim
        in_specs.append(pl.BlockSpec(a.shape, lambda i, nd=nd: (0,) * nd, pipeline_mode=pl.Buffered(1)))
        args.append(a)
    any_spec = pl.BlockSpec(memory_space=pl.ANY)
    st_ins = carry.ins if carry else []
    st_outs = carry.out_shapes if carry else []
    st_sems = carry.sem_shapes() if carry else []
    base_in = len(args)
    in_specs += [any_spec] * len(st_ins)
    args += st_ins
    out_specs, out_shape = [], []
    for s in out_tiled:
        if batch:
            out_specs.append(pl.BlockSpec((batch, tm, s.shape[1]), lambda i: (0, blk(i), 0)))
            out_shape.append(_sds((batch, n_rows, s.shape[1]), s.dtype))
            continue
        out_specs.append(pl.BlockSpec((tm, s.shape[1]), lambda i: (blk(i), 0)))
        out_shape.append(s)
    for s in out_acc:
        nd = len(s.shape)
        out_specs.append(pl.BlockSpec(s.shape, lambda i, nd=nd: (0,) * nd))
        out_shape.append(s)
    for s in out_exacc:
        out_specs.append(pl.BlockSpec((1,) + s.shape[1:], lambda i: (blk(i) // tpe, 0, 0)))
        out_shape.append(s)
    base_out = len(out_shape)
    out_specs += [any_spec] * len(st_outs)
    out_shape += st_outs
    aliases = {base_in + k: base_out + v for k, v in carry.aliases.items()} if carry else {}
    n = [len(tiled), len(halos), len(exs), len(res), len(st_ins), len(out_tiled), len(out_acc), len(out_exacc),
         len(st_outs), len(scratch), len(st_sems)]

    def body(*refs):
        groups, k = [], 0
        for m in n:
            groups.append(refs[k:k + m])
            k += m
        i = pl.program_id(0)
        b = blk(i)

        class ctx:
            first = i == 0
            pos = b % tpe
            seq_first = (b % tpe) == 0
            seq_last = (b % tpe) == tpe - 1
            ex_enter = (i % tpe) == 0

        work = functools.partial(fn, ctx, *groups[0:4], *groups[5:8], groups[9])
        if carry:
            carry.run((groups[4], groups[8], groups[10]), i == 0, i == min(nt - 1, int(carry_mid * nt)), i == nt - 1, work)
        else:
            work()

    outs = _pcall(body, name=name, grid=(nt,), in_specs=in_specs, out_specs=out_specs, out_shape=out_shape,
                  input_output_aliases=aliases, scratch_shapes=list(scratch) + st_sems,
                  compiler_params=_cparams(("arbitrary",)))(*args)
    if batch:
        outs = [o.reshape(batch * n_rows, o.shape[2]) if k < len(out_tiled) else o for k, o in enumerate(outs)]
    return (outs[:base_out], outs[base_out:]) if carry else outs


def _acc(ref, val, first):
    if first is not None:
        @pl.when(first)
        def _():
            ref[...] = jnp.zeros(ref.shape, ref.dtype)
    ref[...] += val


def _sds(shape, dtype=F32):
    return jax.ShapeDtypeStruct(shape, dtype)


def _ln_fwd(r, g, b):
    mu = jnp.mean(r, axis=-1, keepdims=True)
    rc = r - mu
    var = jnp.mean(rc * rc, axis=-1, keepdims=True)
    return rc * lax.rsqrt(var + LN_EPS) * g + b


def _ln_bwd(r, g, dy):
    mu = jnp.mean(r, axis=-1, keepdims=True)
    rc = r - mu
    var = jnp.mean(rc * rc, axis=-1, keepdims=True)
    rstd = lax.rsqrt(var + LN_EPS)
    xhat = rc * rstd
    dxh = dy * g
    dr = rstd * (dxh - jnp.mean(dxh, axis=-1, keepdims=True) - xhat * jnp.mean(dxh * xhat, axis=-1, keepdims=True))
    return dr, jnp.sum(dy * xhat, axis=0, keepdims=True), jnp.sum(dy, axis=0, keepdims=True)


def _ffn_fwd(name, x, mod, k0, w1, w3, w2, lng, lnb, seq, tgt=None, **carry):
    t = x.shape[0]
    tm = 256
    with_loss = tgt is not None

    def fn(ctx, tiled, halos, exs, res, outs, accs, exaccs, scr):
        x_ref = tiled[0]
        mod_ref, = exs
        w1_ref, w3_ref, w2_ref, g_ref, b_ref = res
        xo_ref, r_ref, h_ref, ab_ref, f_ref = outs[:5]
        xv = x_ref[...]
        sh, sc, g = mod_ref[0, k0:k0 + 1, :], mod_ref[0, k0 + 1:k0 + 2, :], mod_ref[0, k0 + 2:k0 + 3, :]
        h = _mx(xv * (1.0 + sc) + sh)
        h_ref[...] = h
        f = jnp.zeros((tm, D), F32)
        for j in range(2):
            a = _dot(h, w1_ref[:, j * FB:(j + 1) * FB])
            b = _dot(h, w3_ref[:, j * FB:(j + 1) * FB])
            ab_ref[:, j * FB:(j + 1) * FB] = a
            ab_ref[:, FF + j * FB:FF + (j + 1) * FB] = b
            s = a * _sigmoid(a) * b
            f = f + _dot(_mx(s), w2_ref[j * FB:(j + 1) * FB, :])
        f_ref[...] = f
        r = ALPHA * xv + 0.5 * g * f
        r_ref[...] = r
        xo = _ln_fwd(r, g_ref[...], b_ref[...])
        xo_ref[...] = xo
        if with_loss:
            e = xo - tiled[1][...]
            outs[5][...] = e * (1.0 / D)
            _acc(accs[0], jnp.sum(e * e) * jnp.ones((8, 128), F32), ctx.first)

    out_tiled = [_sds((t, D)), _sds((t, D)), _sds((t, D), MXU_DTYPE), _sds((t, 2 * FF)), _sds((t, D))]
    if with_loss:
        out_tiled.append(_sds((t, D)))
    return _rowcall(name, fn, t, tm, seq // tm, tiled=[x] + ([tgt] if with_loss else []), exs=[mod],
                    res=[w1, w3, w2, lng, lnb], out_tiled=out_tiled, out_acc=[_sds((8, 128))] if with_loss else [], **carry)


def _ffn_bwd1(name, dxo, r, ab, f, mod, k0, lng, w2t, seq, **carry):
    t = dxo.shape[0]
    tm = 256

    def fn(ctx, tiled, halos, exs, res, outs, accs, exaccs, scr):
        dxo_ref, r_ref, ab_ref, f_ref = tiled
        mod_ref, = exs
        g_ref, w2t_ref = res
        dr_ref, df_ref, s_ref, dab_ref = outs
        g = mod_ref[0, k0 + 2:k0 + 3, :]
        dr, dgam, dbet = _ln_bwd(r_ref[...], g_ref[...], dxo_ref[...])
        dr_ref[...] = dr
        _acc(accs[0], dgam, ctx.first)
        _acc(accs[1], dbet, ctx.first)
        _acc(exaccs[0].at[0], jnp.sum(0.5 * f_ref[...] * dr, axis=0, keepdims=True), ctx.ex_enter)
        df = _mx(0.5 * g * dr)
        df_ref[...] = df
        for j in range(2):
            ds = _dot(df, w2t_ref[:, j * FB:(j + 1) * FB])
            a = ab_ref[:, j * FB:(j + 1) * FB]
            b = ab_ref[:, FF + j * FB:FF + (j + 1) * FB]
            sig = _sigmoid(a)
            silu = a * sig
            s_ref[:, j * FB:(j + 1) * FB] = _mx(silu * b)
            dab_ref[:, j * FB:(j + 1) * FB] = _mx(ds * b * (sig * (1.0 + a * (1.0 - sig))))
            dab_ref[:, FF + j * FB:FF + (j + 1) * FB] = _mx(ds * silu)

    b = mod.shape[0]
    return _rowcall(name, fn, t, tm, seq // tm, tiled=[dxo, r, ab, f], exs=[mod], res=[lng, w2t],
                    out_tiled=[_sds((t, D)), _sds((t, D), MXU_DTYPE), _sds((t, FF), MXU_DTYPE),
                               _sds((t, 2 * FF), MXU_DTYPE)],
                    out_acc=[_sds((1, D)), _sds((1, D))], out_exacc=[_sds((b, 1, D))], **carry)


def _mod_bwd(name, dab, dr, x, mod, k0, wts, seq, extra=(), **carry):
    t = dr.shape[0]
    tm = 256
    nin = 1 + len(extra)
    width = dab.shape[1] + sum(e.shape[1] for e in extra)

    def fn(ctx, tiled, halos, exs, res, outs, accs, exaccs, scr):
        parts = tiled[:nin]
        dr_ref, x_ref = tiled[nin:]
        mod_ref, = exs
        sc = mod_ref[0, k0 + 1:k0 + 2, :]
        if nin == 1:
            dp = parts[0][...]
        else:
            dp = jnp.concatenate([_mx(p[...]) for p in parts], axis=1)
            outs[1][...] = dp
        dh, off = jnp.zeros((tm, D), F32), 0
        for wt_ref in res:
            dh = dh + _dot(dp[:, off:off + wt_ref.shape[0]], wt_ref[...])
            off += wt_ref.shape[0]
        outs[0][...] = ALPHA * dr_ref[...] + dh * (1.0 + sc)
        _acc(exaccs[0].at[0], jnp.sum(dh, axis=0, keepdims=True), ctx.ex_enter)
        _acc(exaccs[1].at[0], jnp.sum(dh * x_ref[...], axis=0, keepdims=True), ctx.ex_enter)

    b = mod.shape[0]
    out_tiled = [_sds((t, D))] + ([_sds((t, width), MXU_DTYPE)] if nin > 1 else [])
    return _rowcall(name, fn, t, tm, seq // tm, tiled=[dab, *extra, dr, x], exs=[mod], res=list(wts),
                    out_tiled=out_tiled, out_exacc=[_sds((b, 1, D)), _sds((b, 1, D))], **carry)


def _tn_matmul(name, a, b, bm, bn, bt=512, carry=None, a_cols=None):
    t = a.shape[0]
    start, m = a_cols or (0, a.shape[1])
    off = start // bm
    n = b.shape[1]
    grid = (m // bm, n // bn, t // bt)
    n_in, n_out = (len(carry.ins), len(carry.out_shapes)) if carry else (0, 0)

    def body(a_ref, b_ref, *refs):
        o_ref = refs[n_in]

        def work():
            @pl.when(pl.program_id(2) == 0)
            def _():
                o_ref[...] = jnp.zeros(o_ref.shape, F32)
            o_ref[...] += lax.dot_general(a_ref[...], b_ref[...], (((0,), (0,)), ((), ())), preferred_element_type=F32)

        if not carry:
            return work()
        step = (pl.program_id(0) * grid[1] + pl.program_id(1)) * grid[2] + pl.program_id(2)
        carry.run((refs[:n_in], refs[n_in + 1:n_in + 1 + n_out], refs[n_in + 1 + n_out:]), step == 0, step == 0,
                  step == grid[0] * grid[1] * grid[2] - 1, work)

    any_spec = pl.BlockSpec(memory_space=pl.ANY)
    outs = _pcall(body, name=name, grid=grid,
                  in_specs=[pl.BlockSpec((bt, bm), lambda i, j, k: (k, i + off)), pl.BlockSpec((bt, bn), lambda i, j, k: (k, j))]
                  + [any_spec] * n_in,
                  out_specs=[pl.BlockSpec((bm, bn), lambda i, j, k: (i, j))] + [any_spec] * n_out,
                  out_shape=[_sds((m, n))] + (carry.out_shapes if carry else []),
                  input_output_aliases={2 + k: 1 + v for k, v in carry.aliases.items()} if carry else {},
                  scratch_shapes=carry.sem_shapes() if carry else [],
                  compiler_params=_cparams(("arbitrary",) * 3 if carry else ("parallel", "parallel", "arbitrary")))(
                      a, b, *(carry.ins if carry else []))
    return (outs[0], outs[1:]) if carry else outs[0]


def _proj_fwd(name, x, mod, w_in, seq):
    t = x.shape[0]
    tm = 256

    def fn(ctx, tiled, halos, exs, res, outs, accs, exaccs, scr):
        mod_ref, = exs
        sh, sc = mod_ref[0, 3:4, :], mod_ref[0, 4:5, :]
        h = _mx(tiled[0][...] * (1.0 + sc) + sh)
        outs[0][...] = h
        outs[1][...] = _dot(h, res[0][...])

    return _rowcall(name, fn, t, tm, seq // tm, tiled=[x], exs=[mod], res=[w_in],
                    out_tiled=[_sds((t, D), MXU_DTYPE), _sds((t, PW))])


def _shift_rows(cur, prev8, j):
    if j == 0:
        return cur
    rolled = pltpu.roll(cur, j, 0)
    top = jnp.where(_iota((8, cur.shape[1]), 0) < j, pltpu.roll(prev8, j, 0), rolled[0:8])
    return jnp.concatenate([top, rolled[8:]], axis=0)


def _shift_rows_up(cur, next8, j):
    if j == 0:
        return cur
    n = cur.shape[0]
    rolled = pltpu.roll(cur, n - j, 0)
    bot = jnp.where(_iota((8, cur.shape[1]), 0) >= 8 - j, pltpu.roll(next8, 8 - j, 0), rolled[n - 8:n])
    return jnp.concatenate([rolled[:n - 8], bot], axis=0)


def _softplus(x):
    return jnp.maximum(x, 0.0) + jnp.log(1.0 + jnp.exp(-jnp.abs(x)))


def _ssd_common(proj_ref, xpre, dtb_ref, alog_ref):
    xbc = xpre * _sigmoid(xpre)
    xs, bm, cm = xbc[:, 0:SW], xbc[:, SW:SW + 256], xbc[:, SW + 256:SW + 512]
    dtraw = proj_ref[:, PW - 128:PW] + dtb_ref[...]
    dt = _softplus(dtraw)
    a = -jnp.exp(alog_ref[...])
    tril = (_iota((CH, CH), 0) >= _iota((CH, CH), 1)).astype(F32)
    acs = _dot_hi(tril, dt * a)
    return xs, bm, cm, dtraw, dt, a, acs, acs.T


def _pair_lane(lo, hi):
    r = lo.shape[0]
    return jnp.where(_iota((r, 128), 1) < HP, lo, hi)


def _ssd_fwd(name, proj, conv_w, conv_b, dt_bias, a_log, d_rep, norm_w, seq):
    t = proj.shape[0]
    nb = t // seq

    def fn(ctx, tiled, halos, exs, res, outs, accs, exaccs, scr):
        @pl.when(ctx.seq_first)
        def _():
            scr[0][...] = jnp.zeros(scr[0].shape, F32)

        for b in range(nb):
            one(ctx, res, [r.at[b] for r in tiled + halos + outs + scr])

    def one(ctx, res, refs):
        proj_ref, halo_ref, yo_ref, xpre_ref, y_ref, sprev_ref, state_ref = refs
        cw_ref, cb_ref, dtb_ref, alog_ref, d_ref, nw_ref = res
        raw = proj_ref[:, SW:SW + D]
        prev8 = halo_ref[:, SW:SW + D] * jnp.where(ctx.seq_first, 0.0, 1.0)
        xpre = cb_ref[...] + cw_ref[3:4, :] * raw
        for j in (1, 2, 3):
            xpre = xpre + cw_ref[3 - j:4 - j, :] * _shift_rows(raw, prev8, j)
        xpre_ref[...] = xpre
        xs, bm, cm, dtraw, dt, a, acs, acst = _ssd_common(proj_ref, xpre, dtb_ref, alog_ref)
        causal = _iota((CH, CH), 0) >= _iota((CH, CH), 1)
        lane_lo = _iota((CH, 128), 1) < HP
        sprev = state_ref[...]
        sprev_ref[...] = sprev
        ys = []
        for g in range(NG):
            bmg, cmg = bm[:, g * NS:(g + 1) * NS], cm[:, g * NS:(g + 1) * NS]
            bmt = bmg.T
            cb = _dot(_mx(cmg), _mx(bmt))
            for q in (2 * g, 2 * g + 1):
                xsq = xs[:, 128 * q:128 * q + 128]
                xd = xsq * _pair_lane(dt[:, 2 * q:2 * q + 1], dt[:, 2 * q + 1:2 * q + 2])
                sp = sprev[:, 128 * q:128 * q + 128]
                ydiag = jnp.zeros((CH, 128), F32)
                snew = jnp.zeros((NS, 128), F32)
                for jj in range(2):
                    h = 2 * q + jj
                    col, row = acs[:, h:h + 1], acst[h:h + 1, :]
                    lm = jnp.where(causal, jnp.exp(jnp.minimum(col - row, 0.0)), 0.0)
                    xm = _mx(jnp.where(lane_lo if jj == 0 else ~lane_lo, xd, 0.0))
                    ydiag = ydiag + _dot(_mx(cb * lm), xm)
                    dec_row = jnp.exp(acst[h:h + 1, CH - 1:CH] - row)
                    snew = snew + _dot(_mx(bmt * dec_row), xm)
                e_pair = jnp.exp(_pair_lane(acs[:, 2 * q:2 * q + 1], acs[:, 2 * q + 1:2 * q + 2]))
                yoff = _dot(_mx(cmg), _mx(sp)) * e_pair
                cd = jnp.exp(_pair_lane(acst[2 * q:2 * q + 1, CH - 1:CH], acst[2 * q + 1:2 * q + 2, CH - 1:CH]))
                state_ref[:, 128 * q:128 * q + 128] = cd * sp + snew
                ys.append(ydiag + yoff + d_ref[:, 128 * q:128 * q + 128] * xsq)
        y = jnp.concatenate(ys, axis=1)
        y_ref[...] = y
        z = proj_ref[:, 0:SW]
        yz = y * (z * _sigmoid(z))
        outp = []
        for g in range(NG):
            seg = yz[:, 256 * g:256 * g + 256]
            rinv = lax.rsqrt(jnp.mean(seg * seg, axis=-1, keepdims=True) + LN_EPS)
            outp.append(seg * rinv * nw_ref[:, 256 * g:256 * g + 256])
        yo_ref[...] = _mx(jnp.concatenate(outp, axis=1))

    return _rowcall(name, fn, t, CH, seq // CH, tiled=[proj], halos=[(proj, 8)],
                    res=[conv_w, conv_b, dt_bias, a_log, d_rep, norm_w],
                    out_tiled=[_sds((t, SW), MXU_DTYPE), _sds((t, D)), _sds((t, SW)), _sds((t, SW))],
                    scratch=[pltpu.VMEM((nb, NS, SW), F32)], batch=nb)


def _ssd_bwd(name, dyo, proj, xpre_all, y_all, sprev_all, conv_w, dt_bias, a_log, d_rep, norm_w, seq, **carry):
    t = proj.shape[0]
    nb = t // seq

    def fn(ctx, tiled, halos, exs, res, outs, accs, exaccs, scr):
        @pl.when(ctx.seq_last)
        def _():
            for r in scr:
                r[...] = jnp.zeros(r.shape, F32)

        @pl.when(ctx.first)
        def _():
            for r in accs:
                r[...] = jnp.zeros(r.shape, F32)

        for b in range(nb):
            one(ctx, None, res, accs, [r.at[b] for r in tiled + halos + outs + scr])

    def one(ctx, first, res, accs, refs):
        dyo_ref, proj_ref, xpre_ref, y_ref, sprev_ref, halo_ref, dzx_ref, ddt_ref, ds_ref, nxt_ref = refs
        cw_ref, dtb_ref, alog_ref, d_ref, nw_ref = res
        dnw_acc, dd_acc, dcw_acc, dcb_acc, ddtb_acc, dalog_acc = accs
        xpre = xpre_ref[...]
        xs, bm, cm, dtraw, dt, a, acs, acst = _ssd_common(proj_ref, xpre, dtb_ref, alog_ref)
        y = y_ref[...]
        z = proj_ref[:, 0:SW]
        sz = _sigmoid(z)
        siluz = z * sz
        yz = y * siluz
        dyo = dyo_ref[...]
        dyz_parts, dnw_parts = [], []
        for g in range(NG):
            seg = yz[:, 256 * g:256 * g + 256]
            rinv = lax.rsqrt(jnp.mean(seg * seg, axis=-1, keepdims=True) + LN_EPS)
            yn = seg * rinv
            dseg = dyo[:, 256 * g:256 * g + 256]
            dnw_parts.append(jnp.sum(dseg * yn, axis=0, keepdims=True))
            dyn = dseg * nw_ref[:, 256 * g:256 * g + 256]
            dyz_parts.append(rinv * (dyn - yn * jnp.mean(dyn * yn, axis=-1, keepdims=True)))
        dyz = jnp.concatenate(dyz_parts, axis=1)
        _acc(dnw_acc, jnp.concatenate(dnw_parts, axis=1), first)
        dy = dyz * siluz
        dz = dyz * y * (sz * (1.0 + z * (1.0 - sz)))
        _acc(dd_acc, jnp.sum(dy * xs, axis=0, keepdims=True), first)

        causal = _iota((CH, CH), 0) >= _iota((CH, CH), 1)
        anti = _iota((CH, CH), 0) <= _iota((CH, CH), 1)
        lane_lo = _iota((CH, 128), 1) < HP
        lane_id = _iota((CH, 128), 1)
        last_row = _iota((CH, 128), 0) == CH - 1
        sprev = sprev_ref[...]
        dacs = jnp.zeros((CH, 128), F32)
        ddt_x = jnp.zeros((CH, 128), F32)
        dxs_parts, dbm_parts, dcm_parts = [], [], []
        for g in range(NG):
            bmg, cmg = bm[:, g * NS:(g + 1) * NS], cm[:, g * NS:(g + 1) * NS]
            bmt, cmt = bmg.T, cmg.T
            cb = _dot(_mx(cmg), _mx(bmt))
            cbt = _dot(_mx(bmg), _mx(cmt))
            dcb = jnp.zeros((CH, CH), F32)
            dcbt = jnp.zeros((CH, CH), F32)
            dbmg = jnp.zeros((CH, NS), F32)
            dcmg = jnp.zeros((CH, NS), F32)
            for q in (2 * g, 2 * g + 1):
                sl = slice(128 * q, 128 * q + 128)
                xsq = xs[:, sl]
                dtp = _pair_lane(dt[:, 2 * q:2 * q + 1], dt[:, 2 * q + 1:2 * q + 2])
                xd = xsq * dtp
                dyq = dy[:, sl]
                sp = sprev[:, sl]
                dsn = ds_ref[:, sl]
                e_pair = jnp.exp(_pair_lane(acs[:, 2 * q:2 * q + 1], acs[:, 2 * q + 1:2 * q + 2]))
                cd = jnp.exp(_pair_lane(acst[2 * q:2 * q + 1, CH - 1:CH], acst[2 * q + 1:2 * q + 2, CH - 1:CH]))
                dye = dyq * e_pair
                dcmg = dcmg + _dot(_mx(dye), _mx(sp.T))
                dsp = _dot(_mx(cmt), _mx(dye)) + cd * dsn
                yoff = _dot(_mx(cmg), _mx(sp)) * e_pair
                dacs_lane = dyq * yoff
                dxd = jnp.zeros((CH, 128), F32)
                sds = jnp.sum(dsn * sp, axis=0, keepdims=True) * cd
                for jj in range(2):
                    h = 2 * q + jj
                    hm = lane_lo if jj == 0 else ~lane_lo
                    col, row = acs[:, h:h + 1], acst[h:h + 1, :]
                    lm = jnp.where(causal, jnp.exp(jnp.minimum(col - row, 0.0)), 0.0)
                    lmt = jnp.where(anti, jnp.exp(jnp.minimum(row - col, 0.0)), 0.0)
                    xm = _mx(jnp.where(hm, xd, 0.0))
                    dym = _mx(jnp.where(hm, dyq, 0.0))
                    gm = _dot_nt(dym, xm)
                    gmt = _dot_nt(xm, dym)
                    dcb = dcb + gm * lm
                    dcbt = dcbt + gmt * lmt
                    dxd = dxd + _dot(_mx(cbt * lmt), dym)
                    w = gm * cb * lm
                    wt = gmt * cbt * lmt
                    dacs_h = jnp.sum(w, axis=1, keepdims=True) - jnp.sum(wt, axis=1, keepdims=True)
                    alast = acst[h:h + 1, CH - 1:CH]
                    dec_col = jnp.exp(alast - col)
                    dsm = _mx(jnp.where(hm[0:NS], dsn, 0.0))
                    qh = _dot_nt(xm, dsm)
                    dbmg = dbmg + qh * dec_col
                    ddec = jnp.sum(qh * bmg, axis=1, keepdims=True)
                    dxd = dxd + _dot(_mx(bmg * dec_col), dsm)
                    dacs_h = dacs_h - ddec * dec_col
                    dacs_h = dacs_h + jnp.sum(jnp.where(hm, dacs_lane, 0.0), axis=1, keepdims=True)
                    tail = jnp.sum(ddec * dec_col, axis=0, keepdims=True) + jnp.sum(
                        jnp.where(hm[0:1], sds, 0.0), axis=1, keepdims=True)
                    dacs = dacs + jnp.where(lane_id == h, dacs_h, 0.0) + jnp.where(
                        last_row & (lane_id == h), tail, 0.0)
                ds_ref[:, sl] = dsp
                for jj in range(2):
                    h = 2 * q + jj
                    hm = lane_lo if jj == 0 else ~lane_lo
                    ddt_x = ddt_x + jnp.where(lane_id == h, jnp.sum(jnp.where(hm, dxd * xsq, 0.0), axis=1, keepdims=True), 0.0)
                dxs_parts.append(dxd * dtp + d_ref[:, sl] * dyq)
            dcmg = dcmg + _dot(_mx(dcb), _mx(bmg))
            dbmg = dbmg + _dot(_mx(dcbt), _mx(cmg))
            dbm_parts.append(dbmg)
            dcm_parts.append(dcmg)
        triu = (_iota((CH, CH), 0) <= _iota((CH, CH), 1)).astype(F32)
        dadt = _dot_hi(triu, dacs)
        ddt = dadt * a + ddt_x
        _acc(dalog_acc, jnp.sum(dadt * dt, axis=0, keepdims=True) * a, first)
        ddtraw = ddt * _sigmoid(dtraw)
        ddt_ref[...] = ddtraw
        _acc(ddtb_acc, jnp.sum(ddtraw, axis=0, keepdims=True), first)
        dxbc = jnp.concatenate(dxs_parts + dbm_parts + dcm_parts, axis=1)
        sx = _sigmoid(xpre)
        dpre = dxbc * (sx * (1.0 + xpre * (1.0 - sx)))
        _acc(dcb_acc, jnp.sum(dpre, axis=0, keepdims=True), first)
        raw = proj_ref[:, SW:SW + D]
        prev8 = halo_ref[:, SW:SW + D] * jnp.where(ctx.seq_first, 0.0, 1.0)
        next8 = nxt_ref[...]
        draw = cw_ref[3:4, :] * dpre
        dcw = [None] * 4
        dcw[3] = jnp.sum(dpre * raw, axis=0, keepdims=True)
        for j in (1, 2, 3):
            dcw[3 - j] = jnp.sum(dpre * _shift_rows(raw, prev8, j), axis=0, keepdims=True)
            draw = draw + cw_ref[3 - j:4 - j, :] * _shift_rows_up(dpre, next8, j)
        _acc(dcw_acc, jnp.concatenate(dcw + [jnp.zeros((4, D), F32)], axis=0), first)
        nxt_ref[...] = dpre[0:8]
        dzx_ref[:, 0:SW] = dz
        dzx_ref[:, SW:SW + D] = draw

    return _rowcall(name, fn, t, CH, seq // CH, tiled=[dyo, proj, xpre_all, y_all, sprev_all], halos=[(proj, 8)],
                    res=[conv_w, dt_bias, a_log, d_rep, norm_w],
                    out_tiled=[_sds((t, SW + D)), _sds((t, 128))],
                    out_acc=[_sds((1, SW)), _sds((1, SW)), _sds((8, D)), _sds((1, D)), _sds((1, 128)), _sds((1, 128))],
                    scratch=[pltpu.VMEM((nb, NS, SW), F32), pltpu.VMEM((nb, 8, D), F32)], reverse=True, batch=nb, **carry)


def _gelu(y):
    k = math.sqrt(2.0 / math.pi)
    return 0.5 * y * (1.0 + jnp.tanh(k * (y + 0.044715 * y * y * y)))


def _gelu_grad(y):
    k = math.sqrt(2.0 / math.pi)
    th = jnp.tanh(k * (y + 0.044715 * y * y * y))
    return 0.5 * (1.0 + th) + 0.5 * y * (1.0 - th * th) * k * (1.0 + 3.0 * 0.044715 * y * y)


S5T = 256


def _cmul_add(xr, xi, ar, ai, sr, si):
    return xr + ar * sr - ai * si, xi + ar * si + ai * sr


def _s5_fwd(name, proj, bbd, cbd, pw, tab, d5, w_glu, b_glu, seq):
    t = proj.shape[0]
    tm = S5T

    def fn(ctx, tiled, halos, exs, res, outs, accs, exaccs, scr):
        proj_ref, = tiled
        bbd_ref, cbd_ref, pw_ref, tab_ref, d_ref, wg_ref, bg_ref = res
        out_ref, xst_ref, y_ref, xb_ref, ub_ref = outs
        carry_ref, = scr

        @pl.when(ctx.seq_first)
        def _():
            carry_ref[...] = jnp.zeros(carry_ref.shape, F32)

        u = proj_ref[:, 1536:2048]
        bu = _dot(_mx(u), bbd_ref[...])
        xr, xi = bu[:, :S5L].reshape(tm // 8, 8, S5L), bu[:, S5L:].reshape(tm // 8, 8, S5L)
        for k, sh in enumerate((1, 2, 4)):
            xr, xi = _cmul_add(xr, xi, pw_ref[k, :, :S5L], pw_ref[k, :, S5L:], pltpu.roll(xr, sh, 1), pltpu.roll(xi, sh, 1))
        xst_ref[:, :S5L] = xr.reshape(tm, S5L)
        xst_ref[:, S5L:] = xi.reshape(tm, S5L)

        def tile_fix(i, c):
            cr, ci = c
            rows = pl.ds(pl.multiple_of(i * 8, 8), 8)
            tr, ti = _cmul_add(xst_ref[rows, :S5L], xst_ref[rows, S5L:], tab_ref[:, :S5L], tab_ref[:, S5L:], cr, ci)
            xst_ref[rows, :S5L] = tr
            xst_ref[rows, S5L:] = ti
            return tr[7:8], ti[7:8]

        cr, ci = lax.fori_loop(0, tm // 8, tile_fix, (carry_ref[0:1, :S5L], carry_ref[0:1, S5L:]))
        carry_ref[0:1, :S5L] = cr
        carry_ref[0:1, S5L:] = ci
        xb = _mx(xst_ref[...])
        xb_ref[...] = xb
        ub_ref[...] = _mx(u)
        y = _dot(xb, cbd_ref[...]) + u * d_ref[...]
        y_ref[...] = y
        g = _gelu(y)
        v = _dot(_mx(g), wg_ref[...]) + bg_ref[...]
        out_ref[...] = _mx(g * _sigmoid(v))

    return _rowcall(name, fn, t, tm, seq // tm, tiled=[proj], res=[bbd, cbd, pw, tab, d5, w_glu, b_glu],
                    out_tiled=[_sds((t, SW), MXU_DTYPE), _sds((t, 2 * S5L)), _sds((t, SW)),
                               _sds((t, 2 * S5L), MXU_DTYPE), _sds((t, SW), MXU_DTYPE)],
                    scratch=[pltpu.VMEM((8, 2 * S5L), F32)])


def _s5_bwd(name, dout, proj, xst, y_all, bbdt, cbdt, pwc, tabc, d5, w_glu, w_glut, b_glu, seq, **carry):
    t = proj.shape[0]
    tm = S5T

    def fn(ctx, tiled, halos, exs, res, outs, accs, exaccs, scr):
        dout_ref, proj_ref, xst_ref, y_ref = tiled
        halo_ref, = halos
        bbdt_ref, cbdt_ref, pw_ref, tab_ref, d_ref, wg_ref, wgt_ref, bg_ref = res
        du_ref, lam_ref, dyb_ref, gb_ref, dvb_ref = outs
        da_acc, dd_acc, dbg_acc = accs
        carry_ref, lamf_ref = scr

        @pl.when(ctx.seq_last)
        def _():
            carry_ref[...] = jnp.zeros(carry_ref.shape, F32)

        u = proj_ref[:, 1536:2048]
        y = y_ref[...]
        g = _gelu(y)
        v = _dot(_mx(g), wg_ref[...]) + bg_ref[...]
        sg = _sigmoid(v)
        dout = dout_ref[...]
        dv = dout * g * sg * (1.0 - sg)
        dvb = _mx(dv)
        dvb_ref[...] = dvb
        gb_ref[...] = _mx(g)
        _acc(dbg_acc, jnp.sum(dv, axis=0, keepdims=True), ctx.first)
        dg = dout * sg + _dot(dvb, wgt_ref[...])
        dy = dg * _gelu_grad(y)
        dyb = _mx(dy)
        dyb_ref[...] = dyb
        _acc(dd_acc, jnp.sum(dy * u, axis=0, keepdims=True), ctx.first)
        dx = _dot(dyb, cbdt_ref[...])
        xr, xi = dx[:, :S5L].reshape(tm // 8, 8, S5L), dx[:, S5L:].reshape(tm // 8, 8, S5L)
        for k, sh in enumerate((1, 2, 4)):
            xr, xi = _cmul_add(xr, xi, pw_ref[k, :, :S5L], pw_ref[k, :, S5L:], pltpu.roll(xr, 8 - sh, 1),
                               pltpu.roll(xi, 8 - sh, 1))
        lamf_ref[:, :S5L] = xr.reshape(tm, S5L)
        lamf_ref[:, S5L:] = xi.reshape(tm, S5L)

        def tile_fix(i, c):
            cr, ci = c
            rows = pl.ds(pl.multiple_of((tm // 8 - 1 - i) * 8, 8), 8)
            tr, ti = _cmul_add(lamf_ref[rows, :S5L], lamf_ref[rows, S5L:], tab_ref[:, :S5L], tab_ref[:, S5L:], cr, ci)
            lamf_ref[rows, :S5L] = tr
            lamf_ref[rows, S5L:] = ti
            return tr[0:1], ti[0:1]

        cr, ci = lax.fori_loop(0, tm // 8, tile_fix, (carry_ref[0:1, :S5L], carry_ref[0:1, S5L:]))
        carry_ref[0:1, :S5L] = cr
        carry_ref[0:1, S5L:] = ci
        lam = lamf_ref[...]
        lamb = _mx(lam)
        lam_ref[...] = lamb
        du_ref[...] = dy * d_ref[...] + _dot(lamb, bbdt_ref[...])
        prev8 = halo_ref[...] * jnp.where(ctx.seq_first, 0.0, 1.0)
        xprev = _shift_rows(xst_ref[...], prev8, 1)
        lr, li = lam[:, :S5L], lam[:, S5L:]
        pr, pi = xprev[:, :S5L], xprev[:, S5L:]
        dar = jnp.sum(lr * pr + li * pi, axis=0, keepdims=True)
        dai = jnp.sum(li * pr - lr * pi, axis=0, keepdims=True)
        _acc(da_acc, jnp.concatenate([dar, dai], axis=1), ctx.first)

    return _rowcall(name, fn, t, tm, seq // tm, tiled=[dout, proj, xst, y_all], halos=[(xst, 8)],
                    res=[bbdt, cbdt, pwc, tabc, d5, w_glu, w_glut, b_glu],
                    out_tiled=[_sds((t, SW)), _sds((t, 2 * S5L), MXU_DTYPE), _sds((t, SW), MXU_DTYPE),
                               _sds((t, SW), MXU_DTYPE), _sds((t, SW), MXU_DTYPE)],
                    out_acc=[_sds((1, 2 * S5L)), _sds((1, SW)), _sds((1, SW))],
                    scratch=[pltpu.VMEM((8, 2 * S5L), F32), pltpu.VMEM((tm, 2 * S5L), F32)], reverse=True, **carry)


def _out_fwd(name, yssd, ys5, x1, mod, w_out, lng, lnb, seq):
    t = x1.shape[0]
    tm = 256

    def fn(ctx, tiled, halos, exs, res, outs, accs, exaccs, scr):
        ya_ref, yb_ref, x_ref = tiled
        mod_ref, = exs
        w_ref, g_ref, b_ref = res
        m = _dot(ya_ref[...], w_ref[0:SW, :]) + _dot(yb_ref[...], w_ref[SW:2 * SW, :])
        r = ALPHA * x_ref[...] + mod_ref[0, 5:6, :] * m
        outs[0][...] = _ln_fwd(r, g_ref[...], b_ref[...])
        outs[1][...] = r
        outs[2][...] = m

    return _rowcall(name, fn, t, tm, seq // tm, tiled=[yssd, ys5, x1], exs=[mod], res=[w_out, lng, lnb],
                    out_tiled=[_sds((t, D)), _sds((t, D)), _sds((t, D))])


def _out_bwd(name, dxo, r, m, mod, lng, w_outt, seq, **carry):
    t = dxo.shape[0]
    tm = 256

    def fn(ctx, tiled, halos, exs, res, outs, accs, exaccs, scr):
        dxo_ref, r_ref, m_ref = tiled
        mod_ref, = exs
        g_ref, wt_ref = res
        dr, dgam, dbet = _ln_bwd(r_ref[...], g_ref[...], dxo_ref[...])
        outs[0][...] = dr
        _acc(accs[0], dgam, ctx.first)
        _acc(accs[1], dbet, ctx.first)
        _acc(exaccs[0].at[0], jnp.sum(dr * m_ref[...], axis=0, keepdims=True), ctx.ex_enter)
        dm = _mx(mod_ref[0, 5:6, :] * dr)
        outs[1][...] = dm
        dyc = _dot(dm, wt_ref[...])
        outs[2][...] = dyc[:, 0:SW]
        outs[3][...] = dyc[:, SW:2 * SW]

    b = mod.shape[0]
    return _rowcall(name, fn, t, tm, seq // tm, tiled=[dxo, r, m], exs=[mod], res=[lng, w_outt],
                    out_tiled=[_sds((t, D)), _sds((t, D), MXU_DTYPE), _sds((t, SW)), _sds((t, SW))],
                    out_acc=[_sds((1, D)), _sds((1, D))], out_exacc=[_sds((b, 1, D))], **carry)


def _s5_discretise(a_re, a_im, log_dt, b_re, b_im):
    dt = jnp.exp(log_dt)[:, None]
    mag = jnp.exp(dt * a_re)
    ab_re, ab_im = mag * jnp.cos(dt * a_im), mag * jnp.sin(dt * a_im)
    den = a_re * a_re + a_im * a_im
    nr, ni = ab_re - 1.0, ab_im
    f_re, f_im = (nr * a_re + ni * a_im) / den, (ni * a_re - nr * a_im) / den
    bb_re = f_re[..., None] * b_re - f_im[..., None] * b_im
    bb_im = f_re[..., None] * b_im + f_im[..., None] * b_re
    return ab_re, ab_im, bb_re, bb_im


def _s5_tables(ab_re, ab_im):
    ar, ai = ab_re.reshape(1, S5L), ab_im.reshape(1, S5L)
    pows = [(ar, ai)]
    for _ in range(7):
        pr, pi = pows[-1]
        pows.append((pr * ar - pi * ai, pr * ai + pi * ar))

    def pack(rows, sign):
        return jnp.concatenate([jnp.concatenate([r for r, _ in rows], axis=0),
                                jnp.concatenate([sign * i for _, i in rows], axis=0)], axis=1)

    row = jnp.arange(8)[:, None]
    pw = jnp.stack([jnp.where(row >= sh, pack([pows[sh - 1]], 1.0), 0.0) for sh in (1, 2, 4)])
    pwc = jnp.stack([jnp.where(row < 8 - sh, pack([pows[sh - 1]], -1.0), 0.0) for sh in (1, 2, 4)])
    tab = pack(pows, 1.0)
    tabc = pack(pows[::-1], -1.0)
    return pw, tab, pwc, tabc


class _GradGroup:
    def __init__(self, tag, grads, place):
        self.tag, self.names, (self.half, self.chip) = tag, list(grads), place
        self.gsh = [_shard_major(g) if n in COL_SHARDED else g.reshape(4, g.shape[0] // 4, g.shape[1])
                    for n, g in grads.items()]

    def sibling(self):
        return _rs_sibling_stage(self.gsh)

    def chips(self, received):
        self.sums, sums_bf = _rs_add(self.tag + "_add", self.gsh, received, self.half)
        return _rs_chips_stage(sums_bf)

    def join(self, received):
        return _rs_join_stage(_rs_sum(self.tag + "_sum", self.sums, received, self.chip, self.half))

    def result(self, joined):
        return dict(zip(self.names, joined))


def _hid(fn, *args, stage=None, **kw):
    if stage is None:
        return fn(*args, **kw), None
    return fn(*args, carry=stage, **kw)


def _local_step(x, tgt, mod, w, sp, seq, dist=None):
    t = x.shape[0]
    mxu = MXU_DTYPE
    big = {}

    def group(tag, grads):
        if dist is None:
            big.update(grads)
            return None
        return _GradGroup(tag, grads, dist[2])

    (x1, r1, h1, ab1, f1), late = _hid(_ffn_fwd, "ffn1_fwd", x, mod, 0, w["ffn1_w1"], w["ffn1_w3"], w["ffn1_w2"], sp["ln1_g"],
                                       sp["ln1_b"], seq, stage=dist and dist[0], **({"carry_mid": 0.75} if dist else {}))
    if dist:
        w = {**w, **dist[1](late)}
    h2, proj = _proj_fwd("proj_fwd", x1, mod, w["w_in"], seq)
    yssd, xpre, yraw, sprev = _ssd_fwd("ssd_fwd", proj, sp["conv_w"], sp["conv_b"], sp["dt_bias"], sp["a_log"],
                                       sp["d_rep"], sp["ssd_norm_w"], seq)
    (ab_re, ab_im, bb_re, bb_im), disc_vjp = jax.vjp(_s5_discretise, sp["s5_a_re"], sp["s5_a_im"], sp["s5_log_dt"],
                                                     sp["s5_b_re"], sp["s5_b_im"])
    eye = jnp.eye(S5G, dtype=F32)
    bbd = jnp.concatenate([jnp.einsum("gk,gph->ghkp", eye, bb_re).reshape(SW, S5L),
                           jnp.einsum("gk,gph->ghkp", eye, bb_im).reshape(SW, S5L)], axis=1).astype(mxu)
    cbd = jnp.concatenate([jnp.einsum("gk,ghp->gpkh", eye, sp["s5_c_re"]).reshape(S5L, SW),
                           -jnp.einsum("gk,ghp->gpkh", eye, sp["s5_c_im"]).reshape(S5L, SW)], axis=0).astype(mxu)
    pw, tab, pwc, tabc = _s5_tables(lax.stop_gradient(ab_re), lax.stop_gradient(ab_im))
    ys5, xst, y5, xstb, ub = _s5_fwd("s5_fwd", proj, bbd, cbd, pw, tab, sp["s5_d"], w["w_glu"], sp["b_glu"], seq)
    x2, r2, m2 = _out_fwd("out_fwd", yssd, ys5, x1, mod, w["w_out"], sp["ln2_g"], sp["ln2_b"], seq)
    x3, r3, h3, ab3, f3, dy3, loss_acc = _ffn_fwd("ffn2_fwd", x2, mod, 6, w["ffn2_w1"], w["ffn2_w3"], w["ffn2_w2"], sp["ln3_g"],
                                                  sp["ln3_b"], seq, tgt=tgt)
    dr3, df3, s3, dab3, dg3g, dg3b, dgate3 = _ffn_bwd1("ffn2_bwd1", dy3, r3, ab3, f3, mod, 6, sp["ln3_g"],
                                                        w["ffn2_w2t"], seq)
    dx2, dsh3, dsc3 = _mod_bwd("ffn2_bwd2", dab3, dr3, x2, mod, 6, [w["ffn2_w1t"], w["ffn2_w3t"]], seq)
    grp = group("rs_ffn2", dict(ffn2_w1=_tn_matmul("ffn2_dw1", dab3, h3, FB, D, a_cols=(0, FF)),
                                ffn2_w3=_tn_matmul("ffn2_dw3", dab3, h3, FB, D, a_cols=(FF, FF)),
                                ffn2_w2=_tn_matmul("ffn2_dw2", s3, df3, FB, D)))
    (dr2, dm2, dyssd, dys5, dg2g, dg2b, dgate2), got = _hid(_out_bwd, "out_bwd", dx2, r2, m2, mod, sp["ln2_g"],
                                                          w["w_outt"], seq, stage=grp and grp.sibling())
    g_w_out = jnp.concatenate([_tn_matmul("dw_out_a", yssd, dm2, SW, D), _tn_matmul("dw_out_b", ys5, dm2, SW, D)], axis=0)
    (du, lam, dy5b, g5b, dv5b, da5, dd5, dbglu), got = _hid(
        _s5_bwd, "s5_bwd", dys5, proj, xst, y5, bbd.T, cbd.T, pwc, tabc, sp["s5_d"], w["w_glu"], w["w_glut"], sp["b_glu"],
        seq, stage=grp and grp.chips(got))
    g_w_glu = _tn_matmul("dw_glu", g5b, dv5b, SW, SW)
    dbfull = _tn_matmul("s5_db", lam, ub, D, SW)
    dcfull = _tn_matmul("s5_dc", xstb, dy5b, D, SW)
    (dzx, ddt, dnw, ddl, dcw, dcb, ddtb, dalog), got = _hid(
        _ssd_bwd, "ssd_bwd", dyssd, proj, xpre, yraw, sprev, sp["conv_w"], sp["dt_bias"], sp["a_log"], sp["d_rep"],
        sp["ssd_norm_w"], seq, stage=grp and grp.join(got))
    if grp:
        big.update(grp.result(got))
    dx1, dproj, dsh2, dsc2 = _mod_bwd("proj_bwd", dzx, dr2, x1, mod, 3, [w["w_int"]], seq, extra=(du, ddt))
    gwi = _tn_matmul("dw_in", h2, dproj, D, PW)
    grp = group("rs_mix", dict(w_in=jnp.concatenate([gwi[:, :1536], gwi[:, 2048:2056], gwi[:, 1536:2048]], axis=1),
                               w_glu=g_w_glu, w_out=g_w_out))
    (dr1, df1, s1, dab1, dg1g, dg1b, dgate1), got = _hid(_ffn_bwd1, "ffn1_bwd1", dx1, r1, ab1, f1, mod, 0, sp["ln1_g"],
                                                          w["ffn1_w2t"], seq, stage=grp and grp.sibling())
    g1, got = _hid(_tn_matmul, "ffn1_dw1", dab1, h1, FB, D, a_cols=(0, FF), stage=grp and grp.chips(got))
    g3, got = _hid(_tn_matmul, "ffn1_dw3", dab1, h1, FB, D, a_cols=(FF, FF), stage=grp and grp.join(got))
    if grp:
        big.update(grp.result(got))
    g2 = _tn_matmul("ffn1_dw2", s1, df1, FB, D)
    grp = group("rs_ffn1", dict(ffn1_w1=g1, ffn1_w3=g3, ffn1_w2=g2))
    (dx0, dsh1, dsc1), got = _hid(_mod_bwd, "ffn1_bwd2", dab1, dr1, x, mod, 0, [w["ffn1_w1t"], w["ffn1_w3t"]], seq,
                                  stage=grp and grp.sibling())
    if grp:
        got = _run_stage("rs_ffn1_chips", grp.chips(got))
        big.update(grp.result(_run_stage("rs_ffn1_join", grp.join(got))))
    dmod = jnp.concatenate([dsh1, dsc1, dgate1, dsh2, dsc2, dgate2, dsh3, dsc3, dgate3], axis=1)
    d4 = lambda a: a.reshape(S5G, S5P, S5G, S5H)
    dbb_re = jnp.einsum("gpgh->gph", d4(dbfull[:S5L]))
    dbb_im = jnp.einsum("gpgh->gph", d4(dbfull[S5L:]))
    dc_re = jnp.einsum("gpgh->ghp", d4(dcfull[:S5L]))
    dc_im = -jnp.einsum("gpgh->ghp", d4(dcfull[S5L:]))
    g_a_re, g_a_im, g_log_dt, g_b_re, g_b_im = disc_vjp(
        (da5[:, :S5L].reshape(S5G, S5P), da5[:, S5L:].reshape(S5G, S5P), dbb_re, dbb_im))
    small = dict(ln1_g=dg1g, ln1_b=dg1b, ln2_g=dg2g, ln2_b=dg2b, ln3_g=dg3g, ln3_b=dg3b, conv_w=dcw[0:4], conv_b=dcb,
                 dt_bias=ddtb[:, :NH], a_log=dalog[:, :NH], d_ssd=jnp.sum(ddl.reshape(NH, HP), axis=1).reshape(1, NH),
                 ssd_norm_w=dnw, s5_a_re=g_a_re, s5_a_im=g_a_im, s5_log_dt=g_log_dt, s5_b_re=g_b_re, s5_b_im=g_b_im,
                 s5_c_re=dc_re, s5_c_im=dc_im, s5_d=dd5, w_glu_b=dbglu)
    return loss_acc[0, 0], dx0, dmod, big, small


def _place():
    return lax.axis_index("x"), lax.axis_index("y"), lax.axis_index("c")


def _other_chips(x, y):
    return [(1 - x, y), (x, 1 - y), (1 - x, 1 - y)]


def _allgather8(name, a):
    r, n = a.shape

    def body(x_ref, out_ref, send_sems, recv_sems, local_sem):
        x, y, c = _place()
        me, sibling = (x, y, c), (x, y, 1 - c)
        chips = _other_chips(x, y)

        def rows(px, py, pc):
            return out_ref.at[pl.ds(pl.multiple_of((4 * px + 2 * py + pc) * r, 8), r), :]

        def copy(k, block, to, src=None):
            return pltpu.make_async_remote_copy(src_ref=rows(*block) if src is None else src, dst_ref=rows(*block),
                                                send_sem=send_sems.at[k], recv_sem=recv_sems.at[k], device_id=to,
                                                device_id_type=MESH_T)

        mine = pltpu.make_async_copy(x_ref, rows(*me), local_sem)
        mine.start()
        first = [copy(0, me, sibling, src=x_ref)]
        first += [copy(1 + j, me, (*chip, c), src=x_ref) for j, chip in enumerate(chips)]
        for cp in first:
            cp.start()
        passed = [copy(4 + j, (*chip, c), sibling) for j, chip in enumerate(chips)]
        for j, chip in enumerate(chips):
            copy(1 + j, (*chip, c), me).wait_recv()
            passed[j].start()
        copy(0, sibling, me).wait_recv()
        for j, chip in enumerate(chips):
            copy(4 + j, (*chip, 1 - c), me).wait_recv()
        for cp in first + passed:
            cp.wait_send()
        mine.wait()

    out = _pcall(body, name=name, out_shape=_sds((8 * r, n), a.dtype),
                 in_specs=[pl.BlockSpec(memory_space=pltpu.VMEM)], out_specs=pl.BlockSpec(memory_space=pltpu.VMEM),
                 scratch_shapes=[pltpu.SemaphoreType.DMA((7,)), pltpu.SemaphoreType.DMA((7,)), pltpu.SemaphoreType.DMA],
                 compiler_params=_cparams())(a)
    return out.reshape(8, r, n)


class _Stage:
    def __init__(self, ins, out_shapes, n_sems, start, finish, mid=None, aliases=None):
        self.ins, self.out_shapes, self.n_sems = list(ins), list(out_shapes), tuple(n_sems)
        self.start, self.mid, self.finish = start, mid, finish
        self.aliases = dict(aliases or {})

    def sem_shapes(self):
        return [pltpu.SemaphoreType.DMA((n,)) for n in self.n_sems]

    def run(self, refs, at_start=None, at_mid=None, at_finish=None, between=None):
        if between is None:
            for part in (self.start, self.mid, self.finish):
                if part is not None:
                    part(*refs)
            return
        pl.when(at_start)(functools.partial(self.start, *refs))
        if self.mid is not None:
            pl.when(at_mid)(functools.partial(self.mid, *refs))
        between()
        pl.when(at_finish)(functools.partial(self.finish, *refs))


def _run_stage(name, st):
    n_in, n_out = len(st.ins), len(st.out_shapes)

    def body(*refs):
        st.run((refs[:n_in], refs[n_in:n_in + n_out], refs[n_in + n_out:]))

    any_spec = pl.BlockSpec(memory_space=pl.ANY)
    return _pcall(body, name=name, out_shape=st.out_shapes, in_specs=[any_spec] * n_in, out_specs=[any_spec] * n_out,
                  input_output_aliases=st.aliases, scratch_shapes=st.sem_shapes(), compiler_params=_cparams())(*st.ins)


def _rows(ref_rows, half, align):
    hr = ref_rows // 2
    return pl.ds(pl.multiple_of(half * hr, align), hr)


def _gather_stage(shards):
    n = len(shards)
    pairs = [(i, j) for i in range(n) for j in range(3)]

    def env(ins, outs, sems):
        x, y, c = _place()
        chips = _other_chips(x, y)

        def copy(i, k, chip, half, to, src=None):
            dst = outs[i].at[2 * chip[0] + chip[1], _rows(shards[i].shape[0], half, 16)]
            return pltpu.make_async_remote_copy(src_ref=dst if src is None else src, dst_ref=dst,
                                                send_sem=sems[0].at[6 * i + k], recv_sem=sems[1].at[6 * i + k],
                                                device_id=to, device_id_type=MESH_T)

        def first(i, j):
            return copy(i, j, (x, y), c, (*chips[j], c), src=ins[i].at[_rows(shards[i].shape[0], c, 16)])

        def passed(i, j, half):
            return copy(i, 3 + j, chips[j], half, (x, y, 1 - c))

        def landed(i, j):
            return copy(i, j, chips[j], c, (x, y, 1 - c))

        return c, first, passed, landed

    def start(ins, outs, sems):
        c, first, passed, landed = env(ins, outs, sems)
        for i, j in pairs:
            first(i, j).start()

    def mid(ins, outs, sems):
        c, first, passed, landed = env(ins, outs, sems)
        for i, j in pairs:
            landed(i, j).wait_recv()
            passed(i, j, c).start()

    def finish(ins, outs, sems):
        c, first, passed, landed = env(ins, outs, sems)
        for i, j in pairs:
            passed(i, j, 1 - c).wait_recv()
        for i, j in pairs:
            first(i, j).wait_send()
            passed(i, j, c).wait_send()

    return _Stage(shards, [_sds((4,) + s.shape, s.dtype) for s in shards], (6 * n, 6 * n), start, finish, mid)


def _rs_sibling_stage(gs):
    n = len(gs)

    def copies(ins, outs, sems):
        x, y, c = _place()
        return [pltpu.make_async_remote_copy(src_ref=ins[i].at[:, _rows(gs[i].shape[1], 1 - c, 8)], dst_ref=outs[i],
                                             send_sem=sems[0].at[i], recv_sem=sems[1].at[i], device_id=(x, y, 1 - c),
                                             device_id_type=MESH_T) for i in range(n)]

    def start(*refs):
        for cp in copies(*refs):
            cp.start()

    def finish(*refs):
        for cp in copies(*refs):
            cp.wait()

    return _Stage(gs, [_sds((4, g.shape[1] // 2, g.shape[2]), g.dtype) for g in gs], (n, n), start, finish)


def _rs_chips_stage(hs):
    n = len(hs)

    def copies(ins, outs, sems):
        x, y, c = _place()
        return [pltpu.make_async_remote_copy(src_ref=ins[i].at[2 * chip[0] + chip[1]], dst_ref=outs[i].at[j],
                                             send_sem=sems[0].at[3 * i + j], recv_sem=sems[1].at[3 * i + j],
                                             device_id=(*chip, c), device_id_type=MESH_T)
                for i in range(n) for j, chip in enumerate(_other_chips(x, y))]

    def start(*refs):
        for cp in copies(*refs):
            cp.start()

    def finish(*refs):
        for cp in copies(*refs):
            cp.wait()

    return _Stage(hs, [_sds((3,) + h.shape[1:], h.dtype) for h in hs], (3 * n, 3 * n), start, finish)


def _rs_join_stage(fs):
    n = len(fs)

    def copy(outs, sems, i, half):
        x, y, c = _place()
        part = outs[i].at[_rows(fs[i].shape[0], c if half == "mine" else 1 - c, 8)]
        return pltpu.make_async_remote_copy(src_ref=part, dst_ref=part, send_sem=sems[0].at[i], recv_sem=sems[1].at[i],
                                            device_id=(x, y, 1 - c), device_id_type=MESH_T)

    def start(ins, outs, sems):
        for i in range(n):
            copy(outs, sems, i, "mine").start()

    def finish(ins, outs, sems):
        for i in range(n):
            copy(outs, sems, i, "theirs").wait_recv()
        for i in range(n):
            copy(outs, sems, i, "mine").wait_send()

    return _Stage(fs, [_sds(f.shape, f.dtype) for f in fs], (n, n), start, finish, aliases={i: i for i in range(n)})


def _row_block(r, cap=2048):
    b = min(r, cap)
    while r % b or b % 8:
        b -= 8
    return b


RS_SPLIT = 2


def _rs_add(name, gs, r1s, sel):
    n = len(gs)

    def body(sel_ref, *refs):
        g_refs, r_refs, h_refs, b_refs = (refs[k * n:(k + 1) * n] for k in range(4))
        for i in range(n):
            h = g_refs[i][...] + r_refs[i][...]
            h_refs[i][...] = h
            b_refs[i][...] = h.astype(BF16)

    def blk(g):
        return (1, g.shape[1] // 2 // RS_SPLIT, g.shape[2])

    here = lambda k, j, s: (k, j, 0)
    in_specs = [pl.BlockSpec(blk(g), lambda k, j, s: (k, s[0] * RS_SPLIT + j, 0)) for g in gs]
    in_specs += [pl.BlockSpec(blk(g), here) for g in gs]
    outs = _pcall(body, name=name,
                  out_shape=[_sds((4, g.shape[1] // 2, g.shape[2])) for g in gs]
                  + [_sds((4, g.shape[1] // 2, g.shape[2]), BF16) for g in gs],
                  grid_spec=pltpu.PrefetchScalarGridSpec(num_scalar_prefetch=1, grid=(4, RS_SPLIT), in_specs=in_specs,
                                                         out_specs=[pl.BlockSpec(blk(g), here) for g in gs] * 2),
                  compiler_params=_cparams(("parallel", "parallel")))(sel.reshape(1).astype(jnp.int32), *gs, *r1s)
    return outs[:n], outs[n:]


def _rs_sum(name, hs, r2s, chip, half):
    n = len(hs)

    def body(sel_ref, *refs):
        h_refs, r_refs, o_refs = (refs[k * n:(k + 1) * n] for k in range(3))
        for i in range(n):
            r = r_refs[i]
            o_refs[i][...] = ((h_refs[i][0] + r[0].astype(F32)) + r[1].astype(F32)) + r[2].astype(F32)

    def rows(h):
        return h.shape[1] // RS_SPLIT

    in_specs = [pl.BlockSpec((1, rows(h), h.shape[2]), lambda j, s: (s[0], j, 0)) for h in hs]
    in_specs += [pl.BlockSpec((3, rows(h), h.shape[2]), lambda j, s: (0, j, 0)) for h in hs]
    sel = jnp.stack([chip, half]).astype(jnp.int32)
    return _pcall(body, name=name, out_shape=[_sds((2 * h.shape[1], h.shape[2])) for h in hs],
                  grid_spec=pltpu.PrefetchScalarGridSpec(
                      num_scalar_prefetch=1, grid=(RS_SPLIT,), in_specs=in_specs,
                      out_specs=[pl.BlockSpec((rows(h), h.shape[2]), lambda j, s: (s[1] * RS_SPLIT + j, 0)) for h in hs]),
                  compiler_params=_cparams(("parallel",)))(sel, *hs, *r2s)


def _sum8(name, a):
    _, r, n = a.shape
    br = _row_block(r)

    def body(a_ref, o_ref):
        acc = a_ref[0]
        for k in range(1, 8):
            acc = acc + a_ref[k]
        o_ref[...] = acc

    return _pcall(body, name=name, out_shape=_sds((r, n)), grid=(r // br,),
                  in_specs=[pl.BlockSpec((8, br, n), lambda j: (0, j, 0))], out_specs=pl.BlockSpec((br, n), lambda j: (j, 0)),
                  compiler_params=_cparams(("parallel",)))(a)


def _adamw(name, ws, gs, ms, vs, nblk):
    n = len(ws)

    def body(*refs):
        w_refs, g_refs, m_refs, v_refs, d_refs, nm_refs, nv_refs = (refs[k * n:(k + 1) * n] for k in range(7))
        for i in range(n):
            gv = g_refs[i][...]
            nm = ADAM_B1 * m_refs[i][...] + (1.0 - ADAM_B1) * gv
            nv = ADAM_B2 * v_refs[i][...] + (1.0 - ADAM_B2) * (gv * gv)
            nm_refs[i][...] = nm
            nv_refs[i][...] = nv
            m_hat = nm / (1.0 - ADAM_B1 ** ADAM_STEP)
            v_hat = nv / (1.0 - ADAM_B2 ** ADAM_STEP)
            d_refs[i][...] = -ADAM_LR * (m_hat / (jnp.sqrt(v_hat) + ADAM_EPS) + ADAM_WD * w_refs[i][...])

    specs = [pl.BlockSpec((w.shape[0] // nblk, w.shape[1]), lambda j: (j, 0)) for w in ws]
    outs = _pcall(body, name=name, out_shape=[_sds(w.shape) for w in ws] * 3, grid=(nblk,), in_specs=specs * 4,
                  out_specs=specs * 3, compiler_params=_cparams(("parallel",)))(*ws, *gs, *ms, *vs)
    return outs[:n], outs[n:2 * n], outs[2 * n:]


ADA_COLS = 2304
ADA_BLK = 768


def _ada_fwd(name, c_all, w_shard, b_cols):
    nb = c_all.shape[0]

    def body(c_ref, w_ref, b_ref, o_ref):
        cv = c_ref[...]
        cs = _mx(cv * _sigmoid(cv))
        o_ref[...] = _dot(cs, _mx(w_ref[...])) + b_ref[...]

    return _pcall(body, name=name, out_shape=_sds((nb, ADA_COLS)), grid=(ADA_COLS // ADA_BLK,),
                  in_specs=[pl.BlockSpec((nb, D), lambda j: (0, 0)), pl.BlockSpec((D, ADA_BLK), lambda j: (0, j)),
                            pl.BlockSpec((1, ADA_BLK), lambda j: (0, j))],
                  out_specs=pl.BlockSpec((nb, ADA_BLK), lambda j: (0, j)),
                  compiler_params=_cparams(("parallel",)))(c_all, w_shard, b_cols)


def _ada_bwd(name, c_all, dmod_cols, dmod_all):
    nb = c_all.shape[0]

    def body(c_ref, dc_ref, da_ref, gw_ref, gb_ref):
        cv = c_ref[...]
        cs = _mx(cv * _sigmoid(cv))
        gw_ref[...] = lax.dot_general(cs, _mx(dc_ref[...]), (((0,), (0,)), ((), ())), preferred_element_type=F32)

        @pl.when(pl.program_id(0) == 0)
        def _():
            gb_ref[...] = jnp.sum(da_ref[...], axis=0, keepdims=True)

    return _pcall(body, name=name, out_shape=[_sds((D, ADA_COLS)), _sds((1, 9 * D))], grid=(ADA_COLS // ADA_BLK,),
                  in_specs=[pl.BlockSpec((nb, D), lambda j: (0, 0)), pl.BlockSpec((nb, ADA_BLK), lambda j: (0, j)),
                            pl.BlockSpec((nb, 9 * D), lambda j: (0, 0))],
                  out_specs=[pl.BlockSpec((D, ADA_BLK), lambda j: (0, j)), pl.BlockSpec((1, 9 * D), lambda j: (0, 0))],
                  compiler_params=_cparams(("arbitrary",)))(c_all, dmod_cols, dmod_all)


BIG = ("ffn1_w1", "ffn1_w3", "ffn1_w2", "w_in", "w_glu", "w_out", "ffn2_w1", "ffn2_w3", "ffn2_w2")
COL_SHARDED = ("w_in",)
TRANSPOSED = ("ffn1_w1", "ffn1_w3", "ffn2_w1", "ffn2_w3")
SMALL = ("b_ada", "ln1_g", "ln1_b", "conv_w", "conv_b", "dt_bias", "a_log", "d_ssd", "ssd_norm_w", "s5_a_re", "s5_a_im",
         "s5_log_dt", "s5_b_re", "s5_b_im", "s5_c_re", "s5_c_im", "s5_d", "b_glu", "ln2_g", "ln2_b", "ln3_g", "ln3_b")
WEIGHTS = ("w_ada", "b_ada", "ffn1_w1", "ffn1_w3", "ffn1_w2", "ln1_g", "ln1_b", "w_in", "conv_w", "conv_b", "dt_bias",
           "a_log", "d_ssd", "ssd_norm_w", "s5_a_re", "s5_a_im", "s5_log_dt", "s5_b_re", "s5_b_im", "s5_c_re", "s5_c_im",
           "s5_d", "w_glu", "b_glu", "w_out", "ln2_g", "ln2_b", "ffn2_w1", "ffn2_w3", "ffn2_w2", "ln3_g", "ln3_b")
BIG_PAD = 2 * 1024 * 128


def _pack(arrs, mult, axis_keep=0):
    lead = arrs[0].shape[:axis_keep]
    flat = jnp.concatenate([a.reshape(lead + (-1,)) for a in arrs], axis=axis_keep)
    pad = (-flat.shape[-1]) % mult
    if pad:
        flat = jnp.concatenate([flat, jnp.zeros(lead + (pad,), flat.dtype)], axis=axis_keep)
    return flat


def _unpack(flat, shapes):
    out, off = [], 0
    for s in shapes:
        size = math.prod(s)
        out.append(flat[..., off:off + size].reshape(flat.shape[:-1] + tuple(s)))
        off += size
    return out


def _shard_major(a):
    rows, cols = a.shape
    return a.reshape(rows, 4, cols // 4).transpose(1, 0, 2)


def _from_shard_major(a):
    _, rows, w = a.shape
    return a.transpose(1, 0, 2).reshape(rows, 4 * w)


def kernel(x, c, w_ada, b_ada, ffn1_w1, ffn1_w3, ffn1_w2, ln1_g, ln1_b, w_in, conv_w, conv_b, dt_bias, a_log, d_ssd, ssd_norm_w, s5_a_re, s5_a_im, s5_log_dt, s5_b_re, s5_b_im, s5_c_re, s5_c_im, s5_d, w_glu, b_glu, w_out, ln2_g, ln2_b, ffn2_w1, ffn2_w3, ffn2_w2, ln3_g, ln3_b, loss_target, m_w_ada, m_b_ada, m_ffn1_w1, m_ffn1_w3, m_ffn1_w2, m_ln1_g, m_ln1_b, m_w_in, m_conv_w, m_conv_b, m_dt_bias, m_a_log, m_d_ssd, m_ssd_norm_w, m_s5_a_re, m_s5_a_im, m_s5_log_dt, m_s5_b_re, m_s5_b_im, m_s5_c_re, m_s5_c_im, m_s5_d, m_w_glu, m_b_glu, m_w_out, m_ln2_g, m_ln2_b, m_ffn2_w1, m_ffn2_w3, m_ffn2_w2, m_ln3_g, m_ln3_b, v_w_ada, v_b_ada, v_ffn1_w1, v_ffn1_w3, v_ffn1_w2, v_ln1_g, v_ln1_b, v_w_in, v_conv_w, v_conv_b, v_dt_bias, v_a_log, v_d_ssd, v_ssd_norm_w, v_s5_a_re, v_s5_a_im, v_s5_log_dt, v_s5_b_re, v_s5_b_im, v_s5_c_re, v_s5_c_im, v_s5_d, v_w_glu, v_b_glu, v_w_out, v_ln2_g, v_ln2_b, v_ffn2_w1, v_ffn2_w3, v_ffn2_w2, v_ln3_g, v_ln3_b):
    a = dict(locals())
    xi, yi, ci = _place()
    chip = 2 * xi + yi
    dev = 2 * chip + ci
    nb, seq, _ = x.shape
    t = nb * seq
    ndev = 8

    c_all = _allgather8("gather_c", c.reshape(-1, 128)).reshape(ndev * nb, D)
    b_cols = lax.dynamic_slice(b_ada, (0, chip * ADA_COLS), (1, ADA_COLS))
    mod_part = _ada_fwd("ada_fwd", c_all, w_ada[0], b_cols)
    mod_parts = _allgather8("gather_mod", mod_part.reshape(-1, 128)).reshape(ndev, ndev * nb, ADA_COLS)
    mod_all = mod_parts[0::2].transpose(1, 0, 2).reshape(ndev * nb, 9 * D)
    mod = lax.dynamic_slice(mod_all, (nb * dev, 0), (nb, 9 * D)).reshape(nb, 9, D)

    def nat(n):
        return jnp.swapaxes(a[n], 1, 2)[0] if n[-7:] in TRANSPOSED else a[n][0]

    def gather(names):
        own = [nat(n).astype(MXU_DTYPE) for n in names]

        def weights(pieces):
            w = {}
            for n, mine, piece in zip(names, own, pieces):
                piece = lax.dynamic_update_slice(piece, mine[None], (chip, 0, 0))
                if n in COL_SHARDED:
                    wi = _from_shard_major(piece)
                    w[n] = jnp.concatenate([wi[:, :1536], wi[:, 1544:2056], wi[:, 1536:1544],
                                            jnp.zeros((D, 120), wi.dtype)], axis=1)
                else:
                    w[n + "t" if n in TRANSPOSED else n] = piece.reshape(-1, piece.shape[-1])
            for n in names:
                have, want = (n + "t", n) if n in TRANSPOSED else (n, n + "t")
                w[want] = w[have].T
            return w

        return _gather_stage(own), weights

    first_stage, first_weights = gather(BIG[:3])
    w = first_weights(_run_stage("gather_w_ffn1", first_stage))
    late_stage, late_weights = gather(BIG[3:])

    cw_all = _allgather8("gather_conv_w", _pack([conv_w[0]], 1024).reshape(-1, 128)).reshape(ndev, -1)[0::2, :4 * 256]
    conv_full = _from_shard_major(cw_all.reshape(4, 4, 256))
    pad8 = lambda v: jnp.concatenate([v.reshape(1, NH), jnp.zeros((1, 128 - NH), F32)], axis=1)
    sp = dict(ln1_g=ln1_g, ln1_b=ln1_b, ln2_g=ln2_g, ln2_b=ln2_b, ln3_g=ln3_g, ln3_b=ln3_b, conv_w=conv_full,
              conv_b=conv_b, dt_bias=pad8(dt_bias), a_log=pad8(a_log), d_rep=jnp.repeat(d_ssd[0], HP)[None],
              ssd_norm_w=ssd_norm_w, s5_a_re=s5_a_re[0], s5_a_im=s5_a_im[0], s5_log_dt=s5_log_dt[0], s5_b_re=s5_b_re[0],
              s5_b_im=s5_b_im[0], s5_c_re=s5_c_re[0], s5_c_im=s5_c_im[0], s5_d=s5_d, b_glu=b_glu)

    lsum, dx0, dmod, gbig, small = _local_step(x.reshape(t, D), loss_target.reshape(t, D), mod, w, sp, seq,
                                               dist=(late_stage, late_weights, (ci, chip)))
    loss = lax.psum(lsum * (0.5 / D), ("x", "y", "c"))

    dmod_all = _allgather8("gather_dmod", dmod.reshape(-1, 128)).reshape(ndev * nb, 9 * D)
    dmod_cols = lax.dynamic_slice(dmod_all, (0, chip * ADA_COLS), (ndev * nb, ADA_COLS))
    g_w_ada, g_b_ada = _ada_bwd("ada_bwd", c_all, dmod_cols, dmod_all)

    gbig["w_ada"] = g_w_ada

    outs = {}
    for call, names in (("adamw_a", ("ffn1_w1", "ffn1_w3", "ffn1_w2", "w_in", "w_glu", "w_out")),
                        ("adamw_b", ("ffn2_w1", "ffn2_w3", "ffn2_w2", "w_ada"))):
        res = _adamw(call, [nat(n) for n in names], [gbig[n] for n in names], [nat("m_" + n) for n in names],
                     [nat("v_" + n) for n in names], 8)
        for kind, arrs in zip(("grad", "delta", "new_m", "new_v"), ([gbig[n] for n in names],) + tuple(res)):
            for n, arr in zip(names, arrs):
                outs[kind, n] = (arr.T if n in TRANSPOSED else arr)[None]

    snames = [n for n in SMALL if n != "b_ada"]
    sgrad = dict(small)
    sgrad["b_glu"] = small["w_glu_b"]
    svec = _pack([sgrad[n] for n in snames], 1024).reshape(-1, 128)
    ssum = _sum8("small_sum", _allgather8("gather_small", svec)).reshape(-1)

    def view2d(u):
        s = u.shape[1:]
        return u.reshape((1, s[0]) if len(s) == 1 else (-1, s[-1]))

    vshape = {n: view2d(a[n]).shape for n in SMALL}
    gsm = dict(zip(snames, _unpack(ssum, [vshape[n] if n != "conv_w" else (4, D) for n in snames])))
    gsm["conv_w"] = lax.dynamic_slice(gsm["conv_w"], (0, chip * 256), (4, 256))
    gsm["b_ada"] = g_b_ada
    res = _adamw("adamw_small", [view2d(a[n]) for n in SMALL], [gsm[n] for n in SMALL],
                 [view2d(a["m_" + n]) for n in SMALL], [view2d(a["v_" + n]) for n in SMALL], 1)
    for kind, arrs in zip(("grad", "delta", "new_m", "new_v"), ([gsm[n] for n in SMALL],) + tuple(res)):
        for n, arr in zip(SMALL, arrs):
            outs[kind, n] = arr.reshape(a[n].shape)

    res = [loss, dx0.reshape(nb, seq, D)]
    for kind in ("grad", "delta", "new_m", "new_v"):
        res += [outs[kind, n] for n in WEIGHTS]
    return tuple(res)
```

```python
import functools
import math

import jax
import jax.numpy as jnp
from jax import lax
from jax.experimental import pallas as pl
from jax.experimental.pallas import tpu as pltpu

F32 = jnp.float32
BF16 = jnp.bfloat16
MXU_DTYPE = jnp.bfloat16

D = 1024
FF = 2816
FB = 1408
NH, HP, NS, NG = 8, 64, 128, 2
CH = 128
SW = 512
S5G, S5P, S5H = 32, 64, 16
S5L = S5G * S5P
PW = 2176
ALPHA = 2.0 ** 0.25
LN_EPS = 1e-5
ADAM_LR, ADAM_B1, ADAM_B2, ADAM_EPS, ADAM_WD, ADAM_STEP = 0.001, 0.9, 0.999, 1e-08, 0.01, 10
VMEM_LIMIT = 56 * 1024 * 1024
MESH_T = pl.DeviceIdType.MESH


def _pcall(body, **kw):
    return pl.pallas_call(body, **kw)


def _cparams(sem=None, **kw):
    return pltpu.CompilerParams(dimension_semantics=sem, vmem_limit_bytes=VMEM_LIMIT, **kw)


def _dot(a, b):
    return jnp.dot(a, b, preferred_element_type=F32)


def _dot_nt(a, b):
    return lax.dot_general(a, b, (((1,), (1,)), ((), ())), preferred_element_type=F32)


def _dot_hi(a, b):
    return jnp.dot(a, b, preferred_element_type=F32, precision=lax.Precision.HIGHEST)


def _mx(a):
    return a.astype(MXU_DTYPE)


def _sigmoid(x):
    return 1.0 / (1.0 + jnp.exp(-x))


def _iota(shape, axis):
    return lax.broadcasted_iota(jnp.int32, shape, axis)


def _rowcall(name, fn, n_rows, tm, tpe, *, tiled=(), halos=(), exs=(), res=(), out_tiled=(), out_acc=(),
             out_exacc=(), scratch=(), reverse=False, batch=None, carry=None, carry_mid=0.0):
    if batch:
        n_rows //= batch
    nt = n_rows // tm

    def blk(i):
        return (nt - 1 - i) if reverse else i

    in_specs, args = [], []
    for a in tiled:
        if batch:
            in_specs.append(pl.BlockSpec((batch, tm, a.shape[1]), lambda i: (0, blk(i), 0)))
            args.append(a.reshape(batch, n_rows, a.shape[1]))
            continue
        in_specs.append(pl.BlockSpec((tm, a.shape[1]), lambda i: (blk(i), 0)))
        args.append(a)
    for a, rows in halos:
        r = tm // rows
        if batch:
            in_specs.append(pl.BlockSpec((batch, rows, a.shape[1]), lambda i, r=r: (0, jnp.maximum(blk(i) * r - 1, 0), 0)))
            args.append(a.reshape(batch, n_rows, a.shape[1]))
            continue
        in_specs.append(pl.BlockSpec((rows, a.shape[1]), lambda i, r=r: (jnp.maximum(blk(i) * r - 1, 0), 0)))
        args.append(a)
    for a in exs:
        in_specs.append(pl.BlockSpec((1,) + a.shape[1:], lambda i: (blk(i) // tpe, 0, 0)))
        args.append(a)
    for a in res:
        nd = a.ndim
        in_specs.append(pl.BlockSpec(a.shape, lambda i, nd=nd: (0,) * nd, pipeline_mode=pl.Buffered(1)))
        args.append(a)
    any_spec = pl.BlockSpec(memory_space=pl.ANY)
    st_ins = carry.ins if carry else []
    st_outs = carry.out_shapes if carry else []
    st_sems = carry.sem_shapes() if carry else []
    base_in = len(args)
    in_specs += [any_spec] * len(st_ins)
    args += st_ins
    out_specs, out_shape = [], []
    for s in out_tiled:
        if batch:
            out_specs.append(pl.BlockSpec((batch, tm, s.shape[1]), lambda i: (0, blk(i), 0)))
            out_shape.append(_sds((batch, n_rows, s.shape[1]), s.dtype))
            continue
        out_specs.append(pl.BlockSpec((tm, s.shape[1]), lambda i: (blk(i), 0)))
        out_shape.append(s)
    for s in out_acc:
        nd = len(s.shape)
        out_specs.append(pl.BlockSpec(s.shape, lambda i, nd=nd: (0,) * nd))
        out_shape.append(s)
    for s in out_exacc:
        out_specs.append(pl.BlockSpec((1,) + s.shape[1:], lambda i: (blk(i) // tpe, 0, 0)))
        out_shape.append(s)
    base_out = len(out_shape)
    out_specs += [any_spec] * len(st_outs)
    out_shape += st_outs
    aliases = {base_in + k: base_out + v for k, v in carry.aliases.items()} if carry else {}
    n = [len(tiled), len(halos), len(exs), len(res), len(st_ins), len(out_tiled), len(out_acc), len(out_exacc),
         len(st_outs), len(scratch), len(st_sems)]

    def body(*refs):
        groups, k = [], 0
        for m in n:
            groups.append(refs[k:k + m])
            k += m
        i = pl.program_id(0)
        b = blk(i)

        class ctx:
            first = i == 0
            pos = b % tpe
            seq_first = (b % tpe) == 0
            seq_last = (b % tpe) == tpe - 1
            ex_enter = (i % tpe) == 0

        work = functools.partial(fn, ctx, *groups[0:4], *groups[5:8], groups[9])
        if carry:
            carry.run((groups[4], groups[8], groups[10]), i == 0, i == min(nt - 1, int(carry_mid * nt)), i == nt - 1, work)
        else:
            work()

    outs = _pcall(body, name=name, grid=(nt,), in_specs=in_specs, out_specs=out_specs, out_shape=out_shape,
                  input_output_aliases=aliases, scratch_shapes=list(scratch) + st_sems,
                  compiler_params=_cparams(("arbitrary",)))(*args)
    if batch:
        outs = [o.reshape(batch * n_rows, o.shape[2]) if k < len(out_tiled) else o for k, o in enumerate(outs)]
    return (outs[:base_out], outs[base_out:]) if carry else outs


def _acc(ref, val, first):
    if first is not None:
        @pl.when(first)
        def _():
            ref[...] = jnp.zeros(ref.shape, ref.dtype)
    ref[...] += val


def _sds(shape, dtype=F32):
    return jax.ShapeDtypeStruct(shape, dtype)


def _ln_fwd(r, g, b):
    mu = jnp.mean(r, axis=-1, keepdims=True)
    rc = r - mu
    var = jnp.mean(rc * rc, axis=-1, keepdims=True)
    return rc * lax.rsqrt(var + LN_EPS) * g + b


def _ln_bwd(r, g, dy):
    mu = jnp.mean(r, axis=-1, keepdims=True)
    rc = r - mu
    var = jnp.mean(rc * rc, axis=-1, keepdims=True)
    rstd = lax.rsqrt(var + LN_EPS)
    xhat = rc * rstd
    dxh = dy * g
    dr = rstd * (dxh - jnp.mean(dxh, axis=-1, keepdims=True) - xhat * jnp.mean(dxh * xhat, axis=-1, keepdims=True))
    return dr, jnp.sum(dy * xhat, axis=0, keepdims=True), jnp.sum(dy, axis=0, keepdims=True)


def _ffn_fwd(name, x, mod, k0, w1, w3, w2, lng, lnb, seq, tgt=None, **carry):
    t = x.shape[0]
    tm = 256
    with_loss = tgt is not None

    def fn(ctx, tiled, halos, exs, res, outs, accs, exaccs, scr):
        x_ref = tiled[0]
        mod_ref, = exs
        w1_ref, w3_ref, w2_ref, g_ref, b_ref = res
        xo_ref, r_ref, h_ref, ab_ref, f_ref = outs[:5]
        xv = x_ref[...]
        sh, sc, g = mod_ref[0, k0:k0 + 1, :], mod_ref[0, k0 + 1:k0 + 2, :], mod_ref[0, k0 + 2:k0 + 3, :]
        h = _mx(xv * (1.0 + sc) + sh)
        h_ref[...] = h
        f = jnp.zeros((tm, D), F32)
        for j in range(2):
            a = _dot(h, w1_ref[:, j * FB:(j + 1) * FB])
            b = _dot(h, w3_ref[:, j * FB:(j + 1) * FB])
            ab_ref[:, j * FB:(j + 1) * FB] = a
            ab_ref[:, FF + j * FB:FF + (j + 1) * FB] = b
            s = a * _sigmoid(a) * b
            f = f + _dot(_mx(s), w2_ref[j * FB:(j + 1) * FB, :])
        f_ref[...] = f
        r = ALPHA * xv + 0.5 * g * f
        r_ref[...] = r
        xo = _ln_fwd(r, g_ref[...], b_ref[...])
        xo_ref[...] = xo
        if with_loss:
            e = xo - tiled[1][...]
            outs[5][...] = e * (1.0 / D)
            _acc(accs[0], jnp.sum(e * e) * jnp.ones((8, 128), F32), ctx.first)

    out_tiled = [_sds((t, D)), _sds((t, D)), _sds((t, D), MXU_DTYPE), _sds((t, 2 * FF)), _sds((t, D))]
    if with_loss:
        out_tiled.append(_sds((t, D)))
    return _rowcall(name, fn, t, tm, seq // tm, tiled=[x] + ([tgt] if with_loss else []), exs=[mod],
                    res=[w1, w3, w2, lng, lnb], out_tiled=out_tiled, out_acc=[_sds((8, 128))] if with_loss else [], **carry)


def _ffn_bwd1(name, dxo, r, ab, f, mod, k0, lng, w2t, seq, **carry):
    t = dxo.shape[0]
    tm = 256

    def fn(ctx, tiled, halos, exs, res, outs, accs, exaccs, scr):
        dxo_ref, r_ref, ab_ref, f_ref = tiled
        mod_ref, = exs
        g_ref, w2t_ref = res
        dr_ref, df_ref, s_ref, dab_ref = outs
        g = mod_ref[0, k0 + 2:k0 + 3, :]
        dr, dgam, dbet = _ln_bwd(r_ref[...], g_ref[...], dxo_ref[...])
        dr_ref[...] = dr
        _acc(accs[0], dgam, ctx.first)
        _acc(accs[1], dbet, ctx.first)
        _acc(exaccs[0].at[0], jnp.sum(0.5 * f_ref[...] * dr, axis=0, keepdims=True), ctx.ex_enter)
        df = _mx(0.5 * g * dr)
        df_ref[...] = df
        for j in range(2):
            ds = _dot(df, w2t_ref[:, j * FB:(j + 1) * FB])
            a = ab_ref[:, j * FB:(j + 1) * FB]
            b = ab_ref[:, FF + j * FB:FF + (j + 1) * FB]
            sig = _sigmoid(a)
            silu = a * sig
            s_ref[:, j * FB:(j + 1) * FB] = _mx(silu * b)
            dab_ref[:, j * FB:(j + 1) * FB] = _mx(ds * b * (sig * (1.0 + a * (1.0 - sig))))
            dab_ref[:, FF + j * FB:FF + (j + 1) * FB] = _mx(ds * silu)

    b = mod.shape[0]
    return _rowcall(name, fn, t, tm, seq // tm, tiled=[dxo, r, ab, f], exs=[mod], res=[lng, w2t],
                    out_tiled=[_sds((t, D)), _sds((t, D), MXU_DTYPE), _sds((t, FF), MXU_DTYPE),
                               _sds((t, 2 * FF), MXU_DTYPE)],
                    out_acc=[_sds((1, D)), _sds((1, D))], out_exacc=[_sds((b, 1, D))], **carry)


def _mod_bwd(name, dab, dr, x, mod, k0, wts, seq, extra=(), **carry):
    t = dr.shape[0]
    tm = 256
    nin = 1 + len(extra)
    width = dab.shape[1] + sum(e.shape[1] for e in extra)

    def fn(ctx, tiled, halos, exs, res, outs, accs, exaccs, scr):
        parts = tiled[:nin]
        dr_ref, x_ref = tiled[nin:]
        mod_ref, = exs
        sc = mod_ref[0, k0 + 1:k0 + 2, :]
        if nin == 1:
            dp = parts[0][...]
        else:
            dp = jnp.concatenate([_mx(p[...]) for p in parts], axis=1)
            outs[1][...] = dp
        dh, off = jnp.zeros((tm, D), F32), 0
        for wt_ref in res:
            dh = dh + _dot(dp[:, off:off + wt_ref.shape[0]], wt_ref[...])
            off += wt_ref.shape[0]
        outs[0][...] = ALPHA * dr_ref[...] + dh * (1.0 + sc)
        _acc(exaccs[0].at[0], jnp.sum(dh, axis=0, keepdims=True), ctx.ex_enter)
        _acc(exaccs[1].at[0], jnp.sum(dh * x_ref[...], axis=0, keepdims=True), ctx.ex_enter)

    b = mod.shape[0]
    out_tiled = [_sds((t, D))] + ([_sds((t, width), MXU_DTYPE)] if nin > 1 else [])
    return _rowcall(name, fn, t, tm, seq // tm, tiled=[dab, *extra, dr, x], exs=[mod], res=list(wts),
                    out_tiled=out_tiled, out_exacc=[_sds((b, 1, D)), _sds((b, 1, D))], **carry)


def _tn_matmul(name, a, b, bm, bn, bt=512, carry=None, a_cols=None):
    t = a.shape[0]
    start, m = a_cols or (0, a.shape[1])
    off = start // bm
    n = b.shape[1]
    grid = (m // bm, n // bn, t // bt)
    n_in, n_out = (len(carry.ins), len(carry.out_shapes)) if carry else (0, 0)

    def body(a_ref, b_ref, *refs):
        o_ref = refs[n_in]

        def work():
            @pl.when(pl.program_id(2) == 0)
            def _():
                o_ref[...] = jnp.zeros(o_ref.shape, F32)
            o_ref[...] += lax.dot_general(a_ref[...], b_ref[...], (((0,), (0,)), ((), ())), preferred_element_type=F32)

        if not carry:
            return work()
        step = (pl.program_id(0) * grid[1] + pl.program_id(1)) * grid[2] + pl.program_id(2)
        carry.run((refs[:n_in], refs[n_in + 1:n_in + 1 + n_out], refs[n_in + 1 + n_out:]), step == 0, step == 0,
                  step == grid[0] * grid[1] * grid[2] - 1, work)

    any_spec = pl.BlockSpec(memory_space=pl.ANY)
    outs = _pcall(body, name=name, grid=grid,
                  in_specs=[pl.BlockSpec((bt, bm), lambda i, j, k: (k, i + off)), pl.BlockSpec((bt, bn), lambda i, j, k: (k, j))]
                  + [any_spec] * n_in,
                  out_specs=[pl.BlockSpec((bm, bn), lambda i, j, k: (i, j))] + [any_spec] * n_out,
                  out_shape=[_sds((m, n))] + (carry.out_shapes if carry else []),
                  input_output_aliases={2 + k: 1 + v for k, v in carry.aliases.items()} if carry else {},
                  scratch_shapes=carry.sem_shapes() if carry else [],
                  compiler_params=_cparams(("arbitrary",) * 3 if carry else ("parallel", "parallel", "arbitrary")))(
                      a, b, *(carry.ins if carry else []))
    return (outs[0], outs[1:]) if carry else outs[0]


S5_BLK = 8


def _tn_diag(name, a, b, bt=512):
    t = a.shape[0]
    rows, cols = S5_BLK * S5P, S5_BLK * S5H
    nblk = a.shape[1] // rows

    def body(a_ref, b_ref, o_ref):
        @pl.when(pl.program_id(1) == 0)
        def _():
            o_ref[...] = jnp.zeros(o_ref.shape, F32)
        o_ref[0] += lax.dot_general(a_ref[...], b_ref[...], (((0,), (0,)), ((), ())), preferred_element_type=F32)

    return _pcall(body, name=name, grid=(nblk, t // bt),
                  in_specs=[pl.BlockSpec((bt, rows), lambda i, k: (k, i)),
                            pl.BlockSpec((bt, cols), lambda i, k: (k, i % (S5G // S5_BLK)))],
                  out_specs=pl.BlockSpec((1, rows, cols), lambda i, k: (i, 0, 0)), out_shape=_sds((nblk, rows, cols)),
                  compiler_params=_cparams(("parallel", "arbitrary")))(a, b)


def _diag_groups(o):
    o = o.reshape(2, S5G // S5_BLK, S5_BLK, S5P, S5_BLK, S5H)
    return jnp.einsum("rbgpgh->rbgph", o).reshape(2, S5G, S5P, S5H)


def _proj_fwd(name, x, mod, w_in, seq):
    t = x.shape[0]
    tm = 256

    def fn(ctx, tiled, halos, exs, res, outs, accs, exaccs, scr):
        mod_ref, = exs
        sh, sc = mod_ref[0, 3:4, :], mod_ref[0, 4:5, :]
        h = _mx(tiled[0][...] * (1.0 + sc) + sh)
        outs[0][...] = h
        outs[1][...] = _dot(h, res[0][...])

    return _rowcall(name, fn, t, tm, seq // tm, tiled=[x], exs=[mod], res=[w_in],
                    out_tiled=[_sds((t, D), MXU_DTYPE), _sds((t, PW))])


def _shift_rows(cur, prev8, j):
    if j == 0:
        return cur
    rolled = pltpu.roll(cur, j, 0)
    top = jnp.where(_iota((8, cur.shape[1]), 0) < j, pltpu.roll(prev8, j, 0), rolled[0:8])
    return jnp.concatenate([top, rolled[8:]], axis=0)


def _shift_rows_up(cur, next8, j):
    if j == 0:
        return cur
    n = cur.shape[0]
    rolled = pltpu.roll(cur, n - j, 0)
    bot = jnp.where(_iota((8, cur.shape[1]), 0) >= 8 - j, pltpu.roll(next8, 8 - j, 0), rolled[n - 8:n])
    return jnp.concatenate([rolled[:n - 8], bot], axis=0)


def _softplus(x):
    return jnp.maximum(x, 0.0) + jnp.log(1.0 + jnp.exp(-jnp.abs(x)))


def _ssd_common(proj_ref, xpre, dtb_ref, alog_ref):
    xbc = xpre * _sigmoid(xpre)
    xs, bm, cm = xbc[:, 0:SW], xbc[:, SW:SW + 256], xbc[:, SW + 256:SW + 512]
    dtraw = proj_ref[:, PW - 128:PW] + dtb_ref[...]
    dt = _softplus(dtraw)
    a = -jnp.exp(alog_ref[...])
    tril = (_iota((CH, CH), 0) >= _iota((CH, CH), 1)).astype(F32)
    acs = _dot_hi(tril, dt * a)
    return xs, bm, cm, dtraw, dt, a, acs, acs.T


def _pair_lane(lo, hi):
    r = lo.shape[0]
    return jnp.where(_iota((r, 128), 1) < HP, lo, hi)


def _ssd_fwd(name, proj, conv_w, conv_b, dt_bias, a_log, d_rep, norm_w, seq):
    t = proj.shape[0]
    nb = t // seq

    def fn(ctx, tiled, halos, exs, res, outs, accs, exaccs, scr):
        @pl.when(ctx.seq_first)
        def _():
            scr[0][...] = jnp.zeros(scr[0].shape, F32)

        for b in range(nb):
            one(ctx, res, [r.at[b] for r in tiled + halos + outs + scr])

    def one(ctx, res, refs):
        proj_ref, halo_ref, yo_ref, xpre_ref, y_ref, sprev_ref, state_ref = refs
        cw_ref, cb_ref, dtb_ref, alog_ref, d_ref, nw_ref = res
        raw = proj_ref[:, SW:SW + D]
        prev8 = halo_ref[:, SW:SW + D] * jnp.where(ctx.seq_first, 0.0, 1.0)
        xpre = cb_ref[...] + cw_ref[3:4, :] * raw
        for j in (1, 2, 3):
            xpre = xpre + cw_ref[3 - j:4 - j, :] * _shift_rows(raw, prev8, j)
        xpre_ref[...] = xpre
        xs, bm, cm, dtraw, dt, a, acs, acst = _ssd_common(proj_ref, xpre, dtb_ref, alog_ref)
        causal = _iota((CH, CH), 0) >= _iota((CH, CH), 1)
        lane_lo = _iota((CH, 128), 1) < HP
        sprev = state_ref[...]
        sprev_ref[...] = sprev
        ys = []
        for g in range(NG):
            bmg, cmg = bm[:, g * NS:(g + 1) * NS], cm[:, g * NS:(g + 1) * NS]
            bmt = bmg.T
            cb = _dot(_mx(cmg), _mx(bmt))
            for q in (2 * g, 2 * g + 1):
                xsq = xs[:, 128 * q:128 * q + 128]
                xd = xsq * _pair_lane(dt[:, 2 * q:2 * q + 1], dt[:, 2 * q + 1:2 * q + 2])
                sp = sprev[:, 128 * q:128 * q + 128]
                ydiag = jnp.zeros((CH, 128), F32)
                snew = jnp.zeros((NS, 128), F32)
                for jj in range(2):
                    h = 2 * q + jj
                    col, row = acs[:, h:h + 1], acst[h:h + 1, :]
                    lm = jnp.where(causal, jnp.exp(jnp.minimum(col - row, 0.0)), 0.0)
                    xm = _mx(jnp.where(lane_lo if jj == 0 else ~lane_lo, xd, 0.0))
                    ydiag = ydiag + _dot(_mx(cb * lm), xm)
                    dec_row = jnp.exp(acst[h:h + 1, CH - 1:CH] - row)
                    snew = snew + _dot(_mx(bmt * dec_row), xm)
                e_pair = jnp.exp(_pair_lane(acs[:, 2 * q:2 * q + 1], acs[:, 2 * q + 1:2 * q + 2]))
                yoff = _dot(_mx(cmg), _mx(sp)) * e_pair
                cd = jnp.exp(_pair_lane(acst[2 * q:2 * q + 1, CH - 1:CH], acst[2 * q + 1:2 * q + 2, CH - 1:CH]))
                state_ref[:, 128 * q:128 * q + 128] = cd * sp + snew
                ys.append(ydiag + yoff + d_ref[:, 128 * q:128 * q + 128] * xsq)
        y = jnp.concatenate(ys, axis=1)
        y_ref[...] = y
        z = proj_ref[:, 0:SW]
        yz = y * (z * _sigmoid(z))
        outp = []
        for g in range(NG):
            seg = yz[:, 256 * g:256 * g + 256]
            rinv = lax.rsqrt(jnp.mean(seg * seg, axis=-1, keepdims=True) + LN_EPS)
            outp.append(seg * rinv * nw_ref[:, 256 * g:256 * g + 256])
        yo_ref[...] = _mx(jnp.concatenate(outp, axis=1))

    return _rowcall(name, fn, t, CH, seq // CH, tiled=[proj], halos=[(proj, 8)],
                    res=[conv_w, conv_b, dt_bias, a_log, d_rep, norm_w],
                    out_tiled=[_sds((t, SW), MXU_DTYPE), _sds((t, D)), _sds((t, SW)), _sds((t, SW))],
                    scratch=[pltpu.VMEM((nb, NS, SW), F32)], batch=nb)


def _ssd_bwd(name, dyo, proj, xpre_all, y_all, sprev_all, conv_w, dt_bias, a_log, d_rep, norm_w, seq, **carry):
    t = proj.shape[0]
    nb = t // seq

    def fn(ctx, tiled, halos, exs, res, outs, accs, exaccs, scr):
        @pl.when(ctx.seq_last)
        def _():
            for r in scr:
                r[...] = jnp.zeros(r.shape, F32)

        @pl.when(ctx.first)
        def _():
            for r in accs:
                r[...] = jnp.zeros(r.shape, F32)

        for b in range(nb):
            one(ctx, None, res, accs, [r.at[b] for r in tiled + halos + outs + scr])

    def one(ctx, first, res, accs, refs):
        dyo_ref, proj_ref, xpre_ref, y_ref, sprev_ref, halo_ref, dzx_ref, ddt_ref, ds_ref, nxt_ref = refs
        cw_ref, dtb_ref, alog_ref, d_ref, nw_ref = res
        dnw_acc, dd_acc, dcw_acc, dcb_acc, ddtb_acc, dalog_acc = accs
        xpre = xpre_ref[...]
        xs, bm, cm, dtraw, dt, a, acs, acst = _ssd_common(proj_ref, xpre, dtb_ref, alog_ref)
        y = y_ref[...]
        z = proj_ref[:, 0:SW]
        sz = _sigmoid(z)
        siluz = z * sz
        yz = y * siluz
        dyo = dyo_ref[...]
        dyz_parts, dnw_parts = [], []
        for g in range(NG):
            seg = yz[:, 256 * g:256 * g + 256]
            rinv = lax.rsqrt(jnp.mean(seg * seg, axis=-1, keepdims=True) + LN_EPS)
            yn = seg * rinv
            dseg = dyo[:, 256 * g:256 * g + 256]
            dnw_parts.append(jnp.sum(dseg * yn, axis=0, keepdims=True))
            dyn = dseg * nw_ref[:, 256 * g:256 * g + 256]
            dyz_parts.append(rinv * (dyn - yn * jnp.mean(dyn * yn, axis=-1, keepdims=True)))
        dyz = jnp.concatenate(dyz_parts, axis=1)
        _acc(dnw_acc, jnp.concatenate(dnw_parts, axis=1), first)
        dy = dyz * siluz
        dz = dyz * y * (sz * (1.0 + z * (1.0 - sz)))
        _acc(dd_acc, jnp.sum(dy * xs, axis=0, keepdims=True), first)

        causal = _iota((CH, CH), 0) >= _iota((CH, CH), 1)
        anti = _iota((CH, CH), 0) <= _iota((CH, CH), 1)
        lane_lo = _iota((CH, 128), 1) < HP
        lane_id = _iota((CH, 128), 1)
        last_row = _iota((CH, 128), 0) == CH - 1
        sprev = sprev_ref[...]
        dacs = jnp.zeros((CH, 128), F32)
        ddt_x = jnp.zeros((CH, 128), F32)
        dxs_parts, dbm_parts, dcm_parts = [], [], []
        for g in range(NG):
            bmg, cmg = bm[:, g * NS:(g + 1) * NS], cm[:, g * NS:(g + 1) * NS]
            bmt, cmt = bmg.T, cmg.T
            cb = _dot(_mx(cmg), _mx(bmt))
            cbt = _dot(_mx(bmg), _mx(cmt))
            dcb = jnp.zeros((CH, CH), F32)
            dcbt = jnp.zeros((CH, CH), F32)
            dbmg = jnp.zeros((CH, NS), F32)
            dcmg = jnp.zeros((CH, NS), F32)
            for q in (2 * g, 2 * g + 1):
                sl = slice(128 * q, 128 * q + 128)
                xsq = xs[:, sl]
                dtp = _pair_lane(dt[:, 2 * q:2 * q + 1], dt[:, 2 * q + 1:2 * q + 2])
                xd = xsq * dtp
                dyq = dy[:, sl]
                sp = sprev[:, sl]
                dsn = ds_ref[:, sl]
                e_pair = jnp.exp(_pair_lane(acs[:, 2 * q:2 * q + 1], acs[:, 2 * q + 1:2 * q + 2]))
                cd = jnp.exp(_pair_lane(acst[2 * q:2 * q + 1, CH - 1:CH], acst[2 * q + 1:2 * q + 2, CH - 1:CH]))
                dye = dyq * e_pair
                dcmg = dcmg + _dot(_mx(dye), _mx(sp.T))
                dsp = _dot(_mx(cmt), _mx(dye)) + cd * dsn
                yoff = _dot(_mx(cmg), _mx(sp)) * e_pair
                dacs_lane = dyq * yoff
                dxd = jnp.zeros((CH, 128), F32)
                sds = jnp.sum(dsn * sp, axis=0, keepdims=True) * cd
                for jj in range(2):
                    h = 2 * q + jj
                    hm = lane_lo if jj == 0 else ~lane_lo
                    col, row = acs[:, h:h + 1], acst[h:h + 1, :]
                    lm = jnp.where(causal, jnp.exp(jnp.minimum(col - row, 0.0)), 0.0)
                    lmt = jnp.where(anti, jnp.exp(jnp.minimum(row - col, 0.0)), 0.0)
                    xm = _mx(jnp.where(hm, xd, 0.0))
                    dym = _mx(jnp.where(hm, dyq, 0.0))
                    gm = _dot_nt(dym, xm)
                    gmt = _dot_nt(xm, dym)
                    dcb = dcb + gm * lm
                    dcbt = dcbt + gmt * lmt
                    dxd = dxd + _dot(_mx(cbt * lmt), dym)
                    w = gm * cb * lm
                    wt = gmt * cbt * lmt
                    dacs_h = jnp.sum(w, axis=1, keepdims=True) - jnp.sum(wt, axis=1, keepdims=True)
                    alast = acst[h:h + 1, CH - 1:CH]
                    dec_col = jnp.exp(alast - col)
                    dsm = _mx(jnp.where(hm[0:NS], dsn, 0.0))
                    qh = _dot_nt(xm, dsm)
                    dbmg = dbmg + qh * dec_col
                    ddec = jnp.sum(qh * bmg, axis=1, keepdims=True)
                    dxd = dxd + _dot(_mx(bmg * dec_col), dsm)
                    dacs_h = dacs_h - ddec * dec_col
                    dacs_h = dacs_h + jnp.sum(jnp.where(hm, dacs_lane, 0.0), axis=1, keepdims=True)
                    tail = jnp.sum(ddec * dec_col, axis=0, keepdims=True) + jnp.sum(
                        jnp.where(hm[0:1], sds, 0.0), axis=1, keepdims=True)
                    dacs = dacs + jnp.where(lane_id == h, dacs_h, 0.0) + jnp.where(
                        last_row & (lane_id == h), tail, 0.0)
                ds_ref[:, sl] = dsp
                for jj in range(2):
                    h = 2 * q + jj
                    hm = lane_lo if jj == 0 else ~lane_lo
                    ddt_x = ddt_x + jnp.where(lane_id == h, jnp.sum(jnp.where(hm, dxd * xsq, 0.0), axis=1, keepdims=True), 0.0)
                dxs_parts.append(dxd * dtp + d_ref[:, sl] * dyq)
            dcmg = dcmg + _dot(_mx(dcb), _mx(bmg))
            dbmg = dbmg + _dot(_mx(dcbt), _mx(cmg))
            dbm_parts.append(dbmg)
            dcm_parts.append(dcmg)
        triu = (_iota((CH, CH), 0) <= _iota((CH, CH), 1)).astype(F32)
        dadt = _dot_hi(triu, dacs)
        ddt = dadt * a + ddt_x
        _acc(dalog_acc, jnp.sum(dadt * dt, axis=0, keepdims=True) * a, first)
        ddtraw = ddt * _sigmoid(dtraw)
        ddt_ref[...] = ddtraw
        _acc(ddtb_acc, jnp.sum(ddtraw, axis=0, keepdims=True), first)
        dxbc = jnp.concatenate(dxs_parts + dbm_parts + dcm_parts, axis=1)
        sx = _sigmoid(xpre)
        dpre = dxbc * (sx * (1.0 + xpre * (1.0 - sx)))
        _acc(dcb_acc, jnp.sum(dpre, axis=0, keepdims=True), first)
        raw = proj_ref[:, SW:SW + D]
        prev8 = halo_ref[:, SW:SW + D] * jnp.where(ctx.seq_first, 0.0, 1.0)
        next8 = nxt_ref[...]
        draw = cw_ref[3:4, :] * dpre
        dcw = [None] * 4
        dcw[3] = jnp.sum(dpre * raw, axis=0, keepdims=True)
        for j in (1, 2, 3):
            dcw[3 - j] = jnp.sum(dpre * _shift_rows(raw, prev8, j), axis=0, keepdims=True)
            draw = draw + cw_ref[3 - j:4 - j, :] * _shift_rows_up(dpre, next8, j)
        _acc(dcw_acc, jnp.concatenate(dcw + [jnp.zeros((4, D), F32)], axis=0), first)
        nxt_ref[...] = dpre[0:8]
        dzx_ref[:, 0:SW] = dz
        dzx_ref[:, SW:SW + D] = draw

    return _rowcall(name, fn, t, CH, seq // CH, tiled=[dyo, proj, xpre_all, y_all, sprev_all], halos=[(proj, 8)],
                    res=[conv_w, dt_bias, a_log, d_rep, norm_w],
                    out_tiled=[_sds((t, SW + D)), _sds((t, 128))],
                    out_acc=[_sds((1, SW)), _sds((1, SW)), _sds((8, D)), _sds((1, D)), _sds((1, 128)), _sds((1, 128))],
                    scratch=[pltpu.VMEM((nb, NS, SW), F32), pltpu.VMEM((nb, 8, D), F32)], reverse=True, batch=nb, **carry)


def _gelu(y):
    k = math.sqrt(2.0 / math.pi)
    return 0.5 * y * (1.0 + jnp.tanh(k * (y + 0.044715 * y * y * y)))


def _gelu_grad(y):
    k = math.sqrt(2.0 / math.pi)
    th = jnp.tanh(k * (y + 0.044715 * y * y * y))
    return 0.5 * (1.0 + th) + 0.5 * y * (1.0 - th * th) * k * (1.0 + 3.0 * 0.044715 * y * y)


S5T = 256


def _cmul_add(xr, xi, ar, ai, sr, si):
    return xr + ar * sr - ai * si, xi + ar * si + ai * sr


def _s5_fwd(name, proj, bbd, cbd, pw, tab, d5, w_glu, b_glu, seq):
    t = proj.shape[0]
    tm = S5T

    def fn(ctx, tiled, halos, exs, res, outs, accs, exaccs, scr):
        proj_ref, = tiled
        bbd_ref, cbd_ref, pw_ref, tab_ref, d_ref, wg_ref, bg_ref = res
        out_ref, xst_ref, y_ref, xb_ref, ub_ref = outs
        carry_ref, = scr

        @pl.when(ctx.seq_first)
        def _():
            carry_ref[...] = jnp.zeros(carry_ref.shape, F32)

        u = proj_ref[:, 1536:2048]
        bu = _dot(_mx(u), bbd_ref[...])
        xr, xi = bu[:, :S5L].reshape(tm // 8, 8, S5L), bu[:, S5L:].reshape(tm // 8, 8, S5L)
        for k, sh in enumerate((1, 2, 4)):
            xr, xi = _cmul_add(xr, xi, pw_ref[k, :, :S5L], pw_ref[k, :, S5L:], pltpu.roll(xr, sh, 1), pltpu.roll(xi, sh, 1))
        xst_ref[:, :S5L] = xr.reshape(tm, S5L)
        xst_ref[:, S5L:] = xi.reshape(tm, S5L)

        def tile_fix(i, c):
            cr, ci = c
            rows = pl.ds(pl.multiple_of(i * 8, 8), 8)
            tr, ti = _cmul_add(xst_ref[rows, :S5L], xst_ref[rows, S5L:], tab_ref[:, :S5L], tab_ref[:, S5L:], cr, ci)
            xst_ref[rows, :S5L] = tr
            xst_ref[rows, S5L:] = ti
            return tr[7:8], ti[7:8]

        cr, ci = lax.fori_loop(0, tm // 8, tile_fix, (carry_ref[0:1, :S5L], carry_ref[0:1, S5L:]))
        carry_ref[0:1, :S5L] = cr
        carry_ref[0:1, S5L:] = ci
        xb = _mx(xst_ref[...])
        xb_ref[...] = xb
        ub_ref[...] = _mx(u)
        y = _dot(xb, cbd_ref[...]) + u * d_ref[...]
        y_ref[...] = y
        g = _gelu(y)
        v = _dot(_mx(g), wg_ref[...]) + bg_ref[...]
        out_ref[...] = _mx(g * _sigmoid(v))

    return _rowcall(name, fn, t, tm, seq // tm, tiled=[proj], res=[bbd, cbd, pw, tab, d5, w_glu, b_glu],
                    out_tiled=[_sds((t, SW), MXU_DTYPE), _sds((t, 2 * S5L)), _sds((t, SW)),
                               _sds((t, 2 * S5L), MXU_DTYPE), _sds((t, SW), MXU_DTYPE)],
                    scratch=[pltpu.VMEM((8, 2 * S5L), F32)])


def _s5_bwd(name, dout, proj, xst, y_all, bbdt, cbdt, pwc, tabc, d5, w_glu, w_glut, b_glu, seq, **carry):
    t = proj.shape[0]
    tm = S5T

    def fn(ctx, tiled, halos, exs, res, outs, accs, exaccs, scr):
        dout_ref, proj_ref, xst_ref, y_ref = tiled
        halo_ref, = halos
        bbdt_ref, cbdt_ref, pw_ref, tab_ref, d_ref, wg_ref, wgt_ref, bg_ref = res
        du_ref, lam_ref, dyb_ref, gb_ref, dvb_ref = outs
        da_acc, dd_acc, dbg_acc = accs
        carry_ref, lamf_ref = scr

        @pl.when(ctx.seq_last)
        def _():
            carry_ref[...] = jnp.zeros(carry_ref.shape, F32)

        u = proj_ref[:, 1536:2048]
        y = y_ref[...]
        g = _gelu(y)
        v = _dot(_mx(g), wg_ref[...]) + bg_ref[...]
        sg = _sigmoid(v)
        dout = dout_ref[...]
        dv = dout * g * sg * (1.0 - sg)
        dvb = _mx(dv)
        dvb_ref[...] = dvb
        gb_ref[...] = _mx(g)
        _acc(dbg_acc, jnp.sum(dv, axis=0, keepdims=True), ctx.first)
        dg = dout * sg + _dot(dvb, wgt_ref[...])
        dy = dg * _gelu_grad(y)
        dyb = _mx(dy)
        dyb_ref[...] = dyb
        _acc(dd_acc, jnp.sum(dy * u, axis=0, keepdims=True), ctx.first)
        dx = _dot(dyb, cbdt_ref[...])
        xr, xi = dx[:, :S5L].reshape(tm // 8, 8, S5L), dx[:, S5L:].reshape(tm // 8, 8, S5L)
        for k, sh in enumerate((1, 2, 4)):
            xr, xi = _cmul_add(xr, xi, pw_ref[k, :, :S5L], pw_ref[k, :, S5L:], pltpu.roll(xr, 8 - sh, 1),
                               pltpu.roll(xi, 8 - sh, 1))
        lamf_ref[:, :S5L] = xr.reshape(tm, S5L)
        lamf_ref[:, S5L:] = xi.reshape(tm, S5L)

        def tile_fix(i, c):
            cr, ci = c
            rows = pl.ds(pl.multiple_of((tm // 8 - 1 - i) * 8, 8), 8)
            tr, ti = _cmul_add(lamf_ref[rows, :S5L], lamf_ref[rows, S5L:], tab_ref[:, :S5L], tab_ref[:, S5L:], cr, ci)
            lamf_ref[rows, :S5L] = tr
            lamf_ref[rows, S5L:] = ti
            return tr[0:1], ti[0:1]

        cr, ci = lax.fori_loop(0, tm // 8, tile_fix, (carry_ref[0:1, :S5L], carry_ref[0:1, S5L:]))
        carry_ref[0:1, :S5L] = cr
        carry_ref[0:1, S5L:] = ci
        lam = lamf_ref[...]
        lamb = _mx(lam)
        lam_ref[...] = lamb
        du_ref[...] = dy * d_ref[...] + _dot(lamb, bbdt_ref[...])
        prev8 = halo_ref[...] * jnp.where(ctx.seq_first, 0.0, 1.0)
        xprev = _shift_rows(xst_ref[...], prev8, 1)
        lr, li = lam[:, :S5L], lam[:, S5L:]
        pr, pi = xprev[:, :S5L], xprev[:, S5L:]
        dar = jnp.sum(lr * pr + li * pi, axis=0, keepdims=True)
        dai = jnp.sum(li * pr - lr * pi, axis=0, keepdims=True)
        _acc(da_acc, jnp.concatenate([dar, dai], axis=1), ctx.first)

    return _rowcall(name, fn, t, tm, seq // tm, tiled=[dout, proj, xst, y_all], halos=[(xst, 8)],
                    res=[bbdt, cbdt, pwc, tabc, d5, w_glu, w_glut, b_glu],
                    out_tiled=[_sds((t, SW)), _sds((t, 2 * S5L), MXU_DTYPE), _sds((t, SW), MXU_DTYPE),
                               _sds((t, SW), MXU_DTYPE), _sds((t, SW), MXU_DTYPE)],
                    out_acc=[_sds((1, 2 * S5L)), _sds((1, SW)), _sds((1, SW))],
                    scratch=[pltpu.VMEM((8, 2 * S5L), F32), pltpu.VMEM((tm, 2 * S5L), F32)], reverse=True, **carry)


def _out_fwd(name, yssd, ys5, x1, mod, w_out, lng, lnb, seq):
    t = x1.shape[0]
    tm = 256

    def fn(ctx, tiled, halos, exs, res, outs, accs, exaccs, scr):
        ya_ref, yb_ref, x_ref = tiled
        mod_ref, = exs
        w_ref, g_ref, b_ref = res
        m = _dot(ya_ref[...], w_ref[0:SW, :]) + _dot(yb_ref[...], w_ref[SW:2 * SW, :])
        r = ALPHA * x_ref[...] + mod_ref[0, 5:6, :] * m
        outs[0][...] = _ln_fwd(r, g_ref[...], b_ref[...])
        outs[1][...] = r
        outs[2][...] = m

    return _rowcall(name, fn, t, tm, seq // tm, tiled=[yssd, ys5, x1], exs=[mod], res=[w_out, lng, lnb],
                    out_tiled=[_sds((t, D)), _sds((t, D)), _sds((t, D))])


def _out_bwd(name, dxo, r, m, mod, lng, w_outt, seq, **carry):
    t = dxo.shape[0]
    tm = 256

    def fn(ctx, tiled, halos, exs, res, outs, accs, exaccs, scr):
        dxo_ref, r_ref, m_ref = tiled
        mod_ref, = exs
        g_ref, wt_ref = res
        dr, dgam, dbet = _ln_bwd(r_ref[...], g_ref[...], dxo_ref[...])
        outs[0][...] = dr
        _acc(accs[0], dgam, ctx.first)
        _acc(accs[1], dbet, ctx.first)
        _acc(exaccs[0].at[0], jnp.sum(dr * m_ref[...], axis=0, keepdims=True), ctx.ex_enter)
        dm = _mx(mod_ref[0, 5:6, :] * dr)
        outs[1][...] = dm
        dyc = _dot(dm, wt_ref[...])
        outs[2][...] = dyc[:, 0:SW]
        outs[3][...] = dyc[:, SW:2 * SW]

    b = mod.shape[0]
    return _rowcall(name, fn, t, tm, seq // tm, tiled=[dxo, r, m], exs=[mod], res=[lng, w_outt],
                    out_tiled=[_sds((t, D)), _sds((t, D), MXU_DTYPE), _sds((t, SW)), _sds((t, SW))],
                    out_acc=[_sds((1, D)), _sds((1, D))], out_exacc=[_sds((b, 1, D))], **carry)


def _s5_discretise(a_re, a_im, log_dt, b_re, b_im):
    dt = jnp.exp(log_dt)[:, None]
    mag = jnp.exp(dt * a_re)
    ab_re, ab_im = mag * jnp.cos(dt * a_im), mag * jnp.sin(dt * a_im)
    den = a_re * a_re + a_im * a_im
    nr, ni = ab_re - 1.0, ab_im
    f_re, f_im = (nr * a_re + ni * a_im) / den, (ni * a_re - nr * a_im) / den
    bb_re = f_re[..., None] * b_re - f_im[..., None] * b_im
    bb_im = f_re[..., None] * b_im + f_im[..., None] * b_re
    return ab_re, ab_im, bb_re, bb_im


def _s5_tables(ab_re, ab_im):
    ar, ai = ab_re.reshape(1, S5L), ab_im.reshape(1, S5L)
    pows = [(ar, ai)]
    for _ in range(7):
        pr, pi = pows[-1]
        pows.append((pr * ar - pi * ai, pr * ai + pi * ar))

    def pack(rows, sign):
        return jnp.concatenate([jnp.concatenate([r for r, _ in rows], axis=0),
                                jnp.concatenate([sign * i for _, i in rows], axis=0)], axis=1)

    row = jnp.arange(8)[:, None]
    pw = jnp.stack([jnp.where(row >= sh, pack([pows[sh - 1]], 1.0), 0.0) for sh in (1, 2, 4)])
    pwc = jnp.stack([jnp.where(row < 8 - sh, pack([pows[sh - 1]], -1.0), 0.0) for sh in (1, 2, 4)])
    tab = pack(pows, 1.0)
    tabc = pack(pows[::-1], -1.0)
    return pw, tab, pwc, tabc


class _GradGroup:
    def __init__(self, tag, grads, place):
        self.tag, self.names, (self.half, self.chip) = tag, list(grads), place
        self.gsh = [_shard_major(g) if n in COL_SHARDED else g.reshape(4, g.shape[0] // 4, g.shape[1])
                    for n, g in grads.items()]

    def sibling(self):
        return _rs_sibling_stage(self.gsh)

    def chips(self, received):
        self.sums, sums_bf = _rs_add(self.tag + "_add", self.gsh, received, self.half)
        return _rs_chips_stage(sums_bf)

    def join(self, received):
        return _rs_join_stage(_rs_sum(self.tag + "_sum", self.sums, received, self.chip, self.half))

    def result(self, joined):
        return dict(zip(self.names, joined))


def _hid(fn, *args, stage=None, **kw):
    if stage is None:
        return fn(*args, **kw), None
    return fn(*args, carry=stage, **kw)


def _local_step(x, tgt, mod, w, sp, seq, dist=None):
    t = x.shape[0]
    mxu = MXU_DTYPE
    big = {}

    def group(tag, grads):
        if dist is None:
            big.update(grads)
            return None
        return _GradGroup(tag, grads, dist[2])

    (x1, r1, h1, ab1, f1), late = _hid(_ffn_fwd, "ffn1_fwd", x, mod, 0, w["ffn1_w1"], w["ffn1_w3"], w["ffn1_w2"], sp["ln1_g"],
                                       sp["ln1_b"], seq, stage=dist and dist[0], **({"carry_mid": 0.75} if dist else {}))
    if dist:
        w = {**w, **dist[1](late)}
    h2, proj = _proj_fwd("proj_fwd", x1, mod, w["w_in"], seq)
    yssd, xpre, yraw, sprev = _ssd_fwd("ssd_fwd", proj, sp["conv_w"], sp["conv_b"], sp["dt_bias"], sp["a_log"],
                                       sp["d_rep"], sp["ssd_norm_w"], seq)
    (ab_re, ab_im, bb_re, bb_im), disc_vjp = jax.vjp(_s5_discretise, sp["s5_a_re"], sp["s5_a_im"], sp["s5_log_dt"],
                                                     sp["s5_b_re"], sp["s5_b_im"])
    eye = jnp.eye(S5G, dtype=F32)
    bbd = jnp.concatenate([jnp.einsum("gk,gph->ghkp", eye, bb_re).reshape(SW, S5L),
                           jnp.einsum("gk,gph->ghkp", eye, bb_im).reshape(SW, S5L)], axis=1).astype(mxu)
    cbd = jnp.concatenate([jnp.einsum("gk,ghp->gpkh", eye, sp["s5_c_re"]).reshape(S5L, SW),
                           -jnp.einsum("gk,ghp->gpkh", eye, sp["s5_c_im"]).reshape(S5L, SW)], axis=0).astype(mxu)
    bbdt = jnp.concatenate([jnp.einsum("gk,gph->kpgh", eye, bb_re).reshape(S5L, SW),
                            jnp.einsum("gk,gph->kpgh", eye, bb_im).reshape(S5L, SW)], axis=0).astype(mxu)
    cbdt = jnp.concatenate([jnp.einsum("gk,ghp->khgp", eye, sp["s5_c_re"]).reshape(SW, S5L),
                            -jnp.einsum("gk,ghp->khgp", eye, sp["s5_c_im"]).reshape(SW, S5L)], axis=1).astype(mxu)
    pw, tab, pwc, tabc = _s5_tables(lax.stop_gradient(ab_re), lax.stop_gradient(ab_im))
    ys5, xst, y5, xstb, ub = _s5_fwd("s5_fwd", proj, bbd, cbd, pw, tab, sp["s5_d"], w["w_glu"], sp["b_glu"], seq)
    x2, r2, m2 = _out_fwd("out_fwd", yssd, ys5, x1, mod, w["w_out"], sp["ln2_g"], sp["ln2_b"], seq)
    x3, r3, h3, ab3, f3, dy3, loss_acc = _ffn_fwd("ffn2_fwd", x2, mod, 6, w["ffn2_w1"], w["ffn2_w3"], w["ffn2_w2"], sp["ln3_g"],
                                                  sp["ln3_b"], seq, tgt=tgt)
    dr3, df3, s3, dab3, dg3g, dg3b, dgate3 = _ffn_bwd1("ffn2_bwd1", dy3, r3, ab3, f3, mod, 6, sp["ln3_g"],
                                                        w["ffn2_w2t"], seq)
    dx2, dsh3, dsc3 = _mod_bwd("ffn2_bwd2", dab3, dr3, x2, mod, 6, [w["ffn2_w1t"], w["ffn2_w3t"]], seq)
    grp = group("rs_ffn2", dict(ffn2_w1=_tn_matmul("ffn2_dw1", dab3, h3, FB, D, a_cols=(0, FF)),
                                ffn2_w3=_tn_matmul("ffn2_dw3", dab3, h3, FB, D, a_cols=(FF, FF)),
                                ffn2_w2=_tn_matmul("ffn2_dw2", s3, df3, FB, D)))
    (dr2, dm2, dyssd, dys5, dg2g, dg2b, dgate2), got = _hid(_out_bwd, "out_bwd", dx2, r2, m2, mod, sp["ln2_g"],
                                                          w["w_outt"], seq, stage=grp and grp.sibling())
    g_w_out = jnp.concatenate([_tn_matmul("dw_out_a", yssd, dm2, SW, D), _tn_matmul("dw_out_b", ys5, dm2, SW, D)], axis=0)
    (du, lam, dy5b, g5b, dv5b, da5, dd5, dbglu), got = _hid(
        _s5_bwd, "s5_bwd", dys5, proj, xst, y5, bbdt, cbdt, pwc, tabc, sp["s5_d"], w["w_glu"], w["w_glut"], sp["b_glu"],
        seq, stage=grp and grp.chips(got))
    g_w_glu = _tn_matmul("dw_glu", g5b, dv5b, SW, SW)
    dbb = _diag_groups(_tn_diag("s5_db", lam, ub))
    dcc = _diag_groups(_tn_diag("s5_dc", xstb, dy5b))
    (dzx, ddt, dnw, ddl, dcw, dcb, ddtb, dalog), got = _hid(
        _ssd_bwd, "ssd_bwd", dyssd, proj, xpre, yraw, sprev, sp["conv_w"], sp["dt_bias"], sp["a_log"], sp["d_rep"],
        sp["ssd_norm_w"], seq, stage=grp and grp.join(got))
    if grp:
        big.update(grp.result(got))
    dx1, dproj, dsh2, dsc2 = _mod_bwd("proj_bwd", dzx, dr2, x1, mod, 3, [w["w_int"]], seq, extra=(du, ddt))
    gwi = _tn_matmul("dw_in", h2, dproj, D, PW)
    grp = group("rs_mix", dict(w_in=jnp.concatenate([gwi[:, :1536], gwi[:, 2048:2056], gwi[:, 1536:2048]], axis=1),
                               w_glu=g_w_glu, w_out=g_w_out))
    (dr1, df1, s1, dab1, dg1g, dg1b, dgate1), got = _hid(_ffn_bwd1, "ffn1_bwd1", dx1, r1, ab1, f1, mod, 0, sp["ln1_g"],
                                                          w["ffn1_w2t"], seq, stage=grp and grp.sibling())
    g1, got = _hid(_tn_matmul, "ffn1_dw1", dab1, h1, FB, D, a_cols=(0, FF), stage=grp and grp.chips(got))
    g3, got = _hid(_tn_matmul, "ffn1_dw3", dab1, h1, FB, D, a_cols=(FF, FF), stage=grp and grp.join(got))
    if grp:
        big.update(grp.result(got))
    grp = group("rs_ffn1a", dict(ffn1_w1=g1, ffn1_w3=g3))
    g2, got = _hid(_tn_matmul, "ffn1_dw2", s1, df1, FB, D, stage=grp and grp.sibling())
    grp2 = group("rs_ffn1b", dict(ffn1_w2=g2))
    (dx0, dsh1, dsc1), got = _hid(_mod_bwd, "ffn1_bwd2", dab1, dr1, x, mod, 0, [w["ffn1_w1t"], w["ffn1_w3t"]], seq,
                                  stage=grp and _merge_stages(grp.chips(got), grp2.sibling()))
    if grp:
        n = len(grp.names)
        got = _run_stage("rs_ffn1_tail", _merge_stages(grp.join(got[:n]), grp2.chips(got[n:])))
        big.update(grp.result(got[:n]))
        big.update(grp2.result(_run_stage("rs_ffn1b_join", grp2.join(got[n:]))))
    dmod = jnp.concatenate([dsh1, dsc1, dgate1, dsh2, dsc2, dgate2, dsh3, dsc3, dgate3], axis=1)
    dc_re, dc_im = dcc[0].transpose(0, 2, 1), -dcc[1].transpose(0, 2, 1)
    g_a_re, g_a_im, g_log_dt, g_b_re, g_b_im = disc_vjp(
        (da5[:, :S5L].reshape(S5G, S5P), da5[:, S5L:].reshape(S5G, S5P), dbb[0], dbb[1]))
    small = dict(ln1_g=dg1g, ln1_b=dg1b, ln2_g=dg2g, ln2_b=dg2b, ln3_g=dg3g, ln3_b=dg3b, conv_w=dcw[0:4], conv_b=dcb,
                 dt_bias=ddtb[:, :NH], a_log=dalog[:, :NH], d_ssd=jnp.sum(ddl.reshape(NH, HP), axis=1).reshape(1, NH),
                 ssd_norm_w=dnw, s5_a_re=g_a_re, s5_a_im=g_a_im, s5_log_dt=g_log_dt, s5_b_re=g_b_re, s5_b_im=g_b_im,
                 s5_c_re=dc_re, s5_c_im=dc_im, s5_d=dd5, w_glu_b=dbglu)
    return loss_acc[0, 0], dx0, dmod, big, small


def _place():
    return lax.axis_index("x"), lax.axis_index("y"), lax.axis_index("c")


def _other_chips(x, y):
    return [(1 - x, y), (x, 1 - y), (1 - x, 1 - y)]


def _allgather8(name, a):
    r, n = a.shape

    def body(x_ref, out_ref, send_sems, recv_sems, local_sem):
        x, y, c = _place()
        me, sibling = (x, y, c), (x, y, 1 - c)
        chips = _other_chips(x, y)

        def rows(px, py, pc):
            return out_ref.at[pl.ds(pl.multiple_of((4 * px + 2 * py + pc) * r, 8), r), :]

        def copy(k, block, to, src=None):
            return pltpu.make_async_remote_copy(src_ref=rows(*block) if src is None else src, dst_ref=rows(*block),
                                                send_sem=send_sems.at[k], recv_sem=recv_sems.at[k], device_id=to,
                                                device_id_type=MESH_T)

        mine = pltpu.make_async_copy(x_ref, rows(*me), local_sem)
        mine.start()
        first = [copy(0, me, sibling, src=x_ref)]
        first += [copy(1 + j, me, (*chip, c), src=x_ref) for j, chip in enumerate(chips)]
        for cp in first:
            cp.start()
        passed = [copy(4 + j, (*chip, c), sibling) for j, chip in enumerate(chips)]
        for j, chip in enumerate(chips):
            copy(1 + j, (*chip, c), me).wait_recv()
            passed[j].start()
        copy(0, sibling, me).wait_recv()
        for j, chip in enumerate(chips):
            copy(4 + j, (*chip, 1 - c), me).wait_recv()
        for cp in first + passed:
            cp.wait_send()
        mine.wait()

    out = _pcall(body, name=name, out_shape=_sds((8 * r, n), a.dtype),
                 in_specs=[pl.BlockSpec(memory_space=pltpu.VMEM)], out_specs=pl.BlockSpec(memory_space=pltpu.VMEM),
                 scratch_shapes=[pltpu.SemaphoreType.DMA((7,)), pltpu.SemaphoreType.DMA((7,)), pltpu.SemaphoreType.DMA],
                 compiler_params=_cparams())(a)
    return out.reshape(8, r, n)


class _Stage:
    def __init__(self, ins, out_shapes, n_sems, start, finish, mid=None, aliases=None):
        self.ins, self.out_shapes, self.n_sems = list(ins), list(out_shapes), tuple(n_sems)
        self.start, self.mid, self.finish = start, mid, finish
        self.aliases = dict(aliases or {})

    def sem_shapes(self):
        return [pltpu.SemaphoreType.DMA((n,)) for n in self.n_sems]

    def run(self, refs, at_start=None, at_mid=None, at_finish=None, between=None):
        if between is None:
            for part in (self.start, self.mid, self.finish):
                if part is not None:
                    part(*refs)
            return
        pl.when(at_start)(functools.partial(self.start, *refs))
        if self.mid is not None:
            pl.when(at_mid)(functools.partial(self.mid, *refs))
        between()
        pl.when(at_finish)(functools.partial(self.finish, *refs))


def _merge_stages(a, b):
    n_in, n_out, n_sem = len(a.ins), len(a.out_shapes), len(a.n_sems)

    def both(fa, fb):
        def run(ins, outs, sems):
            fa(ins[:n_in], outs[:n_out], sems[:n_sem])
            fb(ins[n_in:], outs[n_out:], sems[n_sem:])
        return run

    aliases = {**a.aliases, **{n_in + k: n_out + v for k, v in b.aliases.items()}}
    return _Stage(a.ins + b.ins, a.out_shapes + b.out_shapes, a.n_sems + b.n_sems, both(a.start, b.start),
                  both(a.finish, b.finish), aliases=aliases)


def _run_stage(name, st):
    n_in, n_out = len(st.ins), len(st.out_shapes)

    def body(*refs):
        st.run((refs[:n_in], refs[n_in:n_in + n_out], refs[n_in + n_out:]))

    any_spec = pl.BlockSpec(memory_space=pl.ANY)
    return _pcall(body, name=name, out_shape=st.out_shapes, in_specs=[any_spec] * n_in, out_specs=[any_spec] * n_out,
                  input_output_aliases=st.aliases, scratch_shapes=st.sem_shapes(), compiler_params=_cparams())(*st.ins)


def _rows(ref_rows, half, align):
    hr = ref_rows // 2
    return pl.ds(pl.multiple_of(half * hr, align), hr)


def _gather_stage(shards):
    n = len(shards)
    pairs = [(i, j) for i in range(n) for j in range(3)]

    def env(ins, outs, sems):
        x, y, c = _place()
        chips = _other_chips(x, y)

        def copy(i, k, chip, half, to, src=None):
            dst = outs[i].at[2 * chip[0] + chip[1], _rows(shards[i].shape[0], half, 16)]
            return pltpu.make_async_remote_copy(src_ref=dst if src is None else src, dst_ref=dst,
                                                send_sem=sems[0].at[6 * i + k], recv_sem=sems[1].at[6 * i + k],
                                                device_id=to, device_id_type=MESH_T)

        def first(i, j):
            return copy(i, j, (x, y), c, (*chips[j], c), src=ins[i].at[_rows(shards[i].shape[0], c, 16)])

        def passed(i, j, half):
            return copy(i, 3 + j, chips[j], half, (x, y, 1 - c))

        def landed(i, j):
            return copy(i, j, chips[j], c, (x, y, 1 - c))

        return c, first, passed, landed

    def start(ins, outs, sems):
        c, first, passed, landed = env(ins, outs, sems)
        for i, j in pairs:
            first(i, j).start()

    def mid(ins, outs, sems):
        c, first, passed, landed = env(ins, outs, sems)
        for i, j in pairs:
            landed(i, j).wait_recv()
            passed(i, j, c).start()

    def finish(ins, outs, sems):
        c, first, passed, landed = env(ins, outs, sems)
        for i, j in pairs:
            passed(i, j, 1 - c).wait_recv()
        for i, j in pairs:
            first(i, j).wait_send()
            passed(i, j, c).wait_send()

    return _Stage(shards, [_sds((4,) + s.shape, s.dtype) for s in shards], (6 * n, 6 * n), start, finish, mid)


def _rs_sibling_stage(gs):
    n = len(gs)

    def copies(ins, outs, sems):
        x, y, c = _place()
        return [pltpu.make_async_remote_copy(src_ref=ins[i].at[:, _rows(gs[i].shape[1], 1 - c, 8)], dst_ref=outs[i],
                                             send_sem=sems[0].at[i], recv_sem=sems[1].at[i], device_id=(x, y, 1 - c),
                                             device_id_type=MESH_T) for i in range(n)]

    def start(*refs):
        for cp in copies(*refs):
            cp.start()

    def finish(*refs):
        for cp in copies(*refs):
            cp.wait()

    return _Stage(gs, [_sds((4, g.shape[1] // 2, g.shape[2]), g.dtype) for g in gs], (n, n), start, finish)


def _rs_chips_stage(hs):
    n = len(hs)

    def copies(ins, outs, sems):
        x, y, c = _place()
        return [pltpu.make_async_remote_copy(src_ref=ins[i].at[2 * chip[0] + chip[1]], dst_ref=outs[i].at[j],
                                             send_sem=sems[0].at[3 * i + j], recv_sem=sems[1].at[3 * i + j],
                                             device_id=(*chip, c), device_id_type=MESH_T)
                for i in range(n) for j, chip in enumerate(_other_chips(x, y))]

    def start(*refs):
        for cp in copies(*refs):
            cp.start()

    def finish(*refs):
        for cp in copies(*refs):
            cp.wait()

    return _Stage(hs, [_sds((3,) + h.shape[1:], h.dtype) for h in hs], (3 * n, 3 * n), start, finish)


def _rs_join_stage(fs):
    n = len(fs)

    def copy(outs, sems, i, half):
        x, y, c = _place()
        part = outs[i].at[_rows(fs[i].shape[0], c if half == "mine" else 1 - c, 8)]
        return pltpu.make_async_remote_copy(src_ref=part, dst_ref=part, send_sem=sems[0].at[i], recv_sem=sems[1].at[i],
                                            device_id=(x, y, 1 - c), device_id_type=MESH_T)

    def start(ins, outs, sems):
        for i in range(n):
            copy(outs, sems, i, "mine").start()

    def finish(ins, outs, sems):
        for i in range(n):
            copy(outs, sems, i, "theirs").wait_recv()
        for i in range(n):
            copy(outs, sems, i, "mine").wait_send()

    return _Stage(fs, [_sds(f.shape, f.dtype) for f in fs], (n, n), start, finish, aliases={i: i for i in range(n)})


def _row_block(r, cap=2048):
    b = min(r, cap)
    while r % b or b % 8:
        b -= 8
    return b


RS_SPLIT = 2


def _rs_add(name, gs, r1s, sel):
    n = len(gs)

    def body(sel_ref, *refs):
        g_refs, r_refs, h_refs, b_refs = (refs[k * n:(k + 1) * n] for k in range(4))
        for i in range(n):
            h = g_refs[i][...] + r_refs[i][...]
            h_refs[i][...] = h
            b_refs[i][...] = h.astype(BF16)

    def blk(g):
        return (1, g.shape[1] // 2 // RS_SPLIT, g.shape[2])

    here = lambda k, j, s: (k, j, 0)
    in_specs = [pl.BlockSpec(blk(g), lambda k, j, s: (k, s[0] * RS_SPLIT + j, 0)) for g in gs]
    in_specs += [pl.BlockSpec(blk(g), here) for g in gs]
    outs = _pcall(body, name=name,
                  out_shape=[_sds((4, g.shape[1] // 2, g.shape[2])) for g in gs]
                  + [_sds((4, g.shape[1] // 2, g.shape[2]), BF16) for g in gs],
                  grid_spec=pltpu.PrefetchScalarGridSpec(num_scalar_prefetch=1, grid=(4, RS_SPLIT), in_specs=in_specs,
                                                         out_specs=[pl.BlockSpec(blk(g), here) for g in gs] * 2),
                  compiler_params=_cparams(("parallel", "parallel")))(sel.reshape(1).astype(jnp.int32), *gs, *r1s)
    return outs[:n], outs[n:]


def _rs_sum(name, hs, r2s, chip, half):
    n = len(hs)

    def body(sel_ref, *refs):
        h_refs, r_refs, o_refs = (refs[k * n:(k + 1) * n] for k in range(3))
        for i in range(n):
            r = r_refs[i]
            o_refs[i][...] = ((h_refs[i][0] + r[0].astype(F32)) + r[1].astype(F32)) + r[2].astype(F32)

    def rows(h):
        return h.shape[1] // RS_SPLIT

    in_specs = [pl.BlockSpec((1, rows(h), h.shape[2]), lambda j, s: (s[0], j, 0)) for h in hs]
    in_specs += [pl.BlockSpec((3, rows(h), h.shape[2]), lambda j, s: (0, j, 0)) for h in hs]
    sel = jnp.stack([chip, half]).astype(jnp.int32)
    return _pcall(body, name=name, out_shape=[_sds((2 * h.shape[1], h.shape[2])) for h in hs],
                  grid_spec=pltpu.PrefetchScalarGridSpec(
                      num_scalar_prefetch=1, grid=(RS_SPLIT,), in_specs=in_specs,
                      out_specs=[pl.BlockSpec((rows(h), h.shape[2]), lambda j, s: (s[1] * RS_SPLIT + j, 0)) for h in hs]),
                  compiler_params=_cparams(("parallel",)))(sel, *hs, *r2s)


def _sum8(name, a):
    _, r, n = a.shape
    br = _row_block(r)

    def body(a_ref, o_ref):
        acc = a_ref[0]
        for k in range(1, 8):
            acc = acc + a_ref[k]
        o_ref[...] = acc

    return _pcall(body, name=name, out_shape=_sds((r, n)), grid=(r // br,),
                  in_specs=[pl.BlockSpec((8, br, n), lambda j: (0, j, 0))], out_specs=pl.BlockSpec((br, n), lambda j: (j, 0)),
                  compiler_params=_cparams(("parallel",)))(a)


def _adamw(name, ws, gs, ms, vs, nblk):
    n = len(ws)

    def body(*refs):
        w_refs, g_refs, m_refs, v_refs, d_refs, nm_refs, nv_refs = (refs[k * n:(k + 1) * n] for k in range(7))
        for i in range(n):
            gv = g_refs[i][...]
            nm = ADAM_B1 * m_refs[i][...] + (1.0 - ADAM_B1) * gv
            nv = ADAM_B2 * v_refs[i][...] + (1.0 - ADAM_B2) * (gv * gv)
            nm_refs[i][...] = nm
            nv_refs[i][...] = nv
            m_hat = nm / (1.0 - ADAM_B1 ** ADAM_STEP)
            v_hat = nv / (1.0 - ADAM_B2 ** ADAM_STEP)
            d_refs[i][...] = -ADAM_LR * (m_hat / (jnp.sqrt(v_hat) + ADAM_EPS) + ADAM_WD * w_refs[i][...])

    specs = [pl.BlockSpec((w.shape[0] // nblk, w.shape[1]), lambda j: (j, 0)) for w in ws]
    outs = _pcall(body, name=name, out_shape=[_sds(w.shape) for w in ws] * 3, grid=(nblk,), in_specs=specs * 4,
                  out_specs=specs * 3, compiler_params=_cparams(("parallel",)))(*ws, *gs, *ms, *vs)
    return outs[:n], outs[n:2 * n], outs[2 * n:]


ADA_COLS = 2304
ADA_BLK = 768


def _ada_fwd(name, c_all, w_shard, b_cols):
    nb = c_all.shape[0]

    def body(c_ref, w_ref, b_ref, o_ref):
        cv = c_ref[...]
        cs = _mx(cv * _sigmoid(cv))
        o_ref[...] = _dot(cs, _mx(w_ref[...])) + b_ref[...]

    return _pcall(body, name=name, out_shape=_sds((nb, ADA_COLS)), grid=(ADA_COLS // ADA_BLK,),
                  in_specs=[pl.BlockSpec((nb, D), lambda j: (0, 0)), pl.BlockSpec((D, ADA_BLK), lambda j: (0, j)),
                            pl.BlockSpec((1, ADA_BLK), lambda j: (0, j))],
                  out_specs=pl.BlockSpec((nb, ADA_BLK), lambda j: (0, j)),
                  compiler_params=_cparams(("parallel",)))(c_all, w_shard, b_cols)


def _ada_bwd(name, c_all, dmod_cols, dmod_all):
    nb = c_all.shape[0]

    def body(c_ref, dc_ref, da_ref, gw_ref, gb_ref):
        cv = c_ref[...]
        cs = _mx(cv * _sigmoid(cv))
        gw_ref[...] = lax.dot_general(cs, _mx(dc_ref[...]), (((0,), (0,)), ((), ())), preferred_element_type=F32)

        @pl.when(pl.program_id(0) == 0)
        def _():
            gb_ref[...] = jnp.sum(da_ref[...], axis=0, keepdims=True)

    return _pcall(body, name=name, out_shape=[_sds((D, ADA_COLS)), _sds((1, 9 * D))], grid=(ADA_COLS // ADA_BLK,),
                  in_specs=[pl.BlockSpec((nb, D), lambda j: (0, 0)), pl.BlockSpec((nb, ADA_BLK), lambda j: (0, j)),
                            pl.BlockSpec((nb, 9 * D), lambda j: (0, 0))],
                  out_specs=[pl.BlockSpec((D, ADA_BLK), lambda j: (0, j)), pl.BlockSpec((1, 9 * D), lambda j: (0, 0))],
                  compiler_params=_cparams(("arbitrary",)))(c_all, dmod_cols, dmod_all)


BIG = ("ffn1_w1", "ffn1_w3", "ffn1_w2", "w_in", "w_glu", "w_out", "ffn2_w1", "ffn2_w3", "ffn2_w2")
COL_SHARDED = ("w_in",)
TRANSPOSED = ("ffn1_w1", "ffn1_w3", "ffn2_w1", "ffn2_w3")
SMALL = ("b_ada", "ln1_g", "ln1_b", "conv_w", "conv_b", "dt_bias", "a_log", "d_ssd", "ssd_norm_w", "s5_a_re", "s5_a_im",
         "s5_log_dt", "s5_b_re", "s5_b_im", "s5_c_re", "s5_c_im", "s5_d", "b_glu", "ln2_g", "ln2_b", "ln3_g", "ln3_b")
WEIGHTS = ("w_ada", "b_ada", "ffn1_w1", "ffn1_w3", "ffn1_w2", "ln1_g", "ln1_b", "w_in", "conv_w", "conv_b", "dt_bias",
           "a_log", "d_ssd", "ssd_norm_w", "s5_a_re", "s5_a_im", "s5_log_dt", "s5_b_re", "s5_b_im", "s5_c_re", "s5_c_im",
           "s5_d", "w_glu", "b_glu", "w_out", "ln2_g", "ln2_b", "ffn2_w1", "ffn2_w3", "ffn2_w2", "ln3_g", "ln3_b")
BIG_PAD = 2 * 1024 * 128


def _pack(arrs, mult, axis_keep=0):
    lead = arrs[0].shape[:axis_keep]
    flat = jnp.concatenate([a.reshape(lead + (-1,)) for a in arrs], axis=axis_keep)
    pad = (-flat.shape[-1]) % mult
    if pad:
        flat = jnp.concatenate([flat, jnp.zeros(lead + (pad,), flat.dtype)], axis=axis_keep)
    return flat


def _unpack(flat, shapes):
    out, off = [], 0
    for s in shapes:
        size = math.prod(s)
        out.append(flat[..., off:off + size].reshape(flat.shape[:-1] + tuple(s)))
        off += size
    return out


def _shard_major(a):
    rows, cols = a.shape
    return a.reshape(rows, 4, cols // 4).transpose(1, 0, 2)


def _from_shard_major(a):
    _, rows, w = a.shape
    return a.transpose(1, 0, 2).reshape(rows, 4 * w)


def kernel(x, c, w_ada, b_ada, ffn1_w1, ffn1_w3, ffn1_w2, ln1_g, ln1_b, w_in, conv_w, conv_b, dt_bias, a_log, d_ssd, ssd_norm_w, s5_a_re, s5_a_im, s5_log_dt, s5_b_re, s5_b_im, s5_c_re, s5_c_im, s5_d, w_glu, b_glu, w_out, ln2_g, ln2_b, ffn2_w1, ffn2_w3, ffn2_w2, ln3_g, ln3_b, loss_target, m_w_ada, m_b_ada, m_ffn1_w1, m_ffn1_w3, m_ffn1_w2, m_ln1_g, m_ln1_b, m_w_in, m_conv_w, m_conv_b, m_dt_bias, m_a_log, m_d_ssd, m_ssd_norm_w, m_s5_a_re, m_s5_a_im, m_s5_log_dt, m_s5_b_re, m_s5_b_im, m_s5_c_re, m_s5_c_im, m_s5_d, m_w_glu, m_b_glu, m_w_out, m_ln2_g, m_ln2_b, m_ffn2_w1, m_ffn2_w3, m_ffn2_w2, m_ln3_g, m_ln3_b, v_w_ada, v_b_ada, v_ffn1_w1, v_ffn1_w3, v_ffn1_w2, v_ln1_g, v_ln1_b, v_w_in, v_conv_w, v_conv_b, v_dt_bias, v_a_log, v_d_ssd, v_ssd_norm_w, v_s5_a_re, v_s5_a_im, v_s5_log_dt, v_s5_b_re, v_s5_b_im, v_s5_c_re, v_s5_c_im, v_s5_d, v_w_glu, v_b_glu, v_w_out, v_ln2_g, v_ln2_b, v_ffn2_w1, v_ffn2_w3, v_ffn2_w2, v_ln3_g, v_ln3_b):
    a = dict(locals())
    xi, yi, ci = _place()
    chip = 2 * xi + yi
    dev = 2 * chip + ci
    nb, seq, _ = x.shape
    t = nb * seq
    ndev = 8

    c_rows = nb * D // 128
    c_cw = _allgather8("gather_c", jnp.concatenate([c.reshape(c_rows, 128), conv_w.reshape(-1, 128)], axis=0))
    c_all = c_cw[:, :c_rows].reshape(ndev * nb, D)
    b_cols = lax.dynamic_slice(b_ada, (0, chip * ADA_COLS), (1, ADA_COLS))
    mod_part = _ada_fwd("ada_fwd", c_all, w_ada[0], b_cols)
    mod_parts = _allgather8("gather_mod", mod_part.reshape(-1, 128)).reshape(ndev, ndev * nb, ADA_COLS)
    mod_all = mod_parts[0::2].transpose(1, 0, 2).reshape(ndev * nb, 9 * D)
    mod = lax.dynamic_slice(mod_all, (nb * dev, 0), (nb, 9 * D)).reshape(nb, 9, D)

    def nat(n):
        return jnp.swapaxes(a[n], 1, 2)[0] if n[-7:] in TRANSPOSED else a[n][0]

    def gather(names):
        own = [nat(n).astype(MXU_DTYPE) for n in names]

        def weights(pieces):
            w = {}
            for n, mine, piece in zip(names, own, pieces):
                piece = lax.dynamic_update_slice(piece, mine[None], (chip, 0, 0))
                if n in COL_SHARDED:
                    wi = _from_shard_major(piece)
                    w[n] = jnp.concatenate([wi[:, :1536], wi[:, 1544:2056], wi[:, 1536:1544],
                                            jnp.zeros((D, 120), wi.dtype)], axis=1)
                else:
                    w[n + "t" if n in TRANSPOSED else n] = piece.reshape(-1, piece.shape[-1])
            for n in names:
                have, want = (n + "t", n) if n in TRANSPOSED else (n, n + "t")
                w[want] = w[have].T
            return w

        return _gather_stage(own), weights

    first_stage, first_weights = gather(BIG[:3])
    w = first_weights(_run_stage("gather_w_ffn1", first_stage))
    late_stage, late_weights = gather(BIG[3:])

    conv_full = _from_shard_major(c_cw[0::2, c_rows:].reshape(4, 4, 256))
    pad8 = lambda v: jnp.concatenate([v.reshape(1, NH), jnp.zeros((1, 128 - NH), F32)], axis=1)
    sp = dict(ln1_g=ln1_g, ln1_b=ln1_b, ln2_g=ln2_g, ln2_b=ln2_b, ln3_g=ln3_g, ln3_b=ln3_b, conv_w=conv_full,
              conv_b=conv_b, dt_bias=pad8(dt_bias), a_log=pad8(a_log), d_rep=jnp.repeat(d_ssd[0], HP)[None],
              ssd_norm_w=ssd_norm_w, s5_a_re=s5_a_re[0], s5_a_im=s5_a_im[0], s5_log_dt=s5_log_dt[0], s5_b_re=s5_b_re[0],
              s5_b_im=s5_b_im[0], s5_c_re=s5_c_re[0], s5_c_im=s5_c_im[0], s5_d=s5_d, b_glu=b_glu)

    lsum, dx0, dmod, gbig, small = _local_step(x.reshape(t, D), loss_target.reshape(t, D), mod, w, sp, seq,
                                               dist=(late_stage, late_weights, (ci, chip)))
    loss = lax.psum(lsum * (0.5 / D), ("x", "y", "c"))

    dmod_all = _allgather8("gather_dmod", dmod.reshape(-1, 128)).reshape(ndev * nb, 9 * D)
    dmod_cols = lax.dynamic_slice(dmod_all, (0, chip * ADA_COLS), (ndev * nb, ADA_COLS))
    g_w_ada, g_b_ada = _ada_bwd("ada_bwd", c_all, dmod_cols, dmod_all)

    gbig["w_ada"] = g_w_ada

    outs = {}
    for call, names in (("adamw_a", ("ffn1_w1", "ffn1_w3", "ffn1_w2", "w_in", "w_glu", "w_out")),
                        ("adamw_b", ("ffn2_w1", "ffn2_w3", "ffn2_w2", "w_ada"))):
        res = _adamw(call, [nat(n) for n in names], [gbig[n] for n in names], [nat("m_" + n) for n in names],
                     [nat("v_" + n) for n in names], 8)
        for kind, arrs in zip(("grad", "delta", "new_m", "new_v"), ([gbig[n] for n in names],) + tuple(res)):
            for n, arr in zip(names, arrs):
                outs[kind, n] = (arr.T if n in TRANSPOSED else arr)[None]

    snames = [n for n in SMALL if n != "b_ada"]
    sgrad = dict(small)
    sgrad["b_glu"] = small["w_glu_b"]
    svec = _pack([sgrad[n] for n in snames], 1024).reshape(-1, 128)
    ssum = _sum8("small_sum", _allgather8("gather_small", svec)).reshape(-1)

    def view2d(u):
        s = u.shape[1:]
        return u.reshape((1, s[0]) if len(s) == 1 else (-1, s[-1]))

    vshape = {n: view2d(a[n]).shape for n in SMALL}
    gsm = dict(zip(snames, _unpack(ssum, [vshape[n] if n != "conv_w" else (4, D) for n in snames])))
    gsm["conv_w"] = lax.dynamic_slice(gsm["conv_w"], (0, chip * 256), (4, 256))
    gsm["b_ada"] = g_b_ada
    res = _adamw("adamw_small", [view2d(a[n]) for n in SMALL], [gsm[n] for n in SMALL],
                 [view2d(a["m_" + n]) for n in SMALL], [view2d(a["v_" + n]) for n in SMALL], 1)
    for kind, arrs in zip(("grad", "delta", "new_m", "new_v"), ([gsm[n] for n in SMALL],) + tuple(res)):
        for n, arr in zip(SMALL, arrs):
            outs[kind, n] = arr.reshape(a[n].shape)

    res = [loss, dx0.reshape(nb, seq, D)]
    for kind in ("grad", "delta", "new_m", "new_v"):
        res += [outs[kind, n] for n in WEIGHTS]
    return tuple(res)
```

```python
import functools
import math

import jax
import jax.numpy as jnp
from jax import lax
from jax.experimental import pallas as pl
from jax.experimental.pallas import tpu as pltpu

F32 = jnp.float32
BF16 = jnp.bfloat16
MXU_DTYPE = jnp.bfloat16

D = 1024
FF = 2816
FB = 1408
NH, HP, NS, NG = 8, 64, 128, 2
CH = 128
SW = 512
S5G, S5P, S5H = 32, 64, 16
S5L = S5G * S5P
PW = 2176
ALPHA = 2.0 ** 0.25
LN_EPS = 1e-5
ADAM_LR, ADAM_B1, ADAM_B2, ADAM_EPS, ADAM_WD, ADAM_STEP = 0.001, 0.9, 0.999, 1e-08, 0.01, 10
VMEM_LIMIT = 56 * 1024 * 1024
MESH_T = pl.DeviceIdType.MESH


def _pcall(body, **kw):
    return pl.pallas_call(body, **kw)


def _cparams(sem=None, **kw):
    return pltpu.CompilerParams(dimension_semantics=sem, vmem_limit_bytes=VMEM_LIMIT, **kw)


def _dot(a, b):
    return jnp.dot(a, b, preferred_element_type=F32)


def _dot_nt(a, b):
    return lax.dot_general(a, b, (((1,), (1,)), ((), ())), preferred_element_type=F32)


def _dot_hi(a, b):
    return jnp.dot(a, b, preferred_element_type=F32, precision=lax.Precision.HIGHEST)


def _mx(a):
    return a.astype(MXU_DTYPE)


def _sigmoid(x):
    return 1.0 / (1.0 + jnp.exp(-x))


def _iota(shape, axis):
    return lax.broadcasted_iota(jnp.int32, shape, axis)


def _rowcall(name, fn, n_rows, tm, tpe, *, tiled=(), halos=(), exs=(), res=(), out_tiled=(), out_acc=(),
             out_exacc=(), scratch=(), reverse=False, batch=None, carry=None, carry_mid=0.0):
    if batch:
        n_rows //= batch
    nt = n_rows // tm

    def blk(i):
        return (nt - 1 - i) if reverse else i

    in_specs, args = [], []
    for a in tiled:
        if batch:
            in_specs.append(pl.BlockSpec((batch, tm, a.shape[1]), lambda i: (0, blk(i), 0)))
            args.append(a.reshape(batch, n_rows, a.shape[1]))
            continue
        in_specs.append(pl.BlockSpec((tm, a.shape[1]), lambda i: (blk(i), 0)))
        args.append(a)
    for a, rows in halos:
        r = tm // rows
        if batch:
            in_specs.append(pl.BlockSpec((batch, rows, a.shape[1]), lambda i, r=r: (0, jnp.maximum(blk(i) * r - 1, 0), 0)))
            args.append(a.reshape(batch, n_rows, a.shape[1]))
            continue
        in_specs.append(pl.BlockSpec((rows, a.shape[1]), lambda i, r=r: (jnp.maximum(blk(i) * r - 1, 0), 0)))
        args.append(a)
    for a in exs:
        in_specs.append(pl.BlockSpec((1,) + a.shape[1:], lambda i: (blk(i) // tpe, 0, 0)))
        args.append(a)
    for a in res:
        nd = a.ndim
        in_specs.append(pl.BlockSpec(a.shape, lambda i, nd=nd: (0,) * nd, pipeline_mode=pl.Buffered(1)))
        args.append(a)
    any_spec = pl.BlockSpec(memory_space=pl.ANY)
    st_ins = carry.ins if carry else []
    st_outs = carry.out_shapes if carry else []
    st_sems = carry.sem_shapes() if carry else []
    base_in = len(args)
    in_specs += [any_spec] * len(st_ins)
    args += st_ins
    out_specs, out_shape = [], []
    for s in out_tiled:
        if batch:
            out_specs.append(pl.BlockSpec((batch, tm, s.shape[1]), lambda i: (0, blk(i), 0)))
            out_shape.append(_sds((batch, n_rows, s.shape[1]), s.dtype))
            continue
        out_specs.append(pl.BlockSpec((tm, s.shape[1]), lambda i: (blk(i), 0)))
        out_shape.append(s)
    for s in out_acc:
        nd = len(s.shape)
        out_specs.append(pl.BlockSpec(s.shape, lambda i, nd=nd: (0,) * nd))
        out_shape.append(s)
    for s in out_exacc:
        out_specs.append(pl.BlockSpec((1,) + s.shape[1:], lambda i: (blk(i) // tpe, 0, 0)))
        out_shape.append(s)
    base_out = len(out_shape)
    out_specs += [any_spec] * len(st_outs)
    out_shape += st_outs
    aliases = {base_in + k: base_out + v for k, v in carry.aliases.items()} if carry else {}
    n = [len(tiled), len(halos), len(exs), len(res), len(st_ins), len(out_tiled), len(out_acc), len(out_exacc),
         len(st_outs), len(scratch), len(st_sems)]

    def body(*refs):
        groups, k = [], 0
        for m in n:
            groups.append(refs[k:k + m])
            k += m
        i = pl.program_id(0)
        b = blk(i)

        class ctx:
            first = i == 0
            pos = b % tpe
            seq_first = (b % tpe) == 0
            seq_last = (b % tpe) == tpe - 1
            ex_enter = (i % tpe) == 0

        def work():
            for cond, refs_ in ((ctx.first, groups[6]), (ctx.ex_enter, groups[7])):
                if refs_:
                    @pl.when(cond)
                    def _():
                        for r in refs_:
                            r[...] = jnp.zeros(r.shape, r.dtype)
            fn(ctx, *groups[0:4], *groups[5:8], groups[9])

        if carry:
            carry.run((groups[4], groups[8], groups[10]), i == 0, i == min(nt - 1, int(carry_mid * nt)), i == nt - 1, work)
        else:
            work()

    outs = _pcall(body, name=name, grid=(nt,), in_specs=in_specs, out_specs=out_specs, out_shape=out_shape,
                  input_output_aliases=aliases, scratch_shapes=list(scratch) + st_sems,
                  compiler_params=_cparams(("arbitrary",)))(*args)
    if batch:
        outs = [o.reshape(batch * n_rows, o.shape[2]) if k < len(out_tiled) else o for k, o in enumerate(outs)]
    return (outs[:base_out], outs[base_out:]) if carry else outs


def _acc(ref, val):
    ref[...] += val


def _sds(shape, dtype=F32):
    return jax.ShapeDtypeStruct(shape, dtype)


def _ln_fwd(r, g, b):
    mu = jnp.mean(r, axis=-1, keepdims=True)
    rc = r - mu
    var = jnp.mean(rc * rc, axis=-1, keepdims=True)
    return rc * lax.rsqrt(var + LN_EPS) * g + b


def _ln_bwd(r, g, dy):
    mu = jnp.mean(r, axis=-1, keepdims=True)
    rc = r - mu
    var = jnp.mean(rc * rc, axis=-1, keepdims=True)
    rstd = lax.rsqrt(var + LN_EPS)
    xhat = rc * rstd
    dxh = dy * g
    dr = rstd * (dxh - jnp.mean(dxh, axis=-1, keepdims=True) - xhat * jnp.mean(dxh * xhat, axis=-1, keepdims=True))
    return dr, jnp.sum(dy * xhat, axis=0, keepdims=True), jnp.sum(dy, axis=0, keepdims=True)


def _ffn_fwd(name, x, mod, k0, w1, w3, w2, lng, lnb, seq, tgt=None, **carry):
    t = x.shape[0]
    tm = 256
    with_loss = tgt is not None

    def fn(ctx, tiled, halos, exs, res, outs, accs, exaccs, scr):
        x_ref = tiled[0]
        mod_ref, = exs
        w1_ref, w3_ref, w2_ref, g_ref, b_ref = res
        xo_ref, r_ref, h_ref, ab_ref, f_ref = outs[:5]
        xv = x_ref[...]
        sh, sc, g = mod_ref[0, k0:k0 + 1, :], mod_ref[0, k0 + 1:k0 + 2, :], mod_ref[0, k0 + 2:k0 + 3, :]
        h = _mx(xv * (1.0 + sc) + sh)
        h_ref[...] = h
        f = jnp.zeros((tm, D), F32)
        for j in range(2):
            a = _dot(h, w1_ref[:, j * FB:(j + 1) * FB])
            b = _dot(h, w3_ref[:, j * FB:(j + 1) * FB])
            ab_ref[:, j * FB:(j + 1) * FB] = a
            ab_ref[:, FF + j * FB:FF + (j + 1) * FB] = b
            s = a * _sigmoid(a) * b
            f = f + _dot(_mx(s), w2_ref[j * FB:(j + 1) * FB, :])
        f_ref[...] = f
        r = ALPHA * xv + 0.5 * g * f
        r_ref[...] = r
        xo = _ln_fwd(r, g_ref[...], b_ref[...])
        xo_ref[...] = xo
        if with_loss:
            e = xo - tiled[1][...]
            outs[5][...] = e * (1.0 / D)
            _acc(accs[0], jnp.sum(e * e) * jnp.ones((8, 128), F32))

    out_tiled = [_sds((t, D)), _sds((t, D)), _sds((t, D), MXU_DTYPE), _sds((t, 2 * FF)), _sds((t, D))]
    if with_loss:
        out_tiled.append(_sds((t, D)))
    return _rowcall(name, fn, t, tm, seq // tm, tiled=[x] + ([tgt] if with_loss else []), exs=[mod],
                    res=[w1, w3, w2, lng, lnb], out_tiled=out_tiled, out_acc=[_sds((8, 128))] if with_loss else [], **carry)


def _ffn_bwd1(name, dxo, r, ab, f, mod, k0, lng, w2t, seq, **carry):
    t = dxo.shape[0]
    tm = 256

    def fn(ctx, tiled, halos, exs, res, outs, accs, exaccs, scr):
        dxo_ref, r_ref, ab_ref, f_ref = tiled
        mod_ref, = exs
        g_ref, w2t_ref = res
        dr_ref, df_ref, s_ref, dab_ref = outs
        g = mod_ref[0, k0 + 2:k0 + 3, :]
        dr, dgam, dbet = _ln_bwd(r_ref[...], g_ref[...], dxo_ref[...])
        dr_ref[...] = dr
        _acc(accs[0], dgam)
        _acc(accs[1], dbet)
        _acc(exaccs[0].at[0], jnp.sum(0.5 * f_ref[...] * dr, axis=0, keepdims=True))
        df = _mx(0.5 * g * dr)
        df_ref[...] = df
        for j in range(2):
            ds = _dot(df, w2t_ref[:, j * FB:(j + 1) * FB])
            a = ab_ref[:, j * FB:(j + 1) * FB]
            b = ab_ref[:, FF + j * FB:FF + (j + 1) * FB]
            sig = _sigmoid(a)
            silu = a * sig
            s_ref[:, j * FB:(j + 1) * FB] = _mx(silu * b)
            dab_ref[:, j * FB:(j + 1) * FB] = _mx(ds * b * (sig * (1.0 + a * (1.0 - sig))))
            dab_ref[:, FF + j * FB:FF + (j + 1) * FB] = _mx(ds * silu)

    b = mod.shape[0]
    return _rowcall(name, fn, t, tm, seq // tm, tiled=[dxo, r, ab, f], exs=[mod], res=[lng, w2t],
                    out_tiled=[_sds((t, D)), _sds((t, D), MXU_DTYPE), _sds((t, FF), MXU_DTYPE),
                               _sds((t, 2 * FF), MXU_DTYPE)],
                    out_acc=[_sds((1, D)), _sds((1, D))], out_exacc=[_sds((b, 1, D))], **carry)


def _mod_bwd(name, dab, dr, x, mod, k0, wts, seq, extra=(), **carry):
    t = dr.shape[0]
    tm = 256
    nin = 1 + len(extra)
    width = dab.shape[1] + sum(e.shape[1] for e in extra)

    def fn(ctx, tiled, halos, exs, res, outs, accs, exaccs, scr):
        parts = tiled[:nin]
        dr_ref, x_ref = tiled[nin:]
        mod_ref, = exs
        sc = mod_ref[0, k0 + 1:k0 + 2, :]
        if nin == 1:
            dp = parts[0][...]
        else:
            dp = jnp.concatenate([_mx(p[...]) for p in parts], axis=1)
            outs[1][...] = dp
        dh, off = jnp.zeros((tm, D), F32), 0
        for wt_ref in res:
            dh = dh + _dot(dp[:, off:off + wt_ref.shape[0]], wt_ref[...])
            off += wt_ref.shape[0]
        outs[0][...] = ALPHA * dr_ref[...] + dh * (1.0 + sc)
        _acc(exaccs[0].at[0], jnp.sum(dh, axis=0, keepdims=True))
        _acc(exaccs[1].at[0], jnp.sum(dh * x_ref[...], axis=0, keepdims=True))

    b = mod.shape[0]
    out_tiled = [_sds((t, D))] + ([_sds((t, width), MXU_DTYPE)] if nin > 1 else [])
    return _rowcall(name, fn, t, tm, seq // tm, tiled=[dab, *extra, dr, x], exs=[mod], res=list(wts),
                    out_tiled=out_tiled, out_exacc=[_sds((b, 1, D)), _sds((b, 1, D))], **carry)


def _tn_matmul(name, a, b, bm, bn, bt=512, carry=None, a_cols=None):
    t = a.shape[0]
    start, m = a_cols or (0, a.shape[1])
    off = start // bm
    n = b.shape[1]
    grid = (m // bm, n // bn, t // bt)
    n_in, n_out = (len(carry.ins), len(carry.out_shapes)) if carry else (0, 0)

    def body(a_ref, b_ref, *refs):
        o_ref = refs[n_in]

        def work():
            @pl.when(pl.program_id(2) == 0)
            def _():
                o_ref[...] = jnp.zeros(o_ref.shape, F32)
            o_ref[...] += lax.dot_general(a_ref[...], b_ref[...], (((0,), (0,)), ((), ())), preferred_element_type=F32)

        if not carry:
            return work()
        step = (pl.program_id(0) * grid[1] + pl.program_id(1)) * grid[2] + pl.program_id(2)
        carry.run((refs[:n_in], refs[n_in + 1:n_in + 1 + n_out], refs[n_in + 1 + n_out:]), step == 0, step == 0,
                  step == grid[0] * grid[1] * grid[2] - 1, work)

    any_spec = pl.BlockSpec(memory_space=pl.ANY)
    outs = _pcall(body, name=name, grid=grid,
                  in_specs=[pl.BlockSpec((bt, bm), lambda i, j, k: (k, i + off)), pl.BlockSpec((bt, bn), lambda i, j, k: (k, j))]
                  + [any_spec] * n_in,
                  out_specs=[pl.BlockSpec((bm, bn), lambda i, j, k: (i, j))] + [any_spec] * n_out,
                  out_shape=[_sds((m, n))] + (carry.out_shapes if carry else []),
                  input_output_aliases={2 + k: 1 + v for k, v in carry.aliases.items()} if carry else {},
                  scratch_shapes=carry.sem_shapes() if carry else [],
                  compiler_params=_cparams(("arbitrary",) * 3 if carry else ("parallel", "parallel", "arbitrary")))(
                      a, b, *(carry.ins if carry else []))
    return (outs[0], outs[1:]) if carry else outs[0]


S5_BLK = 16


def _tn_diag(name, a, b, bt=1024):
    t = a.shape[0]
    rows, cols = S5_BLK * S5P, S5_BLK * S5H
    nblk = a.shape[1] // rows

    def body(a_ref, b_ref, o_ref):
        @pl.when(pl.program_id(1) == 0)
        def _():
            o_ref[...] = jnp.zeros(o_ref.shape, F32)
        o_ref[0] += lax.dot_general(a_ref[...], b_ref[...], (((0,), (0,)), ((), ())), preferred_element_type=F32)

    return _pcall(body, name=name, grid=(nblk, t // bt),
                  in_specs=[pl.BlockSpec((bt, rows), lambda i, k: (k, i)),
                            pl.BlockSpec((bt, cols), lambda i, k: (k, i % (S5G // S5_BLK)))],
                  out_specs=pl.BlockSpec((1, rows, cols), lambda i, k: (i, 0, 0)), out_shape=_sds((nblk, rows, cols)),
                  compiler_params=_cparams(("parallel", "arbitrary")))(a, b)


def _diag_groups(o):
    o = o.reshape(2, S5G // S5_BLK, S5_BLK, S5P, S5_BLK, S5H)
    return jnp.einsum("rbgpgh->rbgph", o).reshape(2, S5G, S5P, S5H)


def _proj_fwd(name, x, mod, w_in, seq):
    t = x.shape[0]
    tm = 256

    def fn(ctx, tiled, halos, exs, res, outs, accs, exaccs, scr):
        mod_ref, = exs
        sh, sc = mod_ref[0, 3:4, :], mod_ref[0, 4:5, :]
        h = _mx(tiled[0][...] * (1.0 + sc) + sh)
        outs[0][...] = h
        outs[1][...] = _dot(h, res[0][...])

    return _rowcall(name, fn, t, tm, seq // tm, tiled=[x], exs=[mod], res=[w_in],
                    out_tiled=[_sds((t, D), MXU_DTYPE), _sds((t, PW))])


def _shift_rows(cur, prev8, j):
    if j == 0:
        return cur
    rolled = pltpu.roll(cur, j, 0)
    top = jnp.where(_iota((8, cur.shape[1]), 0) < j, pltpu.roll(prev8, j, 0), rolled[0:8])
    return jnp.concatenate([top, rolled[8:]], axis=0)


def _shift_rows_up(cur, next8, j):
    if j == 0:
        return cur
    n = cur.shape[0]
    rolled = pltpu.roll(cur, n - j, 0)
    bot = jnp.where(_iota((8, cur.shape[1]), 0) >= 8 - j, pltpu.roll(next8, 8 - j, 0), rolled[n - 8:n])
    return jnp.concatenate([rolled[:n - 8], bot], axis=0)


def _softplus(x):
    return jnp.maximum(x, 0.0) + jnp.log(1.0 + jnp.exp(-jnp.abs(x)))


def _ssd_common(proj_ref, xpre, dtb_ref, alog_ref):
    xbc = xpre * _sigmoid(xpre)
    xs, bm, cm = xbc[:, 0:SW], xbc[:, SW:SW + 256], xbc[:, SW + 256:SW + 512]
    dtraw = proj_ref[:, PW - 128:PW] + dtb_ref[...]
    dt = _softplus(dtraw)
    a = -jnp.exp(alog_ref[...])
    tril = (_iota((CH, CH), 0) >= _iota((CH, CH), 1)).astype(F32)
    acs = _dot_hi(tril, dt * a)
    return xs, bm, cm, dtraw, dt, a, acs, acs.T


def _pair_lane(lo, hi):
    r = lo.shape[0]
    return jnp.where(_iota((r, 128), 1) < HP, lo, hi)


def _ssd_fwd(name, proj, conv_w, conv_b, dt_bias, a_log, d_rep, norm_w, seq):
    t = proj.shape[0]
    nb = t // seq

    def fn(ctx, tiled, halos, exs, res, outs, accs, exaccs, scr):
        @pl.when(ctx.seq_first)
        def _():
            scr[0][...] = jnp.zeros(scr[0].shape, F32)

        for b in range(nb):
            one(ctx, res, [r.at[b] for r in tiled + halos + outs + scr])

    def one(ctx, res, refs):
        proj_ref, halo_ref, yo_ref, xpre_ref, y_ref, sprev_ref, state_ref = refs
        cw_ref, cb_ref, dtb_ref, alog_ref, d_ref, nw_ref = res
        raw = proj_ref[:, SW:SW + D]
        prev8 = halo_ref[:, SW:SW + D] * jnp.where(ctx.seq_first, 0.0, 1.0)
        xpre = cb_ref[...] + cw_ref[3:4, :] * raw
        for j in (1, 2, 3):
            xpre = xpre + cw_ref[3 - j:4 - j, :] * _shift_rows(raw, prev8, j)
        xpre_ref[...] = xpre
        xs, bm, cm, dtraw, dt, a, acs, acst = _ssd_common(proj_ref, xpre, dtb_ref, alog_ref)
        causal = _iota((CH, CH), 0) >= _iota((CH, CH), 1)
        lane_lo = _iota((CH, 128), 1) < HP
        sprev = state_ref[...]
        sprev_ref[...] = sprev
        ys = []
        for g in range(NG):
            bmg, cmg = bm[:, g * NS:(g + 1) * NS], cm[:, g * NS:(g + 1) * NS]
            bmt = bmg.T
            cb = _dot(_mx(cmg), _mx(bmt))
            for q in (2 * g, 2 * g + 1):
                xsq = xs[:, 128 * q:128 * q + 128]
                xd = xsq * _pair_lane(dt[:, 2 * q:2 * q + 1], dt[:, 2 * q + 1:2 * q + 2])
                sp = sprev[:, 128 * q:128 * q + 128]
                ydiag = jnp.zeros((CH, 128), F32)
                snew = jnp.zeros((NS, 128), F32)
                for jj in range(2):
                    h = 2 * q + jj
                    col, row = acs[:, h:h + 1], acst[h:h + 1, :]
                    lm = jnp.where(causal, jnp.exp(jnp.minimum(col - row, 0.0)), 0.0)
                    xm = _mx(jnp.where(lane_lo if jj == 0 else ~lane_lo, xd, 0.0))
                    ydiag = ydiag + _dot(_mx(cb * lm), xm)
                    dec_row = jnp.exp(acst[h:h + 1, CH - 1:CH] - row)
                    snew = snew + _dot(_mx(bmt * dec_row), xm)
                e_pair = jnp.exp(_pair_lane(acs[:, 2 * q:2 * q + 1], acs[:, 2 * q + 1:2 * q + 2]))
                yoff = _dot(_mx(cmg), _mx(sp)) * e_pair
                cd = jnp.exp(_pair_lane(acst[2 * q:2 * q + 1, CH - 1:CH], acst[2 * q + 1:2 * q + 2, CH - 1:CH]))
                state_ref[:, 128 * q:128 * q + 128] = cd * sp + snew
                ys.append(ydiag + yoff + d_ref[:, 128 * q:128 * q + 128] * xsq)
        y = jnp.concatenate(ys, axis=1)
        y_ref[...] = y
        z = proj_ref[:, 0:SW]
        yz = y * (z * _sigmoid(z))
        outp = []
        for g in range(NG):
            seg = yz[:, 256 * g:256 * g + 256]
            rinv = lax.rsqrt(jnp.mean(seg * seg, axis=-1, keepdims=True) + LN_EPS)
            outp.append(seg * rinv * nw_ref[:, 256 * g:256 * g + 256])
        yo_ref[...] = _mx(jnp.concatenate(outp, axis=1))

    return _rowcall(name, fn, t, CH, seq // CH, tiled=[proj], halos=[(proj, 8)],
                    res=[conv_w, conv_b, dt_bias, a_log, d_rep, norm_w],
                    out_tiled=[_sds((t, SW), MXU_DTYPE), _sds((t, D)), _sds((t, SW)), _sds((t, SW))],
                    scratch=[pltpu.VMEM((nb, NS, SW), F32)], batch=nb)


def _ssd_bwd(name, dyo, proj, xpre_all, y_all, sprev_all, conv_w, dt_bias, a_log, d_rep, norm_w, seq, **carry):
    t = proj.shape[0]
    nb = t // seq

    def fn(ctx, tiled, halos, exs, res, outs, accs, exaccs, scr):
        @pl.when(ctx.seq_last)
        def _():
            for r in scr:
                r[...] = jnp.zeros(r.shape, F32)

        for b in range(nb):
            one(ctx, res, accs, [r.at[b] for r in tiled + halos + outs + scr])

    def one(ctx, res, accs, refs):
        dyo_ref, proj_ref, xpre_ref, y_ref, sprev_ref, halo_ref, dzx_ref, ddt_ref, ds_ref, nxt_ref = refs
        cw_ref, dtb_ref, alog_ref, d_ref, nw_ref = res
        dnw_acc, dd_acc, dcw_acc, dcb_acc, ddtb_acc, dalog_acc = accs
        xpre = xpre_ref[...]
        xs, bm, cm, dtraw, dt, a, acs, acst = _ssd_common(proj_ref, xpre, dtb_ref, alog_ref)
        y = y_ref[...]
        z = proj_ref[:, 0:SW]
        sz = _sigmoid(z)
        siluz = z * sz
        yz = y * siluz
        dyo = dyo_ref[...]
        dyz_parts, dnw_parts = [], []
        for g in range(NG):
            seg = yz[:, 256 * g:256 * g + 256]
            rinv = lax.rsqrt(jnp.mean(seg * seg, axis=-1, keepdims=True) + LN_EPS)
            yn = seg * rinv
            dseg = dyo[:, 256 * g:256 * g + 256]
            dnw_parts.append(jnp.sum(dseg * yn, axis=0, keepdims=True))
            dyn = dseg * nw_ref[:, 256 * g:256 * g + 256]
            dyz_parts.append(rinv * (dyn - yn * jnp.mean(dyn * yn, axis=-1, keepdims=True)))
        dyz = jnp.concatenate(dyz_parts, axis=1)
        _acc(dnw_acc, jnp.concatenate(dnw_parts, axis=1))
        dy = dyz * siluz
        dz = dyz * y * (sz * (1.0 + z * (1.0 - sz)))
        _acc(dd_acc, jnp.sum(dy * xs, axis=0, keepdims=True))

        causal = _iota((CH, CH), 0) >= _iota((CH, CH), 1)
        anti = _iota((CH, CH), 0) <= _iota((CH, CH), 1)
        lane_lo = _iota((CH, 128), 1) < HP
        lane_id = _iota((CH, 128), 1)
        last_row = _iota((CH, 128), 0) == CH - 1
        sprev = sprev_ref[...]
        dacs = jnp.zeros((CH, 128), F32)
        ddt_x = jnp.zeros((CH, 128), F32)
        dxs_parts, dbm_parts, dcm_parts = [], [], []
        for g in range(NG):
            bmg, cmg = bm[:, g * NS:(g + 1) * NS], cm[:, g * NS:(g + 1) * NS]
            bmt, cmt = bmg.T, cmg.T
            cb = _dot(_mx(cmg), _mx(bmt))
            cbt = _dot(_mx(bmg), _mx(cmt))
            dcb = jnp.zeros((CH, CH), F32)
            dcbt = jnp.zeros((CH, CH), F32)
            dbmg = jnp.zeros((CH, NS), F32)
            dcmg = jnp.zeros((CH, NS), F32)
            for q in (2 * g, 2 * g + 1):
                sl = slice(128 * q, 128 * q + 128)
                xsq = xs[:, sl]
                dtp = _pair_lane(dt[:, 2 * q:2 * q + 1], dt[:, 2 * q + 1:2 * q + 2])
                xd = xsq * dtp
                dyq = dy[:, sl]
                sp = sprev[:, sl]
                dsn = ds_ref[:, sl]
                e_pair = jnp.exp(_pair_lane(acs[:, 2 * q:2 * q + 1], acs[:, 2 * q + 1:2 * q + 2]))
                cd = jnp.exp(_pair_lane(acst[2 * q:2 * q + 1, CH - 1:CH], acst[2 * q + 1:2 * q + 2, CH - 1:CH]))
                dye = dyq * e_pair
                dcmg = dcmg + _dot(_mx(dye), _mx(sp.T))
                dsp = _dot(_mx(cmt), _mx(dye)) + cd * dsn
                yoff = _dot(_mx(cmg), _mx(sp)) * e_pair
                dacs_lane = dyq * yoff
                dxd = jnp.zeros((CH, 128), F32)
                sds = jnp.sum(dsn * sp, axis=0, keepdims=True) * cd
                for jj in range(2):
                    h = 2 * q + jj
                    hm = lane_lo if jj == 0 else ~lane_lo
                    col, row = acs[:, h:h + 1], acst[h:h + 1, :]
                    lm = jnp.where(causal, jnp.exp(jnp.minimum(col - row, 0.0)), 0.0)
                    lmt = jnp.where(anti, jnp.exp(jnp.minimum(row - col, 0.0)), 0.0)
                    xm = _mx(jnp.where(hm, xd, 0.0))
                    dym = _mx(jnp.where(hm, dyq, 0.0))
                    gm = _dot_nt(dym, xm)
                    gmt = _dot_nt(xm, dym)
                    dcb = dcb + gm * lm
                    dcbt = dcbt + gmt * lmt
                    dxd = dxd + _dot(_mx(cbt * lmt), dym)
                    w = gm * cb * lm
                    wt = gmt * cbt * lmt
                    dacs_h = jnp.sum(w, axis=1, keepdims=True) - jnp.sum(wt, axis=1, keepdims=True)
                    alast = acst[h:h + 1, CH - 1:CH]
                    dec_col = jnp.exp(alast - col)
                    dsm = _mx(jnp.where(hm[0:NS], dsn, 0.0))
                    qh = _dot_nt(xm, dsm)
                    dbmg = dbmg + qh * dec_col
                    ddec = jnp.sum(qh * bmg, axis=1, keepdims=True)
                    dxd = dxd + _dot(_mx(bmg * dec_col), dsm)
                    dacs_h = dacs_h - ddec * dec_col
                    dacs_h = dacs_h + jnp.sum(jnp.where(hm, dacs_lane, 0.0), axis=1, keepdims=True)
                    tail = jnp.sum(ddec * dec_col, axis=0, keepdims=True) + jnp.sum(
                        jnp.where(hm[0:1], sds, 0.0), axis=1, keepdims=True)
                    dacs = dacs + jnp.where(lane_id == h, dacs_h, 0.0) + jnp.where(
                        last_row & (lane_id == h), tail, 0.0)
                ds_ref[:, sl] = dsp
                for jj in range(2):
                    h = 2 * q + jj
                    hm = lane_lo if jj == 0 else ~lane_lo
                    ddt_x = ddt_x + jnp.where(lane_id == h, jnp.sum(jnp.where(hm, dxd * xsq, 0.0), axis=1, keepdims=True), 0.0)
                dxs_parts.append(dxd * dtp + d_ref[:, sl] * dyq)
            dcmg = dcmg + _dot(_mx(dcb), _mx(bmg))
            dbmg = dbmg + _dot(_mx(dcbt), _mx(cmg))
            dbm_parts.append(dbmg)
            dcm_parts.append(dcmg)
        triu = (_iota((CH, CH), 0) <= _iota((CH, CH), 1)).astype(F32)
        dadt = _dot_hi(triu, dacs)
        ddt = dadt * a + ddt_x
        _acc(dalog_acc, jnp.sum(dadt * dt, axis=0, keepdims=True) * a)
        ddtraw = ddt * _sigmoid(dtraw)
        ddt_ref[...] = ddtraw
        _acc(ddtb_acc, jnp.sum(ddtraw, axis=0, keepdims=True))
        dxbc = jnp.concatenate(dxs_parts + dbm_parts + dcm_parts, axis=1)
        sx = _sigmoid(xpre)
        dpre = dxbc * (sx * (1.0 + xpre * (1.0 - sx)))
        _acc(dcb_acc, jnp.sum(dpre, axis=0, keepdims=True))
        raw = proj_ref[:, SW:SW + D]
        prev8 = halo_ref[:, SW:SW + D] * jnp.where(ctx.seq_first, 0.0, 1.0)
        next8 = nxt_ref[...]
        draw = cw_ref[3:4, :] * dpre
        dcw = [None] * 4
        dcw[3] = jnp.sum(dpre * raw, axis=0, keepdims=True)
        for j in (1, 2, 3):
            dcw[3 - j] = jnp.sum(dpre * _shift_rows(raw, prev8, j), axis=0, keepdims=True)
            draw = draw + cw_ref[3 - j:4 - j, :] * _shift_rows_up(dpre, next8, j)
        _acc(dcw_acc, jnp.concatenate(dcw + [jnp.zeros((4, D), F32)], axis=0))
        nxt_ref[...] = dpre[0:8]
        dzx_ref[:, 0:SW] = dz
        dzx_ref[:, SW:SW + D] = draw

    return _rowcall(name, fn, t, CH, seq // CH, tiled=[dyo, proj, xpre_all, y_all, sprev_all], halos=[(proj, 8)],
                    res=[conv_w, dt_bias, a_log, d_rep, norm_w],
                    out_tiled=[_sds((t, SW + D)), _sds((t, 128))],
                    out_acc=[_sds((1, SW)), _sds((1, SW)), _sds((8, D)), _sds((1, D)), _sds((1, 128)), _sds((1, 128))],
                    scratch=[pltpu.VMEM((nb, NS, SW), F32), pltpu.VMEM((nb, 8, D), F32)], reverse=True, batch=nb, **carry)


def _gelu(y):
    k = math.sqrt(2.0 / math.pi)
    return 0.5 * y * (1.0 + jnp.tanh(k * (y + 0.044715 * y * y * y)))


def _gelu_grad(y):
    k = math.sqrt(2.0 / math.pi)
    th = jnp.tanh(k * (y + 0.044715 * y * y * y))
    return 0.5 * (1.0 + th) + 0.5 * y * (1.0 - th * th) * k * (1.0 + 3.0 * 0.044715 * y * y)


S5T = 256


def _cmul_add(xr, xi, ar, ai, sr, si):
    return xr + ar * sr - ai * si, xi + ar * si + ai * sr


def _s5_fwd(name, proj, bbd, cbd, pw, tab, d5, w_glu, b_glu, seq):
    t = proj.shape[0]
    tm = S5T

    def fn(ctx, tiled, halos, exs, res, outs, accs, exaccs, scr):
        proj_ref, = tiled
        bbd_ref, cbd_ref, pw_ref, tab_ref, d_ref, wg_ref, bg_ref = res
        out_ref, xst_ref, y_ref, xb_ref, ub_ref = outs
        carry_ref, = scr

        @pl.when(ctx.seq_first)
        def _():
            carry_ref[...] = jnp.zeros(carry_ref.shape, F32)

        u = proj_ref[:, 1536:2048]
        bu = _dot(_mx(u), bbd_ref[...])
        xr, xi = bu[:, :S5L].reshape(tm // 8, 8, S5L), bu[:, S5L:].reshape(tm // 8, 8, S5L)
        for k, sh in enumerate((1, 2, 4)):
            xr, xi = _cmul_add(xr, xi, pw_ref[k, :, :S5L], pw_ref[k, :, S5L:], pltpu.roll(xr, sh, 1), pltpu.roll(xi, sh, 1))
        xst_ref[:, :S5L] = xr.reshape(tm, S5L)
        xst_ref[:, S5L:] = xi.reshape(tm, S5L)

        def tile_fix(i, c):
            cr, ci = c
            rows = pl.ds(pl.multiple_of(i * 8, 8), 8)
            tr, ti = _cmul_add(xst_ref[rows, :S5L], xst_ref[rows, S5L:], tab_ref[:, :S5L], tab_ref[:, S5L:], cr, ci)
            xst_ref[rows, :S5L] = tr
            xst_ref[rows, S5L:] = ti
            return tr[7:8], ti[7:8]

        cr, ci = lax.fori_loop(0, tm // 8, tile_fix, (carry_ref[0:1, :S5L], carry_ref[0:1, S5L:]))
        carry_ref[0:1, :S5L] = cr
        carry_ref[0:1, S5L:] = ci
        xb = _mx(xst_ref[...])
        xb_ref[...] = xb
        ub_ref[...] = _mx(u)
        y = _dot(xb, cbd_ref[...]) + u * d_ref[...]
        y_ref[...] = y
        g = _gelu(y)
        v = _dot(_mx(g), wg_ref[...]) + bg_ref[...]
        out_ref[...] = _mx(g * _sigmoid(v))

    return _rowcall(name, fn, t, tm, seq // tm, tiled=[proj], res=[bbd, cbd, pw, tab, d5, w_glu, b_glu],
                    out_tiled=[_sds((t, SW), MXU_DTYPE), _sds((t, 2 * S5L)), _sds((t, SW)),
                               _sds((t, 2 * S5L), MXU_DTYPE), _sds((t, SW), MXU_DTYPE)],
                    scratch=[pltpu.VMEM((8, 2 * S5L), F32)])


def _s5_bwd(name, dout, proj, xst, y_all, bbdt, cbdt, pwc, tabc, d5, w_glu, w_glut, b_glu, seq, **carry):
    t = proj.shape[0]
    tm = S5T

    def fn(ctx, tiled, halos, exs, res, outs, accs, exaccs, scr):
        dout_ref, proj_ref, xst_ref, y_ref = tiled
        halo_ref, = halos
        bbdt_ref, cbdt_ref, pw_ref, tab_ref, d_ref, wg_ref, wgt_ref, bg_ref = res
        du_ref, lam_ref, dyb_ref, gb_ref, dvb_ref = outs
        da_acc, dd_acc, dbg_acc = accs
        carry_ref, lamf_ref = scr

        @pl.when(ctx.seq_last)
        def _():
            carry_ref[...] = jnp.zeros(carry_ref.shape, F32)

        u = proj_ref[:, 1536:2048]
        y = y_ref[...]
        g = _gelu(y)
        v = _dot(_mx(g), wg_ref[...]) + bg_ref[...]
        sg = _sigmoid(v)
        dout = dout_ref[...]
        dv = dout * g * sg * (1.0 - sg)
        dvb = _mx(dv)
        dvb_ref[...] = dvb
        gb_ref[...] = _mx(g)
        _acc(dbg_acc, jnp.sum(dv, axis=0, keepdims=True))
        dg = dout * sg + _dot(dvb, wgt_ref[...])
        dy = dg * _gelu_grad(y)
        dyb = _mx(dy)
        dyb_ref[...] = dyb
        _acc(dd_acc, jnp.sum(dy * u, axis=0, keepdims=True))
        dx = _dot(dyb, cbdt_ref[...])
        xr, xi = dx[:, :S5L].reshape(tm // 8, 8, S5L), dx[:, S5L:].reshape(tm // 8, 8, S5L)
        for k, sh in enumerate((1, 2, 4)):
            xr, xi = _cmul_add(xr, xi, pw_ref[k, :, :S5L], pw_ref[k, :, S5L:], pltpu.roll(xr, 8 - sh, 1),
                               pltpu.roll(xi, 8 - sh, 1))
        lamf_ref[:, :S5L] = xr.reshape(tm, S5L)
        lamf_ref[:, S5L:] = xi.reshape(tm, S5L)

        def tile_fix(i, c):
            cr, ci = c
            rows = pl.ds(pl.multiple_of((tm // 8 - 1 - i) * 8, 8), 8)
            tr, ti = _cmul_add(lamf_ref[rows, :S5L], lamf_ref[rows, S5L:], tab_ref[:, :S5L], tab_ref[:, S5L:], cr, ci)
            lamf_ref[rows, :S5L] = tr
            lamf_ref[rows, S5L:] = ti
            return tr[0:1], ti[0:1]

        cr, ci = lax.fori_loop(0, tm // 8, tile_fix, (carry_ref[0:1, :S5L], carry_ref[0:1, S5L:]))
        carry_ref[0:1, :S5L] = cr
        carry_ref[0:1, S5L:] = ci
        lam = lamf_ref[...]
        lamb = _mx(lam)
        lam_ref[...] = lamb
        du_ref[...] = dy * d_ref[...] + _dot(lamb, bbdt_ref[...])
        prev8 = halo_ref[...] * jnp.where(ctx.seq_first, 0.0, 1.0)
        xprev = _shift_rows(xst_ref[...], prev8, 1)
        lr, li = lam[:, :S5L], lam[:, S5L:]
        pr, pi = xprev[:, :S5L], xprev[:, S5L:]
        dar = jnp.sum(lr * pr + li * pi, axis=0, keepdims=True)
        dai = jnp.sum(li * pr - lr * pi, axis=0, keepdims=True)
        _acc(da_acc, jnp.concatenate([dar, dai], axis=1))

    return _rowcall(name, fn, t, tm, seq // tm, tiled=[dout, proj, xst, y_all], halos=[(xst, 8)],
                    res=[bbdt, cbdt, pwc, tabc, d5, w_glu, w_glut, b_glu],
                    out_tiled=[_sds((t, SW)), _sds((t, 2 * S5L), MXU_DTYPE), _sds((t, SW), MXU_DTYPE),
                               _sds((t, SW), MXU_DTYPE), _sds((t, SW), MXU_DTYPE)],
                    out_acc=[_sds((1, 2 * S5L)), _sds((1, SW)), _sds((1, SW))],
                    scratch=[pltpu.VMEM((8, 2 * S5L), F32), pltpu.VMEM((tm, 2 * S5L), F32)], reverse=True, **carry)


def _out_fwd(name, yssd, ys5, x1, mod, w_out, lng, lnb, seq):
    t = x1.shape[0]
    tm = 256

    def fn(ctx, tiled, halos, exs, res, outs, accs, exaccs, scr):
        ya_ref, yb_ref, x_ref = tiled
        mod_ref, = exs
        w_ref, g_ref, b_ref = res
        m = _dot(ya_ref[...], w_ref[0:SW, :]) + _dot(yb_ref[...], w_ref[SW:2 * SW, :])
        r = ALPHA * x_ref[...] + mod_ref[0, 5:6, :] * m
        outs[0][...] = _ln_fwd(r, g_ref[...], b_ref[...])
        outs[1][...] = r
        outs[2][...] = m

    return _rowcall(name, fn, t, tm, seq // tm, tiled=[yssd, ys5, x1], exs=[mod], res=[w_out, lng, lnb],
                    out_tiled=[_sds((t, D)), _sds((t, D)), _sds((t, D))])


def _out_bwd(name, dxo, r, m, mod, lng, w_outt, seq, **carry):
    t = dxo.shape[0]
    tm = 256

    def fn(ctx, tiled, halos, exs, res, outs, accs, exaccs, scr):
        dxo_ref, r_ref, m_ref = tiled
        mod_ref, = exs
        g_ref, wt_ref = res
        dr, dgam, dbet = _ln_bwd(r_ref[...], g_ref[...], dxo_ref[...])
        outs[0][...] = dr
        _acc(accs[0], dgam)
        _acc(accs[1], dbet)
        _acc(exaccs[0].at[0], jnp.sum(dr * m_ref[...], axis=0, keepdims=True))
        dm = _mx(mod_ref[0, 5:6, :] * dr)
        outs[1][...] = dm
        dyc = _dot(dm, wt_ref[...])
        outs[2][...] = dyc[:, 0:SW]
        outs[3][...] = dyc[:, SW:2 * SW]

    b = mod.shape[0]
    return _rowcall(name, fn, t, tm, seq // tm, tiled=[dxo, r, m], exs=[mod], res=[lng, w_outt],
                    out_tiled=[_sds((t, D)), _sds((t, D), MXU_DTYPE), _sds((t, SW)), _sds((t, SW))],
                    out_acc=[_sds((1, D)), _sds((1, D))], out_exacc=[_sds((b, 1, D))], **carry)


def _s5_discretise(a_re, a_im, log_dt, b_re, b_im):
    dt = jnp.exp(log_dt)[:, None]
    mag = jnp.exp(dt * a_re)
    ab_re, ab_im = mag * jnp.cos(dt * a_im), mag * jnp.sin(dt * a_im)
    den = a_re * a_re + a_im * a_im
    nr, ni = ab_re - 1.0, ab_im
    f_re, f_im = (nr * a_re + ni * a_im) / den, (ni * a_re - nr * a_im) / den
    bb_re = f_re[..., None] * b_re - f_im[..., None] * b_im
    bb_im = f_re[..., None] * b_im + f_im[..., None] * b_re
    return ab_re, ab_im, bb_re, bb_im


def _s5_tables(ab_re, ab_im):
    ar, ai = ab_re.reshape(1, S5L), ab_im.reshape(1, S5L)
    pows = [(ar, ai)]
    for _ in range(7):
        pr, pi = pows[-1]
        pows.append((pr * ar - pi * ai, pr * ai + pi * ar))

    def pack(rows, sign):
        return jnp.concatenate([jnp.concatenate([r for r, _ in rows], axis=0),
                                jnp.concatenate([sign * i for _, i in rows], axis=0)], axis=1)

    row = jnp.arange(8)[:, None]
    pw = jnp.stack([jnp.where(row >= sh, pack([pows[sh - 1]], 1.0), 0.0) for sh in (1, 2, 4)])
    pwc = jnp.stack([jnp.where(row < 8 - sh, pack([pows[sh - 1]], -1.0), 0.0) for sh in (1, 2, 4)])
    tab = pack(pows, 1.0)
    tabc = pack(pows[::-1], -1.0)
    return pw, tab, pwc, tabc


class _GradGroup:
    def __init__(self, tag, grads, place):
        self.tag, self.names, (self.half, self.chip) = tag, list(grads), place
        self.gsh = [_shard_major(g) if n in COL_SHARDED else g.reshape(4, g.shape[0] // 4, g.shape[1])
                    for n, g in grads.items()]

    def sibling(self):
        return _rs_sibling_stage(self.gsh)

    def chips(self, received):
        self.sums, sums_bf = _rs_add(self.tag + "_add", self.gsh, received, self.half)
        return _rs_chips_stage(sums_bf)

    def join(self, received):
        return _rs_join_stage(_rs_sum(self.tag + "_sum", self.sums, received, self.chip, self.half))

    def result(self, joined):
        return dict(zip(self.names, joined))


def _hid(fn, *args, stage=None, **kw):
    if stage is None:
        return fn(*args, **kw), None
    return fn(*args, carry=stage, **kw)


def _local_step(x, tgt, mod, w, sp, seq, dist=None):
    t = x.shape[0]
    mxu = MXU_DTYPE
    big = {}

    def group(tag, grads):
        if dist is None:
            big.update(grads)
            return None
        return _GradGroup(tag, grads, dist[2])

    (x1, r1, h1, ab1, f1), late = _hid(_ffn_fwd, "ffn1_fwd", x, mod, 0, w["ffn1_w1"], w["ffn1_w3"], w["ffn1_w2"], sp["ln1_g"],
                                       sp["ln1_b"], seq, stage=dist and dist[0], **({"carry_mid": 0.75} if dist else {}))
    if dist:
        w = {**w, **dist[1](late)}
    h2, proj = _proj_fwd("proj_fwd", x1, mod, w["w_in"], seq)
    yssd, xpre, yraw, sprev = _ssd_fwd("ssd_fwd", proj, sp["conv_w"], sp["conv_b"], sp["dt_bias"], sp["a_log"],
                                       sp["d_rep"], sp["ssd_norm_w"], seq)
    (ab_re, ab_im, bb_re, bb_im), disc_vjp = jax.vjp(_s5_discretise, sp["s5_a_re"], sp["s5_a_im"], sp["s5_log_dt"],
                                                     sp["s5_b_re"], sp["s5_b_im"])
    eye = jnp.eye(S5G, dtype=F32)
    bbd = jnp.concatenate([jnp.einsum("gk,gph->ghkp", eye, bb_re).reshape(SW, S5L),
                           jnp.einsum("gk,gph->ghkp", eye, bb_im).reshape(SW, S5L)], axis=1).astype(mxu)
    cbd = jnp.concatenate([jnp.einsum("gk,ghp->gpkh", eye, sp["s5_c_re"]).reshape(S5L, SW),
                           -jnp.einsum("gk,ghp->gpkh", eye, sp["s5_c_im"]).reshape(S5L, SW)], axis=0).astype(mxu)
    bbdt = jnp.concatenate([jnp.einsum("gk,gph->kpgh", eye, bb_re).reshape(S5L, SW),
                            jnp.einsum("gk,gph->kpgh", eye, bb_im).reshape(S5L, SW)], axis=0).astype(mxu)
    cbdt = jnp.concatenate([jnp.einsum("gk,ghp->khgp", eye, sp["s5_c_re"]).reshape(SW, S5L),
                            -jnp.einsum("gk,ghp->khgp", eye, sp["s5_c_im"]).reshape(SW, S5L)], axis=1).astype(mxu)
    pw, tab, pwc, tabc = _s5_tables(lax.stop_gradient(ab_re), lax.stop_gradient(ab_im))
    ys5, xst, y5, xstb, ub = _s5_fwd("s5_fwd", proj, bbd, cbd, pw, tab, sp["s5_d"], w["w_glu"], sp["b_glu"], seq)
    x2, r2, m2 = _out_fwd("out_fwd", yssd, ys5, x1, mod, w["w_out"], sp["ln2_g"], sp["ln2_b"], seq)
    x3, r3, h3, ab3, f3, dy3, loss_acc = _ffn_fwd("ffn2_fwd", x2, mod, 6, w["ffn2_w1"], w["ffn2_w3"], w["ffn2_w2"], sp["ln3_g"],
                                                  sp["ln3_b"], seq, tgt=tgt)
    dr3, df3, s3, dab3, dg3g, dg3b, dgate3 = _ffn_bwd1("ffn2_bwd1", dy3, r3, ab3, f3, mod, 6, sp["ln3_g"],
                                                        w["ffn2_w2t"], seq)
    dx2, dsh3, dsc3 = _mod_bwd("ffn2_bwd2", dab3, dr3, x2, mod, 6, [w["ffn2_w1t"], w["ffn2_w3t"]], seq)
    grp = group("rs_ffn2", dict(ffn2_w1=_tn_matmul("ffn2_dw1", dab3, h3, FB, D, a_cols=(0, FF)),
                                ffn2_w3=_tn_matmul("ffn2_dw3", dab3, h3, FB, D, a_cols=(FF, FF)),
                                ffn2_w2=_tn_matmul("ffn2_dw2", s3, df3, FB, D)))
    (dr2, dm2, dyssd, dys5, dg2g, dg2b, dgate2), got = _hid(_out_bwd, "out_bwd", dx2, r2, m2, mod, sp["ln2_g"],
                                                          w["w_outt"], seq, stage=grp and grp.sibling())
    g_w_out = jnp.concatenate([_tn_matmul("dw_out_a", yssd, dm2, SW, D), _tn_matmul("dw_out_b", ys5, dm2, SW, D)], axis=0)
    (du, lam, dy5b, g5b, dv5b, da5, dd5, dbglu), got = _hid(
        _s5_bwd, "s5_bwd", dys5, proj, xst, y5, bbdt, cbdt, pwc, tabc, sp["s5_d"], w["w_glu"], w["w_glut"], sp["b_glu"],
        seq, stage=grp and grp.chips(got))
    g_w_glu = _tn_matmul("dw_glu", g5b, dv5b, SW, SW)
    dbb = _diag_groups(_tn_diag("s5_db", lam, ub))
    dcc = _diag_groups(_tn_diag("s5_dc", xstb, dy5b))
    (dzx, ddt, dnw, ddl, dcw, dcb, ddtb, dalog), got = _hid(
        _ssd_bwd, "ssd_bwd", dyssd, proj, xpre, yraw, sprev, sp["conv_w"], sp["dt_bias"], sp["a_log"], sp["d_rep"],
        sp["ssd_norm_w"], seq, stage=grp and grp.join(got))
    if grp:
        big.update(grp.result(got))
    dx1, dproj, dsh2, dsc2 = _mod_bwd("proj_bwd", dzx, dr2, x1, mod, 3, [w["w_int"]], seq, extra=(du, ddt))
    gwi = _tn_matmul("dw_in", h2, dproj, D, PW)
    grp = group("rs_mix", dict(w_in=jnp.concatenate([gwi[:, :1536], gwi[:, 2048:2056], gwi[:, 1536:2048]], axis=1),
                               w_glu=g_w_glu, w_out=g_w_out))
    (dr1, df1, s1, dab1, dg1g, dg1b, dgate1), got = _hid(_ffn_bwd1, "ffn1_bwd1", dx1, r1, ab1, f1, mod, 0, sp["ln1_g"],
                                                          w["ffn1_w2t"], seq, stage=grp and grp.sibling())
    g1, got = _hid(_tn_matmul, "ffn1_dw1", dab1, h1, FB, D, a_cols=(0, FF), stage=grp and grp.chips(got))
    g3, got = _hid(_tn_matmul, "ffn1_dw3", dab1, h1, FB, D, a_cols=(FF, FF), stage=grp and grp.join(got))
    if grp:
        big.update(grp.result(got))
    grp = group("rs_ffn1a", dict(ffn1_w1=g1, ffn1_w3=g3))
    g2, got = _hid(_tn_matmul, "ffn1_dw2", s1, df1, FB, D, stage=grp and grp.sibling())
    grp2 = group("rs_ffn1b", dict(ffn1_w2=g2))
    (dx0, dsh1, dsc1), got = _hid(_mod_bwd, "ffn1_bwd2", dab1, dr1, x, mod, 0, [w["ffn1_w1t"], w["ffn1_w3t"]], seq,
                                  stage=grp and _merge_stages(grp.chips(got), grp2.sibling()))
    if grp:
        n = len(grp.names)
        got = _run_stage("rs_ffn1_tail", _merge_stages(grp.join(got[:n]), grp2.chips(got[n:])))
        big.update(grp.result(got[:n]))
        big.update(grp2.result(_run_stage("rs_ffn1b_join", grp2.join(got[n:]))))
    dmod = jnp.concatenate([dsh1, dsc1, dgate1, dsh2, dsc2, dgate2, dsh3, dsc3, dgate3], axis=1)
    dc_re, dc_im = dcc[0].transpose(0, 2, 1), -dcc[1].transpose(0, 2, 1)
    g_a_re, g_a_im, g_log_dt, g_b_re, g_b_im = disc_vjp(
        (da5[:, :S5L].reshape(S5G, S5P), da5[:, S5L:].reshape(S5G, S5P), dbb[0], dbb[1]))
    small = dict(ln1_g=dg1g, ln1_b=dg1b, ln2_g=dg2g, ln2_b=dg2b, ln3_g=dg3g, ln3_b=dg3b, conv_w=dcw[0:4], conv_b=dcb,
                 dt_bias=ddtb[:, :NH], a_log=dalog[:, :NH], d_ssd=jnp.sum(ddl.reshape(NH, HP), axis=1).reshape(1, NH),
                 ssd_norm_w=dnw, s5_a_re=g_a_re, s5_a_im=g_a_im, s5_log_dt=g_log_dt, s5_b_re=g_b_re, s5_b_im=g_b_im,
                 s5_c_re=dc_re, s5_c_im=dc_im, s5_d=dd5, w_glu_b=dbglu)
    return loss_acc[0, 0], dx0, dmod, big, small


def _place():
    return lax.axis_index("x"), lax.axis_index("y"), lax.axis_index("c")


def _other_chips(x, y):
    return [(1 - x, y), (x, 1 - y), (1 - x, 1 - y)]


def _allgather8(name, a):
    r, n = a.shape

    def body(x_ref, out_ref, send_sems, recv_sems, local_sem):
        x, y, c = _place()
        me, sibling = (x, y, c), (x, y, 1 - c)
        chips = _other_chips(x, y)

        def rows(px, py, pc):
            return out_ref.at[pl.ds(pl.multiple_of((4 * px + 2 * py + pc) * r, 8), r), :]

        def copy(k, block, to, src=None):
            return pltpu.make_async_remote_copy(src_ref=rows(*block) if src is None else src, dst_ref=rows(*block),
                                                send_sem=send_sems.at[k], recv_sem=recv_sems.at[k], device_id=to,
                                                device_id_type=MESH_T)

        mine = pltpu.make_async_copy(x_ref, rows(*me), local_sem)
        mine.start()
        first = [copy(0, me, sibling, src=x_ref)]
        first += [copy(1 + j, me, (*chip, c), src=x_ref) for j, chip in enumerate(chips)]
        for cp in first:
            cp.start()
        passed = [copy(4 + j, (*chip, c), sibling) for j, chip in enumerate(chips)]
        for j, chip in enumerate(chips):
            copy(1 + j, (*chip, c), me).wait_recv()
            passed[j].start()
        copy(0, sibling, me).wait_recv()
        for j, chip in enumerate(chips):
            copy(4 + j, (*chip, 1 - c), me).wait_recv()
        for cp in first + passed:
            cp.wait_send()
        mine.wait()

    out = _pcall(body, name=name, out_shape=_sds((8 * r, n), a.dtype),
                 in_specs=[pl.BlockSpec(memory_space=pltpu.VMEM)], out_specs=pl.BlockSpec(memory_space=pltpu.VMEM),
                 scratch_shapes=[pltpu.SemaphoreType.DMA((7,)), pltpu.SemaphoreType.DMA((7,)), pltpu.SemaphoreType.DMA],
                 compiler_params=_cparams())(a)
    return out.reshape(8, r, n)


class _Stage:
    def __init__(self, ins, out_shapes, n_sems, start, finish, mid=None, aliases=None):
        self.ins, self.out_shapes, self.n_sems = list(ins), list(out_shapes), tuple(n_sems)
        self.start, self.mid, self.finish = start, mid, finish
        self.aliases = dict(aliases or {})

    def sem_shapes(self):
        return [pltpu.SemaphoreType.DMA((n,)) for n in self.n_sems]

    def run(self, refs, at_start=None, at_mid=None, at_finish=None, between=None):
        if between is None:
            for part in (self.start, self.mid, self.finish):
                if part is not None:
                    part(*refs)
            return
        pl.when(at_start)(functools.partial(self.start, *refs))
        if self.mid is not None:
            pl.when(at_mid)(functools.partial(self.mid, *refs))
        between()
        pl.when(at_finish)(functools.partial(self.finish, *refs))


def _merge_stages(a, b):
    n_in, n_out, n_sem = len(a.ins), len(a.out_shapes), len(a.n_sems)

    def both(fa, fb):
        def run(ins, outs, sems):
            fa(ins[:n_in], outs[:n_out], sems[:n_sem])
            fb(ins[n_in:], outs[n_out:], sems[n_sem:])
        return run

    aliases = {**a.aliases, **{n_in + k: n_out + v for k, v in b.aliases.items()}}
    return _Stage(a.ins + b.ins, a.out_shapes + b.out_shapes, a.n_sems + b.n_sems, both(a.start, b.start),
                  both(a.finish, b.finish), aliases=aliases)


def _run_stage(name, st):
    n_in, n_out = len(st.ins), len(st.out_shapes)

    def body(*refs):
        st.run((refs[:n_in], refs[n_in:n_in + n_out], refs[n_in + n_out:]))

    any_spec = pl.BlockSpec(memory_space=pl.ANY)
    return _pcall(body, name=name, out_shape=st.out_shapes, in_specs=[any_spec] * n_in, out_specs=[any_spec] * n_out,
                  input_output_aliases=st.aliases, scratch_shapes=st.sem_shapes(), compiler_params=_cparams())(*st.ins)


def _rows(ref_rows, half, align):
    hr = ref_rows // 2
    return pl.ds(pl.multiple_of(half * hr, align), hr)


def _gather_stage(shards):
    n = len(shards)
    pairs = [(i, j) for i in range(n) for j in range(3)]

    def env(ins, outs, sems):
        x, y, c = _place()
        chips = _other_chips(x, y)

        def copy(i, k, chip, half, to, src=None):
            dst = outs[i].at[2 * chip[0] + chip[1], _rows(shards[i].shape[0], half, 16)]
            return pltpu.make_async_remote_copy(src_ref=dst if src is None else src, dst_ref=dst,
                                                send_sem=sems[0].at[6 * i + k], recv_sem=sems[1].at[6 * i + k],
                                                device_id=to, device_id_type=MESH_T)

        def first(i, j):
            return copy(i, j, (x, y), c, (*chips[j], c), src=ins[i].at[_rows(shards[i].shape[0], c, 16)])

        def passed(i, j, half):
            return copy(i, 3 + j, chips[j], half, (x, y, 1 - c))

        def landed(i, j):
            return copy(i, j, chips[j], c, (x, y, 1 - c))

        return c, first, passed, landed

    def start(ins, outs, sems):
        c, first, passed, landed = env(ins, outs, sems)
        for i, j in pairs:
            first(i, j).start()

    def mid(ins, outs, sems):
        c, first, passed, landed = env(ins, outs, sems)
        for i, j in pairs:
            landed(i, j).wait_recv()
            passed(i, j, c).start()

    def finish(ins, outs, sems):
        c, first, passed, landed = env(ins, outs, sems)
        for i, j in pairs:
            passed(i, j, 1 - c).wait_recv()
        for i, j in pairs:
            first(i, j).wait_send()
            passed(i, j, c).wait_send()

    return _Stage(shards, [_sds((4,) + s.shape, s.dtype) for s in shards], (6 * n, 6 * n), start, finish, mid)


def _rs_sibling_stage(gs):
    n = len(gs)

    def copies(ins, outs, sems):
        x, y, c = _place()
        return [pltpu.make_async_remote_copy(src_ref=ins[i].at[:, _rows(gs[i].shape[1], 1 - c, 8)], dst_ref=outs[i],
                                             send_sem=sems[0].at[i], recv_sem=sems[1].at[i], device_id=(x, y, 1 - c),
                                             device_id_type=MESH_T) for i in range(n)]

    def start(*refs):
        for cp in copies(*refs):
            cp.start()

    def finish(*refs):
        for cp in copies(*refs):
            cp.wait()

    return _Stage(gs, [_sds((4, g.shape[1] // 2, g.shape[2]), g.dtype) for g in gs], (n, n), start, finish)


def _rs_chips_stage(hs):
    n = len(hs)

    def copies(ins, outs, sems):
        x, y, c = _place()
        return [pltpu.make_async_remote_copy(src_ref=ins[i].at[2 * chip[0] + chip[1]], dst_ref=outs[i].at[j],
                                             send_sem=sems[0].at[3 * i + j], recv_sem=sems[1].at[3 * i + j],
                                             device_id=(*chip, c), device_id_type=MESH_T)
                for i in range(n) for j, chip in enumerate(_other_chips(x, y))]

    def start(*refs):
        for cp in copies(*refs):
            cp.start()

    def finish(*refs):
        for cp in copies(*refs):
            cp.wait()

    return _Stage(hs, [_sds((3,) + h.shape[1:], h.dtype) for h in hs], (3 * n, 3 * n), start, finish)


def _rs_join_stage(fs):
    n = len(fs)

    def copy(outs, sems, i, half):
        x, y, c = _place()
        part = outs[i].at[_rows(fs[i].shape[0], c if half == "mine" else 1 - c, 8)]
        return pltpu.make_async_remote_copy(src_ref=part, dst_ref=part, send_sem=sems[0].at[i], recv_sem=sems[1].at[i],
                                            device_id=(x, y, 1 - c), device_id_type=MESH_T)

    def start(ins, outs, sems):
        for i in range(n):
            copy(outs, sems, i, "mine").start()

    def finish(ins, outs, sems):
        for i in range(n):
            copy(outs, sems, i, "theirs").wait_recv()
        for i in range(n):
            copy(outs, sems, i, "mine").wait_send()

    return _Stage(fs, [_sds(f.shape, f.dtype) for f in fs], (n, n), start, finish, aliases={i: i for i in range(n)})


def _row_block(r, cap=2048):
    b = min(r, cap)
    while r % b or b % 8:
        b -= 8
    return b


RS_SPLIT = 2


def _rs_add(name, gs, r1s, sel):
    n = len(gs)

    def body(sel_ref, *refs):
        g_refs, r_refs, h_refs, b_refs = (refs[k * n:(k + 1) * n] for k in range(4))
        for i in range(n):
            h = g_refs[i][...] + r_refs[i][...]
            h_refs[i][...] = h
            b_refs[i][...] = h.astype(BF16)

    def blk(g):
        return (1, g.shape[1] // 2 // RS_SPLIT, g.shape[2])

    here = lambda k, j, s: (k, j, 0)
    in_specs = [pl.BlockSpec(blk(g), lambda k, j, s: (k, s[0] * RS_SPLIT + j, 0)) for g in gs]
    in_specs += [pl.BlockSpec(blk(g), here) for g in gs]
    outs = _pcall(body, name=name,
                  out_shape=[_sds((4, g.shape[1] // 2, g.shape[2])) for g in gs]
                  + [_sds((4, g.shape[1] // 2, g.shape[2]), BF16) for g in gs],
                  grid_spec=pltpu.PrefetchScalarGridSpec(num_scalar_prefetch=1, grid=(4, RS_SPLIT), in_specs=in_specs,
                                                         out_specs=[pl.BlockSpec(blk(g), here) for g in gs] * 2),
                  compiler_params=_cparams(("parallel", "parallel")))(sel.reshape(1).astype(jnp.int32), *gs, *r1s)
    return outs[:n], outs[n:]


def _rs_sum(name, hs, r2s, chip, half):
    n = len(hs)

    def body(sel_ref, *refs):
        h_refs, r_refs, o_refs = (refs[k * n:(k + 1) * n] for k in range(3))
        for i in range(n):
            r = r_refs[i]
            o_refs[i][...] = ((h_refs[i][0] + r[0].astype(F32)) + r[1].astype(F32)) + r[2].astype(F32)

    def rows(h):
        return h.shape[1] // RS_SPLIT

    in_specs = [pl.BlockSpec((1, rows(h), h.shape[2]), lambda j, s: (s[0], j, 0)) for h in hs]
    in_specs += [pl.BlockSpec((3, rows(h), h.shape[2]), lambda j, s: (0, j, 0)) for h in hs]
    sel = jnp.stack([chip, half]).astype(jnp.int32)
    return _pcall(body, name=name, out_shape=[_sds((2 * h.shape[1], h.shape[2])) for h in hs],
                  grid_spec=pltpu.PrefetchScalarGridSpec(
                      num_scalar_prefetch=1, grid=(RS_SPLIT,), in_specs=in_specs,
                      out_specs=[pl.BlockSpec((rows(h), h.shape[2]), lambda j, s: (s[1] * RS_SPLIT + j, 0)) for h in hs]),
                  compiler_params=_cparams(("parallel",)))(sel, *hs, *r2s)


def _sum8(name, a):
    _, r, n = a.shape
    br = _row_block(r)

    def body(a_ref, o_ref):
        acc = a_ref[0]
        for k in range(1, 8):
            acc = acc + a_ref[k]
        o_ref[...] = acc

    return _pcall(body, name=name, out_shape=_sds((r, n)), grid=(r // br,),
                  in_specs=[pl.BlockSpec((8, br, n), lambda j: (0, j, 0))], out_specs=pl.BlockSpec((br, n), lambda j: (j, 0)),
                  compiler_params=_cparams(("parallel",)))(a)


def _adamw(name, ws, gs, ms, vs, nblk):
    n = len(ws)

    def body(*refs):
        w_refs, g_refs, m_refs, v_refs, d_refs, nm_refs, nv_refs = (refs[k * n:(k + 1) * n] for k in range(7))
        for i in range(n):
            gv = g_refs[i][...]
            nm = ADAM_B1 * m_refs[i][...] + (1.0 - ADAM_B1) * gv
            nv = ADAM_B2 * v_refs[i][...] + (1.0 - ADAM_B2) * (gv * gv)
            nm_refs[i][...] = nm
            nv_refs[i][...] = nv
            m_hat = nm / (1.0 - ADAM_B1 ** ADAM_STEP)
            v_hat = nv / (1.0 - ADAM_B2 ** ADAM_STEP)
            d_refs[i][...] = -ADAM_LR * (m_hat / (jnp.sqrt(v_hat) + ADAM_EPS) + ADAM_WD * w_refs[i][...])

    specs = [pl.BlockSpec((w.shape[0] // nblk, w.shape[1]), lambda j: (j, 0)) for w in ws]
    outs = _pcall(body, name=name, out_shape=[_sds(w.shape) for w in ws] * 3, grid=(nblk,), in_specs=specs * 4,
                  out_specs=specs * 3, compiler_params=_cparams(("parallel",)))(*ws, *gs, *ms, *vs)
    return outs[:n], outs[n:2 * n], outs[2 * n:]


ADA_COLS = 2304
ADA_BLK = 768


def _ada_fwd(name, c_all, w_shard, b_cols):
    nb = c_all.shape[0]

    def body(c_ref, w_ref, b_ref, o_ref):
        cv = c_ref[...]
        cs = _mx(cv * _sigmoid(cv))
        o_ref[...] = _dot(cs, _mx(w_ref[...])) + b_ref[...]

    return _pcall(body, name=name, out_shape=_sds((nb, ADA_COLS)), grid=(ADA_COLS // ADA_BLK,),
                  in_specs=[pl.BlockSpec((nb, D), lambda j: (0, 0)), pl.BlockSpec((D, ADA_BLK), lambda j: (0, j)),
                            pl.BlockSpec((1, ADA_BLK), lambda j: (0, j))],
                  out_specs=pl.BlockSpec((nb, ADA_BLK), lambda j: (0, j)),
                  compiler_params=_cparams(("parallel",)))(c_all, w_shard, b_cols)


def _ada_bwd(name, c_all, dmod_cols, dmod_all):
    nb = c_all.shape[0]

    def body(c_ref, dc_ref, da_ref, gw_ref, gb_ref):
        cv = c_ref[...]
        cs = _mx(cv * _sigmoid(cv))
        gw_ref[...] = lax.dot_general(cs, _mx(dc_ref[...]), (((0,), (0,)), ((), ())), preferred_element_type=F32)

        @pl.when(pl.program_id(0) == 0)
        def _():
            gb_ref[...] = jnp.sum(da_ref[...], axis=0, keepdims=True)

    return _pcall(body, name=name, out_shape=[_sds((D, ADA_COLS)), _sds((1, 9 * D))], grid=(ADA_COLS // ADA_BLK,),
                  in_specs=[pl.BlockSpec((nb, D), lambda j: (0, 0)), pl.BlockSpec((nb, ADA_BLK), lambda j: (0, j)),
                            pl.BlockSpec((nb, 9 * D), lambda j: (0, 0))],
                  out_specs=[pl.BlockSpec((D, ADA_BLK), lambda j: (0, j)), pl.BlockSpec((1, 9 * D), lambda j: (0, 0))],
                  compiler_params=_cparams(("arbitrary",)))(c_all, dmod_cols, dmod_all)


BIG = ("ffn1_w1", "ffn1_w3", "ffn1_w2", "w_in", "w_glu", "w_out", "ffn2_w1", "ffn2_w3", "ffn2_w2")
COL_SHARDED = ("w_in",)
TRANSPOSED = ("ffn1_w1", "ffn1_w3", "ffn2_w1", "ffn2_w3")
SMALL = ("b_ada", "ln1_g", "ln1_b", "conv_w", "conv_b", "dt_bias", "a_log", "d_ssd", "ssd_norm_w", "s5_a_re", "s5_a_im",
         "s5_log_dt", "s5_b_re", "s5_b_im", "s5_c_re", "s5_c_im", "s5_d", "b_glu", "ln2_g", "ln2_b", "ln3_g", "ln3_b")
WEIGHTS = ("w_ada", "b_ada", "ffn1_w1", "ffn1_w3", "ffn1_w2", "ln1_g", "ln1_b", "w_in", "conv_w", "conv_b", "dt_bias",
           "a_log", "d_ssd", "ssd_norm_w", "s5_a_re", "s5_a_im", "s5_log_dt", "s5_b_re", "s5_b_im", "s5_c_re", "s5_c_im",
           "s5_d", "w_glu", "b_glu", "w_out", "ln2_g", "ln2_b", "ffn2_w1", "ffn2_w3", "ffn2_w2", "ln3_g", "ln3_b")
BIG_PAD = 2 * 1024 * 128


def _pack(arrs, mult, axis_keep=0):
    lead = arrs[0].shape[:axis_keep]
    flat = jnp.concatenate([a.reshape(lead + (-1,)) for a in arrs], axis=axis_keep)
    pad = (-flat.shape[-1]) % mult
    if pad:
        flat = jnp.concatenate([flat, jnp.zeros(lead + (pad,), flat.dtype)], axis=axis_keep)
    return flat


def _unpack(flat, shapes):
    out, off = [], 0
    for s in shapes:
        size = math.prod(s)
        out.append(flat[..., off:off + size].reshape(flat.shape[:-1] + tuple(s)))
        off += size
    return out


def _shard_major(a):
    rows, cols = a.shape
    return a.reshape(rows, 4, cols // 4).transpose(1, 0, 2)


def _from_shard_major(a):
    _, rows, w = a.shape
    return a.transpose(1, 0, 2).reshape(rows, 4 * w)


def kernel(x, c, w_ada, b_ada, ffn1_w1, ffn1_w3, ffn1_w2, ln1_g, ln1_b, w_in, conv_w, conv_b, dt_bias, a_log, d_ssd, ssd_norm_w, s5_a_re, s5_a_im, s5_log_dt, s5_b_re, s5_b_im, s5_c_re, s5_c_im, s5_d, w_glu, b_glu, w_out, ln2_g, ln2_b, ffn2_w1, ffn2_w3, ffn2_w2, ln3_g, ln3_b, loss_target, m_w_ada, m_b_ada, m_ffn1_w1, m_ffn1_w3, m_ffn1_w2, m_ln1_g, m_ln1_b, m_w_in, m_conv_w, m_conv_b, m_dt_bias, m_a_log, m_d_ssd, m_ssd_norm_w, m_s5_a_re, m_s5_a_im, m_s5_log_dt, m_s5_b_re, m_s5_b_im, m_s5_c_re, m_s5_c_im, m_s5_d, m_w_glu, m_b_glu, m_w_out, m_ln2_g, m_ln2_b, m_ffn2_w1, m_ffn2_w3, m_ffn2_w2, m_ln3_g, m_ln3_b, v_w_ada, v_b_ada, v_ffn1_w1, v_ffn1_w3, v_ffn1_w2, v_ln1_g, v_ln1_b, v_w_in, v_conv_w, v_conv_b, v_dt_bias, v_a_log, v_d_ssd, v_ssd_norm_w, v_s5_a_re, v_s5_a_im, v_s5_log_dt, v_s5_b_re, v_s5_b_im, v_s5_c_re, v_s5_c_im, v_s5_d, v_w_glu, v_b_glu, v_w_out, v_ln2_g, v_ln2_b, v_ffn2_w1, v_ffn2_w3, v_ffn2_w2, v_ln3_g, v_ln3_b):
    a = dict(locals())
    xi, yi, ci = _place()
    chip = 2 * xi + yi
    dev = 2 * chip + ci
    nb, seq, _ = x.shape
    t = nb * seq
    ndev = 8

    c_rows = nb * D // 128
    c_cw = _allgather8("gather_c", jnp.concatenate([c.reshape(c_rows, 128), conv_w.reshape(-1, 128)], axis=0))
    c_all = c_cw[:, :c_rows].reshape(ndev * nb, D)
    b_cols = lax.dynamic_slice(b_ada, (0, chip * ADA_COLS), (1, ADA_COLS))
    mod_part = _ada_fwd("ada_fwd", c_all, w_ada[0], b_cols)
    mod_parts = _allgather8("gather_mod", mod_part.reshape(-1, 128)).reshape(ndev, ndev * nb, ADA_COLS)
    mod_all = mod_parts[0::2].transpose(1, 0, 2).reshape(ndev * nb, 9 * D)
    mod = lax.dynamic_slice(mod_all, (nb * dev, 0), (nb, 9 * D)).reshape(nb, 9, D)

    def nat(n):
        return jnp.swapaxes(a[n], 1, 2)[0] if n[-7:] in TRANSPOSED else a[n][0]

    def gather(names):
        own = [nat(n).astype(MXU_DTYPE) for n in names]

        def weights(pieces):
            w = {}
            for n, mine, piece in zip(names, own, pieces):
                piece = lax.dynamic_update_slice(piece, mine[None], (chip, 0, 0))
                if n in COL_SHARDED:
                    wi = _from_shard_major(piece)
                    w[n] = jnp.concatenate([wi[:, :1536], wi[:, 1544:2056], wi[:, 1536:1544],
                                            jnp.zeros((D, 120), wi.dtype)], axis=1)
                else:
                    w[n + "t" if n in TRANSPOSED else n] = piece.reshape(-1, piece.shape[-1])
            for n in names:
                have, want = (n + "t", n) if n in TRANSPOSED else (n, n + "t")
                w[want] = w[have].T
            return w

        return _gather_stage(own), weights

    first_stage, first_weights = gather(BIG[:3])
    w = first_weights(_run_stage("gather_w_ffn1", first_stage))
    late_stage, late_weights = gather(BIG[3:])

    conv_full = _from_shard_major(c_cw[0::2, c_rows:].reshape(4, 4, 256))
    pad8 = lambda v: jnp.concatenate([v.reshape(1, NH), jnp.zeros((1, 128 - NH), F32)], axis=1)
    sp = dict(ln1_g=ln1_g, ln1_b=ln1_b, ln2_g=ln2_g, ln2_b=ln2_b, ln3_g=ln3_g, ln3_b=ln3_b, conv_w=conv_full,
              conv_b=conv_b, dt_bias=pad8(dt_bias), a_log=pad8(a_log), d_rep=jnp.repeat(d_ssd[0], HP)[None],
              ssd_norm_w=ssd_norm_w, s5_a_re=s5_a_re[0], s5_a_im=s5_a_im[0], s5_log_dt=s5_log_dt[0], s5_b_re=s5_b_re[0],
              s5_b_im=s5_b_im[0], s5_c_re=s5_c_re[0], s5_c_im=s5_c_im[0], s5_d=s5_d, b_glu=b_glu)

    lsum, dx0, dmod, gbig, small = _local_step(x.reshape(t, D), loss_target.reshape(t, D), mod, w, sp, seq,
                                               dist=(late_stage, late_weights, (ci, chip)))
    loss = lax.psum(lsum * (0.5 / D), ("x", "y", "c"))

    dmod_all = _allgather8("gather_dmod", dmod.reshape(-1, 128)).reshape(ndev * nb, 9 * D)
    dmod_cols = lax.dynamic_slice(dmod_all, (0, chip * ADA_COLS), (ndev * nb, ADA_COLS))
    g_w_ada, g_b_ada = _ada_bwd("ada_bwd", c_all, dmod_cols, dmod_all)

    gbig["w_ada"] = g_w_ada

    outs = {}
    for call, names in (("adamw_a", ("ffn1_w1", "ffn1_w3", "ffn1_w2", "w_in", "w_glu", "w_out")),
                        ("adamw_b", ("ffn2_w1", "ffn2_w3", "ffn2_w2", "w_ada"))):
        res = _adamw(call, [nat(n) for n in names], [gbig[n] for n in names], [nat("m_" + n) for n in names],
                     [nat("v_" + n) for n in names], 8)
        for kind, arrs in zip(("grad", "delta", "new_m", "new_v"), ([gbig[n] for n in names],) + tuple(res)):
            for n, arr in zip(names, arrs):
                outs[kind, n] = (arr.T if n in TRANSPOSED else arr)[None]

    snames = [n for n in SMALL if n != "b_ada"]
    sgrad = dict(small)
    sgrad["b_glu"] = small["w_glu_b"]
    svec = _pack([sgrad[n] for n in snames], 1024).reshape(-1, 128)
    ssum = _sum8("small_sum", _allgather8("gather_small", svec)).reshape(-1)

    def view2d(u):
        s = u.shape[1:]
        return u.reshape((1, s[0]) if len(s) == 1 else (-1, s[-1]))

    vshape = {n: view2d(a[n]).shape for n in SMALL}
    gsm = dict(zip(snames, _unpack(ssum, [vshape[n] if n != "conv_w" else (4, D) for n in snames])))
    gsm["conv_w"] = lax.dynamic_slice(gsm["conv_w"], (0, chip * 256), (4, 256))
    gsm["b_ada"] = g_b_ada
    res = _adamw("adamw_small", [view2d(a[n]) for n in SMALL], [gsm[n] for n in SMALL],
                 [view2d(a["m_" + n]) for n in SMALL], [view2d(a["v_" + n]) for n in SMALL], 1)
    for kind, arrs in zip(("grad", "delta", "new_m", "new_v"), ([gsm[n] for n in SMALL],) + tuple(res)):
        for n, arr in zip(SMALL, arrs):
            outs[kind, n] = arr.reshape(a[n].shape)

    res = [loss, dx0.reshape(nb, seq, D)]
    for kind in ("grad", "delta", "new_m", "new_v"):
        res += [outs[kind, n] for n in WEIGHTS]
    return tuple(res)
```

```python
import functools
import math

import jax
import jax.numpy as jnp
from jax import lax
from jax.experimental import pallas as pl
from jax.experimental.pallas import tpu as pltpu

F32 = jnp.float32
BF16 = jnp.bfloat16
MXU_DTYPE = jnp.bfloat16

D = 1024
FF = 2816
FB = 1408
NH, HP, NS, NG = 8, 64, 128, 2
CH = 128
SW = 512
S5G, S5P, S5H = 32, 64, 16
S5L = S5G * S5P
PW = 2176
ALPHA = 2.0 ** 0.25
LN_EPS = 1e-5
ADAM_LR, ADAM_B1, ADAM_B2, ADAM_EPS, ADAM_WD, ADAM_STEP = 0.001, 0.9, 0.999, 1e-08, 0.01, 10
VMEM_LIMIT = 56 * 1024 * 1024
TM_WIDE = 512
MESH_T = pl.DeviceIdType.MESH


def _pcall(body, **kw):
    return pl.pallas_call(body, **kw)


def _cparams(sem=None, **kw):
    return pltpu.CompilerParams(dimension_semantics=sem, vmem_limit_bytes=VMEM_LIMIT, **kw)


def _dot(a, b):
    return jnp.dot(a, b, preferred_element_type=F32)


def _dot_nt(a, b):
    return lax.dot_general(a, b, (((1,), (1,)), ((), ())), preferred_element_type=F32)


def _dot_hi(a, b):
    return jnp.dot(a, b, preferred_element_type=F32, precision=lax.Precision.HIGHEST)


def _mx(a):
    return a.astype(MXU_DTYPE)


def _sigmoid(x):
    return 1.0 / (1.0 + jnp.exp(-x))


def _iota(shape, axis):
    return lax.broadcasted_iota(jnp.int32, shape, axis)


def _rowcall(name, fn, n_rows, tm, tpe, *, tiled=(), halos=(), exs=(), res=(), out_tiled=(), out_acc=(),
             out_exacc=(), scratch=(), reverse=False, batch=None, carry=None, carry_mid=0.0):
    if batch:
        n_rows //= batch
    nt = n_rows // tm

    def blk(i):
        return (nt - 1 - i) if reverse else i

    in_specs, args = [], []
    for a in tiled:
        if batch:
            in_specs.append(pl.BlockSpec((batch, tm, a.shape[1]), lambda i: (0, blk(i), 0)))
            args.append(a.reshape(batch, n_rows, a.shape[1]))
            continue
        in_specs.append(pl.BlockSpec((tm, a.shape[1]), lambda i: (blk(i), 0)))
        args.append(a)
    for a, rows in halos:
        r = tm // rows
        if batch:
            in_specs.append(pl.BlockSpec((batch, rows, a.shape[1]), lambda i, r=r: (0, jnp.maximum(blk(i) * r - 1, 0), 0)))
            args.append(a.reshape(batch, n_rows, a.shape[1]))
            continue
        in_specs.append(pl.BlockSpec((rows, a.shape[1]), lambda i, r=r: (jnp.maximum(blk(i) * r - 1, 0), 0)))
        args.append(a)
    for a in exs:
        in_specs.append(pl.BlockSpec((1,) + a.shape[1:], lambda i: (blk(i) // tpe, 0, 0)))
        args.append(a)
    for a in res:
        nd = a.ndim
        in_specs.append(pl.BlockSpec(a.shape, lambda i, nd=nd: (0,) * nd, pipeline_mode=pl.Buffered(1)))
        args.append(a)
    any_spec = pl.BlockSpec(memory_space=pl.ANY)
    st_ins = carry.ins if carry else []
    st_outs = carry.out_shapes if carry else []
    st_sems = carry.sem_shapes() if carry else []
    base_in = len(args)
    in_specs += [any_spec] * len(st_ins)
    args += st_ins
    out_specs, out_shape = [], []
    for s in out_tiled:
        if batch:
            out_specs.append(pl.BlockSpec((batch, tm, s.shape[1]), lambda i: (0, blk(i), 0)))
            out_shape.append(_sds((batch, n_rows, s.shape[1]), s.dtype))
            continue
        out_specs.append(pl.BlockSpec((tm, s.shape[1]), lambda i: (blk(i), 0)))
        out_shape.append(s)
    for s in out_acc:
        nd = len(s.shape)
        out_specs.append(pl.BlockSpec(s.shape, lambda i, nd=nd: (0,) * nd))
        out_shape.append(s)
    for s in out_exacc:
        out_specs.append(pl.BlockSpec((1,) + s.shape[1:], lambda i: (blk(i) // tpe, 0, 0)))
        out_shape.append(s)
    base_out = len(out_shape)
    out_specs += [any_spec] * len(st_outs)
    out_shape += st_outs
    aliases = {base_in + k: base_out + v for k, v in carry.aliases.items()} if carry else {}
    n = [len(tiled), len(halos), len(exs), len(res), len(st_ins), len(out_tiled), len(out_acc), len(out_exacc),
         len(st_outs), len(scratch), len(st_sems)]

    def body(*refs):
        groups, k = [], 0
        for m in n:
            groups.append(refs[k:k + m])
            k += m
        i = pl.program_id(0)
        b = blk(i)

        class ctx:
            first = i == 0
            pos = b % tpe
            seq_first = (b % tpe) == 0
            seq_last = (b % tpe) == tpe - 1
            ex_enter = (i % tpe) == 0

        def work():
            for cond, refs_ in ((ctx.first, groups[6]), (ctx.ex_enter, groups[7])):
                if refs_:
                    @pl.when(cond)
                    def _():
                        for r in refs_:
                            r[...] = jnp.zeros(r.shape, r.dtype)
            fn(ctx, *groups[0:4], *groups[5:8], groups[9])

        if carry:
            carry.run((groups[4], groups[8], groups[10]), i == 0, i == min(nt - 1, int(carry_mid * nt)), i == nt - 1, work)
        else:
            work()

    outs = _pcall(body, name=name, grid=(nt,), in_specs=in_specs, out_specs=out_specs, out_shape=out_shape,
                  input_output_aliases=aliases, scratch_shapes=list(scratch) + st_sems,
                  compiler_params=_cparams(("arbitrary",)))(*args)
    if batch:
        outs = [o.reshape(batch * n_rows, o.shape[2]) if k < len(out_tiled) else o for k, o in enumerate(outs)]
    return (outs[:base_out], outs[base_out:]) if carry else outs


def _acc(ref, val):
    ref[...] += val


def _sds(shape, dtype=F32):
    return jax.ShapeDtypeStruct(shape, dtype)


def _ln_fwd(r, g, b):
    mu = jnp.mean(r, axis=-1, keepdims=True)
    rc = r - mu
    var = jnp.mean(rc * rc, axis=-1, keepdims=True)
    return rc * lax.rsqrt(var + LN_EPS) * g + b


def _ln_bwd(r, g, dy):
    mu = jnp.mean(r, axis=-1, keepdims=True)
    rc = r - mu
    var = jnp.mean(rc * rc, axis=-1, keepdims=True)
    rstd = lax.rsqrt(var + LN_EPS)
    xhat = rc * rstd
    dxh = dy * g
    dr = rstd * (dxh - jnp.mean(dxh, axis=-1, keepdims=True) - xhat * jnp.mean(dxh * xhat, axis=-1, keepdims=True))
    return dr, jnp.sum(dy * xhat, axis=0, keepdims=True), jnp.sum(dy, axis=0, keepdims=True)


def _ffn_fwd(name, x, mod, k0, w1, w3, w2, lng, lnb, seq, tgt=None, **carry):
    t = x.shape[0]
    tm = 256
    with_loss = tgt is not None

    def fn(ctx, tiled, halos, exs, res, outs, accs, exaccs, scr):
        x_ref = tiled[0]
        mod_ref, = exs
        w1_ref, w3_ref, w2_ref, g_ref, b_ref = res
        xo_ref, r_ref, h_ref, ab_ref, f_ref = outs[:5]
        xv = x_ref[...]
        sh, sc, g = mod_ref[0, k0:k0 + 1, :], mod_ref[0, k0 + 1:k0 + 2, :], mod_ref[0, k0 + 2:k0 + 3, :]
        h = _mx(xv * (1.0 + sc) + sh)
        h_ref[...] = h
        f = jnp.zeros((tm, D), F32)
        for j in range(2):
            a = _dot(h, w1_ref[:, j * FB:(j + 1) * FB])
            b = _dot(h, w3_ref[:, j * FB:(j + 1) * FB])
            ab_ref[:, j * FB:(j + 1) * FB] = a
            ab_ref[:, FF + j * FB:FF + (j + 1) * FB] = b
            s = a * _sigmoid(a) * b
            f = f + _dot(_mx(s), w2_ref[j * FB:(j + 1) * FB, :])
        f_ref[...] = f
        r = ALPHA * xv + 0.5 * g * f
        r_ref[...] = r
        xo = _ln_fwd(r, g_ref[...], b_ref[...])
        xo_ref[...] = xo
        if with_loss:
            e = xo - tiled[1][...]
            outs[5][...] = e * (1.0 / D)
            _acc(accs[0], jnp.sum(e * e) * jnp.ones((8, 128), F32))

    out_tiled = [_sds((t, D)), _sds((t, D)), _sds((t, D), MXU_DTYPE), _sds((t, 2 * FF)), _sds((t, D))]
    if with_loss:
        out_tiled.append(_sds((t, D)))
    return _rowcall(name, fn, t, tm, seq // tm, tiled=[x] + ([tgt] if with_loss else []), exs=[mod],
                    res=[w1, w3, w2, lng, lnb], out_tiled=out_tiled, out_acc=[_sds((8, 128))] if with_loss else [], **carry)


def _ffn_bwd1(name, dxo, r, ab, f, mod, k0, lng, w2t, seq, **carry):
    t = dxo.shape[0]
    tm = 256

    def fn(ctx, tiled, halos, exs, res, outs, accs, exaccs, scr):
        dxo_ref, r_ref, ab_ref, f_ref = tiled
        mod_ref, = exs
        g_ref, w2t_ref = res
        dr_ref, df_ref, s_ref, dab_ref = outs
        g = mod_ref[0, k0 + 2:k0 + 3, :]
        dr, dgam, dbet = _ln_bwd(r_ref[...], g_ref[...], dxo_ref[...])
        dr_ref[...] = dr
        _acc(accs[0], dgam)
        _acc(accs[1], dbet)
        _acc(exaccs[0].at[0], jnp.sum(0.5 * f_ref[...] * dr, axis=0, keepdims=True))
        df = _mx(0.5 * g * dr)
        df_ref[...] = df
        for j in range(2):
            ds = _dot(df, w2t_ref[:, j * FB:(j + 1) * FB])
            a = ab_ref[:, j * FB:(j + 1) * FB]
            b = ab_ref[:, FF + j * FB:FF + (j + 1) * FB]
            sig = _sigmoid(a)
            silu = a * sig
            s_ref[:, j * FB:(j + 1) * FB] = _mx(silu * b)
            dab_ref[:, j * FB:(j + 1) * FB] = _mx(ds * b * (sig * (1.0 + a * (1.0 - sig))))
            dab_ref[:, FF + j * FB:FF + (j + 1) * FB] = _mx(ds * silu)

    b = mod.shape[0]
    return _rowcall(name, fn, t, tm, seq // tm, tiled=[dxo, r, ab, f], exs=[mod], res=[lng, w2t],
                    out_tiled=[_sds((t, D)), _sds((t, D), MXU_DTYPE), _sds((t, FF), MXU_DTYPE),
                               _sds((t, 2 * FF), MXU_DTYPE)],
                    out_acc=[_sds((1, D)), _sds((1, D))], out_exacc=[_sds((b, 1, D))], **carry)


def _ffn_bwd(name, dxo, r, ab, f, x, mod, k0, lng, w2t, w1t, w3t, seq):
    t = dxo.shape[0]
    tm = 256

    def fn(ctx, tiled, halos, exs, res, outs, accs, exaccs, scr):
        dxo_ref, r_ref, ab_ref, f_ref, x_ref = tiled
        mod_ref, = exs
        g_ref, w2t_ref, w1t_ref, w3t_ref = res
        dx_ref, df_ref, s_ref, dab_ref = outs
        sc, g = mod_ref[0, k0 + 1:k0 + 2, :], mod_ref[0, k0 + 2:k0 + 3, :]
        dr, dgam, dbet = _ln_bwd(r_ref[...], g_ref[...], dxo_ref[...])
        _acc(accs[0], dgam)
        _acc(accs[1], dbet)
        _acc(exaccs[0].at[0], jnp.sum(0.5 * f_ref[...] * dr, axis=0, keepdims=True))
        df = _mx(0.5 * g * dr)
        df_ref[...] = df
        dh = jnp.zeros((tm, D), F32)
        for j in range(2):
            blk = slice(j * FB, (j + 1) * FB)
            ds = _dot(df, w2t_ref[:, blk])
            a = ab_ref[:, blk]
            b = ab_ref[:, FF + j * FB:FF + (j + 1) * FB]
            sig = _sigmoid(a)
            silu = a * sig
            s_ref[:, blk] = _mx(silu * b)
            da = _mx(ds * b * (sig * (1.0 + a * (1.0 - sig))))
            db = _mx(ds * silu)
            dab_ref[:, blk] = da
            dab_ref[:, FF + j * FB:FF + (j + 1) * FB] = db
            dh = dh + _dot(da, w1t_ref[blk, :]) + _dot(db, w3t_ref[blk, :])
        dx_ref[...] = ALPHA * dr + dh * (1.0 + sc)
        _acc(exaccs[1].at[0], jnp.sum(dh, axis=0, keepdims=True))
        _acc(exaccs[2].at[0], jnp.sum(dh * x_ref[...], axis=0, keepdims=True))

    b = mod.shape[0]
    return _rowcall(name, fn, t, tm, seq // tm, tiled=[dxo, r, ab, f, x], exs=[mod], res=[lng, w2t, w1t, w3t],
                    out_tiled=[_sds((t, D)), _sds((t, D), MXU_DTYPE), _sds((t, FF), MXU_DTYPE),
                               _sds((t, 2 * FF), MXU_DTYPE)],
                    out_acc=[_sds((1, D)), _sds((1, D))], out_exacc=[_sds((b, 1, D))] * 3)


def _mod_bwd(name, dab, dr, x, mod, k0, wts, seq, extra=(), **carry):
    t = dr.shape[0]
    tm = min(TM_WIDE, seq)
    nin = 1 + len(extra)
    width = dab.shape[1] + sum(e.shape[1] for e in extra)

    def fn(ctx, tiled, halos, exs, res, outs, accs, exaccs, scr):
        parts = tiled[:nin]
        dr_ref, x_ref = tiled[nin:]
        mod_ref, = exs
        sc = mod_ref[0, k0 + 1:k0 + 2, :]
        if nin == 1:
            dp = parts[0][...]
        else:
            dp = jnp.concatenate([_mx(p[...]) for p in parts], axis=1)
            outs[1][...] = dp
        dh, off = jnp.zeros((tm, D), F32), 0
        for wt_ref in res:
            dh = dh + _dot(dp[:, off:off + wt_ref.shape[0]], wt_ref[...])
            off += wt_ref.shape[0]
        outs[0][...] = ALPHA * dr_ref[...] + dh * (1.0 + sc)
        _acc(exaccs[0].at[0], jnp.sum(dh, axis=0, keepdims=True))
        _acc(exaccs[1].at[0], jnp.sum(dh * x_ref[...], axis=0, keepdims=True))

    b = mod.shape[0]
    out_tiled = [_sds((t, D))] + ([_sds((t, width), MXU_DTYPE)] if nin > 1 else [])
    return _rowcall(name, fn, t, tm, seq // tm, tiled=[dab, *extra, dr, x], exs=[mod], res=list(wts),
                    out_tiled=out_tiled, out_exacc=[_sds((b, 1, D)), _sds((b, 1, D))], **carry)


def _tn_matmul(name, a, b, bm, bn, bt=1024, carry=None, a_cols=None):
    t = a.shape[0]
    bt = min(bt, t)
    start, m = a_cols or (0, a.shape[1])
    off = start // bm
    n = b.shape[1]
    grid = (m // bm, n // bn, t // bt)
    n_in, n_out = (len(carry.ins), len(carry.out_shapes)) if carry else (0, 0)

    def body(a_ref, b_ref, *refs):
        o_ref = refs[n_in]

        def work():
            @pl.when(pl.program_id(2) == 0)
            def _():
                o_ref[...] = jnp.zeros(o_ref.shape, F32)
            o_ref[...] += lax.dot_general(a_ref[...], b_ref[...], (((0,), (0,)), ((), ())), preferred_element_type=F32)

        if not carry:
            return work()
        step = (pl.program_id(0) * grid[1] + pl.program_id(1)) * grid[2] + pl.program_id(2)
        carry.run((refs[:n_in], refs[n_in + 1:n_in + 1 + n_out], refs[n_in + 1 + n_out:]), step == 0, step == 0,
                  step == grid[0] * grid[1] * grid[2] - 1, work)

    any_spec = pl.BlockSpec(memory_space=pl.ANY)
    outs = _pcall(body, name=name, grid=grid,
                  in_specs=[pl.BlockSpec((bt, bm), lambda i, j, k: (k, i + off)), pl.BlockSpec((bt, bn), lambda i, j, k: (k, j))]
                  + [any_spec] * n_in,
                  out_specs=[pl.BlockSpec((bm, bn), lambda i, j, k: (i, j))] + [any_spec] * n_out,
                  out_shape=[_sds((m, n))] + (carry.out_shapes if carry else []),
                  input_output_aliases={2 + k: 1 + v for k, v in carry.aliases.items()} if carry else {},
                  scratch_shapes=carry.sem_shapes() if carry else [],
                  compiler_params=_cparams(("arbitrary",) * 3 if carry else ("parallel", "parallel", "arbitrary")))(
                      a, b, *(carry.ins if carry else []))
    return (outs[0], outs[1:]) if carry else outs[0]


S5_BLK = 16


def _tn_diag(name, a, b, bt=1024):
    t = a.shape[0]
    bt = min(bt, t)
    rows, cols = S5_BLK * S5P, S5_BLK * S5H
    nblk = a.shape[1] // rows

    def body(a_ref, b_ref, o_ref):
        @pl.when(pl.program_id(1) == 0)
        def _():
            o_ref[...] = jnp.zeros(o_ref.shape, F32)
        o_ref[0] += lax.dot_general(a_ref[...], b_ref[...], (((0,), (0,)), ((), ())), preferred_element_type=F32)

    return _pcall(body, name=name, grid=(nblk, t // bt),
                  in_specs=[pl.BlockSpec((bt, rows), lambda i, k: (k, i)),
                            pl.BlockSpec((bt, cols), lambda i, k: (k, i % (S5G // S5_BLK)))],
                  out_specs=pl.BlockSpec((1, rows, cols), lambda i, k: (i, 0, 0)), out_shape=_sds((nblk, rows, cols)),
                  compiler_params=_cparams(("parallel", "arbitrary")))(a, b)


def _diag_groups(o):
    o = o.reshape(2, S5G // S5_BLK, S5_BLK, S5P, S5_BLK, S5H)
    return jnp.einsum("rbgpgh->rbgph", o).reshape(2, S5G, S5P, S5H)


def _proj_fwd(name, x, mod, w_in, seq):
    t = x.shape[0]
    tm = min(TM_WIDE, seq)

    def fn(ctx, tiled, halos, exs, res, outs, accs, exaccs, scr):
        mod_ref, = exs
        sh, sc = mod_ref[0, 3:4, :], mod_ref[0, 4:5, :]
        h = _mx(tiled[0][...] * (1.0 + sc) + sh)
        outs[0][...] = h
        outs[1][...] = _dot(h, res[0][...])

    return _rowcall(name, fn, t, tm, seq // tm, tiled=[x], exs=[mod], res=[w_in],
                    out_tiled=[_sds((t, D), MXU_DTYPE), _sds((t, PW))])


def _shift_rows(cur, prev8, j):
    if j == 0:
        return cur
    rolled = pltpu.roll(cur, j, 0)
    top = jnp.where(_iota((8, cur.shape[1]), 0) < j, pltpu.roll(prev8, j, 0), rolled[0:8])
    return jnp.concatenate([top, rolled[8:]], axis=0)


def _shift_rows_up(cur, next8, j):
    if j == 0:
        return cur
    n = cur.shape[0]
    rolled = pltpu.roll(cur, n - j, 0)
    bot = jnp.where(_iota((8, cur.shape[1]), 0) >= 8 - j, pltpu.roll(next8, 8 - j, 0), rolled[n - 8:n])
    return jnp.concatenate([rolled[:n - 8], bot], axis=0)


def _softplus(x):
    return jnp.maximum(x, 0.0) + jnp.log(1.0 + jnp.exp(-jnp.abs(x)))


def _ssd_common(proj_ref, xpre, dtb_ref, alog_ref):
    xbc = xpre * _sigmoid(xpre)
    xs, bm, cm = xbc[:, 0:SW], xbc[:, SW:SW + 256], xbc[:, SW + 256:SW + 512]
    dtraw = proj_ref[:, PW - 128:PW] + dtb_ref[...]
    dt = _softplus(dtraw)
    a = -jnp.exp(alog_ref[...])
    tril = (_iota((CH, CH), 0) >= _iota((CH, CH), 1)).astype(F32)
    acs = _dot_hi(tril, dt * a)
    return xs, bm, cm, dtraw, dt, a, acs, acs.T


def _pair_lane(lo, hi):
    r = lo.shape[0]
    return jnp.where(_iota((r, 128), 1) < HP, lo, hi)


def _ssd_fwd(name, proj, conv_w, conv_b, dt_bias, a_log, d_rep, norm_w, seq):
    t = proj.shape[0]
    nb = t // seq

    def fn(ctx, tiled, halos, exs, res, outs, accs, exaccs, scr):
        @pl.when(ctx.seq_first)
        def _():
            scr[0][...] = jnp.zeros(scr[0].shape, F32)

        for b in range(nb):
            one(ctx, res, [r.at[b] for r in tiled + halos + outs + scr])

    def one(ctx, res, refs):
        proj_ref, halo_ref, yo_ref, xpre_ref, y_ref, sprev_ref, state_ref = refs
        cw_ref, cb_ref, dtb_ref, alog_ref, d_ref, nw_ref = res
        raw = proj_ref[:, SW:SW + D]
        prev8 = halo_ref[:, SW:SW + D] * jnp.where(ctx.seq_first, 0.0, 1.0)
        xpre = cb_ref[...] + cw_ref[3:4, :] * raw
        for j in (1, 2, 3):
            xpre = xpre + cw_ref[3 - j:4 - j, :] * _shift_rows(raw, prev8, j)
        xpre_ref[...] = xpre
        xs, bm, cm, dtraw, dt, a, acs, acst = _ssd_common(proj_ref, xpre, dtb_ref, alog_ref)
        causal = _iota((CH, CH), 0) >= _iota((CH, CH), 1)
        lane_lo = _iota((CH, 128), 1) < HP
        sprev = state_ref[...]
        sprev_ref[...] = sprev
        ys = []
        for g in range(NG):
            bmg, cmg = bm[:, g * NS:(g + 1) * NS], cm[:, g * NS:(g + 1) * NS]
            bmt = bmg.T
            cb = _dot(_mx(cmg), _mx(bmt))
            for q in (2 * g, 2 * g + 1):
                xsq = xs[:, 128 * q:128 * q + 128]
                xd = xsq * _pair_lane(dt[:, 2 * q:2 * q + 1], dt[:, 2 * q + 1:2 * q + 2])
                sp = sprev[:, 128 * q:128 * q + 128]
                ydiag = jnp.zeros((CH, 128), F32)
                snew = jnp.zeros((NS, 128), F32)
                for jj in range(2):
                    h = 2 * q + jj
                    col, row = acs[:, h:h + 1], acst[h:h + 1, :]
                    lm = jnp.where(causal, jnp.exp(jnp.minimum(col - row, 0.0)), 0.0)
                    xm = _mx(jnp.where(lane_lo if jj == 0 else ~lane_lo, xd, 0.0))
                    ydiag = ydiag + _dot(_mx(cb * lm), xm)
                    dec_row = jnp.exp(acst[h:h + 1, CH - 1:CH] - row)
                    snew = snew + _dot(_mx(bmt * dec_row), xm)
                e_pair = jnp.exp(_pair_lane(acs[:, 2 * q:2 * q + 1], acs[:, 2 * q + 1:2 * q + 2]))
                yoff = _dot(_mx(cmg), _mx(sp)) * e_pair
                cd = jnp.exp(_pair_lane(acst[2 * q:2 * q + 1, CH - 1:CH], acst[2 * q + 1:2 * q + 2, CH - 1:CH]))
                state_ref[:, 128 * q:128 * q + 128] = cd * sp + snew
                ys.append(ydiag + yoff + d_ref[:, 128 * q:128 * q + 128] * xsq)
        y = jnp.concatenate(ys, axis=1)
        y_ref[...] = y
        z = proj_ref[:, 0:SW]
        yz = y * (z * _sigmoid(z))
        outp = []
        for g in range(NG):
            seg = yz[:, 256 * g:256 * g + 256]
            rinv = lax.rsqrt(jnp.mean(seg * seg, axis=-1, keepdims=True) + LN_EPS)
            outp.append(seg * rinv * nw_ref[:, 256 * g:256 * g + 256])
        yo_ref[...] = _mx(jnp.concatenate(outp, axis=1))

    return _rowcall(name, fn, t, CH, seq // CH, tiled=[proj], halos=[(proj, 8)],
                    res=[conv_w, conv_b, dt_bias, a_log, d_rep, norm_w],
                    out_tiled=[_sds((t, SW), MXU_DTYPE), _sds((t, D)), _sds((t, SW)), _sds((t, SW))],
                    scratch=[pltpu.VMEM((nb, NS, SW), F32)], batch=nb)


def _ssd_bwd(name, dyo, proj, xpre_all, y_all, sprev_all, conv_w, dt_bias, a_log, d_rep, norm_w, seq, **carry):
    t = proj.shape[0]
    nb = t // seq

    def fn(ctx, tiled, halos, exs, res, outs, accs, exaccs, scr):
        @pl.when(ctx.seq_last)
        def _():
            for r in scr:
                r[...] = jnp.zeros(r.shape, F32)

        for b in range(nb):
            one(ctx, res, accs, [r.at[b] for r in tiled + halos + outs + scr])

    def one(ctx, res, accs, refs):
        dyo_ref, proj_ref, xpre_ref, y_ref, sprev_ref, halo_ref, dzx_ref, ddt_ref, ds_ref, nxt_ref = refs
        cw_ref, dtb_ref, alog_ref, d_ref, nw_ref = res
        dnw_acc, dd_acc, dcw_acc, dcb_acc, ddtb_acc, dalog_acc = accs
        xpre = xpre_ref[...]
        xs, bm, cm, dtraw, dt, a, acs, acst = _ssd_common(proj_ref, xpre, dtb_ref, alog_ref)
        y = y_ref[...]
        z = proj_ref[:, 0:SW]
        sz = _sigmoid(z)
        siluz = z * sz
        yz = y * siluz
        dyo = dyo_ref[...]
        dyz_parts, dnw_parts = [], []
        for g in range(NG):
            seg = yz[:, 256 * g:256 * g + 256]
            rinv = lax.rsqrt(jnp.mean(seg * seg, axis=-1, keepdims=True) + LN_EPS)
            yn = seg * rinv
            dseg = dyo[:, 256 * g:256 * g + 256]
            dnw_parts.append(jnp.sum(dseg * yn, axis=0, keepdims=True))
            dyn = dseg * nw_ref[:, 256 * g:256 * g + 256]
            dyz_parts.append(rinv * (dyn - yn * jnp.mean(dyn * yn, axis=-1, keepdims=True)))
        dyz = jnp.concatenate(dyz_parts, axis=1)
        _acc(dnw_acc, jnp.concatenate(dnw_parts, axis=1))
        dy = dyz * siluz
        dz = dyz * y * (sz * (1.0 + z * (1.0 - sz)))
        _acc(dd_acc, jnp.sum(dy * xs, axis=0, keepdims=True))

        causal = _iota((CH, CH), 0) >= _iota((CH, CH), 1)
        anti = _iota((CH, CH), 0) <= _iota((CH, CH), 1)
        lane_lo = _iota((CH, 128), 1) < HP
        lane_id = _iota((CH, 128), 1)
        last_row = _iota((CH, 128), 0) == CH - 1
        sprev = sprev_ref[...]
        dacs = jnp.zeros((CH, 128), F32)
        ddt_x = jnp.zeros((CH, 128), F32)
        dxs_parts, dbm_parts, dcm_parts = [], [], []
        for g in range(NG):
            bmg, cmg = bm[:, g * NS:(g + 1) * NS], cm[:, g * NS:(g + 1) * NS]
            bmt, cmt = bmg.T, cmg.T
            cb = _dot(_mx(cmg), _mx(bmt))
            cbt = _dot(_mx(bmg), _mx(cmt))
            dcb = jnp.zeros((CH, CH), F32)
            dcbt = jnp.zeros((CH, CH), F32)
            dbmg = jnp.zeros((CH, NS), F32)
            dcmg = jnp.zeros((CH, NS), F32)
            for q in (2 * g, 2 * g + 1):
                sl = slice(128 * q, 128 * q + 128)
                xsq = xs[:, sl]
                dtp = _pair_lane(dt[:, 2 * q:2 * q + 1], dt[:, 2 * q + 1:2 * q + 2])
                xd = xsq * dtp
                dyq = dy[:, sl]
                sp = sprev[:, sl]
                dsn = ds_ref[:, sl]
                e_pair = jnp.exp(_pair_lane(acs[:, 2 * q:2 * q + 1], acs[:, 2 * q + 1:2 * q + 2]))
                cd = jnp.exp(_pair_lane(acst[2 * q:2 * q + 1, CH - 1:CH], acst[2 * q + 1:2 * q + 2, CH - 1:CH]))
                dye = dyq * e_pair
                dcmg = dcmg + _dot(_mx(dye), _mx(sp.T))
                dsp = _dot(_mx(cmt), _mx(dye)) + cd * dsn
                yoff = _dot(_mx(cmg), _mx(sp)) * e_pair
                dacs_lane = dyq * yoff
                dxd = jnp.zeros((CH, 128), F32)
                sds = jnp.sum(dsn * sp, axis=0, keepdims=True) * cd
                for jj in range(2):
                    h = 2 * q + jj
                    hm = lane_lo if jj == 0 else ~lane_lo
                    col, row = acs[:, h:h + 1], acst[h:h + 1, :]
                    lm = jnp.where(causal, jnp.exp(jnp.minimum(col - row, 0.0)), 0.0)
                    lmt = jnp.where(anti, jnp.exp(jnp.minimum(row - col, 0.0)), 0.0)
                    xm = _mx(jnp.where(hm, xd, 0.0))
                    dym = _mx(jnp.where(hm, dyq, 0.0))
                    gm = _dot_nt(dym, xm)
                    gmt = _dot_nt(xm, dym)
                    dcb = dcb + gm * lm
                    dcbt = dcbt + gmt * lmt
                    dxd = dxd + _dot(_mx(cbt * lmt), dym)
                    w = gm * cb * lm
                    wt = gmt * cbt * lmt
                    dacs_h = jnp.sum(w, axis=1, keepdims=True) - jnp.sum(wt, axis=1, keepdims=True)
                    alast = acst[h:h + 1, CH - 1:CH]
                    dec_col = jnp.exp(alast - col)
                    dsm = _mx(jnp.where(hm[0:NS], dsn, 0.0))
                    qh = _dot_nt(xm, dsm)
                    dbmg = dbmg + qh * dec_col
                    ddec = jnp.sum(qh * bmg, axis=1, keepdims=True)
                    dxd = dxd + _dot(_mx(bmg * dec_col), dsm)
                    dacs_h = dacs_h - ddec * dec_col
                    dacs_h = dacs_h + jnp.sum(jnp.where(hm, dacs_lane, 0.0), axis=1, keepdims=True)
                    tail = jnp.sum(ddec * dec_col, axis=0, keepdims=True) + jnp.sum(
                        jnp.where(hm[0:1], sds, 0.0), axis=1, keepdims=True)
                    dacs = dacs + jnp.where(lane_id == h, dacs_h, 0.0) + jnp.where(
                        last_row & (lane_id == h), tail, 0.0)
                ds_ref[:, sl] = dsp
                for jj in range(2):
                    h = 2 * q + jj
                    hm = lane_lo if jj == 0 else ~lane_lo
                    ddt_x = ddt_x + jnp.where(lane_id == h, jnp.sum(jnp.where(hm, dxd * xsq, 0.0), axis=1, keepdims=True), 0.0)
                dxs_parts.append(dxd * dtp + d_ref[:, sl] * dyq)
            dcmg = dcmg + _dot(_mx(dcb), _mx(bmg))
            dbmg = dbmg + _dot(_mx(dcbt), _mx(cmg))
            dbm_parts.append(dbmg)
            dcm_parts.append(dcmg)
        triu = (_iota((CH, CH), 0) <= _iota((CH, CH), 1)).astype(F32)
        dadt = _dot_hi(triu, dacs)
        ddt = dadt * a + ddt_x
        _acc(dalog_acc, jnp.sum(dadt * dt, axis=0, keepdims=True) * a)
        ddtraw = ddt * _sigmoid(dtraw)
        ddt_ref[...] = ddtraw
        _acc(ddtb_acc, jnp.sum(ddtraw, axis=0, keepdims=True))
        dxbc = jnp.concatenate(dxs_parts + dbm_parts + dcm_parts, axis=1)
        sx = _sigmoid(xpre)
        dpre = dxbc * (sx * (1.0 + xpre * (1.0 - sx)))
        _acc(dcb_acc, jnp.sum(dpre, axis=0, keepdims=True))
        raw = proj_ref[:, SW:SW + D]
        prev8 = halo_ref[:, SW:SW + D] * jnp.where(ctx.seq_first, 0.0, 1.0)
        next8 = nxt_ref[...]
        draw = cw_ref[3:4, :] * dpre
        dcw = [None] * 4
        dcw[3] = jnp.sum(dpre * raw, axis=0, keepdims=True)
        for j in (1, 2, 3):
            dcw[3 - j] = jnp.sum(dpre * _shift_rows(raw, prev8, j), axis=0, keepdims=True)
            draw = draw + cw_ref[3 - j:4 - j, :] * _shift_rows_up(dpre, next8, j)
        _acc(dcw_acc, jnp.concatenate(dcw + [jnp.zeros((4, D), F32)], axis=0))
        nxt_ref[...] = dpre[0:8]
        dzx_ref[:, 0:SW] = dz
        dzx_ref[:, SW:SW + D] = draw

    return _rowcall(name, fn, t, CH, seq // CH, tiled=[dyo, proj, xpre_all, y_all, sprev_all], halos=[(proj, 8)],
                    res=[conv_w, dt_bias, a_log, d_rep, norm_w],
                    out_tiled=[_sds((t, SW + D)), _sds((t, 128))],
                    out_acc=[_sds((1, SW)), _sds((1, SW)), _sds((8, D)), _sds((1, D)), _sds((1, 128)), _sds((1, 128))],
                    scratch=[pltpu.VMEM((nb, NS, SW), F32), pltpu.VMEM((nb, 8, D), F32)], reverse=True, batch=nb, **carry)


def _gelu(y):
    k = math.sqrt(2.0 / math.pi)
    return 0.5 * y * (1.0 + jnp.tanh(k * (y + 0.044715 * y * y * y)))


def _gelu_grad(y):
    k = math.sqrt(2.0 / math.pi)
    th = jnp.tanh(k * (y + 0.044715 * y * y * y))
    return 0.5 * (1.0 + th) + 0.5 * y * (1.0 - th * th) * k * (1.0 + 3.0 * 0.044715 * y * y)


S5T = 256


def _cmul_add(xr, xi, ar, ai, sr, si):
    return xr + ar * sr - ai * si, xi + ar * si + ai * sr


def _s5_fwd(name, proj, bbd, cbd, pw, tab, d5, w_glu, b_glu, seq):
    t = proj.shape[0]
    tm = S5T

    def fn(ctx, tiled, halos, exs, res, outs, accs, exaccs, scr):
        proj_ref, = tiled
        bbd_ref, cbd_ref, pw_ref, tab_ref, d_ref, wg_ref, bg_ref = res
        out_ref, xst_ref, y_ref, xb_ref, ub_ref = outs
        carry_ref, = scr

        @pl.when(ctx.seq_first)
        def _():
            carry_ref[...] = jnp.zeros(carry_ref.shape, F32)

        u = proj_ref[:, 1536:2048]
        bu = _dot(_mx(u), bbd_ref[...])
        xr, xi = bu[:, :S5L].reshape(tm // 8, 8, S5L), bu[:, S5L:].reshape(tm // 8, 8, S5L)
        for k, sh in enumerate((1, 2, 4)):
            xr, xi = _cmul_add(xr, xi, pw_ref[k, :, :S5L], pw_ref[k, :, S5L:], pltpu.roll(xr, sh, 1), pltpu.roll(xi, sh, 1))
        xst_ref[:, :S5L] = xr.reshape(tm, S5L)
        xst_ref[:, S5L:] = xi.reshape(tm, S5L)

        def tile_fix(i, c):
            cr, ci = c
            rows = pl.ds(pl.multiple_of(i * 8, 8), 8)
            tr, ti = _cmul_add(xst_ref[rows, :S5L], xst_ref[rows, S5L:], tab_ref[:, :S5L], tab_ref[:, S5L:], cr, ci)
            xst_ref[rows, :S5L] = tr
            xst_ref[rows, S5L:] = ti
            return tr[7:8], ti[7:8]

        cr, ci = lax.fori_loop(0, tm // 8, tile_fix, (carry_ref[0:1, :S5L], carry_ref[0:1, S5L:]))
        carry_ref[0:1, :S5L] = cr
        carry_ref[0:1, S5L:] = ci
        xb = _mx(xst_ref[...])
        xb_ref[...] = xb
        ub_ref[...] = _mx(u)
        y = _dot(xb, cbd_ref[...]) + u * d_ref[...]
        y_ref[...] = y
        g = _gelu(y)
        v = _dot(_mx(g), wg_ref[...]) + bg_ref[...]
        out_ref[...] = _mx(g * _sigmoid(v))

    return _rowcall(name, fn, t, tm, seq // tm, tiled=[proj], res=[bbd, cbd, pw, tab, d5, w_glu, b_glu],
                    out_tiled=[_sds((t, SW), MXU_DTYPE), _sds((t, 2 * S5L)), _sds((t, SW)),
                               _sds((t, 2 * S5L), MXU_DTYPE), _sds((t, SW), MXU_DTYPE)],
                    scratch=[pltpu.VMEM((8, 2 * S5L), F32)])


def _s5_bwd(name, dout, proj, xst, y_all, bbdt, cbdt, pwc, tabc, d5, w_glu, w_glut, b_glu, seq, **carry):
    t = proj.shape[0]
    tm = S5T

    def fn(ctx, tiled, halos, exs, res, outs, accs, exaccs, scr):
        dout_ref, proj_ref, xst_ref, y_ref = tiled
        halo_ref, = halos
        bbdt_ref, cbdt_ref, pw_ref, tab_ref, d_ref, wg_ref, wgt_ref, bg_ref = res
        du_ref, lam_ref, dyb_ref, gb_ref, dvb_ref = outs
        da_acc, dd_acc, dbg_acc = accs
        carry_ref, lamf_ref = scr

        @pl.when(ctx.seq_last)
        def _():
            carry_ref[...] = jnp.zeros(carry_ref.shape, F32)

        u = proj_ref[:, 1536:2048]
        y = y_ref[...]
        g = _gelu(y)
        v = _dot(_mx(g), wg_ref[...]) + bg_ref[...]
        sg = _sigmoid(v)
        dout = dout_ref[...]
        dv = dout * g * sg * (1.0 - sg)
        dvb = _mx(dv)
        dvb_ref[...] = dvb
        gb_ref[...] = _mx(g)
        _acc(dbg_acc, jnp.sum(dv, axis=0, keepdims=True))
        dg = dout * sg + _dot(dvb, wgt_ref[...])
        dy = dg * _gelu_grad(y)
        dyb = _mx(dy)
        dyb_ref[...] = dyb
        _acc(dd_acc, jnp.sum(dy * u, axis=0, keepdims=True))
        dx = _dot(dyb, cbdt_ref[...])
        xr, xi = dx[:, :S5L].reshape(tm // 8, 8, S5L), dx[:, S5L:].reshape(tm // 8, 8, S5L)
        for k, sh in enumerate((1, 2, 4)):
            xr, xi = _cmul_add(xr, xi, pw_ref[k, :, :S5L], pw_ref[k, :, S5L:], pltpu.roll(xr, 8 - sh, 1),
                               pltpu.roll(xi, 8 - sh, 1))
        lamf_ref[:, :S5L] = xr.reshape(tm, S5L)
        lamf_ref[:, S5L:] = xi.reshape(tm, S5L)

        def tile_fix(i, c):
            cr, ci = c
            rows = pl.ds(pl.multiple_of((tm // 8 - 1 - i) * 8, 8), 8)
            tr, ti = _cmul_add(lamf_ref[rows, :S5L], lamf_ref[rows, S5L:], tab_ref[:, :S5L], tab_ref[:, S5L:], cr, ci)
            lamf_ref[rows, :S5L] = tr
            lamf_ref[rows, S5L:] = ti
            return tr[0:1], ti[0:1]

        cr, ci = lax.fori_loop(0, tm // 8, tile_fix, (carry_ref[0:1, :S5L], carry_ref[0:1, S5L:]))
        carry_ref[0:1, :S5L] = cr
        carry_ref[0:1, S5L:] = ci
        lam = lamf_ref[...]
        lamb = _mx(lam)
        lam_ref[...] = lamb
        du_ref[...] = dy * d_ref[...] + _dot(lamb, bbdt_ref[...])
        prev8 = halo_ref[...] * jnp.where(ctx.seq_first, 0.0, 1.0)
        xprev = _shift_rows(xst_ref[...], prev8, 1)
        lr, li = lam[:, :S5L], lam[:, S5L:]
        pr, pi = xprev[:, :S5L], xprev[:, S5L:]
        dar = jnp.sum(lr * pr + li * pi, axis=0, keepdims=True)
        dai = jnp.sum(li * pr - lr * pi, axis=0, keepdims=True)
        _acc(da_acc, jnp.concatenate([dar, dai], axis=1))

    return _rowcall(name, fn, t, tm, seq // tm, tiled=[dout, proj, xst, y_all], halos=[(xst, 8)],
                    res=[bbdt, cbdt, pwc, tabc, d5, w_glu, w_glut, b_glu],
                    out_tiled=[_sds((t, SW)), _sds((t, 2 * S5L), MXU_DTYPE), _sds((t, SW), MXU_DTYPE),
                               _sds((t, SW), MXU_DTYPE), _sds((t, SW), MXU_DTYPE)],
                    out_acc=[_sds((1, 2 * S5L)), _sds((1, SW)), _sds((1, SW))],
                    scratch=[pltpu.VMEM((8, 2 * S5L), F32), pltpu.VMEM((tm, 2 * S5L), F32)], reverse=True, **carry)


def _out_fwd(name, yssd, ys5, x1, mod, w_out, lng, lnb, seq):
    t = x1.shape[0]
    tm = min(TM_WIDE, seq)

    def fn(ctx, tiled, halos, exs, res, outs, accs, exaccs, scr):
        ya_ref, yb_ref, x_ref = tiled
        mod_ref, = exs
        w_ref, g_ref, b_ref = res
        m = _dot(ya_ref[...], w_ref[0:SW, :]) + _dot(yb_ref[...], w_ref[SW:2 * SW, :])
        r = ALPHA * x_ref[...] + mod_ref[0, 5:6, :] * m
        outs[0][...] = _ln_fwd(r, g_ref[...], b_ref[...])
        outs[1][...] = r
        outs[2][...] = m

    return _rowcall(name, fn, t, tm, seq // tm, tiled=[yssd, ys5, x1], exs=[mod], res=[w_out, lng, lnb],
                    out_tiled=[_sds((t, D)), _sds((t, D)), _sds((t, D))])


def _out_bwd(name, dxo, r, m, mod, lng, w_outt, seq, **carry):
    t = dxo.shape[0]
    tm = min(TM_WIDE, seq)

    def fn(ctx, tiled, halos, exs, res, outs, accs, exaccs, scr):
        dxo_ref, r_ref, m_ref = tiled
        mod_ref, = exs
        g_ref, wt_ref = res
        dr, dgam, dbet = _ln_bwd(r_ref[...], g_ref[...], dxo_ref[...])
        outs[0][...] = dr
        _acc(accs[0], dgam)
        _acc(accs[1], dbet)
        _acc(exaccs[0].at[0], jnp.sum(dr * m_ref[...], axis=0, keepdims=True))
        dm = _mx(mod_ref[0, 5:6, :] * dr)
        outs[1][...] = dm
        dyc = _dot(dm, wt_ref[...])
        outs[2][...] = dyc[:, 0:SW]
        outs[3][...] = dyc[:, SW:2 * SW]

    b = mod.shape[0]
    return _rowcall(name, fn, t, tm, seq // tm, tiled=[dxo, r, m], exs=[mod], res=[lng, w_outt],
                    out_tiled=[_sds((t, D)), _sds((t, D), MXU_DTYPE), _sds((t, SW)), _sds((t, SW))],
                    out_acc=[_sds((1, D)), _sds((1, D))], out_exacc=[_sds((b, 1, D))], **carry)


def _s5_discretise(a_re, a_im, log_dt, b_re, b_im):
    dt = jnp.exp(log_dt)[:, None]
    mag = jnp.exp(dt * a_re)
    ab_re, ab_im = mag * jnp.cos(dt * a_im), mag * jnp.sin(dt * a_im)
    den = a_re * a_re + a_im * a_im
    nr, ni = ab_re - 1.0, ab_im
    f_re, f_im = (nr * a_re + ni * a_im) / den, (ni * a_re - nr * a_im) / den
    bb_re = f_re[..., None] * b_re - f_im[..., None] * b_im
    bb_im = f_re[..., None] * b_im + f_im[..., None] * b_re
    return ab_re, ab_im, bb_re, bb_im


def _s5_tables(ab_re, ab_im):
    ar, ai = ab_re.reshape(1, S5L), ab_im.reshape(1, S5L)
    pows = [(ar, ai)]
    for _ in range(7):
        pr, pi = pows[-1]
        pows.append((pr * ar - pi * ai, pr * ai + pi * ar))

    def pack(rows, sign):
        return jnp.concatenate([jnp.concatenate([r for r, _ in rows], axis=0),
                                jnp.concatenate([sign * i for _, i in rows], axis=0)], axis=1)

    row = jnp.arange(8)[:, None]
    pw = jnp.stack([jnp.where(row >= sh, pack([pows[sh - 1]], 1.0), 0.0) for sh in (1, 2, 4)])
    pwc = jnp.stack([jnp.where(row < 8 - sh, pack([pows[sh - 1]], -1.0), 0.0) for sh in (1, 2, 4)])
    tab = pack(pows, 1.0)
    tabc = pack(pows[::-1], -1.0)
    return pw, tab, pwc, tabc


class _GradGroup:
    def __init__(self, tag, grads, place):
        self.tag, self.names, (self.half, self.chip) = tag, list(grads), place
        self.gsh = [_shard_major(g) if n in COL_SHARDED else g.reshape(4, g.shape[0] // 4, g.shape[1])
                    for n, g in grads.items()]

    def sibling(self):
        return _rs_sibling_stage(self.gsh)

    def chips(self, received):
        self.sums, sums_bf = _rs_add(self.tag + "_add", self.gsh, received, self.half)
        return _rs_chips_stage(sums_bf)

    def join(self, received):
        return _rs_join_stage(_rs_sum(self.tag + "_sum", self.sums, received, self.chip, self.half))

    def result(self, joined):
        return dict(zip(self.names, joined))


def _hid(fn, *args, stage=None, **kw):
    if stage is None:
        return fn(*args, **kw), None
    return fn(*args, carry=stage, **kw)


def _local_step(x, tgt, mod, w, sp, seq, dist=None):
    t = x.shape[0]
    mxu = MXU_DTYPE
    big = {}

    def group(tag, grads):
        if dist is None:
            big.update(grads)
            return None
        return _GradGroup(tag, grads, dist[2])

    (x1, r1, h1, ab1, f1), late = _hid(_ffn_fwd, "ffn1_fwd", x, mod, 0, w["ffn1_w1"], w["ffn1_w3"], w["ffn1_w2"], sp["ln1_g"],
                                       sp["ln1_b"], seq, stage=dist and dist[0], **({"carry_mid": 0.75} if dist else {}))
    if dist:
        w = {**w, **dist[1](late)}
    h2, proj = _proj_fwd("proj_fwd", x1, mod, w["w_in"], seq)
    yssd, xpre, yraw, sprev = _ssd_fwd("ssd_fwd", proj, sp["conv_w"], sp["conv_b"], sp["dt_bias"], sp["a_log"],
                                       sp["d_rep"], sp["ssd_norm_w"], seq)
    (ab_re, ab_im, bb_re, bb_im), disc_vjp = jax.vjp(_s5_discretise, sp["s5_a_re"], sp["s5_a_im"], sp["s5_log_dt"],
                                                     sp["s5_b_re"], sp["s5_b_im"])
    eye = jnp.eye(S5G, dtype=F32)
    bbd = jnp.concatenate([jnp.einsum("gk,gph->ghkp", eye, bb_re).reshape(SW, S5L),
                           jnp.einsum("gk,gph->ghkp", eye, bb_im).reshape(SW, S5L)], axis=1).astype(mxu)
    cbd = jnp.concatenate([jnp.einsum("gk,ghp->gpkh", eye, sp["s5_c_re"]).reshape(S5L, SW),
                           -jnp.einsum("gk,ghp->gpkh", eye, sp["s5_c_im"]).reshape(S5L, SW)], axis=0).astype(mxu)
    bbdt = jnp.concatenate([jnp.einsum("gk,gph->kpgh", eye, bb_re).reshape(S5L, SW),
                            jnp.einsum("gk,gph->kpgh", eye, bb_im).reshape(S5L, SW)], axis=0).astype(mxu)
    cbdt = jnp.concatenate([jnp.einsum("gk,ghp->khgp", eye, sp["s5_c_re"]).reshape(SW, S5L),
                            -jnp.einsum("gk,ghp->khgp", eye, sp["s5_c_im"]).reshape(SW, S5L)], axis=1).astype(mxu)
    pw, tab, pwc, tabc = _s5_tables(lax.stop_gradient(ab_re), lax.stop_gradient(ab_im))
    ys5, xst, y5, xstb, ub = _s5_fwd("s5_fwd", proj, bbd, cbd, pw, tab, sp["s5_d"], w["w_glu"], sp["b_glu"], seq)
    x2, r2, m2 = _out_fwd("out_fwd", yssd, ys5, x1, mod, w["w_out"], sp["ln2_g"], sp["ln2_b"], seq)
    x3, r3, h3, ab3, f3, dy3, loss_acc = _ffn_fwd("ffn2_fwd", x2, mod, 6, w["ffn2_w1"], w["ffn2_w3"], w["ffn2_w2"], sp["ln3_g"],
                                                  sp["ln3_b"], seq, tgt=tgt)
    dx2, df3, s3, dab3, dg3g, dg3b, dgate3, dsh3, dsc3 = _ffn_bwd("ffn2_bwd", dy3, r3, ab3, f3, x2, mod, 6, sp["ln3_g"],
                                                                  w["ffn2_w2t"], w["ffn2_w1t"], w["ffn2_w3t"], seq)
    grp = group("rs_ffn2", dict(ffn2_w1=_tn_matmul("ffn2_dw1", dab3, h3, FB, D, a_cols=(0, FF)),
                                ffn2_w3=_tn_matmul("ffn2_dw3", dab3, h3, FB, D, a_cols=(FF, FF)),
                                ffn2_w2=_tn_matmul("ffn2_dw2", s3, df3, FB, D)))
    (dr2, dm2, dyssd, dys5, dg2g, dg2b, dgate2), got = _hid(_out_bwd, "out_bwd", dx2, r2, m2, mod, sp["ln2_g"],
                                                          w["w_outt"], seq, stage=grp and grp.sibling())
    g_w_out = jnp.concatenate([_tn_matmul("dw_out_a", yssd, dm2, SW, D), _tn_matmul("dw_out_b", ys5, dm2, SW, D)], axis=0)
    (du, lam, dy5b, g5b, dv5b, da5, dd5, dbglu), got = _hid(
        _s5_bwd, "s5_bwd", dys5, proj, xst, y5, bbdt, cbdt, pwc, tabc, sp["s5_d"], w["w_glu"], w["w_glut"], sp["b_glu"],
        seq, stage=grp and grp.chips(got))
    g_w_glu = _tn_matmul("dw_glu", g5b, dv5b, SW, SW)
    dbb = _diag_groups(_tn_diag("s5_db", lam, ub))
    dcc = _diag_groups(_tn_diag("s5_dc", xstb, dy5b))
    (dzx, ddt, dnw, ddl, dcw, dcb, ddtb, dalog), got = _hid(
        _ssd_bwd, "ssd_bwd", dyssd, proj, xpre, yraw, sprev, sp["conv_w"], sp["dt_bias"], sp["a_log"], sp["d_rep"],
        sp["ssd_norm_w"], seq, stage=grp and grp.join(got))
    if grp:
        big.update(grp.result(got))
    dx1, dproj, dsh2, dsc2 = _mod_bwd("proj_bwd", dzx, dr2, x1, mod, 3, [w["w_int"]], seq, extra=(du, ddt))
    gwi = _tn_matmul("dw_in", h2, dproj, D, PW)
    grp = group("rs_mix", dict(w_in=jnp.concatenate([gwi[:, :1536], gwi[:, 2048:2056], gwi[:, 1536:2048]], axis=1),
                               w_glu=g_w_glu, w_out=g_w_out))
    (dr1, df1, s1, dab1, dg1g, dg1b, dgate1), got = _hid(_ffn_bwd1, "ffn1_bwd1", dx1, r1, ab1, f1, mod, 0, sp["ln1_g"],
                                                          w["ffn1_w2t"], seq, stage=grp and grp.sibling())
    g1, got = _hid(_tn_matmul, "ffn1_dw1", dab1, h1, FB, D, a_cols=(0, FF), stage=grp and grp.chips(got))
    g3, got = _hid(_tn_matmul, "ffn1_dw3", dab1, h1, FB, D, a_cols=(FF, FF), stage=grp and grp.join(got))
    if grp:
        big.update(grp.result(got))
    grp = group("rs_ffn1a", dict(ffn1_w1=g1, ffn1_w3=g3))
    g2, got = _hid(_tn_matmul, "ffn1_dw2", s1, df1, FB, D, stage=grp and grp.sibling())
    grp2 = group("rs_ffn1b", dict(ffn1_w2=g2))
    (dx0, dsh1, dsc1), got = _hid(_mod_bwd, "ffn1_bwd2", dab1, dr1, x, mod, 0, [w["ffn1_w1t"], w["ffn1_w3t"]], seq,
                                  stage=grp and _merge_stages(grp.chips(got), grp2.sibling()))
    if grp:
        n = len(grp.names)
        got = _run_stage("rs_ffn1_tail", _merge_stages(grp.join(got[:n]), grp2.chips(got[n:])))
        big.update(grp.result(got[:n]))
        big.update(grp2.result(_run_stage("rs_ffn1b_join", grp2.join(got[n:]))))
    dmod = jnp.concatenate([dsh1, dsc1, dgate1, dsh2, dsc2, dgate2, dsh3, dsc3, dgate3], axis=1)
    dc_re, dc_im = dcc[0].transpose(0, 2, 1), -dcc[1].transpose(0, 2, 1)
    g_a_re, g_a_im, g_log_dt, g_b_re, g_b_im = disc_vjp(
        (da5[:, :S5L].reshape(S5G, S5P), da5[:, S5L:].reshape(S5G, S5P), dbb[0], dbb[1]))
    small = dict(ln1_g=dg1g, ln1_b=dg1b, ln2_g=dg2g, ln2_b=dg2b, ln3_g=dg3g, ln3_b=dg3b, conv_w=dcw[0:4], conv_b=dcb,
                 dt_bias=ddtb[:, :NH], a_log=dalog[:, :NH], d_ssd=jnp.sum(ddl.reshape(NH, HP), axis=1).reshape(1, NH),
                 ssd_norm_w=dnw, s5_a_re=g_a_re, s5_a_im=g_a_im, s5_log_dt=g_log_dt, s5_b_re=g_b_re, s5_b_im=g_b_im,
                 s5_c_re=dc_re, s5_c_im=dc_im, s5_d=dd5, w_glu_b=dbglu)
    return loss_acc[0, 0], dx0, dmod, big, small


def _place():
    return lax.axis_index("x"), lax.axis_index("y"), lax.axis_index("c")


def _other_chips(x, y):
    return [(1 - x, y), (x, 1 - y), (1 - x, 1 - y)]


def _allgather8(name, a):
    r, n = a.shape

    def body(x_ref, out_ref, send_sems, recv_sems, local_sem):
        x, y, c = _place()
        me, sibling = (x, y, c), (x, y, 1 - c)
        chips = _other_chips(x, y)

        def rows(px, py, pc):
            return out_ref.at[pl.ds(pl.multiple_of((4 * px + 2 * py + pc) * r, 8), r), :]

        def copy(k, block, to, src=None):
            return pltpu.make_async_remote_copy(src_ref=rows(*block) if src is None else src, dst_ref=rows(*block),
                                                send_sem=send_sems.at[k], recv_sem=recv_sems.at[k], device_id=to,
                                                device_id_type=MESH_T)

        mine = pltpu.make_async_copy(x_ref, rows(*me), local_sem)
        mine.start()
        first = [copy(0, me, sibling, src=x_ref)]
        first += [copy(1 + j, me, (*chip, c), src=x_ref) for j, chip in enumerate(chips)]
        for cp in first:
            cp.start()
        passed = [copy(4 + j, (*chip, c), sibling) for j, chip in enumerate(chips)]
        for j, chip in enumerate(chips):
            copy(1 + j, (*chip, c), me).wait_recv()
            passed[j].start()
        copy(0, sibling, me).wait_recv()
        for j, chip in enumerate(chips):
            copy(4 + j, (*chip, 1 - c), me).wait_recv()
        for cp in first + passed:
            cp.wait_send()
        mine.wait()

    out = _pcall(body, name=name, out_shape=_sds((8 * r, n), a.dtype),
                 in_specs=[pl.BlockSpec(memory_space=pltpu.VMEM)], out_specs=pl.BlockSpec(memory_space=pltpu.VMEM),
                 scratch_shapes=[pltpu.SemaphoreType.DMA((7,)), pltpu.SemaphoreType.DMA((7,)), pltpu.SemaphoreType.DMA],
                 compiler_params=_cparams())(a)
    return out.reshape(8, r, n)


class _Stage:
    def __init__(self, ins, out_shapes, n_sems, start, finish, mid=None, aliases=None):
        self.ins, self.out_shapes, self.n_sems = list(ins), list(out_shapes), tuple(n_sems)
        self.start, self.mid, self.finish = start, mid, finish
        self.aliases = dict(aliases or {})

    def sem_shapes(self):
        return [pltpu.SemaphoreType.DMA((n,)) for n in self.n_sems]

    def run(self, refs, at_start=None, at_mid=None, at_finish=None, between=None):
        if between is None:
            for part in (self.start, self.mid, self.finish):
                if part is not None:
                    part(*refs)
            return
        pl.when(at_start)(functools.partial(self.start, *refs))
        if self.mid is not None:
            pl.when(at_mid)(functools.partial(self.mid, *refs))
        between()
        pl.when(at_finish)(functools.partial(self.finish, *refs))


def _merge_stages(a, b):
    n_in, n_out, n_sem = len(a.ins), len(a.out_shapes), len(a.n_sems)

    def both(fa, fb):
        def run(ins, outs, sems):
            fa(ins[:n_in], outs[:n_out], sems[:n_sem])
            fb(ins[n_in:], outs[n_out:], sems[n_sem:])
        return run

    aliases = {**a.aliases, **{n_in + k: n_out + v for k, v in b.aliases.items()}}
    return _Stage(a.ins + b.ins, a.out_shapes + b.out_shapes, a.n_sems + b.n_sems, both(a.start, b.start),
                  both(a.finish, b.finish), aliases=aliases)


def _run_stage(name, st):
    n_in, n_out = len(st.ins), len(st.out_shapes)

    def body(*refs):
        st.run((refs[:n_in], refs[n_in:n_in + n_out], refs[n_in + n_out:]))

    any_spec = pl.BlockSpec(memory_space=pl.ANY)
    return _pcall(body, name=name, out_shape=st.out_shapes, in_specs=[any_spec] * n_in, out_specs=[any_spec] * n_out,
                  input_output_aliases=st.aliases, scratch_shapes=st.sem_shapes(), compiler_params=_cparams())(*st.ins)


def _rows(ref_rows, half, align):
    hr = ref_rows // 2
    return pl.ds(pl.multiple_of(half * hr, align), hr)


def _gather_stage(shards):
    n = len(shards)
    pairs = [(i, j) for i in range(n) for j in range(3)]

    def env(ins, outs, sems):
        x, y, c = _place()
        chips = _other_chips(x, y)

        def copy(i, k, chip, half, to, src=None):
            dst = outs[i].at[2 * chip[0] + chip[1], _rows(shards[i].shape[0], half, 16)]
            return pltpu.make_async_remote_copy(src_ref=dst if src is None else src, dst_ref=dst,
                                                send_sem=sems[0].at[6 * i + k], recv_sem=sems[1].at[6 * i + k],
                                                device_id=to, device_id_type=MESH_T)

        def first(i, j):
            return copy(i, j, (x, y), c, (*chips[j], c), src=ins[i].at[_rows(shards[i].shape[0], c, 16)])

        def passed(i, j, half):
            return copy(i, 3 + j, chips[j], half, (x, y, 1 - c))

        def landed(i, j):
            return copy(i, j, chips[j], c, (x, y, 1 - c))

        return c, first, passed, landed

    def start(ins, outs, sems):
        c, first, passed, landed = env(ins, outs, sems)
        for i, j in pairs:
            first(i, j).start()

    def mid(ins, outs, sems):
        c, first, passed, landed = env(ins, outs, sems)
        for i, j in pairs:
            landed(i, j).wait_recv()
            passed(i, j, c).start()

    def finish(ins, outs, sems):
        c, first, passed, landed = env(ins, outs, sems)
        for i, j in pairs:
            passed(i, j, 1 - c).wait_recv()
        for i, j in pairs:
            first(i, j).wait_send()
            passed(i, j, c).wait_send()

    return _Stage(shards, [_sds((4,) + s.shape, s.dtype) for s in shards], (6 * n, 6 * n), start, finish, mid)


def _rs_sibling_stage(gs):
    n = len(gs)

    def copies(ins, outs, sems):
        x, y, c = _place()
        return [pltpu.make_async_remote_copy(src_ref=ins[i].at[:, _rows(gs[i].shape[1], 1 - c, 8)], dst_ref=outs[i],
                                             send_sem=sems[0].at[i], recv_sem=sems[1].at[i], device_id=(x, y, 1 - c),
                                             device_id_type=MESH_T) for i in range(n)]

    def start(*refs):
        for cp in copies(*refs):
            cp.start()

    def finish(*refs):
        for cp in copies(*refs):
            cp.wait()

    return _Stage(gs, [_sds((4, g.shape[1] // 2, g.shape[2]), g.dtype) for g in gs], (n, n), start, finish)


def _rs_chips_stage(hs):
    n = len(hs)

    def copies(ins, outs, sems):
        x, y, c = _place()
        return [pltpu.make_async_remote_copy(src_ref=ins[i].at[2 * chip[0] + chip[1]], dst_ref=outs[i].at[j],
                                             send_sem=sems[0].at[3 * i + j], recv_sem=sems[1].at[3 * i + j],
                                             device_id=(*chip, c), device_id_type=MESH_T)
                for i in range(n) for j, chip in enumerate(_other_chips(x, y))]

    def start(*refs):
        for cp in copies(*refs):
            cp.start()

    def finish(*refs):
        for cp in copies(*refs):
            cp.wait()

    return _Stage(hs, [_sds((3,) + h.shape[1:], h.dtype) for h in hs], (3 * n, 3 * n), start, finish)


def _rs_join_stage(fs):
    n = len(fs)

    def copy(outs, sems, i, half):
        x, y, c = _place()
        part = outs[i].at[_rows(fs[i].shape[0], c if half == "mine" else 1 - c, 8)]
        return pltpu.make_async_remote_copy(src_ref=part, dst_ref=part, send_sem=sems[0].at[i], recv_sem=sems[1].at[i],
                                            device_id=(x, y, 1 - c), device_id_type=MESH_T)

    def start(ins, outs, sems):
        for i in range(n):
            copy(outs, sems, i, "mine").start()

    def finish(ins, outs, sems):
        for i in range(n):
            copy(outs, sems, i, "theirs").wait_recv()
        for i in range(n):
            copy(outs, sems, i, "mine").wait_send()

    return _Stage(fs, [_sds(f.shape, f.dtype) for f in fs], (n, n), start, finish, aliases={i: i for i in range(n)})


def _row_block(r, cap=2048):
    b = min(r, cap)
    while r % b or b % 8:
        b -= 8
    return b


RS_SPLIT = 2


def _rs_add(name, gs, r1s, sel):
    n = len(gs)

    def body(sel_ref, *refs):
        g_refs, r_refs, h_refs, b_refs = (refs[k * n:(k + 1) * n] for k in range(4))
        for i in range(n):
            h = g_refs[i][...] + r_refs[i][...]
            h_refs[i][...] = h
            b_refs[i][...] = h.astype(BF16)

    def blk(g):
        return (1, g.shape[1] // 2 // RS_SPLIT, g.shape[2])

    here = lambda k, j, s: (k, j, 0)
    in_specs = [pl.BlockSpec(blk(g), lambda k, j, s: (k, s[0] * RS_SPLIT + j, 0)) for g in gs]
    in_specs += [pl.BlockSpec(blk(g), here) for g in gs]
    outs = _pcall(body, name=name,
                  out_shape=[_sds((4, g.shape[1] // 2, g.shape[2])) for g in gs]
                  + [_sds((4, g.shape[1] // 2, g.shape[2]), BF16) for g in gs],
                  grid_spec=pltpu.PrefetchScalarGridSpec(num_scalar_prefetch=1, grid=(4, RS_SPLIT), in_specs=in_specs,
                                                         out_specs=[pl.BlockSpec(blk(g), here) for g in gs] * 2),
                  compiler_params=_cparams(("parallel", "parallel")))(sel.reshape(1).astype(jnp.int32), *gs, *r1s)
    return outs[:n], outs[n:]


def _rs_sum(name, hs, r2s, chip, half):
    n = len(hs)

    def body(sel_ref, *refs):
        h_refs, r_refs, o_refs = (refs[k * n:(k + 1) * n] for k in range(3))
        for i in range(n):
            r = r_refs[i]
            o_refs[i][...] = ((h_refs[i][0] + r[0].astype(F32)) + r[1].astype(F32)) + r[2].astype(F32)

    def rows(h):
        return h.shape[1] // RS_SPLIT

    in_specs = [pl.BlockSpec((1, rows(h), h.shape[2]), lambda j, s: (s[0], j, 0)) for h in hs]
    in_specs += [pl.BlockSpec((3, rows(h), h.shape[2]), lambda j, s: (0, j, 0)) for h in hs]
    sel = jnp.stack([chip, half]).astype(jnp.int32)
    return _pcall(body, name=name, out_shape=[_sds((2 * h.shape[1], h.shape[2])) for h in hs],
                  grid_spec=pltpu.PrefetchScalarGridSpec(
                      num_scalar_prefetch=1, grid=(RS_SPLIT,), in_specs=in_specs,
                      out_specs=[pl.BlockSpec((rows(h), h.shape[2]), lambda j, s: (s[1] * RS_SPLIT + j, 0)) for h in hs]),
                  compiler_params=_cparams(("parallel",)))(sel, *hs, *r2s)


def _sum8(name, a):
    _, r, n = a.shape
    br = _row_block(r)

    def body(a_ref, o_ref):
        acc = a_ref[0]
        for k in range(1, 8):
            acc = acc + a_ref[k]
        o_ref[...] = acc

    return _pcall(body, name=name, out_shape=_sds((r, n)), grid=(r // br,),
                  in_specs=[pl.BlockSpec((8, br, n), lambda j: (0, j, 0))], out_specs=pl.BlockSpec((br, n), lambda j: (j, 0)),
                  compiler_params=_cparams(("parallel",)))(a)


def _adamw(name, ws, gs, ms, vs, nblk):
    n = len(ws)

    def body(*refs):
        w_refs, g_refs, m_refs, v_refs, d_refs, nm_refs, nv_refs = (refs[k * n:(k + 1) * n] for k in range(7))
        for i in range(n):
            gv = g_refs[i][...]
            nm = ADAM_B1 * m_refs[i][...] + (1.0 - ADAM_B1) * gv
            nv = ADAM_B2 * v_refs[i][...] + (1.0 - ADAM_B2) * (gv * gv)
            nm_refs[i][...] = nm
            nv_refs[i][...] = nv
            m_hat = nm / (1.0 - ADAM_B1 ** ADAM_STEP)
            v_hat = nv / (1.0 - ADAM_B2 ** ADAM_STEP)
            d_refs[i][...] = -ADAM_LR * (m_hat / (jnp.sqrt(v_hat) + ADAM_EPS) + ADAM_WD * w_refs[i][...])

    specs = [pl.BlockSpec((w.shape[0] // nblk, w.shape[1]), lambda j: (j, 0)) for w in ws]
    outs = _pcall(body, name=name, out_shape=[_sds(w.shape) for w in ws] * 3, grid=(nblk,), in_specs=specs * 4,
                  out_specs=specs * 3, compiler_params=_cparams(("parallel",)))(*ws, *gs, *ms, *vs)
    return outs[:n], outs[n:2 * n], outs[2 * n:]


ADA_COLS = 2304
ADA_BLK = 768


def _ada_fwd(name, c_all, w_shard, b_cols):
    nb = c_all.shape[0]

    def body(c_ref, w_ref, b_ref, o_ref):
        cv = c_ref[...]
        cs = _mx(cv * _sigmoid(cv))
        o_ref[...] = _dot(cs, _mx(w_ref[...])) + b_ref[...]

    return _pcall(body, name=name, out_shape=_sds((nb, ADA_COLS)), grid=(ADA_COLS // ADA_BLK,),
                  in_specs=[pl.BlockSpec((nb, D), lambda j: (0, 0)), pl.BlockSpec((D, ADA_BLK), lambda j: (0, j)),
                            pl.BlockSpec((1, ADA_BLK), lambda j: (0, j))],
                  out_specs=pl.BlockSpec((nb, ADA_BLK), lambda j: (0, j)),
                  compiler_params=_cparams(("parallel",)))(c_all, w_shard, b_cols)


def _ada_bwd(name, c_all, dmod_cols, dmod_all):
    nb = c_all.shape[0]

    def body(c_ref, dc_ref, da_ref, gw_ref, gb_ref):
        cv = c_ref[...]
        cs = _mx(cv * _sigmoid(cv))
        gw_ref[...] = lax.dot_general(cs, _mx(dc_ref[...]), (((0,), (0,)), ((), ())), preferred_element_type=F32)

        @pl.when(pl.program_id(0) == 0)
        def _():
            gb_ref[...] = jnp.sum(da_ref[...], axis=0, keepdims=True)

    return _pcall(body, name=name, out_shape=[_sds((D, ADA_COLS)), _sds((1, 9 * D))], grid=(ADA_COLS // ADA_BLK,),
                  in_specs=[pl.BlockSpec((nb, D), lambda j: (0, 0)), pl.BlockSpec((nb, ADA_BLK), lambda j: (0, j)),
                            pl.BlockSpec((nb, 9 * D), lambda j: (0, 0))],
                  out_specs=[pl.BlockSpec((D, ADA_BLK), lambda j: (0, j)), pl.BlockSpec((1, 9 * D), lambda j: (0, 0))],
                  compiler_params=_cparams(("arbitrary",)))(c_all, dmod_cols, dmod_all)


BIG = ("ffn1_w1", "ffn1_w3", "ffn1_w2", "w_in", "w_glu", "w_out", "ffn2_w1", "ffn2_w3", "ffn2_w2")
COL_SHARDED = ("w_in",)
TRANSPOSED = ("ffn1_w1", "ffn1_w3", "ffn2_w1", "ffn2_w3")
SMALL = ("b_ada", "ln1_g", "ln1_b", "conv_w", "conv_b", "dt_bias", "a_log", "d_ssd", "ssd_norm_w", "s5_a_re", "s5_a_im",
         "s5_log_dt", "s5_b_re", "s5_b_im", "s5_c_re", "s5_c_im", "s5_d", "b_glu", "ln2_g", "ln2_b", "ln3_g", "ln3_b")
WEIGHTS = ("w_ada", "b_ada", "ffn1_w1", "ffn1_w3", "ffn1_w2", "ln1_g", "ln1_b", "w_in", "conv_w", "conv_b", "dt_bias",
           "a_log", "d_ssd", "ssd_norm_w", "s5_a_re", "s5_a_im", "s5_log_dt", "s5_b_re", "s5_b_im", "s5_c_re", "s5_c_im",
           "s5_d", "w_glu", "b_glu", "w_out", "ln2_g", "ln2_b", "ffn2_w1", "ffn2_w3", "ffn2_w2", "ln3_g", "ln3_b")
BIG_PAD = 2 * 1024 * 128


def _pack(arrs, mult, axis_keep=0):
    lead = arrs[0].shape[:axis_keep]
    flat = jnp.concatenate([a.reshape(lead + (-1,)) for a in arrs], axis=axis_keep)
    pad = (-flat.shape[-1]) % mult
    if pad:
        flat = jnp.concatenate([flat, jnp.zeros(lead + (pad,), flat.dtype)], axis=axis_keep)
    return flat


def _unpack(flat, shapes):
    out, off = [], 0
    for s in shapes:
        size = math.prod(s)
        out.append(flat[..., off:off + size].reshape(flat.shape[:-1] + tuple(s)))
        off += size
    return out


def _shard_major(a):
    rows, cols = a.shape
    return a.reshape(rows, 4, cols // 4).transpose(1, 0, 2)


def _from_shard_major(a):
    _, rows, w = a.shape
    return a.transpose(1, 0, 2).reshape(rows, 4 * w)


def kernel(x, c, w_ada, b_ada, ffn1_w1, ffn1_w3, ffn1_w2, ln1_g, ln1_b, w_in, conv_w, conv_b, dt_bias, a_log, d_ssd, ssd_norm_w, s5_a_re, s5_a_im, s5_log_dt, s5_b_re, s5_b_im, s5_c_re, s5_c_im, s5_d, w_glu, b_glu, w_out, ln2_g, ln2_b, ffn2_w1, ffn2_w3, ffn2_w2, ln3_g, ln3_b, loss_target, m_w_ada, m_b_ada, m_ffn1_w1, m_ffn1_w3, m_ffn1_w2, m_ln1_g, m_ln1_b, m_w_in, m_conv_w, m_conv_b, m_dt_bias, m_a_log, m_d_ssd, m_ssd_norm_w, m_s5_a_re, m_s5_a_im, m_s5_log_dt, m_s5_b_re, m_s5_b_im, m_s5_c_re, m_s5_c_im, m_s5_d, m_w_glu, m_b_glu, m_w_out, m_ln2_g, m_ln2_b, m_ffn2_w1, m_ffn2_w3, m_ffn2_w2, m_ln3_g, m_ln3_b, v_w_ada, v_b_ada, v_ffn1_w1, v_ffn1_w3, v_ffn1_w2, v_ln1_g, v_ln1_b, v_w_in, v_conv_w, v_conv_b, v_dt_bias, v_a_log, v_d_ssd, v_ssd_norm_w, v_s5_a_re, v_s5_a_im, v_s5_log_dt, v_s5_b_re, v_s5_b_im, v_s5_c_re, v_s5_c_im, v_s5_d, v_w_glu, v_b_glu, v_w_out, v_ln2_g, v_ln2_b, v_ffn2_w1, v_ffn2_w3, v_ffn2_w2, v_ln3_g, v_ln3_b):
    a = dict(locals())
    xi, yi, ci = _place()
    chip = 2 * xi + yi
    dev = 2 * chip + ci
    nb, seq, _ = x.shape
    t = nb * seq
    ndev = 8

    c_rows = nb * D // 128
    c_cw = _allgather8("gather_c", jnp.concatenate([c.reshape(c_rows, 128), conv_w.reshape(-1, 128)], axis=0))
    c_all = c_cw[:, :c_rows].reshape(ndev * nb, D)
    b_cols = lax.dynamic_slice(b_ada, (0, chip * ADA_COLS), (1, ADA_COLS))
    mod_part = _ada_fwd("ada_fwd", c_all, w_ada[0], b_cols)
    mod_parts = _allgather8("gather_mod", mod_part.reshape(-1, 128)).reshape(ndev, ndev * nb, ADA_COLS)
    mod_all = mod_parts[0::2].transpose(1, 0, 2).reshape(ndev * nb, 9 * D)
    mod = lax.dynamic_slice(mod_all, (nb * dev, 0), (nb, 9 * D)).reshape(nb, 9, D)

    def nat(n):
        return jnp.swapaxes(a[n], 1, 2)[0] if n[-7:] in TRANSPOSED else a[n][0]

    def gather(names):
        own = [nat(n).astype(MXU_DTYPE) for n in names]

        def weights(pieces):
            w = {}
            for n, mine, piece in zip(names, own, pieces):
                piece = lax.dynamic_update_slice(piece, mine[None], (chip, 0, 0))
                if n in COL_SHARDED:
                    wi = _from_shard_major(piece)
                    w[n] = jnp.concatenate([wi[:, :1536], wi[:, 1544:2056], wi[:, 1536:1544],
                                            jnp.zeros((D, 120), wi.dtype)], axis=1)
                else:
                    w[n + "t" if n in TRANSPOSED else n] = piece.reshape(-1, piece.shape[-1])
            for n in names:
                have, want = (n + "t", n) if n in TRANSPOSED else (n, n + "t")
                w[want] = w[have].T
            return w

        return _gather_stage(own), weights

    first_stage, first_weights = gather(BIG[:3])
    w = first_weights(_run_stage("gather_w_ffn1", first_stage))
    late_stage, late_weights = gather(BIG[3:])

    conv_full = _from_shard_major(c_cw[0::2, c_rows:].reshape(4, 4, 256))
    pad8 = lambda v: jnp.concatenate([v.reshape(1, NH), jnp.zeros((1, 128 - NH), F32)], axis=1)
    sp = dict(ln1_g=ln1_g, ln1_b=ln1_b, ln2_g=ln2_g, ln2_b=ln2_b, ln3_g=ln3_g, ln3_b=ln3_b, conv_w=conv_full,
              conv_b=conv_b, dt_bias=pad8(dt_bias), a_log=pad8(a_log), d_rep=jnp.repeat(d_ssd[0], HP)[None],
              ssd_norm_w=ssd_norm_w, s5_a_re=s5_a_re[0], s5_a_im=s5_a_im[0], s5_log_dt=s5_log_dt[0], s5_b_re=s5_b_re[0],
              s5_b_im=s5_b_im[0], s5_c_re=s5_c_re[0], s5_c_im=s5_c_im[0], s5_d=s5_d, b_glu=b_glu)

    lsum, dx0, dmod, gbig, small = _local_step(x.reshape(t, D), loss_target.reshape(t, D), mod, w, sp, seq,
                                               dist=(late_stage, late_weights, (ci, chip)))
    loss = lax.psum(lsum * (0.5 / D), ("x", "y", "c"))

    dmod_all = _allgather8("gather_dmod", dmod.reshape(-1, 128)).reshape(ndev * nb, 9 * D)
    dmod_cols = lax.dynamic_slice(dmod_all, (0, chip * ADA_COLS), (ndev * nb, ADA_COLS))
    g_w_ada, g_b_ada = _ada_bwd("ada_bwd", c_all, dmod_cols, dmod_all)

    gbig["w_ada"] = g_w_ada

    outs = {}
    for call, names in (("adamw_a", ("ffn1_w1", "ffn1_w3", "ffn1_w2", "w_in", "w_glu", "w_out")),
                        ("adamw_b", ("ffn2_w1", "ffn2_w3", "ffn2_w2", "w_ada"))):
        res = _adamw(call, [nat(n) for n in names], [gbig[n] for n in names], [nat("m_" + n) for n in names],
                     [nat("v_" + n) for n in names], 8)
        for kind, arrs in zip(("grad", "delta", "new_m", "new_v"), ([gbig[n] for n in names],) + tuple(res)):
            for n, arr in zip(names, arrs):
                outs[kind, n] = (arr.T if n in TRANSPOSED else arr)[None]

    snames = [n for n in SMALL if n != "b_ada"]
    sgrad = dict(small)
    sgrad["b_glu"] = small["w_glu_b"]
    svec = _pack([sgrad[n] for n in snames], 1024).reshape(-1, 128)
    ssum = _sum8("small_sum", _allgather8("gather_small", svec)).reshape(-1)

    def view2d(u):
        s = u.shape[1:]
        return u.reshape((1, s[0]) if len(s) == 1 else (-1, s[-1]))

    vshape = {n: view2d(a[n]).shape for n in SMALL}
    gsm = dict(zip(snames, _unpack(ssum, [vshape[n] if n != "conv_w" else (4, D) for n in snames])))
    gsm["conv_w"] = lax.dynamic_slice(gsm["conv_w"], (0, chip * 256), (4, 256))
    gsm["b_ada"] = g_b_ada
    res = _adamw("adamw_small", [view2d(a[n]) for n in SMALL], [gsm[n] for n in SMALL],
                 [view2d(a["m_" + n]) for n in SMALL], [view2d(a["v_" + n]) for n in SMALL], 1)
    for kind, arrs in zip(("grad", "delta", "new_m", "new_v"), ([gsm[n] for n in SMALL],) + tuple(res)):
        for n, arr in zip(SMALL, arrs):
            outs[kind, n] = arr.reshape(a[n].shape)

    res = [loss, dx0.reshape(nb, seq, D)]
    for kind in ("grad", "delta", "new_m", "new_v"):
        res += [outs[kind, n] for n in WEIGHTS]
    return tuple(res)
```

```python
import functools
import math

import jax
import jax.numpy as jnp
from jax import lax
from jax.experimental import pallas as pl
from jax.experimental.pallas import tpu as pltpu

F32 = jnp.float32
BF16 = jnp.bfloat16
MXU_DTYPE = jnp.bfloat16

D = 1024
FF = 2816
FB = 1408
NH, HP, NS, NG = 8, 64, 128, 2
CH = 128
SW = 512
S5G, S5P, S5H = 32, 64, 16
S5L = S5G * S5P
PW = 2176
ALPHA = 2.0 ** 0.25
LN_EPS = 1e-5
ADAM_LR, ADAM_B1, ADAM_B2, ADAM_EPS, ADAM_WD, ADAM_STEP = 0.001, 0.9, 0.999, 1e-08, 0.01, 10
VMEM_LIMIT = 56 * 1024 * 1024
TM_WIDE = 512
MESH_T = pl.DeviceIdType.MESH


def _pcall(body, **kw):
    return pl.pallas_call(body, **kw)


def _cparams(sem=None, **kw):
    return pltpu.CompilerParams(dimension_semantics=sem, vmem_limit_bytes=VMEM_LIMIT, **kw)


def _dot(a, b):
    return jnp.dot(a, b, preferred_element_type=F32)


def _dot_nt(a, b):
    return lax.dot_general(a, b, (((1,), (1,)), ((), ())), preferred_element_type=F32)


def _dot_hi(a, b):
    return jnp.dot(a, b, preferred_element_type=F32, precision=lax.Precision.HIGHEST)


def _mx(a):
    return a.astype(MXU_DTYPE)


def _sigmoid(x):
    return 1.0 / (1.0 + jnp.exp(-x))


def _iota(shape, axis):
    return lax.broadcasted_iota(jnp.int32, shape, axis)


def _rowcall(name, fn, n_rows, tm, tpe, *, tiled=(), halos=(), exs=(), res=(), out_tiled=(), out_acc=(),
             out_exacc=(), scratch=(), reverse=False, batch=None, carry=None, carry_mid=0.0):
    if batch:
        n_rows //= batch
    nt = n_rows // tm

    def blk(i):
        return (nt - 1 - i) if reverse else i

    in_specs, args = [], []
    for a in tiled:
        if batch:
            in_specs.append(pl.BlockSpec((batch, tm, a.shape[1]), lambda i: (0, blk(i), 0)))
            args.append(a.reshape(batch, n_rows, a.shape[1]))
            continue
        in_specs.append(pl.BlockSpec((tm, a.shape[1]), lambda i: (blk(i), 0)))
        args.append(a)
    for a, rows in halos:
        r = tm // rows
        if batch:
            in_specs.append(pl.BlockSpec((batch, rows, a.shape[1]), lambda i, r=r: (0, jnp.maximum(blk(i) * r - 1, 0), 0)))
            args.append(a.reshape(batch, n_rows, a.shape[1]))
            continue
        in_specs.append(pl.BlockSpec((rows, a.shape[1]), lambda i, r=r: (jnp.maximum(blk(i) * r - 1, 0), 0)))
        args.append(a)
    for a in exs:
        in_specs.append(pl.BlockSpec((1,) + a.shape[1:], lambda i: (blk(i) // tpe, 0, 0)))
        args.append(a)
    for a in res:
        nd = a.ndim
        in_specs.append(pl.BlockSpec(a.shape, lambda i, nd=nd: (0,) * nd, pipeline_mode=pl.Buffered(1)))
        args.append(a)
    any_spec = pl.BlockSpec(memory_space=pl.ANY)
    st_ins = carry.ins if carry else []
    st_outs = carry.out_shapes if carry else []
    st_sems = carry.sem_shapes() if carry else []
    base_in = len(args)
    in_specs += [any_spec] * len(st_ins)
    args += st_ins
    out_specs, out_shape = [], []
    for s in out_tiled:
        if batch:
            out_specs.append(pl.BlockSpec((batch, tm, s.shape[1]), lambda i: (0, blk(i), 0)))
            out_shape.append(_sds((batch, n_rows, s.shape[1]), s.dtype))
            continue
        out_specs.append(pl.BlockSpec((tm, s.shape[1]), lambda i: (blk(i), 0)))
        out_shape.append(s)
    for s in out_acc:
        nd = len(s.shape)
        out_specs.append(pl.BlockSpec(s.shape, lambda i, nd=nd: (0,) * nd))
        out_shape.append(s)
    for s in out_exacc:
        out_specs.append(pl.BlockSpec((1,) + s.shape[1:], lambda i: (blk(i) // tpe, 0, 0)))
        out_shape.append(s)
    base_out = len(out_shape)
    out_specs += [any_spec] * len(st_outs)
    out_shape += st_outs
    aliases = {base_in + k: base_out + v for k, v in carry.aliases.items()} if carry else {}
    n = [len(tiled), len(halos), len(exs), len(res), len(st_ins), len(out_tiled), len(out_acc), len(out_exacc),
         len(st_outs), len(scratch), len(st_sems)]

    def body(*refs):
        groups, k = [], 0
        for m in n:
            groups.append(refs[k:k + m])
            k += m
        i = pl.program_id(0)
        b = blk(i)

        class ctx:
            first = i == 0
            pos = b % tpe
            seq_first = (b % tpe) == 0
            seq_last = (b % tpe) == tpe - 1
            ex_enter = (i % tpe) == 0

        def work():
            for cond, refs_ in ((ctx.first, groups[6]), (ctx.ex_enter, groups[7])):
                if refs_:
                    @pl.when(cond)
                    def _():
                        for r in refs_:
                            r[...] = jnp.zeros(r.shape, r.dtype)
            fn(ctx, *groups[0:4], *groups[5:8], groups[9])

        if carry:
            carry.run((groups[4], groups[8], groups[10]), i == 0, i == min(nt - 1, int(carry_mid * nt)), i == nt - 1, work)
        else:
            work()

    outs = _pcall(body, name=name, grid=(nt,), in_specs=in_specs, out_specs=out_specs, out_shape=out_shape,
                  input_output_aliases=aliases, scratch_shapes=list(scratch) + st_sems,
                  compiler_params=_cparams(("arbitrary",)))(*args)
    if batch:
        outs = [o.reshape(batch * n_rows, o.shape[2]) if k < len(out_tiled) else o for k, o in enumerate(outs)]
    return (outs[:base_out], outs[base_out:]) if carry else outs


def _acc(ref, val):
    ref[...] += val


def _sds(shape, dtype=F32):
    return jax.ShapeDtypeStruct(shape, dtype)


def _ln_fwd(r, g, b):
    mu = jnp.mean(r, axis=-1, keepdims=True)
    rc = r - mu
    var = jnp.mean(rc * rc, axis=-1, keepdims=True)
    return rc * lax.rsqrt(var + LN_EPS) * g + b


def _ln_bwd(r, g, dy):
    mu = jnp.mean(r, axis=-1, keepdims=True)
    rc = r - mu
    var = jnp.mean(rc * rc, axis=-1, keepdims=True)
    rstd = lax.rsqrt(var + LN_EPS)
    xhat = rc * rstd
    dxh = dy * g
    dr = rstd * (dxh - jnp.mean(dxh, axis=-1, keepdims=True) - xhat * jnp.mean(dxh * xhat, axis=-1, keepdims=True))
    return dr, jnp.sum(dy * xhat, axis=0, keepdims=True), jnp.sum(dy, axis=0, keepdims=True)


def _ffn_fwd_hidden(name, x, mod, k0, w1, w3, seq, **carry):
    t = x.shape[0]
    tm = min(TM_WIDE, seq)

    def fn(ctx, tiled, halos, exs, res, outs, accs, exaccs, scr):
        mod_ref, = exs
        w1_ref, w3_ref = res
        h_ref, ab_ref, s_ref = outs
        sh, sc = mod_ref[0, k0:k0 + 1, :], mod_ref[0, k0 + 1:k0 + 2, :]
        h = _mx(tiled[0][...] * (1.0 + sc) + sh)
        h_ref[...] = h
        for j in range(2):
            a = _dot(h, w1_ref[:, j * FB:(j + 1) * FB])
            b = _dot(h, w3_ref[:, j * FB:(j + 1) * FB])
            ab_ref[:, j * FB:(j + 1) * FB] = _mx(a)
            ab_ref[:, FF + j * FB:FF + (j + 1) * FB] = _mx(b)
            s_ref[:, j * FB:(j + 1) * FB] = _mx(a * _sigmoid(a) * b)

    return _rowcall(name, fn, t, tm, seq // tm, tiled=[x], exs=[mod], res=[w1, w3],
                    out_tiled=[_sds((t, D), MXU_DTYPE), _sds((t, 2 * FF), MXU_DTYPE), _sds((t, FF), MXU_DTYPE)], **carry)


def _ffn_fwd_out(name, s, x, mod, k0, w2, lng, lnb, seq, tgt=None, **carry):
    t = x.shape[0]
    tm = min(TM_WIDE, seq)
    with_loss = tgt is not None

    def fn(ctx, tiled, halos, exs, res, outs, accs, exaccs, scr):
        mod_ref, = exs
        w2_ref, g_ref, b_ref = res
        xo_ref, r_ref, f_ref = outs[:3]
        f = _dot(tiled[0][...], w2_ref[...])
        f_ref[...] = f
        r = ALPHA * tiled[1][...] + 0.5 * mod_ref[0, k0 + 2:k0 + 3, :] * f
        r_ref[...] = r
        xo = _ln_fwd(r, g_ref[...], b_ref[...])
        xo_ref[...] = xo
        if with_loss:
            e = xo - tiled[2][...]
            outs[3][...] = e * (1.0 / D)
            _acc(accs[0], jnp.sum(e * e) * jnp.ones((8, 128), F32))

    return _rowcall(name, fn, t, tm, seq // tm, tiled=[s, x] + ([tgt] if with_loss else []), exs=[mod],
                    res=[w2, lng, lnb], out_tiled=[_sds((t, D))] * (4 if with_loss else 3),
                    out_acc=[_sds((8, 128))] if with_loss else [], **carry)


def _ffn_bwd1(name, dxo, r, ab, f, mod, k0, lng, w2t, seq, **carry):
    t = dxo.shape[0]
    tm = 256

    def fn(ctx, tiled, halos, exs, res, outs, accs, exaccs, scr):
        dxo_ref, r_ref, ab_ref, f_ref = tiled
        mod_ref, = exs
        g_ref, w2t_ref = res
        dr_ref, df_ref, dab_ref = outs
        g = mod_ref[0, k0 + 2:k0 + 3, :]
        dr, dgam, dbet = _ln_bwd(r_ref[...], g_ref[...], dxo_ref[...])
        dr_ref[...] = dr
        _acc(accs[0], dgam)
        _acc(accs[1], dbet)
        _acc(exaccs[0].at[0], jnp.sum(0.5 * f_ref[...] * dr, axis=0, keepdims=True))
        df = _mx(0.5 * g * dr)
        df_ref[...] = df
        for j in range(2):
            ds = _dot(df, w2t_ref[:, j * FB:(j + 1) * FB])
            a = ab_ref[:, j * FB:(j + 1) * FB].astype(F32)
            b = ab_ref[:, FF + j * FB:FF + (j + 1) * FB].astype(F32)
            sig = _sigmoid(a)
            dab_ref[:, j * FB:(j + 1) * FB] = _mx(ds * b * (sig * (1.0 + a * (1.0 - sig))))
            dab_ref[:, FF + j * FB:FF + (j + 1) * FB] = _mx(ds * (a * sig))

    b = mod.shape[0]
    return _rowcall(name, fn, t, tm, seq // tm, tiled=[dxo, r, ab, f], exs=[mod], res=[lng, w2t],
                    out_tiled=[_sds((t, D)), _sds((t, D), MXU_DTYPE), _sds((t, 2 * FF), MXU_DTYPE)],
                    out_acc=[_sds((1, D)), _sds((1, D))], out_exacc=[_sds((b, 1, D))], **carry)


def _ffn_bwd(name, dxo, r, ab, f, x, mod, k0, lng, w2t, w1t, w3t, seq):
    t = dxo.shape[0]
    tm = 256

    def fn(ctx, tiled, halos, exs, res, outs, accs, exaccs, scr):
        dxo_ref, r_ref, ab_ref, f_ref, x_ref = tiled
        mod_ref, = exs
        g_ref, w2t_ref, w1t_ref, w3t_ref = res
        dx_ref, df_ref, dab_ref = outs
        sc, g = mod_ref[0, k0 + 1:k0 + 2, :], mod_ref[0, k0 + 2:k0 + 3, :]
        dr, dgam, dbet = _ln_bwd(r_ref[...], g_ref[...], dxo_ref[...])
        _acc(accs[0], dgam)
        _acc(accs[1], dbet)
        _acc(exaccs[0].at[0], jnp.sum(0.5 * f_ref[...] * dr, axis=0, keepdims=True))
        df = _mx(0.5 * g * dr)
        df_ref[...] = df
        dh = jnp.zeros((tm, D), F32)
        for j in range(2):
            blk = slice(j * FB, (j + 1) * FB)
            ds = _dot(df, w2t_ref[:, blk])
            a = ab_ref[:, blk].astype(F32)
            b = ab_ref[:, FF + j * FB:FF + (j + 1) * FB].astype(F32)
            sig = _sigmoid(a)
            da = _mx(ds * b * (sig * (1.0 + a * (1.0 - sig))))
            db = _mx(ds * (a * sig))
            dab_ref[:, blk] = da
            dab_ref[:, FF + j * FB:FF + (j + 1) * FB] = db
            dh = dh + _dot(da, w1t_ref[blk, :]) + _dot(db, w3t_ref[blk, :])
        dx_ref[...] = ALPHA * dr + dh * (1.0 + sc)
        _acc(exaccs[1].at[0], jnp.sum(dh, axis=0, keepdims=True))
        _acc(exaccs[2].at[0], jnp.sum(dh * x_ref[...], axis=0, keepdims=True))

    b = mod.shape[0]
    return _rowcall(name, fn, t, tm, seq // tm, tiled=[dxo, r, ab, f, x], exs=[mod], res=[lng, w2t, w1t, w3t],
                    out_tiled=[_sds((t, D)), _sds((t, D), MXU_DTYPE), _sds((t, 2 * FF), MXU_DTYPE)],
                    out_acc=[_sds((1, D)), _sds((1, D))], out_exacc=[_sds((b, 1, D))] * 3)


def _mod_bwd(name, dab, dr, x, mod, k0, wts, seq, extra=(), **carry):
    t = dr.shape[0]
    tm = min(TM_WIDE, seq)
    nin = 1 + len(extra)
    width = dab.shape[1] + sum(e.shape[1] for e in extra)

    def fn(ctx, tiled, halos, exs, res, outs, accs, exaccs, scr):
        parts = tiled[:nin]
        dr_ref, x_ref = tiled[nin:]
        mod_ref, = exs
        sc = mod_ref[0, k0 + 1:k0 + 2, :]
        if nin == 1:
            dp = parts[0][...]
        else:
            dp = jnp.concatenate([_mx(p[...]) for p in parts], axis=1)
            outs[1][...] = dp
        dh, off = jnp.zeros((tm, D), F32), 0
        for wt_ref in res:
            dh = dh + _dot(dp[:, off:off + wt_ref.shape[0]], wt_ref[...])
            off += wt_ref.shape[0]
        outs[0][...] = ALPHA * dr_ref[...] + dh * (1.0 + sc)
        _acc(exaccs[0].at[0], jnp.sum(dh, axis=0, keepdims=True))
        _acc(exaccs[1].at[0], jnp.sum(dh * x_ref[...], axis=0, keepdims=True))

    b = mod.shape[0]
    out_tiled = [_sds((t, D))] + ([_sds((t, width), MXU_DTYPE)] if nin > 1 else [])
    return _rowcall(name, fn, t, tm, seq // tm, tiled=[dab, *extra, dr, x], exs=[mod], res=list(wts),
                    out_tiled=out_tiled, out_exacc=[_sds((b, 1, D)), _sds((b, 1, D))], **carry)


def _tn_matmul(name, a, b, bm, bn, bt=1024, carry=None, a_cols=None):
    t = a.shape[0]
    bt = min(bt, t)
    start, m = a_cols or (0, a.shape[1])
    off = start // bm
    n = b.shape[1]
    grid = (m // bm, n // bn, t // bt)
    n_in, n_out = (len(carry.ins), len(carry.out_shapes)) if carry else (0, 0)

    def body(a_ref, b_ref, *refs):
        o_ref = refs[n_in]

        def work():
            @pl.when(pl.program_id(2) == 0)
            def _():
                o_ref[...] = jnp.zeros(o_ref.shape, F32)
            o_ref[...] += lax.dot_general(a_ref[...], b_ref[...], (((0,), (0,)), ((), ())), preferred_element_type=F32)

        if not carry:
            return work()
        step = (pl.program_id(0) * grid[1] + pl.program_id(1)) * grid[2] + pl.program_id(2)
        carry.run((refs[:n_in], refs[n_in + 1:n_in + 1 + n_out], refs[n_in + 1 + n_out:]), step == 0, step == 0,
                  step == grid[0] * grid[1] * grid[2] - 1, work)

    any_spec = pl.BlockSpec(memory_space=pl.ANY)
    outs = _pcall(body, name=name, grid=grid,
                  in_specs=[pl.BlockSpec((bt, bm), lambda i, j, k: (k, i + off)), pl.BlockSpec((bt, bn), lambda i, j, k: (k, j))]
                  + [any_spec] * n_in,
                  out_specs=[pl.BlockSpec((bm, bn), lambda i, j, k: (i, j))] + [any_spec] * n_out,
                  out_shape=[_sds((m, n))] + (carry.out_shapes if carry else []),
                  input_output_aliases={2 + k: 1 + v for k, v in carry.aliases.items()} if carry else {},
                  scratch_shapes=carry.sem_shapes() if carry else [],
                  compiler_params=_cparams(("arbitrary",) * 3 if carry else ("parallel", "parallel", "arbitrary")))(
                      a, b, *(carry.ins if carry else []))
    return (outs[0], outs[1:]) if carry else outs[0]


S5_BLK = 16


def _tn_diag(name, a, b, bt=1024):
    t = a.shape[0]
    bt = min(bt, t)
    rows, cols = S5_BLK * S5P, S5_BLK * S5H
    nblk = a.shape[1] // rows

    def body(a_ref, b_ref, o_ref):
        @pl.when(pl.program_id(1) == 0)
        def _():
            o_ref[...] = jnp.zeros(o_ref.shape, F32)
        o_ref[0] += lax.dot_general(a_ref[...], b_ref[...], (((0,), (0,)), ((), ())), preferred_element_type=F32)

    return _pcall(body, name=name, grid=(nblk, t // bt),
                  in_specs=[pl.BlockSpec((bt, rows), lambda i, k: (k, i)),
                            pl.BlockSpec((bt, cols), lambda i, k: (k, i % (S5G // S5_BLK)))],
                  out_specs=pl.BlockSpec((1, rows, cols), lambda i, k: (i, 0, 0)), out_shape=_sds((nblk, rows, cols)),
                  compiler_params=_cparams(("parallel", "arbitrary")))(a, b)


def _diag_groups(o):
    o = o.reshape(2, S5G // S5_BLK, S5_BLK, S5P, S5_BLK, S5H)
    return jnp.einsum("rbgpgh->rbgph", o).reshape(2, S5G, S5P, S5H)


def _proj_fwd(name, x, mod, w_in, seq):
    t = x.shape[0]
    tm = min(TM_WIDE, seq)

    def fn(ctx, tiled, halos, exs, res, outs, accs, exaccs, scr):
        mod_ref, = exs
        sh, sc = mod_ref[0, 3:4, :], mod_ref[0, 4:5, :]
        h = _mx(tiled[0][...] * (1.0 + sc) + sh)
        outs[0][...] = h
        outs[1][...] = _dot(h, res[0][...])

    return _rowcall(name, fn, t, tm, seq // tm, tiled=[x], exs=[mod], res=[w_in],
                    out_tiled=[_sds((t, D), MXU_DTYPE), _sds((t, PW))])


def _shift_rows(cur, prev8, j):
    if j == 0:
        return cur
    rolled = pltpu.roll(cur, j, 0)
    top = jnp.where(_iota((8, cur.shape[1]), 0) < j, pltpu.roll(prev8, j, 0), rolled[0:8])
    return jnp.concatenate([top, rolled[8:]], axis=0)


def _shift_rows_up(cur, next8, j):
    if j == 0:
        return cur
    n = cur.shape[0]
    rolled = pltpu.roll(cur, n - j, 0)
    bot = jnp.where(_iota((8, cur.shape[1]), 0) >= 8 - j, pltpu.roll(next8, 8 - j, 0), rolled[n - 8:n])
    return jnp.concatenate([rolled[:n - 8], bot], axis=0)


def _softplus(x):
    return jnp.maximum(x, 0.0) + jnp.log(1.0 + jnp.exp(-jnp.abs(x)))


def _ssd_common(proj_ref, xpre, dtb_ref, alog_ref):
    xbc = xpre * _sigmoid(xpre)
    xs, bm, cm = xbc[:, 0:SW], xbc[:, SW:SW + 256], xbc[:, SW + 256:SW + 512]
    dtraw = proj_ref[:, PW - 128:PW] + dtb_ref[...]
    dt = _softplus(dtraw)
    a = -jnp.exp(alog_ref[...])
    tril = (_iota((CH, CH), 0) >= _iota((CH, CH), 1)).astype(F32)
    acs = _dot_hi(tril, dt * a)
    return xs, bm, cm, dtraw, dt, a, acs, acs.T


def _pair_lane(lo, hi):
    r = lo.shape[0]
    return jnp.where(_iota((r, 128), 1) < HP, lo, hi)


def _ssd_fwd(name, proj, conv_w, conv_b, dt_bias, a_log, d_rep, norm_w, seq):
    t = proj.shape[0]
    nb = t // seq

    def fn(ctx, tiled, halos, exs, res, outs, accs, exaccs, scr):
        @pl.when(ctx.seq_first)
        def _():
            scr[0][...] = jnp.zeros(scr[0].shape, F32)

        for b in range(nb):
            one(ctx, res, [r.at[b] for r in tiled + halos + outs + scr])

    def one(ctx, res, refs):
        proj_ref, halo_ref, yo_ref, xpre_ref, y_ref, sprev_ref, state_ref = refs
        cw_ref, cb_ref, dtb_ref, alog_ref, d_ref, nw_ref = res
        raw = proj_ref[:, SW:SW + D]
        prev8 = halo_ref[:, SW:SW + D] * jnp.where(ctx.seq_first, 0.0, 1.0)
        xpre = cb_ref[...] + cw_ref[3:4, :] * raw
        for j in (1, 2, 3):
            xpre = xpre + cw_ref[3 - j:4 - j, :] * _shift_rows(raw, prev8, j)
        xpre_ref[...] = xpre
        xs, bm, cm, dtraw, dt, a, acs, acst = _ssd_common(proj_ref, xpre, dtb_ref, alog_ref)
        causal = _iota((CH, CH), 0) >= _iota((CH, CH), 1)
        lane_lo = _iota((CH, 128), 1) < HP
        sprev = state_ref[...]
        sprev_ref[...] = sprev
        ys = []
        for g in range(NG):
            bmg, cmg = bm[:, g * NS:(g + 1) * NS], cm[:, g * NS:(g + 1) * NS]
            bmt = bmg.T
            cb = _dot(_mx(cmg), _mx(bmt))
            for q in (2 * g, 2 * g + 1):
                xsq = xs[:, 128 * q:128 * q + 128]
                xd = xsq * _pair_lane(dt[:, 2 * q:2 * q + 1], dt[:, 2 * q + 1:2 * q + 2])
                sp = sprev[:, 128 * q:128 * q + 128]
                ydiag = jnp.zeros((CH, 128), F32)
                snew = jnp.zeros((NS, 128), F32)
                for jj in range(2):
                    h = 2 * q + jj
                    col, row = acs[:, h:h + 1], acst[h:h + 1, :]
                    lm = jnp.where(causal, jnp.exp(jnp.minimum(col - row, 0.0)), 0.0)
                    xm = _mx(jnp.where(lane_lo if jj == 0 else ~lane_lo, xd, 0.0))
                    ydiag = ydiag + _dot(_mx(cb * lm), xm)
                    dec_row = jnp.exp(acst[h:h + 1, CH - 1:CH] - row)
                    snew = snew + _dot(_mx(bmt * dec_row), xm)
                e_pair = jnp.exp(_pair_lane(acs[:, 2 * q:2 * q + 1], acs[:, 2 * q + 1:2 * q + 2]))
                yoff = _dot(_mx(cmg), _mx(sp)) * e_pair
                cd = jnp.exp(_pair_lane(acst[2 * q:2 * q + 1, CH - 1:CH], acst[2 * q + 1:2 * q + 2, CH - 1:CH]))
                state_ref[:, 128 * q:128 * q + 128] = cd * sp + snew
                ys.append(ydiag + yoff + d_ref[:, 128 * q:128 * q + 128] * xsq)
        y = jnp.concatenate(ys, axis=1)
        y_ref[...] = y
        z = proj_ref[:, 0:SW]
        yz = y * (z * _sigmoid(z))
        outp = []
        for g in range(NG):
            seg = yz[:, 256 * g:256 * g + 256]
            rinv = lax.rsqrt(jnp.mean(seg * seg, axis=-1, keepdims=True) + LN_EPS)
            outp.append(seg * rinv * nw_ref[:, 256 * g:256 * g + 256])
        yo_ref[...] = _mx(jnp.concatenate(outp, axis=1))

    return _rowcall(name, fn, t, CH, seq // CH, tiled=[proj], halos=[(proj, 8)],
                    res=[conv_w, conv_b, dt_bias, a_log, d_rep, norm_w],
                    out_tiled=[_sds((t, SW), MXU_DTYPE), _sds((t, D)), _sds((t, SW)), _sds((t, SW))],
                    scratch=[pltpu.VMEM((nb, NS, SW), F32)], batch=nb)


def _ssd_bwd(name, dyo, proj, xpre_all, y_all, sprev_all, conv_w, dt_bias, a_log, d_rep, norm_w, seq, **carry):
    t = proj.shape[0]
    nb = t // seq

    def fn(ctx, tiled, halos, exs, res, outs, accs, exaccs, scr):
        @pl.when(ctx.seq_last)
        def _():
            for r in scr:
                r[...] = jnp.zeros(r.shape, F32)

        for b in range(nb):
            one(ctx, res, accs, [r.at[b] for r in tiled + halos + outs + scr])

    def one(ctx, res, accs, refs):
        dyo_ref, proj_ref, xpre_ref, y_ref, sprev_ref, halo_ref, dzx_ref, ddt_ref, ds_ref, nxt_ref = refs
        cw_ref, dtb_ref, alog_ref, d_ref, nw_ref = res
        dnw_acc, dd_acc, dcw_acc, dcb_acc, ddtb_acc, dalog_acc = accs
        xpre = xpre_ref[...]
        xs, bm, cm, dtraw, dt, a, acs, acst = _ssd_common(proj_ref, xpre, dtb_ref, alog_ref)
        y = y_ref[...]
        z = proj_ref[:, 0:SW]
        sz = _sigmoid(z)
        siluz = z * sz
        yz = y * siluz
        dyo = dyo_ref[...]
        dyz_parts, dnw_parts = [], []
        for g in range(NG):
            seg = yz[:, 256 * g:256 * g + 256]
            rinv = lax.rsqrt(jnp.mean(seg * seg, axis=-1, keepdims=True) + LN_EPS)
            yn = seg * rinv
            dseg = dyo[:, 256 * g:256 * g + 256]
            dnw_parts.append(jnp.sum(dseg * yn, axis=0, keepdims=True))
            dyn = dseg * nw_ref[:, 256 * g:256 * g + 256]
            dyz_parts.append(rinv * (dyn - yn * jnp.mean(dyn * yn, axis=-1, keepdims=True)))
        dyz = jnp.concatenate(dyz_parts, axis=1)
        _acc(dnw_acc, jnp.concatenate(dnw_parts, axis=1))
        dy = dyz * siluz
        dz = dyz * y * (sz * (1.0 + z * (1.0 - sz)))
        _acc(dd_acc, jnp.sum(dy * xs, axis=0, keepdims=True))

        causal = _iota((CH, CH), 0) >= _iota((CH, CH), 1)
        anti = _iota((CH, CH), 0) <= _iota((CH, CH), 1)
        lane_lo = _iota((CH, 128), 1) < HP
        lane_id = _iota((CH, 128), 1)
        last_row = _iota((CH, 128), 0) == CH - 1
        sprev = sprev_ref[...]
        dacs = jnp.zeros((CH, 128), F32)
        ddt_x = jnp.zeros((CH, 128), F32)
        dxs_parts, dbm_parts, dcm_parts = [], [], []
        for g in range(NG):
            bmg, cmg = bm[:, g * NS:(g + 1) * NS], cm[:, g * NS:(g + 1) * NS]
            bmt, cmt = bmg.T, cmg.T
            cb = _dot(_mx(cmg), _mx(bmt))
            cbt = _dot(_mx(bmg), _mx(cmt))
            dcb = jnp.zeros((CH, CH), F32)
            dcbt = jnp.zeros((CH, CH), F32)
            dbmg = jnp.zeros((CH, NS), F32)
            dcmg = jnp.zeros((CH, NS), F32)
            for q in (2 * g, 2 * g + 1):
                sl = slice(128 * q, 128 * q + 128)
                xsq = xs[:, sl]
                dtp = _pair_lane(dt[:, 2 * q:2 * q + 1], dt[:, 2 * q + 1:2 * q + 2])
                xd = xsq * dtp
                dyq = dy[:, sl]
                sp = sprev[:, sl]
                dsn = ds_ref[:, sl]
                e_pair = jnp.exp(_pair_lane(acs[:, 2 * q:2 * q + 1], acs[:, 2 * q + 1:2 * q + 2]))
                cd = jnp.exp(_pair_lane(acst[2 * q:2 * q + 1, CH - 1:CH], acst[2 * q + 1:2 * q + 2, CH - 1:CH]))
                dye = dyq * e_pair
                dcmg = dcmg + _dot(_mx(dye), _mx(sp.T))
                dsp = _dot(_mx(cmt), _mx(dye)) + cd * dsn
                yoff = _dot(_mx(cmg), _mx(sp)) * e_pair
                dacs_lane = dyq * yoff
                dxd = jnp.zeros((CH, 128), F32)
                sds = jnp.sum(dsn * sp, axis=0, keepdims=True) * cd
                for jj in range(2):
                    h = 2 * q + jj
                    hm = lane_lo if jj == 0 else ~lane_lo
                    col, row = acs[:, h:h + 1], acst[h:h + 1, :]
                    lm = jnp.where(causal, jnp.exp(jnp.minimum(col - row, 0.0)), 0.0)
                    lmt = jnp.where(anti, jnp.exp(jnp.minimum(row - col, 0.0)), 0.0)
                    xm = _mx(jnp.where(hm, xd, 0.0))
                    dym = _mx(jnp.where(hm, dyq, 0.0))
                    gm = _dot_nt(dym, xm)
                    gmt = _dot_nt(xm, dym)
                    dcb = dcb + gm * lm
                    dcbt = dcbt + gmt * lmt
                    dxd = dxd + _dot(_mx(cbt * lmt), dym)
                    w = gm * cb * lm
                    wt = gmt * cbt * lmt
                    dacs_h = jnp.sum(w, axis=1, keepdims=True) - jnp.sum(wt, axis=1, keepdims=True)
                    alast = acst[h:h + 1, CH - 1:CH]
                    dec_col = jnp.exp(alast - col)
                    dsm = _mx(jnp.where(hm[0:NS], dsn, 0.0))
                    qh = _dot_nt(xm, dsm)
                    dbmg = dbmg + qh * dec_col
                    ddec = jnp.sum(qh * bmg, axis=1, keepdims=True)
                    dxd = dxd + _dot(_mx(bmg * dec_col), dsm)
                    dacs_h = dacs_h - ddec * dec_col
                    dacs_h = dacs_h + jnp.sum(jnp.where(hm, dacs_lane, 0.0), axis=1, keepdims=True)
                    tail = jnp.sum(ddec * dec_col, axis=0, keepdims=True) + jnp.sum(
                        jnp.where(hm[0:1], sds, 0.0), axis=1, keepdims=True)
                    dacs = dacs + jnp.where(lane_id == h, dacs_h, 0.0) + jnp.where(
                        last_row & (lane_id == h), tail, 0.0)
                ds_ref[:, sl] = dsp
                for jj in range(2):
                    h = 2 * q + jj
                    hm = lane_lo if jj == 0 else ~lane_lo
                    ddt_x = ddt_x + jnp.where(lane_id == h, jnp.sum(jnp.where(hm, dxd * xsq, 0.0), axis=1, keepdims=True), 0.0)
                dxs_parts.append(dxd * dtp + d_ref[:, sl] * dyq)
            dcmg = dcmg + _dot(_mx(dcb), _mx(bmg))
            dbmg = dbmg + _dot(_mx(dcbt), _mx(cmg))
            dbm_parts.append(dbmg)
            dcm_parts.append(dcmg)
        triu = (_iota((CH, CH), 0) <= _iota((CH, CH), 1)).astype(F32)
        dadt = _dot_hi(triu, dacs)
        ddt = dadt * a + ddt_x
        _acc(dalog_acc, jnp.sum(dadt * dt, axis=0, keepdims=True) * a)
        ddtraw = ddt * _sigmoid(dtraw)
        ddt_ref[...] = ddtraw
        _acc(ddtb_acc, jnp.sum(ddtraw, axis=0, keepdims=True))
        dxbc = jnp.concatenate(dxs_parts + dbm_parts + dcm_parts, axis=1)
        sx = _sigmoid(xpre)
        dpre = dxbc * (sx * (1.0 + xpre * (1.0 - sx)))
        _acc(dcb_acc, jnp.sum(dpre, axis=0, keepdims=True))
        raw = proj_ref[:, SW:SW + D]
        prev8 = halo_ref[:, SW:SW + D] * jnp.where(ctx.seq_first, 0.0, 1.0)
        next8 = nxt_ref[...]
        draw = cw_ref[3:4, :] * dpre
        dcw = [None] * 4
        dcw[3] = jnp.sum(dpre * raw, axis=0, keepdims=True)
        for j in (1, 2, 3):
            dcw[3 - j] = jnp.sum(dpre * _shift_rows(raw, prev8, j), axis=0, keepdims=True)
            draw = draw + cw_ref[3 - j:4 - j, :] * _shift_rows_up(dpre, next8, j)
        _acc(dcw_acc, jnp.concatenate(dcw + [jnp.zeros((4, D), F32)], axis=0))
        nxt_ref[...] = dpre[0:8]
        dzx_ref[:, 0:SW] = dz
        dzx_ref[:, SW:SW + D] = draw

    return _rowcall(name, fn, t, CH, seq // CH, tiled=[dyo, proj, xpre_all, y_all, sprev_all], halos=[(proj, 8)],
                    res=[conv_w, dt_bias, a_log, d_rep, norm_w],
                    out_tiled=[_sds((t, SW + D)), _sds((t, 128))],
                    out_acc=[_sds((1, SW)), _sds((1, SW)), _sds((8, D)), _sds((1, D)), _sds((1, 128)), _sds((1, 128))],
                    scratch=[pltpu.VMEM((nb, NS, SW), F32), pltpu.VMEM((nb, 8, D), F32)], reverse=True, batch=nb, **carry)


def _gelu(y):
    k = math.sqrt(2.0 / math.pi)
    return 0.5 * y * (1.0 + jnp.tanh(k * (y + 0.044715 * y * y * y)))


def _gelu_grad(y):
    k = math.sqrt(2.0 / math.pi)
    th = jnp.tanh(k * (y + 0.044715 * y * y * y))
    return 0.5 * (1.0 + th) + 0.5 * y * (1.0 - th * th) * k * (1.0 + 3.0 * 0.044715 * y * y)


S5T = 256


def _cmul_add(xr, xi, ar, ai, sr, si):
    return xr + ar * sr - ai * si, xi + ar * si + ai * sr


def _s5_fwd(name, proj, bbd, cbd, pw, tab, d5, w_glu, b_glu, seq, **carry):
    t = proj.shape[0]
    tm = S5T

    def fn(ctx, tiled, halos, exs, res, outs, accs, exaccs, scr):
        proj_ref, = tiled
        bbd_ref, cbd_ref, pw_ref, tab_ref, d_ref, wg_ref, bg_ref = res
        out_ref, xst_ref, y_ref, xb_ref, ub_ref = outs
        carry_ref, = scr

        @pl.when(ctx.seq_first)
        def _():
            carry_ref[...] = jnp.zeros(carry_ref.shape, F32)

        u = proj_ref[:, 1536:2048]
        bu = _dot(_mx(u), bbd_ref[...])
        xr, xi = bu[:, :S5L].reshape(tm // 8, 8, S5L), bu[:, S5L:].reshape(tm // 8, 8, S5L)
        for k, sh in enumerate((1, 2, 4)):
            xr, xi = _cmul_add(xr, xi, pw_ref[k, :, :S5L], pw_ref[k, :, S5L:], pltpu.roll(xr, sh, 1), pltpu.roll(xi, sh, 1))
        xst_ref[:, :S5L] = xr.reshape(tm, S5L)
        xst_ref[:, S5L:] = xi.reshape(tm, S5L)

        def tile_fix(i, c):
            cr, ci = c
            rows = pl.ds(pl.multiple_of(i * 8, 8), 8)
            tr, ti = _cmul_add(xst_ref[rows, :S5L], xst_ref[rows, S5L:], tab_ref[:, :S5L], tab_ref[:, S5L:], cr, ci)
            xst_ref[rows, :S5L] = tr
            xst_ref[rows, S5L:] = ti
            return tr[7:8], ti[7:8]

        cr, ci = lax.fori_loop(0, tm // 8, tile_fix, (carry_ref[0:1, :S5L], carry_ref[0:1, S5L:]))
        carry_ref[0:1, :S5L] = cr
        carry_ref[0:1, S5L:] = ci
        xb = _mx(xst_ref[...])
        xb_ref[...] = xb
        ub_ref[...] = _mx(u)
        y = _dot(xb, cbd_ref[...]) + u * d_ref[...]
        y_ref[...] = y
        g = _gelu(y)
        v = _dot(_mx(g), wg_ref[...]) + bg_ref[...]
        out_ref[...] = _mx(g * _sigmoid(v))

    return _rowcall(name, fn, t, tm, seq // tm, tiled=[proj], res=[bbd, cbd, pw, tab, d5, w_glu, b_glu],
                    out_tiled=[_sds((t, SW), MXU_DTYPE), _sds((t, 2 * S5L)), _sds((t, SW)),
                               _sds((t, 2 * S5L), MXU_DTYPE), _sds((t, SW), MXU_DTYPE)],
                    scratch=[pltpu.VMEM((8, 2 * S5L), F32)], **carry)


def _s5_bwd(name, dout, proj, xst, y_all, bbdt, cbdt, pwc, tabc, d5, w_glu, w_glut, b_glu, seq, **carry):
    t = proj.shape[0]
    tm = S5T

    def fn(ctx, tiled, halos, exs, res, outs, accs, exaccs, scr):
        dout_ref, proj_ref, xst_ref, y_ref = tiled
        halo_ref, = halos
        bbdt_ref, cbdt_ref, pw_ref, tab_ref, d_ref, wg_ref, wgt_ref, bg_ref = res
        du_ref, lam_ref, dyb_ref, gb_ref, dvb_ref = outs
        da_acc, dd_acc, dbg_acc = accs
        carry_ref, lamf_ref = scr

        @pl.when(ctx.seq_last)
        def _():
            carry_ref[...] = jnp.zeros(carry_ref.shape, F32)

        u = proj_ref[:, 1536:2048]
        y = y_ref[...]
        g = _gelu(y)
        v = _dot(_mx(g), wg_ref[...]) + bg_ref[...]
        sg = _sigmoid(v)
        dout = dout_ref[...]
        dv = dout * g * sg * (1.0 - sg)
        dvb = _mx(dv)
        dvb_ref[...] = dvb
        gb_ref[...] = _mx(g)
        _acc(dbg_acc, jnp.sum(dv, axis=0, keepdims=True))
        dg = dout * sg + _dot(dvb, wgt_ref[...])
        dy = dg * _gelu_grad(y)
        dyb = _mx(dy)
        dyb_ref[...] = dyb
        _acc(dd_acc, jnp.sum(dy * u, axis=0, keepdims=True))
        dx = _dot(dyb, cbdt_ref[...])
        xr, xi = dx[:, :S5L].reshape(tm // 8, 8, S5L), dx[:, S5L:].reshape(tm // 8, 8, S5L)
        for k, sh in enumerate((1, 2, 4)):
            xr, xi = _cmul_add(xr, xi, pw_ref[k, :, :S5L], pw_ref[k, :, S5L:], pltpu.roll(xr, 8 - sh, 1),
                               pltpu.roll(xi, 8 - sh, 1))
        lamf_ref[:, :S5L] = xr.reshape(tm, S5L)
        lamf_ref[:, S5L:] = xi.reshape(tm, S5L)

        def tile_fix(i, c):
            cr, ci = c
            rows = pl.ds(pl.multiple_of((tm // 8 - 1 - i) * 8, 8), 8)
            tr, ti = _cmul_add(lamf_ref[rows, :S5L], lamf_ref[rows, S5L:], tab_ref[:, :S5L], tab_ref[:, S5L:], cr, ci)
            lamf_ref[rows, :S5L] = tr
            lamf_ref[rows, S5L:] = ti
            return tr[0:1], ti[0:1]

        cr, ci = lax.fori_loop(0, tm // 8, tile_fix, (carry_ref[0:1, :S5L], carry_ref[0:1, S5L:]))
        carry_ref[0:1, :S5L] = cr
        carry_ref[0:1, S5L:] = ci
        lam = lamf_ref[...]
        lamb = _mx(lam)
        lam_ref[...] = lamb
        du_ref[...] = dy * d_ref[...] + _dot(lamb, bbdt_ref[...])
        prev8 = halo_ref[...] * jnp.where(ctx.seq_first, 0.0, 1.0)
        xprev = _shift_rows(xst_ref[...], prev8, 1)
        lr, li = lam[:, :S5L], lam[:, S5L:]
        pr, pi = xprev[:, :S5L], xprev[:, S5L:]
        dar = jnp.sum(lr * pr + li * pi, axis=0, keepdims=True)
        dai = jnp.sum(li * pr - lr * pi, axis=0, keepdims=True)
        _acc(da_acc, jnp.concatenate([dar, dai], axis=1))

    return _rowcall(name, fn, t, tm, seq // tm, tiled=[dout, proj, xst, y_all], halos=[(xst, 8)],
                    res=[bbdt, cbdt, pwc, tabc, d5, w_glu, w_glut, b_glu],
                    out_tiled=[_sds((t, SW)), _sds((t, 2 * S5L), MXU_DTYPE), _sds((t, SW), MXU_DTYPE),
                               _sds((t, SW), MXU_DTYPE), _sds((t, SW), MXU_DTYPE)],
                    out_acc=[_sds((1, 2 * S5L)), _sds((1, SW)), _sds((1, SW))],
                    scratch=[pltpu.VMEM((8, 2 * S5L), F32), pltpu.VMEM((tm, 2 * S5L), F32)], reverse=True, **carry)


def _out_fwd(name, yssd, ys5, x1, mod, w_out, lng, lnb, seq):
    t = x1.shape[0]
    tm = min(TM_WIDE, seq)

    def fn(ctx, tiled, halos, exs, res, outs, accs, exaccs, scr):
        ya_ref, yb_ref, x_ref = tiled
        mod_ref, = exs
        w_ref, g_ref, b_ref = res
        m = _dot(ya_ref[...], w_ref[0:SW, :]) + _dot(yb_ref[...], w_ref[SW:2 * SW, :])
        r = ALPHA * x_ref[...] + mod_ref[0, 5:6, :] * m
        outs[0][...] = _ln_fwd(r, g_ref[...], b_ref[...])
        outs[1][...] = r
        outs[2][...] = m

    return _rowcall(name, fn, t, tm, seq // tm, tiled=[yssd, ys5, x1], exs=[mod], res=[w_out, lng, lnb],
                    out_tiled=[_sds((t, D)), _sds((t, D)), _sds((t, D))])


def _out_bwd(name, dxo, r, m, mod, lng, w_outt, seq, **carry):
    t = dxo.shape[0]
    tm = min(TM_WIDE, seq)

    def fn(ctx, tiled, halos, exs, res, outs, accs, exaccs, scr):
        dxo_ref, r_ref, m_ref = tiled
        mod_ref, = exs
        g_ref, wt_ref = res
        dr, dgam, dbet = _ln_bwd(r_ref[...], g_ref[...], dxo_ref[...])
        outs[0][...] = dr
        _acc(accs[0], dgam)
        _acc(accs[1], dbet)
        _acc(exaccs[0].at[0], jnp.sum(dr * m_ref[...], axis=0, keepdims=True))
        dm = _mx(mod_ref[0, 5:6, :] * dr)
        outs[1][...] = dm
        dyc = _dot(dm, wt_ref[...])
        outs[2][...] = dyc[:, 0:SW]
        outs[3][...] = dyc[:, SW:2 * SW]

    b = mod.shape[0]
    return _rowcall(name, fn, t, tm, seq // tm, tiled=[dxo, r, m], exs=[mod], res=[lng, w_outt],
                    out_tiled=[_sds((t, D)), _sds((t, D), MXU_DTYPE), _sds((t, SW)), _sds((t, SW))],
                    out_acc=[_sds((1, D)), _sds((1, D))], out_exacc=[_sds((b, 1, D))], **carry)


def _s5_discretise(a_re, a_im, log_dt, b_re, b_im):
    dt = jnp.exp(log_dt)[:, None]
    mag = jnp.exp(dt * a_re)
    ab_re, ab_im = mag * jnp.cos(dt * a_im), mag * jnp.sin(dt * a_im)
    den = a_re * a_re + a_im * a_im
    nr, ni = ab_re - 1.0, ab_im
    f_re, f_im = (nr * a_re + ni * a_im) / den, (ni * a_re - nr * a_im) / den
    bb_re = f_re[..., None] * b_re - f_im[..., None] * b_im
    bb_im = f_re[..., None] * b_im + f_im[..., None] * b_re
    return ab_re, ab_im, bb_re, bb_im


def _s5_tables(ab_re, ab_im):
    ar, ai = ab_re.reshape(1, S5L), ab_im.reshape(1, S5L)
    pows = [(ar, ai)]
    for _ in range(7):
        pr, pi = pows[-1]
        pows.append((pr * ar - pi * ai, pr * ai + pi * ar))

    def pack(rows, sign):
        return jnp.concatenate([jnp.concatenate([r for r, _ in rows], axis=0),
                                jnp.concatenate([sign * i for _, i in rows], axis=0)], axis=1)

    row = jnp.arange(8)[:, None]
    pw = jnp.stack([jnp.where(row >= sh, pack([pows[sh - 1]], 1.0), 0.0) for sh in (1, 2, 4)])
    pwc = jnp.stack([jnp.where(row < 8 - sh, pack([pows[sh - 1]], -1.0), 0.0) for sh in (1, 2, 4)])
    tab = pack(pows, 1.0)
    tabc = pack(pows[::-1], -1.0)
    return pw, tab, pwc, tabc


class _GradGroup:
    def __init__(self, tag, grads, place):
        self.tag, self.names, (self.half, self.chip) = tag, list(grads), place
        self.gsh = [_shard_major(g) if n in COL_SHARDED else g.reshape(4, g.shape[0] // 4, g.shape[1])
                    for n, g in grads.items()]

    def sibling(self):
        return _rs_sibling_stage(self.gsh)

    def chips(self, received):
        self.sums, sums_bf = _rs_add(self.tag + "_add", self.gsh, received, self.half)
        return _rs_chips_stage(sums_bf)

    def join(self, received):
        return _rs_join_stage(_rs_sum(self.tag + "_sum", self.sums, received, self.chip, self.half))

    def result(self, joined):
        return dict(zip(self.names, joined))


def _hid(fn, *args, stage=None, **kw):
    if stage is None:
        return fn(*args, **kw), None
    return fn(*args, carry=stage, **kw)


def _local_step(x, tgt, mod, w, sp, seq, dist=None):
    t = x.shape[0]
    mxu = MXU_DTYPE
    big = {}

    def group(tag, grads):
        if dist is None:
            big.update(grads)
            return None
        return _GradGroup(tag, grads, dist[2])

    mid = {"carry_mid": 0.6} if dist else {}
    (h1, ab1, s1), got = _hid(_ffn_fwd_hidden, "ffn1_fwd_a", x, mod, 0, w["ffn1_w1"], w["ffn1_w3"], seq,
                              stage=dist and dist[0][0], **mid)
    if dist:
        w = {**w, **dist[0][1](got)}
    x1, r1, f1 = _ffn_fwd_out("ffn1_fwd_b", s1, x, mod, 0, w["ffn1_w2"], sp["ln1_g"], sp["ln1_b"], seq)
    h2, proj = _proj_fwd("proj_fwd", x1, mod, w["w_in"], seq)
    yssd, xpre, yraw, sprev = _ssd_fwd("ssd_fwd", proj, sp["conv_w"], sp["conv_b"], sp["dt_bias"], sp["a_log"],
                                       sp["d_rep"], sp["ssd_norm_w"], seq)
    (ab_re, ab_im, bb_re, bb_im), disc_vjp = jax.vjp(_s5_discretise, sp["s5_a_re"], sp["s5_a_im"], sp["s5_log_dt"],
                                                     sp["s5_b_re"], sp["s5_b_im"])
    eye = jnp.eye(S5G, dtype=F32)
    bbd = jnp.concatenate([jnp.einsum("gk,gph->ghkp", eye, bb_re).reshape(SW, S5L),
                           jnp.einsum("gk,gph->ghkp", eye, bb_im).reshape(SW, S5L)], axis=1).astype(mxu)
    cbd = jnp.concatenate([jnp.einsum("gk,ghp->gpkh", eye, sp["s5_c_re"]).reshape(S5L, SW),
                           -jnp.einsum("gk,ghp->gpkh", eye, sp["s5_c_im"]).reshape(S5L, SW)], axis=0).astype(mxu)
    bbdt = jnp.concatenate([jnp.einsum("gk,gph->kpgh", eye, bb_re).reshape(S5L, SW),
                            jnp.einsum("gk,gph->kpgh", eye, bb_im).reshape(S5L, SW)], axis=0).astype(mxu)
    cbdt = jnp.concatenate([jnp.einsum("gk,ghp->khgp", eye, sp["s5_c_re"]).reshape(SW, S5L),
                            -jnp.einsum("gk,ghp->khgp", eye, sp["s5_c_im"]).reshape(SW, S5L)], axis=1).astype(mxu)
    pw, tab, pwc, tabc = _s5_tables(lax.stop_gradient(ab_re), lax.stop_gradient(ab_im))
    (ys5, xst, y5, xstb, ub), got = _hid(_s5_fwd, "s5_fwd", proj, bbd, cbd, pw, tab, sp["s5_d"], w["w_glu"], sp["b_glu"],
                                         seq, stage=dist and dist[1][0], **mid)
    if dist:
        w = {**w, **dist[1][1](got)}
    x2, r2, m2 = _out_fwd("out_fwd", yssd, ys5, x1, mod, w["w_out"], sp["ln2_g"], sp["ln2_b"], seq)
    h3, ab3, s3 = _ffn_fwd_hidden("ffn2_fwd_a", x2, mod, 6, w["ffn2_w1"], w["ffn2_w3"], seq)
    x3, r3, f3, dy3, loss_acc = _ffn_fwd_out("ffn2_fwd_b", s3, x2, mod, 6, w["ffn2_w2"], sp["ln3_g"], sp["ln3_b"], seq, tgt=tgt)
    dx2, df3, dab3, dg3g, dg3b, dgate3, dsh3, dsc3 = _ffn_bwd("ffn2_bwd", dy3, r3, ab3, f3, x2, mod, 6, sp["ln3_g"],
                                                              w["ffn2_w2t"], w["ffn2_w1t"], w["ffn2_w3t"], seq)
    grp = group("rs_ffn2", dict(ffn2_w1=_tn_matmul("ffn2_dw1", dab3, h3, FB, D, a_cols=(0, FF)),
                                ffn2_w3=_tn_matmul("ffn2_dw3", dab3, h3, FB, D, a_cols=(FF, FF)),
                                ffn2_w2=_tn_matmul("ffn2_dw2", s3, df3, FB, D)))
    (dr2, dm2, dyssd, dys5, dg2g, dg2b, dgate2), got = _hid(_out_bwd, "out_bwd", dx2, r2, m2, mod, sp["ln2_g"],
                                                          w["w_outt"], seq, stage=grp and grp.sibling())
    g_w_out = jnp.concatenate([_tn_matmul("dw_out_a", yssd, dm2, SW, D), _tn_matmul("dw_out_b", ys5, dm2, SW, D)], axis=0)
    (du, lam, dy5b, g5b, dv5b, da5, dd5, dbglu), got = _hid(
        _s5_bwd, "s5_bwd", dys5, proj, xst, y5, bbdt, cbdt, pwc, tabc, sp["s5_d"], w["w_glu"], w["w_glut"], sp["b_glu"],
        seq, stage=grp and grp.chips(got))
    g_w_glu = _tn_matmul("dw_glu", g5b, dv5b, SW, SW)
    dbb = _diag_groups(_tn_diag("s5_db", lam, ub))
    dcc = _diag_groups(_tn_diag("s5_dc", xstb, dy5b))
    (dzx, ddt, dnw, ddl, dcw, dcb, ddtb, dalog), got = _hid(
        _ssd_bwd, "ssd_bwd", dyssd, proj, xpre, yraw, sprev, sp["conv_w"], sp["dt_bias"], sp["a_log"], sp["d_rep"],
        sp["ssd_norm_w"], seq, stage=grp and grp.join(got))
    if grp:
        big.update(grp.result(got))
    dx1, dproj, dsh2, dsc2 = _mod_bwd("proj_bwd", dzx, dr2, x1, mod, 3, [w["w_int"]], seq, extra=(du, ddt))
    gwi = _tn_matmul("dw_in", h2, dproj, D, PW)
    grp = group("rs_mix", dict(w_in=jnp.concatenate([gwi[:, :1536], gwi[:, 2048:2056], gwi[:, 1536:2048]], axis=1),
                               w_glu=g_w_glu, w_out=g_w_out))
    (dr1, df1, dab1, dg1g, dg1b, dgate1), got = _hid(_ffn_bwd1, "ffn1_bwd1", dx1, r1, ab1, f1, mod, 0, sp["ln1_g"],
                                                          w["ffn1_w2t"], seq, stage=grp and grp.sibling())
    g1, got = _hid(_tn_matmul, "ffn1_dw1", dab1, h1, FB, D, a_cols=(0, FF), stage=grp and grp.chips(got))
    g3, got = _hid(_tn_matmul, "ffn1_dw3", dab1, h1, FB, D, a_cols=(FF, FF), stage=grp and grp.join(got))
    if grp:
        big.update(grp.result(got))
    grp = group("rs_ffn1a", dict(ffn1_w1=g1, ffn1_w3=g3))
    g2, got = _hid(_tn_matmul, "ffn1_dw2", s1, df1, FB, D, stage=grp and grp.sibling())
    grp2 = group("rs_ffn1b", dict(ffn1_w2=g2))
    (dx0, dsh1, dsc1), got = _hid(_mod_bwd, "ffn1_bwd2", dab1, dr1, x, mod, 0, [w["ffn1_w1t"], w["ffn1_w3t"]], seq,
                                  stage=grp and _merge_stages(grp.chips(got), grp2.sibling()))
    if grp:
        n = len(grp.names)
        got = _run_stage("rs_ffn1_tail", _merge_stages(grp.join(got[:n]), grp2.chips(got[n:])))
        big.update(grp.result(got[:n]))
        big.update(grp2.result(_run_stage("rs_ffn1b_join", grp2.join(got[n:]))))
    dmod = jnp.concatenate([dsh1, dsc1, dgate1, dsh2, dsc2, dgate2, dsh3, dsc3, dgate3], axis=1)
    dc_re, dc_im = dcc[0].transpose(0, 2, 1), -dcc[1].transpose(0, 2, 1)
    g_a_re, g_a_im, g_log_dt, g_b_re, g_b_im = disc_vjp(
        (da5[:, :S5L].reshape(S5G, S5P), da5[:, S5L:].reshape(S5G, S5P), dbb[0], dbb[1]))
    small = dict(ln1_g=dg1g, ln1_b=dg1b, ln2_g=dg2g, ln2_b=dg2b, ln3_g=dg3g, ln3_b=dg3b, conv_w=dcw[0:4], conv_b=dcb,
                 dt_bias=ddtb[:, :NH], a_log=dalog[:, :NH], d_ssd=jnp.sum(ddl.reshape(NH, HP), axis=1).reshape(1, NH),
                 ssd_norm_w=dnw, s5_a_re=g_a_re, s5_a_im=g_a_im, s5_log_dt=g_log_dt, s5_b_re=g_b_re, s5_b_im=g_b_im,
                 s5_c_re=dc_re, s5_c_im=dc_im, s5_d=dd5, w_glu_b=dbglu)
    return loss_acc[0, 0], dx0, dmod, big, small


def _place():
    return lax.axis_index("x"), lax.axis_index("y"), lax.axis_index("c")


def _other_chips(x, y):
    return [(1 - x, y), (x, 1 - y), (1 - x, 1 - y)]


def _allgather8(name, a):
    r, n = a.shape

    def body(x_ref, out_ref, send_sems, recv_sems, local_sem):
        x, y, c = _place()
        me, sibling = (x, y, c), (x, y, 1 - c)
        chips = _other_chips(x, y)

        def rows(px, py, pc):
            return out_ref.at[pl.ds(pl.multiple_of((4 * px + 2 * py + pc) * r, 8), r), :]

        def copy(k, block, to, src=None):
            return pltpu.make_async_remote_copy(src_ref=rows(*block) if src is None else src, dst_ref=rows(*block),
                                                send_sem=send_sems.at[k], recv_sem=recv_sems.at[k], device_id=to,
                                                device_id_type=MESH_T)

        mine = pltpu.make_async_copy(x_ref, rows(*me), local_sem)
        mine.start()
        first = [copy(0, me, sibling, src=x_ref)]
        first += [copy(1 + j, me, (*chip, c), src=x_ref) for j, chip in enumerate(chips)]
        for cp in first:
            cp.start()
        passed = [copy(4 + j, (*chip, c), sibling) for j, chip in enumerate(chips)]
        for j, chip in enumerate(chips):
            copy(1 + j, (*chip, c), me).wait_recv()
            passed[j].start()
        copy(0, sibling, me).wait_recv()
        for j, chip in enumerate(chips):
            copy(4 + j, (*chip, 1 - c), me).wait_recv()
        for cp in first + passed:
            cp.wait_send()
        mine.wait()

    out = _pcall(body, name=name, out_shape=_sds((8 * r, n), a.dtype),
                 in_specs=[pl.BlockSpec(memory_space=pltpu.VMEM)], out_specs=pl.BlockSpec(memory_space=pltpu.VMEM),
                 scratch_shapes=[pltpu.SemaphoreType.DMA((7,)), pltpu.SemaphoreType.DMA((7,)), pltpu.SemaphoreType.DMA],
                 compiler_params=_cparams())(a)
    return out.reshape(8, r, n)


class _Stage:
    def __init__(self, ins, out_shapes, n_sems, start, finish, mid=None, aliases=None):
        self.ins, self.out_shapes, self.n_sems = list(ins), list(out_shapes), tuple(n_sems)
        self.start, self.mid, self.finish = start, mid, finish
        self.aliases = dict(aliases or {})

    def sem_shapes(self):
        return [pltpu.SemaphoreType.DMA((n,)) for n in self.n_sems]

    def run(self, refs, at_start=None, at_mid=None, at_finish=None, between=None):
        if between is None:
            for part in (self.start, self.mid, self.finish):
                if part is not None:
                    part(*refs)
            return
        pl.when(at_start)(functools.partial(self.start, *refs))
        if self.mid is not None:
            pl.when(at_mid)(functools.partial(self.mid, *refs))
        between()
        pl.when(at_finish)(functools.partial(self.finish, *refs))


def _merge_stages(a, b):
    n_in, n_out, n_sem = len(a.ins), len(a.out_shapes), len(a.n_sems)

    def both(fa, fb):
        def run(ins, outs, sems):
            fa(ins[:n_in], outs[:n_out], sems[:n_sem])
            fb(ins[n_in:], outs[n_out:], sems[n_sem:])
        return run

    aliases = {**a.aliases, **{n_in + k: n_out + v for k, v in b.aliases.items()}}
    return _Stage(a.ins + b.ins, a.out_shapes + b.out_shapes, a.n_sems + b.n_sems, both(a.start, b.start),
                  both(a.finish, b.finish), aliases=aliases)


def _run_stage(name, st):
    n_in, n_out = len(st.ins), len(st.out_shapes)

    def body(*refs):
        st.run((refs[:n_in], refs[n_in:n_in + n_out], refs[n_in + n_out:]))

    any_spec = pl.BlockSpec(memory_space=pl.ANY)
    return _pcall(body, name=name, out_shape=st.out_shapes, in_specs=[any_spec] * n_in, out_specs=[any_spec] * n_out,
                  input_output_aliases=st.aliases, scratch_shapes=st.sem_shapes(), compiler_params=_cparams())(*st.ins)


def _rows(ref_rows, half, align):
    hr = ref_rows // 2
    return pl.ds(pl.multiple_of(half * hr, align), hr)


def _gather_stage(shards):
    n = len(shards)
    pairs = [(i, j) for i in range(n) for j in range(3)]

    def env(ins, outs, sems):
        x, y, c = _place()
        chips = _other_chips(x, y)

        def copy(i, k, chip, half, to, src=None):
            dst = outs[i].at[2 * chip[0] + chip[1], _rows(shards[i].shape[0], half, 16)]
            return pltpu.make_async_remote_copy(src_ref=dst if src is None else src, dst_ref=dst,
                                                send_sem=sems[0].at[6 * i + k], recv_sem=sems[1].at[6 * i + k],
                                                device_id=to, device_id_type=MESH_T)

        def first(i, j):
            return copy(i, j, (x, y), c, (*chips[j], c), src=ins[i].at[_rows(shards[i].shape[0], c, 16)])

        def passed(i, j, half):
            return copy(i, 3 + j, chips[j], half, (x, y, 1 - c))

        def landed(i, j):
            return copy(i, j, chips[j], c, (x, y, 1 - c))

        return c, first, passed, landed

    def start(ins, outs, sems):
        c, first, passed, landed = env(ins, outs, sems)
        for i, j in pairs:
            first(i, j).start()

    def mid(ins, outs, sems):
        c, first, passed, landed = env(ins, outs, sems)
        for i, j in pairs:
            landed(i, j).wait_recv()
            passed(i, j, c).start()

    def finish(ins, outs, sems):
        c, first, passed, landed = env(ins, outs, sems)
        for i, j in pairs:
            passed(i, j, 1 - c).wait_recv()
        for i, j in pairs:
            first(i, j).wait_send()
            passed(i, j, c).wait_send()

    return _Stage(shards, [_sds((4,) + s.shape, s.dtype) for s in shards], (6 * n, 6 * n), start, finish, mid)


def _rs_sibling_stage(gs):
    n = len(gs)

    def copies(ins, outs, sems):
        x, y, c = _place()
        return [pltpu.make_async_remote_copy(src_ref=ins[i].at[:, _rows(gs[i].shape[1], 1 - c, 8)], dst_ref=outs[i],
                                             send_sem=sems[0].at[i], recv_sem=sems[1].at[i], device_id=(x, y, 1 - c),
                                             device_id_type=MESH_T) for i in range(n)]

    def start(*refs):
        for cp in copies(*refs):
            cp.start()

    def finish(*refs):
        for cp in copies(*refs):
            cp.wait()

    return _Stage(gs, [_sds((4, g.shape[1] // 2, g.shape[2]), g.dtype) for g in gs], (n, n), start, finish)


def _rs_chips_stage(hs):
    n = len(hs)

    def copies(ins, outs, sems):
        x, y, c = _place()
        return [pltpu.make_async_remote_copy(src_ref=ins[i].at[2 * chip[0] + chip[1]], dst_ref=outs[i].at[j],
                                             send_sem=sems[0].at[3 * i + j], recv_sem=sems[1].at[3 * i + j],
                                             device_id=(*chip, c), device_id_type=MESH_T)
                for i in range(n) for j, chip in enumerate(_other_chips(x, y))]

    def start(*refs):
        for cp in copies(*refs):
            cp.start()

    def finish(*refs):
        for cp in copies(*refs):
            cp.wait()

    return _Stage(hs, [_sds((3,) + h.shape[1:], h.dtype) for h in hs], (3 * n, 3 * n), start, finish)


def _rs_join_stage(fs):
    n = len(fs)

    def copy(outs, sems, i, half):
        x, y, c = _place()
        part = outs[i].at[_rows(fs[i].shape[0], c if half == "mine" else 1 - c, 8)]
        return pltpu.make_async_remote_copy(src_ref=part, dst_ref=part, send_sem=sems[0].at[i], recv_sem=sems[1].at[i],
                                            device_id=(x, y, 1 - c), device_id_type=MESH_T)

    def start(ins, outs, sems):
        for i in range(n):
            copy(outs, sems, i, "mine").start()

    def finish(ins, outs, sems):
        for i in range(n):
            copy(outs, sems, i, "theirs").wait_recv()
        for i in range(n):
            copy(outs, sems, i, "mine").wait_send()

    return _Stage(fs, [_sds(f.shape, f.dtype) for f in fs], (n, n), start, finish, aliases={i: i for i in range(n)})


def _row_block(r, cap=2048):
    b = min(r, cap)
    while r % b or b % 8:
        b -= 8
    return b


RS_SPLIT = 2


def _rs_add(name, gs, r1s, sel):
    n = len(gs)

    def body(sel_ref, *refs):
        g_refs, r_refs, h_refs, b_refs = (refs[k * n:(k + 1) * n] for k in range(4))
        for i in range(n):
            h = g_refs[i][...] + r_refs[i][...]
            h_refs[i][...] = h
            b_refs[i][...] = h.astype(BF16)

    def blk(g):
        return (1, g.shape[1] // 2 // RS_SPLIT, g.shape[2])

    here = lambda k, j, s: (k, j, 0)
    in_specs = [pl.BlockSpec(blk(g), lambda k, j, s: (k, s[0] * RS_SPLIT + j, 0)) for g in gs]
    in_specs += [pl.BlockSpec(blk(g), here) for g in gs]
    outs = _pcall(body, name=name,
                  out_shape=[_sds((4, g.shape[1] // 2, g.shape[2])) for g in gs]
                  + [_sds((4, g.shape[1] // 2, g.shape[2]), BF16) for g in gs],
                  grid_spec=pltpu.PrefetchScalarGridSpec(num_scalar_prefetch=1, grid=(4, RS_SPLIT), in_specs=in_specs,
                                                         out_specs=[pl.BlockSpec(blk(g), here) for g in gs] * 2),
                  compiler_params=_cparams(("parallel", "parallel")))(sel.reshape(1).astype(jnp.int32), *gs, *r1s)
    return outs[:n], outs[n:]


def _rs_sum(name, hs, r2s, chip, half):
    n = len(hs)

    def body(sel_ref, *refs):
        h_refs, r_refs, o_refs = (refs[k * n:(k + 1) * n] for k in range(3))
        for i in range(n):
            r = r_refs[i]
            o_refs[i][...] = ((h_refs[i][0] + r[0].astype(F32)) + r[1].astype(F32)) + r[2].astype(F32)

    def rows(h):
        return h.shape[1] // RS_SPLIT

    in_specs = [pl.BlockSpec((1, rows(h), h.shape[2]), lambda j, s: (s[0], j, 0)) for h in hs]
    in_specs += [pl.BlockSpec((3, rows(h), h.shape[2]), lambda j, s: (0, j, 0)) for h in hs]
    sel = jnp.stack([chip, half]).astype(jnp.int32)
    return _pcall(body, name=name, out_shape=[_sds((2 * h.shape[1], h.shape[2])) for h in hs],
                  grid_spec=pltpu.PrefetchScalarGridSpec(
                      num_scalar_prefetch=1, grid=(RS_SPLIT,), in_specs=in_specs,
                      out_specs=[pl.BlockSpec((rows(h), h.shape[2]), lambda j, s: (s[1] * RS_SPLIT + j, 0)) for h in hs]),
                  compiler_params=_cparams(("parallel",)))(sel, *hs, *r2s)


def _sum8(name, a):
    _, r, n = a.shape
    br = _row_block(r)

    def body(a_ref, o_ref):
        acc = a_ref[0]
        for k in range(1, 8):
            acc = acc + a_ref[k]
        o_ref[...] = acc

    return _pcall(body, name=name, out_shape=_sds((r, n)), grid=(r // br,),
                  in_specs=[pl.BlockSpec((8, br, n), lambda j: (0, j, 0))], out_specs=pl.BlockSpec((br, n), lambda j: (j, 0)),
                  compiler_params=_cparams(("parallel",)))(a)


def _adamw(name, ws, gs, ms, vs, nblk):
    n = len(ws)

    def body(*refs):
        w_refs, g_refs, m_refs, v_refs, d_refs, nm_refs, nv_refs = (refs[k * n:(k + 1) * n] for k in range(7))
        for i in range(n):
            gv = g_refs[i][...]
            nm = ADAM_B1 * m_refs[i][...] + (1.0 - ADAM_B1) * gv
            nv = ADAM_B2 * v_refs[i][...] + (1.0 - ADAM_B2) * (gv * gv)
            nm_refs[i][...] = nm
            nv_refs[i][...] = nv
            m_hat = nm / (1.0 - ADAM_B1 ** ADAM_STEP)
            v_hat = nv / (1.0 - ADAM_B2 ** ADAM_STEP)
            d_refs[i][...] = -ADAM_LR * (m_hat / (jnp.sqrt(v_hat) + ADAM_EPS) + ADAM_WD * w_refs[i][...])

    specs = [pl.BlockSpec((w.shape[0] // nblk, w.shape[1]), lambda j: (j, 0)) for w in ws]
    outs = _pcall(body, name=name, out_shape=[_sds(w.shape) for w in ws] * 3, grid=(nblk,), in_specs=specs * 4,
                  out_specs=specs * 3, compiler_params=_cparams(("parallel",)))(*ws, *gs, *ms, *vs)
    return outs[:n], outs[n:2 * n], outs[2 * n:]


ADA_COLS = 2304
ADA_BLK = 768


def _ada_fwd(name, c_all, w_shard, b_cols):
    nb = c_all.shape[0]

    def body(c_ref, w_ref, b_ref, o_ref):
        cv = c_ref[...]
        cs = _mx(cv * _sigmoid(cv))
        o_ref[...] = _dot(cs, _mx(w_ref[...])) + b_ref[...]

    return _pcall(body, name=name, out_shape=_sds((nb, ADA_COLS)), grid=(ADA_COLS // ADA_BLK,),
                  in_specs=[pl.BlockSpec((nb, D), lambda j: (0, 0)), pl.BlockSpec((D, ADA_BLK), lambda j: (0, j)),
                            pl.BlockSpec((1, ADA_BLK), lambda j: (0, j))],
                  out_specs=pl.BlockSpec((nb, ADA_BLK), lambda j: (0, j)),
                  compiler_params=_cparams(("parallel",)))(c_all, w_shard, b_cols)


def _ada_bwd(name, c_all, dmod_cols, dmod_all):
    nb = c_all.shape[0]

    def body(c_ref, dc_ref, da_ref, gw_ref, gb_ref):
        cv = c_ref[...]
        cs = _mx(cv * _sigmoid(cv))
        gw_ref[...] = lax.dot_general(cs, _mx(dc_ref[...]), (((0,), (0,)), ((), ())), preferred_element_type=F32)

        @pl.when(pl.program_id(0) == 0)
        def _():
            gb_ref[...] = jnp.sum(da_ref[...], axis=0, keepdims=True)

    return _pcall(body, name=name, out_shape=[_sds((D, ADA_COLS)), _sds((1, 9 * D))], grid=(ADA_COLS // ADA_BLK,),
                  in_specs=[pl.BlockSpec((nb, D), lambda j: (0, 0)), pl.BlockSpec((nb, ADA_BLK), lambda j: (0, j)),
                            pl.BlockSpec((nb, 9 * D), lambda j: (0, 0))],
                  out_specs=[pl.BlockSpec((D, ADA_BLK), lambda j: (0, j)), pl.BlockSpec((1, 9 * D), lambda j: (0, 0))],
                  compiler_params=_cparams(("arbitrary",)))(c_all, dmod_cols, dmod_all)


BIG = ("ffn1_w1", "ffn1_w3", "ffn1_w2", "w_in", "w_glu", "w_out", "ffn2_w1", "ffn2_w3", "ffn2_w2")
COL_SHARDED = ("w_in",)
TRANSPOSED = ("ffn1_w1", "ffn1_w3", "ffn2_w1", "ffn2_w3")
SMALL = ("b_ada", "ln1_g", "ln1_b", "conv_w", "conv_b", "dt_bias", "a_log", "d_ssd", "ssd_norm_w", "s5_a_re", "s5_a_im",
         "s5_log_dt", "s5_b_re", "s5_b_im", "s5_c_re", "s5_c_im", "s5_d", "b_glu", "ln2_g", "ln2_b", "ln3_g", "ln3_b")
WEIGHTS = ("w_ada", "b_ada", "ffn1_w1", "ffn1_w3", "ffn1_w2", "ln1_g", "ln1_b", "w_in", "conv_w", "conv_b", "dt_bias",
           "a_log", "d_ssd", "ssd_norm_w", "s5_a_re", "s5_a_im", "s5_log_dt", "s5_b_re", "s5_b_im", "s5_c_re", "s5_c_im",
           "s5_d", "w_glu", "b_glu", "w_out", "ln2_g", "ln2_b", "ffn2_w1", "ffn2_w3", "ffn2_w2", "ln3_g", "ln3_b")
BIG_PAD = 2 * 1024 * 128


def _pack(arrs, mult, axis_keep=0):
    lead = arrs[0].shape[:axis_keep]
    flat = jnp.concatenate([a.reshape(lead + (-1,)) for a in arrs], axis=axis_keep)
    pad = (-flat.shape[-1]) % mult
    if pad:
        flat = jnp.concatenate([flat, jnp.zeros(lead + (pad,), flat.dtype)], axis=axis_keep)
    return flat


def _unpack(flat, shapes):
    out, off = [], 0
    for s in shapes:
        size = math.prod(s)
        out.append(flat[..., off:off + size].reshape(flat.shape[:-1] + tuple(s)))
        off += size
    return out


def _shard_major(a):
    rows, cols = a.shape
    return a.reshape(rows, 4, cols // 4).transpose(1, 0, 2)


def _from_shard_major(a):
    _, rows, w = a.shape
    return a.transpose(1, 0, 2).reshape(rows, 4 * w)


def kernel(x, c, w_ada, b_ada, ffn1_w1, ffn1_w3, ffn1_w2, ln1_g, ln1_b, w_in, conv_w, conv_b, dt_bias, a_log, d_ssd, ssd_norm_w, s5_a_re, s5_a_im, s5_log_dt, s5_b_re, s5_b_im, s5_c_re, s5_c_im, s5_d, w_glu, b_glu, w_out, ln2_g, ln2_b, ffn2_w1, ffn2_w3, ffn2_w2, ln3_g, ln3_b, loss_target, m_w_ada, m_b_ada, m_ffn1_w1, m_ffn1_w3, m_ffn1_w2, m_ln1_g, m_ln1_b, m_w_in, m_conv_w, m_conv_b, m_dt_bias, m_a_log, m_d_ssd, m_ssd_norm_w, m_s5_a_re, m_s5_a_im, m_s5_log_dt, m_s5_b_re, m_s5_b_im, m_s5_c_re, m_s5_c_im, m_s5_d, m_w_glu, m_b_glu, m_w_out, m_ln2_g, m_ln2_b, m_ffn2_w1, m_ffn2_w3, m_ffn2_w2, m_ln3_g, m_ln3_b, v_w_ada, v_b_ada, v_ffn1_w1, v_ffn1_w3, v_ffn1_w2, v_ln1_g, v_ln1_b, v_w_in, v_conv_w, v_conv_b, v_dt_bias, v_a_log, v_d_ssd, v_ssd_norm_w, v_s5_a_re, v_s5_a_im, v_s5_log_dt, v_s5_b_re, v_s5_b_im, v_s5_c_re, v_s5_c_im, v_s5_d, v_w_glu, v_b_glu, v_w_out, v_ln2_g, v_ln2_b, v_ffn2_w1, v_ffn2_w3, v_ffn2_w2, v_ln3_g, v_ln3_b):
    a = dict(locals())
    xi, yi, ci = _place()
    chip = 2 * xi + yi
    dev = 2 * chip + ci
    nb, seq, _ = x.shape
    t = nb * seq
    ndev = 8

    c_rows = nb * D // 128
    c_cw = _allgather8("gather_c", jnp.concatenate([c.reshape(c_rows, 128), conv_w.reshape(-1, 128)], axis=0))
    c_all = c_cw[:, :c_rows].reshape(ndev * nb, D)
    b_cols = lax.dynamic_slice(b_ada, (0, chip * ADA_COLS), (1, ADA_COLS))
    mod_part = _ada_fwd("ada_fwd", c_all, w_ada[0], b_cols)
    mod_parts = _allgather8("gather_mod", mod_part.reshape(-1, 128)).reshape(ndev, ndev * nb, ADA_COLS)
    mod_all = mod_parts[0::2].transpose(1, 0, 2).reshape(ndev * nb, 9 * D)
    mod = lax.dynamic_slice(mod_all, (nb * dev, 0), (nb, 9 * D)).reshape(nb, 9, D)

    def nat(n):
        return jnp.swapaxes(a[n], 1, 2)[0] if n[-7:] in TRANSPOSED else a[n][0]

    def gather(names):
        own = [nat(n).astype(MXU_DTYPE) for n in names]

        def weights(pieces):
            w = {}
            for n, mine, piece in zip(names, own, pieces):
                piece = lax.dynamic_update_slice(piece, mine[None], (chip, 0, 0))
                if n in COL_SHARDED:
                    wi = _from_shard_major(piece)
                    w[n] = jnp.concatenate([wi[:, :1536], wi[:, 1544:2056], wi[:, 1536:1544],
                                            jnp.zeros((D, 120), wi.dtype)], axis=1)
                else:
                    w[n + "t" if n in TRANSPOSED else n] = piece.reshape(-1, piece.shape[-1])
            for n in names:
                have, want = (n + "t", n) if n in TRANSPOSED else (n, n + "t")
                w[want] = w[have].T
            return w

        return _gather_stage(own), weights

    first_stage, first_weights = gather(("ffn1_w1", "ffn1_w3"))
    w = first_weights(_run_stage("gather_w_first", first_stage))

    conv_full = _from_shard_major(c_cw[0::2, c_rows:].reshape(4, 4, 256))
    pad8 = lambda v: jnp.concatenate([v.reshape(1, NH), jnp.zeros((1, 128 - NH), F32)], axis=1)
    sp = dict(ln1_g=ln1_g, ln1_b=ln1_b, ln2_g=ln2_g, ln2_b=ln2_b, ln3_g=ln3_g, ln3_b=ln3_b, conv_w=conv_full,
              conv_b=conv_b, dt_bias=pad8(dt_bias), a_log=pad8(a_log), d_rep=jnp.repeat(d_ssd[0], HP)[None],
              ssd_norm_w=ssd_norm_w, s5_a_re=s5_a_re[0], s5_a_im=s5_a_im[0], s5_log_dt=s5_log_dt[0], s5_b_re=s5_b_re[0],
              s5_b_im=s5_b_im[0], s5_c_re=s5_c_re[0], s5_c_im=s5_c_im[0], s5_d=s5_d, b_glu=b_glu)

    lsum, dx0, dmod, gbig, small = _local_step(x.reshape(t, D), loss_target.reshape(t, D), mod, w, sp, seq,
                                               dist=(gather(("ffn1_w2", "w_in", "w_glu", "w_out")),
                                                     gather(("ffn2_w1", "ffn2_w3", "ffn2_w2")), (ci, chip)))
    loss = lax.psum(lsum * (0.5 / D), ("x", "y", "c"))

    dmod_all = _allgather8("gather_dmod", dmod.reshape(-1, 128)).reshape(ndev * nb, 9 * D)
    dmod_cols = lax.dynamic_slice(dmod_all, (0, chip * ADA_COLS), (ndev * nb, ADA_COLS))
    g_w_ada, g_b_ada = _ada_bwd("ada_bwd", c_all, dmod_cols, dmod_all)

    gbig["w_ada"] = g_w_ada

    outs = {}
    for call, names in (("adamw_a", ("ffn1_w1", "ffn1_w3", "ffn1_w2", "w_in", "w_glu", "w_out")),
                        ("adamw_b", ("ffn2_w1", "ffn2_w3", "ffn2_w2", "w_ada"))):
        res = _adamw(call, [nat(n) for n in names], [gbig[n] for n in names], [nat("m_" + n) for n in names],
                     [nat("v_" + n) for n in names], 8)
        for kind, arrs in zip(("grad", "delta", "new_m", "new_v"), ([gbig[n] for n in names],) + tuple(res)):
            for n, arr in zip(names, arrs):
                outs[kind, n] = (arr.T if n in TRANSPOSED else arr)[None]

    snames = [n for n in SMALL if n != "b_ada"]
    sgrad = dict(small)
    sgrad["b_glu"] = small["w_glu_b"]
    svec = _pack([sgrad[n] for n in snames], 1024).reshape(-1, 128)
    ssum = _sum8("small_sum", _allgather8("gather_small", svec)).reshape(-1)

    def view2d(u):
        s = u.shape[1:]
        return u.reshape((1, s[0]) if len(s) == 1 else (-1, s[-1]))

    vshape = {n: view2d(a[n]).shape for n in SMALL}
    gsm = dict(zip(snames, _unpack(ssum, [vshape[n] if n != "conv_w" else (4, D) for n in snames])))
    gsm["conv_w"] = lax.dynamic_slice(gsm["conv_w"], (0, chip * 256), (4, 256))
    gsm["b_ada"] = g_b_ada
    res = _adamw("adamw_small", [view2d(a[n]) for n in SMALL], [gsm[n] for n in SMALL],
                 [view2d(a["m_" + n]) for n in SMALL], [view2d(a["v_" + n]) for n in SMALL], 1)
    for kind, arrs in zip(("grad", "delta", "new_m", "new_v"), ([gsm[n] for n in SMALL],) + tuple(res)):
        for n, arr in zip(SMALL, arrs):
            outs[kind, n] = arr.reshape(a[n].shape)

    res = [loss, dx0.reshape(nb, seq, D)]
    for kind in ("grad", "delta", "new_m", "new_v"):
        res += [outs[kind, n] for n in WEIGHTS]
    return tuple(res)
```

```python
import functools
import math

import jax
import jax.numpy as jnp
from jax import lax
from jax.experimental import pallas as pl
from jax.experimental.pallas import tpu as pltpu

F32 = jnp.float32
BF16 = jnp.bfloat16
MXU_DTYPE = jnp.bfloat16

D = 1024
FF = 2816
FB = 1408
NH, HP, NS, NG = 8, 64, 128, 2
CH = 128
SW = 512
S5G, S5P, S5H = 32, 64, 16
S5L = S5G * S5P
PW = 2176
ALPHA = 2.0 ** 0.25
LN_EPS = 1e-5
ADAM_LR, ADAM_B1, ADAM_B2, ADAM_EPS, ADAM_WD, ADAM_STEP = 0.001, 0.9, 0.999, 1e-08, 0.01, 10
VMEM_LIMIT = 56 * 1024 * 1024
TM_WIDE = 512
MESH_T = pl.DeviceIdType.MESH


def _pcall(body, **kw):
    return pl.pallas_call(body, **kw)


def _cparams(sem=None, **kw):
    return pltpu.CompilerParams(dimension_semantics=sem, vmem_limit_bytes=VMEM_LIMIT, **kw)


def _dot(a, b):
    return jnp.dot(a, b, preferred_element_type=F32)


def _dot_nt(a, b):
    return lax.dot_general(a, b, (((1,), (1,)), ((), ())), preferred_element_type=F32)


def _dot_hi(a, b):
    return jnp.dot(a, b, preferred_element_type=F32, precision=lax.Precision.HIGHEST)


def _mx(a):
    return a.astype(MXU_DTYPE)


def _sigmoid(x):
    return 1.0 / (1.0 + jnp.exp(-x))


def _iota(shape, axis):
    return lax.broadcasted_iota(jnp.int32, shape, axis)


def _rowcall(name, fn, n_rows, tm, tpe, *, tiled=(), halos=(), exs=(), res=(), out_tiled=(), out_acc=(),
             out_exacc=(), scratch=(), reverse=False, batch=None, carry=None, carry_mid=0.0):
    if batch:
        n_rows //= batch
    nt = n_rows // tm

    def blk(i):
        return (nt - 1 - i) if reverse else i

    in_specs, args = [], []
    for a in tiled:
        if batch:
            in_specs.append(pl.BlockSpec((batch, tm, a.shape[1]), lambda i: (0, blk(i), 0)))
            args.append(a.reshape(batch, n_rows, a.shape[1]))
            continue
        in_specs.append(pl.BlockSpec((tm, a.shape[1]), lambda i: (blk(i), 0)))
        args.append(a)
    for a, rows in halos:
        r = tm // rows
        if batch:
            in_specs.append(pl.BlockSpec((batch, rows, a.shape[1]), lambda i, r=r: (0, jnp.maximum(blk(i) * r - 1, 0), 0)))
            args.append(a.reshape(batch, n_rows, a.shape[1]))
            continue
        in_specs.append(pl.BlockSpec((rows, a.shape[1]), lambda i, r=r: (jnp.maximum(blk(i) * r - 1, 0), 0)))
        args.append(a)
    for a in exs:
        in_specs.append(pl.BlockSpec((1,) + a.shape[1:], lambda i: (blk(i) // tpe, 0, 0)))
        args.append(a)
    for a in res:
        nd = a.ndim
        in_specs.append(pl.BlockSpec(a.shape, lambda i, nd=nd: (0,) * nd, pipeline_mode=pl.Buffered(1)))
        args.append(a)
    any_spec = pl.BlockSpec(memory_space=pl.ANY)
    st_ins = carry.ins if carry else []
    st_outs = carry.out_shapes if carry else []
    st_sems = carry.sem_shapes() if carry else []
    base_in = len(args)
    in_specs += [any_spec] * len(st_ins)
    args += st_ins
    out_specs, out_shape = [], []
    for s in out_tiled:
        if batch:
            out_specs.append(pl.BlockSpec((batch, tm, s.shape[1]), lambda i: (0, blk(i), 0)))
            out_shape.append(_sds((batch, n_rows, s.shape[1]), s.dtype))
            continue
        out_specs.append(pl.BlockSpec((tm, s.shape[1]), lambda i: (blk(i), 0)))
        out_shape.append(s)
    for s in out_acc:
        nd = len(s.shape)
        out_specs.append(pl.BlockSpec(s.shape, lambda i, nd=nd: (0,) * nd))
        out_shape.append(s)
    for s in out_exacc:
        out_specs.append(pl.BlockSpec((1,) + s.shape[1:], lambda i: (blk(i) // tpe, 0, 0)))
        out_shape.append(s)
    base_out = len(out_shape)
    out_specs += [any_spec] * len(st_outs)
    out_shape += st_outs
    aliases = {base_in + k: base_out + v for k, v in carry.aliases.items()} if carry else {}
    n = [len(tiled), len(halos), len(exs), len(res), len(st_ins), len(out_tiled), len(out_acc), len(out_exacc),
         len(st_outs), len(scratch), len(st_sems)]

    def body(*refs):
        groups, k = [], 0
        for m in n:
            groups.append(refs[k:k + m])
            k += m
        i = pl.program_id(0)
        b = blk(i)

        class ctx:
            first = i == 0
            pos = b % tpe
            seq_first = (b % tpe) == 0
            seq_last = (b % tpe) == tpe - 1
            ex_enter = (i % tpe) == 0

        def work():
            for cond, refs_ in ((ctx.first, groups[6]), (ctx.ex_enter, groups[7])):
                if refs_:
                    @pl.when(cond)
                    def _():
                        for r in refs_:
                            r[...] = jnp.zeros(r.shape, r.dtype)
            fn(ctx, *groups[0:4], *groups[5:8], groups[9])

        if carry:
            carry.run((groups[4], groups[8], groups[10]), i == 0, i == min(nt - 1, int(carry_mid * nt)), i == nt - 1, work)
        else:
            work()

    outs = _pcall(body, name=name, grid=(nt,), in_specs=in_specs, out_specs=out_specs, out_shape=out_shape,
                  input_output_aliases=aliases, scratch_shapes=list(scratch) + st_sems,
                  compiler_params=_cparams(("arbitrary",)))(*args)
    if batch:
        outs = [o.reshape(batch * n_rows, o.shape[2]) if k < len(out_tiled) else o for k, o in enumerate(outs)]
    return (outs[:base_out], outs[base_out:]) if carry else outs


def _acc(ref, val):
    ref[...] += val


def _sds(shape, dtype=F32):
    return jax.ShapeDtypeStruct(shape, dtype)


def _ln_fwd(r, g, b):
    mu = jnp.mean(r, axis=-1, keepdims=True)
    rc = r - mu
    var = jnp.mean(rc * rc, axis=-1, keepdims=True)
    return rc * lax.rsqrt(var + LN_EPS) * g + b


def _ln_bwd(r, g, dy):
    mu = jnp.mean(r, axis=-1, keepdims=True)
    rc = r - mu
    var = jnp.mean(rc * rc, axis=-1, keepdims=True)
    rstd = lax.rsqrt(var + LN_EPS)
    xhat = rc * rstd
    dxh = dy * g
    dr = rstd * (dxh - jnp.mean(dxh, axis=-1, keepdims=True) - xhat * jnp.mean(dxh * xhat, axis=-1, keepdims=True))
    return dr, jnp.sum(dy * xhat, axis=0, keepdims=True), jnp.sum(dy, axis=0, keepdims=True)


def _ffn_fwd_hidden(name, x, mod, k0, w1, w3, seq, **carry):
    t = x.shape[0]
    tm = min(TM_WIDE, seq)

    def fn(ctx, tiled, halos, exs, res, outs, accs, exaccs, scr):
        mod_ref, = exs
        w1_ref, w3_ref = res
        h_ref, ab_ref, s_ref = outs
        sh, sc = mod_ref[0, k0:k0 + 1, :], mod_ref[0, k0 + 1:k0 + 2, :]
        h = _mx(tiled[0][...] * (1.0 + sc) + sh)
        h_ref[...] = h
        for j in range(2):
            a = _dot_nt(h, w1_ref[j * FB:(j + 1) * FB, :])
            b = _dot_nt(h, w3_ref[j * FB:(j + 1) * FB, :])
            ab_ref[:, j * FB:(j + 1) * FB] = _mx(a)
            ab_ref[:, FF + j * FB:FF + (j + 1) * FB] = _mx(b)
            s_ref[:, j * FB:(j + 1) * FB] = _mx(a * _sigmoid(a) * b)

    return _rowcall(name, fn, t, tm, seq // tm, tiled=[x], exs=[mod], res=[w1, w3],
                    out_tiled=[_sds((t, D), MXU_DTYPE), _sds((t, 2 * FF), MXU_DTYPE), _sds((t, FF), MXU_DTYPE)], **carry)


def _ffn_fwd_out(name, s, x, mod, k0, w2, lng, lnb, seq, tgt=None, **carry):
    t = x.shape[0]
    tm = min(TM_WIDE, seq)
    with_loss = tgt is not None

    def fn(ctx, tiled, halos, exs, res, outs, accs, exaccs, scr):
        mod_ref, = exs
        w2_ref, g_ref, b_ref = res
        xo_ref, r_ref, f_ref = outs[:3]
        f = _dot(tiled[0][...], w2_ref[...])
        f_ref[...] = f
        r = ALPHA * tiled[1][...] + 0.5 * mod_ref[0, k0 + 2:k0 + 3, :] * f
        r_ref[...] = r
        xo = _ln_fwd(r, g_ref[...], b_ref[...])
        xo_ref[...] = xo
        if with_loss:
            e = xo - tiled[2][...]
            outs[3][...] = e * (1.0 / D)
            _acc(accs[0], jnp.sum(e * e) * jnp.ones((8, 128), F32))

    return _rowcall(name, fn, t, tm, seq // tm, tiled=[s, x] + ([tgt] if with_loss else []), exs=[mod],
                    res=[w2, lng, lnb], out_tiled=[_sds((t, D))] * (4 if with_loss else 3),
                    out_acc=[_sds((8, 128))] if with_loss else [], **carry)


def _ffn_bwd1(name, dxo, r, ab, f, mod, k0, lng, w2, seq, **carry):
    t = dxo.shape[0]
    tm = 256

    def fn(ctx, tiled, halos, exs, res, outs, accs, exaccs, scr):
        dxo_ref, r_ref, ab_ref, f_ref = tiled
        mod_ref, = exs
        g_ref, w2_ref = res
        dr_ref, df_ref, dab_ref = outs
        g = mod_ref[0, k0 + 2:k0 + 3, :]
        dr, dgam, dbet = _ln_bwd(r_ref[...], g_ref[...], dxo_ref[...])
        dr_ref[...] = dr
        _acc(accs[0], dgam)
        _acc(accs[1], dbet)
        _acc(exaccs[0].at[0], jnp.sum(0.5 * f_ref[...] * dr, axis=0, keepdims=True))
        df = _mx(0.5 * g * dr)
        df_ref[...] = df
        for j in range(2):
            ds = _dot_nt(df, w2_ref[j * FB:(j + 1) * FB, :])
            a = ab_ref[:, j * FB:(j + 1) * FB].astype(F32)
            b = ab_ref[:, FF + j * FB:FF + (j + 1) * FB].astype(F32)
            sig = _sigmoid(a)
            dab_ref[:, j * FB:(j + 1) * FB] = _mx(ds * b * (sig * (1.0 + a * (1.0 - sig))))
            dab_ref[:, FF + j * FB:FF + (j + 1) * FB] = _mx(ds * (a * sig))

    b = mod.shape[0]
    return _rowcall(name, fn, t, tm, seq // tm, tiled=[dxo, r, ab, f], exs=[mod], res=[lng, w2],
                    out_tiled=[_sds((t, D)), _sds((t, D), MXU_DTYPE), _sds((t, 2 * FF), MXU_DTYPE)],
                    out_acc=[_sds((1, D)), _sds((1, D))], out_exacc=[_sds((b, 1, D))], **carry)


def _ffn_bwd(name, dxo, r, ab, f, x, mod, k0, lng, w2, w1t, w3t, seq):
    t = dxo.shape[0]
    tm = 256

    def fn(ctx, tiled, halos, exs, res, outs, accs, exaccs, scr):
        dxo_ref, r_ref, ab_ref, f_ref, x_ref = tiled
        mod_ref, = exs
        g_ref, w2_ref, w1t_ref, w3t_ref = res
        dx_ref, df_ref, dab_ref = outs
        sc, g = mod_ref[0, k0 + 1:k0 + 2, :], mod_ref[0, k0 + 2:k0 + 3, :]
        dr, dgam, dbet = _ln_bwd(r_ref[...], g_ref[...], dxo_ref[...])
        _acc(accs[0], dgam)
        _acc(accs[1], dbet)
        _acc(exaccs[0].at[0], jnp.sum(0.5 * f_ref[...] * dr, axis=0, keepdims=True))
        df = _mx(0.5 * g * dr)
        df_ref[...] = df
        dh = jnp.zeros((tm, D), F32)
        for j in range(2):
            blk = slice(j * FB, (j + 1) * FB)
            ds = _dot_nt(df, w2_ref[blk, :])
            a = ab_ref[:, blk].astype(F32)
            b = ab_ref[:, FF + j * FB:FF + (j + 1) * FB].astype(F32)
            sig = _sigmoid(a)
            da = _mx(ds * b * (sig * (1.0 + a * (1.0 - sig))))
            db = _mx(ds * (a * sig))
            dab_ref[:, blk] = da
            dab_ref[:, FF + j * FB:FF + (j + 1) * FB] = db
            dh = dh + _dot(da, w1t_ref[blk, :]) + _dot(db, w3t_ref[blk, :])
        dx_ref[...] = ALPHA * dr + dh * (1.0 + sc)
        _acc(exaccs[1].at[0], jnp.sum(dh, axis=0, keepdims=True))
        _acc(exaccs[2].at[0], jnp.sum(dh * x_ref[...], axis=0, keepdims=True))

    b = mod.shape[0]
    return _rowcall(name, fn, t, tm, seq // tm, tiled=[dxo, r, ab, f, x], exs=[mod], res=[lng, w2, w1t, w3t],
                    out_tiled=[_sds((t, D)), _sds((t, D), MXU_DTYPE), _sds((t, 2 * FF), MXU_DTYPE)],
                    out_acc=[_sds((1, D)), _sds((1, D))], out_exacc=[_sds((b, 1, D))] * 3)


def _mod_bwd(name, dab, dr, x, mod, k0, wts, seq, extra=(), nt=False, **carry):
    t = dr.shape[0]
    tm = min(TM_WIDE, seq)
    nin = 1 + len(extra)
    width = dab.shape[1] + sum(e.shape[1] for e in extra)

    def fn(ctx, tiled, halos, exs, res, outs, accs, exaccs, scr):
        parts = tiled[:nin]
        dr_ref, x_ref = tiled[nin:]
        mod_ref, = exs
        sc = mod_ref[0, k0 + 1:k0 + 2, :]
        if nin == 1:
            dp = parts[0][...]
        else:
            dp = jnp.concatenate([_mx(p[...]) for p in parts], axis=1)
            outs[1][...] = dp
        dh, off = jnp.zeros((tm, D), F32), 0
        for w_ref in res:
            k = w_ref.shape[1 if nt else 0]
            dh = dh + (_dot_nt if nt else _dot)(dp[:, off:off + k], w_ref[...])
            off += k
        outs[0][...] = ALPHA * dr_ref[...] + dh * (1.0 + sc)
        _acc(exaccs[0].at[0], jnp.sum(dh, axis=0, keepdims=True))
        _acc(exaccs[1].at[0], jnp.sum(dh * x_ref[...], axis=0, keepdims=True))

    b = mod.shape[0]
    out_tiled = [_sds((t, D))] + ([_sds((t, width), MXU_DTYPE)] if nin > 1 else [])
    return _rowcall(name, fn, t, tm, seq // tm, tiled=[dab, *extra, dr, x], exs=[mod], res=list(wts),
                    out_tiled=out_tiled, out_exacc=[_sds((b, 1, D)), _sds((b, 1, D))], **carry)


def _tn_matmul(name, a, b, bm, bn, bt=1024, carry=None, a_cols=None):
    t = a.shape[0]
    bt = min(bt, t)
    start, m = a_cols or (0, a.shape[1])
    off = start // bm
    n = b.shape[1]
    grid = (m // bm, n // bn, t // bt)
    n_in, n_out = (len(carry.ins), len(carry.out_shapes)) if carry else (0, 0)

    def body(a_ref, b_ref, *refs):
        o_ref = refs[n_in]

        def work():
            @pl.when(pl.program_id(2) == 0)
            def _():
                o_ref[...] = jnp.zeros(o_ref.shape, F32)
            o_ref[...] += lax.dot_general(a_ref[...], b_ref[...], (((0,), (0,)), ((), ())), preferred_element_type=F32)

        if not carry:
            return work()
        step = (pl.program_id(0) * grid[1] + pl.program_id(1)) * grid[2] + pl.program_id(2)
        carry.run((refs[:n_in], refs[n_in + 1:n_in + 1 + n_out], refs[n_in + 1 + n_out:]), step == 0, step == 0,
                  step == grid[0] * grid[1] * grid[2] - 1, work)

    any_spec = pl.BlockSpec(memory_space=pl.ANY)
    outs = _pcall(body, name=name, grid=grid,
                  in_specs=[pl.BlockSpec((bt, bm), lambda i, j, k: (k, i + off)), pl.BlockSpec((bt, bn), lambda i, j, k: (k, j))]
                  + [any_spec] * n_in,
                  out_specs=[pl.BlockSpec((bm, bn), lambda i, j, k: (i, j))] + [any_spec] * n_out,
                  out_shape=[_sds((m, n))] + (carry.out_shapes if carry else []),
                  input_output_aliases={2 + k: 1 + v for k, v in carry.aliases.items()} if carry else {},
                  scratch_shapes=carry.sem_shapes() if carry else [],
                  compiler_params=_cparams(("arbitrary",) * 3 if carry else ("parallel", "parallel", "arbitrary")))(
                      a, b, *(carry.ins if carry else []))
    return (outs[0], outs[1:]) if carry else outs[0]


S5_BLK = 16


def _tn_diag(name, a, b, bt=1024):
    t = a.shape[0]
    bt = min(bt, t)
    rows, cols = S5_BLK * S5P, S5_BLK * S5H
    nblk = a.shape[1] // rows

    def body(a_ref, b_ref, o_ref):
        @pl.when(pl.program_id(1) == 0)
        def _():
            o_ref[...] = jnp.zeros(o_ref.shape, F32)
        o_ref[0] += lax.dot_general(a_ref[...], b_ref[...], (((0,), (0,)), ((), ())), preferred_element_type=F32)

    return _pcall(body, name=name, grid=(nblk, t // bt),
                  in_specs=[pl.BlockSpec((bt, rows), lambda i, k: (k, i)),
                            pl.BlockSpec((bt, cols), lambda i, k: (k, i % (S5G // S5_BLK)))],
                  out_specs=pl.BlockSpec((1, rows, cols), lambda i, k: (i, 0, 0)), out_shape=_sds((nblk, rows, cols)),
                  compiler_params=_cparams(("parallel", "arbitrary")))(a, b)


def _diag_groups(o):
    o = o.reshape(2, S5G // S5_BLK, S5_BLK, S5P, S5_BLK, S5H)
    return jnp.einsum("rbgpgh->rbgph", o).reshape(2, S5G, S5P, S5H)


def _proj_fwd(name, x, mod, w_in, seq):
    t = x.shape[0]
    tm = min(TM_WIDE, seq)

    def fn(ctx, tiled, halos, exs, res, outs, accs, exaccs, scr):
        mod_ref, = exs
        sh, sc = mod_ref[0, 3:4, :], mod_ref[0, 4:5, :]
        h = _mx(tiled[0][...] * (1.0 + sc) + sh)
        outs[0][...] = h
        outs[1][...] = _dot(h, res[0][...])

    return _rowcall(name, fn, t, tm, seq // tm, tiled=[x], exs=[mod], res=[w_in],
                    out_tiled=[_sds((t, D), MXU_DTYPE), _sds((t, PW))])


def _shift_rows(cur, prev8, j):
    if j == 0:
        return cur
    rolled = pltpu.roll(cur, j, 0)
    top = jnp.where(_iota((8, cur.shape[1]), 0) < j, pltpu.roll(prev8, j, 0), rolled[0:8])
    return jnp.concatenate([top, rolled[8:]], axis=0)


def _shift_rows_up(cur, next8, j):
    if j == 0:
        return cur
    n = cur.shape[0]
    rolled = pltpu.roll(cur, n - j, 0)
    bot = jnp.where(_iota((8, cur.shape[1]), 0) >= 8 - j, pltpu.roll(next8, 8 - j, 0), rolled[n - 8:n])
    return jnp.concatenate([rolled[:n - 8], bot], axis=0)


def _softplus(x):
    return jnp.maximum(x, 0.0) + jnp.log(1.0 + jnp.exp(-jnp.abs(x)))


def _ssd_common(proj_ref, xpre, dtb_ref, alog_ref):
    xbc = xpre * _sigmoid(xpre)
    xs, bm, cm = xbc[:, 0:SW], xbc[:, SW:SW + 256], xbc[:, SW + 256:SW + 512]
    dtraw = proj_ref[:, PW - 128:PW] + dtb_ref[...]
    dt = _softplus(dtraw)
    a = -jnp.exp(alog_ref[...])
    tril = (_iota((CH, CH), 0) >= _iota((CH, CH), 1)).astype(F32)
    acs = _dot_hi(tril, dt * a)
    return xs, bm, cm, dtraw, dt, a, acs, acs.T


def _pair_lane(lo, hi):
    r = lo.shape[0]
    return jnp.where(_iota((r, 128), 1) < HP, lo, hi)


def _ssd_fwd(name, proj, conv_w, conv_b, dt_bias, a_log, d_rep, norm_w, seq):
    t = proj.shape[0]
    nb = t // seq

    def fn(ctx, tiled, halos, exs, res, outs, accs, exaccs, scr):
        @pl.when(ctx.seq_first)
        def _():
            scr[0][...] = jnp.zeros(scr[0].shape, F32)

        for b in range(nb):
            one(ctx, res, [r.at[b] for r in tiled + halos + outs + scr])

    def one(ctx, res, refs):
        proj_ref, halo_ref, yo_ref, xpre_ref, y_ref, sprev_ref, state_ref = refs
        cw_ref, cb_ref, dtb_ref, alog_ref, d_ref, nw_ref = res
        raw = proj_ref[:, SW:SW + D]
        prev8 = halo_ref[:, SW:SW + D] * jnp.where(ctx.seq_first, 0.0, 1.0)
        xpre = cb_ref[...] + cw_ref[3:4, :] * raw
        for j in (1, 2, 3):
            xpre = xpre + cw_ref[3 - j:4 - j, :] * _shift_rows(raw, prev8, j)
        xpre_ref[...] = xpre
        xs, bm, cm, dtraw, dt, a, acs, acst = _ssd_common(proj_ref, xpre, dtb_ref, alog_ref)
        causal = _iota((CH, CH), 0) >= _iota((CH, CH), 1)
        lane_lo = _iota((CH, 128), 1) < HP
        sprev = state_ref[...]
        sprev_ref[...] = sprev
        ys = []
        for g in range(NG):
            bmg, cmg = bm[:, g * NS:(g + 1) * NS], cm[:, g * NS:(g + 1) * NS]
            bmt = bmg.T
            cb = _dot(_mx(cmg), _mx(bmt))
            for q in (2 * g, 2 * g + 1):
                xsq = xs[:, 128 * q:128 * q + 128]
                xd = xsq * _pair_lane(dt[:, 2 * q:2 * q + 1], dt[:, 2 * q + 1:2 * q + 2])
                sp = sprev[:, 128 * q:128 * q + 128]
                ydiag = jnp.zeros((CH, 128), F32)
                snew = jnp.zeros((NS, 128), F32)
                for jj in range(2):
                    h = 2 * q + jj
                    col, row = acs[:, h:h + 1], acst[h:h + 1, :]
                    lm = jnp.where(causal, jnp.exp(jnp.minimum(col - row, 0.0)), 0.0)
                    xm = _mx(jnp.where(lane_lo if jj == 0 else ~lane_lo, xd, 0.0))
                    ydiag = ydiag + _dot(_mx(cb * lm), xm)
                    dec_row = jnp.exp(acst[h:h + 1, CH - 1:CH] - row)
                    snew = snew + _dot(_mx(bmt * dec_row), xm)
                e_pair = jnp.exp(_pair_lane(acs[:, 2 * q:2 * q + 1], acs[:, 2 * q + 1:2 * q + 2]))
                yoff = _dot(_mx(cmg), _mx(sp)) * e_pair
                cd = jnp.exp(_pair_lane(acst[2 * q:2 * q + 1, CH - 1:CH], acst[2 * q + 1:2 * q + 2, CH - 1:CH]))
                state_ref[:, 128 * q:128 * q + 128] = cd * sp + snew
                ys.append(ydiag + yoff + d_ref[:, 128 * q:128 * q + 128] * xsq)
        y = jnp.concatenate(ys, axis=1)
        y_ref[...] = y
        z = proj_ref[:, 0:SW]
        yz = y * (z * _sigmoid(z))
        outp = []
        for g in range(NG):
            seg = yz[:, 256 * g:256 * g + 256]
            rinv = lax.rsqrt(jnp.mean(seg * seg, axis=-1, keepdims=True) + LN_EPS)
            outp.append(seg * rinv * nw_ref[:, 256 * g:256 * g + 256])
        yo_ref[...] = _mx(jnp.concatenate(outp, axis=1))

    return _rowcall(name, fn, t, CH, seq // CH, tiled=[proj], halos=[(proj, 8)],
                    res=[conv_w, conv_b, dt_bias, a_log, d_rep, norm_w],
                    out_tiled=[_sds((t, SW), MXU_DTYPE), _sds((t, D)), _sds((t, SW)), _sds((t, SW))],
                    scratch=[pltpu.VMEM((nb, NS, SW), F32)], batch=nb)


def _ssd_bwd(name, dyo, proj, xpre_all, y_all, sprev_all, conv_w, dt_bias, a_log, d_rep, norm_w, seq, **carry):
    t = proj.shape[0]
    nb = t // seq

    def fn(ctx, tiled, halos, exs, res, outs, accs, exaccs, scr):
        @pl.when(ctx.seq_last)
        def _():
            for r in scr:
                r[...] = jnp.zeros(r.shape, F32)

        for b in range(nb):
            one(ctx, res, accs, [r.at[b] for r in tiled + halos + outs + scr])

    def one(ctx, res, accs, refs):
        dyo_ref, proj_ref, xpre_ref, y_ref, sprev_ref, halo_ref, dzx_ref, ddt_ref, ds_ref, nxt_ref = refs
        cw_ref, dtb_ref, alog_ref, d_ref, nw_ref = res
        dnw_acc, dd_acc, dcw_acc, dcb_acc, ddtb_acc, dalog_acc = accs
        xpre = xpre_ref[...]
        xs, bm, cm, dtraw, dt, a, acs, acst = _ssd_common(proj_ref, xpre, dtb_ref, alog_ref)
        y = y_ref[...]
        z = proj_ref[:, 0:SW]
        sz = _sigmoid(z)
        siluz = z * sz
        yz = y * siluz
        dyo = dyo_ref[...]
        dyz_parts, dnw_parts = [], []
        for g in range(NG):
            seg = yz[:, 256 * g:256 * g + 256]
            rinv = lax.rsqrt(jnp.mean(seg * seg, axis=-1, keepdims=True) + LN_EPS)
            yn = seg * rinv
            dseg = dyo[:, 256 * g:256 * g + 256]
            dnw_parts.append(jnp.sum(dseg * yn, axis=0, keepdims=True))
            dyn = dseg * nw_ref[:, 256 * g:256 * g + 256]
            dyz_parts.append(rinv * (dyn - yn * jnp.mean(dyn * yn, axis=-1, keepdims=True)))
        dyz = jnp.concatenate(dyz_parts, axis=1)
        _acc(dnw_acc, jnp.concatenate(dnw_parts, axis=1))
        dy = dyz * siluz
        dz = dyz * y * (sz * (1.0 + z * (1.0 - sz)))
        _acc(dd_acc, jnp.sum(dy * xs, axis=0, keepdims=True))

        causal = _iota((CH, CH), 0) >= _iota((CH, CH), 1)
        anti = _iota((CH, CH), 0) <= _iota((CH, CH), 1)
        lane_lo = _iota((CH, 128), 1) < HP
        lane_id = _iota((CH, 128), 1)
        last_row = _iota((CH, 128), 0) == CH - 1
        sprev = sprev_ref[...]
        dacs = jnp.zeros((CH, 128), F32)
        ddt_x = jnp.zeros((CH, 128), F32)
        dxs_parts, dbm_parts, dcm_parts = [], [], []
        for g in range(NG):
            bmg, cmg = bm[:, g * NS:(g + 1) * NS], cm[:, g * NS:(g + 1) * NS]
            bmt, cmt = bmg.T, cmg.T
            cb = _dot(_mx(cmg), _mx(bmt))
            cbt = _dot(_mx(bmg), _mx(cmt))
            dcb = jnp.zeros((CH, CH), F32)
            dcbt = jnp.zeros((CH, CH), F32)
            dbmg = jnp.zeros((CH, NS), F32)
            dcmg = jnp.zeros((CH, NS), F32)
            for q in (2 * g, 2 * g + 1):
                sl = slice(128 * q, 128 * q + 128)
                xsq = xs[:, sl]
                dtp = _pair_lane(dt[:, 2 * q:2 * q + 1], dt[:, 2 * q + 1:2 * q + 2])
                xd = xsq * dtp
                dyq = dy[:, sl]
                sp = sprev[:, sl]
                dsn = ds_ref[:, sl]
                e_pair = jnp.exp(_pair_lane(acs[:, 2 * q:2 * q + 1], acs[:, 2 * q + 1:2 * q + 2]))
                cd = jnp.exp(_pair_lane(acst[2 * q:2 * q + 1, CH - 1:CH], acst[2 * q + 1:2 * q + 2, CH - 1:CH]))
                dye = dyq * e_pair
                dcmg = dcmg + _dot(_mx(dye), _mx(sp.T))
                dsp = _dot(_mx(cmt), _mx(dye)) + cd * dsn
                yoff = _dot(_mx(cmg), _mx(sp)) * e_pair
                dacs_lane = dyq * yoff
                dxd = jnp.zeros((CH, 128), F32)
                sds = jnp.sum(dsn * sp, axis=0, keepdims=True) * cd
                for jj in range(2):
                    h = 2 * q + jj
                    hm = lane_lo if jj == 0 else ~lane_lo
                    col, row = acs[:, h:h + 1], acst[h:h + 1, :]
                    lm = jnp.where(causal, jnp.exp(jnp.minimum(col - row, 0.0)), 0.0)
                    lmt = jnp.where(anti, jnp.exp(jnp.minimum(row - col, 0.0)), 0.0)
                    xm = _mx(jnp.where(hm, xd, 0.0))
                    dym = _mx(jnp.where(hm, dyq, 0.0))
                    gm = _dot_nt(dym, xm)
                    gmt = _dot_nt(xm, dym)
                    dcb = dcb + gm * lm
                    dcbt = dcbt + gmt * lmt
                    dxd = dxd + _dot(_mx(cbt * lmt), dym)
                    w = gm * cb * lm
                    wt = gmt * cbt * lmt
                    dacs_h = jnp.sum(w, axis=1, keepdims=True) - jnp.sum(wt, axis=1, keepdims=True)
                    alast = acst[h:h + 1, CH - 1:CH]
                    dec_col = jnp.exp(alast - col)
                    dsm = _mx(jnp.where(hm[0:NS], dsn, 0.0))
                    qh = _dot_nt(xm, dsm)
                    dbmg = dbmg + qh * dec_col
                    ddec = jnp.sum(qh * bmg, axis=1, keepdims=True)
                    dxd = dxd + _dot(_mx(bmg * dec_col), dsm)
                    dacs_h = dacs_h - ddec * dec_col
                    dacs_h = dacs_h + jnp.sum(jnp.where(hm, dacs_lane, 0.0), axis=1, keepdims=True)
                    tail = jnp.sum(ddec * dec_col, axis=0, keepdims=True) + jnp.sum(
                        jnp.where(hm[0:1], sds, 0.0), axis=1, keepdims=True)
                    dacs = dacs + jnp.where(lane_id == h, dacs_h, 0.0) + jnp.where(
                        last_row & (lane_id == h), tail, 0.0)
                ds_ref[:, sl] = dsp
                for jj in range(2):
                    h = 2 * q + jj
                    hm = lane_lo if jj == 0 else ~lane_lo
                    ddt_x = ddt_x + jnp.where(lane_id == h, jnp.sum(jnp.where(hm, dxd * xsq, 0.0), axis=1, keepdims=True), 0.0)
                dxs_parts.append(dxd * dtp + d_ref[:, sl] * dyq)
            dcmg = dcmg + _dot(_mx(dcb), _mx(bmg))
            dbmg = dbmg + _dot(_mx(dcbt), _mx(cmg))
            dbm_parts.append(dbmg)
            dcm_parts.append(dcmg)
        triu = (_iota((CH, CH), 0) <= _iota((CH, CH), 1)).astype(F32)
        dadt = _dot_hi(triu, dacs)
        ddt = dadt * a + ddt_x
        _acc(dalog_acc, jnp.sum(dadt * dt, axis=0, keepdims=True) * a)
        ddtraw = ddt * _sigmoid(dtraw)
        ddt_ref[...] = ddtraw
        _acc(ddtb_acc, jnp.sum(ddtraw, axis=0, keepdims=True))
        dxbc = jnp.concatenate(dxs_parts + dbm_parts + dcm_parts, axis=1)
        sx = _sigmoid(xpre)
        dpre = dxbc * (sx * (1.0 + xpre * (1.0 - sx)))
        _acc(dcb_acc, jnp.sum(dpre, axis=0, keepdims=True))
        raw = proj_ref[:, SW:SW + D]
        prev8 = halo_ref[:, SW:SW + D] * jnp.where(ctx.seq_first, 0.0, 1.0)
        next8 = nxt_ref[...]
        draw = cw_ref[3:4, :] * dpre
        dcw = [None] * 4
        dcw[3] = jnp.sum(dpre * raw, axis=0, keepdims=True)
        for j in (1, 2, 3):
            dcw[3 - j] = jnp.sum(dpre * _shift_rows(raw, prev8, j), axis=0, keepdims=True)
            draw = draw + cw_ref[3 - j:4 - j, :] * _shift_rows_up(dpre, next8, j)
        _acc(dcw_acc, jnp.concatenate(dcw + [jnp.zeros((4, D), F32)], axis=0))
        nxt_ref[...] = dpre[0:8]
        dzx_ref[:, 0:SW] = dz
        dzx_ref[:, SW:SW + D] = draw

    return _rowcall(name, fn, t, CH, seq // CH, tiled=[dyo, proj, xpre_all, y_all, sprev_all], halos=[(proj, 8)],
                    res=[conv_w, dt_bias, a_log, d_rep, norm_w],
                    out_tiled=[_sds((t, SW + D)), _sds((t, 128))],
                    out_acc=[_sds((1, SW)), _sds((1, SW)), _sds((8, D)), _sds((1, D)), _sds((1, 128)), _sds((1, 128))],
                    scratch=[pltpu.VMEM((nb, NS, SW), F32), pltpu.VMEM((nb, 8, D), F32)], reverse=True, batch=nb, **carry)


def _gelu(y):
    k = math.sqrt(2.0 / math.pi)
    return 0.5 * y * (1.0 + jnp.tanh(k * (y + 0.044715 * y * y * y)))


def _gelu_grad(y):
    k = math.sqrt(2.0 / math.pi)
    th = jnp.tanh(k * (y + 0.044715 * y * y * y))
    return 0.5 * (1.0 + th) + 0.5 * y * (1.0 - th * th) * k * (1.0 + 3.0 * 0.044715 * y * y)


S5T = 256


def _cmul_add(xr, xi, ar, ai, sr, si):
    return xr + ar * sr - ai * si, xi + ar * si + ai * sr


def _s5_fwd(name, proj, bbd, cbd, pw, tab, d5, w_glu, b_glu, seq, **carry):
    t = proj.shape[0]
    tm = S5T

    def fn(ctx, tiled, halos, exs, res, outs, accs, exaccs, scr):
        proj_ref, = tiled
        bbd_ref, cbd_ref, pw_ref, tab_ref, d_ref, wg_ref, bg_ref = res
        out_ref, xst_ref, y_ref, xb_ref, ub_ref = outs
        carry_ref, = scr

        @pl.when(ctx.seq_first)
        def _():
            carry_ref[...] = jnp.zeros(carry_ref.shape, F32)

        u = proj_ref[:, 1536:2048]
        bu = _dot(_mx(u), bbd_ref[...])
        xr, xi = bu[:, :S5L].reshape(tm // 8, 8, S5L), bu[:, S5L:].reshape(tm // 8, 8, S5L)
        for k, sh in enumerate((1, 2, 4)):
            xr, xi = _cmul_add(xr, xi, pw_ref[k, :, :S5L], pw_ref[k, :, S5L:], pltpu.roll(xr, sh, 1), pltpu.roll(xi, sh, 1))
        xst_ref[:, :S5L] = xr.reshape(tm, S5L)
        xst_ref[:, S5L:] = xi.reshape(tm, S5L)

        def tile_fix(i, c):
            cr, ci = c
            rows = pl.ds(pl.multiple_of(i * 8, 8), 8)
            tr, ti = _cmul_add(xst_ref[rows, :S5L], xst_ref[rows, S5L:], tab_ref[:, :S5L], tab_ref[:, S5L:], cr, ci)
            xst_ref[rows, :S5L] = tr
            xst_ref[rows, S5L:] = ti
            return tr[7:8], ti[7:8]

        cr, ci = lax.fori_loop(0, tm // 8, tile_fix, (carry_ref[0:1, :S5L], carry_ref[0:1, S5L:]))
        carry_ref[0:1, :S5L] = cr
        carry_ref[0:1, S5L:] = ci
        xb = _mx(xst_ref[...])
        xb_ref[...] = xb
        ub_ref[...] = _mx(u)
        y = _dot(xb, cbd_ref[...]) + u * d_ref[...]
        y_ref[...] = y
        g = _gelu(y)
        v = _dot(_mx(g), wg_ref[...]) + bg_ref[...]
        out_ref[...] = _mx(g * _sigmoid(v))

    return _rowcall(name, fn, t, tm, seq // tm, tiled=[proj], res=[bbd, cbd, pw, tab, d5, w_glu, b_glu],
                    out_tiled=[_sds((t, SW), MXU_DTYPE), _sds((t, 2 * S5L)), _sds((t, SW)),
                               _sds((t, 2 * S5L), MXU_DTYPE), _sds((t, SW), MXU_DTYPE)],
                    scratch=[pltpu.VMEM((8, 2 * S5L), F32)], **carry)


def _s5_bwd(name, dout, proj, xst, y_all, bbdt, cbdt, pwc, tabc, d5, w_glu, b_glu, seq, **carry):
    t = proj.shape[0]
    tm = S5T

    def fn(ctx, tiled, halos, exs, res, outs, accs, exaccs, scr):
        dout_ref, proj_ref, xst_ref, y_ref = tiled
        halo_ref, = halos
        bbdt_ref, cbdt_ref, pw_ref, tab_ref, d_ref, wg_ref, bg_ref = res
        du_ref, lam_ref, dyb_ref, gb_ref, dvb_ref = outs
        da_acc, dd_acc, dbg_acc = accs
        carry_ref, lamf_ref = scr

        @pl.when(ctx.seq_last)
        def _():
            carry_ref[...] = jnp.zeros(carry_ref.shape, F32)

        u = proj_ref[:, 1536:2048]
        y = y_ref[...]
        g = _gelu(y)
        v = _dot(_mx(g), wg_ref[...]) + bg_ref[...]
        sg = _sigmoid(v)
        dout = dout_ref[...]
        dv = dout * g * sg * (1.0 - sg)
        dvb = _mx(dv)
        dvb_ref[...] = dvb
        gb_ref[...] = _mx(g)
        _acc(dbg_acc, jnp.sum(dv, axis=0, keepdims=True))
        dg = dout * sg + _dot_nt(dvb, wg_ref[...])
        dy = dg * _gelu_grad(y)
        dyb = _mx(dy)
        dyb_ref[...] = dyb
        _acc(dd_acc, jnp.sum(dy * u, axis=0, keepdims=True))
        dx = _dot(dyb, cbdt_ref[...])
        xr, xi = dx[:, :S5L].reshape(tm // 8, 8, S5L), dx[:, S5L:].reshape(tm // 8, 8, S5L)
        for k, sh in enumerate((1, 2, 4)):
            xr, xi = _cmul_add(xr, xi, pw_ref[k, :, :S5L], pw_ref[k, :, S5L:], pltpu.roll(xr, 8 - sh, 1),
                               pltpu.roll(xi, 8 - sh, 1))
        lamf_ref[:, :S5L] = xr.reshape(tm, S5L)
        lamf_ref[:, S5L:] = xi.reshape(tm, S5L)

        def tile_fix(i, c):
            cr, ci = c
            rows = pl.ds(pl.multiple_of((tm // 8 - 1 - i) * 8, 8), 8)
            tr, ti = _cmul_add(lamf_ref[rows, :S5L], lamf_ref[rows, S5L:], tab_ref[:, :S5L], tab_ref[:, S5L:], cr, ci)
            lamf_ref[rows, :S5L] = tr
            lamf_ref[rows, S5L:] = ti
            return tr[0:1], ti[0:1]

        cr, ci = lax.fori_loop(0, tm // 8, tile_fix, (carry_ref[0:1, :S5L], carry_ref[0:1, S5L:]))
        carry_ref[0:1, :S5L] = cr
        carry_ref[0:1, S5L:] = ci
        lam = lamf_ref[...]
        lamb = _mx(lam)
        lam_ref[...] = lamb
        du_ref[...] = dy * d_ref[...] + _dot(lamb, bbdt_ref[...])
        prev8 = halo_ref[...] * jnp.where(ctx.seq_first, 0.0, 1.0)
        xprev = _shift_rows(xst_ref[...], prev8, 1)
        lr, li = lam[:, :S5L], lam[:, S5L:]
        pr, pi = xprev[:, :S5L], xprev[:, S5L:]
        dar = jnp.sum(lr * pr + li * pi, axis=0, keepdims=True)
        dai = jnp.sum(li * pr - lr * pi, axis=0, keepdims=True)
        _acc(da_acc, jnp.concatenate([dar, dai], axis=1))

    return _rowcall(name, fn, t, tm, seq // tm, tiled=[dout, proj, xst, y_all], halos=[(xst, 8)],
                    res=[bbdt, cbdt, pwc, tabc, d5, w_glu, b_glu],
                    out_tiled=[_sds((t, SW)), _sds((t, 2 * S5L), MXU_DTYPE), _sds((t, SW), MXU_DTYPE),
                               _sds((t, SW), MXU_DTYPE), _sds((t, SW), MXU_DTYPE)],
                    out_acc=[_sds((1, 2 * S5L)), _sds((1, SW)), _sds((1, SW))],
                    scratch=[pltpu.VMEM((8, 2 * S5L), F32), pltpu.VMEM((tm, 2 * S5L), F32)], reverse=True, **carry)


def _out_fwd(name, yssd, ys5, x1, mod, w_out, lng, lnb, seq):
    t = x1.shape[0]
    tm = min(TM_WIDE, seq)

    def fn(ctx, tiled, halos, exs, res, outs, accs, exaccs, scr):
        ya_ref, yb_ref, x_ref = tiled
        mod_ref, = exs
        w_ref, g_ref, b_ref = res
        m = _dot(ya_ref[...], w_ref[0:SW, :]) + _dot(yb_ref[...], w_ref[SW:2 * SW, :])
        r = ALPHA * x_ref[...] + mod_ref[0, 5:6, :] * m
        outs[0][...] = _ln_fwd(r, g_ref[...], b_ref[...])
        outs[1][...] = r
        outs[2][...] = m

    return _rowcall(name, fn, t, tm, seq // tm, tiled=[yssd, ys5, x1], exs=[mod], res=[w_out, lng, lnb],
                    out_tiled=[_sds((t, D)), _sds((t, D)), _sds((t, D))])


def _out_bwd(name, dxo, r, m, mod, lng, w_out, seq, **carry):
    t = dxo.shape[0]
    tm = min(TM_WIDE, seq)

    def fn(ctx, tiled, halos, exs, res, outs, accs, exaccs, scr):
        dxo_ref, r_ref, m_ref = tiled
        mod_ref, = exs
        g_ref, w_ref = res
        dr, dgam, dbet = _ln_bwd(r_ref[...], g_ref[...], dxo_ref[...])
        outs[0][...] = dr
        _acc(accs[0], dgam)
        _acc(accs[1], dbet)
        _acc(exaccs[0].at[0], jnp.sum(dr * m_ref[...], axis=0, keepdims=True))
        dm = _mx(mod_ref[0, 5:6, :] * dr)
        outs[1][...] = dm
        dyc = _dot_nt(dm, w_ref[...])
        outs[2][...] = dyc[:, 0:SW]
        outs[3][...] = dyc[:, SW:2 * SW]

    b = mod.shape[0]
    return _rowcall(name, fn, t, tm, seq // tm, tiled=[dxo, r, m], exs=[mod], res=[lng, w_out],
                    out_tiled=[_sds((t, D)), _sds((t, D), MXU_DTYPE), _sds((t, SW)), _sds((t, SW))],
                    out_acc=[_sds((1, D)), _sds((1, D))], out_exacc=[_sds((b, 1, D))], **carry)


def _s5_discretise(a_re, a_im, log_dt, b_re, b_im):
    dt = jnp.exp(log_dt)[:, None]
    mag = jnp.exp(dt * a_re)
    ab_re, ab_im = mag * jnp.cos(dt * a_im), mag * jnp.sin(dt * a_im)
    den = a_re * a_re + a_im * a_im
    nr, ni = ab_re - 1.0, ab_im
    f_re, f_im = (nr * a_re + ni * a_im) / den, (ni * a_re - nr * a_im) / den
    bb_re = f_re[..., None] * b_re - f_im[..., None] * b_im
    bb_im = f_re[..., None] * b_im + f_im[..., None] * b_re
    return ab_re, ab_im, bb_re, bb_im


def _s5_tables(ab_re, ab_im):
    ar, ai = ab_re.reshape(1, S5L), ab_im.reshape(1, S5L)
    pows = [(ar, ai)]
    for _ in range(7):
        pr, pi = pows[-1]
        pows.append((pr * ar - pi * ai, pr * ai + pi * ar))

    def pack(rows, sign):
        return jnp.concatenate([jnp.concatenate([r for r, _ in rows], axis=0),
                                jnp.concatenate([sign * i for _, i in rows], axis=0)], axis=1)

    row = jnp.arange(8)[:, None]
    pw = jnp.stack([jnp.where(row >= sh, pack([pows[sh - 1]], 1.0), 0.0) for sh in (1, 2, 4)])
    pwc = jnp.stack([jnp.where(row < 8 - sh, pack([pows[sh - 1]], -1.0), 0.0) for sh in (1, 2, 4)])
    tab = pack(pows, 1.0)
    tabc = pack(pows[::-1], -1.0)
    return pw, tab, pwc, tabc


class _GradGroup:
    def __init__(self, tag, grads, place):
        self.tag, self.names, (self.half, self.chip) = tag, list(grads), place
        self.gsh = [_shard_major(g) if n in COL_SHARDED else g.reshape(4, g.shape[0] // 4, g.shape[1])
                    for n, g in grads.items()]

    def sibling(self):
        return _rs_sibling_stage(self.gsh)

    def chips(self, received):
        self.sums, sums_bf = _rs_add(self.tag + "_add", self.gsh, received, self.half)
        return _rs_chips_stage(sums_bf)

    def join(self, received):
        return _rs_join_stage(_rs_sum(self.tag + "_sum", self.sums, received, self.chip, self.half))

    def result(self, joined):
        return dict(zip(self.names, joined))


def _hid(fn, *args, stage=None, **kw):
    if stage is None:
        return fn(*args, **kw), None
    return fn(*args, carry=stage, **kw)


def _local_step(x, tgt, mod, w, sp, seq, dist=None):
    t = x.shape[0]
    mxu = MXU_DTYPE
    big = {}

    def group(tag, grads):
        if dist is None:
            big.update(grads)
            return None
        return _GradGroup(tag, grads, dist[2])

    mid = {"carry_mid": 0.6} if dist else {}
    (h1, ab1, s1), got = _hid(_ffn_fwd_hidden, "ffn1_fwd_a", x, mod, 0, w["ffn1_w1t"], w["ffn1_w3t"], seq,
                              stage=dist and dist[0][0], **mid)
    if dist:
        w = {**w, **dist[0][1](got)}
    x1, r1, f1 = _ffn_fwd_out("ffn1_fwd_b", s1, x, mod, 0, w["ffn1_w2"], sp["ln1_g"], sp["ln1_b"], seq)
    h2, proj = _proj_fwd("proj_fwd", x1, mod, w["w_in"], seq)
    yssd, xpre, yraw, sprev = _ssd_fwd("ssd_fwd", proj, sp["conv_w"], sp["conv_b"], sp["dt_bias"], sp["a_log"],
                                       sp["d_rep"], sp["ssd_norm_w"], seq)
    (ab_re, ab_im, bb_re, bb_im), disc_vjp = jax.vjp(_s5_discretise, sp["s5_a_re"], sp["s5_a_im"], sp["s5_log_dt"],
                                                     sp["s5_b_re"], sp["s5_b_im"])
    eye = jnp.eye(S5G, dtype=F32)
    bbd = jnp.concatenate([jnp.einsum("gk,gph->ghkp", eye, bb_re).reshape(SW, S5L),
                           jnp.einsum("gk,gph->ghkp", eye, bb_im).reshape(SW, S5L)], axis=1).astype(mxu)
    cbd = jnp.concatenate([jnp.einsum("gk,ghp->gpkh", eye, sp["s5_c_re"]).reshape(S5L, SW),
                           -jnp.einsum("gk,ghp->gpkh", eye, sp["s5_c_im"]).reshape(S5L, SW)], axis=0).astype(mxu)
    bbdt = jnp.concatenate([jnp.einsum("gk,gph->kpgh", eye, bb_re).reshape(S5L, SW),
                            jnp.einsum("gk,gph->kpgh", eye, bb_im).reshape(S5L, SW)], axis=0).astype(mxu)
    cbdt = jnp.concatenate([jnp.einsum("gk,ghp->khgp", eye, sp["s5_c_re"]).reshape(SW, S5L),
                            -jnp.einsum("gk,ghp->khgp", eye, sp["s5_c_im"]).reshape(SW, S5L)], axis=1).astype(mxu)
    pw, tab, pwc, tabc = _s5_tables(lax.stop_gradient(ab_re), lax.stop_gradient(ab_im))
    (ys5, xst, y5, xstb, ub), got = _hid(_s5_fwd, "s5_fwd", proj, bbd, cbd, pw, tab, sp["s5_d"], w["w_glu"], sp["b_glu"],
                                         seq, stage=dist and dist[1][0], **mid)
    if dist:
        w = {**w, **dist[1][1](got)}
    x2, r2, m2 = _out_fwd("out_fwd", yssd, ys5, x1, mod, w["w_out"], sp["ln2_g"], sp["ln2_b"], seq)
    h3, ab3, s3 = _ffn_fwd_hidden("ffn2_fwd_a", x2, mod, 6, w["ffn2_w1t"], w["ffn2_w3t"], seq)
    x3, r3, f3, dy3, loss_acc = _ffn_fwd_out("ffn2_fwd_b", s3, x2, mod, 6, w["ffn2_w2"], sp["ln3_g"], sp["ln3_b"], seq, tgt=tgt)
    dx2, df3, dab3, dg3g, dg3b, dgate3, dsh3, dsc3 = _ffn_bwd("ffn2_bwd", dy3, r3, ab3, f3, x2, mod, 6, sp["ln3_g"],
                                                              w["ffn2_w2"], w["ffn2_w1t"], w["ffn2_w3t"], seq)
    grp = group("rs_ffn2", dict(ffn2_w1=_tn_matmul("ffn2_dw1", dab3, h3, FB, D, a_cols=(0, FF)),
                                ffn2_w3=_tn_matmul("ffn2_dw3", dab3, h3, FB, D, a_cols=(FF, FF)),
                                ffn2_w2=_tn_matmul("ffn2_dw2", s3, df3, FB, D)))
    (dr2, dm2, dyssd, dys5, dg2g, dg2b, dgate2), got = _hid(_out_bwd, "out_bwd", dx2, r2, m2, mod, sp["ln2_g"],
                                                          w["w_out"], seq, stage=grp and grp.sibling())
    g_w_out = jnp.concatenate([_tn_matmul("dw_out_a", yssd, dm2, SW, D), _tn_matmul("dw_out_b", ys5, dm2, SW, D)], axis=0)
    (du, lam, dy5b, g5b, dv5b, da5, dd5, dbglu), got = _hid(
        _s5_bwd, "s5_bwd", dys5, proj, xst, y5, bbdt, cbdt, pwc, tabc, sp["s5_d"], w["w_glu"], sp["b_glu"],
        seq, stage=grp and grp.chips(got))
    g_w_glu = _tn_matmul("dw_glu", g5b, dv5b, SW, SW)
    dbb = _diag_groups(_tn_diag("s5_db", lam, ub))
    dcc = _diag_groups(_tn_diag("s5_dc", xstb, dy5b))
    (dzx, ddt, dnw, ddl, dcw, dcb, ddtb, dalog), got = _hid(
        _ssd_bwd, "ssd_bwd", dyssd, proj, xpre, yraw, sprev, sp["conv_w"], sp["dt_bias"], sp["a_log"], sp["d_rep"],
        sp["ssd_norm_w"], seq, stage=grp and grp.join(got))
    if grp:
        big.update(grp.result(got))
    dx1, dproj, dsh2, dsc2 = _mod_bwd("proj_bwd", dzx, dr2, x1, mod, 3, [w["w_in"]], seq, extra=(du, ddt), nt=True)
    gwi = _tn_matmul("dw_in", h2, dproj, D, PW)
    grp = group("rs_mix", dict(w_in=jnp.concatenate([gwi[:, :1536], gwi[:, 2048:2056], gwi[:, 1536:2048]], axis=1),
                               w_glu=g_w_glu, w_out=g_w_out))
    (dr1, df1, dab1, dg1g, dg1b, dgate1), got = _hid(_ffn_bwd1, "ffn1_bwd1", dx1, r1, ab1, f1, mod, 0, sp["ln1_g"],
                                                          w["ffn1_w2"], seq, stage=grp and grp.sibling())
    g1, got = _hid(_tn_matmul, "ffn1_dw1", dab1, h1, FB, D, a_cols=(0, FF), stage=grp and grp.chips(got))
    g3, got = _hid(_tn_matmul, "ffn1_dw3", dab1, h1, FB, D, a_cols=(FF, FF), stage=grp and grp.join(got))
    if grp:
        big.update(grp.result(got))
    grp = group("rs_ffn1a", dict(ffn1_w1=g1, ffn1_w3=g3))
    g2, got = _hid(_tn_matmul, "ffn1_dw2", s1, df1, FB, D, stage=grp and grp.sibling())
    grp2 = group("rs_ffn1b", dict(ffn1_w2=g2))
    (dx0, dsh1, dsc1), got = _hid(_mod_bwd, "ffn1_bwd2", dab1, dr1, x, mod, 0, [w["ffn1_w1t"], w["ffn1_w3t"]], seq,
                                  stage=grp and _merge_stages(grp.chips(got), grp2.sibling()))
    if grp:
        n = len(grp.names)
        got = _run_stage("rs_ffn1_tail", _merge_stages(grp.join(got[:n]), grp2.chips(got[n:])))
        big.update(grp.result(got[:n]))
        big.update(grp2.result(_run_stage("rs_ffn1b_join", grp2.join(got[n:]))))
    dmod = jnp.concatenate([dsh1, dsc1, dgate1, dsh2, dsc2, dgate2, dsh3, dsc3, dgate3], axis=1)
    dc_re, dc_im = dcc[0].transpose(0, 2, 1), -dcc[1].transpose(0, 2, 1)
    g_a_re, g_a_im, g_log_dt, g_b_re, g_b_im = disc_vjp(
        (da5[:, :S5L].reshape(S5G, S5P), da5[:, S5L:].reshape(S5G, S5P), dbb[0], dbb[1]))
    small = dict(ln1_g=dg1g, ln1_b=dg1b, ln2_g=dg2g, ln2_b=dg2b, ln3_g=dg3g, ln3_b=dg3b, conv_w=dcw[0:4], conv_b=dcb,
                 dt_bias=ddtb[:, :NH], a_log=dalog[:, :NH], d_ssd=jnp.sum(ddl.reshape(NH, HP), axis=1).reshape(1, NH),
                 ssd_norm_w=dnw, s5_a_re=g_a_re, s5_a_im=g_a_im, s5_log_dt=g_log_dt, s5_b_re=g_b_re, s5_b_im=g_b_im,
                 s5_c_re=dc_re, s5_c_im=dc_im, s5_d=dd5, w_glu_b=dbglu)
    return loss_acc[0, 0], dx0, dmod, big, small


def _place():
    return lax.axis_index("x"), lax.axis_index("y"), lax.axis_index("c")


def _other_chips(x, y):
    return [(1 - x, y), (x, 1 - y), (1 - x, 1 - y)]


def _allgather8(name, a):
    r, n = a.shape

    def body(x_ref, out_ref, send_sems, recv_sems, local_sem):
        x, y, c = _place()
        me, sibling = (x, y, c), (x, y, 1 - c)
        chips = _other_chips(x, y)

        def rows(px, py, pc):
            return out_ref.at[pl.ds(pl.multiple_of((4 * px + 2 * py + pc) * r, 8), r), :]

        def copy(k, block, to, src=None):
            return pltpu.make_async_remote_copy(src_ref=rows(*block) if src is None else src, dst_ref=rows(*block),
                                                send_sem=send_sems.at[k], recv_sem=recv_sems.at[k], device_id=to,
                                                device_id_type=MESH_T)

        mine = pltpu.make_async_copy(x_ref, rows(*me), local_sem)
        mine.start()
        first = [copy(0, me, sibling, src=x_ref)]
        first += [copy(1 + j, me, (*chip, c), src=x_ref) for j, chip in enumerate(chips)]
        for cp in first:
            cp.start()
        passed = [copy(4 + j, (*chip, c), sibling) for j, chip in enumerate(chips)]
        for j, chip in enumerate(chips):
            copy(1 + j, (*chip, c), me).wait_recv()
            passed[j].start()
        copy(0, sibling, me).wait_recv()
        for j, chip in enumerate(chips):
            copy(4 + j, (*chip, 1 - c), me).wait_recv()
        for cp in first + passed:
            cp.wait_send()
        mine.wait()

    out = _pcall(body, name=name, out_shape=_sds((8 * r, n), a.dtype),
                 in_specs=[pl.BlockSpec(memory_space=pltpu.VMEM)], out_specs=pl.BlockSpec(memory_space=pltpu.VMEM),
                 scratch_shapes=[pltpu.SemaphoreType.DMA((7,)), pltpu.SemaphoreType.DMA((7,)), pltpu.SemaphoreType.DMA],
                 compiler_params=_cparams())(a)
    return out.reshape(8, r, n)


class _Stage:
    def __init__(self, ins, out_shapes, n_sems, start, finish, mid=None, aliases=None):
        self.ins, self.out_shapes, self.n_sems = list(ins), list(out_shapes), tuple(n_sems)
        self.start, self.mid, self.finish = start, mid, finish
        self.aliases = dict(aliases or {})

    def sem_shapes(self):
        return [pltpu.SemaphoreType.DMA((n,)) for n in self.n_sems]

    def run(self, refs, at_start=None, at_mid=None, at_finish=None, between=None):
        if between is None:
            for part in (self.start, self.mid, self.finish):
                if part is not None:
                    part(*refs)
            return
        pl.when(at_start)(functools.partial(self.start, *refs))
        if self.mid is not None:
            pl.when(at_mid)(functools.partial(self.mid, *refs))
        between()
        pl.when(at_finish)(functools.partial(self.finish, *refs))


def _merge_stages(a, b):
    n_in, n_out, n_sem = len(a.ins), len(a.out_shapes), len(a.n_sems)

    def both(fa, fb):
        def run(ins, outs, sems):
            fa(ins[:n_in], outs[:n_out], sems[:n_sem])
            fb(ins[n_in:], outs[n_out:], sems[n_sem:])
        return run

    aliases = {**a.aliases, **{n_in + k: n_out + v for k, v in b.aliases.items()}}
    return _Stage(a.ins + b.ins, a.out_shapes + b.out_shapes, a.n_sems + b.n_sems, both(a.start, b.start),
                  both(a.finish, b.finish), aliases=aliases)


def _run_stage(name, st):
    n_in, n_out = len(st.ins), len(st.out_shapes)

    def body(*refs):
        st.run((refs[:n_in], refs[n_in:n_in + n_out], refs[n_in + n_out:]))

    any_spec = pl.BlockSpec(memory_space=pl.ANY)
    return _pcall(body, name=name, out_shape=st.out_shapes, in_specs=[any_spec] * n_in, out_specs=[any_spec] * n_out,
                  input_output_aliases=st.aliases, scratch_shapes=st.sem_shapes(), compiler_params=_cparams())(*st.ins)


def _rows(ref_rows, half, align):
    hr = ref_rows // 2
    return pl.ds(pl.multiple_of(half * hr, align), hr)


def _gather_stage(shards):
    n = len(shards)
    pairs = [(i, j) for i in range(n) for j in range(3)]

    def env(ins, outs, sems):
        x, y, c = _place()
        chips = _other_chips(x, y)

        def copy(i, k, chip, half, to, src=None):
            dst = outs[i].at[2 * chip[0] + chip[1], _rows(shards[i].shape[0], half, 16)]
            return pltpu.make_async_remote_copy(src_ref=dst if src is None else src, dst_ref=dst,
                                                send_sem=sems[0].at[6 * i + k], recv_sem=sems[1].at[6 * i + k],
                                                device_id=to, device_id_type=MESH_T)

        def first(i, j):
            return copy(i, j, (x, y), c, (*chips[j], c), src=ins[i].at[_rows(shards[i].shape[0], c, 16)])

        def passed(i, j, half):
            return copy(i, 3 + j, chips[j], half, (x, y, 1 - c))

        def landed(i, j):
            return copy(i, j, chips[j], c, (x, y, 1 - c))

        return c, first, passed, landed

    def start(ins, outs, sems):
        c, first, passed, landed = env(ins, outs, sems)
        for i, j in pairs:
            first(i, j).start()

    def mid(ins, outs, sems):
        c, first, passed, landed = env(ins, outs, sems)
        for i, j in pairs:
            landed(i, j).wait_recv()
            passed(i, j, c).start()

    def finish(ins, outs, sems):
        c, first, passed, landed = env(ins, outs, sems)
        for i, j in pairs:
            passed(i, j, 1 - c).wait_recv()
        for i, j in pairs:
            first(i, j).wait_send()
            passed(i, j, c).wait_send()

    return _Stage(shards, [_sds((4,) + s.shape, s.dtype) for s in shards], (6 * n, 6 * n), start, finish, mid)


def _rs_sibling_stage(gs):
    n = len(gs)

    def copies(ins, outs, sems):
        x, y, c = _place()
        return [pltpu.make_async_remote_copy(src_ref=ins[i].at[:, _rows(gs[i].shape[1], 1 - c, 8)], dst_ref=outs[i],
                                             send_sem=sems[0].at[i], recv_sem=sems[1].at[i], device_id=(x, y, 1 - c),
                                             device_id_type=MESH_T) for i in range(n)]

    def start(*refs):
        for cp in copies(*refs):
            cp.start()

    def finish(*refs):
        for cp in copies(*refs):
            cp.wait()

    return _Stage(gs, [_sds((4, g.shape[1] // 2, g.shape[2]), g.dtype) for g in gs], (n, n), start, finish)


def _rs_chips_stage(hs):
    n = len(hs)

    def copies(ins, outs, sems):
        x, y, c = _place()
        return [pltpu.make_async_remote_copy(src_ref=ins[i].at[2 * chip[0] + chip[1]], dst_ref=outs[i].at[j],
                                             send_sem=sems[0].at[3 * i + j], recv_sem=sems[1].at[3 * i + j],
                                             device_id=(*chip, c), device_id_type=MESH_T)
                for i in range(n) for j, chip in enumerate(_other_chips(x, y))]

    def start(*refs):
        for cp in copies(*refs):
            cp.start()

    def finish(*refs):
        for cp in copies(*refs):
            cp.wait()

    return _Stage(hs, [_sds((3,) + h.shape[1:], h.dtype) for h in hs], (3 * n, 3 * n), start, finish)


def _rs_join_stage(fs):
    n = len(fs)

    def copy(outs, sems, i, half):
        x, y, c = _place()
        part = outs[i].at[_rows(fs[i].shape[0], c if half == "mine" else 1 - c, 8)]
        return pltpu.make_async_remote_copy(src_ref=part, dst_ref=part, send_sem=sems[0].at[i], recv_sem=sems[1].at[i],
                                            device_id=(x, y, 1 - c), device_id_type=MESH_T)

    def start(ins, outs, sems):
        for i in range(n):
            copy(outs, sems, i, "mine").start()

    def finish(ins, outs, sems):
        for i in range(n):
            copy(outs, sems, i, "theirs").wait_recv()
        for i in range(n):
            copy(outs, sems, i, "mine").wait_send()

    return _Stage(fs, [_sds(f.shape, f.dtype) for f in fs], (n, n), start, finish, aliases={i: i for i in range(n)})


def _row_block(r, cap=2048):
    b = min(r, cap)
    while r % b or b % 8:
        b -= 8
    return b


RS_SPLIT = 2


def _rs_add(name, gs, r1s, sel):
    n = len(gs)

    def body(sel_ref, *refs):
        g_refs, r_refs, h_refs, b_refs = (refs[k * n:(k + 1) * n] for k in range(4))
        for i in range(n):
            h = g_refs[i][...] + r_refs[i][...]
            h_refs[i][...] = h
            b_refs[i][...] = h.astype(BF16)

    def blk(g):
        return (1, g.shape[1] // 2 // RS_SPLIT, g.shape[2])

    here = lambda k, j, s: (k, j, 0)
    in_specs = [pl.BlockSpec(blk(g), lambda k, j, s: (k, s[0] * RS_SPLIT + j, 0)) for g in gs]
    in_specs += [pl.BlockSpec(blk(g), here) for g in gs]
    outs = _pcall(body, name=name,
                  out_shape=[_sds((4, g.shape[1] // 2, g.shape[2])) for g in gs]
                  + [_sds((4, g.shape[1] // 2, g.shape[2]), BF16) for g in gs],
                  grid_spec=pltpu.PrefetchScalarGridSpec(num_scalar_prefetch=1, grid=(4, RS_SPLIT), in_specs=in_specs,
                                                         out_specs=[pl.BlockSpec(blk(g), here) for g in gs] * 2),
                  compiler_params=_cparams(("parallel", "parallel")))(sel.reshape(1).astype(jnp.int32), *gs, *r1s)
    return outs[:n], outs[n:]


def _rs_sum(name, hs, r2s, chip, half):
    n = len(hs)

    def body(sel_ref, *refs):
        h_refs, r_refs, o_refs = (refs[k * n:(k + 1) * n] for k in range(3))
        for i in range(n):
            r = r_refs[i]
            o_refs[i][...] = ((h_refs[i][0] + r[0].astype(F32)) + r[1].astype(F32)) + r[2].astype(F32)

    def rows(h):
        return h.shape[1] // RS_SPLIT

    in_specs = [pl.BlockSpec((1, rows(h), h.shape[2]), lambda j, s: (s[0], j, 0)) for h in hs]
    in_specs += [pl.BlockSpec((3, rows(h), h.shape[2]), lambda j, s: (0, j, 0)) for h in hs]
    sel = jnp.stack([chip, half]).astype(jnp.int32)
    return _pcall(body, name=name, out_shape=[_sds((2 * h.shape[1], h.shape[2])) for h in hs],
                  grid_spec=pltpu.PrefetchScalarGridSpec(
                      num_scalar_prefetch=1, grid=(RS_SPLIT,), in_specs=in_specs,
                      out_specs=[pl.BlockSpec((rows(h), h.shape[2]), lambda j, s: (s[1] * RS_SPLIT + j, 0)) for h in hs]),
                  compiler_params=_cparams(("parallel",)))(sel, *hs, *r2s)


def _sum8(name, a):
    _, r, n = a.shape
    br = _row_block(r)

    def body(a_ref, o_ref):
        acc = a_ref[0]
        for k in range(1, 8):
            acc = acc + a_ref[k]
        o_ref[...] = acc

    return _pcall(body, name=name, out_shape=_sds((r, n)), grid=(r // br,),
                  in_specs=[pl.BlockSpec((8, br, n), lambda j: (0, j, 0))], out_specs=pl.BlockSpec((br, n), lambda j: (j, 0)),
                  compiler_params=_cparams(("parallel",)))(a)


def _adamw(name, ws, gs, ms, vs, nblk):
    n = len(ws)

    def body(*refs):
        w_refs, g_refs, m_refs, v_refs, d_refs, nm_refs, nv_refs = (refs[k * n:(k + 1) * n] for k in range(7))
        for i in range(n):
            gv = g_refs[i][...]
            nm = ADAM_B1 * m_refs[i][...] + (1.0 - ADAM_B1) * gv
            nv = ADAM_B2 * v_refs[i][...] + (1.0 - ADAM_B2) * (gv * gv)
            nm_refs[i][...] = nm
            nv_refs[i][...] = nv
            m_hat = nm / (1.0 - ADAM_B1 ** ADAM_STEP)
            v_hat = nv / (1.0 - ADAM_B2 ** ADAM_STEP)
            d_refs[i][...] = -ADAM_LR * (m_hat / (jnp.sqrt(v_hat) + ADAM_EPS) + ADAM_WD * w_refs[i][...])

    specs = [pl.BlockSpec((w.shape[0] // nblk, w.shape[1]), lambda j: (j, 0)) for w in ws]
    outs = _pcall(body, name=name, out_shape=[_sds(w.shape) for w in ws] * 3, grid=(nblk,), in_specs=specs * 4,
                  out_specs=specs * 3, compiler_params=_cparams(("parallel",)))(*ws, *gs, *ms, *vs)
    return outs[:n], outs[n:2 * n], outs[2 * n:]


ADA_COLS = 2304
ADA_BLK = 768


def _ada_fwd(name, c_all, w_shard, b_cols):
    nb = c_all.shape[0]

    def body(c_ref, w_ref, b_ref, o_ref):
        cv = c_ref[...]
        cs = _mx(cv * _sigmoid(cv))
        o_ref[...] = _dot(cs, _mx(w_ref[...])) + b_ref[...]

    return _pcall(body, name=name, out_shape=_sds((nb, ADA_COLS)), grid=(ADA_COLS // ADA_BLK,),
                  in_specs=[pl.BlockSpec((nb, D), lambda j: (0, 0)), pl.BlockSpec((D, ADA_BLK), lambda j: (0, j)),
                            pl.BlockSpec((1, ADA_BLK), lambda j: (0, j))],
                  out_specs=pl.BlockSpec((nb, ADA_BLK), lambda j: (0, j)),
                  compiler_params=_cparams(("parallel",)))(c_all, w_shard, b_cols)


def _ada_bwd(name, c_all, dmod_cols, dmod_all):
    nb = c_all.shape[0]

    def body(c_ref, dc_ref, da_ref, gw_ref, gb_ref):
        cv = c_ref[...]
        cs = _mx(cv * _sigmoid(cv))
        gw_ref[...] = lax.dot_general(cs, _mx(dc_ref[...]), (((0,), (0,)), ((), ())), preferred_element_type=F32)

        @pl.when(pl.program_id(0) == 0)
        def _():
            gb_ref[...] = jnp.sum(da_ref[...], axis=0, keepdims=True)

    return _pcall(body, name=name, out_shape=[_sds((D, ADA_COLS)), _sds((1, 9 * D))], grid=(ADA_COLS // ADA_BLK,),
                  in_specs=[pl.BlockSpec((nb, D), lambda j: (0, 0)), pl.BlockSpec((nb, ADA_BLK), lambda j: (0, j)),
                            pl.BlockSpec((nb, 9 * D), lambda j: (0, 0))],
                  out_specs=[pl.BlockSpec((D, ADA_BLK), lambda j: (0, j)), pl.BlockSpec((1, 9 * D), lambda j: (0, 0))],
                  compiler_params=_cparams(("arbitrary",)))(c_all, dmod_cols, dmod_all)


BIG = ("ffn1_w1", "ffn1_w3", "ffn1_w2", "w_in", "w_glu", "w_out", "ffn2_w1", "ffn2_w3", "ffn2_w2")
COL_SHARDED = ("w_in",)
TRANSPOSED = ("ffn1_w1", "ffn1_w3", "ffn2_w1", "ffn2_w3")
SMALL = ("b_ada", "ln1_g", "ln1_b", "conv_w", "conv_b", "dt_bias", "a_log", "d_ssd", "ssd_norm_w", "s5_a_re", "s5_a_im",
         "s5_log_dt", "s5_b_re", "s5_b_im", "s5_c_re", "s5_c_im", "s5_d", "b_glu", "ln2_g", "ln2_b", "ln3_g", "ln3_b")
WEIGHTS = ("w_ada", "b_ada", "ffn1_w1", "ffn1_w3", "ffn1_w2", "ln1_g", "ln1_b", "w_in", "conv_w", "conv_b", "dt_bias",
           "a_log", "d_ssd", "ssd_norm_w", "s5_a_re", "s5_a_im", "s5_log_dt", "s5_b_re", "s5_b_im", "s5_c_re", "s5_c_im",
           "s5_d", "w_glu", "b_glu", "w_out", "ln2_g", "ln2_b", "ffn2_w1", "ffn2_w3", "ffn2_w2", "ln3_g", "ln3_b")
BIG_PAD = 2 * 1024 * 128


def _pack(arrs, mult, axis_keep=0):
    lead = arrs[0].shape[:axis_keep]
    flat = jnp.concatenate([a.reshape(lead + (-1,)) for a in arrs], axis=axis_keep)
    pad = (-flat.shape[-1]) % mult
    if pad:
        flat = jnp.concatenate([flat, jnp.zeros(lead + (pad,), flat.dtype)], axis=axis_keep)
    return flat


def _unpack(flat, shapes):
    out, off = [], 0
    for s in shapes:
        size = math.prod(s)
        out.append(flat[..., off:off + size].reshape(flat.shape[:-1] + tuple(s)))
        off += size
    return out


def _shard_major(a):
    rows, cols = a.shape
    return a.reshape(rows, 4, cols // 4).transpose(1, 0, 2)


def _from_shard_major(a):
    _, rows, w = a.shape
    return a.transpose(1, 0, 2).reshape(rows, 4 * w)


def kernel(x, c, w_ada, b_ada, ffn1_w1, ffn1_w3, ffn1_w2, ln1_g, ln1_b, w_in, conv_w, conv_b, dt_bias, a_log, d_ssd, ssd_norm_w, s5_a_re, s5_a_im, s5_log_dt, s5_b_re, s5_b_im, s5_c_re, s5_c_im, s5_d, w_glu, b_glu, w_out, ln2_g, ln2_b, ffn2_w1, ffn2_w3, ffn2_w2, ln3_g, ln3_b, loss_target, m_w_ada, m_b_ada, m_ffn1_w1, m_ffn1_w3, m_ffn1_w2, m_ln1_g, m_ln1_b, m_w_in, m_conv_w, m_conv_b, m_dt_bias, m_a_log, m_d_ssd, m_ssd_norm_w, m_s5_a_re, m_s5_a_im, m_s5_log_dt, m_s5_b_re, m_s5_b_im, m_s5_c_re, m_s5_c_im, m_s5_d, m_w_glu, m_b_glu, m_w_out, m_ln2_g, m_ln2_b, m_ffn2_w1, m_ffn2_w3, m_ffn2_w2, m_ln3_g, m_ln3_b, v_w_ada, v_b_ada, v_ffn1_w1, v_ffn1_w3, v_ffn1_w2, v_ln1_g, v_ln1_b, v_w_in, v_conv_w, v_conv_b, v_dt_bias, v_a_log, v_d_ssd, v_ssd_norm_w, v_s5_a_re, v_s5_a_im, v_s5_log_dt, v_s5_b_re, v_s5_b_im, v_s5_c_re, v_s5_c_im, v_s5_d, v_w_glu, v_b_glu, v_w_out, v_ln2_g, v_ln2_b, v_ffn2_w1, v_ffn2_w3, v_ffn2_w2, v_ln3_g, v_ln3_b):
    a = dict(locals())
    xi, yi, ci = _place()
    chip = 2 * xi + yi
    dev = 2 * chip + ci
    nb, seq, _ = x.shape
    t = nb * seq
    ndev = 8

    c_rows = nb * D // 128
    c_cw = _allgather8("gather_c", jnp.concatenate([c.reshape(c_rows, 128), conv_w.reshape(-1, 128)], axis=0))
    c_all = c_cw[:, :c_rows].reshape(ndev * nb, D)
    b_cols = lax.dynamic_slice(b_ada, (0, chip * ADA_COLS), (1, ADA_COLS))
    mod_part = _ada_fwd("ada_fwd", c_all, w_ada[0], b_cols)
    mod_parts = _allgather8("gather_mod", mod_part.reshape(-1, 128)).reshape(ndev, ndev * nb, ADA_COLS)
    mod_all = mod_parts[0::2].transpose(1, 0, 2).reshape(ndev * nb, 9 * D)
    mod = lax.dynamic_slice(mod_all, (nb * dev, 0), (nb, 9 * D)).reshape(nb, 9, D)

    def nat(n):
        return jnp.swapaxes(a[n], 1, 2)[0] if n[-7:] in TRANSPOSED else a[n][0]

    def gather(names):
        own = [nat(n).astype(MXU_DTYPE) for n in names]

        def weights(pieces):
            w = {}
            for n, mine, piece in zip(names, own, pieces):
                piece = lax.dynamic_update_slice(piece, mine[None], (chip, 0, 0))
                if n in COL_SHARDED:
                    wi = _from_shard_major(piece)
                    w[n] = jnp.concatenate([wi[:, :1536], wi[:, 1544:2056], wi[:, 1536:1544],
                                            jnp.zeros((D, 120), wi.dtype)], axis=1)
                else:
                    w[n + "t" if n in TRANSPOSED else n] = piece.reshape(-1, piece.shape[-1])
            return w

        return _gather_stage(own), weights

    first_stage, first_weights = gather(("ffn1_w1", "ffn1_w3"))
    w = first_weights(_run_stage("gather_w_first", first_stage))

    conv_full = _from_shard_major(c_cw[0::2, c_rows:].reshape(4, 4, 256))
    pad8 = lambda v: jnp.concatenate([v.reshape(1, NH), jnp.zeros((1, 128 - NH), F32)], axis=1)
    sp = dict(ln1_g=ln1_g, ln1_b=ln1_b, ln2_g=ln2_g, ln2_b=ln2_b, ln3_g=ln3_g, ln3_b=ln3_b, conv_w=conv_full,
              conv_b=conv_b, dt_bias=pad8(dt_bias), a_log=pad8(a_log), d_rep=jnp.repeat(d_ssd[0], HP)[None],
              ssd_norm_w=ssd_norm_w, s5_a_re=s5_a_re[0], s5_a_im=s5_a_im[0], s5_log_dt=s5_log_dt[0], s5_b_re=s5_b_re[0],
              s5_b_im=s5_b_im[0], s5_c_re=s5_c_re[0], s5_c_im=s5_c_im[0], s5_d=s5_d, b_glu=b_glu)

    lsum, dx0, dmod, gbig, small = _local_step(x.reshape(t, D), loss_target.reshape(t, D), mod, w, sp, seq,
                                               dist=(gather(("ffn1_w2", "w_in", "w_glu", "w_out")),
                                                     gather(("ffn2_w1", "ffn2_w3", "ffn2_w2")), (ci, chip)))
    loss = lax.psum(lsum * (0.5 / D), ("x", "y", "c"))

    dmod_all = _allgather8("gather_dmod", dmod.reshape(-1, 128)).reshape(ndev * nb, 9 * D)
    dmod_cols = lax.dynamic_slice(dmod_all, (0, chip * ADA_COLS), (ndev * nb, ADA_COLS))
    g_w_ada, g_b_ada = _ada_bwd("ada_bwd", c_all, dmod_cols, dmod_all)

    gbig["w_ada"] = g_w_ada

    outs = {}
    for call, names in (("adamw_a", ("ffn1_w1", "ffn1_w3", "ffn1_w2", "w_in", "w_glu", "w_out")),
                        ("adamw_b", ("ffn2_w1", "ffn2_w3", "ffn2_w2", "w_ada"))):
        res = _adamw(call, [nat(n) for n in names], [gbig[n] for n in names], [nat("m_" + n) for n in names],
                     [nat("v_" + n) for n in names], 8)
        for kind, arrs in zip(("grad", "delta", "new_m", "new_v"), ([gbig[n] for n in names],) + tuple(res)):
            for n, arr in zip(names, arrs):
                outs[kind, n] = (arr.T if n in TRANSPOSED else arr)[None]

    snames = [n for n in SMALL if n != "b_ada"]
    sgrad = dict(small)
    sgrad["b_glu"] = small["w_glu_b"]
    svec = _pack([sgrad[n] for n in snames], 1024).reshape(-1, 128)
    ssum = _sum8("small_sum", _allgather8("gather_small", svec)).reshape(-1)

    def view2d(u):
        s = u.shape[1:]
        return u.reshape((1, s[0]) if len(s) == 1 else (-1, s[-1]))

    vshape = {n: view2d(a[n]).shape for n in SMALL}
    gsm = dict(zip(snames, _unpack(ssum, [vshape[n] if n != "conv_w" else (4, D) for n in snames])))
    gsm["conv_w"] = lax.dynamic_slice(gsm["conv_w"], (0, chip * 256), (4, 256))
    gsm["b_ada"] = g_b_ada
    res = _adamw("adamw_small", [view2d(a[n]) for n in SMALL], [gsm[n] for n in SMALL],
                 [view2d(a["m_" + n]) for n in SMALL], [view2d(a["v_" + n]) for n in SMALL], 1)
    for kind, arrs in zip(("grad", "delta", "new_m", "new_v"), ([gsm[n] for n in SMALL],) + tuple(res)):
        for n, arr in zip(SMALL, arrs):
            outs[kind, n] = arr.reshape(a[n].shape)

    res = [loss, dx0.reshape(nb, seq, D)]
    for kind in ("grad", "delta", "new_m", "new_v"):
        res += [outs[kind, n] for n in WEIGHTS]
    return tuple(res)
```

```python
import functools
import math

import jax
import jax.numpy as jnp
from jax import lax
from jax.experimental import pallas as pl
from jax.experimental.pallas import tpu as pltpu

F32 = jnp.float32
BF16 = jnp.bfloat16
MXU_DTYPE = jnp.bfloat16

D = 1024
FF = 2816
FB = 1408
NH, HP, NS, NG = 8, 64, 128, 2
CH = 128
SW = 512
S5G, S5P, S5H = 32, 64, 16
S5L = S5G * S5P
PW = 2176
ALPHA = 2.0 ** 0.25
LN_EPS = 1e-5
ADAM_LR, ADAM_B1, ADAM_B2, ADAM_EPS, ADAM_WD, ADAM_STEP = 0.001, 0.9, 0.999, 1e-08, 0.01, 10
VMEM_LIMIT = 56 * 1024 * 1024
TM_WIDE = 512
MESH_T = pl.DeviceIdType.MESH


def _pcall(body, **kw):
    return pl.pallas_call(body, **kw)


def _cparams(sem=None, **kw):
    return pltpu.CompilerParams(dimension_semantics=sem, vmem_limit_bytes=VMEM_LIMIT, **kw)


def _dot(a, b):
    return jnp.dot(a, b, preferred_element_type=F32)


def _dot_nt(a, b):
    return lax.dot_general(a, b, (((1,), (1,)), ((), ())), preferred_element_type=F32)


def _dot_hi(a, b):
    return jnp.dot(a, b, preferred_element_type=F32, precision=lax.Precision.HIGHEST)


def _mx(a):
    return a.astype(MXU_DTYPE)


def _sigmoid(x):
    return 1.0 / (1.0 + jnp.exp(-x))


def _iota(shape, axis):
    return lax.broadcasted_iota(jnp.int32, shape, axis)


def _rowcall(name, fn, n_rows, tm, tpe, *, tiled=(), halos=(), exs=(), res=(), out_tiled=(), out_acc=(),
             out_exacc=(), scratch=(), reverse=False, batch=None, carry=None, carry_mid=0.0):
    if batch:
        n_rows //= batch
    nt = n_rows // tm

    def blk(i):
        return (nt - 1 - i) if reverse else i

    in_specs, args = [], []
    for a in tiled:
        if batch:
            in_specs.append(pl.BlockSpec((batch, tm, a.shape[1]), lambda i: (0, blk(i), 0)))
            args.append(a.reshape(batch, n_rows, a.shape[1]))
            continue
        in_specs.append(pl.BlockSpec((tm, a.shape[1]), lambda i: (blk(i), 0)))
        args.append(a)
    for a, rows in halos:
        r = tm // rows
        if batch:
            in_specs.append(pl.BlockSpec((batch, rows, a.shape[1]), lambda i, r=r: (0, jnp.maximum(blk(i) * r - 1, 0), 0)))
            args.append(a.reshape(batch, n_rows, a.shape[1]))
            continue
        in_specs.append(pl.BlockSpec((rows, a.shape[1]), lambda i, r=r: (jnp.maximum(blk(i) * r - 1, 0), 0)))
        args.append(a)
    for a in exs:
        in_specs.append(pl.BlockSpec((1,) + a.shape[1:], lambda i: (blk(i) // tpe, 0, 0)))
        args.append(a)
    for a in res:
        nd = a.ndim
        in_specs.append(pl.BlockSpec(a.shape, lambda i, nd=nd: (0,) * nd, pipeline_mode=pl.Buffered(1)))
        args.append(a)
    any_spec = pl.BlockSpec(memory_space=pl.ANY)
    st_ins = carry.ins if carry else []
    st_outs = carry.out_shapes if carry else []
    st_sems = carry.sem_shapes() if carry else []
    base_in = len(args)
    in_specs += [any_spec] * len(st_ins)
    args += st_ins
    out_specs, out_shape = [], []
    for s in out_tiled:
        if batch:
            out_specs.append(pl.BlockSpec((batch, tm, s.shape[1]), lambda i: (0, blk(i), 0)))
            out_shape.append(_sds((batch, n_rows, s.shape[1]), s.dtype))
            continue
        out_specs.append(pl.BlockSpec((tm, s.shape[1]), lambda i: (blk(i), 0)))
        out_shape.append(s)
    for s in out_acc:
        nd = len(s.shape)
        out_specs.append(pl.BlockSpec(s.shape, lambda i, nd=nd: (0,) * nd))
        out_shape.append(s)
    for s in out_exacc:
        out_specs.append(pl.BlockSpec((1,) + s.shape[1:], lambda i: (blk(i) // tpe, 0, 0)))
        out_shape.append(s)
    base_out = len(out_shape)
    out_specs += [any_spec] * len(st_outs)
    out_shape += st_outs
    aliases = {base_in + k: base_out + v for k, v in carry.aliases.items()} if carry else {}
    n = [len(tiled), len(halos), len(exs), len(res), len(st_ins), len(out_tiled), len(out_acc), len(out_exacc),
         len(st_outs), len(scratch), len(st_sems)]

    def body(*refs):
        groups, k = [], 0
        for m in n:
            groups.append(refs[k:k + m])
            k += m
        i = pl.program_id(0)
        b = blk(i)

        class ctx:
            first = i == 0
            pos = b % tpe
            seq_first = (b % tpe) == 0
            seq_last = (b % tpe) == tpe - 1
            ex_enter = (i % tpe) == 0

        def work():
            for cond, refs_ in ((ctx.first, groups[6]), (ctx.ex_enter, groups[7])):
                if refs_:
                    @pl.when(cond)
                    def _():
                        for r in refs_:
                            r[...] = jnp.zeros(r.shape, r.dtype)
            fn(ctx, *groups[0:4], *groups[5:8], groups[9])

        if carry:
            carry.run((groups[4], groups[8], groups[10]), i == 0, i == min(nt - 1, int(carry_mid * nt)), i == nt - 1, work)
        else:
            work()

    outs = _pcall(body, name=name, grid=(nt,), in_specs=in_specs, out_specs=out_specs, out_shape=out_shape,
                  input_output_aliases=aliases, scratch_shapes=list(scratch) + st_sems,
                  compiler_params=_cparams(("arbitrary",)))(*args)
    if batch:
        outs = [o.reshape(batch * n_rows, o.shape[2]) if k < len(out_tiled) else o for k, o in enumerate(outs)]
    return (outs[:base_out], outs[base_out:]) if carry else outs


def _acc(ref, val):
    ref[...] += val


def _sds(shape, dtype=F32):
    return jax.ShapeDtypeStruct(shape, dtype)


def _ln_fwd(r, g, b):
    mu = jnp.mean(r, axis=-1, keepdims=True)
    rc = r - mu
    var = jnp.mean(rc * rc, axis=-1, keepdims=True)
    return rc * lax.rsqrt(var + LN_EPS) * g + b


def _ln_bwd(r, g, dy):
    mu = jnp.mean(r, axis=-1, keepdims=True)
    rc = r - mu
    var = jnp.mean(rc * rc, axis=-1, keepdims=True)
    rstd = lax.rsqrt(var + LN_EPS)
    xhat = rc * rstd
    dxh = dy * g
    dr = rstd * (dxh - jnp.mean(dxh, axis=-1, keepdims=True) - xhat * jnp.mean(dxh * xhat, axis=-1, keepdims=True))
    return dr, jnp.sum(dy * xhat, axis=0, keepdims=True), jnp.sum(dy, axis=0, keepdims=True)


def _ffn_fwd_hidden(name, x, mod, k0, w1, w3, seq, **carry):
    t = x.shape[0]
    tm = min(TM_WIDE, seq)

    def fn(ctx, tiled, halos, exs, res, outs, accs, exaccs, scr):
        mod_ref, = exs
        w1_ref, w3_ref = res
        h_ref, ab_ref, s_ref = outs
        sh, sc = mod_ref[0, k0:k0 + 1, :], mod_ref[0, k0 + 1:k0 + 2, :]
        h = _mx(tiled[0][...] * (1.0 + sc) + sh)
        h_ref[...] = h
        for j in range(2):
            a = _dot_nt(h, w1_ref[j * FB:(j + 1) * FB, :])
            b = _dot_nt(h, w3_ref[j * FB:(j + 1) * FB, :])
            ab_ref[:, j * FB:(j + 1) * FB] = _mx(a)
            ab_ref[:, FF + j * FB:FF + (j + 1) * FB] = _mx(b)
            s_ref[:, j * FB:(j + 1) * FB] = _mx(a * _sigmoid(a) * b)

    return _rowcall(name, fn, t, tm, seq // tm, tiled=[x], exs=[mod], res=[w1, w3],
                    out_tiled=[_sds((t, D), MXU_DTYPE), _sds((t, 2 * FF), MXU_DTYPE), _sds((t, FF), MXU_DTYPE)], **carry)


def _ffn_fwd_out(name, s, x, mod, k0, w2, lng, lnb, seq, tgt=None, **carry):
    t = x.shape[0]
    tm = min(TM_WIDE, seq)
    with_loss = tgt is not None

    def fn(ctx, tiled, halos, exs, res, outs, accs, exaccs, scr):
        mod_ref, = exs
        w2_ref, g_ref, b_ref = res
        xo_ref, r_ref, f_ref = outs[:3]
        f = _dot(tiled[0][...], w2_ref[...])
        f_ref[...] = f
        r = ALPHA * tiled[1][...] + 0.5 * mod_ref[0, k0 + 2:k0 + 3, :] * f
        r_ref[...] = r
        xo = _ln_fwd(r, g_ref[...], b_ref[...])
        xo_ref[...] = xo
        if with_loss:
            e = xo - tiled[2][...]
            outs[3][...] = e * (1.0 / D)
            _acc(accs[0], jnp.sum(e * e) * jnp.ones((8, 128), F32))

    return _rowcall(name, fn, t, tm, seq // tm, tiled=[s, x] + ([tgt] if with_loss else []), exs=[mod],
                    res=[w2, lng, lnb], out_tiled=[_sds((t, D))] * (4 if with_loss else 3),
                    out_acc=[_sds((8, 128))] if with_loss else [], **carry)


def _ffn_bwd1(name, dxo, r, ab, f, mod, k0, lng, w2, seq, **carry):
    t = dxo.shape[0]
    tm = 256

    def fn(ctx, tiled, halos, exs, res, outs, accs, exaccs, scr):
        dxo_ref, r_ref, ab_ref, f_ref = tiled
        mod_ref, = exs
        g_ref, w2_ref = res
        dr_ref, df_ref, dab_ref = outs
        g = mod_ref[0, k0 + 2:k0 + 3, :]
        dr, dgam, dbet = _ln_bwd(r_ref[...], g_ref[...], dxo_ref[...])
        dr_ref[...] = dr
        _acc(accs[0], dgam)
        _acc(accs[1], dbet)
        _acc(exaccs[0].at[0], jnp.sum(0.5 * f_ref[...] * dr, axis=0, keepdims=True))
        df = _mx(0.5 * g * dr)
        df_ref[...] = df
        for j in range(2):
            ds = _dot_nt(df, w2_ref[j * FB:(j + 1) * FB, :])
            a = ab_ref[:, j * FB:(j + 1) * FB].astype(F32)
            b = ab_ref[:, FF + j * FB:FF + (j + 1) * FB].astype(F32)
            sig = _sigmoid(a)
            dab_ref[:, j * FB:(j + 1) * FB] = _mx(ds * b * (sig * (1.0 + a * (1.0 - sig))))
            dab_ref[:, FF + j * FB:FF + (j + 1) * FB] = _mx(ds * (a * sig))

    b = mod.shape[0]
    return _rowcall(name, fn, t, tm, seq // tm, tiled=[dxo, r, ab, f], exs=[mod], res=[lng, w2],
                    out_tiled=[_sds((t, D)), _sds((t, D), MXU_DTYPE), _sds((t, 2 * FF), MXU_DTYPE)],
                    out_acc=[_sds((1, D)), _sds((1, D))], out_exacc=[_sds((b, 1, D))], **carry)


def _ffn_bwd(name, dxo, r, ab, f, x, mod, k0, lng, w2, w1t, w3t, seq):
    t = dxo.shape[0]
    tm = 256

    def fn(ctx, tiled, halos, exs, res, outs, accs, exaccs, scr):
        dxo_ref, r_ref, ab_ref, f_ref, x_ref = tiled
        mod_ref, = exs
        g_ref, w2_ref, w1t_ref, w3t_ref = res
        dx_ref, df_ref, dab_ref = outs
        sc, g = mod_ref[0, k0 + 1:k0 + 2, :], mod_ref[0, k0 + 2:k0 + 3, :]
        dr, dgam, dbet = _ln_bwd(r_ref[...], g_ref[...], dxo_ref[...])
        _acc(accs[0], dgam)
        _acc(accs[1], dbet)
        _acc(exaccs[0].at[0], jnp.sum(0.5 * f_ref[...] * dr, axis=0, keepdims=True))
        df = _mx(0.5 * g * dr)
        df_ref[...] = df
        dh = jnp.zeros((tm, D), F32)
        for j in range(2):
            blk = slice(j * FB, (j + 1) * FB)
            ds = _dot_nt(df, w2_ref[blk, :])
            a = ab_ref[:, blk].astype(F32)
            b = ab_ref[:, FF + j * FB:FF + (j + 1) * FB].astype(F32)
            sig = _sigmoid(a)
            da = _mx(ds * b * (sig * (1.0 + a * (1.0 - sig))))
            db = _mx(ds * (a * sig))
            dab_ref[:, blk] = da
            dab_ref[:, FF + j * FB:FF + (j + 1) * FB] = db
            dh = dh + _dot(da, w1t_ref[blk, :]) + _dot(db, w3t_ref[blk, :])
        dx_ref[...] = ALPHA * dr + dh * (1.0 + sc)
        _acc(exaccs[1].at[0], jnp.sum(dh, axis=0, keepdims=True))
        _acc(exaccs[2].at[0], jnp.sum(dh * x_ref[...], axis=0, keepdims=True))

    b = mod.shape[0]
    return _rowcall(name, fn, t, tm, seq // tm, tiled=[dxo, r, ab, f, x], exs=[mod], res=[lng, w2, w1t, w3t],
                    out_tiled=[_sds((t, D)), _sds((t, D), MXU_DTYPE), _sds((t, 2 * FF), MXU_DTYPE)],
                    out_acc=[_sds((1, D)), _sds((1, D))], out_exacc=[_sds((b, 1, D))] * 3)


def _mod_bwd(name, dab, dr, x, mod, k0, wts, seq, extra=(), nt=False, **carry):
    t = dr.shape[0]
    tm = min(TM_WIDE, seq)
    nin = 1 + len(extra)
    width = dab.shape[1] + sum(e.shape[1] for e in extra)

    def fn(ctx, tiled, halos, exs, res, outs, accs, exaccs, scr):
        parts = tiled[:nin]
        dr_ref, x_ref = tiled[nin:]
        mod_ref, = exs
        sc = mod_ref[0, k0 + 1:k0 + 2, :]
        if nin == 1:
            dp = parts[0][...]
        else:
            dp = jnp.concatenate([_mx(p[...]) for p in parts], axis=1)
            outs[1][...] = dp
        dh, off = jnp.zeros((tm, D), F32), 0
        for w_ref in res:
            k = w_ref.shape[1 if nt else 0]
            dh = dh + (_dot_nt if nt else _dot)(dp[:, off:off + k], w_ref[...])
            off += k
        outs[0][...] = ALPHA * dr_ref[...] + dh * (1.0 + sc)
        _acc(exaccs[0].at[0], jnp.sum(dh, axis=0, keepdims=True))
        _acc(exaccs[1].at[0], jnp.sum(dh * x_ref[...], axis=0, keepdims=True))

    b = mod.shape[0]
    out_tiled = [_sds((t, D))] + ([_sds((t, width), MXU_DTYPE)] if nin > 1 else [])
    return _rowcall(name, fn, t, tm, seq // tm, tiled=[dab, *extra, dr, x], exs=[mod], res=list(wts),
                    out_tiled=out_tiled, out_exacc=[_sds((b, 1, D)), _sds((b, 1, D))], **carry)


def _tn_matmul(name, a, b, bm, bn, bt=1024, carry=None, a_cols=None):
    t = a.shape[0]
    bt = min(bt, t)
    start, m = a_cols or (0, a.shape[1])
    off = start // bm
    n = b.shape[1]
    grid = (m // bm, n // bn, t // bt)
    n_in, n_out = (len(carry.ins), len(carry.out_shapes)) if carry else (0, 0)

    def body(a_ref, b_ref, *refs):
        o_ref = refs[n_in]

        def work():
            @pl.when(pl.program_id(2) == 0)
            def _():
                o_ref[...] = jnp.zeros(o_ref.shape, F32)
            o_ref[...] += lax.dot_general(a_ref[...], b_ref[...], (((0,), (0,)), ((), ())), preferred_element_type=F32)

        if not carry:
            return work()
        step = (pl.program_id(0) * grid[1] + pl.program_id(1)) * grid[2] + pl.program_id(2)
        carry.run((refs[:n_in], refs[n_in + 1:n_in + 1 + n_out], refs[n_in + 1 + n_out:]), step == 0, step == 0,
                  step == grid[0] * grid[1] * grid[2] - 1, work)

    any_spec = pl.BlockSpec(memory_space=pl.ANY)
    outs = _pcall(body, name=name, grid=grid,
                  in_specs=[pl.BlockSpec((bt, bm), lambda i, j, k: (k, i + off)), pl.BlockSpec((bt, bn), lambda i, j, k: (k, j))]
                  + [any_spec] * n_in,
                  out_specs=[pl.BlockSpec((bm, bn), lambda i, j, k: (i, j))] + [any_spec] * n_out,
                  out_shape=[_sds((m, n))] + (carry.out_shapes if carry else []),
                  input_output_aliases={2 + k: 1 + v for k, v in carry.aliases.items()} if carry else {},
                  scratch_shapes=carry.sem_shapes() if carry else [],
                  compiler_params=_cparams(("arbitrary",) * 3 if carry else ("parallel", "parallel", "arbitrary")))(
                      a, b, *(carry.ins if carry else []))
    return (outs[0], outs[1:]) if carry else outs[0]


S5_BLK = 16


def _tn_diag(name, a, b, bt=1024):
    t = a.shape[0]
    bt = min(bt, t)
    rows, cols = S5_BLK * S5P, S5_BLK * S5H
    nblk = a.shape[1] // rows

    def body(a_ref, b_ref, o_ref):
        @pl.when(pl.program_id(1) == 0)
        def _():
            o_ref[...] = jnp.zeros(o_ref.shape, F32)
        o_ref[0] += lax.dot_general(a_ref[...], b_ref[...], (((0,), (0,)), ((), ())), preferred_element_type=F32)

    return _pcall(body, name=name, grid=(nblk, t // bt),
                  in_specs=[pl.BlockSpec((bt, rows), lambda i, k: (k, i)),
                            pl.BlockSpec((bt, cols), lambda i, k: (k, i % (S5G // S5_BLK)))],
                  out_specs=pl.BlockSpec((1, rows, cols), lambda i, k: (i, 0, 0)), out_shape=_sds((nblk, rows, cols)),
                  compiler_params=_cparams(("parallel", "arbitrary")))(a, b)


def _diag_groups(o):
    o = o.reshape(2, S5G // S5_BLK, S5_BLK, S5P, S5_BLK, S5H)
    return jnp.einsum("rbgpgh->rbgph", o).reshape(2, S5G, S5P, S5H)


def _proj_fwd(name, x, mod, w_in, seq):
    t = x.shape[0]
    tm = min(TM_WIDE, seq)

    def fn(ctx, tiled, halos, exs, res, outs, accs, exaccs, scr):
        mod_ref, = exs
        sh, sc = mod_ref[0, 3:4, :], mod_ref[0, 4:5, :]
        h = _mx(tiled[0][...] * (1.0 + sc) + sh)
        outs[0][...] = h
        outs[1][...] = _dot(h, res[0][...])

    return _rowcall(name, fn, t, tm, seq // tm, tiled=[x], exs=[mod], res=[w_in],
                    out_tiled=[_sds((t, D), MXU_DTYPE), _sds((t, PW))])


def _shift_rows(cur, prev8, j):
    if j == 0:
        return cur
    rolled = pltpu.roll(cur, j, 0)
    top = jnp.where(_iota((8, cur.shape[1]), 0) < j, pltpu.roll(prev8, j, 0), rolled[0:8])
    return jnp.concatenate([top, rolled[8:]], axis=0)


def _shift_rows_up(cur, next8, j):
    if j == 0:
        return cur
    n = cur.shape[0]
    rolled = pltpu.roll(cur, n - j, 0)
    bot = jnp.where(_iota((8, cur.shape[1]), 0) >= 8 - j, pltpu.roll(next8, 8 - j, 0), rolled[n - 8:n])
    return jnp.concatenate([rolled[:n - 8], bot], axis=0)


def _softplus(x):
    return jnp.maximum(x, 0.0) + jnp.log(1.0 + jnp.exp(-jnp.abs(x)))


def _ssd_common(proj_ref, xpre, dtb_ref, alog_ref):
    xbc = xpre * _sigmoid(xpre)
    xs, bm, cm = xbc[:, 0:SW], xbc[:, SW:SW + 256], xbc[:, SW + 256:SW + 512]
    dtraw = proj_ref[:, PW - 128:PW] + dtb_ref[...]
    dt = _softplus(dtraw)
    a = -jnp.exp(alog_ref[...])
    tril = (_iota((CH, CH), 0) >= _iota((CH, CH), 1)).astype(F32)
    acs = _dot_hi(tril, dt * a)
    return xs, bm, cm, dtraw, dt, a, acs, acs.T


def _pair_lane(lo, hi):
    r = lo.shape[0]
    return jnp.where(_iota((r, 128), 1) < HP, lo, hi)


def _ssd_fwd(name, proj, conv_w, conv_b, dt_bias, a_log, d_rep, norm_w, seq):
    t = proj.shape[0]
    nb = t // seq

    def fn(ctx, tiled, halos, exs, res, outs, accs, exaccs, scr):
        @pl.when(ctx.seq_first)
        def _():
            scr[0][...] = jnp.zeros(scr[0].shape, F32)

        for b in range(nb):
            one(ctx, res, [r.at[b] for r in tiled + halos + outs + scr])

    def one(ctx, res, refs):
        proj_ref, halo_ref, yo_ref, xpre_ref, y_ref, sprev_ref, state_ref = refs
        cw_ref, cb_ref, dtb_ref, alog_ref, d_ref, nw_ref = res
        raw = proj_ref[:, SW:SW + D]
        prev8 = halo_ref[:, SW:SW + D] * jnp.where(ctx.seq_first, 0.0, 1.0)
        xpre = cb_ref[...] + cw_ref[3:4, :] * raw
        for j in (1, 2, 3):
            xpre = xpre + cw_ref[3 - j:4 - j, :] * _shift_rows(raw, prev8, j)
        xpre_ref[...] = xpre
        xs, bm, cm, dtraw, dt, a, acs, acst = _ssd_common(proj_ref, xpre, dtb_ref, alog_ref)
        causal = _iota((CH, CH), 0) >= _iota((CH, CH), 1)
        lane_lo = _iota((CH, 128), 1) < HP
        sprev = state_ref[...]
        sprev_ref[...] = sprev
        ys = []
        for g in range(NG):
            bmg, cmg = bm[:, g * NS:(g + 1) * NS], cm[:, g * NS:(g + 1) * NS]
            bmt = bmg.T
            cb = _dot(_mx(cmg), _mx(bmt))
            for q in (2 * g, 2 * g + 1):
                xsq = xs[:, 128 * q:128 * q + 128]
                xd = xsq * _pair_lane(dt[:, 2 * q:2 * q + 1], dt[:, 2 * q + 1:2 * q + 2])
                sp = sprev[:, 128 * q:128 * q + 128]
                ydiag = jnp.zeros((CH, 128), F32)
                snew = jnp.zeros((NS, 128), F32)
                for jj in range(2):
                    h = 2 * q + jj
                    col, row = acs[:, h:h + 1], acst[h:h + 1, :]
                    lm = jnp.where(causal, jnp.exp(jnp.minimum(col - row, 0.0)), 0.0)
                    xm = _mx(jnp.where(lane_lo if jj == 0 else ~lane_lo, xd, 0.0))
                    ydiag = ydiag + _dot(_mx(cb * lm), xm)
                    dec_row = jnp.exp(acst[h:h + 1, CH - 1:CH] - row)
                    snew = snew + _dot(_mx(bmt * dec_row), xm)
                e_pair = jnp.exp(_pair_lane(acs[:, 2 * q:2 * q + 1], acs[:, 2 * q + 1:2 * q + 2]))
                yoff = _dot(_mx(cmg), _mx(sp)) * e_pair
                cd = jnp.exp(_pair_lane(acst[2 * q:2 * q + 1, CH - 1:CH], acst[2 * q + 1:2 * q + 2, CH - 1:CH]))
                state_ref[:, 128 * q:128 * q + 128] = cd * sp + snew
                ys.append(ydiag + yoff + d_ref[:, 128 * q:128 * q + 128] * xsq)
        y = jnp.concatenate(ys, axis=1)
        y_ref[...] = y
        z = proj_ref[:, 0:SW]
        yz = y * (z * _sigmoid(z))
        outp = []
        for g in range(NG):
            seg = yz[:, 256 * g:256 * g + 256]
            rinv = lax.rsqrt(jnp.mean(seg * seg, axis=-1, keepdims=True) + LN_EPS)
            outp.append(seg * rinv * nw_ref[:, 256 * g:256 * g + 256])
        yo_ref[...] = _mx(jnp.concatenate(outp, axis=1))

    return _rowcall(name, fn, t, CH, seq // CH, tiled=[proj], halos=[(proj, 8)],
                    res=[conv_w, conv_b, dt_bias, a_log, d_rep, norm_w],
                    out_tiled=[_sds((t, SW), MXU_DTYPE), _sds((t, D)), _sds((t, SW)), _sds((t, SW))],
                    scratch=[pltpu.VMEM((nb, NS, SW), F32)], batch=nb)


def _ssd_bwd(name, dyo, proj, xpre_all, y_all, sprev_all, conv_w, dt_bias, a_log, d_rep, norm_w, seq, **carry):
    t = proj.shape[0]
    nb = t // seq

    def fn(ctx, tiled, halos, exs, res, outs, accs, exaccs, scr):
        @pl.when(ctx.seq_last)
        def _():
            for r in scr:
                r[...] = jnp.zeros(r.shape, F32)

        for b in range(nb):
            one(ctx, res, accs, [r.at[b] for r in tiled + halos + outs + scr])

    def one(ctx, res, accs, refs):
        dyo_ref, proj_ref, xpre_ref, y_ref, sprev_ref, halo_ref, dzx_ref, ddt_ref, ds_ref, nxt_ref = refs
        cw_ref, dtb_ref, alog_ref, d_ref, nw_ref = res
        dnw_acc, dd_acc, dcw_acc, dcb_acc, ddtb_acc, dalog_acc = accs
        xpre = xpre_ref[...]
        xs, bm, cm, dtraw, dt, a, acs, acst = _ssd_common(proj_ref, xpre, dtb_ref, alog_ref)
        y = y_ref[...]
        z = proj_ref[:, 0:SW]
        sz = _sigmoid(z)
        siluz = z * sz
        yz = y * siluz
        dyo = dyo_ref[...]
        dyz_parts, dnw_parts = [], []
        for g in range(NG):
            seg = yz[:, 256 * g:256 * g + 256]
            rinv = lax.rsqrt(jnp.mean(seg * seg, axis=-1, keepdims=True) + LN_EPS)
            yn = seg * rinv
            dseg = dyo[:, 256 * g:256 * g + 256]
            dnw_parts.append(jnp.sum(dseg * yn, axis=0, keepdims=True))
            dyn = dseg * nw_ref[:, 256 * g:256 * g + 256]
            dyz_parts.append(rinv * (dyn - yn * jnp.mean(dyn * yn, axis=-1, keepdims=True)))
        dyz = jnp.concatenate(dyz_parts, axis=1)
        _acc(dnw_acc, jnp.concatenate(dnw_parts, axis=1))
        dy = dyz * siluz
        dz = dyz * y * (sz * (1.0 + z * (1.0 - sz)))
        _acc(dd_acc, jnp.sum(dy * xs, axis=0, keepdims=True))

        causal = _iota((CH, CH), 0) >= _iota((CH, CH), 1)
        anti = _iota((CH, CH), 0) <= _iota((CH, CH), 1)
        lane_lo = _iota((CH, 128), 1) < HP
        lane_id = _iota((CH, 128), 1)
        last_row = _iota((CH, 128), 0) == CH - 1
        sprev = sprev_ref[...]
        dacs = jnp.zeros((CH, 128), F32)
        ddt_x = jnp.zeros((CH, 128), F32)
        dxs_parts, dbm_parts, dcm_parts = [], [], []
        for g in range(NG):
            bmg, cmg = bm[:, g * NS:(g + 1) * NS], cm[:, g * NS:(g + 1) * NS]
            bmt, cmt = bmg.T, cmg.T
            cb = _dot(_mx(cmg), _mx(bmt))
            cbt = _dot(_mx(bmg), _mx(cmt))
            dcb = jnp.zeros((CH, CH), F32)
            dcbt = jnp.zeros((CH, CH), F32)
            dbmg = jnp.zeros((CH, NS), F32)
            dcmg = jnp.zeros((CH, NS), F32)
            for q in (2 * g, 2 * g + 1):
                sl = slice(128 * q, 128 * q + 128)
                xsq = xs[:, sl]
                dtp = _pair_lane(dt[:, 2 * q:2 * q + 1], dt[:, 2 * q + 1:2 * q + 2])
                xd = xsq * dtp
                dyq = dy[:, sl]
                sp = sprev[:, sl]
                dsn = ds_ref[:, sl]
                e_pair = jnp.exp(_pair_lane(acs[:, 2 * q:2 * q + 1], acs[:, 2 * q + 1:2 * q + 2]))
                cd = jnp.exp(_pair_lane(acst[2 * q:2 * q + 1, CH - 1:CH], acst[2 * q + 1:2 * q + 2, CH - 1:CH]))
                dye = dyq * e_pair
                dcmg = dcmg + _dot(_mx(dye), _mx(sp.T))
                dsp = _dot(_mx(cmt), _mx(dye)) + cd * dsn
                yoff = _dot(_mx(cmg), _mx(sp)) * e_pair
                dacs_lane = dyq * yoff
                dxd = jnp.zeros((CH, 128), F32)
                sds = jnp.sum(dsn * sp, axis=0, keepdims=True) * cd
                for jj in range(2):
                    h = 2 * q + jj
                    hm = lane_lo if jj == 0 else ~lane_lo
                    col, row = acs[:, h:h + 1], acst[h:h + 1, :]
                    lm = jnp.where(causal, jnp.exp(jnp.minimum(col - row, 0.0)), 0.0)
                    lmt = jnp.where(anti, jnp.exp(jnp.minimum(row - col, 0.0)), 0.0)
                    xm = _mx(jnp.where(hm, xd, 0.0))
                    dym = _mx(jnp.where(hm, dyq, 0.0))
                    gm = _dot_nt(dym, xm)
                    gmt = _dot_nt(xm, dym)
                    dcb = dcb + gm * lm
                    dcbt = dcbt + gmt * lmt
                    dxd = dxd + _dot(_mx(cbt * lmt), dym)
                    w = gm * cb * lm
                    wt = gmt * cbt * lmt
                    dacs_h = jnp.sum(w, axis=1, keepdims=True) - jnp.sum(wt, axis=1, keepdims=True)
                    alast = acst[h:h + 1, CH - 1:CH]
                    dec_col = jnp.exp(alast - col)
                    dsm = _mx(jnp.where(hm[0:NS], dsn, 0.0))
                    qh = _dot_nt(xm, dsm)
                    dbmg = dbmg + qh * dec_col
                    ddec = jnp.sum(qh * bmg, axis=1, keepdims=True)
                    dxd = dxd + _dot(_mx(bmg * dec_col), dsm)
                    dacs_h = dacs_h - ddec * dec_col
                    dacs_h = dacs_h + jnp.sum(jnp.where(hm, dacs_lane, 0.0), axis=1, keepdims=True)
                    tail = jnp.sum(ddec * dec_col, axis=0, keepdims=True) + jnp.sum(
                        jnp.where(hm[0:1], sds, 0.0), axis=1, keepdims=True)
                    dacs = dacs + jnp.where(lane_id == h, dacs_h, 0.0) + jnp.where(
                        last_row & (lane_id == h), tail, 0.0)
                ds_ref[:, sl] = dsp
                for jj in range(2):
                    h = 2 * q + jj
                    hm = lane_lo if jj == 0 else ~lane_lo
                    ddt_x = ddt_x + jnp.where(lane_id == h, jnp.sum(jnp.where(hm, dxd * xsq, 0.0), axis=1, keepdims=True), 0.0)
                dxs_parts.append(dxd * dtp + d_ref[:, sl] * dyq)
            dcmg = dcmg + _dot(_mx(dcb), _mx(bmg))
            dbmg = dbmg + _dot(_mx(dcbt), _mx(cmg))
            dbm_parts.append(dbmg)
            dcm_parts.append(dcmg)
        triu = (_iota((CH, CH), 0) <= _iota((CH, CH), 1)).astype(F32)
        dadt = _dot_hi(triu, dacs)
        ddt = dadt * a + ddt_x
        _acc(dalog_acc, jnp.sum(dadt * dt, axis=0, keepdims=True) * a)
        ddtraw = ddt * _sigmoid(dtraw)
        ddt_ref[...] = ddtraw
        _acc(ddtb_acc, jnp.sum(ddtraw, axis=0, keepdims=True))
        dxbc = jnp.concatenate(dxs_parts + dbm_parts + dcm_parts, axis=1)
        sx = _sigmoid(xpre)
        dpre = dxbc * (sx * (1.0 + xpre * (1.0 - sx)))
        _acc(dcb_acc, jnp.sum(dpre, axis=0, keepdims=True))
        raw = proj_ref[:, SW:SW + D]
        prev8 = halo_ref[:, SW:SW + D] * jnp.where(ctx.seq_first, 0.0, 1.0)
        next8 = nxt_ref[...]
        draw = cw_ref[3:4, :] * dpre
        dcw = [None] * 4
        dcw[3] = jnp.sum(dpre * raw, axis=0, keepdims=True)
        for j in (1, 2, 3):
            dcw[3 - j] = jnp.sum(dpre * _shift_rows(raw, prev8, j), axis=0, keepdims=True)
            draw = draw + cw_ref[3 - j:4 - j, :] * _shift_rows_up(dpre, next8, j)
        _acc(dcw_acc, jnp.concatenate(dcw + [jnp.zeros((4, D), F32)], axis=0))
        nxt_ref[...] = dpre[0:8]
        dzx_ref[:, 0:SW] = dz
        dzx_ref[:, SW:SW + D] = draw

    return _rowcall(name, fn, t, CH, seq // CH, tiled=[dyo, proj, xpre_all, y_all, sprev_all], halos=[(proj, 8)],
                    res=[conv_w, dt_bias, a_log, d_rep, norm_w],
                    out_tiled=[_sds((t, SW + D)), _sds((t, 128))],
                    out_acc=[_sds((1, SW)), _sds((1, SW)), _sds((8, D)), _sds((1, D)), _sds((1, 128)), _sds((1, 128))],
                    scratch=[pltpu.VMEM((nb, NS, SW), F32), pltpu.VMEM((nb, 8, D), F32)], reverse=True, batch=nb, **carry)


def _gelu(y):
    k = math.sqrt(2.0 / math.pi)
    return 0.5 * y * (1.0 + jnp.tanh(k * (y + 0.044715 * y * y * y)))


def _gelu_grad(y):
    k = math.sqrt(2.0 / math.pi)
    th = jnp.tanh(k * (y + 0.044715 * y * y * y))
    return 0.5 * (1.0 + th) + 0.5 * y * (1.0 - th * th) * k * (1.0 + 3.0 * 0.044715 * y * y)


S5T = 256


def _cmul_add(xr, xi, ar, ai, sr, si):
    return xr + ar * sr - ai * si, xi + ar * si + ai * sr


def _s5_fwd(name, proj, bbd, cbd, pw, tab, d5, w_glu, b_glu, seq, **carry):
    t = proj.shape[0]
    tm = S5T

    def fn(ctx, tiled, halos, exs, res, outs, accs, exaccs, scr):
        proj_ref, = tiled
        bbd_ref, cbd_ref, pw_ref, tab_ref, d_ref, wg_ref, bg_ref = res
        out_ref, xst_ref, y_ref, xb_ref, ub_ref = outs
        carry_ref, = scr

        @pl.when(ctx.seq_first)
        def _():
            carry_ref[...] = jnp.zeros(carry_ref.shape, F32)

        u = proj_ref[:, 1536:2048]
        bu = _dot(_mx(u), bbd_ref[...])
        xr, xi = bu[:, :S5L].reshape(tm // 8, 8, S5L), bu[:, S5L:].reshape(tm // 8, 8, S5L)
        for k, sh in enumerate((1, 2, 4)):
            xr, xi = _cmul_add(xr, xi, pw_ref[k, :, :S5L], pw_ref[k, :, S5L:], pltpu.roll(xr, sh, 1), pltpu.roll(xi, sh, 1))
        xst_ref[:, :S5L] = xr.reshape(tm, S5L)
        xst_ref[:, S5L:] = xi.reshape(tm, S5L)

        def tile_fix(i, c):
            cr, ci = c
            rows = pl.ds(pl.multiple_of(i * 8, 8), 8)
            tr, ti = _cmul_add(xst_ref[rows, :S5L], xst_ref[rows, S5L:], tab_ref[:, :S5L], tab_ref[:, S5L:], cr, ci)
            xst_ref[rows, :S5L] = tr
            xst_ref[rows, S5L:] = ti
            return tr[7:8], ti[7:8]

        cr, ci = lax.fori_loop(0, tm // 8, tile_fix, (carry_ref[0:1, :S5L], carry_ref[0:1, S5L:]))
        carry_ref[0:1, :S5L] = cr
        carry_ref[0:1, S5L:] = ci
        xb = _mx(xst_ref[...])
        xb_ref[...] = xb
        ub_ref[...] = _mx(u)
        y = _dot(xb, cbd_ref[...]) + u * d_ref[...]
        y_ref[...] = y
        g = _gelu(y)
        v = _dot(_mx(g), wg_ref[...]) + bg_ref[...]
        out_ref[...] = _mx(g * _sigmoid(v))

    return _rowcall(name, fn, t, tm, seq // tm, tiled=[proj], res=[bbd, cbd, pw, tab, d5, w_glu, b_glu],
                    out_tiled=[_sds((t, SW), MXU_DTYPE), _sds((t, 2 * S5L)), _sds((t, SW)),
                               _sds((t, 2 * S5L), MXU_DTYPE), _sds((t, SW), MXU_DTYPE)],
                    scratch=[pltpu.VMEM((8, 2 * S5L), F32)], **carry)


def _s5_bwd(name, dout, proj, xst, y_all, bbdt, cbdt, pwc, tabc, d5, w_glu, b_glu, seq, **carry):
    t = proj.shape[0]
    tm = S5T

    def fn(ctx, tiled, halos, exs, res, outs, accs, exaccs, scr):
        dout_ref, proj_ref, xst_ref, y_ref = tiled
        halo_ref, = halos
        bbdt_ref, cbdt_ref, pw_ref, tab_ref, d_ref, wg_ref, bg_ref = res
        du_ref, lam_ref, dyb_ref, gb_ref, dvb_ref = outs
        da_acc, dd_acc, dbg_acc = accs
        carry_ref, lamf_ref = scr

        @pl.when(ctx.seq_last)
        def _():
            carry_ref[...] = jnp.zeros(carry_ref.shape, F32)

        u = proj_ref[:, 1536:2048]
        y = y_ref[...]
        g = _gelu(y)
        v = _dot(_mx(g), wg_ref[...]) + bg_ref[...]
        sg = _sigmoid(v)
        dout = dout_ref[...]
        dv = dout * g * sg * (1.0 - sg)
        dvb = _mx(dv)
        dvb_ref[...] = dvb
        gb_ref[...] = _mx(g)
        _acc(dbg_acc, jnp.sum(dv, axis=0, keepdims=True))
        dg = dout * sg + _dot_nt(dvb, wg_ref[...])
        dy = dg * _gelu_grad(y)
        dyb = _mx(dy)
        dyb_ref[...] = dyb
        _acc(dd_acc, jnp.sum(dy * u, axis=0, keepdims=True))
        dx = _dot(dyb, cbdt_ref[...])
        xr, xi = dx[:, :S5L].reshape(tm // 8, 8, S5L), dx[:, S5L:].reshape(tm // 8, 8, S5L)
        for k, sh in enumerate((1, 2, 4)):
            xr, xi = _cmul_add(xr, xi, pw_ref[k, :, :S5L], pw_ref[k, :, S5L:], pltpu.roll(xr, 8 - sh, 1),
                               pltpu.roll(xi, 8 - sh, 1))
        lamf_ref[:, :S5L] = xr.reshape(tm, S5L)
        lamf_ref[:, S5L:] = xi.reshape(tm, S5L)

        def tile_fix(i, c):
            cr, ci = c
            rows = pl.ds(pl.multiple_of((tm // 8 - 1 - i) * 8, 8), 8)
            tr, ti = _cmul_add(lamf_ref[rows, :S5L], lamf_ref[rows, S5L:], tab_ref[:, :S5L], tab_ref[:, S5L:], cr, ci)
            lamf_ref[rows, :S5L] = tr
            lamf_ref[rows, S5L:] = ti
            return tr[0:1], ti[0:1]

        cr, ci = lax.fori_loop(0, tm // 8, tile_fix, (carry_ref[0:1, :S5L], carry_ref[0:1, S5L:]))
        carry_ref[0:1, :S5L] = cr
        carry_ref[0:1, S5L:] = ci
        lam = lamf_ref[...]
        lamb = _mx(lam)
        lam_ref[...] = lamb
        du_ref[...] = dy * d_ref[...] + _dot(lamb, bbdt_ref[...])
        prev8 = halo_ref[...] * jnp.where(ctx.seq_first, 0.0, 1.0)
        xprev = _shift_rows(xst_ref[...], prev8, 1)
        lr, li = lam[:, :S5L], lam[:, S5L:]
        pr, pi = xprev[:, :S5L], xprev[:, S5L:]
        dar = jnp.sum(lr * pr + li * pi, axis=0, keepdims=True)
        dai = jnp.sum(li * pr - lr * pi, axis=0, keepdims=True)
        _acc(da_acc, jnp.concatenate([dar, dai], axis=1))

    return _rowcall(name, fn, t, tm, seq // tm, tiled=[dout, proj, xst, y_all], halos=[(xst, 8)],
                    res=[bbdt, cbdt, pwc, tabc, d5, w_glu, b_glu],
                    out_tiled=[_sds((t, SW)), _sds((t, 2 * S5L), MXU_DTYPE), _sds((t, SW), MXU_DTYPE),
                               _sds((t, SW), MXU_DTYPE), _sds((t, SW), MXU_DTYPE)],
                    out_acc=[_sds((1, 2 * S5L)), _sds((1, SW)), _sds((1, SW))],
                    scratch=[pltpu.VMEM((8, 2 * S5L), F32), pltpu.VMEM((tm, 2 * S5L), F32)], reverse=True, **carry)


def _out_fwd(name, yssd, ys5, x1, mod, w_out, lng, lnb, seq):
    t = x1.shape[0]
    tm = min(TM_WIDE, seq)

    def fn(ctx, tiled, halos, exs, res, outs, accs, exaccs, scr):
        ya_ref, yb_ref, x_ref = tiled
        mod_ref, = exs
        w_ref, g_ref, b_ref = res
        m = _dot(ya_ref[...], w_ref[0:SW, :]) + _dot(yb_ref[...], w_ref[SW:2 * SW, :])
        r = ALPHA * x_ref[...] + mod_ref[0, 5:6, :] * m
        outs[0][...] = _ln_fwd(r, g_ref[...], b_ref[...])
        outs[1][...] = r
        outs[2][...] = m

    return _rowcall(name, fn, t, tm, seq // tm, tiled=[yssd, ys5, x1], exs=[mod], res=[w_out, lng, lnb],
                    out_tiled=[_sds((t, D)), _sds((t, D)), _sds((t, D))])


def _out_bwd(name, dxo, r, m, mod, lng, w_out, seq, **carry):
    t = dxo.shape[0]
    tm = min(TM_WIDE, seq)

    def fn(ctx, tiled, halos, exs, res, outs, accs, exaccs, scr):
        dxo_ref, r_ref, m_ref = tiled
        mod_ref, = exs
        g_ref, w_ref = res
        dr, dgam, dbet = _ln_bwd(r_ref[...], g_ref[...], dxo_ref[...])
        outs[0][...] = dr
        _acc(accs[0], dgam)
        _acc(accs[1], dbet)
        _acc(exaccs[0].at[0], jnp.sum(dr * m_ref[...], axis=0, keepdims=True))
        dm = _mx(mod_ref[0, 5:6, :] * dr)
        outs[1][...] = dm
        dyc = _dot_nt(dm, w_ref[...])
        outs[2][...] = dyc[:, 0:SW]
        outs[3][...] = dyc[:, SW:2 * SW]

    b = mod.shape[0]
    return _rowcall(name, fn, t, tm, seq // tm, tiled=[dxo, r, m], exs=[mod], res=[lng, w_out],
                    out_tiled=[_sds((t, D)), _sds((t, D), MXU_DTYPE), _sds((t, SW)), _sds((t, SW))],
                    out_acc=[_sds((1, D)), _sds((1, D))], out_exacc=[_sds((b, 1, D))], **carry)


def _s5_discretise(a_re, a_im, log_dt, b_re, b_im):
    dt = jnp.exp(log_dt)[:, None]
    mag = jnp.exp(dt * a_re)
    ab_re, ab_im = mag * jnp.cos(dt * a_im), mag * jnp.sin(dt * a_im)
    den = a_re * a_re + a_im * a_im
    nr, ni = ab_re - 1.0, ab_im
    f_re, f_im = (nr * a_re + ni * a_im) / den, (ni * a_re - nr * a_im) / den
    bb_re = f_re[..., None] * b_re - f_im[..., None] * b_im
    bb_im = f_re[..., None] * b_im + f_im[..., None] * b_re
    return ab_re, ab_im, bb_re, bb_im


def _s5_tables(ab_re, ab_im):
    ar, ai = ab_re.reshape(1, S5L), ab_im.reshape(1, S5L)
    pows = [(ar, ai)]
    for _ in range(7):
        pr, pi = pows[-1]
        pows.append((pr * ar - pi * ai, pr * ai + pi * ar))

    def pack(rows, sign):
        return jnp.concatenate([jnp.concatenate([r for r, _ in rows], axis=0),
                                jnp.concatenate([sign * i for _, i in rows], axis=0)], axis=1)

    row = jnp.arange(8)[:, None]
    pw = jnp.stack([jnp.where(row >= sh, pack([pows[sh - 1]], 1.0), 0.0) for sh in (1, 2, 4)])
    pwc = jnp.stack([jnp.where(row < 8 - sh, pack([pows[sh - 1]], -1.0), 0.0) for sh in (1, 2, 4)])
    tab = pack(pows, 1.0)
    tabc = pack(pows[::-1], -1.0)
    return pw, tab, pwc, tabc


class _GradGroup:
    def __init__(self, tag, grads, place):
        self.tag, self.names, (self.half, self.chip) = tag, list(grads), place
        self.gsh = [_shard_major(g) if n in COL_SHARDED else g.reshape(4, g.shape[0] // 4, g.shape[1])
                    for n, g in grads.items()]

    def sibling(self):
        return _rs_sibling_stage(self.gsh)

    def chips(self, received):
        self.sums, sums_bf = _rs_add(self.tag + "_add", self.gsh, received, self.half)
        return _rs_chips_stage(sums_bf)

    def join(self, received):
        return _rs_join_stage(_rs_sum(self.tag + "_sum", self.sums, received, self.chip, self.half))

    def result(self, joined):
        return dict(zip(self.names, joined))


def _hid(fn, *args, stage=None, **kw):
    if stage is None:
        return fn(*args, **kw), None
    return fn(*args, carry=stage, **kw)


def _local_step(x, tgt, mod, w, sp, seq, dist=None):
    t = x.shape[0]
    mxu = MXU_DTYPE
    big = {}

    def group(tag, grads):
        if dist is None:
            big.update(grads)
            return None
        return _GradGroup(tag, grads, dist[2])

    mid = {"carry_mid": 0.6} if dist else {}
    (h1, ab1, s1), got = _hid(_ffn_fwd_hidden, "ffn1_fwd_a", x, mod, 0, w["ffn1_w1t"], w["ffn1_w3t"], seq,
                              stage=dist and dist[0][0], **mid)
    if dist:
        w = {**w, **dist[0][1](got)}
    x1, r1, f1 = _ffn_fwd_out("ffn1_fwd_b", s1, x, mod, 0, w["ffn1_w2"], sp["ln1_g"], sp["ln1_b"], seq)
    h2, proj = _proj_fwd("proj_fwd", x1, mod, w["w_in"], seq)
    yssd, xpre, yraw, sprev = _ssd_fwd("ssd_fwd", proj, sp["conv_w"], sp["conv_b"], sp["dt_bias"], sp["a_log"],
                                       sp["d_rep"], sp["ssd_norm_w"], seq)
    (ab_re, ab_im, bb_re, bb_im), disc_vjp = jax.vjp(_s5_discretise, sp["s5_a_re"], sp["s5_a_im"], sp["s5_log_dt"],
                                                     sp["s5_b_re"], sp["s5_b_im"])
    eye = jnp.eye(S5G, dtype=F32)
    bbd = jnp.concatenate([jnp.einsum("gk,gph->ghkp", eye, bb_re).reshape(SW, S5L),
                           jnp.einsum("gk,gph->ghkp", eye, bb_im).reshape(SW, S5L)], axis=1).astype(mxu)
    cbd = jnp.concatenate([jnp.einsum("gk,ghp->gpkh", eye, sp["s5_c_re"]).reshape(S5L, SW),
                           -jnp.einsum("gk,ghp->gpkh", eye, sp["s5_c_im"]).reshape(S5L, SW)], axis=0).astype(mxu)
    bbdt = jnp.concatenate([jnp.einsum("gk,gph->kpgh", eye, bb_re).reshape(S5L, SW),
                            jnp.einsum("gk,gph->kpgh", eye, bb_im).reshape(S5L, SW)], axis=0).astype(mxu)
    cbdt = jnp.concatenate([jnp.einsum("gk,ghp->khgp", eye, sp["s5_c_re"]).reshape(SW, S5L),
                            -jnp.einsum("gk,ghp->khgp", eye, sp["s5_c_im"]).reshape(SW, S5L)], axis=1).astype(mxu)
    pw, tab, pwc, tabc = _s5_tables(lax.stop_gradient(ab_re), lax.stop_gradient(ab_im))
    (ys5, xst, y5, xstb, ub), got = _hid(_s5_fwd, "s5_fwd", proj, bbd, cbd, pw, tab, sp["s5_d"], w["w_glu"], sp["b_glu"],
                                         seq, stage=dist and dist[1][0], **mid)
    if dist:
        w = {**w, **dist[1][1](got)}
    x2, r2, m2 = _out_fwd("out_fwd", yssd, ys5, x1, mod, w["w_out"], sp["ln2_g"], sp["ln2_b"], seq)
    h3, ab3, s3 = _ffn_fwd_hidden("ffn2_fwd_a", x2, mod, 6, w["ffn2_w1t"], w["ffn2_w3t"], seq)
    x3, r3, f3, dy3, loss_acc = _ffn_fwd_out("ffn2_fwd_b", s3, x2, mod, 6, w["ffn2_w2"], sp["ln3_g"], sp["ln3_b"], seq, tgt=tgt)
    dx2, df3, dab3, dg3g, dg3b, dgate3, dsh3, dsc3 = _ffn_bwd("ffn2_bwd", dy3, r3, ab3, f3, x2, mod, 6, sp["ln3_g"],
                                                              w["ffn2_w2"], w["ffn2_w1t"], w["ffn2_w3t"], seq)
    grp = group("rs_ffn2", dict(ffn2_w1=_tn_matmul("ffn2_dw1", dab3, h3, FB, D, a_cols=(0, FF)),
                                ffn2_w3=_tn_matmul("ffn2_dw3", dab3, h3, FB, D, a_cols=(FF, FF)),
                                ffn2_w2=_tn_matmul("ffn2_dw2", s3, df3, FB, D)))
    (dr2, dm2, dyssd, dys5, dg2g, dg2b, dgate2), got = _hid(_out_bwd, "out_bwd", dx2, r2, m2, mod, sp["ln2_g"],
                                                          w["w_out"], seq, stage=grp and grp.sibling())
    g_w_out = jnp.concatenate([_tn_matmul("dw_out_a", yssd, dm2, SW, D), _tn_matmul("dw_out_b", ys5, dm2, SW, D)], axis=0)
    (du, lam, dy5b, g5b, dv5b, da5, dd5, dbglu), got = _hid(
        _s5_bwd, "s5_bwd", dys5, proj, xst, y5, bbdt, cbdt, pwc, tabc, sp["s5_d"], w["w_glu"], sp["b_glu"],
        seq, stage=grp and grp.chips(got))
    g_w_glu = _tn_matmul("dw_glu", g5b, dv5b, SW, SW)
    dbb = _diag_groups(_tn_diag("s5_db", lam, ub))
    dcc = _diag_groups(_tn_diag("s5_dc", xstb, dy5b))
    (dzx, ddt, dnw, ddl, dcw, dcb, ddtb, dalog), got = _hid(
        _ssd_bwd, "ssd_bwd", dyssd, proj, xpre, yraw, sprev, sp["conv_w"], sp["dt_bias"], sp["a_log"], sp["d_rep"],
        sp["ssd_norm_w"], seq, stage=grp and grp.join(got))
    if grp:
        big.update(grp.result(got))
    dx1, dproj, dsh2, dsc2 = _mod_bwd("proj_bwd", dzx, dr2, x1, mod, 3, [w["w_in"]], seq, extra=(du, ddt), nt=True)
    gwi = _tn_matmul("dw_in", h2, dproj, D, PW)
    grp = group("rs_mix", dict(w_in=jnp.concatenate([gwi[:, :1536], gwi[:, 2048:2056], gwi[:, 1536:2048]], axis=1),
                               w_glu=g_w_glu, w_out=g_w_out))
    (dr1, df1, dab1, dg1g, dg1b, dgate1), got = _hid(_ffn_bwd1, "ffn1_bwd1", dx1, r1, ab1, f1, mod, 0, sp["ln1_g"],
                                                          w["ffn1_w2"], seq, stage=grp and grp.sibling())
    g1, got = _hid(_tn_matmul, "ffn1_dw1", dab1, h1, FB, D, a_cols=(0, FF), stage=grp and grp.chips(got))
    g3, got = _hid(_tn_matmul, "ffn1_dw3", dab1, h1, FB, D, a_cols=(FF, FF), stage=grp and grp.join(got))
    if grp:
        big.update(grp.result(got))
    grp = group("rs_ffn1a", dict(ffn1_w1=g1, ffn1_w3=g3))
    g2, got = _hid(_tn_matmul, "ffn1_dw2", s1, df1, FB, D, stage=grp and grp.sibling())
    grp2 = group("rs_ffn1b", dict(ffn1_w2=g2))
    (dx0, dsh1, dsc1), got = _hid(_mod_bwd, "ffn1_bwd2", dab1, dr1, x, mod, 0, [w["ffn1_w1t"], w["ffn1_w3t"]], seq,
                                  stage=grp and _merge_stages(grp.chips(got), grp2.sibling()))
    if grp:
        n = len(grp.names)
        got = _run_stage("rs_ffn1_tail", _merge_stages(grp.join(got[:n]), grp2.chips(got[n:])))
        big.update(grp.result(got[:n]))
        big.update(grp2.result(_run_stage("rs_ffn1b_join", grp2.join(got[n:]))))
    dmod = jnp.concatenate([dsh1, dsc1, dgate1, dsh2, dsc2, dgate2, dsh3, dsc3, dgate3], axis=1)
    dc_re, dc_im = dcc[0].transpose(0, 2, 1), -dcc[1].transpose(0, 2, 1)
    g_a_re, g_a_im, g_log_dt, g_b_re, g_b_im = disc_vjp(
        (da5[:, :S5L].reshape(S5G, S5P), da5[:, S5L:].reshape(S5G, S5P), dbb[0], dbb[1]))
    small = dict(ln1_g=dg1g, ln1_b=dg1b, ln2_g=dg2g, ln2_b=dg2b, ln3_g=dg3g, ln3_b=dg3b, conv_w=dcw[0:4], conv_b=dcb,
                 dt_bias=ddtb[:, :NH], a_log=dalog[:, :NH], d_ssd=jnp.sum(ddl.reshape(NH, HP), axis=1).reshape(1, NH),
                 ssd_norm_w=dnw, s5_a_re=g_a_re, s5_a_im=g_a_im, s5_log_dt=g_log_dt, s5_b_re=g_b_re, s5_b_im=g_b_im,
                 s5_c_re=dc_re, s5_c_im=dc_im, s5_d=dd5, w_glu_b=dbglu)
    return loss_acc[0, 0], dx0, dmod, big, small


def _place():
    return lax.axis_index("x"), lax.axis_index("y"), lax.axis_index("c")


def _other_chips(x, y):
    return [(1 - x, y), (x, 1 - y), (1 - x, 1 - y)]


def _allgather8(name, a):
    r, n = a.shape

    def body(x_ref, out_ref, send_sems, recv_sems, local_sem):
        x, y, c = _place()
        me, sibling = (x, y, c), (x, y, 1 - c)
        chips = _other_chips(x, y)

        def rows(px, py, pc):
            return out_ref.at[pl.ds(pl.multiple_of((4 * px + 2 * py + pc) * r, 8), r), :]

        def copy(k, block, to, src=None):
            return pltpu.make_async_remote_copy(src_ref=rows(*block) if src is None else src, dst_ref=rows(*block),
                                                send_sem=send_sems.at[k], recv_sem=recv_sems.at[k], device_id=to,
                                                device_id_type=MESH_T)

        mine = pltpu.make_async_copy(x_ref, rows(*me), local_sem)
        mine.start()
        first = [copy(0, me, sibling, src=x_ref)]
        first += [copy(1 + j, me, (*chip, c), src=x_ref) for j, chip in enumerate(chips)]
        for cp in first:
            cp.start()
        passed = [copy(4 + j, (*chip, c), sibling) for j, chip in enumerate(chips)]
        for j, chip in enumerate(chips):
            copy(1 + j, (*chip, c), me).wait_recv()
            passed[j].start()
        copy(0, sibling, me).wait_recv()
        for j, chip in enumerate(chips):
            copy(4 + j, (*chip, 1 - c), me).wait_recv()
        for cp in first + passed:
            cp.wait_send()
        mine.wait()

    out = _pcall(body, name=name, out_shape=_sds((8 * r, n), a.dtype),
                 in_specs=[pl.BlockSpec(memory_space=pltpu.VMEM)], out_specs=pl.BlockSpec(memory_space=pltpu.VMEM),
                 scratch_shapes=[pltpu.SemaphoreType.DMA((7,)), pltpu.SemaphoreType.DMA((7,)), pltpu.SemaphoreType.DMA],
                 compiler_params=_cparams())(a)
    return out.reshape(8, r, n)


class _Stage:
    def __init__(self, ins, out_shapes, n_sems, start, finish, mid=None, aliases=None):
        self.ins, self.out_shapes, self.n_sems = list(ins), list(out_shapes), tuple(n_sems)
        self.start, self.mid, self.finish = start, mid, finish
        self.aliases = dict(aliases or {})

    def sem_shapes(self):
        return [pltpu.SemaphoreType.DMA((n,)) for n in self.n_sems]

    def run(self, refs, at_start=None, at_mid=None, at_finish=None, between=None):
        if between is None:
            for part in (self.start, self.mid, self.finish):
                if part is not None:
                    part(*refs)
            return
        pl.when(at_start)(functools.partial(self.start, *refs))
        if self.mid is not None:
            pl.when(at_mid)(functools.partial(self.mid, *refs))
        between()
        pl.when(at_finish)(functools.partial(self.finish, *refs))


def _merge_stages(a, b):
    n_in, n_out, n_sem = len(a.ins), len(a.out_shapes), len(a.n_sems)

    def both(fa, fb):
        def run(ins, outs, sems):
            fa(ins[:n_in], outs[:n_out], sems[:n_sem])
            fb(ins[n_in:], outs[n_out:], sems[n_sem:])
        return run

    aliases = {**a.aliases, **{n_in + k: n_out + v for k, v in b.aliases.items()}}
    return _Stage(a.ins + b.ins, a.out_shapes + b.out_shapes, a.n_sems + b.n_sems, both(a.start, b.start),
                  both(a.finish, b.finish), aliases=aliases)


def _run_stage(name, st):
    n_in, n_out = len(st.ins), len(st.out_shapes)

    def body(*refs):
        st.run((refs[:n_in], refs[n_in:n_in + n_out], refs[n_in + n_out:]))

    any_spec = pl.BlockSpec(memory_space=pl.ANY)
    return _pcall(body, name=name, out_shape=st.out_shapes, in_specs=[any_spec] * n_in, out_specs=[any_spec] * n_out,
                  input_output_aliases=st.aliases, scratch_shapes=st.sem_shapes(), compiler_params=_cparams())(*st.ins)


def _rows(ref_rows, half, align):
    hr = ref_rows // 2
    return pl.ds(pl.multiple_of(half * hr, align), hr)


def _gather_stage(shards):
    n = len(shards)
    pairs = [(i, j) for i in range(n) for j in range(3)]
    align = [32 // s.dtype.itemsize for s in shards]

    def env(ins, outs, sems):
        x, y, c = _place()
        chips = _other_chips(x, y)

        def copy(i, k, chip, half, to, src=None):
            dst = outs[i].at[2 * chip[0] + chip[1], _rows(shards[i].shape[0], half, align[i])]
            return pltpu.make_async_remote_copy(src_ref=dst if src is None else src, dst_ref=dst,
                                                send_sem=sems[0].at[6 * i + k], recv_sem=sems[1].at[6 * i + k],
                                                device_id=to, device_id_type=MESH_T)

        def first(i, j):
            return copy(i, j, (x, y), c, (*chips[j], c), src=ins[i].at[_rows(shards[i].shape[0], c, align[i])])

        def passed(i, j, half):
            return copy(i, 3 + j, chips[j], half, (x, y, 1 - c))

        def landed(i, j):
            return copy(i, j, chips[j], c, (x, y, 1 - c))

        return c, first, passed, landed

    def start(ins, outs, sems):
        c, first, passed, landed = env(ins, outs, sems)
        for i, j in pairs:
            first(i, j).start()

    def mid(ins, outs, sems):
        c, first, passed, landed = env(ins, outs, sems)
        for i, j in pairs:
            landed(i, j).wait_recv()
            passed(i, j, c).start()

    def finish(ins, outs, sems):
        c, first, passed, landed = env(ins, outs, sems)
        for i, j in pairs:
            passed(i, j, 1 - c).wait_recv()
        for i, j in pairs:
            first(i, j).wait_send()
            passed(i, j, c).wait_send()

    return _Stage(shards, [_sds((4,) + s.shape, s.dtype) for s in shards], (6 * n, 6 * n), start, finish, mid)


def _rs_sibling_stage(gs):
    n = len(gs)

    def copies(ins, outs, sems):
        x, y, c = _place()
        return [pltpu.make_async_remote_copy(src_ref=ins[i].at[:, _rows(gs[i].shape[1], 1 - c, 8)], dst_ref=outs[i],
                                             send_sem=sems[0].at[i], recv_sem=sems[1].at[i], device_id=(x, y, 1 - c),
                                             device_id_type=MESH_T) for i in range(n)]

    def start(*refs):
        for cp in copies(*refs):
            cp.start()

    def finish(*refs):
        for cp in copies(*refs):
            cp.wait()

    return _Stage(gs, [_sds((4, g.shape[1] // 2, g.shape[2]), g.dtype) for g in gs], (n, n), start, finish)


def _rs_chips_stage(hs):
    n = len(hs)

    def copies(ins, outs, sems):
        x, y, c = _place()
        return [pltpu.make_async_remote_copy(src_ref=ins[i].at[2 * chip[0] + chip[1]], dst_ref=outs[i].at[j],
                                             send_sem=sems[0].at[3 * i + j], recv_sem=sems[1].at[3 * i + j],
                                             device_id=(*chip, c), device_id_type=MESH_T)
                for i in range(n) for j, chip in enumerate(_other_chips(x, y))]

    def start(*refs):
        for cp in copies(*refs):
            cp.start()

    def finish(*refs):
        for cp in copies(*refs):
            cp.wait()

    return _Stage(hs, [_sds((3,) + h.shape[1:], h.dtype) for h in hs], (3 * n, 3 * n), start, finish)


def _rs_join_stage(fs):
    n = len(fs)

    def copy(outs, sems, i, half):
        x, y, c = _place()
        part = outs[i].at[_rows(fs[i].shape[0], c if half == "mine" else 1 - c, 8)]
        return pltpu.make_async_remote_copy(src_ref=part, dst_ref=part, send_sem=sems[0].at[i], recv_sem=sems[1].at[i],
                                            device_id=(x, y, 1 - c), device_id_type=MESH_T)

    def start(ins, outs, sems):
        for i in range(n):
            copy(outs, sems, i, "mine").start()

    def finish(ins, outs, sems):
        for i in range(n):
            copy(outs, sems, i, "theirs").wait_recv()
        for i in range(n):
            copy(outs, sems, i, "mine").wait_send()

    return _Stage(fs, [_sds(f.shape, f.dtype) for f in fs], (n, n), start, finish, aliases={i: i for i in range(n)})


def _row_block(r, cap=2048):
    b = min(r, cap)
    while r % b or b % 8:
        b -= 8
    return b


RS_SPLIT = 2


def _rs_add(name, gs, r1s, sel):
    n = len(gs)

    def body(sel_ref, *refs):
        g_refs, r_refs, h_refs, b_refs = (refs[k * n:(k + 1) * n] for k in range(4))
        for i in range(n):
            h = g_refs[i][...] + r_refs[i][...]
            h_refs[i][...] = h
            b_refs[i][...] = h.astype(BF16)

    def blk(g):
        return (1, g.shape[1] // 2 // RS_SPLIT, g.shape[2])

    here = lambda k, j, s: (k, j, 0)
    in_specs = [pl.BlockSpec(blk(g), lambda k, j, s: (k, s[0] * RS_SPLIT + j, 0)) for g in gs]
    in_specs += [pl.BlockSpec(blk(g), here) for g in gs]
    outs = _pcall(body, name=name,
                  out_shape=[_sds((4, g.shape[1] // 2, g.shape[2])) for g in gs]
                  + [_sds((4, g.shape[1] // 2, g.shape[2]), BF16) for g in gs],
                  grid_spec=pltpu.PrefetchScalarGridSpec(num_scalar_prefetch=1, grid=(4, RS_SPLIT), in_specs=in_specs,
                                                         out_specs=[pl.BlockSpec(blk(g), here) for g in gs] * 2),
                  compiler_params=_cparams(("parallel", "parallel")))(sel.reshape(1).astype(jnp.int32), *gs, *r1s)
    return outs[:n], outs[n:]


def _rs_sum(name, hs, r2s, chip, half):
    n = len(hs)

    def body(sel_ref, *refs):
        h_refs, r_refs, o_refs = (refs[k * n:(k + 1) * n] for k in range(3))
        for i in range(n):
            r = r_refs[i]
            o_refs[i][...] = ((h_refs[i][0] + r[0].astype(F32)) + r[1].astype(F32)) + r[2].astype(F32)

    def rows(h):
        return h.shape[1] // RS_SPLIT

    in_specs = [pl.BlockSpec((1, rows(h), h.shape[2]), lambda j, s: (s[0], j, 0)) for h in hs]
    in_specs += [pl.BlockSpec((3, rows(h), h.shape[2]), lambda j, s: (0, j, 0)) for h in hs]
    sel = jnp.stack([chip, half]).astype(jnp.int32)
    return _pcall(body, name=name, out_shape=[_sds((2 * h.shape[1], h.shape[2])) for h in hs],
                  grid_spec=pltpu.PrefetchScalarGridSpec(
                      num_scalar_prefetch=1, grid=(RS_SPLIT,), in_specs=in_specs,
                      out_specs=[pl.BlockSpec((rows(h), h.shape[2]), lambda j, s: (s[1] * RS_SPLIT + j, 0)) for h in hs]),
                  compiler_params=_cparams(("parallel",)))(sel, *hs, *r2s)


def _sum8(name, a):
    _, r, n = a.shape
    br = _row_block(r)

    def body(a_ref, o_ref):
        acc = a_ref[0]
        for k in range(1, 8):
            acc = acc + a_ref[k]
        o_ref[...] = acc

    return _pcall(body, name=name, out_shape=_sds((r, n)), grid=(r // br,),
                  in_specs=[pl.BlockSpec((8, br, n), lambda j: (0, j, 0))], out_specs=pl.BlockSpec((br, n), lambda j: (j, 0)),
                  compiler_params=_cparams(("parallel",)))(a)


def _adamw(name, ws, gs, ms, vs, nblk):
    n = len(ws)

    def body(*refs):
        w_refs, g_refs, m_refs, v_refs, d_refs, nm_refs, nv_refs = (refs[k * n:(k + 1) * n] for k in range(7))
        for i in range(n):
            gv = g_refs[i][...]
            nm = ADAM_B1 * m_refs[i][...] + (1.0 - ADAM_B1) * gv
            nv = ADAM_B2 * v_refs[i][...] + (1.0 - ADAM_B2) * (gv * gv)
            nm_refs[i][...] = nm
            nv_refs[i][...] = nv
            m_hat = nm / (1.0 - ADAM_B1 ** ADAM_STEP)
            v_hat = nv / (1.0 - ADAM_B2 ** ADAM_STEP)
            d_refs[i][...] = -ADAM_LR * (m_hat / (jnp.sqrt(v_hat) + ADAM_EPS) + ADAM_WD * w_refs[i][...])

    specs = [pl.BlockSpec((w.shape[0] // nblk, w.shape[1]), lambda j: (j, 0)) for w in ws]
    outs = _pcall(body, name=name, out_shape=[_sds(w.shape) for w in ws] * 3, grid=(nblk,), in_specs=specs * 4,
                  out_specs=specs * 3, compiler_params=_cparams(("parallel",)))(*ws, *gs, *ms, *vs)
    return outs[:n], outs[n:2 * n], outs[2 * n:]


ADA_COLS = 2304
ADA_BLK = 768


def _ada_fwd(name, c_all, w_shard, b_cols):
    nb = c_all.shape[0]

    def body(c_ref, w_ref, b_ref, o_ref):
        cv = c_ref[...]
        cs = _mx(cv * _sigmoid(cv))
        o_ref[...] = _dot(cs, _mx(w_ref[...])) + b_ref[...]

    return _pcall(body, name=name, out_shape=_sds((nb, ADA_COLS)), grid=(ADA_COLS // ADA_BLK,),
                  in_specs=[pl.BlockSpec((nb, D), lambda j: (0, 0)), pl.BlockSpec((D, ADA_BLK), lambda j: (0, j)),
                            pl.BlockSpec((1, ADA_BLK), lambda j: (0, j))],
                  out_specs=pl.BlockSpec((nb, ADA_BLK), lambda j: (0, j)),
                  compiler_params=_cparams(("parallel",)))(c_all, w_shard, b_cols)


def _ada_bwd(name, c_all, dmod_cols, dmod_all):
    nb = c_all.shape[0]

    def body(c_ref, dc_ref, da_ref, gw_ref, gb_ref):
        cv = c_ref[...]
        cs = _mx(cv * _sigmoid(cv))
        gw_ref[...] = lax.dot_general(cs, _mx(dc_ref[...]), (((0,), (0,)), ((), ())), preferred_element_type=F32)

        @pl.when(pl.program_id(0) == 0)
        def _():
            gb_ref[...] = jnp.sum(da_ref[...], axis=0, keepdims=True)

    return _pcall(body, name=name, out_shape=[_sds((D, ADA_COLS)), _sds((1, 9 * D))], grid=(ADA_COLS // ADA_BLK,),
                  in_specs=[pl.BlockSpec((nb, D), lambda j: (0, 0)), pl.BlockSpec((nb, ADA_BLK), lambda j: (0, j)),
                            pl.BlockSpec((nb, 9 * D), lambda j: (0, 0))],
                  out_specs=[pl.BlockSpec((D, ADA_BLK), lambda j: (0, j)), pl.BlockSpec((1, 9 * D), lambda j: (0, 0))],
                  compiler_params=_cparams(("arbitrary",)))(c_all, dmod_cols, dmod_all)


BIG = ("ffn1_w1", "ffn1_w3", "ffn1_w2", "w_in", "w_glu", "w_out", "ffn2_w1", "ffn2_w3", "ffn2_w2")
COL_SHARDED = ("w_in",)
TRANSPOSED = ("ffn1_w1", "ffn1_w3", "ffn2_w1", "ffn2_w3")
SMALL = ("b_ada", "ln1_g", "ln1_b", "conv_w", "conv_b", "dt_bias", "a_log", "d_ssd", "ssd_norm_w", "s5_a_re", "s5_a_im",
         "s5_log_dt", "s5_b_re", "s5_b_im", "s5_c_re", "s5_c_im", "s5_d", "b_glu", "ln2_g", "ln2_b", "ln3_g", "ln3_b")
WEIGHTS = ("w_ada", "b_ada", "ffn1_w1", "ffn1_w3", "ffn1_w2", "ln1_g", "ln1_b", "w_in", "conv_w", "conv_b", "dt_bias",
           "a_log", "d_ssd", "ssd_norm_w", "s5_a_re", "s5_a_im", "s5_log_dt", "s5_b_re", "s5_b_im", "s5_c_re", "s5_c_im",
           "s5_d", "w_glu", "b_glu", "w_out", "ln2_g", "ln2_b", "ffn2_w1", "ffn2_w3", "ffn2_w2", "ln3_g", "ln3_b")
BIG_PAD = 2 * 1024 * 128


def _pack(arrs, mult, axis_keep=0):
    lead = arrs[0].shape[:axis_keep]
    flat = jnp.concatenate([a.reshape(lead + (-1,)) for a in arrs], axis=axis_keep)
    pad = (-flat.shape[-1]) % mult
    if pad:
        flat = jnp.concatenate([flat, jnp.zeros(lead + (pad,), flat.dtype)], axis=axis_keep)
    return flat


def _unpack(flat, shapes):
    out, off = [], 0
    for s in shapes:
        size = math.prod(s)
        out.append(flat[..., off:off + size].reshape(flat.shape[:-1] + tuple(s)))
        off += size
    return out


def _shard_major(a):
    rows, cols = a.shape
    return a.reshape(rows, 4, cols // 4).transpose(1, 0, 2)


def _from_shard_major(a):
    _, rows, w = a.shape
    return a.transpose(1, 0, 2).reshape(rows, 4 * w)


def kernel(x, c, w_ada, b_ada, ffn1_w1, ffn1_w3, ffn1_w2, ln1_g, ln1_b, w_in, conv_w, conv_b, dt_bias, a_log, d_ssd, ssd_norm_w, s5_a_re, s5_a_im, s5_log_dt, s5_b_re, s5_b_im, s5_c_re, s5_c_im, s5_d, w_glu, b_glu, w_out, ln2_g, ln2_b, ffn2_w1, ffn2_w3, ffn2_w2, ln3_g, ln3_b, loss_target, m_w_ada, m_b_ada, m_ffn1_w1, m_ffn1_w3, m_ffn1_w2, m_ln1_g, m_ln1_b, m_w_in, m_conv_w, m_conv_b, m_dt_bias, m_a_log, m_d_ssd, m_ssd_norm_w, m_s5_a_re, m_s5_a_im, m_s5_log_dt, m_s5_b_re, m_s5_b_im, m_s5_c_re, m_s5_c_im, m_s5_d, m_w_glu, m_b_glu, m_w_out, m_ln2_g, m_ln2_b, m_ffn2_w1, m_ffn2_w3, m_ffn2_w2, m_ln3_g, m_ln3_b, v_w_ada, v_b_ada, v_ffn1_w1, v_ffn1_w3, v_ffn1_w2, v_ln1_g, v_ln1_b, v_w_in, v_conv_w, v_conv_b, v_dt_bias, v_a_log, v_d_ssd, v_ssd_norm_w, v_s5_a_re, v_s5_a_im, v_s5_log_dt, v_s5_b_re, v_s5_b_im, v_s5_c_re, v_s5_c_im, v_s5_d, v_w_glu, v_b_glu, v_w_out, v_ln2_g, v_ln2_b, v_ffn2_w1, v_ffn2_w3, v_ffn2_w2, v_ln3_g, v_ln3_b):
    a = dict(locals())
    xi, yi, ci = _place()
    chip = 2 * xi + yi
    dev = 2 * chip + ci
    nb, seq, _ = x.shape
    t = nb * seq
    ndev = 8

    c_rows = nb * D // 128
    c_cw = _allgather8("gather_c", jnp.concatenate([c.reshape(c_rows, 128), conv_w.reshape(-1, 128)], axis=0))
    c_all = c_cw[:, :c_rows].reshape(ndev * nb, D)
    b_cols = lax.dynamic_slice(b_ada, (0, chip * ADA_COLS), (1, ADA_COLS))
    mod_part = _ada_fwd("ada_fwd", c_all, w_ada[0], b_cols)

    def nat(n):
        return jnp.swapaxes(a[n], 1, 2)[0] if n[-7:] in TRANSPOSED else a[n][0]

    def gather(names, extra=()):
        own = [nat(n).astype(MXU_DTYPE) for n in names]

        def weights(pieces):
            w = {}
            for n, mine, piece in zip(names, own, pieces):
                piece = lax.dynamic_update_slice(piece, mine[None], (chip, 0, 0))
                if n in COL_SHARDED:
                    wi = _from_shard_major(piece)
                    w[n] = jnp.concatenate([wi[:, :1536], wi[:, 1544:2056], wi[:, 1536:1544],
                                            jnp.zeros((D, 120), wi.dtype)], axis=1)
                else:
                    w[n + "t" if n in TRANSPOSED else n] = piece.reshape(-1, piece.shape[-1])
            return w

        return _gather_stage(own + list(extra)), weights

    first_stage, first_weights = gather(("ffn1_w1", "ffn1_w3"), extra=[mod_part])
    got = _run_stage("gather_w_first", first_stage)
    w = first_weights(got[:2])
    mod_all = lax.dynamic_update_slice(got[2], mod_part[None], (chip, 0, 0)).transpose(1, 0, 2).reshape(ndev * nb, 9 * D)
    mod = lax.dynamic_slice(mod_all, (nb * dev, 0), (nb, 9 * D)).reshape(nb, 9, D)

    conv_full = _from_shard_major(c_cw[0::2, c_rows:].reshape(4, 4, 256))
    pad8 = lambda v: jnp.concatenate([v.reshape(1, NH), jnp.zeros((1, 128 - NH), F32)], axis=1)
    sp = dict(ln1_g=ln1_g, ln1_b=ln1_b, ln2_g=ln2_g, ln2_b=ln2_b, ln3_g=ln3_g, ln3_b=ln3_b, conv_w=conv_full,
              conv_b=conv_b, dt_bias=pad8(dt_bias), a_log=pad8(a_log), d_rep=jnp.repeat(d_ssd[0], HP)[None],
              ssd_norm_w=ssd_norm_w, s5_a_re=s5_a_re[0], s5_a_im=s5_a_im[0], s5_log_dt=s5_log_dt[0], s5_b_re=s5_b_re[0],
              s5_b_im=s5_b_im[0], s5_c_re=s5_c_re[0], s5_c_im=s5_c_im[0], s5_d=s5_d, b_glu=b_glu)

    lsum, dx0, dmod, gbig, small = _local_step(x.reshape(t, D), loss_target.reshape(t, D), mod, w, sp, seq,
                                               dist=(gather(("ffn1_w2", "w_in", "w_glu", "w_out")),
                                                     gather(("ffn2_w1", "ffn2_w3", "ffn2_w2")), (ci, chip)))
    loss = lax.psum(lsum * (0.5 / D), ("x", "y", "c"))

    snames = [n for n in SMALL if n != "b_ada"]
    sgrad = dict(small)
    sgrad["b_glu"] = small["w_glu_b"]
    dm_rows = nb * 9 * D // 128
    tail = _allgather8("gather_small", jnp.concatenate(
        [dmod.reshape(dm_rows, 128), _pack([sgrad[n] for n in snames], 1024).reshape(-1, 128)], axis=0))
    dmod_all = tail[:, :dm_rows].reshape(ndev * nb, 9 * D)
    dmod_cols = lax.dynamic_slice(dmod_all, (0, chip * ADA_COLS), (ndev * nb, ADA_COLS))
    g_w_ada, g_b_ada = _ada_bwd("ada_bwd", c_all, dmod_cols, dmod_all)

    gbig["w_ada"] = g_w_ada

    outs = {}
    for call, names in (("adamw_a", ("ffn1_w1", "ffn1_w3", "ffn1_w2", "w_in", "w_glu", "w_out")),
                        ("adamw_b", ("ffn2_w1", "ffn2_w3", "ffn2_w2", "w_ada"))):
        res = _adamw(call, [nat(n) for n in names], [gbig[n] for n in names], [nat("m_" + n) for n in names],
                     [nat("v_" + n) for n in names], 8)
        for kind, arrs in zip(("grad", "delta", "new_m", "new_v"), ([gbig[n] for n in names],) + tuple(res)):
            for n, arr in zip(names, arrs):
                outs[kind, n] = (arr.T if n in TRANSPOSED else arr)[None]

    ssum = _sum8("small_sum", tail)[dm_rows:].reshape(-1)

    def view2d(u):
        s = u.shape[1:]
        return u.reshape((1, s[0]) if len(s) == 1 else (-1, s[-1]))

    vshape = {n: view2d(a[n]).shape for n in SMALL}
    gsm = dict(zip(snames, _unpack(ssum, [vshape[n] if n != "conv_w" else (4, D) for n in snames])))
    gsm["conv_w"] = lax.dynamic_slice(gsm["conv_w"], (0, chip * 256), (4, 256))
    gsm["b_ada"] = g_b_ada
    res = _adamw("adamw_small", [view2d(a[n]) for n in SMALL], [gsm[n] for n in SMALL],
                 [view2d(a["m_" + n]) for n in SMALL], [view2d(a["v_" + n]) for n in SMALL], 1)
    for kind, arrs in zip(("grad", "delta", "new_m", "new_v"), ([gsm[n] for n in SMALL],) + tuple(res)):
        for n, arr in zip(SMALL, arrs):
            outs[kind, n] = arr.reshape(a[n].shape)

    res = [loss, dx0.reshape(nb, seq, D)]
    for kind in ("grad", "delta", "new_m", "new_v"):
        res += [outs[kind, n] for n in WEIGHTS]
    return tuple(res)
```

```python
import functools
import math

import jax
import jax.numpy as jnp
from jax import lax
from jax.experimental import pallas as pl
from jax.experimental.pallas import tpu as pltpu

F32 = jnp.float32
BF16 = jnp.bfloat16
MXU_DTYPE = jnp.bfloat16

D = 1024
FF = 2816
FB = 1408
NH, HP, NS, NG = 8, 64, 128, 2
CH = 128
SW = 512
S5G, S5P, S5H = 32, 64, 16
S5L = S5G * S5P
PW = 2176
ALPHA = 2.0 ** 0.25
LN_EPS = 1e-5
ADAM_LR, ADAM_B1, ADAM_B2, ADAM_EPS, ADAM_WD, ADAM_STEP = 0.001, 0.9, 0.999, 1e-08, 0.01, 10
VMEM_LIMIT = 56 * 1024 * 1024
TM_WIDE = 512
MESH_T = pl.DeviceIdType.MESH


def _pcall(body, **kw):
    return pl.pallas_call(body, **kw)


def _cparams(sem=None, **kw):
    return pltpu.CompilerParams(dimension_semantics=sem, vmem_limit_bytes=VMEM_LIMIT, **kw)


def _dot(a, b):
    return jnp.dot(a, b, preferred_element_type=F32)


def _dot_nt(a, b):
    return lax.dot_general(a, b, (((1,), (1,)), ((), ())), preferred_element_type=F32)


def _dot_hi(a, b):
    return jnp.dot(a, b, preferred_element_type=F32, precision=lax.Precision.HIGHEST)


def _mx(a):
    return a.astype(MXU_DTYPE)


def _sigmoid(x):
    return 1.0 / (1.0 + jnp.exp(-x))


def _iota(shape, axis):
    return lax.broadcasted_iota(jnp.int32, shape, axis)


def _rowcall(name, fn, n_rows, tm, tpe, *, tiled=(), halos=(), exs=(), res=(), out_tiled=(), out_acc=(),
             out_exacc=(), scratch=(), reverse=False, batch=None, carry=None, carry_mid=0.0):
    if batch:
        n_rows //= batch
    nt = n_rows // tm

    def blk(i):
        return (nt - 1 - i) if reverse else i

    in_specs, args = [], []
    for a in tiled:
        if batch:
            in_specs.append(pl.BlockSpec((batch, tm, a.shape[1]), lambda i: (0, blk(i), 0)))
            args.append(a.reshape(batch, n_rows, a.shape[1]))
            continue
        in_specs.append(pl.BlockSpec((tm, a.shape[1]), lambda i: (blk(i), 0)))
        args.append(a)
    for a, rows in halos:
        r = tm // rows
        if batch:
            in_specs.append(pl.BlockSpec((batch, rows, a.shape[1]), lambda i, r=r: (0, jnp.maximum(blk(i) * r - 1, 0), 0)))
            args.append(a.reshape(batch, n_rows, a.shape[1]))
            continue
        in_specs.append(pl.BlockSpec((rows, a.shape[1]), lambda i, r=r: (jnp.maximum(blk(i) * r - 1, 0), 0)))
        args.append(a)
    for a in exs:
        in_specs.append(pl.BlockSpec((1,) + a.shape[1:], lambda i: (blk(i) // tpe, 0, 0)))
        args.append(a)
    for a in res:
        nd = a.ndim
        in_specs.append(pl.BlockSpec(a.shape, lambda i, nd=nd: (0,) * nd, pipeline_mode=pl.Buffered(1)))
        args.append(a)
    any_spec = pl.BlockSpec(memory_space=pl.ANY)
    st_ins = carry.ins if carry else []
    st_outs = carry.out_shapes if carry else []
    st_sems = carry.sem_shapes() if carry else []
    base_in = len(args)
    in_specs += [any_spec] * len(st_ins)
    args += st_ins
    out_specs, out_shape = [], []
    for s in out_tiled:
        if batch:
            out_specs.append(pl.BlockSpec((batch, tm, s.shape[1]), lambda i: (0, blk(i), 0)))
            out_shape.append(_sds((batch, n_rows, s.shape[1]), s.dtype))
            continue
        out_specs.append(pl.BlockSpec((tm, s.shape[1]), lambda i: (blk(i), 0)))
        out_shape.append(s)
    for s in out_acc:
        nd = len(s.shape)
        out_specs.append(pl.BlockSpec(s.shape, lambda i, nd=nd: (0,) * nd))
        out_shape.append(s)
    for s in out_exacc:
        out_specs.append(pl.BlockSpec((1,) + s.shape[1:], lambda i: (blk(i) // tpe, 0, 0)))
        out_shape.append(s)
    base_out = len(out_shape)
    out_specs += [any_spec] * len(st_outs)
    out_shape += st_outs
    aliases = {base_in + k: base_out + v for k, v in carry.aliases.items()} if carry else {}
    n = [len(tiled), len(halos), len(exs), len(res), len(st_ins), len(out_tiled), len(out_acc), len(out_exacc),
         len(st_outs), len(scratch), len(st_sems)]

    def body(*refs):
        groups, k = [], 0
        for m in n:
            groups.append(refs[k:k + m])
            k += m
        i = pl.program_id(0)
        b = blk(i)

        class ctx:
            first = i == 0
            pos = b % tpe
            seq_first = (b % tpe) == 0
            seq_last = (b % tpe) == tpe - 1
            ex_enter = (i % tpe) == 0

        def work():
            for cond, refs_ in ((ctx.first, groups[6]), (ctx.ex_enter, groups[7])):
                if refs_:
                    @pl.when(cond)
                    def _():
                        for r in refs_:
                            r[...] = jnp.zeros(r.shape, r.dtype)
            fn(ctx, *groups[0:4], *groups[5:8], groups[9])

        if carry:
            carry.run((groups[4], groups[8], groups[10]), i == 0, i == min(nt - 1, int(carry_mid * nt)), i == nt - 1, work)
        else:
            work()

    outs = _pcall(body, name=name, grid=(nt,), in_specs=in_specs, out_specs=out_specs, out_shape=out_shape,
                  input_output_aliases=aliases, scratch_shapes=list(scratch) + st_sems,
                  compiler_params=_cparams(("arbitrary",)))(*args)
    if batch:
        outs = [o.reshape(batch * n_rows, o.shape[2]) if k < len(out_tiled) else o for k, o in enumerate(outs)]
    return (outs[:base_out], outs[base_out:]) if carry else outs


def _acc(ref, val):
    ref[...] += val


def _sds(shape, dtype=F32):
    return jax.ShapeDtypeStruct(shape, dtype)


def _ln_fwd(r, g, b):
    mu = jnp.mean(r, axis=-1, keepdims=True)
    rc = r - mu
    var = jnp.mean(rc * rc, axis=-1, keepdims=True)
    return rc * lax.rsqrt(var + LN_EPS) * g + b


def _ln_bwd(r, g, dy):
    mu = jnp.mean(r, axis=-1, keepdims=True)
    rc = r - mu
    var = jnp.mean(rc * rc, axis=-1, keepdims=True)
    rstd = lax.rsqrt(var + LN_EPS)
    xhat = rc * rstd
    dxh = dy * g
    dr = rstd * (dxh - jnp.mean(dxh, axis=-1, keepdims=True) - xhat * jnp.mean(dxh * xhat, axis=-1, keepdims=True))
    return dr, jnp.sum(dy * xhat, axis=0, keepdims=True), jnp.sum(dy, axis=0, keepdims=True)


def _ffn_fwd_hidden(name, x, mod, k0, w1, w3, seq, **carry):
    t = x.shape[0]
    tm = min(TM_WIDE, seq)

    def fn(ctx, tiled, halos, exs, res, outs, accs, exaccs, scr):
        mod_ref, = exs
        w1_ref, w3_ref = res
        h_ref, ab_ref, s_ref = outs
        sh, sc = mod_ref[0, k0:k0 + 1, :], mod_ref[0, k0 + 1:k0 + 2, :]
        h = _mx(tiled[0][...] * (1.0 + sc) + sh)
        h_ref[...] = h
        for j in range(2):
            a = _dot_nt(h, w1_ref[j * FB:(j + 1) * FB, :])
            b = _dot_nt(h, w3_ref[j * FB:(j + 1) * FB, :])
            ab_ref[:, j * FB:(j + 1) * FB] = _mx(a)
            ab_ref[:, FF + j * FB:FF + (j + 1) * FB] = _mx(b)
            s_ref[:, j * FB:(j + 1) * FB] = _mx(a * _sigmoid(a) * b)

    return _rowcall(name, fn, t, tm, seq // tm, tiled=[x], exs=[mod], res=[w1, w3],
                    out_tiled=[_sds((t, D), MXU_DTYPE), _sds((t, 2 * FF), MXU_DTYPE), _sds((t, FF), MXU_DTYPE)], **carry)


def _ffn_fwd_out(name, s, x, mod, k0, w2, lng, lnb, seq, tgt=None, **carry):
    t = x.shape[0]
    tm = min(TM_WIDE, seq)
    with_loss = tgt is not None

    def fn(ctx, tiled, halos, exs, res, outs, accs, exaccs, scr):
        mod_ref, = exs
        w2_ref, g_ref, b_ref = res
        xo_ref, r_ref, f_ref = outs[:3]
        f = _dot(tiled[0][...], w2_ref[...])
        f_ref[...] = f
        r = ALPHA * tiled[1][...] + 0.5 * mod_ref[0, k0 + 2:k0 + 3, :] * f
        r_ref[...] = r
        xo = _ln_fwd(r, g_ref[...], b_ref[...])
        xo_ref[...] = xo
        if with_loss:
            e = xo - tiled[2][...]
            outs[3][...] = e * (1.0 / D)
            _acc(accs[0], jnp.sum(e * e) * jnp.ones((8, 128), F32))

    return _rowcall(name, fn, t, tm, seq // tm, tiled=[s, x] + ([tgt] if with_loss else []), exs=[mod],
                    res=[w2, lng, lnb], out_tiled=[_sds((t, D))] * (4 if with_loss else 3),
                    out_acc=[_sds((8, 128))] if with_loss else [], **carry)


def _ffn_bwd1(name, dxo, r, ab, f, mod, k0, lng, w2, seq, **carry):
    t = dxo.shape[0]
    tm = 256

    def fn(ctx, tiled, halos, exs, res, outs, accs, exaccs, scr):
        dxo_ref, r_ref, ab_ref, f_ref = tiled
        mod_ref, = exs
        g_ref, w2_ref = res
        dr_ref, df_ref, dab_ref = outs
        g = mod_ref[0, k0 + 2:k0 + 3, :]
        dr, dgam, dbet = _ln_bwd(r_ref[...], g_ref[...], dxo_ref[...])
        dr_ref[...] = dr
        _acc(accs[0], dgam)
        _acc(accs[1], dbet)
        _acc(exaccs[0].at[0], jnp.sum(0.5 * f_ref[...] * dr, axis=0, keepdims=True))
        df = _mx(0.5 * g * dr)
        df_ref[...] = df
        for j in range(2):
            ds = _dot_nt(df, w2_ref[j * FB:(j + 1) * FB, :])
            a = ab_ref[:, j * FB:(j + 1) * FB].astype(F32)
            b = ab_ref[:, FF + j * FB:FF + (j + 1) * FB].astype(F32)
            sig = _sigmoid(a)
            dab_ref[:, j * FB:(j + 1) * FB] = _mx(ds * b * (sig * (1.0 + a * (1.0 - sig))))
            dab_ref[:, FF + j * FB:FF + (j + 1) * FB] = _mx(ds * (a * sig))

    b = mod.shape[0]
    return _rowcall(name, fn, t, tm, seq // tm, tiled=[dxo, r, ab, f], exs=[mod], res=[lng, w2],
                    out_tiled=[_sds((t, D)), _sds((t, D), MXU_DTYPE), _sds((t, 2 * FF), MXU_DTYPE)],
                    out_acc=[_sds((1, D)), _sds((1, D))], out_exacc=[_sds((b, 1, D))], **carry)


def _ffn_bwd(name, dxo, r, ab, f, x, mod, k0, lng, w2, w1t, w3t, seq):
    t = dxo.shape[0]
    tm = 256

    def fn(ctx, tiled, halos, exs, res, outs, accs, exaccs, scr):
        dxo_ref, r_ref, ab_ref, f_ref, x_ref = tiled
        mod_ref, = exs
        g_ref, w2_ref, w1t_ref, w3t_ref = res
        dx_ref, df_ref, dab_ref = outs
        sc, g = mod_ref[0, k0 + 1:k0 + 2, :], mod_ref[0, k0 + 2:k0 + 3, :]
        dr, dgam, dbet = _ln_bwd(r_ref[...], g_ref[...], dxo_ref[...])
        _acc(accs[0], dgam)
        _acc(accs[1], dbet)
        _acc(exaccs[0].at[0], jnp.sum(0.5 * f_ref[...] * dr, axis=0, keepdims=True))
        df = _mx(0.5 * g * dr)
        df_ref[...] = df
        dh = jnp.zeros((tm, D), F32)
        for j in range(2):
            blk = slice(j * FB, (j + 1) * FB)
            ds = _dot_nt(df, w2_ref[blk, :])
            a = ab_ref[:, blk].astype(F32)
            b = ab_ref[:, FF + j * FB:FF + (j + 1) * FB].astype(F32)
            sig = _sigmoid(a)
            da = _mx(ds * b * (sig * (1.0 + a * (1.0 - sig))))
            db = _mx(ds * (a * sig))
            dab_ref[:, blk] = da
            dab_ref[:, FF + j * FB:FF + (j + 1) * FB] = db
            dh = dh + _dot(da, w1t_ref[blk, :]) + _dot(db, w3t_ref[blk, :])
        dx_ref[...] = ALPHA * dr + dh * (1.0 + sc)
        _acc(exaccs[1].at[0], jnp.sum(dh, axis=0, keepdims=True))
        _acc(exaccs[2].at[0], jnp.sum(dh * x_ref[...], axis=0, keepdims=True))

    b = mod.shape[0]
    return _rowcall(name, fn, t, tm, seq // tm, tiled=[dxo, r, ab, f, x], exs=[mod], res=[lng, w2, w1t, w3t],
                    out_tiled=[_sds((t, D)), _sds((t, D), MXU_DTYPE), _sds((t, 2 * FF), MXU_DTYPE)],
                    out_acc=[_sds((1, D)), _sds((1, D))], out_exacc=[_sds((b, 1, D))] * 3)


def _mod_bwd(name, dab, dr, x, mod, k0, wts, seq, extra=(), nt=False, **carry):
    t = dr.shape[0]
    tm = min(TM_WIDE, seq)
    nin = 1 + len(extra)
    width = dab.shape[1] + sum(e.shape[1] for e in extra)

    def fn(ctx, tiled, halos, exs, res, outs, accs, exaccs, scr):
        parts = tiled[:nin]
        dr_ref, x_ref = tiled[nin:]
        mod_ref, = exs
        sc = mod_ref[0, k0 + 1:k0 + 2, :]
        if nin == 1:
            dp = parts[0][...]
        else:
            dp = jnp.concatenate([_mx(p[...]) for p in parts], axis=1)
            outs[1][...] = dp
        dh, off = jnp.zeros((tm, D), F32), 0
        for w_ref in res:
            k = w_ref.shape[1 if nt else 0]
            dh = dh + (_dot_nt if nt else _dot)(dp[:, off:off + k], w_ref[...])
            off += k
        outs[0][...] = ALPHA * dr_ref[...] + dh * (1.0 + sc)
        _acc(exaccs[0].at[0], jnp.sum(dh, axis=0, keepdims=True))
        _acc(exaccs[1].at[0], jnp.sum(dh * x_ref[...], axis=0, keepdims=True))

    b = mod.shape[0]
    out_tiled = [_sds((t, D))] + ([_sds((t, width), MXU_DTYPE)] if nin > 1 else [])
    return _rowcall(name, fn, t, tm, seq // tm, tiled=[dab, *extra, dr, x], exs=[mod], res=list(wts),
                    out_tiled=out_tiled, out_exacc=[_sds((b, 1, D)), _sds((b, 1, D))], **carry)


def _tn_matmul(name, a, b, bm, bn, bt=1024, carry=None, a_cols=None):
    t = a.shape[0]
    bt = min(bt, t)
    start, m = a_cols or (0, a.shape[1])
    off = start // bm
    n = b.shape[1]
    grid = (m // bm, n // bn, t // bt)
    n_in, n_out = (len(carry.ins), len(carry.out_shapes)) if carry else (0, 0)

    def body(a_ref, b_ref, *refs):
        o_ref = refs[n_in]

        def work():
            @pl.when(pl.program_id(2) == 0)
            def _():
                o_ref[...] = jnp.zeros(o_ref.shape, F32)
            o_ref[...] += lax.dot_general(a_ref[...], b_ref[...], (((0,), (0,)), ((), ())), preferred_element_type=F32)

        if not carry:
            return work()
        step = (pl.program_id(0) * grid[1] + pl.program_id(1)) * grid[2] + pl.program_id(2)
        carry.run((refs[:n_in], refs[n_in + 1:n_in + 1 + n_out], refs[n_in + 1 + n_out:]), step == 0, step == 0,
                  step == grid[0] * grid[1] * grid[2] - 1, work)

    any_spec = pl.BlockSpec(memory_space=pl.ANY)
    outs = _pcall(body, name=name, grid=grid,
                  in_specs=[pl.BlockSpec((bt, bm), lambda i, j, k: (k, i + off)), pl.BlockSpec((bt, bn), lambda i, j, k: (k, j))]
                  + [any_spec] * n_in,
                  out_specs=[pl.BlockSpec((bm, bn), lambda i, j, k: (i, j))] + [any_spec] * n_out,
                  out_shape=[_sds((m, n))] + (carry.out_shapes if carry else []),
                  input_output_aliases={2 + k: 1 + v for k, v in carry.aliases.items()} if carry else {},
                  scratch_shapes=carry.sem_shapes() if carry else [],
                  compiler_params=_cparams(("arbitrary",) * 3 if carry else ("parallel", "parallel", "arbitrary")))(
                      a, b, *(carry.ins if carry else []))
    return (outs[0], outs[1:]) if carry else outs[0]


S5_BLK = 16


def _tn_diag(name, a, b, bt=1024):
    t = a.shape[0]
    bt = min(bt, t)
    rows, cols = S5_BLK * S5P, S5_BLK * S5H
    nblk = a.shape[1] // rows

    def body(a_ref, b_ref, o_ref):
        @pl.when(pl.program_id(1) == 0)
        def _():
            o_ref[...] = jnp.zeros(o_ref.shape, F32)
        o_ref[0] += lax.dot_general(a_ref[...], b_ref[...], (((0,), (0,)), ((), ())), preferred_element_type=F32)

    return _pcall(body, name=name, grid=(nblk, t // bt),
                  in_specs=[pl.BlockSpec((bt, rows), lambda i, k: (k, i)),
                            pl.BlockSpec((bt, cols), lambda i, k: (k, i % (S5G // S5_BLK)))],
                  out_specs=pl.BlockSpec((1, rows, cols), lambda i, k: (i, 0, 0)), out_shape=_sds((nblk, rows, cols)),
                  compiler_params=_cparams(("parallel", "arbitrary")))(a, b)


def _diag_groups(o):
    o = o.reshape(2, S5G // S5_BLK, S5_BLK, S5P, S5_BLK, S5H)
    return jnp.einsum("rbgpgh->rbgph", o).reshape(2, S5G, S5P, S5H)


def _proj_fwd(name, x, mod, w_in, seq):
    t = x.shape[0]
    tm = min(TM_WIDE, seq)

    def fn(ctx, tiled, halos, exs, res, outs, accs, exaccs, scr):
        mod_ref, = exs
        sh, sc = mod_ref[0, 3:4, :], mod_ref[0, 4:5, :]
        h = _mx(tiled[0][...] * (1.0 + sc) + sh)
        outs[0][...] = h
        outs[1][...] = _dot(h, res[0][...])

    return _rowcall(name, fn, t, tm, seq // tm, tiled=[x], exs=[mod], res=[w_in],
                    out_tiled=[_sds((t, D), MXU_DTYPE), _sds((t, PW))])


def _shift_rows(cur, prev8, j):
    if j == 0:
        return cur
    rolled = pltpu.roll(cur, j, 0)
    top = jnp.where(_iota((8, cur.shape[1]), 0) < j, pltpu.roll(prev8, j, 0), rolled[0:8])
    return jnp.concatenate([top, rolled[8:]], axis=0)


def _shift_rows_up(cur, next8, j):
    if j == 0:
        return cur
    n = cur.shape[0]
    rolled = pltpu.roll(cur, n - j, 0)
    bot = jnp.where(_iota((8, cur.shape[1]), 0) >= 8 - j, pltpu.roll(next8, 8 - j, 0), rolled[n - 8:n])
    return jnp.concatenate([rolled[:n - 8], bot], axis=0)


def _softplus(x):
    return jnp.maximum(x, 0.0) + jnp.log(1.0 + jnp.exp(-jnp.abs(x)))


def _ssd_common(proj_ref, xpre, dtb_ref, alog_ref):
    xbc = xpre * _sigmoid(xpre)
    xs, bm, cm = xbc[:, 0:SW], xbc[:, SW:SW + 256], xbc[:, SW + 256:SW + 512]
    dtraw = proj_ref[:, PW - 128:PW] + dtb_ref[...]
    dt = _softplus(dtraw)
    a = -jnp.exp(alog_ref[...])
    tril = (_iota((CH, CH), 0) >= _iota((CH, CH), 1)).astype(F32)
    acs = _dot_hi(tril, dt * a)
    return xs, bm, cm, dtraw, dt, a, acs, acs.T


def _pair_lane(lo, hi):
    r = lo.shape[0]
    return jnp.where(_iota((r, 128), 1) < HP, lo, hi)


def _ssd_fwd(name, proj, conv_w, conv_b, dt_bias, a_log, d_rep, norm_w, seq):
    t = proj.shape[0]
    nb = t // seq

    def fn(ctx, tiled, halos, exs, res, outs, accs, exaccs, scr):
        @pl.when(ctx.seq_first)
        def _():
            scr[0][...] = jnp.zeros(scr[0].shape, F32)

        for b in range(nb):
            one(ctx, res, [r.at[b] for r in tiled + halos + outs + scr])

    def one(ctx, res, refs):
        proj_ref, halo_ref, yo_ref, xpre_ref, y_ref, sprev_ref, state_ref = refs
        cw_ref, cb_ref, dtb_ref, alog_ref, d_ref, nw_ref = res
        raw = proj_ref[:, SW:SW + D]
        prev8 = halo_ref[:, SW:SW + D] * jnp.where(ctx.seq_first, 0.0, 1.0)
        xpre = cb_ref[...] + cw_ref[3:4, :] * raw
        for j in (1, 2, 3):
            xpre = xpre + cw_ref[3 - j:4 - j, :] * _shift_rows(raw, prev8, j)
        xpre_ref[...] = xpre
        xs, bm, cm, dtraw, dt, a, acs, acst = _ssd_common(proj_ref, xpre, dtb_ref, alog_ref)
        causal = _iota((CH, CH), 0) >= _iota((CH, CH), 1)
        lane_lo = _iota((CH, 128), 1) < HP
        sprev = state_ref[...]
        sprev_ref[...] = sprev
        ys = []
        for g in range(NG):
            bmg, cmg = bm[:, g * NS:(g + 1) * NS], cm[:, g * NS:(g + 1) * NS]
            bmt = bmg.T
            cb = _dot(_mx(cmg), _mx(bmt))
            for q in (2 * g, 2 * g + 1):
                xsq = xs[:, 128 * q:128 * q + 128]
                xd = xsq * _pair_lane(dt[:, 2 * q:2 * q + 1], dt[:, 2 * q + 1:2 * q + 2])
                sp = sprev[:, 128 * q:128 * q + 128]
                ydiag = jnp.zeros((CH, 128), F32)
                snew = jnp.zeros((NS, 128), F32)
                for jj in range(2):
                    h = 2 * q + jj
                    col, row = acs[:, h:h + 1], acst[h:h + 1, :]
                    lm = jnp.where(causal, jnp.exp(jnp.minimum(col - row, 0.0)), 0.0)
                    xm = _mx(jnp.where(lane_lo if jj == 0 else ~lane_lo, xd, 0.0))
                    ydiag = ydiag + _dot(_mx(cb * lm), xm)
                    dec_row = jnp.exp(acst[h:h + 1, CH - 1:CH] - row)
                    snew = snew + _dot(_mx(bmt * dec_row), xm)
                e_pair = jnp.exp(_pair_lane(acs[:, 2 * q:2 * q + 1], acs[:, 2 * q + 1:2 * q + 2]))
                yoff = _dot(_mx(cmg), _mx(sp)) * e_pair
                cd = jnp.exp(_pair_lane(acst[2 * q:2 * q + 1, CH - 1:CH], acst[2 * q + 1:2 * q + 2, CH - 1:CH]))
                state_ref[:, 128 * q:128 * q + 128] = cd * sp + snew
                ys.append(ydiag + yoff + d_ref[:, 128 * q:128 * q + 128] * xsq)
        y = jnp.concatenate(ys, axis=1)
        y_ref[...] = y
        z = proj_ref[:, 0:SW]
        yz = y * (z * _sigmoid(z))
        outp = []
        for g in range(NG):
            seg = yz[:, 256 * g:256 * g + 256]
            rinv = lax.rsqrt(jnp.mean(seg * seg, axis=-1, keepdims=True) + LN_EPS)
            outp.append(seg * rinv * nw_ref[:, 256 * g:256 * g + 256])
        yo_ref[...] = _mx(jnp.concatenate(outp, axis=1))

    return _rowcall(name, fn, t, CH, seq // CH, tiled=[proj], halos=[(proj, 8)],
                    res=[conv_w, conv_b, dt_bias, a_log, d_rep, norm_w],
                    out_tiled=[_sds((t, SW), MXU_DTYPE), _sds((t, D)), _sds((t, SW)), _sds((t, SW))],
                    scratch=[pltpu.VMEM((nb, NS, SW), F32)], batch=nb)


def _ssd_bwd(name, dyo, proj, xpre_all, y_all, sprev_all, conv_w, dt_bias, a_log, d_rep, norm_w, seq, **carry):
    t = proj.shape[0]
    nb = t // seq

    def fn(ctx, tiled, halos, exs, res, outs, accs, exaccs, scr):
        @pl.when(ctx.seq_last)
        def _():
            for r in scr:
                r[...] = jnp.zeros(r.shape, F32)

        for b in range(nb):
            one(ctx, res, accs, [r.at[b] for r in tiled + halos + outs + scr])

    def one(ctx, res, accs, refs):
        dyo_ref, proj_ref, xpre_ref, y_ref, sprev_ref, halo_ref, dzx_ref, ddt_ref, ds_ref, nxt_ref = refs
        cw_ref, dtb_ref, alog_ref, d_ref, nw_ref = res
        dnw_acc, dd_acc, dcw_acc, dcb_acc, ddtb_acc, dalog_acc = accs
        xpre = xpre_ref[...]
        xs, bm, cm, dtraw, dt, a, acs, acst = _ssd_common(proj_ref, xpre, dtb_ref, alog_ref)
        y = y_ref[...]
        z = proj_ref[:, 0:SW]
        sz = _sigmoid(z)
        siluz = z * sz
        yz = y * siluz
        dyo = dyo_ref[...]
        dyz_parts, dnw_parts = [], []
        for g in range(NG):
            seg = yz[:, 256 * g:256 * g + 256]
            rinv = lax.rsqrt(jnp.mean(seg * seg, axis=-1, keepdims=True) + LN_EPS)
            yn = seg * rinv
            dseg = dyo[:, 256 * g:256 * g + 256]
            dnw_parts.append(jnp.sum(dseg * yn, axis=0, keepdims=True))
            dyn = dseg * nw_ref[:, 256 * g:256 * g + 256]
            dyz_parts.append(rinv * (dyn - yn * jnp.mean(dyn * yn, axis=-1, keepdims=True)))
        dyz = jnp.concatenate(dyz_parts, axis=1)
        _acc(dnw_acc, jnp.concatenate(dnw_parts, axis=1))
        dy = dyz * siluz
        dz = dyz * y * (sz * (1.0 + z * (1.0 - sz)))
        _acc(dd_acc, jnp.sum(dy * xs, axis=0, keepdims=True))

        causal = _iota((CH, CH), 0) >= _iota((CH, CH), 1)
        anti = _iota((CH, CH), 0) <= _iota((CH, CH), 1)
        lane_lo = _iota((CH, 128), 1) < HP
        lane_id = _iota((CH, 128), 1)
        last_row = _iota((CH, 128), 0) == CH - 1
        sprev = sprev_ref[...]
        dacs = jnp.zeros((CH, 128), F32)
        ddt_x = jnp.zeros((CH, 128), F32)
        dxs_parts, dbm_parts, dcm_parts = [], [], []
        for g in range(NG):
            bmg, cmg = bm[:, g * NS:(g + 1) * NS], cm[:, g * NS:(g + 1) * NS]
            bmt, cmt = bmg.T, cmg.T
            cb = _dot(_mx(cmg), _mx(bmt))
            cbt = _dot(_mx(bmg), _mx(cmt))
            dcb = jnp.zeros((CH, CH), F32)
            dcbt = jnp.zeros((CH, CH), F32)
            dbmg = jnp.zeros((CH, NS), F32)
            dcmg = jnp.zeros((CH, NS), F32)
            for q in (2 * g, 2 * g + 1):
                sl = slice(128 * q, 128 * q + 128)
                xsq = xs[:, sl]
                dtp = _pair_lane(dt[:, 2 * q:2 * q + 1], dt[:, 2 * q + 1:2 * q + 2])
                xd = xsq * dtp
                dyq = dy[:, sl]
                sp = sprev[:, sl]
                dsn = ds_ref[:, sl]
                e_pair = jnp.exp(_pair_lane(acs[:, 2 * q:2 * q + 1], acs[:, 2 * q + 1:2 * q + 2]))
                cd = jnp.exp(_pair_lane(acst[2 * q:2 * q + 1, CH - 1:CH], acst[2 * q + 1:2 * q + 2, CH - 1:CH]))
                dye = dyq * e_pair
                dcmg = dcmg + _dot(_mx(dye), _mx(sp.T))
                dsp = _dot(_mx(cmt), _mx(dye)) + cd * dsn
                yoff = _dot(_mx(cmg), _mx(sp)) * e_pair
                dacs_lane = dyq * yoff
                dxd = jnp.zeros((CH, 128), F32)
                sds = jnp.sum(dsn * sp, axis=0, keepdims=True) * cd
                for jj in range(2):
                    h = 2 * q + jj
                    hm = lane_lo if jj == 0 else ~lane_lo
                    col, row = acs[:, h:h + 1], acst[h:h + 1, :]
                    lm = jnp.where(causal, jnp.exp(jnp.minimum(col - row, 0.0)), 0.0)
                    lmt = jnp.where(anti, jnp.exp(jnp.minimum(row - col, 0.0)), 0.0)
                    xm = _mx(jnp.where(hm, xd, 0.0))
                    dym = _mx(jnp.where(hm, dyq, 0.0))
                    gm = _dot_nt(dym, xm)
                    gmt = _dot_nt(xm, dym)
                    dcb = dcb + gm * lm
                    dcbt = dcbt + gmt * lmt
                    dxd = dxd + _dot(_mx(cbt * lmt), dym)
                    w = gm * cb * lm
                    wt = gmt * cbt * lmt
                    dacs_h = jnp.sum(w, axis=1, keepdims=True) - jnp.sum(wt, axis=1, keepdims=True)
                    alast = acst[h:h + 1, CH - 1:CH]
                    dec_col = jnp.exp(alast - col)
                    dsm = _mx(jnp.where(hm[0:NS], dsn, 0.0))
                    qh = _dot_nt(xm, dsm)
                    dbmg = dbmg + qh * dec_col
                    ddec = jnp.sum(qh * bmg, axis=1, keepdims=True)
                    dxd = dxd + _dot(_mx(bmg * dec_col), dsm)
                    dacs_h = dacs_h - ddec * dec_col
                    dacs_h = dacs_h + jnp.sum(jnp.where(hm, dacs_lane, 0.0), axis=1, keepdims=True)
                    tail = jnp.sum(ddec * dec_col, axis=0, keepdims=True) + jnp.sum(
                        jnp.where(hm[0:1], sds, 0.0), axis=1, keepdims=True)
                    dacs = dacs + jnp.where(lane_id == h, dacs_h, 0.0) + jnp.where(
                        last_row & (lane_id == h), tail, 0.0)
                ds_ref[:, sl] = dsp
                for jj in range(2):
                    h = 2 * q + jj
                    hm = lane_lo if jj == 0 else ~lane_lo
                    ddt_x = ddt_x + jnp.where(lane_id == h, jnp.sum(jnp.where(hm, dxd * xsq, 0.0), axis=1, keepdims=True), 0.0)
                dxs_parts.append(dxd * dtp + d_ref[:, sl] * dyq)
            dcmg = dcmg + _dot(_mx(dcb), _mx(bmg))
            dbmg = dbmg + _dot(_mx(dcbt), _mx(cmg))
            dbm_parts.append(dbmg)
            dcm_parts.append(dcmg)
        triu = (_iota((CH, CH), 0) <= _iota((CH, CH), 1)).astype(F32)
        dadt = _dot_hi(triu, dacs)
        ddt = dadt * a + ddt_x
        _acc(dalog_acc, jnp.sum(dadt * dt, axis=0, keepdims=True) * a)
        ddtraw = ddt * _sigmoid(dtraw)
        ddt_ref[...] = ddtraw
        _acc(ddtb_acc, jnp.sum(ddtraw, axis=0, keepdims=True))
        dxbc = jnp.concatenate(dxs_parts + dbm_parts + dcm_parts, axis=1)
        sx = _sigmoid(xpre)
        dpre = dxbc * (sx * (1.0 + xpre * (1.0 - sx)))
        _acc(dcb_acc, jnp.sum(dpre, axis=0, keepdims=True))
        raw = proj_ref[:, SW:SW + D]
        prev8 = halo_ref[:, SW:SW + D] * jnp.where(ctx.seq_first, 0.0, 1.0)
        next8 = nxt_ref[...]
        draw = cw_ref[3:4, :] * dpre
        dcw = [None] * 4
        dcw[3] = jnp.sum(dpre * raw, axis=0, keepdims=True)
        for j in (1, 2, 3):
            dcw[3 - j] = jnp.sum(dpre * _shift_rows(raw, prev8, j), axis=0, keepdims=True)
            draw = draw + cw_ref[3 - j:4 - j, :] * _shift_rows_up(dpre, next8, j)
        _acc(dcw_acc, jnp.concatenate(dcw + [jnp.zeros((4, D), F32)], axis=0))
        nxt_ref[...] = dpre[0:8]
        dzx_ref[:, 0:SW] = dz
        dzx_ref[:, SW:SW + D] = draw

    return _rowcall(name, fn, t, CH, seq // CH, tiled=[dyo, proj, xpre_all, y_all, sprev_all], halos=[(proj, 8)],
                    res=[conv_w, dt_bias, a_log, d_rep, norm_w],
                    out_tiled=[_sds((t, SW + D)), _sds((t, 128))],
                    out_acc=[_sds((1, SW)), _sds((1, SW)), _sds((8, D)), _sds((1, D)), _sds((1, 128)), _sds((1, 128))],
                    scratch=[pltpu.VMEM((nb, NS, SW), F32), pltpu.VMEM((nb, 8, D), F32)], reverse=True, batch=nb, **carry)


def _gelu(y):
    k = math.sqrt(2.0 / math.pi)
    return 0.5 * y * (1.0 + jnp.tanh(k * (y + 0.044715 * y * y * y)))


def _gelu_grad(y):
    k = math.sqrt(2.0 / math.pi)
    th = jnp.tanh(k * (y + 0.044715 * y * y * y))
    return 0.5 * (1.0 + th) + 0.5 * y * (1.0 - th * th) * k * (1.0 + 3.0 * 0.044715 * y * y)


S5T = 256


def _cmul_add(xr, xi, ar, ai, sr, si):
    return xr + ar * sr - ai * si, xi + ar * si + ai * sr


def _s5_fwd(name, proj, bbd, cbd, pw, tab, d5, w_glu, b_glu, seq, **carry):
    t = proj.shape[0]
    tm = S5T

    def fn(ctx, tiled, halos, exs, res, outs, accs, exaccs, scr):
        proj_ref, = tiled
        bbd_ref, cbd_ref, pw_ref, tab_ref, d_ref, wg_ref, bg_ref = res
        out_ref, xst_ref, y_ref, xb_ref, ub_ref = outs
        carry_ref, = scr

        @pl.when(ctx.seq_first)
        def _():
            carry_ref[...] = jnp.zeros(carry_ref.shape, F32)

        u = proj_ref[:, 1536:2048]
        bu = _dot(_mx(u), bbd_ref[...])
        xr, xi = bu[:, :S5L].reshape(tm // 8, 8, S5L), bu[:, S5L:].reshape(tm // 8, 8, S5L)
        for k, sh in enumerate((1, 2, 4)):
            xr, xi = _cmul_add(xr, xi, pw_ref[k, :, :S5L], pw_ref[k, :, S5L:], pltpu.roll(xr, sh, 1), pltpu.roll(xi, sh, 1))
        cr, ci = carry_ref[0:1, :S5L], carry_ref[0:1, S5L:]
        for i in range(tm // 8):
            tr, ti = _cmul_add(xr[i], xi[i], tab_ref[:, :S5L], tab_ref[:, S5L:], cr, ci)
            xst_ref[8 * i:8 * i + 8, :S5L] = tr
            xst_ref[8 * i:8 * i + 8, S5L:] = ti
            cr, ci = tr[7:8], ti[7:8]
        carry_ref[0:1, :S5L] = cr
        carry_ref[0:1, S5L:] = ci
        xb = _mx(xst_ref[...])
        xb_ref[...] = xb
        ub_ref[...] = _mx(u)
        y = _dot(xb, cbd_ref[...]) + u * d_ref[...]
        y_ref[...] = y
        g = _gelu(y)
        v = _dot(_mx(g), wg_ref[...]) + bg_ref[...]
        out_ref[...] = _mx(g * _sigmoid(v))

    return _rowcall(name, fn, t, tm, seq // tm, tiled=[proj], res=[bbd, cbd, pw, tab, d5, w_glu, b_glu],
                    out_tiled=[_sds((t, SW), MXU_DTYPE), _sds((t, 2 * S5L)), _sds((t, SW)),
                               _sds((t, 2 * S5L), MXU_DTYPE), _sds((t, SW), MXU_DTYPE)],
                    scratch=[pltpu.VMEM((8, 2 * S5L), F32)], **carry)


def _s5_bwd(name, dout, proj, xst, y_all, bbdt, cbdt, pwc, tabc, d5, w_glu, b_glu, seq, **carry):
    t = proj.shape[0]
    tm = S5T

    def fn(ctx, tiled, halos, exs, res, outs, accs, exaccs, scr):
        dout_ref, proj_ref, xst_ref, y_ref = tiled
        halo_ref, = halos
        bbdt_ref, cbdt_ref, pw_ref, tab_ref, d_ref, wg_ref, bg_ref = res
        du_ref, lam_ref, dyb_ref, gb_ref, dvb_ref = outs
        da_acc, dd_acc, dbg_acc = accs
        carry_ref, lamf_ref = scr

        @pl.when(ctx.seq_last)
        def _():
            carry_ref[...] = jnp.zeros(carry_ref.shape, F32)

        u = proj_ref[:, 1536:2048]
        y = y_ref[...]
        g = _gelu(y)
        v = _dot(_mx(g), wg_ref[...]) + bg_ref[...]
        sg = _sigmoid(v)
        dout = dout_ref[...]
        dv = dout * g * sg * (1.0 - sg)
        dvb = _mx(dv)
        dvb_ref[...] = dvb
        gb_ref[...] = _mx(g)
        _acc(dbg_acc, jnp.sum(dv, axis=0, keepdims=True))
        dg = dout * sg + _dot_nt(dvb, wg_ref[...])
        dy = dg * _gelu_grad(y)
        dyb = _mx(dy)
        dyb_ref[...] = dyb
        _acc(dd_acc, jnp.sum(dy * u, axis=0, keepdims=True))
        dx = _dot(dyb, cbdt_ref[...])
        xr, xi = dx[:, :S5L].reshape(tm // 8, 8, S5L), dx[:, S5L:].reshape(tm // 8, 8, S5L)
        for k, sh in enumerate((1, 2, 4)):
            xr, xi = _cmul_add(xr, xi, pw_ref[k, :, :S5L], pw_ref[k, :, S5L:], pltpu.roll(xr, 8 - sh, 1),
                               pltpu.roll(xi, 8 - sh, 1))
        cr, ci = carry_ref[0:1, :S5L], carry_ref[0:1, S5L:]
        for i in reversed(range(tm // 8)):
            tr, ti = _cmul_add(xr[i], xi[i], tab_ref[:, :S5L], tab_ref[:, S5L:], cr, ci)
            lamf_ref[8 * i:8 * i + 8, :S5L] = tr
            lamf_ref[8 * i:8 * i + 8, S5L:] = ti
            cr, ci = tr[0:1], ti[0:1]
        carry_ref[0:1, :S5L] = cr
        carry_ref[0:1, S5L:] = ci
        lam = lamf_ref[...]
        lamb = _mx(lam)
        lam_ref[...] = lamb
        du_ref[...] = dy * d_ref[...] + _dot(lamb, bbdt_ref[...])
        prev8 = halo_ref[...] * jnp.where(ctx.seq_first, 0.0, 1.0)
        xprev = _shift_rows(xst_ref[...], prev8, 1)
        lr, li = lam[:, :S5L], lam[:, S5L:]
        pr, pi = xprev[:, :S5L], xprev[:, S5L:]
        dar = jnp.sum(lr * pr + li * pi, axis=0, keepdims=True)
        dai = jnp.sum(li * pr - lr * pi, axis=0, keepdims=True)
        _acc(da_acc, jnp.concatenate([dar, dai], axis=1))

    return _rowcall(name, fn, t, tm, seq // tm, tiled=[dout, proj, xst, y_all], halos=[(xst, 8)],
                    res=[bbdt, cbdt, pwc, tabc, d5, w_glu, b_glu],
                    out_tiled=[_sds((t, SW)), _sds((t, 2 * S5L), MXU_DTYPE), _sds((t, SW), MXU_DTYPE),
                               _sds((t, SW), MXU_DTYPE), _sds((t, SW), MXU_DTYPE)],
                    out_acc=[_sds((1, 2 * S5L)), _sds((1, SW)), _sds((1, SW))],
                    scratch=[pltpu.VMEM((8, 2 * S5L), F32), pltpu.VMEM((tm, 2 * S5L), F32)], reverse=True, **carry)


def _out_fwd(name, yssd, ys5, x1, mod, w_out, lng, lnb, seq):
    t = x1.shape[0]
    tm = min(TM_WIDE, seq)

    def fn(ctx, tiled, halos, exs, res, outs, accs, exaccs, scr):
        ya_ref, yb_ref, x_ref = tiled
        mod_ref, = exs
        w_ref, g_ref, b_ref = res
        m = _dot(ya_ref[...], w_ref[0:SW, :]) + _dot(yb_ref[...], w_ref[SW:2 * SW, :])
        r = ALPHA * x_ref[...] + mod_ref[0, 5:6, :] * m
        outs[0][...] = _ln_fwd(r, g_ref[...], b_ref[...])
        outs[1][...] = r
        outs[2][...] = m

    return _rowcall(name, fn, t, tm, seq // tm, tiled=[yssd, ys5, x1], exs=[mod], res=[w_out, lng, lnb],
                    out_tiled=[_sds((t, D)), _sds((t, D)), _sds((t, D))])


def _out_bwd(name, dxo, r, m, mod, lng, w_out, seq, **carry):
    t = dxo.shape[0]
    tm = min(TM_WIDE, seq)

    def fn(ctx, tiled, halos, exs, res, outs, accs, exaccs, scr):
        dxo_ref, r_ref, m_ref = tiled
        mod_ref, = exs
        g_ref, w_ref = res
        dr, dgam, dbet = _ln_bwd(r_ref[...], g_ref[...], dxo_ref[...])
        outs[0][...] = dr
        _acc(accs[0], dgam)
        _acc(accs[1], dbet)
        _acc(exaccs[0].at[0], jnp.sum(dr * m_ref[...], axis=0, keepdims=True))
        dm = _mx(mod_ref[0, 5:6, :] * dr)
        outs[1][...] = dm
        dyc = _dot_nt(dm, w_ref[...])
        outs[2][...] = dyc[:, 0:SW]
        outs[3][...] = dyc[:, SW:2 * SW]

    b = mod.shape[0]
    return _rowcall(name, fn, t, tm, seq // tm, tiled=[dxo, r, m], exs=[mod], res=[lng, w_out],
                    out_tiled=[_sds((t, D)), _sds((t, D), MXU_DTYPE), _sds((t, SW)), _sds((t, SW))],
                    out_acc=[_sds((1, D)), _sds((1, D))], out_exacc=[_sds((b, 1, D))], **carry)


def _s5_discretise(a_re, a_im, log_dt, b_re, b_im):
    dt = jnp.exp(log_dt)[:, None]
    mag = jnp.exp(dt * a_re)
    ab_re, ab_im = mag * jnp.cos(dt * a_im), mag * jnp.sin(dt * a_im)
    den = a_re * a_re + a_im * a_im
    nr, ni = ab_re - 1.0, ab_im
    f_re, f_im = (nr * a_re + ni * a_im) / den, (ni * a_re - nr * a_im) / den
    bb_re = f_re[..., None] * b_re - f_im[..., None] * b_im
    bb_im = f_re[..., None] * b_im + f_im[..., None] * b_re
    return ab_re, ab_im, bb_re, bb_im


def _s5_tables(ab_re, ab_im):
    ar, ai = ab_re.reshape(1, S5L), ab_im.reshape(1, S5L)
    pows = [(ar, ai)]
    for _ in range(7):
        pr, pi = pows[-1]
        pows.append((pr * ar - pi * ai, pr * ai + pi * ar))

    def pack(rows, sign):
        return jnp.concatenate([jnp.concatenate([r for r, _ in rows], axis=0),
                                jnp.concatenate([sign * i for _, i in rows], axis=0)], axis=1)

    row = jnp.arange(8)[:, None]
    pw = jnp.stack([jnp.where(row >= sh, pack([pows[sh - 1]], 1.0), 0.0) for sh in (1, 2, 4)])
    pwc = jnp.stack([jnp.where(row < 8 - sh, pack([pows[sh - 1]], -1.0), 0.0) for sh in (1, 2, 4)])
    tab = pack(pows, 1.0)
    tabc = pack(pows[::-1], -1.0)
    return pw, tab, pwc, tabc


class _GradGroup:
    def __init__(self, tag, grads, place):
        self.tag, self.names, (self.half, self.chip) = tag, list(grads), place
        self.gsh = [_shard_major(g) if n in COL_SHARDED else g.reshape(4, g.shape[0] // 4, g.shape[1])
                    for n, g in grads.items()]

    def sibling(self):
        return _rs_sibling_stage(self.gsh)

    def chips(self, received):
        self.from_sibling = received
        return _rs_chips_stage(_rs_add(self.tag + "_add", self.gsh, received, self.half))

    def join(self, received):
        return _rs_join_stage(_rs_sum(self.tag + "_sum", self.gsh, self.from_sibling, received, self.chip, self.half))

    def result(self, joined):
        return dict(zip(self.names, joined))


def _hid(fn, *args, stage=None, **kw):
    if stage is None:
        return fn(*args, **kw), None
    return fn(*args, carry=stage, **kw)


def _local_step(x, tgt, mod, w, sp, seq, dist=None):
    t = x.shape[0]
    mxu = MXU_DTYPE
    big = {}

    def group(tag, grads):
        if dist is None:
            big.update(grads)
            return None
        return _GradGroup(tag, grads, dist[2])

    mid = {"carry_mid": 0.6} if dist else {}
    (h1, ab1, s1), got = _hid(_ffn_fwd_hidden, "ffn1_fwd_a", x, mod, 0, w["ffn1_w1t"], w["ffn1_w3t"], seq,
                              stage=dist and dist[0][0], **mid)
    if dist:
        w = {**w, **dist[0][1](got)}
    x1, r1, f1 = _ffn_fwd_out("ffn1_fwd_b", s1, x, mod, 0, w["ffn1_w2"], sp["ln1_g"], sp["ln1_b"], seq)
    h2, proj = _proj_fwd("proj_fwd", x1, mod, w["w_in"], seq)
    yssd, xpre, yraw, sprev = _ssd_fwd("ssd_fwd", proj, sp["conv_w"], sp["conv_b"], sp["dt_bias"], sp["a_log"],
                                       sp["d_rep"], sp["ssd_norm_w"], seq)
    (ab_re, ab_im, bb_re, bb_im), disc_vjp = jax.vjp(_s5_discretise, sp["s5_a_re"], sp["s5_a_im"], sp["s5_log_dt"],
                                                     sp["s5_b_re"], sp["s5_b_im"])
    eye = jnp.eye(S5G, dtype=F32)
    bbd = jnp.concatenate([jnp.einsum("gk,gph->ghkp", eye, bb_re).reshape(SW, S5L),
                           jnp.einsum("gk,gph->ghkp", eye, bb_im).reshape(SW, S5L)], axis=1).astype(mxu)
    cbd = jnp.concatenate([jnp.einsum("gk,ghp->gpkh", eye, sp["s5_c_re"]).reshape(S5L, SW),
                           -jnp.einsum("gk,ghp->gpkh", eye, sp["s5_c_im"]).reshape(S5L, SW)], axis=0).astype(mxu)
    bbdt = jnp.concatenate([jnp.einsum("gk,gph->kpgh", eye, bb_re).reshape(S5L, SW),
                            jnp.einsum("gk,gph->kpgh", eye, bb_im).reshape(S5L, SW)], axis=0).astype(mxu)
    cbdt = jnp.concatenate([jnp.einsum("gk,ghp->khgp", eye, sp["s5_c_re"]).reshape(SW, S5L),
                            -jnp.einsum("gk,ghp->khgp", eye, sp["s5_c_im"]).reshape(SW, S5L)], axis=1).astype(mxu)
    pw, tab, pwc, tabc = _s5_tables(lax.stop_gradient(ab_re), lax.stop_gradient(ab_im))
    (ys5, xst, y5, xstb, ub), got = _hid(_s5_fwd, "s5_fwd", proj, bbd, cbd, pw, tab, sp["s5_d"], w["w_glu"], sp["b_glu"],
                                         seq, stage=dist and dist[1][0], **mid)
    if dist:
        w = {**w, **dist[1][1](got)}
    x2, r2, m2 = _out_fwd("out_fwd", yssd, ys5, x1, mod, w["w_out"], sp["ln2_g"], sp["ln2_b"], seq)
    h3, ab3, s3 = _ffn_fwd_hidden("ffn2_fwd_a", x2, mod, 6, w["ffn2_w1t"], w["ffn2_w3t"], seq)
    x3, r3, f3, dy3, loss_acc = _ffn_fwd_out("ffn2_fwd_b", s3, x2, mod, 6, w["ffn2_w2"], sp["ln3_g"], sp["ln3_b"], seq, tgt=tgt)
    dx2, df3, dab3, dg3g, dg3b, dgate3, dsh3, dsc3 = _ffn_bwd("ffn2_bwd", dy3, r3, ab3, f3, x2, mod, 6, sp["ln3_g"],
                                                              w["ffn2_w2"], w["ffn2_w1t"], w["ffn2_w3t"], seq)
    grp = group("rs_ffn2", dict(ffn2_w1=_tn_matmul("ffn2_dw1", dab3, h3, FB, D, a_cols=(0, FF)),
                                ffn2_w3=_tn_matmul("ffn2_dw3", dab3, h3, FB, D, a_cols=(FF, FF)),
                                ffn2_w2=_tn_matmul("ffn2_dw2", s3, df3, FB, D)))
    (dr2, dm2, dyssd, dys5, dg2g, dg2b, dgate2), got = _hid(_out_bwd, "out_bwd", dx2, r2, m2, mod, sp["ln2_g"],
                                                          w["w_out"], seq, stage=grp and grp.sibling())
    g_w_out = jnp.concatenate([_tn_matmul("dw_out_a", yssd, dm2, SW, D), _tn_matmul("dw_out_b", ys5, dm2, SW, D)], axis=0)
    (du, lam, dy5b, g5b, dv5b, da5, dd5, dbglu), got = _hid(
        _s5_bwd, "s5_bwd", dys5, proj, xst, y5, bbdt, cbdt, pwc, tabc, sp["s5_d"], w["w_glu"], sp["b_glu"],
        seq, stage=grp and grp.chips(got))
    g_w_glu = _tn_matmul("dw_glu", g5b, dv5b, SW, SW)
    dbb = _diag_groups(_tn_diag("s5_db", lam, ub))
    dcc = _diag_groups(_tn_diag("s5_dc", xstb, dy5b))
    (dzx, ddt, dnw, ddl, dcw, dcb, ddtb, dalog), got = _hid(
        _ssd_bwd, "ssd_bwd", dyssd, proj, xpre, yraw, sprev, sp["conv_w"], sp["dt_bias"], sp["a_log"], sp["d_rep"],
        sp["ssd_norm_w"], seq, stage=grp and grp.join(got))
    if grp:
        big.update(grp.result(got))
    dx1, dproj, dsh2, dsc2 = _mod_bwd("proj_bwd", dzx, dr2, x1, mod, 3, [w["w_in"]], seq, extra=(du, ddt), nt=True)
    gwi = _tn_matmul("dw_in", h2, dproj, D, PW)
    grp = group("rs_mix", dict(w_in=jnp.concatenate([gwi[:, :1536], gwi[:, 2048:2056], gwi[:, 1536:2048]], axis=1),
                               w_glu=g_w_glu, w_out=g_w_out))
    (dr1, df1, dab1, dg1g, dg1b, dgate1), got = _hid(_ffn_bwd1, "ffn1_bwd1", dx1, r1, ab1, f1, mod, 0, sp["ln1_g"],
                                                          w["ffn1_w2"], seq, stage=grp and grp.sibling())
    g1, got = _hid(_tn_matmul, "ffn1_dw1", dab1, h1, FB, D, a_cols=(0, FF), stage=grp and grp.chips(got))
    g3, got = _hid(_tn_matmul, "ffn1_dw3", dab1, h1, FB, D, a_cols=(FF, FF), stage=grp and grp.join(got))
    if grp:
        big.update(grp.result(got))
    grp = group("rs_ffn1a", dict(ffn1_w1=g1, ffn1_w3=g3))
    g2, got = _hid(_tn_matmul, "ffn1_dw2", s1, df1, FB, D, stage=grp and grp.sibling())
    grp2 = group("rs_ffn1b", dict(ffn1_w2=g2))
    (dx0, dsh1, dsc1), got = _hid(_mod_bwd, "ffn1_bwd2", dab1, dr1, x, mod, 0, [w["ffn1_w1t"], w["ffn1_w3t"]], seq,
                                  stage=grp and _merge_stages(grp.chips(got), grp2.sibling()))
    if grp:
        n = len(grp.names)
        got = _run_stage("rs_ffn1_tail", _merge_stages(grp.join(got[:n]), grp2.chips(got[n:])))
        big.update(grp.result(got[:n]))
        big.update(grp2.result(_run_stage("rs_ffn1b_join", grp2.join(got[n:]))))
    dmod = jnp.concatenate([dsh1, dsc1, dgate1, dsh2, dsc2, dgate2, dsh3, dsc3, dgate3], axis=1)
    dc_re, dc_im = dcc[0].transpose(0, 2, 1), -dcc[1].transpose(0, 2, 1)
    g_a_re, g_a_im, g_log_dt, g_b_re, g_b_im = disc_vjp(
        (da5[:, :S5L].reshape(S5G, S5P), da5[:, S5L:].reshape(S5G, S5P), dbb[0], dbb[1]))
    small = dict(ln1_g=dg1g, ln1_b=dg1b, ln2_g=dg2g, ln2_b=dg2b, ln3_g=dg3g, ln3_b=dg3b, conv_w=dcw[0:4], conv_b=dcb,
                 dt_bias=ddtb[:, :NH], a_log=dalog[:, :NH], d_ssd=jnp.sum(ddl.reshape(NH, HP), axis=1).reshape(1, NH),
                 ssd_norm_w=dnw, s5_a_re=g_a_re, s5_a_im=g_a_im, s5_log_dt=g_log_dt, s5_b_re=g_b_re, s5_b_im=g_b_im,
                 s5_c_re=dc_re, s5_c_im=dc_im, s5_d=dd5, w_glu_b=dbglu)
    return loss_acc[0, 0], dx0, dmod, big, small


def _place():
    return lax.axis_index("x"), lax.axis_index("y"), lax.axis_index("c")


def _other_chips(x, y):
    return [(1 - x, y), (x, 1 - y), (1 - x, 1 - y)]


def _allgather8(name, a):
    r, n = a.shape

    def body(x_ref, out_ref, send_sems, recv_sems, local_sem):
        x, y, c = _place()
        me, sibling = (x, y, c), (x, y, 1 - c)
        chips = _other_chips(x, y)

        def rows(px, py, pc):
            return out_ref.at[pl.ds(pl.multiple_of((4 * px + 2 * py + pc) * r, 8), r), :]

        def copy(k, block, to, src=None):
            return pltpu.make_async_remote_copy(src_ref=rows(*block) if src is None else src, dst_ref=rows(*block),
                                                send_sem=send_sems.at[k], recv_sem=recv_sems.at[k], device_id=to,
                                                device_id_type=MESH_T)

        mine = pltpu.make_async_copy(x_ref, rows(*me), local_sem)
        mine.start()
        first = [copy(0, me, sibling, src=x_ref)]
        first += [copy(1 + j, me, (*chip, c), src=x_ref) for j, chip in enumerate(chips)]
        for cp in first:
            cp.start()
        passed = [copy(4 + j, (*chip, c), sibling) for j, chip in enumerate(chips)]
        for j, chip in enumerate(chips):
            copy(1 + j, (*chip, c), me).wait_recv()
            passed[j].start()
        copy(0, sibling, me).wait_recv()
        for j, chip in enumerate(chips):
            copy(4 + j, (*chip, 1 - c), me).wait_recv()
        for cp in first + passed:
            cp.wait_send()
        mine.wait()

    out = _pcall(body, name=name, out_shape=_sds((8 * r, n), a.dtype),
                 in_specs=[pl.BlockSpec(memory_space=pltpu.VMEM)], out_specs=pl.BlockSpec(memory_space=pltpu.VMEM),
                 scratch_shapes=[pltpu.SemaphoreType.DMA((7,)), pltpu.SemaphoreType.DMA((7,)), pltpu.SemaphoreType.DMA],
                 compiler_params=_cparams())(a)
    return out.reshape(8, r, n)


class _Stage:
    def __init__(self, ins, out_shapes, n_sems, start, finish, mid=None, aliases=None):
        self.ins, self.out_shapes, self.n_sems = list(ins), list(out_shapes), tuple(n_sems)
        self.start, self.mid, self.finish = start, mid, finish
        self.aliases = dict(aliases or {})

    def sem_shapes(self):
        return [pltpu.SemaphoreType.DMA((n,)) for n in self.n_sems]

    def run(self, refs, at_start=None, at_mid=None, at_finish=None, between=None):
        if between is None:
            for part in (self.start, self.mid, self.finish):
                if part is not None:
                    part(*refs)
            return
        pl.when(at_start)(functools.partial(self.start, *refs))
        if self.mid is not None:
            pl.when(at_mid)(functools.partial(self.mid, *refs))
        between()
        pl.when(at_finish)(functools.partial(self.finish, *refs))


def _merge_stages(a, b):
    n_in, n_out, n_sem = len(a.ins), len(a.out_shapes), len(a.n_sems)

    def both(fa, fb):
        def run(ins, outs, sems):
            fa(ins[:n_in], outs[:n_out], sems[:n_sem])
            fb(ins[n_in:], outs[n_out:], sems[n_sem:])
        return run

    aliases = {**a.aliases, **{n_in + k: n_out + v for k, v in b.aliases.items()}}
    return _Stage(a.ins + b.ins, a.out_shapes + b.out_shapes, a.n_sems + b.n_sems, both(a.start, b.start),
                  both(a.finish, b.finish), aliases=aliases)


def _run_stage(name, st):
    n_in, n_out = len(st.ins), len(st.out_shapes)

    def body(*refs):
        st.run((refs[:n_in], refs[n_in:n_in + n_out], refs[n_in + n_out:]))

    any_spec = pl.BlockSpec(memory_space=pl.ANY)
    return _pcall(body, name=name, out_shape=st.out_shapes, in_specs=[any_spec] * n_in, out_specs=[any_spec] * n_out,
                  input_output_aliases=st.aliases, scratch_shapes=st.sem_shapes(), compiler_params=_cparams())(*st.ins)


def _rows(ref_rows, half, align):
    hr = ref_rows // 2
    return pl.ds(pl.multiple_of(half * hr, align), hr)


def _gather_stage(shards):
    n = len(shards)
    pairs = [(i, j) for i in range(n) for j in range(3)]
    align = [32 // s.dtype.itemsize for s in shards]

    def env(ins, outs, sems):
        x, y, c = _place()
        chips = _other_chips(x, y)

        def copy(i, k, chip, half, to, src=None):
            dst = outs[i].at[2 * chip[0] + chip[1], _rows(shards[i].shape[0], half, align[i])]
            return pltpu.make_async_remote_copy(src_ref=dst if src is None else src, dst_ref=dst,
                                                send_sem=sems[0].at[6 * i + k], recv_sem=sems[1].at[6 * i + k],
                                                device_id=to, device_id_type=MESH_T)

        def first(i, j):
            return copy(i, j, (x, y), c, (*chips[j], c), src=ins[i].at[_rows(shards[i].shape[0], c, align[i])])

        def passed(i, j, half):
            return copy(i, 3 + j, chips[j], half, (x, y, 1 - c))

        def landed(i, j):
            return copy(i, j, chips[j], c, (x, y, 1 - c))

        return c, first, passed, landed

    def start(ins, outs, sems):
        c, first, passed, landed = env(ins, outs, sems)
        for i, j in pairs:
            first(i, j).start()

    def mid(ins, outs, sems):
        c, first, passed, landed = env(ins, outs, sems)
        for i, j in pairs:
            landed(i, j).wait_recv()
            passed(i, j, c).start()

    def finish(ins, outs, sems):
        c, first, passed, landed = env(ins, outs, sems)
        for i, j in pairs:
            passed(i, j, 1 - c).wait_recv()
        for i, j in pairs:
            first(i, j).wait_send()
            passed(i, j, c).wait_send()

    return _Stage(shards, [_sds((4,) + s.shape, s.dtype) for s in shards], (6 * n, 6 * n), start, finish, mid)


def _rs_sibling_stage(gs):
    n = len(gs)

    def copies(ins, outs, sems):
        x, y, c = _place()
        return [pltpu.make_async_remote_copy(src_ref=ins[i].at[:, _rows(gs[i].shape[1], 1 - c, 8)], dst_ref=outs[i],
                                             send_sem=sems[0].at[i], recv_sem=sems[1].at[i], device_id=(x, y, 1 - c),
                                             device_id_type=MESH_T) for i in range(n)]

    def start(*refs):
        for cp in copies(*refs):
            cp.start()

    def finish(*refs):
        for cp in copies(*refs):
            cp.wait()

    return _Stage(gs, [_sds((4, g.shape[1] // 2, g.shape[2]), g.dtype) for g in gs], (n, n), start, finish)


def _rs_chips_stage(hs):
    n = len(hs)

    def copies(ins, outs, sems):
        x, y, c = _place()
        return [pltpu.make_async_remote_copy(src_ref=ins[i].at[2 * chip[0] + chip[1]], dst_ref=outs[i].at[j],
                                             send_sem=sems[0].at[3 * i + j], recv_sem=sems[1].at[3 * i + j],
                                             device_id=(*chip, c), device_id_type=MESH_T)
                for i in range(n) for j, chip in enumerate(_other_chips(x, y))]

    def start(*refs):
        for cp in copies(*refs):
            cp.start()

    def finish(*refs):
        for cp in copies(*refs):
            cp.wait()

    return _Stage(hs, [_sds((3,) + h.shape[1:], h.dtype) for h in hs], (3 * n, 3 * n), start, finish)


def _rs_join_stage(fs):
    n = len(fs)

    def copy(outs, sems, i, half):
        x, y, c = _place()
        part = outs[i].at[_rows(fs[i].shape[0], c if half == "mine" else 1 - c, 8)]
        return pltpu.make_async_remote_copy(src_ref=part, dst_ref=part, send_sem=sems[0].at[i], recv_sem=sems[1].at[i],
                                            device_id=(x, y, 1 - c), device_id_type=MESH_T)

    def start(ins, outs, sems):
        for i in range(n):
            copy(outs, sems, i, "mine").start()

    def finish(ins, outs, sems):
        for i in range(n):
            copy(outs, sems, i, "theirs").wait_recv()
        for i in range(n):
            copy(outs, sems, i, "mine").wait_send()

    return _Stage(fs, [_sds(f.shape, f.dtype) for f in fs], (n, n), start, finish, aliases={i: i for i in range(n)})


def _row_block(r, cap=2048):
    b = min(r, cap)
    while r % b or b % 8:
        b -= 8
    return b


RS_SPLIT = 2


def _rs_add(name, gs, r1s, sel):
    n = len(gs)

    def body(sel_ref, *refs):
        g_refs, r_refs, b_refs = (refs[k * n:(k + 1) * n] for k in range(3))
        for i in range(n):
            b_refs[i][...] = (g_refs[i][...] + r_refs[i][...]).astype(BF16)

    def blk(g):
        return (1, g.shape[1] // 2 // RS_SPLIT, g.shape[2])

    here = lambda k, j, s: (k, j, 0)
    in_specs = [pl.BlockSpec(blk(g), lambda k, j, s: (k, s[0] * RS_SPLIT + j, 0)) for g in gs]
    in_specs += [pl.BlockSpec(blk(g), here) for g in gs]
    return _pcall(body, name=name, out_shape=[_sds((4, g.shape[1] // 2, g.shape[2]), BF16) for g in gs],
                  grid_spec=pltpu.PrefetchScalarGridSpec(num_scalar_prefetch=1, grid=(4, RS_SPLIT), in_specs=in_specs,
                                                         out_specs=[pl.BlockSpec(blk(g), here) for g in gs]),
                  compiler_params=_cparams(("parallel", "parallel")))(sel.reshape(1).astype(jnp.int32), *gs, *r1s)


def _rs_sum(name, gs, r1s, r2s, chip, half):
    n = len(gs)

    def body(sel_ref, *refs):
        g_refs, h_refs, r_refs, o_refs = (refs[k * n:(k + 1) * n] for k in range(4))
        for i in range(n):
            r = r_refs[i]
            o_refs[i][...] = (((g_refs[i][0] + h_refs[i][0]) + r[0].astype(F32)) + r[1].astype(F32)) + r[2].astype(F32)

    def rows(g):
        return g.shape[1] // 2 // RS_SPLIT

    in_specs = [pl.BlockSpec((1, rows(g), g.shape[2]), lambda j, s: (s[0], s[1] * RS_SPLIT + j, 0)) for g in gs]
    in_specs += [pl.BlockSpec((1, rows(g), g.shape[2]), lambda j, s: (s[0], j, 0)) for g in gs]
    in_specs += [pl.BlockSpec((3, rows(g), g.shape[2]), lambda j, s: (0, j, 0)) for g in gs]
    sel = jnp.stack([chip, half]).astype(jnp.int32)
    return _pcall(body, name=name, out_shape=[_sds(g.shape[1:]) for g in gs],
                  grid_spec=pltpu.PrefetchScalarGridSpec(
                      num_scalar_prefetch=1, grid=(RS_SPLIT,), in_specs=in_specs,
                      out_specs=[pl.BlockSpec((rows(g), g.shape[2]), lambda j, s: (s[1] * RS_SPLIT + j, 0)) for g in gs]),
                  compiler_params=_cparams(("parallel",)))(sel, *gs, *r1s, *r2s)


def _sum8(name, a):
    _, r, n = a.shape
    br = _row_block(r)

    def body(a_ref, o_ref):
        acc = a_ref[0]
        for k in range(1, 8):
            acc = acc + a_ref[k]
        o_ref[...] = acc

    return _pcall(body, name=name, out_shape=_sds((r, n)), grid=(r // br,),
                  in_specs=[pl.BlockSpec((8, br, n), lambda j: (0, j, 0))], out_specs=pl.BlockSpec((br, n), lambda j: (j, 0)),
                  compiler_params=_cparams(("parallel",)))(a)


def _adamw(name, ws, gs, ms, vs, nblk):
    n = len(ws)

    def body(*refs):
        w_refs, g_refs, m_refs, v_refs, d_refs, nm_refs, nv_refs = (refs[k * n:(k + 1) * n] for k in range(7))
        for i in range(n):
            gv = g_refs[i][...]
            nm = ADAM_B1 * m_refs[i][...] + (1.0 - ADAM_B1) * gv
            nv = ADAM_B2 * v_refs[i][...] + (1.0 - ADAM_B2) * (gv * gv)
            nm_refs[i][...] = nm
            nv_refs[i][...] = nv
            m_hat = nm / (1.0 - ADAM_B1 ** ADAM_STEP)
            v_hat = nv / (1.0 - ADAM_B2 ** ADAM_STEP)
            d_refs[i][...] = -ADAM_LR * (m_hat / (jnp.sqrt(v_hat) + ADAM_EPS) + ADAM_WD * w_refs[i][...])

    specs = [pl.BlockSpec((w.shape[0] // nblk, w.shape[1]), lambda j: (j, 0)) for w in ws]
    outs = _pcall(body, name=name, out_shape=[_sds(w.shape) for w in ws] * 3, grid=(nblk,), in_specs=specs * 4,
                  out_specs=specs * 3, compiler_params=_cparams(("parallel",)))(*ws, *gs, *ms, *vs)
    return outs[:n], outs[n:2 * n], outs[2 * n:]


ADA_COLS = 2304
ADA_BLK = 768


def _ada_fwd(name, c_all, w_shard, b_cols):
    nb = c_all.shape[0]

    def body(c_ref, w_ref, b_ref, o_ref):
        cv = c_ref[...]
        cs = _mx(cv * _sigmoid(cv))
        o_ref[...] = _dot(cs, _mx(w_ref[...])) + b_ref[...]

    return _pcall(body, name=name, out_shape=_sds((nb, ADA_COLS)), grid=(ADA_COLS // ADA_BLK,),
                  in_specs=[pl.BlockSpec((nb, D), lambda j: (0, 0)), pl.BlockSpec((D, ADA_BLK), lambda j: (0, j)),
                            pl.BlockSpec((1, ADA_BLK), lambda j: (0, j))],
                  out_specs=pl.BlockSpec((nb, ADA_BLK), lambda j: (0, j)),
                  compiler_params=_cparams(("parallel",)))(c_all, w_shard, b_cols)


def _ada_bwd(name, c_all, dmod_cols, dmod_all):
    nb = c_all.shape[0]

    def body(c_ref, dc_ref, da_ref, gw_ref, gb_ref):
        cv = c_ref[...]
        cs = _mx(cv * _sigmoid(cv))
        gw_ref[...] = lax.dot_general(cs, _mx(dc_ref[...]), (((0,), (0,)), ((), ())), preferred_element_type=F32)

        @pl.when(pl.program_id(0) == 0)
        def _():
            gb_ref[...] = jnp.sum(da_ref[...], axis=0, keepdims=True)

    return _pcall(body, name=name, out_shape=[_sds((D, ADA_COLS)), _sds((1, 9 * D))], grid=(ADA_COLS // ADA_BLK,),
                  in_specs=[pl.BlockSpec((nb, D), lambda j: (0, 0)), pl.BlockSpec((nb, ADA_BLK), lambda j: (0, j)),
                            pl.BlockSpec((nb, 9 * D), lambda j: (0, 0))],
                  out_specs=[pl.BlockSpec((D, ADA_BLK), lambda j: (0, j)), pl.BlockSpec((1, 9 * D), lambda j: (0, 0))],
                  compiler_params=_cparams(("arbitrary",)))(c_all, dmod_cols, dmod_all)


BIG = ("ffn1_w1", "ffn1_w3", "ffn1_w2", "w_in", "w_glu", "w_out", "ffn2_w1", "ffn2_w3", "ffn2_w2")
COL_SHARDED = ("w_in",)
TRANSPOSED = ("ffn1_w1", "ffn1_w3", "ffn2_w1", "ffn2_w3")
SMALL = ("b_ada", "ln1_g", "ln1_b", "conv_w", "conv_b", "dt_bias", "a_log", "d_ssd", "ssd_norm_w", "s5_a_re", "s5_a_im",
         "s5_log_dt", "s5_b_re", "s5_b_im", "s5_c_re", "s5_c_im", "s5_d", "b_glu", "ln2_g", "ln2_b", "ln3_g", "ln3_b")
WEIGHTS = ("w_ada", "b_ada", "ffn1_w1", "ffn1_w3", "ffn1_w2", "ln1_g", "ln1_b", "w_in", "conv_w", "conv_b", "dt_bias",
           "a_log", "d_ssd", "ssd_norm_w", "s5_a_re", "s5_a_im", "s5_log_dt", "s5_b_re", "s5_b_im", "s5_c_re", "s5_c_im",
           "s5_d", "w_glu", "b_glu", "w_out", "ln2_g", "ln2_b", "ffn2_w1", "ffn2_w3", "ffn2_w2", "ln3_g", "ln3_b")
BIG_PAD = 2 * 1024 * 128


def _pack(arrs, mult, axis_keep=0):
    lead = arrs[0].shape[:axis_keep]
    flat = jnp.concatenate([a.reshape(lead + (-1,)) for a in arrs], axis=axis_keep)
    pad = (-flat.shape[-1]) % mult
    if pad:
        flat = jnp.concatenate([flat, jnp.zeros(lead + (pad,), flat.dtype)], axis=axis_keep)
    return flat


def _unpack(flat, shapes):
    out, off = [], 0
    for s in shapes:
        size = math.prod(s)
        out.append(flat[..., off:off + size].reshape(flat.shape[:-1] + tuple(s)))
        off += size
    return out


def _shard_major(a):
    rows, cols = a.shape
    return a.reshape(rows, 4, cols // 4).transpose(1, 0, 2)


def _from_shard_major(a):
    _, rows, w = a.shape
    return a.transpose(1, 0, 2).reshape(rows, 4 * w)


def kernel(x, c, w_ada, b_ada, ffn1_w1, ffn1_w3, ffn1_w2, ln1_g, ln1_b, w_in, conv_w, conv_b, dt_bias, a_log, d_ssd, ssd_norm_w, s5_a_re, s5_a_im, s5_log_dt, s5_b_re, s5_b_im, s5_c_re, s5_c_im, s5_d, w_glu, b_glu, w_out, ln2_g, ln2_b, ffn2_w1, ffn2_w3, ffn2_w2, ln3_g, ln3_b, loss_target, m_w_ada, m_b_ada, m_ffn1_w1, m_ffn1_w3, m_ffn1_w2, m_ln1_g, m_ln1_b, m_w_in, m_conv_w, m_conv_b, m_dt_bias, m_a_log, m_d_ssd, m_ssd_norm_w, m_s5_a_re, m_s5_a_im, m_s5_log_dt, m_s5_b_re, m_s5_b_im, m_s5_c_re, m_s5_c_im, m_s5_d, m_w_glu, m_b_glu, m_w_out, m_ln2_g, m_ln2_b, m_ffn2_w1, m_ffn2_w3, m_ffn2_w2, m_ln3_g, m_ln3_b, v_w_ada, v_b_ada, v_ffn1_w1, v_ffn1_w3, v_ffn1_w2, v_ln1_g, v_ln1_b, v_w_in, v_conv_w, v_conv_b, v_dt_bias, v_a_log, v_d_ssd, v_ssd_norm_w, v_s5_a_re, v_s5_a_im, v_s5_log_dt, v_s5_b_re, v_s5_b_im, v_s5_c_re, v_s5_c_im, v_s5_d, v_w_glu, v_b_glu, v_w_out, v_ln2_g, v_ln2_b, v_ffn2_w1, v_ffn2_w3, v_ffn2_w2, v_ln3_g, v_ln3_b):
    a = dict(locals())
    xi, yi, ci = _place()
    chip = 2 * xi + yi
    dev = 2 * chip + ci
    nb, seq, _ = x.shape
    t = nb * seq
    ndev = 8

    c_rows = nb * D // 128
    c_cw = _allgather8("gather_c", jnp.concatenate([c.reshape(c_rows, 128), conv_w.reshape(-1, 128)], axis=0))
    c_all = c_cw[:, :c_rows].reshape(ndev * nb, D)
    b_cols = lax.dynamic_slice(b_ada, (0, chip * ADA_COLS), (1, ADA_COLS))
    mod_part = _ada_fwd("ada_fwd", c_all, w_ada[0], b_cols)

    def nat(n):
        return jnp.swapaxes(a[n], 1, 2)[0] if n[-7:] in TRANSPOSED else a[n][0]

    def gather(names, extra=()):
        own = [nat(n).astype(MXU_DTYPE) for n in names]

        def weights(pieces):
            w = {}
            for n, mine, piece in zip(names, own, pieces):
                piece = lax.dynamic_update_slice(piece, mine[None], (chip, 0, 0))
                if n in COL_SHARDED:
                    wi = _from_shard_major(piece)
                    w[n] = jnp.concatenate([wi[:, :1536], wi[:, 1544:2056], wi[:, 1536:1544],
                                            jnp.zeros((D, 120), wi.dtype)], axis=1)
                else:
                    w[n + "t" if n in TRANSPOSED else n] = piece.reshape(-1, piece.shape[-1])
            return w

        return _gather_stage(own + list(extra)), weights

    first_stage, first_weights = gather(("ffn1_w1", "ffn1_w3"), extra=[mod_part])
    got = _run_stage("gather_w_first", first_stage)
    w = first_weights(got[:2])
    mod_all = lax.dynamic_update_slice(got[2], mod_part[None], (chip, 0, 0)).transpose(1, 0, 2).reshape(ndev * nb, 9 * D)
    mod = lax.dynamic_slice(mod_all, (nb * dev, 0), (nb, 9 * D)).reshape(nb, 9, D)

    conv_full = _from_shard_major(c_cw[0::2, c_rows:].reshape(4, 4, 256))
    pad8 = lambda v: jnp.concatenate([v.reshape(1, NH), jnp.zeros((1, 128 - NH), F32)], axis=1)
    sp = dict(ln1_g=ln1_g, ln1_b=ln1_b, ln2_g=ln2_g, ln2_b=ln2_b, ln3_g=ln3_g, ln3_b=ln3_b, conv_w=conv_full,
              conv_b=conv_b, dt_bias=pad8(dt_bias), a_log=pad8(a_log), d_rep=jnp.repeat(d_ssd[0], HP)[None],
              ssd_norm_w=ssd_norm_w, s5_a_re=s5_a_re[0], s5_a_im=s5_a_im[0], s5_log_dt=s5_log_dt[0], s5_b_re=s5_b_re[0],
              s5_b_im=s5_b_im[0], s5_c_re=s5_c_re[0], s5_c_im=s5_c_im[0], s5_d=s5_d, b_glu=b_glu)

    lsum, dx0, dmod, gbig, small = _local_step(x.reshape(t, D), loss_target.reshape(t, D), mod, w, sp, seq,
                                               dist=(gather(("ffn1_w2", "w_in", "w_glu", "w_out")),
                                                     gather(("ffn2_w1", "ffn2_w3", "ffn2_w2")), (ci, chip)))
    loss = lax.psum(lsum * (0.5 / D), ("x", "y", "c"))

    snames = [n for n in SMALL if n != "b_ada"]
    sgrad = dict(small)
    sgrad["b_glu"] = small["w_glu_b"]
    dm_rows = nb * 9 * D // 128
    tail = _allgather8("gather_small", jnp.concatenate(
        [dmod.reshape(dm_rows, 128), _pack([sgrad[n] for n in snames], 1024).reshape(-1, 128)], axis=0))
    dmod_all = tail[:, :dm_rows].reshape(ndev * nb, 9 * D)
    dmod_cols = lax.dynamic_slice(dmod_all, (0, chip * ADA_COLS), (ndev * nb, ADA_COLS))
    g_w_ada, g_b_ada = _ada_bwd("ada_bwd", c_all, dmod_cols, dmod_all)

    gbig["w_ada"] = g_w_ada

    outs = {}
    for call, names in (("adamw_a", ("ffn1_w1", "ffn1_w3", "ffn1_w2", "w_in", "w_glu", "w_out")),
                        ("adamw_b", ("ffn2_w1", "ffn2_w3", "ffn2_w2", "w_ada"))):
        res = _adamw(call, [nat(n) for n in names], [gbig[n] for n in names], [nat("m_" + n) for n in names],
                     [nat("v_" + n) for n in names], 8)
        for kind, arrs in zip(("grad", "delta", "new_m", "new_v"), ([gbig[n] for n in names],) + tuple(res)):
            for n, arr in zip(names, arrs):
                outs[kind, n] = (arr.T if n in TRANSPOSED else arr)[None]

    ssum = _sum8("small_sum", tail)[dm_rows:].reshape(-1)

    def view2d(u):
        s = u.shape[1:]
        return u.reshape((1, s[0]) if len(s) == 1 else (-1, s[-1]))

    vshape = {n: view2d(a[n]).shape for n in SMALL}
    gsm = dict(zip(snames, _unpack(ssum, [vshape[n] if n != "conv_w" else (4, D) for n in snames])))
    gsm["conv_w"] = lax.dynamic_slice(gsm["conv_w"], (0, chip * 256), (4, 256))
    gsm["b_ada"] = g_b_ada
    res = _adamw("adamw_small", [view2d(a[n]) for n in SMALL], [gsm[n] for n in SMALL],
                 [view2d(a["m_" + n]) for n in SMALL], [view2d(a["v_" + n]) for n in SMALL], 1)
    for kind, arrs in zip(("grad", "delta", "new_m", "new_v"), ([gsm[n] for n in SMALL],) + tuple(res)):
        for n, arr in zip(SMALL, arrs):
            outs[kind, n] = arr.reshape(a[n].shape)

    res = [loss, dx0.reshape(nb, seq, D)]
    for kind in ("grad", "delta", "new_m", "new_v"):
        res += [outs[kind, n] for n in WEIGHTS]
    return tuple(res)
```

```python
import functools
import math

import jax
import jax.numpy as jnp
from jax import lax
from jax.experimental import pallas as pl
from jax.experimental.pallas import tpu as pltpu

F32 = jnp.float32
BF16 = jnp.bfloat16
MXU_DTYPE = jnp.bfloat16

D = 1024
FF = 2816
FB = 1408
NH, HP, NS, NG = 8, 64, 128, 2
CH = 128
SW = 512
S5G, S5P, S5H = 32, 64, 16
S5L = S5G * S5P
PW = 2176
ALPHA = 2.0 ** 0.25
LN_EPS = 1e-5
ADAM_LR, ADAM_B1, ADAM_B2, ADAM_EPS, ADAM_WD, ADAM_STEP = 0.001, 0.9, 0.999, 1e-08, 0.01, 10
VMEM_LIMIT = 56 * 1024 * 1024
TM_WIDE = 512
MESH_T = pl.DeviceIdType.MESH


def _pcall(body, **kw):
    return pl.pallas_call(body, **kw)


def _cparams(sem=None, **kw):
    return pltpu.CompilerParams(dimension_semantics=sem, vmem_limit_bytes=VMEM_LIMIT, **kw)


def _dot(a, b):
    return jnp.dot(a, b, preferred_element_type=F32)


def _dot_nt(a, b):
    return lax.dot_general(a, b, (((1,), (1,)), ((), ())), preferred_element_type=F32)


def _dot_hi(a, b):
    return jnp.dot(a, b, preferred_element_type=F32, precision=lax.Precision.HIGHEST)


def _mx(a):
    return a.astype(MXU_DTYPE)


def _sigmoid(x):
    return 1.0 / (1.0 + jnp.exp(-x))


def _iota(shape, axis):
    return lax.broadcasted_iota(jnp.int32, shape, axis)


def _rowcall(name, fn, n_rows, tm, tpe, *, tiled=(), halos=(), exs=(), res=(), out_tiled=(), out_acc=(),
             out_exacc=(), scratch=(), reverse=False, batch=None, carry=None, carry_mid=0.0):
    if batch:
        n_rows //= batch
    nt = n_rows // tm

    def blk(i):
        return (nt - 1 - i) if reverse else i

    in_specs, args = [], []
    for a in tiled:
        if batch:
            in_specs.append(pl.BlockSpec((batch, tm, a.shape[1]), lambda i: (0, blk(i), 0)))
            args.append(a.reshape(batch, n_rows, a.shape[1]))
            continue
        in_specs.append(pl.BlockSpec((tm, a.shape[1]), lambda i: (blk(i), 0)))
        args.append(a)
    for a, rows in halos:
        r = tm // rows
        if batch:
            in_specs.append(pl.BlockSpec((batch, rows, a.shape[1]), lambda i, r=r: (0, jnp.maximum(blk(i) * r - 1, 0), 0)))
            args.append(a.reshape(batch, n_rows, a.shape[1]))
            continue
        in_specs.append(pl.BlockSpec((rows, a.shape[1]), lambda i, r=r: (jnp.maximum(blk(i) * r - 1, 0), 0)))
        args.append(a)
    for a in exs:
        in_specs.append(pl.BlockSpec((1,) + a.shape[1:], lambda i: (blk(i) // tpe, 0, 0)))
        args.append(a)
    for a in res:
        nd = a.ndim
        in_specs.append(pl.BlockSpec(a.shape, lambda i, nd=nd: (0,) * nd, pipeline_mode=pl.Buffered(1)))
        args.append(a)
    any_spec = pl.BlockSpec(memory_space=pl.ANY)
    st_ins = carry.ins if carry else []
    st_outs = carry.out_shapes if carry else []
    st_sems = carry.sem_shapes() if carry else []
    base_in = len(args)
    in_specs += [any_spec] * len(st_ins)
    args += st_ins
    out_specs, out_shape = [], []
    for s in out_tiled:
        if batch:
            out_specs.append(pl.BlockSpec((batch, tm, s.shape[1]), lambda i: (0, blk(i), 0)))
            out_shape.append(_sds((batch, n_rows, s.shape[1]), s.dtype))
            continue
        out_specs.append(pl.BlockSpec((tm, s.shape[1]), lambda i: (blk(i), 0)))
        out_shape.append(s)
    for s in out_acc:
        nd = len(s.shape)
        out_specs.append(pl.BlockSpec(s.shape, lambda i, nd=nd: (0,) * nd))
        out_shape.append(s)
    for s in out_exacc:
        out_specs.append(pl.BlockSpec((1,) + s.shape[1:], lambda i: (blk(i) // tpe, 0, 0)))
        out_shape.append(s)
    base_out = len(out_shape)
    out_specs += [any_spec] * len(st_outs)
    out_shape += st_outs
    aliases = {base_in + k: base_out + v for k, v in carry.aliases.items()} if carry else {}
    n = [len(tiled), len(halos), len(exs), len(res), len(st_ins), len(out_tiled), len(out_acc), len(out_exacc),
         len(st_outs), len(scratch), len(st_sems)]

    def body(*refs):
        groups, k = [], 0
        for m in n:
            groups.append(refs[k:k + m])
            k += m
        i = pl.program_id(0)
        b = blk(i)

        class ctx:
            first = i == 0
            pos = b % tpe
            seq_first = (b % tpe) == 0
            seq_last = (b % tpe) == tpe - 1
            ex_enter = (i % tpe) == 0

        def work():
            for cond, refs_ in ((ctx.first, groups[6]), (ctx.ex_enter, groups[7])):
                if refs_:
                    @pl.when(cond)
                    def _():
                        for r in refs_:
                            r[...] = jnp.zeros(r.shape, r.dtype)
            fn(ctx, *groups[0:4], *groups[5:8], groups[9])

        if carry:
            carry.run((groups[4], groups[8], groups[10]), i == 0, i == min(nt - 1, int(carry_mid * nt)), i == nt - 1, work)
        else:
            work()

    outs = _pcall(body, name=name, grid=(nt,), in_specs=in_specs, out_specs=out_specs, out_shape=out_shape,
                  input_output_aliases=aliases, scratch_shapes=list(scratch) + st_sems,
                  compiler_params=_cparams(("arbitrary",)))(*args)
    if batch:
        outs = [o.reshape(batch * n_rows, o.shape[2]) if k < len(out_tiled) else o for k, o in enumerate(outs)]
    return (outs[:base_out], outs[base_out:]) if carry else outs


def _acc(ref, val):
    ref[...] += val


def _sds(shape, dtype=F32):
    return jax.ShapeDtypeStruct(shape, dtype)


def _ln_fwd(r, g, b):
    mu = jnp.mean(r, axis=-1, keepdims=True)
    rc = r - mu
    var = jnp.mean(rc * rc, axis=-1, keepdims=True)
    return rc * lax.rsqrt(var + LN_EPS) * g + b


def _ln_bwd(r, g, dy):
    mu = jnp.mean(r, axis=-1, keepdims=True)
    rc = r - mu
    var = jnp.mean(rc * rc, axis=-1, keepdims=True)
    rstd = lax.rsqrt(var + LN_EPS)
    xhat = rc * rstd
    dxh = dy * g
    dr = rstd * (dxh - jnp.mean(dxh, axis=-1, keepdims=True) - xhat * jnp.mean(dxh * xhat, axis=-1, keepdims=True))
    return dr, jnp.sum(dy * xhat, axis=0, keepdims=True), jnp.sum(dy, axis=0, keepdims=True)


def _ffn_fwd_hidden(name, x, mod, k0, w1, w3, seq, **carry):
    t = x.shape[0]
    tm = min(TM_WIDE, seq)

    def fn(ctx, tiled, halos, exs, res, outs, accs, exaccs, scr):
        mod_ref, = exs
        w1_ref, w3_ref = res
        h_ref, ab_ref, s_ref = outs
        sh, sc = mod_ref[0, k0:k0 + 1, :], mod_ref[0, k0 + 1:k0 + 2, :]
        h = _mx(tiled[0][...] * (1.0 + sc) + sh)
        h_ref[...] = h
        for j in range(2):
            a = _dot_nt(h, w1_ref[j * FB:(j + 1) * FB, :])
            b = _dot_nt(h, w3_ref[j * FB:(j + 1) * FB, :])
            ab_ref[:, j * FB:(j + 1) * FB] = _mx(a)
            ab_ref[:, FF + j * FB:FF + (j + 1) * FB] = _mx(b)
            s_ref[:, j * FB:(j + 1) * FB] = _mx(a * _sigmoid(a) * b)

    return _rowcall(name, fn, t, tm, seq // tm, tiled=[x], exs=[mod], res=[w1, w3],
                    out_tiled=[_sds((t, D), MXU_DTYPE), _sds((t, 2 * FF), MXU_DTYPE), _sds((t, FF), MXU_DTYPE)], **carry)


def _ffn_fwd_out(name, s, x, mod, k0, w2, lng, lnb, seq, tgt=None, **carry):
    t = x.shape[0]
    tm = min(TM_WIDE, seq)
    with_loss = tgt is not None

    def fn(ctx, tiled, halos, exs, res, outs, accs, exaccs, scr):
        mod_ref, = exs
        w2_ref, g_ref, b_ref = res
        xo_ref, r_ref, f_ref = outs[:3]
        f = _dot(tiled[0][...], w2_ref[...])
        f_ref[...] = f
        r = ALPHA * tiled[1][...] + 0.5 * mod_ref[0, k0 + 2:k0 + 3, :] * f
        r_ref[...] = r
        xo = _ln_fwd(r, g_ref[...], b_ref[...])
        xo_ref[...] = xo
        if with_loss:
            e = xo - tiled[2][...]
            outs[3][...] = e * (1.0 / D)
            _acc(accs[0], jnp.sum(e * e) * jnp.ones((8, 128), F32))

    return _rowcall(name, fn, t, tm, seq // tm, tiled=[s, x] + ([tgt] if with_loss else []), exs=[mod],
                    res=[w2, lng, lnb], out_tiled=[_sds((t, D))] * (4 if with_loss else 3),
                    out_acc=[_sds((8, 128))] if with_loss else [], **carry)


def _ffn_bwd1(name, dxo, r, ab, f, mod, k0, lng, w2, seq, **carry):
    t = dxo.shape[0]
    tm = 256

    def fn(ctx, tiled, halos, exs, res, outs, accs, exaccs, scr):
        dxo_ref, r_ref, ab_ref, f_ref = tiled
        mod_ref, = exs
        g_ref, w2_ref = res
        dr_ref, df_ref, dab_ref = outs
        g = mod_ref[0, k0 + 2:k0 + 3, :]
        dr, dgam, dbet = _ln_bwd(r_ref[...], g_ref[...], dxo_ref[...])
        dr_ref[...] = dr
        _acc(accs[0], dgam)
        _acc(accs[1], dbet)
        _acc(exaccs[0].at[0], jnp.sum(0.5 * f_ref[...] * dr, axis=0, keepdims=True))
        df = _mx(0.5 * g * dr)
        df_ref[...] = df
        for j in range(2):
            ds = _dot_nt(df, w2_ref[j * FB:(j + 1) * FB, :])
            a = ab_ref[:, j * FB:(j + 1) * FB].astype(F32)
            b = ab_ref[:, FF + j * FB:FF + (j + 1) * FB].astype(F32)
            sig = _sigmoid(a)
            dab_ref[:, j * FB:(j + 1) * FB] = _mx(ds * b * (sig * (1.0 + a * (1.0 - sig))))
            dab_ref[:, FF + j * FB:FF + (j + 1) * FB] = _mx(ds * (a * sig))

    b = mod.shape[0]
    return _rowcall(name, fn, t, tm, seq // tm, tiled=[dxo, r, ab, f], exs=[mod], res=[lng, w2],
                    out_tiled=[_sds((t, D)), _sds((t, D), MXU_DTYPE), _sds((t, 2 * FF), MXU_DTYPE)],
                    out_acc=[_sds((1, D)), _sds((1, D))], out_exacc=[_sds((b, 1, D))], **carry)


def _ffn_bwd(name, dxo, r, ab, f, x, mod, k0, lng, w2, w1t, w3t, seq):
    t = dxo.shape[0]
    tm = 256

    def fn(ctx, tiled, halos, exs, res, outs, accs, exaccs, scr):
        dxo_ref, r_ref, ab_ref, f_ref, x_ref = tiled
        mod_ref, = exs
        g_ref, w2_ref, w1t_ref, w3t_ref = res
        dx_ref, df_ref, dab_ref = outs
        sc, g = mod_ref[0, k0 + 1:k0 + 2, :], mod_ref[0, k0 + 2:k0 + 3, :]
        dr, dgam, dbet = _ln_bwd(r_ref[...], g_ref[...], dxo_ref[...])
        _acc(accs[0], dgam)
        _acc(accs[1], dbet)
        _acc(exaccs[0].at[0], jnp.sum(0.5 * f_ref[...] * dr, axis=0, keepdims=True))
        df = _mx(0.5 * g * dr)
        df_ref[...] = df
        dh = jnp.zeros((tm, D), F32)
        for j in range(2):
            blk = slice(j * FB, (j + 1) * FB)
            ds = _dot_nt(df, w2_ref[blk, :])
            a = ab_ref[:, blk].astype(F32)
            b = ab_ref[:, FF + j * FB:FF + (j + 1) * FB].astype(F32)
            sig = _sigmoid(a)
            da = _mx(ds * b * (sig * (1.0 + a * (1.0 - sig))))
            db = _mx(ds * (a * sig))
            dab_ref[:, blk] = da
            dab_ref[:, FF + j * FB:FF + (j + 1) * FB] = db
            dh = dh + _dot(da, w1t_ref[blk, :]) + _dot(db, w3t_ref[blk, :])
        dx_ref[...] = ALPHA * dr + dh * (1.0 + sc)
        _acc(exaccs[1].at[0], jnp.sum(dh, axis=0, keepdims=True))
        _acc(exaccs[2].at[0], jnp.sum(dh * x_ref[...], axis=0, keepdims=True))

    b = mod.shape[0]
    return _rowcall(name, fn, t, tm, seq // tm, tiled=[dxo, r, ab, f, x], exs=[mod], res=[lng, w2, w1t, w3t],
                    out_tiled=[_sds((t, D)), _sds((t, D), MXU_DTYPE), _sds((t, 2 * FF), MXU_DTYPE)],
                    out_acc=[_sds((1, D)), _sds((1, D))], out_exacc=[_sds((b, 1, D))] * 3)


def _mod_bwd(name, dab, dr, x, mod, k0, wts, seq, extra=(), nt=False, **carry):
    t = dr.shape[0]
    tm = min(TM_WIDE, seq)
    nin = 1 + len(extra)
    width = dab.shape[1] + sum(e.shape[1] for e in extra)

    def fn(ctx, tiled, halos, exs, res, outs, accs, exaccs, scr):
        parts = tiled[:nin]
        dr_ref, x_ref = tiled[nin:]
        mod_ref, = exs
        sc = mod_ref[0, k0 + 1:k0 + 2, :]
        if nin == 1:
            dp = parts[0][...]
        else:
            dp = jnp.concatenate([_mx(p[...]) for p in parts], axis=1)
            outs[1][...] = dp
        dh, off = jnp.zeros((tm, D), F32), 0
        for w_ref in res:
            k = w_ref.shape[1 if nt else 0]
            dh = dh + (_dot_nt if nt else _dot)(dp[:, off:off + k], w_ref[...])
            off += k
        outs[0][...] = ALPHA * dr_ref[...] + dh * (1.0 + sc)
        _acc(exaccs[0].at[0], jnp.sum(dh, axis=0, keepdims=True))
        _acc(exaccs[1].at[0], jnp.sum(dh * x_ref[...], axis=0, keepdims=True))

    b = mod.shape[0]
    out_tiled = [_sds((t, D))] + ([_sds((t, width), MXU_DTYPE)] if nin > 1 else [])
    return _rowcall(name, fn, t, tm, seq // tm, tiled=[dab, *extra, dr, x], exs=[mod], res=list(wts),
                    out_tiled=out_tiled, out_exacc=[_sds((b, 1, D)), _sds((b, 1, D))], **carry)


def _tn_matmul(name, a, b, bm, bn, bt=1024, carry=None, a_cols=None):
    t = a.shape[0]
    bt = min(bt, t)
    start, m = a_cols or (0, a.shape[1])
    off = start // bm
    n = b.shape[1]
    grid = (m // bm, n // bn, t // bt)
    n_in, n_out = (len(carry.ins), len(carry.out_shapes)) if carry else (0, 0)

    def body(a_ref, b_ref, *refs):
        o_ref = refs[n_in]

        def work():
            @pl.when(pl.program_id(2) == 0)
            def _():
                o_ref[...] = jnp.zeros(o_ref.shape, F32)
            o_ref[...] += lax.dot_general(a_ref[...], b_ref[...], (((0,), (0,)), ((), ())), preferred_element_type=F32)

        if not carry:
            return work()
        step = (pl.program_id(0) * grid[1] + pl.program_id(1)) * grid[2] + pl.program_id(2)
        carry.run((refs[:n_in], refs[n_in + 1:n_in + 1 + n_out], refs[n_in + 1 + n_out:]), step == 0, step == 0,
                  step == grid[0] * grid[1] * grid[2] - 1, work)

    any_spec = pl.BlockSpec(memory_space=pl.ANY)
    outs = _pcall(body, name=name, grid=grid,
                  in_specs=[pl.BlockSpec((bt, bm), lambda i, j, k: (k, i + off)), pl.BlockSpec((bt, bn), lambda i, j, k: (k, j))]
                  + [any_spec] * n_in,
                  out_specs=[pl.BlockSpec((bm, bn), lambda i, j, k: (i, j))] + [any_spec] * n_out,
                  out_shape=[_sds((m, n))] + (carry.out_shapes if carry else []),
                  input_output_aliases={2 + k: 1 + v for k, v in carry.aliases.items()} if carry else {},
                  scratch_shapes=carry.sem_shapes() if carry else [],
                  compiler_params=_cparams(("arbitrary",) * 3 if carry else ("parallel", "parallel", "arbitrary")))(
                      a, b, *(carry.ins if carry else []))
    return (outs[0], outs[1:]) if carry else outs[0]


S5_BLK = 16


def _tn_diag(name, a, b, bt=1024):
    t = a.shape[0]
    bt = min(bt, t)
    rows, cols = S5_BLK * S5P, S5_BLK * S5H
    nblk = a.shape[1] // rows

    def body(a_ref, b_ref, o_ref):
        @pl.when(pl.program_id(1) == 0)
        def _():
            o_ref[...] = jnp.zeros(o_ref.shape, F32)
        o_ref[0] += lax.dot_general(a_ref[...], b_ref[...], (((0,), (0,)), ((), ())), preferred_element_type=F32)

    return _pcall(body, name=name, grid=(nblk, t // bt),
                  in_specs=[pl.BlockSpec((bt, rows), lambda i, k: (k, i)),
                            pl.BlockSpec((bt, cols), lambda i, k: (k, i % (S5G // S5_BLK)))],
                  out_specs=pl.BlockSpec((1, rows, cols), lambda i, k: (i, 0, 0)), out_shape=_sds((nblk, rows, cols)),
                  compiler_params=_cparams(("parallel", "arbitrary")))(a, b)


def _diag_groups(o):
    o = o.reshape(2, S5G // S5_BLK, S5_BLK, S5P, S5_BLK, S5H)
    return jnp.einsum("rbgpgh->rbgph", o).reshape(2, S5G, S5P, S5H)


def _proj_fwd(name, x, mod, w_in, seq):
    t = x.shape[0]
    tm = min(TM_WIDE, seq)

    def fn(ctx, tiled, halos, exs, res, outs, accs, exaccs, scr):
        mod_ref, = exs
        sh, sc = mod_ref[0, 3:4, :], mod_ref[0, 4:5, :]
        h = _mx(tiled[0][...] * (1.0 + sc) + sh)
        outs[0][...] = h
        outs[1][...] = _dot(h, res[0][...])

    return _rowcall(name, fn, t, tm, seq // tm, tiled=[x], exs=[mod], res=[w_in],
                    out_tiled=[_sds((t, D), MXU_DTYPE), _sds((t, PW))])


def _shift_rows(cur, prev8, j):
    if j == 0:
        return cur
    rolled = pltpu.roll(cur, j, 0)
    top = jnp.where(_iota((8, cur.shape[1]), 0) < j, pltpu.roll(prev8, j, 0), rolled[0:8])
    return jnp.concatenate([top, rolled[8:]], axis=0)


def _shift_rows_up(cur, next8, j):
    if j == 0:
        return cur
    n = cur.shape[0]
    rolled = pltpu.roll(cur, n - j, 0)
    bot = jnp.where(_iota((8, cur.shape[1]), 0) >= 8 - j, pltpu.roll(next8, 8 - j, 0), rolled[n - 8:n])
    return jnp.concatenate([rolled[:n - 8], bot], axis=0)


def _softplus(x):
    return jnp.maximum(x, 0.0) + jnp.log(1.0 + jnp.exp(-jnp.abs(x)))


def _ssd_common(proj_ref, xpre, dtb_ref, alog_ref):
    xbc = xpre * _sigmoid(xpre)
    xs, bm, cm = xbc[:, 0:SW], xbc[:, SW:SW + 256], xbc[:, SW + 256:SW + 512]
    dtraw = proj_ref[:, PW - 128:PW] + dtb_ref[...]
    dt = _softplus(dtraw)
    a = -jnp.exp(alog_ref[...])
    tril = (_iota((CH, CH), 0) >= _iota((CH, CH), 1)).astype(F32)
    acs = _dot_hi(tril, dt * a)
    return xs, bm, cm, dtraw, dt, a, acs, acs.T


def _pair_lane(lo, hi):
    r = lo.shape[0]
    return jnp.where(_iota((r, 128), 1) < HP, lo, hi)


def _ssd_fwd(name, proj, conv_w, conv_b, dt_bias, a_log, d_rep, norm_w, seq):
    t = proj.shape[0]
    nb = t // seq

    def fn(ctx, tiled, halos, exs, res, outs, accs, exaccs, scr):
        @pl.when(ctx.seq_first)
        def _():
            scr[0][...] = jnp.zeros(scr[0].shape, F32)

        for b in range(nb):
            one(ctx, res, [r.at[b] for r in tiled + halos + outs + scr])

    def one(ctx, res, refs):
        proj_ref, halo_ref, yo_ref, xpre_ref, y_ref, sprev_ref, state_ref = refs
        cw_ref, cb_ref, dtb_ref, alog_ref, d_ref, nw_ref = res
        raw = proj_ref[:, SW:SW + D]
        prev8 = halo_ref[:, SW:SW + D] * jnp.where(ctx.seq_first, 0.0, 1.0)
        xpre = cb_ref[...] + cw_ref[3:4, :] * raw
        for j in (1, 2, 3):
            xpre = xpre + cw_ref[3 - j:4 - j, :] * _shift_rows(raw, prev8, j)
        xpre_ref[...] = xpre
        xs, bm, cm, dtraw, dt, a, acs, acst = _ssd_common(proj_ref, xpre, dtb_ref, alog_ref)
        causal = _iota((CH, CH), 0) >= _iota((CH, CH), 1)
        lane_lo = _iota((CH, 128), 1) < HP
        sprev = state_ref[...]
        sprev_ref[...] = sprev
        ys = []
        for g in range(NG):
            bmg, cmg = bm[:, g * NS:(g + 1) * NS], cm[:, g * NS:(g + 1) * NS]
            bmt = bmg.T
            cb = _dot(_mx(cmg), _mx(bmt))
            for q in (2 * g, 2 * g + 1):
                xsq = xs[:, 128 * q:128 * q + 128]
                xd = xsq * _pair_lane(dt[:, 2 * q:2 * q + 1], dt[:, 2 * q + 1:2 * q + 2])
                sp = sprev[:, 128 * q:128 * q + 128]
                ydiag = jnp.zeros((CH, 128), F32)
                snew = jnp.zeros((NS, 128), F32)
                for jj in range(2):
                    h = 2 * q + jj
                    col, row = acs[:, h:h + 1], acst[h:h + 1, :]
                    lm = jnp.where(causal, jnp.exp(jnp.minimum(col - row, 0.0)), 0.0)
                    xm = _mx(jnp.where(lane_lo if jj == 0 else ~lane_lo, xd, 0.0))
                    ydiag = ydiag + _dot(_mx(cb * lm), xm)
                    dec_row = jnp.exp(acst[h:h + 1, CH - 1:CH] - row)
                    snew = snew + _dot(_mx(bmt * dec_row), xm)
                e_pair = jnp.exp(_pair_lane(acs[:, 2 * q:2 * q + 1], acs[:, 2 * q + 1:2 * q + 2]))
                yoff = _dot(_mx(cmg), _mx(sp)) * e_pair
                cd = jnp.exp(_pair_lane(acst[2 * q:2 * q + 1, CH - 1:CH], acst[2 * q + 1:2 * q + 2, CH - 1:CH]))
                state_ref[:, 128 * q:128 * q + 128] = cd * sp + snew
                ys.append(ydiag + yoff + d_ref[:, 128 * q:128 * q + 128] * xsq)
        y = jnp.concatenate(ys, axis=1)
        y_ref[...] = y
        z = proj_ref[:, 0:SW]
        yz = y * (z * _sigmoid(z))
        outp = []
        for g in range(NG):
            seg = yz[:, 256 * g:256 * g + 256]
            rinv = lax.rsqrt(jnp.mean(seg * seg, axis=-1, keepdims=True) + LN_EPS)
            outp.append(seg * rinv * nw_ref[:, 256 * g:256 * g + 256])
        yo_ref[...] = _mx(jnp.concatenate(outp, axis=1))

    return _rowcall(name, fn, t, CH, seq // CH, tiled=[proj], halos=[(proj, 8)],
                    res=[conv_w, conv_b, dt_bias, a_log, d_rep, norm_w],
                    out_tiled=[_sds((t, SW), MXU_DTYPE), _sds((t, D)), _sds((t, SW)), _sds((t, SW))],
                    scratch=[pltpu.VMEM((nb, NS, SW), F32)], batch=nb)


def _ssd_bwd(name, dyo, proj, xpre_all, y_all, sprev_all, conv_w, dt_bias, a_log, d_rep, norm_w, seq, **carry):
    t = proj.shape[0]
    nb = t // seq

    def fn(ctx, tiled, halos, exs, res, outs, accs, exaccs, scr):
        @pl.when(ctx.seq_last)
        def _():
            for r in scr:
                r[...] = jnp.zeros(r.shape, F32)

        for b in range(nb):
            one(ctx, res, accs, [r.at[b] for r in tiled + halos + outs + scr])

    def one(ctx, res, accs, refs):
        dyo_ref, proj_ref, xpre_ref, y_ref, sprev_ref, halo_ref, dzx_ref, ddt_ref, ds_ref, nxt_ref = refs
        cw_ref, dtb_ref, alog_ref, d_ref, nw_ref = res
        dnw_acc, dd_acc, dcw_acc, dcb_acc, ddtb_acc, dalog_acc = accs
        xpre = xpre_ref[...]
        xs, bm, cm, dtraw, dt, a, acs, acst = _ssd_common(proj_ref, xpre, dtb_ref, alog_ref)
        y = y_ref[...]
        z = proj_ref[:, 0:SW]
        sz = _sigmoid(z)
        siluz = z * sz
        yz = y * siluz
        dyo = dyo_ref[...]
        dyz_parts, dnw_parts = [], []
        for g in range(NG):
            seg = yz[:, 256 * g:256 * g + 256]
            rinv = lax.rsqrt(jnp.mean(seg * seg, axis=-1, keepdims=True) + LN_EPS)
            yn = seg * rinv
            dseg = dyo[:, 256 * g:256 * g + 256]
            dnw_parts.append(jnp.sum(dseg * yn, axis=0, keepdims=True))
            dyn = dseg * nw_ref[:, 256 * g:256 * g + 256]
            dyz_parts.append(rinv * (dyn - yn * jnp.mean(dyn * yn, axis=-1, keepdims=True)))
        dyz = jnp.concatenate(dyz_parts, axis=1)
        _acc(dnw_acc, jnp.concatenate(dnw_parts, axis=1))
        dy = dyz * siluz
        dz = dyz * y * (sz * (1.0 + z * (1.0 - sz)))
        _acc(dd_acc, jnp.sum(dy * xs, axis=0, keepdims=True))

        causal = _iota((CH, CH), 0) >= _iota((CH, CH), 1)
        anti = _iota((CH, CH), 0) <= _iota((CH, CH), 1)
        lane_lo = _iota((CH, 128), 1) < HP
        lane_id = _iota((CH, 128), 1)
        last_row = _iota((CH, 128), 0) == CH - 1
        sprev = sprev_ref[...]
        dacs = jnp.zeros((CH, 128), F32)
        ddt_x = jnp.zeros((CH, 128), F32)
        dxs_parts, dbm_parts, dcm_parts = [], [], []
        for g in range(NG):
            bmg, cmg = bm[:, g * NS:(g + 1) * NS], cm[:, g * NS:(g + 1) * NS]
            bmt, cmt = bmg.T, cmg.T
            cb = _dot(_mx(cmg), _mx(bmt))
            cbt = _dot(_mx(bmg), _mx(cmt))
            dcb = jnp.zeros((CH, CH), F32)
            dcbt = jnp.zeros((CH, CH), F32)
            dbmg = jnp.zeros((CH, NS), F32)
            dcmg = jnp.zeros((CH, NS), F32)
            for q in (2 * g, 2 * g + 1):
                sl = slice(128 * q, 128 * q + 128)
                xsq = xs[:, sl]
                dtp = _pair_lane(dt[:, 2 * q:2 * q + 1], dt[:, 2 * q + 1:2 * q + 2])
                xd = xsq * dtp
                dyq = dy[:, sl]
                sp = sprev[:, sl]
                dsn = ds_ref[:, sl]
                e_pair = jnp.exp(_pair_lane(acs[:, 2 * q:2 * q + 1], acs[:, 2 * q + 1:2 * q + 2]))
                cd = jnp.exp(_pair_lane(acst[2 * q:2 * q + 1, CH - 1:CH], acst[2 * q + 1:2 * q + 2, CH - 1:CH]))
                dye = dyq * e_pair
                dcmg = dcmg + _dot(_mx(dye), _mx(sp.T))
                dsp = _dot(_mx(cmt), _mx(dye)) + cd * dsn
                yoff = _dot(_mx(cmg), _mx(sp)) * e_pair
                dacs_lane = dyq * yoff
                dxd = jnp.zeros((CH, 128), F32)
                sds = jnp.sum(dsn * sp, axis=0, keepdims=True) * cd
                for jj in range(2):
                    h = 2 * q + jj
                    hm = lane_lo if jj == 0 else ~lane_lo
                    col, row = acs[:, h:h + 1], acst[h:h + 1, :]
                    lm = jnp.where(causal, jnp.exp(jnp.minimum(col - row, 0.0)), 0.0)
                    lmt = jnp.where(anti, jnp.exp(jnp.minimum(row - col, 0.0)), 0.0)
                    xm = _mx(jnp.where(hm, xd, 0.0))
                    dym = _mx(jnp.where(hm, dyq, 0.0))
                    gm = _dot_nt(dym, xm)
                    gmt = _dot_nt(xm, dym)
                    dcb = dcb + gm * lm
                    dcbt = dcbt + gmt * lmt
                    dxd = dxd + _dot(_mx(cbt * lmt), dym)
                    w = gm * cb * lm
                    wt = gmt * cbt * lmt
                    dacs_h = jnp.sum(w, axis=1, keepdims=True) - jnp.sum(wt, axis=1, keepdims=True)
                    alast = acst[h:h + 1, CH - 1:CH]
                    dec_col = jnp.exp(alast - col)
                    dsm = _mx(jnp.where(hm[0:NS], dsn, 0.0))
                    qh = _dot_nt(xm, dsm)
                    dbmg = dbmg + qh * dec_col
                    ddec = jnp.sum(qh * bmg, axis=1, keepdims=True)
                    dxd = dxd + _dot(_mx(bmg * dec_col), dsm)
                    dacs_h = dacs_h - ddec * dec_col
                    dacs_h = dacs_h + jnp.sum(jnp.where(hm, dacs_lane, 0.0), axis=1, keepdims=True)
                    tail = jnp.sum(ddec * dec_col, axis=0, keepdims=True) + jnp.sum(
                        jnp.where(hm[0:1], sds, 0.0), axis=1, keepdims=True)
                    dacs = dacs + jnp.where(lane_id == h, dacs_h, 0.0) + jnp.where(
                        last_row & (lane_id == h), tail, 0.0)
                ds_ref[:, sl] = dsp
                for jj in range(2):
                    h = 2 * q + jj
                    hm = lane_lo if jj == 0 else ~lane_lo
                    ddt_x = ddt_x + jnp.where(lane_id == h, jnp.sum(jnp.where(hm, dxd * xsq, 0.0), axis=1, keepdims=True), 0.0)
                dxs_parts.append(dxd * dtp + d_ref[:, sl] * dyq)
            dcmg = dcmg + _dot(_mx(dcb), _mx(bmg))
            dbmg = dbmg + _dot(_mx(dcbt), _mx(cmg))
            dbm_parts.append(dbmg)
            dcm_parts.append(dcmg)
        triu = (_iota((CH, CH), 0) <= _iota((CH, CH), 1)).astype(F32)
        dadt = _dot_hi(triu, dacs)
        ddt = dadt * a + ddt_x
        _acc(dalog_acc, jnp.sum(dadt * dt, axis=0, keepdims=True) * a)
        ddtraw = ddt * _sigmoid(dtraw)
        ddt_ref[...] = ddtraw
        _acc(ddtb_acc, jnp.sum(ddtraw, axis=0, keepdims=True))
        dxbc = jnp.concatenate(dxs_parts + dbm_parts + dcm_parts, axis=1)
        sx = _sigmoid(xpre)
        dpre = dxbc * (sx * (1.0 + xpre * (1.0 - sx)))
        _acc(dcb_acc, jnp.sum(dpre, axis=0, keepdims=True))
        raw = proj_ref[:, SW:SW + D]
        prev8 = halo_ref[:, SW:SW + D] * jnp.where(ctx.seq_first, 0.0, 1.0)
        next8 = nxt_ref[...]
        draw = cw_ref[3:4, :] * dpre
        dcw = [None] * 4
        dcw[3] = jnp.sum(dpre * raw, axis=0, keepdims=True)
        for j in (1, 2, 3):
            dcw[3 - j] = jnp.sum(dpre * _shift_rows(raw, prev8, j), axis=0, keepdims=True)
            draw = draw + cw_ref[3 - j:4 - j, :] * _shift_rows_up(dpre, next8, j)
        _acc(dcw_acc, jnp.concatenate(dcw + [jnp.zeros((4, D), F32)], axis=0))
        nxt_ref[...] = dpre[0:8]
        dzx_ref[:, 0:SW] = dz
        dzx_ref[:, SW:SW + D] = draw

    return _rowcall(name, fn, t, CH, seq // CH, tiled=[dyo, proj, xpre_all, y_all, sprev_all], halos=[(proj, 8)],
                    res=[conv_w, dt_bias, a_log, d_rep, norm_w],
                    out_tiled=[_sds((t, SW + D)), _sds((t, 128))],
                    out_acc=[_sds((1, SW)), _sds((1, SW)), _sds((8, D)), _sds((1, D)), _sds((1, 128)), _sds((1, 128))],
                    scratch=[pltpu.VMEM((nb, NS, SW), F32), pltpu.VMEM((nb, 8, D), F32)], reverse=True, batch=nb, **carry)


def _gelu(y):
    k = math.sqrt(2.0 / math.pi)
    return 0.5 * y * (1.0 + jnp.tanh(k * (y + 0.044715 * y * y * y)))


def _gelu_grad(y):
    k = math.sqrt(2.0 / math.pi)
    th = jnp.tanh(k * (y + 0.044715 * y * y * y))
    return 0.5 * (1.0 + th) + 0.5 * y * (1.0 - th * th) * k * (1.0 + 3.0 * 0.044715 * y * y)


S5T = 256


def _cmul_add(xr, xi, ar, ai, sr, si):
    return xr + ar * sr - ai * si, xi + ar * si + ai * sr


def _s5_fwd(name, proj, bbd, cbd, pw, tab, d5, w_glu, b_glu, seq, **carry):
    t = proj.shape[0]
    tm = S5T

    def fn(ctx, tiled, halos, exs, res, outs, accs, exaccs, scr):
        proj_ref, = tiled
        bbd_ref, cbd_ref, pw_ref, tab_ref, d_ref, wg_ref, bg_ref = res
        out_ref, xst_ref, y_ref, xb_ref, ub_ref = outs
        carry_ref, = scr

        @pl.when(ctx.seq_first)
        def _():
            carry_ref[...] = jnp.zeros(carry_ref.shape, F32)

        u = proj_ref[:, 1536:2048]
        bu = _dot(_mx(u), bbd_ref[...])
        xr, xi = bu[:, :S5L].reshape(tm // 8, 8, S5L), bu[:, S5L:].reshape(tm // 8, 8, S5L)
        for k, sh in enumerate((1, 2, 4)):
            xr, xi = _cmul_add(xr, xi, pw_ref[k, :, :S5L], pw_ref[k, :, S5L:], pltpu.roll(xr, sh, 1), pltpu.roll(xi, sh, 1))
        cr, ci = carry_ref[0:1, :S5L], carry_ref[0:1, S5L:]
        for i in range(tm // 8):
            tr, ti = _cmul_add(xr[i], xi[i], tab_ref[:, :S5L], tab_ref[:, S5L:], cr, ci)
            xst_ref[8 * i:8 * i + 8, :S5L] = tr
            xst_ref[8 * i:8 * i + 8, S5L:] = ti
            cr, ci = tr[7:8], ti[7:8]
        carry_ref[0:1, :S5L] = cr
        carry_ref[0:1, S5L:] = ci
        xb = _mx(xst_ref[...])
        xb_ref[...] = xb
        ub_ref[...] = _mx(u)
        y = _dot(xb, cbd_ref[...]) + u * d_ref[...]
        y_ref[...] = y
        g = _gelu(y)
        v = _dot(_mx(g), wg_ref[...]) + bg_ref[...]
        out_ref[...] = _mx(g * _sigmoid(v))

    return _rowcall(name, fn, t, tm, seq // tm, tiled=[proj], res=[bbd, cbd, pw, tab, d5, w_glu, b_glu],
                    out_tiled=[_sds((t, SW), MXU_DTYPE), _sds((t, 2 * S5L)), _sds((t, SW)),
                               _sds((t, 2 * S5L), MXU_DTYPE), _sds((t, SW), MXU_DTYPE)],
                    scratch=[pltpu.VMEM((8, 2 * S5L), F32)], **carry)


def _s5_bwd(name, dout, proj, xst, y_all, bbdt, cbdt, pwc, tabc, d5, w_glu, b_glu, seq, **carry):
    t = proj.shape[0]
    tm = S5T

    def fn(ctx, tiled, halos, exs, res, outs, accs, exaccs, scr):
        dout_ref, proj_ref, xst_ref, y_ref = tiled
        halo_ref, = halos
        bbdt_ref, cbdt_ref, pw_ref, tab_ref, d_ref, wg_ref, bg_ref = res
        du_ref, lam_ref, dyb_ref, gb_ref, dvb_ref = outs
        da_acc, dd_acc, dbg_acc = accs
        carry_ref, lamf_ref = scr

        @pl.when(ctx.seq_last)
        def _():
            carry_ref[...] = jnp.zeros(carry_ref.shape, F32)

        u = proj_ref[:, 1536:2048]
        y = y_ref[...]
        g = _gelu(y)
        v = _dot(_mx(g), wg_ref[...]) + bg_ref[...]
        sg = _sigmoid(v)
        dout = dout_ref[...]
        dv = dout * g * sg * (1.0 - sg)
        dvb = _mx(dv)
        dvb_ref[...] = dvb
        gb_ref[...] = _mx(g)
        _acc(dbg_acc, jnp.sum(dv, axis=0, keepdims=True))
        dg = dout * sg + _dot_nt(dvb, wg_ref[...])
        dy = dg * _gelu_grad(y)
        dyb = _mx(dy)
        dyb_ref[...] = dyb
        _acc(dd_acc, jnp.sum(dy * u, axis=0, keepdims=True))
        dx = _dot(dyb, cbdt_ref[...])
        xr, xi = dx[:, :S5L].reshape(tm // 8, 8, S5L), dx[:, S5L:].reshape(tm // 8, 8, S5L)
        for k, sh in enumerate((1, 2, 4)):
            xr, xi = _cmul_add(xr, xi, pw_ref[k, :, :S5L], pw_ref[k, :, S5L:], pltpu.roll(xr, 8 - sh, 1),
                               pltpu.roll(xi, 8 - sh, 1))
        cr, ci = carry_ref[0:1, :S5L], carry_ref[0:1, S5L:]
        for i in reversed(range(tm // 8)):
            tr, ti = _cmul_add(xr[i], xi[i], tab_ref[:, :S5L], tab_ref[:, S5L:], cr, ci)
            lamf_ref[8 * i:8 * i + 8, :S5L] = tr
            lamf_ref[8 * i:8 * i + 8, S5L:] = ti
            cr, ci = tr[0:1], ti[0:1]
        carry_ref[0:1, :S5L] = cr
        carry_ref[0:1, S5L:] = ci
        lam = lamf_ref[...]
        lamb = _mx(lam)
        lam_ref[...] = lamb
        du_ref[...] = dy * d_ref[...] + _dot(lamb, bbdt_ref[...])
        prev8 = halo_ref[...] * jnp.where(ctx.seq_first, 0.0, 1.0)
        xprev = _shift_rows(xst_ref[...], prev8, 1)
        lr, li = lam[:, :S5L], lam[:, S5L:]
        pr, pi = xprev[:, :S5L], xprev[:, S5L:]
        dar = jnp.sum(lr * pr + li * pi, axis=0, keepdims=True)
        dai = jnp.sum(li * pr - lr * pi, axis=0, keepdims=True)
        _acc(da_acc, jnp.concatenate([dar, dai], axis=1))

    return _rowcall(name, fn, t, tm, seq // tm, tiled=[dout, proj, xst, y_all], halos=[(xst, 8)],
                    res=[bbdt, cbdt, pwc, tabc, d5, w_glu, b_glu],
                    out_tiled=[_sds((t, SW)), _sds((t, 2 * S5L), MXU_DTYPE), _sds((t, SW), MXU_DTYPE),
                               _sds((t, SW), MXU_DTYPE), _sds((t, SW), MXU_DTYPE)],
                    out_acc=[_sds((1, 2 * S5L)), _sds((1, SW)), _sds((1, SW))],
                    scratch=[pltpu.VMEM((8, 2 * S5L), F32), pltpu.VMEM((tm, 2 * S5L), F32)], reverse=True, **carry)


def _out_fwd(name, yssd, ys5, x1, mod, w_out, lng, lnb, seq):
    t = x1.shape[0]
    tm = min(TM_WIDE, seq)

    def fn(ctx, tiled, halos, exs, res, outs, accs, exaccs, scr):
        ya_ref, yb_ref, x_ref = tiled
        mod_ref, = exs
        w_ref, g_ref, b_ref = res
        m = _dot(ya_ref[...], w_ref[0:SW, :]) + _dot(yb_ref[...], w_ref[SW:2 * SW, :])
        r = ALPHA * x_ref[...] + mod_ref[0, 5:6, :] * m
        outs[0][...] = _ln_fwd(r, g_ref[...], b_ref[...])
        outs[1][...] = r
        outs[2][...] = m

    return _rowcall(name, fn, t, tm, seq // tm, tiled=[yssd, ys5, x1], exs=[mod], res=[w_out, lng, lnb],
                    out_tiled=[_sds((t, D)), _sds((t, D)), _sds((t, D))])


def _out_bwd(name, dxo, r, m, mod, lng, w_out, seq, **carry):
    t = dxo.shape[0]
    tm = min(TM_WIDE, seq)

    def fn(ctx, tiled, halos, exs, res, outs, accs, exaccs, scr):
        dxo_ref, r_ref, m_ref = tiled
        mod_ref, = exs
        g_ref, w_ref = res
        dr, dgam, dbet = _ln_bwd(r_ref[...], g_ref[...], dxo_ref[...])
        outs[0][...] = dr
        _acc(accs[0], dgam)
        _acc(accs[1], dbet)
        _acc(exaccs[0].at[0], jnp.sum(dr * m_ref[...], axis=0, keepdims=True))
        dm = _mx(mod_ref[0, 5:6, :] * dr)
        outs[1][...] = dm
        dyc = _dot_nt(dm, w_ref[...])
        outs[2][...] = dyc[:, 0:SW]
        outs[3][...] = dyc[:, SW:2 * SW]

    b = mod.shape[0]
    return _rowcall(name, fn, t, tm, seq // tm, tiled=[dxo, r, m], exs=[mod], res=[lng, w_out],
                    out_tiled=[_sds((t, D)), _sds((t, D), MXU_DTYPE), _sds((t, SW)), _sds((t, SW))],
                    out_acc=[_sds((1, D)), _sds((1, D))], out_exacc=[_sds((b, 1, D))], **carry)


def _s5_discretise(a_re, a_im, log_dt, b_re, b_im):
    dt = jnp.exp(log_dt)[:, None]
    mag = jnp.exp(dt * a_re)
    ab_re, ab_im = mag * jnp.cos(dt * a_im), mag * jnp.sin(dt * a_im)
    den = a_re * a_re + a_im * a_im
    nr, ni = ab_re - 1.0, ab_im
    f_re, f_im = (nr * a_re + ni * a_im) / den, (ni * a_re - nr * a_im) / den
    bb_re = f_re[..., None] * b_re - f_im[..., None] * b_im
    bb_im = f_re[..., None] * b_im + f_im[..., None] * b_re
    return ab_re, ab_im, bb_re, bb_im


def _s5_tables(ab_re, ab_im):
    ar, ai = ab_re.reshape(1, S5L), ab_im.reshape(1, S5L)
    pows = [(ar, ai)]
    for _ in range(7):
        pr, pi = pows[-1]
        pows.append((pr * ar - pi * ai, pr * ai + pi * ar))

    def pack(rows, sign):
        return jnp.concatenate([jnp.concatenate([r for r, _ in rows], axis=0),
                                jnp.concatenate([sign * i for _, i in rows], axis=0)], axis=1)

    row = jnp.arange(8)[:, None]
    pw = jnp.stack([jnp.where(row >= sh, pack([pows[sh - 1]], 1.0), 0.0) for sh in (1, 2, 4)])
    pwc = jnp.stack([jnp.where(row < 8 - sh, pack([pows[sh - 1]], -1.0), 0.0) for sh in (1, 2, 4)])
    tab = pack(pows, 1.0)
    tabc = pack(pows[::-1], -1.0)
    return pw, tab, pwc, tabc


class _GradGroup:
    def __init__(self, tag, grads, place):
        self.tag, self.names, (self.half, self.chip) = tag, list(grads), place
        self.gsh = [_shard_major(g) if n in COL_SHARDED else g.reshape(4, g.shape[0] // 4, g.shape[1])
                    for n, g in grads.items()]

    def sibling(self):
        return _rs_sibling_stage(self.gsh)

    def chips(self, received):
        self.from_sibling = received
        return _rs_chips_stage(_rs_add(self.tag + "_add", self.gsh, received, self.half))

    def join(self, received):
        return _rs_join_stage(_rs_sum(self.tag + "_sum", self.gsh, self.from_sibling, received, self.chip, self.half))

    def result(self, joined):
        return dict(zip(self.names, joined))


def _hid(fn, *args, stage=None, **kw):
    if stage is None:
        return fn(*args, **kw), None
    return fn(*args, carry=stage, **kw)


def _local_step(x, tgt, mod, w, sp, seq, dist=None):
    t = x.shape[0]
    mxu = MXU_DTYPE
    big = {}

    def group(tag, grads):
        if dist is None:
            big.update(grads)
            return None
        return _GradGroup(tag, grads, dist[2])

    mid = {"carry_mid": 0.6} if dist else {}
    (h1, ab1, s1), got = _hid(_ffn_fwd_hidden, "ffn1_fwd_a", x, mod, 0, w["ffn1_w1t"], w["ffn1_w3t"], seq,
                              stage=dist and dist[0][0], **mid)
    if dist:
        w = {**w, **dist[0][1](got)}
    x1, r1, f1 = _ffn_fwd_out("ffn1_fwd_b", s1, x, mod, 0, w["ffn1_w2"], sp["ln1_g"], sp["ln1_b"], seq)
    h2, proj = _proj_fwd("proj_fwd", x1, mod, w["w_in"], seq)
    yssd, xpre, yraw, sprev = _ssd_fwd("ssd_fwd", proj, sp["conv_w"], sp["conv_b"], sp["dt_bias"], sp["a_log"],
                                       sp["d_rep"], sp["ssd_norm_w"], seq)
    (ab_re, ab_im, bb_re, bb_im), disc_vjp = jax.vjp(_s5_discretise, sp["s5_a_re"], sp["s5_a_im"], sp["s5_log_dt"],
                                                     sp["s5_b_re"], sp["s5_b_im"])
    eye = jnp.eye(S5G, dtype=F32)
    bbd = jnp.concatenate([jnp.einsum("gk,gph->ghkp", eye, bb_re).reshape(SW, S5L),
                           jnp.einsum("gk,gph->ghkp", eye, bb_im).reshape(SW, S5L)], axis=1).astype(mxu)
    cbd = jnp.concatenate([jnp.einsum("gk,ghp->gpkh", eye, sp["s5_c_re"]).reshape(S5L, SW),
                           -jnp.einsum("gk,ghp->gpkh", eye, sp["s5_c_im"]).reshape(S5L, SW)], axis=0).astype(mxu)
    bbdt = jnp.concatenate([jnp.einsum("gk,gph->kpgh", eye, bb_re).reshape(S5L, SW),
                            jnp.einsum("gk,gph->kpgh", eye, bb_im).reshape(S5L, SW)], axis=0).astype(mxu)
    cbdt = jnp.concatenate([jnp.einsum("gk,ghp->khgp", eye, sp["s5_c_re"]).reshape(SW, S5L),
                            -jnp.einsum("gk,ghp->khgp", eye, sp["s5_c_im"]).reshape(SW, S5L)], axis=1).astype(mxu)
    pw, tab, pwc, tabc = _s5_tables(lax.stop_gradient(ab_re), lax.stop_gradient(ab_im))
    (ys5, xst, y5, xstb, ub), got = _hid(_s5_fwd, "s5_fwd", proj, bbd, cbd, pw, tab, sp["s5_d"], w["w_glu"], sp["b_glu"],
                                         seq, stage=dist and dist[1][0], **mid)
    if dist:
        w = {**w, **dist[1][1](got)}
    x2, r2, m2 = _out_fwd("out_fwd", yssd, ys5, x1, mod, w["w_out"], sp["ln2_g"], sp["ln2_b"], seq)
    h3, ab3, s3 = _ffn_fwd_hidden("ffn2_fwd_a", x2, mod, 6, w["ffn2_w1t"], w["ffn2_w3t"], seq)
    x3, r3, f3, dy3, loss_acc = _ffn_fwd_out("ffn2_fwd_b", s3, x2, mod, 6, w["ffn2_w2"], sp["ln3_g"], sp["ln3_b"], seq, tgt=tgt)
    dx2, df3, dab3, dg3g, dg3b, dgate3, dsh3, dsc3 = _ffn_bwd("ffn2_bwd", dy3, r3, ab3, f3, x2, mod, 6, sp["ln3_g"],
                                                              w["ffn2_w2"], w["ffn2_w1t"], w["ffn2_w3t"], seq)
    grp = group("rs_ffn2", dict(ffn2_w1=_tn_matmul("ffn2_dw1", dab3, h3, FB, D, a_cols=(0, FF)),
                                ffn2_w3=_tn_matmul("ffn2_dw3", dab3, h3, FB, D, a_cols=(FF, FF)),
                                ffn2_w2=_tn_matmul("ffn2_dw2", s3, df3, FB, D)))
    (dr2, dm2, dyssd, dys5, dg2g, dg2b, dgate2), got = _hid(_out_bwd, "out_bwd", dx2, r2, m2, mod, sp["ln2_g"],
                                                          w["w_out"], seq, stage=grp and grp.sibling())
    g_w_out = jnp.concatenate([_tn_matmul("dw_out_a", yssd, dm2, SW, D), _tn_matmul("dw_out_b", ys5, dm2, SW, D)], axis=0)
    (du, lam, dy5b, g5b, dv5b, da5, dd5, dbglu), got = _hid(
        _s5_bwd, "s5_bwd", dys5, proj, xst, y5, bbdt, cbdt, pwc, tabc, sp["s5_d"], w["w_glu"], sp["b_glu"],
        seq, stage=grp and grp.chips(got))
    g_w_glu = _tn_matmul("dw_glu", g5b, dv5b, SW, SW)
    dbb = _diag_groups(_tn_diag("s5_db", lam, ub))
    dcc = _diag_groups(_tn_diag("s5_dc", xstb, dy5b))
    (dzx, ddt, dnw, ddl, dcw, dcb, ddtb, dalog), got = _hid(
        _ssd_bwd, "ssd_bwd", dyssd, proj, xpre, yraw, sprev, sp["conv_w"], sp["dt_bias"], sp["a_log"], sp["d_rep"],
        sp["ssd_norm_w"], seq, stage=grp and grp.join(got))
    if grp:
        big.update(grp.result(got))
    dx1, dproj, dsh2, dsc2 = _mod_bwd("proj_bwd", dzx, dr2, x1, mod, 3, [w["w_in"]], seq, extra=(du, ddt), nt=True)
    gwi = _tn_matmul("dw_in", h2, dproj, D, PW)
    grp = group("rs_mix", dict(w_in=jnp.concatenate([gwi[:, :1536], gwi[:, 2048:2056], gwi[:, 1536:2048]], axis=1),
                               w_glu=g_w_glu, w_out=g_w_out))
    (dr1, df1, dab1, dg1g, dg1b, dgate1), got = _hid(_ffn_bwd1, "ffn1_bwd1", dx1, r1, ab1, f1, mod, 0, sp["ln1_g"],
                                                          w["ffn1_w2"], seq, stage=grp and grp.sibling())
    g1, got = _hid(_tn_matmul, "ffn1_dw1", dab1, h1, FB, D, a_cols=(0, FF), stage=grp and grp.chips(got))
    g3, got = _hid(_tn_matmul, "ffn1_dw3", dab1, h1, FB, D, a_cols=(FF, FF), stage=grp and grp.join(got))
    if grp:
        big.update(grp.result(got))
    grp = group("rs_ffn1a", dict(ffn1_w1=g1, ffn1_w3=g3))
    g2, got = _hid(_tn_matmul, "ffn1_dw2", s1, df1, FB, D, stage=grp and grp.sibling())
    grp2 = group("rs_ffn1b", dict(ffn1_w2=g2))
    (dx0, dsh1, dsc1), got = _hid(_mod_bwd, "ffn1_bwd2", dab1, dr1, x, mod, 0, [w["ffn1_w1t"], w["ffn1_w3t"]], seq,
                                  stage=grp and _merge_stages(grp.chips(got), grp2.sibling()))
    if grp:
        n = len(grp.names)
        got = _run_stage("rs_ffn1_tail", _merge_stages(grp.join(got[:n]), grp2.chips(got[n:])))
        big.update(grp.result(got[:n]))
        big.update(grp2.result(_run_stage("rs_ffn1b_join", grp2.join(got[n:]))))
    dmod = jnp.concatenate([dsh1, dsc1, dgate1, dsh2, dsc2, dgate2, dsh3, dsc3, dgate3], axis=1)
    dc_re, dc_im = dcc[0].transpose(0, 2, 1), -dcc[1].transpose(0, 2, 1)
    g_a_re, g_a_im, g_log_dt, g_b_re, g_b_im = disc_vjp(
        (da5[:, :S5L].reshape(S5G, S5P), da5[:, S5L:].reshape(S5G, S5P), dbb[0], dbb[1]))
    small = dict(ln1_g=dg1g, ln1_b=dg1b, ln2_g=dg2g, ln2_b=dg2b, ln3_g=dg3g, ln3_b=dg3b, conv_w=dcw[0:4], conv_b=dcb,
                 dt_bias=ddtb[:, :NH], a_log=dalog[:, :NH], d_ssd=jnp.sum(ddl.reshape(NH, HP), axis=1).reshape(1, NH),
                 ssd_norm_w=dnw, s5_a_re=g_a_re, s5_a_im=g_a_im, s5_log_dt=g_log_dt, s5_b_re=g_b_re, s5_b_im=g_b_im,
                 s5_c_re=dc_re, s5_c_im=dc_im, s5_d=dd5, w_glu_b=dbglu)
    return loss_acc[0, 0], dx0, dmod, big, small


def _place():
    return lax.axis_index("x"), lax.axis_index("y"), lax.axis_index("c")


def _other_chips(x, y):
    return [(1 - x, y), (x, 1 - y), (1 - x, 1 - y)]


def _allgather8(name, a):
    r, n = a.shape

    def body(x_ref, out_ref, send_sems, recv_sems, local_sem):
        x, y, c = _place()
        me, sibling = (x, y, c), (x, y, 1 - c)
        chips = _other_chips(x, y)

        def rows(px, py, pc):
            return out_ref.at[pl.ds(pl.multiple_of((4 * px + 2 * py + pc) * r, 8), r), :]

        def copy(k, block, to, src=None):
            return pltpu.make_async_remote_copy(src_ref=rows(*block) if src is None else src, dst_ref=rows(*block),
                                                send_sem=send_sems.at[k], recv_sem=recv_sems.at[k], device_id=to,
                                                device_id_type=MESH_T)

        mine = pltpu.make_async_copy(x_ref, rows(*me), local_sem)
        mine.start()
        first = [copy(0, me, sibling, src=x_ref)]
        first += [copy(1 + j, me, (*chip, c), src=x_ref) for j, chip in enumerate(chips)]
        for cp in first:
            cp.start()
        passed = [copy(4 + j, (*chip, c), sibling) for j, chip in enumerate(chips)]
        for j, chip in enumerate(chips):
            copy(1 + j, (*chip, c), me).wait_recv()
            passed[j].start()
        copy(0, sibling, me).wait_recv()
        for j, chip in enumerate(chips):
            copy(4 + j, (*chip, 1 - c), me).wait_recv()
        for cp in first + passed:
            cp.wait_send()
        mine.wait()

    out = _pcall(body, name=name, out_shape=_sds((8 * r, n), a.dtype),
                 in_specs=[pl.BlockSpec(memory_space=pltpu.VMEM)], out_specs=pl.BlockSpec(memory_space=pltpu.VMEM),
                 scratch_shapes=[pltpu.SemaphoreType.DMA((7,)), pltpu.SemaphoreType.DMA((7,)), pltpu.SemaphoreType.DMA],
                 compiler_params=_cparams())(a)
    return out.reshape(8, r, n)


class _Stage:
    def __init__(self, ins, out_shapes, n_sems, start, finish, mid=None, aliases=None):
        self.ins, self.out_shapes, self.n_sems = list(ins), list(out_shapes), tuple(n_sems)
        self.start, self.mid, self.finish = start, mid, finish
        self.aliases = dict(aliases or {})

    def sem_shapes(self):
        return [pltpu.SemaphoreType.DMA((n,)) for n in self.n_sems]

    def run(self, refs, at_start=None, at_mid=None, at_finish=None, between=None):
        if between is None:
            for part in (self.start, self.mid, self.finish):
                if part is not None:
                    part(*refs)
            return
        pl.when(at_start)(functools.partial(self.start, *refs))
        if self.mid is not None:
            pl.when(at_mid)(functools.partial(self.mid, *refs))
        between()
        pl.when(at_finish)(functools.partial(self.finish, *refs))


def _merge_stages(a, b):
    n_in, n_out, n_sem = len(a.ins), len(a.out_shapes), len(a.n_sems)

    def both(fa, fb):
        def run(ins, outs, sems):
            fa(ins[:n_in], outs[:n_out], sems[:n_sem])
            fb(ins[n_in:], outs[n_out:], sems[n_sem:])
        return run

    aliases = {**a.aliases, **{n_in + k: n_out + v for k, v in b.aliases.items()}}
    return _Stage(a.ins + b.ins, a.out_shapes + b.out_shapes, a.n_sems + b.n_sems, both(a.start, b.start),
                  both(a.finish, b.finish), aliases=aliases)


def _run_stage(name, st):
    n_in, n_out = len(st.ins), len(st.out_shapes)

    def body(*refs):
        st.run((refs[:n_in], refs[n_in:n_in + n_out], refs[n_in + n_out:]))

    any_spec = pl.BlockSpec(memory_space=pl.ANY)
    return _pcall(body, name=name, out_shape=st.out_shapes, in_specs=[any_spec] * n_in, out_specs=[any_spec] * n_out,
                  input_output_aliases=st.aliases, scratch_shapes=st.sem_shapes(), compiler_params=_cparams())(*st.ins)


def _rows(ref_rows, half, align):
    hr = ref_rows // 2
    return pl.ds(pl.multiple_of(half * hr, align), hr)


def _gather_stage(shards):
    n = len(shards)
    pairs = [(i, j) for i in range(n) for j in range(3)]
    align = [32 // s.dtype.itemsize for s in shards]

    def env(ins, outs, sems):
        x, y, c = _place()
        chips = _other_chips(x, y)

        def copy(i, k, chip, half, to, src=None):
            dst = outs[i].at[2 * chip[0] + chip[1], _rows(shards[i].shape[0], half, align[i])]
            return pltpu.make_async_remote_copy(src_ref=dst if src is None else src, dst_ref=dst,
                                                send_sem=sems[0].at[6 * i + k], recv_sem=sems[1].at[6 * i + k],
                                                device_id=to, device_id_type=MESH_T)

        def first(i, j):
            return copy(i, j, (x, y), c, (*chips[j], c), src=ins[i].at[_rows(shards[i].shape[0], c, align[i])])

        def passed(i, j, half):
            return copy(i, 3 + j, chips[j], half, (x, y, 1 - c))

        def landed(i, j):
            return copy(i, j, chips[j], c, (x, y, 1 - c))

        return c, first, passed, landed

    def start(ins, outs, sems):
        c, first, passed, landed = env(ins, outs, sems)
        for i, j in pairs:
            first(i, j).start()

    def mid(ins, outs, sems):
        c, first, passed, landed = env(ins, outs, sems)
        for i, j in pairs:
            landed(i, j).wait_recv()
            passed(i, j, c).start()

    def finish(ins, outs, sems):
        c, first, passed, landed = env(ins, outs, sems)
        for i, j in pairs:
            passed(i, j, 1 - c).wait_recv()
        for i, j in pairs:
            first(i, j).wait_send()
            passed(i, j, c).wait_send()

    return _Stage(shards, [_sds((4,) + s.shape, s.dtype) for s in shards], (6 * n, 6 * n), start, finish, mid)


def _rs_sibling_stage(gs):
    n = len(gs)

    def copies(ins, outs, sems):
        x, y, c = _place()
        return [pltpu.make_async_remote_copy(src_ref=ins[i].at[:, _rows(gs[i].shape[1], 1 - c, 8)], dst_ref=outs[i],
                                             send_sem=sems[0].at[i], recv_sem=sems[1].at[i], device_id=(x, y, 1 - c),
                                             device_id_type=MESH_T) for i in range(n)]

    def start(*refs):
        for cp in copies(*refs):
            cp.start()

    def finish(*refs):
        for cp in copies(*refs):
            cp.wait()

    return _Stage(gs, [_sds((4, g.shape[1] // 2, g.shape[2]), g.dtype) for g in gs], (n, n), start, finish)


def _rs_chips_stage(hs):
    n = len(hs)

    def copies(ins, outs, sems):
        x, y, c = _place()
        return [pltpu.make_async_remote_copy(src_ref=ins[i].at[2 * chip[0] + chip[1]], dst_ref=outs[i].at[j],
                                             send_sem=sems[0].at[3 * i + j], recv_sem=sems[1].at[3 * i + j],
                                             device_id=(*chip, c), device_id_type=MESH_T)
                for i in range(n) for j, chip in enumerate(_other_chips(x, y))]

    def start(*refs):
        for cp in copies(*refs):
            cp.start()

    def finish(*refs):
        for cp in copies(*refs):
            cp.wait()

    return _Stage(hs, [_sds((3,) + h.shape[1:], h.dtype) for h in hs], (3 * n, 3 * n), start, finish)


def _rs_join_stage(fs):
    n = len(fs)

    def copy(outs, sems, i, half):
        x, y, c = _place()
        part = outs[i].at[_rows(fs[i].shape[0], c if half == "mine" else 1 - c, 8)]
        return pltpu.make_async_remote_copy(src_ref=part, dst_ref=part, send_sem=sems[0].at[i], recv_sem=sems[1].at[i],
                                            device_id=(x, y, 1 - c), device_id_type=MESH_T)

    def start(ins, outs, sems):
        for i in range(n):
            copy(outs, sems, i, "mine").start()

    def finish(ins, outs, sems):
        for i in range(n):
            copy(outs, sems, i, "theirs").wait_recv()
        for i in range(n):
            copy(outs, sems, i, "mine").wait_send()

    return _Stage(fs, [_sds(f.shape, f.dtype) for f in fs], (n, n), start, finish, aliases={i: i for i in range(n)})


def _row_block(r, cap=2048):
    b = min(r, cap)
    while r % b or b % 8:
        b -= 8
    return b


RS_SPLIT = 2


def _rs_add(name, gs, r1s, sel):
    n = len(gs)

    def body(sel_ref, *refs):
        g_refs, r_refs, b_refs = (refs[k * n:(k + 1) * n] for k in range(3))
        for i in range(n):
            b_refs[i][...] = (g_refs[i][...] + r_refs[i][...]).astype(BF16)

    def blk(g):
        return (1, g.shape[1] // 2 // RS_SPLIT, g.shape[2])

    here = lambda k, j, s: (k, j, 0)
    in_specs = [pl.BlockSpec(blk(g), lambda k, j, s: (k, s[0] * RS_SPLIT + j, 0)) for g in gs]
    in_specs += [pl.BlockSpec(blk(g), here) for g in gs]
    return _pcall(body, name=name, out_shape=[_sds((4, g.shape[1] // 2, g.shape[2]), BF16) for g in gs],
                  grid_spec=pltpu.PrefetchScalarGridSpec(num_scalar_prefetch=1, grid=(4, RS_SPLIT), in_specs=in_specs,
                                                         out_specs=[pl.BlockSpec(blk(g), here) for g in gs]),
                  compiler_params=_cparams(("parallel", "parallel")))(sel.reshape(1).astype(jnp.int32), *gs, *r1s)


def _rs_sum(name, gs, r1s, r2s, chip, half):
    n = len(gs)

    def body(sel_ref, *refs):
        g_refs, h_refs, r_refs, o_refs = (refs[k * n:(k + 1) * n] for k in range(4))
        for i in range(n):
            r = r_refs[i]
            o_refs[i][...] = (((g_refs[i][0] + h_refs[i][0]) + r[0].astype(F32)) + r[1].astype(F32)) + r[2].astype(F32)

    def rows(g):
        return g.shape[1] // 2 // RS_SPLIT

    in_specs = [pl.BlockSpec((1, rows(g), g.shape[2]), lambda j, s: (s[0], s[1] * RS_SPLIT + j, 0)) for g in gs]
    in_specs += [pl.BlockSpec((1, rows(g), g.shape[2]), lambda j, s: (s[0], j, 0)) for g in gs]
    in_specs += [pl.BlockSpec((3, rows(g), g.shape[2]), lambda j, s: (0, j, 0)) for g in gs]
    sel = jnp.stack([chip, half]).astype(jnp.int32)
    return _pcall(body, name=name, out_shape=[_sds(g.shape[1:]) for g in gs],
                  grid_spec=pltpu.PrefetchScalarGridSpec(
                      num_scalar_prefetch=1, grid=(RS_SPLIT,), in_specs=in_specs,
                      out_specs=[pl.BlockSpec((rows(g), g.shape[2]), lambda j, s: (s[1] * RS_SPLIT + j, 0)) for g in gs]),
                  compiler_params=_cparams(("parallel",)))(sel, *gs, *r1s, *r2s)


def _sum8(name, a):
    _, r, n = a.shape
    br = _row_block(r)

    def body(a_ref, o_ref):
        acc = a_ref[0]
        for k in range(1, 8):
            acc = acc + a_ref[k]
        o_ref[...] = acc

    return _pcall(body, name=name, out_shape=_sds((r, n)), grid=(r // br,),
                  in_specs=[pl.BlockSpec((8, br, n), lambda j: (0, j, 0))], out_specs=pl.BlockSpec((br, n), lambda j: (j, 0)),
                  compiler_params=_cparams(("parallel",)))(a)


def _adamw(name, ws, gs, ms, vs, nblk):
    n = len(ws)

    def body(*refs):
        w_refs, g_refs, m_refs, v_refs, d_refs, nm_refs, nv_refs = (refs[k * n:(k + 1) * n] for k in range(7))
        for i in range(n):
            gv = g_refs[i][...]
            nm = ADAM_B1 * m_refs[i][...] + (1.0 - ADAM_B1) * gv
            nv = ADAM_B2 * v_refs[i][...] + (1.0 - ADAM_B2) * (gv * gv)
            nm_refs[i][...] = nm
            nv_refs[i][...] = nv
            m_hat = nm / (1.0 - ADAM_B1 ** ADAM_STEP)
            v_hat = nv / (1.0 - ADAM_B2 ** ADAM_STEP)
            d_refs[i][...] = -ADAM_LR * (m_hat / (jnp.sqrt(v_hat) + ADAM_EPS) + ADAM_WD * w_refs[i][...])

    specs = [pl.BlockSpec((w.shape[0] // nblk, w.shape[1]), lambda j: (j, 0)) for w in ws]
    outs = _pcall(body, name=name, out_shape=[_sds(w.shape) for w in ws] * 3, grid=(nblk,), in_specs=specs * 4,
                  out_specs=specs * 3, compiler_params=_cparams(("parallel",)))(*ws, *gs, *ms, *vs)
    return outs[:n], outs[n:2 * n], outs[2 * n:]


ADA_COLS = 2304
ADA_BLK = 768


def _ada_fwd(name, c_all, w_shard, b_cols):
    nb = c_all.shape[0]

    def body(c_ref, w_ref, b_ref, o_ref):
        cv = c_ref[...]
        cs = _mx(cv * _sigmoid(cv))
        o_ref[...] = _dot(cs, _mx(w_ref[...])) + b_ref[...]

    return _pcall(body, name=name, out_shape=_sds((nb, ADA_COLS)), grid=(ADA_COLS // ADA_BLK,),
                  in_specs=[pl.BlockSpec((nb, D), lambda j: (0, 0)), pl.BlockSpec((D, ADA_BLK), lambda j: (0, j)),
                            pl.BlockSpec((1, ADA_BLK), lambda j: (0, j))],
                  out_specs=pl.BlockSpec((nb, ADA_BLK), lambda j: (0, j)),
                  compiler_params=_cparams(("parallel",)))(c_all, w_shard, b_cols)


def _ada_bwd(name, c_all, dmod_cols, dmod_all):
    nb = c_all.shape[0]

    def body(c_ref, dc_ref, da_ref, gw_ref, gb_ref):
        cv = c_ref[...]
        cs = _mx(cv * _sigmoid(cv))
        gw_ref[...] = lax.dot_general(cs, _mx(dc_ref[...]), (((0,), (0,)), ((), ())), preferred_element_type=F32)

        @pl.when(pl.program_id(0) == 0)
        def _():
            gb_ref[...] = jnp.sum(da_ref[...], axis=0, keepdims=True)

    return _pcall(body, name=name, out_shape=[_sds((D, ADA_COLS)), _sds((1, 9 * D))], grid=(ADA_COLS // ADA_BLK,),
                  in_specs=[pl.BlockSpec((nb, D), lambda j: (0, 0)), pl.BlockSpec((nb, ADA_BLK), lambda j: (0, j)),
                            pl.BlockSpec((nb, 9 * D), lambda j: (0, 0))],
                  out_specs=[pl.BlockSpec((D, ADA_BLK), lambda j: (0, j)), pl.BlockSpec((1, 9 * D), lambda j: (0, 0))],
                  compiler_params=_cparams(("arbitrary",)))(c_all, dmod_cols, dmod_all)


BIG = ("ffn1_w1", "ffn1_w3", "ffn1_w2", "w_in", "w_glu", "w_out", "ffn2_w1", "ffn2_w3", "ffn2_w2")
COL_SHARDED = ("w_in",)
TRANSPOSED = ("ffn1_w1", "ffn1_w3", "ffn2_w1", "ffn2_w3")
SMALL = ("b_ada", "ln1_g", "ln1_b", "conv_w", "conv_b", "dt_bias", "a_log", "d_ssd", "ssd_norm_w", "s5_a_re", "s5_a_im",
         "s5_log_dt", "s5_b_re", "s5_b_im", "s5_c_re", "s5_c_im", "s5_d", "b_glu", "ln2_g", "ln2_b", "ln3_g", "ln3_b")
WEIGHTS = ("w_ada", "b_ada", "ffn1_w1", "ffn1_w3", "ffn1_w2", "ln1_g", "ln1_b", "w_in", "conv_w", "conv_b", "dt_bias",
           "a_log", "d_ssd", "ssd_norm_w", "s5_a_re", "s5_a_im", "s5_log_dt", "s5_b_re", "s5_b_im", "s5_c_re", "s5_c_im",
           "s5_d", "w_glu", "b_glu", "w_out", "ln2_g", "ln2_b", "ffn2_w1", "ffn2_w3", "ffn2_w2", "ln3_g", "ln3_b")
BIG_PAD = 2 * 1024 * 128


def _pack(arrs, mult, axis_keep=0):
    lead = arrs[0].shape[:axis_keep]
    flat = jnp.concatenate([a.reshape(lead + (-1,)) for a in arrs], axis=axis_keep)
    pad = (-flat.shape[-1]) % mult
    if pad:
        flat = jnp.concatenate([flat, jnp.zeros(lead + (pad,), flat.dtype)], axis=axis_keep)
    return flat


def _unpack(flat, shapes):
    out, off = [], 0
    for s in shapes:
        size = math.prod(s)
        out.append(flat[..., off:off + size].reshape(flat.shape[:-1] + tuple(s)))
        off += size
    return out


def _shard_major(a):
    rows, cols = a.shape
    return a.reshape(rows, 4, cols // 4).transpose(1, 0, 2)


def _from_shard_major(a):
    _, rows, w = a.shape
    return a.transpose(1, 0, 2).reshape(rows, 4 * w)


def kernel(x, c, w_ada, b_ada, ffn1_w1, ffn1_w3, ffn1_w2, ln1_g, ln1_b, w_in, conv_w, conv_b, dt_bias, a_log, d_ssd, ssd_norm_w, s5_a_re, s5_a_im, s5_log_dt, s5_b_re, s5_b_im, s5_c_re, s5_c_im, s5_d, w_glu, b_glu, w_out, ln2_g, ln2_b, ffn2_w1, ffn2_w3, ffn2_w2, ln3_g, ln3_b, loss_target, m_w_ada, m_b_ada, m_ffn1_w1, m_ffn1_w3, m_ffn1_w2, m_ln1_g, m_ln1_b, m_w_in, m_conv_w, m_conv_b, m_dt_bias, m_a_log, m_d_ssd, m_ssd_norm_w, m_s5_a_re, m_s5_a_im, m_s5_log_dt, m_s5_b_re, m_s5_b_im, m_s5_c_re, m_s5_c_im, m_s5_d, m_w_glu, m_b_glu, m_w_out, m_ln2_g, m_ln2_b, m_ffn2_w1, m_ffn2_w3, m_ffn2_w2, m_ln3_g, m_ln3_b, v_w_ada, v_b_ada, v_ffn1_w1, v_ffn1_w3, v_ffn1_w2, v_ln1_g, v_ln1_b, v_w_in, v_conv_w, v_conv_b, v_dt_bias, v_a_log, v_d_ssd, v_ssd_norm_w, v_s5_a_re, v_s5_a_im, v_s5_log_dt, v_s5_b_re, v_s5_b_im, v_s5_c_re, v_s5_c_im, v_s5_d, v_w_glu, v_b_glu, v_w_out, v_ln2_g, v_ln2_b, v_ffn2_w1, v_ffn2_w3, v_ffn2_w2, v_ln3_g, v_ln3_b):
    a = dict(locals())
    xi, yi, ci = _place()
    chip = 2 * xi + yi
    dev = 2 * chip + ci
    nb, seq, _ = x.shape
    t = nb * seq
    ndev = 8

    c_rows = nb * D // 128
    c_cw = _allgather8("gather_c", jnp.concatenate([c.reshape(c_rows, 128), conv_w.reshape(-1, 128)], axis=0))
    c_all = c_cw[:, :c_rows].reshape(ndev * nb, D)
    b_cols = lax.dynamic_slice(b_ada, (0, chip * ADA_COLS), (1, ADA_COLS))
    mod_part = _ada_fwd("ada_fwd", c_all, w_ada[0], b_cols)

    def nat(n):
        return jnp.swapaxes(a[n], 1, 2)[0] if n[-7:] in TRANSPOSED else a[n][0]

    def gather(names, extra=()):
        own = [nat(n).astype(MXU_DTYPE) for n in names]

        def weights(pieces):
            w = {}
            for n, mine, piece in zip(names, own, pieces):
                piece = lax.dynamic_update_slice(piece, mine[None], (chip, 0, 0))
                if n in COL_SHARDED:
                    wi = _from_shard_major(piece)
                    w[n] = jnp.concatenate([wi[:, :1536], wi[:, 1544:2056], wi[:, 1536:1544],
                                            jnp.zeros((D, 120), wi.dtype)], axis=1)
                else:
                    w[n + "t" if n in TRANSPOSED else n] = piece.reshape(-1, piece.shape[-1])
            return w

        return _gather_stage(own + list(extra)), weights

    first_stage, first_weights = gather(("ffn1_w1", "ffn1_w3"), extra=[mod_part])
    got = _run_stage("gather_w_first", first_stage)
    w = first_weights(got[:2])
    mod_all = lax.dynamic_update_slice(got[2], mod_part[None], (chip, 0, 0)).transpose(1, 0, 2).reshape(ndev * nb, 9 * D)
    mod = lax.dynamic_slice(mod_all, (nb * dev, 0), (nb, 9 * D)).reshape(nb, 9, D)

    conv_full = _from_shard_major(c_cw[0::2, c_rows:].reshape(4, 4, 256))
    pad8 = lambda v: jnp.concatenate([v.reshape(1, NH), jnp.zeros((1, 128 - NH), F32)], axis=1)
    sp = dict(ln1_g=ln1_g, ln1_b=ln1_b, ln2_g=ln2_g, ln2_b=ln2_b, ln3_g=ln3_g, ln3_b=ln3_b, conv_w=conv_full,
              conv_b=conv_b, dt_bias=pad8(dt_bias), a_log=pad8(a_log), d_rep=jnp.repeat(d_ssd[0], HP)[None],
              ssd_norm_w=ssd_norm_w, s5_a_re=s5_a_re[0], s5_a_im=s5_a_im[0], s5_log_dt=s5_log_dt[0], s5_b_re=s5_b_re[0],
              s5_b_im=s5_b_im[0], s5_c_re=s5_c_re[0], s5_c_im=s5_c_im[0], s5_d=s5_d, b_glu=b_glu)

    lsum, dx0, dmod, gbig, small = _local_step(x.reshape(t, D), loss_target.reshape(t, D), mod, w, sp, seq,
                                               dist=(gather(("ffn1_w2", "w_in", "w_glu", "w_out")),
                                                     gather(("ffn2_w1", "ffn2_w3", "ffn2_w2")), (ci, chip)))

    snames = [n for n in SMALL if n != "b_ada"]
    sgrad = dict(small)
    sgrad["b_glu"] = small["w_glu_b"]
    dm_rows = nb * 9 * D // 128
    tail = _allgather8("gather_small", jnp.concatenate(
        [dmod.reshape(dm_rows, 128),
         _pack([(lsum * (0.5 / D)).reshape(1)] + [sgrad[n] for n in snames], 1024).reshape(-1, 128)], axis=0))
    dmod_all = tail[:, :dm_rows].reshape(ndev * nb, 9 * D)
    dmod_cols = lax.dynamic_slice(dmod_all, (0, chip * ADA_COLS), (ndev * nb, ADA_COLS))
    g_w_ada, g_b_ada = _ada_bwd("ada_bwd", c_all, dmod_cols, dmod_all)

    gbig["w_ada"] = g_w_ada

    outs = {}
    for call, names in (("adamw_a", ("ffn1_w1", "ffn1_w3", "ffn1_w2", "w_in", "w_glu", "w_out")),
                        ("adamw_b", ("ffn2_w1", "ffn2_w3", "ffn2_w2", "w_ada"))):
        res = _adamw(call, [nat(n) for n in names], [gbig[n] for n in names], [nat("m_" + n) for n in names],
                     [nat("v_" + n) for n in names], 8)
        for kind, arrs in zip(("grad", "delta", "new_m", "new_v"), ([gbig[n] for n in names],) + tuple(res)):
            for n, arr in zip(names, arrs):
                outs[kind, n] = (arr.T if n in TRANSPOSED else arr)[None]

    ssum = _sum8("small_sum", tail)[dm_rows:].reshape(-1)

    def view2d(u):
        s = u.shape[1:]
        return u.reshape((1, s[0]) if len(s) == 1 else (-1, s[-1]))

    vshape = {n: view2d(a[n]).shape for n in SMALL}
    loss, *sums = _unpack(ssum, [()] + [vshape[n] if n != "conv_w" else (4, D) for n in snames])
    gsm = dict(zip(snames, sums))
    gsm["conv_w"] = lax.dynamic_slice(gsm["conv_w"], (0, chip * 256), (4, 256))
    gsm["b_ada"] = g_b_ada
    res = _adamw("adamw_small", [view2d(a[n]) for n in SMALL], [gsm[n] for n in SMALL],
                 [view2d(a["m_" + n]) for n in SMALL], [view2d(a["v_" + n]) for n in SMALL], 1)
    for kind, arrs in zip(("grad", "delta", "new_m", "new_v"), ([gsm[n] for n in SMALL],) + tuple(res)):
        for n, arr in zip(SMALL, arrs):
            outs[kind, n] = arr.reshape(a[n].shape)

    res = [loss, dx0.reshape(nb, seq, D)]
    for kind in ("grad", "delta", "new_m", "new_v"):
        res += [outs[kind, n] for n in WEIGHTS]
    return tuple(res)
```

```python
import functools
import math

import jax
import jax.numpy as jnp
from jax import lax
from jax.experimental import pallas as pl
from jax.experimental.pallas import tpu as pltpu

F32 = jnp.float32
BF16 = jnp.bfloat16
MXU_DTYPE = jnp.bfloat16

D = 1024
FF = 2816
FB = 1408
NH, HP, NS, NG = 8, 64, 128, 2
CH = 128
SW = 512
S5G, S5P, S5H = 32, 64, 16
S5L = S5G * S5P
PW = 2176
ALPHA = 2.0 ** 0.25
LN_EPS = 1e-5
ADAM_LR, ADAM_B1, ADAM_B2, ADAM_EPS, ADAM_WD, ADAM_STEP = 0.001, 0.9, 0.999, 1e-08, 0.01, 10
VMEM_LIMIT = 56 * 1024 * 1024
TM_WIDE = 512
MESH_T = pl.DeviceIdType.MESH


def _pcall(body, **kw):
    return pl.pallas_call(body, **kw)


def _cparams(sem=None, **kw):
    return pltpu.CompilerParams(dimension_semantics=sem, vmem_limit_bytes=VMEM_LIMIT, **kw)


def _dot(a, b):
    return jnp.dot(a, b, preferred_element_type=F32)


def _dot_nt(a, b):
    return lax.dot_general(a, b, (((1,), (1,)), ((), ())), preferred_element_type=F32)


def _dot_hi(a, b):
    return jnp.dot(a, b, preferred_element_type=F32, precision=lax.Precision.HIGHEST)


def _mx(a):
    return a.astype(MXU_DTYPE)


def _sigmoid(x):
    return 1.0 / (1.0 + jnp.exp(-x))


def _iota(shape, axis):
    return lax.broadcasted_iota(jnp.int32, shape, axis)


def _rowcall(name, fn, n_rows, tm, tpe, *, tiled=(), halos=(), exs=(), res=(), out_tiled=(), out_acc=(),
             out_exacc=(), scratch=(), reverse=False, batch=None, carry=None, carry_mid=0.0):
    if batch:
        n_rows //= batch
    nt = n_rows // tm

    def blk(i):
        return (nt - 1 - i) if reverse else i

    in_specs, args = [], []
    for a in tiled:
        if batch:
            in_specs.append(pl.BlockSpec((batch, tm, a.shape[1]), lambda i: (0, blk(i), 0)))
            args.append(a.reshape(batch, n_rows, a.shape[1]))
            continue
        in_specs.append(pl.BlockSpec((tm, a.shape[1]), lambda i: (blk(i), 0)))
        args.append(a)
    for a, rows in halos:
        r = tm // rows
        if batch:
            in_specs.append(pl.BlockSpec((batch, rows, a.shape[1]), lambda i, r=r: (0, jnp.maximum(blk(i) * r - 1, 0), 0)))
            args.append(a.reshape(batch, n_rows, a.shape[1]))
            continue
        in_specs.append(pl.BlockSpec((rows, a.shape[1]), lambda i, r=r: (jnp.maximum(blk(i) * r - 1, 0), 0)))
        args.append(a)
    for a in exs:
        in_specs.append(pl.BlockSpec((1,) + a.shape[1:], lambda i: (blk(i) // tpe, 0, 0)))
        args.append(a)
    for a in res:
        nd = a.ndim
        in_specs.append(pl.BlockSpec(a.shape, lambda i, nd=nd: (0,) * nd, pipeline_mode=pl.Buffered(1)))
        args.append(a)
    any_spec = pl.BlockSpec(memory_space=pl.ANY)
    st_ins = carry.ins if carry else []
    st_outs = carry.out_shapes if carry else []
    st_sems = carry.sem_shapes() if carry else []
    base_in = len(args)
    in_specs += [any_spec] * len(st_ins)
    args += st_ins
    out_specs, out_shape = [], []
    for s in out_tiled:
        if batch:
            out_specs.append(pl.BlockSpec((batch, tm, s.shape[1]), lambda i: (0, blk(i), 0)))
            out_shape.append(_sds((batch, n_rows, s.shape[1]), s.dtype))
            continue
        out_specs.append(pl.BlockSpec((tm, s.shape[1]), lambda i: (blk(i), 0)))
        out_shape.append(s)
    for s in out_acc:
        nd = len(s.shape)
        out_specs.append(pl.BlockSpec(s.shape, lambda i, nd=nd: (0,) * nd))
        out_shape.append(s)
    for s in out_exacc:
        out_specs.append(pl.BlockSpec((1,) + s.shape[1:], lambda i: (blk(i) // tpe, 0, 0)))
        out_shape.append(s)
    base_out = len(out_shape)
    out_specs += [any_spec] * len(st_outs)
    out_shape += st_outs
    aliases = {base_in + k: base_out + v for k, v in carry.aliases.items()} if carry else {}
    n = [len(tiled), len(halos), len(exs), len(res), len(st_ins), len(out_tiled), len(out_acc), len(out_exacc),
         len(st_outs), len(scratch), len(st_sems)]

    def body(*refs):
        groups, k = [], 0
        for m in n:
            groups.append(refs[k:k + m])
            k += m
        i = pl.program_id(0)
        b = blk(i)

        class ctx:
            first = i == 0
            pos = b % tpe
            seq_first = (b % tpe) == 0
            seq_last = (b % tpe) == tpe - 1
            ex_enter = (i % tpe) == 0

        def work():
            for cond, refs_ in ((ctx.first, groups[6]), (ctx.ex_enter, groups[7])):
                if refs_:
                    @pl.when(cond)
                    def _():
                        for r in refs_:
                            r[...] = jnp.zeros(r.shape, r.dtype)
            fn(ctx, *groups[0:4], *groups[5:8], groups[9])

        if carry:
            carry.run((groups[4], groups[8], groups[10]), i == 0, i == min(nt - 1, int(carry_mid * nt)), i == nt - 1, work)
        else:
            work()

    outs = _pcall(body, name=name, grid=(nt,), in_specs=in_specs, out_specs=out_specs, out_shape=out_shape,
                  input_output_aliases=aliases, scratch_shapes=list(scratch) + st_sems,
                  compiler_params=_cparams(("arbitrary",)))(*args)
    if batch:
        outs = [o.reshape(batch * n_rows, o.shape[2]) if k < len(out_tiled) else o for k, o in enumerate(outs)]
    return (outs[:base_out], outs[base_out:]) if carry else outs


def _acc(ref, val):
    ref[...] += val


def _sds(shape, dtype=F32):
    return jax.ShapeDtypeStruct(shape, dtype)


def _ln_fwd(r, g, b):
    mu = jnp.mean(r, axis=-1, keepdims=True)
    rc = r - mu
    var = jnp.mean(rc * rc, axis=-1, keepdims=True)
    return rc * lax.rsqrt(var + LN_EPS) * g + b


def _ln_bwd(r, g, dy):
    mu = jnp.mean(r, axis=-1, keepdims=True)
    rc = r - mu
    var = jnp.mean(rc * rc, axis=-1, keepdims=True)
    rstd = lax.rsqrt(var + LN_EPS)
    xhat = rc * rstd
    dxh = dy * g
    dr = rstd * (dxh - jnp.mean(dxh, axis=-1, keepdims=True) - xhat * jnp.mean(dxh * xhat, axis=-1, keepdims=True))
    return dr, jnp.sum(dy * xhat, axis=0, keepdims=True), jnp.sum(dy, axis=0, keepdims=True)


def _ffn_fwd_hidden(name, x, mod, k0, w1, w3, seq, **carry):
    t = x.shape[0]
    tm = min(TM_WIDE, seq)

    def fn(ctx, tiled, halos, exs, res, outs, accs, exaccs, scr):
        mod_ref, = exs
        w1_ref, w3_ref = res
        h_ref, ab_ref, s_ref = outs
        sh, sc = mod_ref[0, k0:k0 + 1, :], mod_ref[0, k0 + 1:k0 + 2, :]
        h = _mx(tiled[0][...] * (1.0 + sc) + sh)
        h_ref[...] = h
        for j in range(2):
            a = _dot_nt(h, w1_ref[j * FB:(j + 1) * FB, :])
            b = _dot_nt(h, w3_ref[j * FB:(j + 1) * FB, :])
            ab_ref[:, j * FB:(j + 1) * FB] = _mx(a)
            ab_ref[:, FF + j * FB:FF + (j + 1) * FB] = _mx(b)
            s_ref[:, j * FB:(j + 1) * FB] = _mx(a * _sigmoid(a) * b)

    return _rowcall(name, fn, t, tm, seq // tm, tiled=[x], exs=[mod], res=[w1, w3],
                    out_tiled=[_sds((t, D), MXU_DTYPE), _sds((t, 2 * FF), MXU_DTYPE), _sds((t, FF), MXU_DTYPE)], **carry)


def _ffn_fwd_out(name, s, x, mod, k0, w2, lng, lnb, seq, tgt=None, **carry):
    t = x.shape[0]
    tm = min(TM_WIDE, seq)
    with_loss = tgt is not None

    def fn(ctx, tiled, halos, exs, res, outs, accs, exaccs, scr):
        mod_ref, = exs
        w2_ref, g_ref, b_ref = res
        xo_ref, r_ref, f_ref = outs[:3]
        f = _dot(tiled[0][...], w2_ref[...])
        f_ref[...] = f
        r = ALPHA * tiled[1][...] + 0.5 * mod_ref[0, k0 + 2:k0 + 3, :] * f
        r_ref[...] = r
        xo = _ln_fwd(r, g_ref[...], b_ref[...])
        xo_ref[...] = xo
        if with_loss:
            e = xo - tiled[2][...]
            outs[3][...] = e * (1.0 / D)
            _acc(accs[0], jnp.sum(e * e) * jnp.ones((8, 128), F32))

    return _rowcall(name, fn, t, tm, seq // tm, tiled=[s, x] + ([tgt] if with_loss else []), exs=[mod],
                    res=[w2, lng, lnb], out_tiled=[_sds((t, D))] * (4 if with_loss else 3),
                    out_acc=[_sds((8, 128))] if with_loss else [], **carry)


def _ffn_bwd1(name, dxo, r, ab, f, mod, k0, lng, w2, seq, **carry):
    t = dxo.shape[0]
    tm = 256

    def fn(ctx, tiled, halos, exs, res, outs, accs, exaccs, scr):
        dxo_ref, r_ref, ab_ref, f_ref = tiled
        mod_ref, = exs
        g_ref, w2_ref = res
        dr_ref, df_ref, dab_ref = outs
        g = mod_ref[0, k0 + 2:k0 + 3, :]
        dr, dgam, dbet = _ln_bwd(r_ref[...], g_ref[...], dxo_ref[...])
        dr_ref[...] = dr
        _acc(accs[0], dgam)
        _acc(accs[1], dbet)
        _acc(exaccs[0].at[0], jnp.sum(0.5 * f_ref[...] * dr, axis=0, keepdims=True))
        df = _mx(0.5 * g * dr)
        df_ref[...] = df
        for j in range(2):
            ds = _dot_nt(df, w2_ref[j * FB:(j + 1) * FB, :])
            a = ab_ref[:, j * FB:(j + 1) * FB].astype(F32)
            b = ab_ref[:, FF + j * FB:FF + (j + 1) * FB].astype(F32)
            sig = _sigmoid(a)
            dab_ref[:, j * FB:(j + 1) * FB] = _mx(ds * b * (sig * (1.0 + a * (1.0 - sig))))
            dab_ref[:, FF + j * FB:FF + (j + 1) * FB] = _mx(ds * (a * sig))

    b = mod.shape[0]
    return _rowcall(name, fn, t, tm, seq // tm, tiled=[dxo, r, ab, f], exs=[mod], res=[lng, w2],
                    out_tiled=[_sds((t, D)), _sds((t, D), MXU_DTYPE), _sds((t, 2 * FF), MXU_DTYPE)],
                    out_acc=[_sds((1, D)), _sds((1, D))], out_exacc=[_sds((b, 1, D))], **carry)


def _ffn_bwd(name, dxo, r, ab, f, x, mod, k0, lng, w2, w1t, w3t, seq):
    t = dxo.shape[0]
    tm = 256

    def fn(ctx, tiled, halos, exs, res, outs, accs, exaccs, scr):
        dxo_ref, r_ref, ab_ref, f_ref, x_ref = tiled
        mod_ref, = exs
        g_ref, w2_ref, w1t_ref, w3t_ref = res
        dx_ref, df_ref, dab_ref = outs
        sc, g = mod_ref[0, k0 + 1:k0 + 2, :], mod_ref[0, k0 + 2:k0 + 3, :]
        dr, dgam, dbet = _ln_bwd(r_ref[...], g_ref[...], dxo_ref[...])
        _acc(accs[0], dgam)
        _acc(accs[1], dbet)
        _acc(exaccs[0].at[0], jnp.sum(0.5 * f_ref[...] * dr, axis=0, keepdims=True))
        df = _mx(0.5 * g * dr)
        df_ref[...] = df
        dh = jnp.zeros((tm, D), F32)
        for j in range(2):
            blk = slice(j * FB, (j + 1) * FB)
            ds = _dot_nt(df, w2_ref[blk, :])
            a = ab_ref[:, blk].astype(F32)
            b = ab_ref[:, FF + j * FB:FF + (j + 1) * FB].astype(F32)
            sig = _sigmoid(a)
            da = _mx(ds * b * (sig * (1.0 + a * (1.0 - sig))))
            db = _mx(ds * (a * sig))
            dab_ref[:, blk] = da
            dab_ref[:, FF + j * FB:FF + (j + 1) * FB] = db
            dh = dh + _dot(da, w1t_ref[blk, :]) + _dot(db, w3t_ref[blk, :])
        dx_ref[...] = ALPHA * dr + dh * (1.0 + sc)
        _acc(exaccs[1].at[0], jnp.sum(dh, axis=0, keepdims=True))
        _acc(exaccs[2].at[0], jnp.sum(dh * x_ref[...], axis=0, keepdims=True))

    b = mod.shape[0]
    return _rowcall(name, fn, t, tm, seq // tm, tiled=[dxo, r, ab, f, x], exs=[mod], res=[lng, w2, w1t, w3t],
                    out_tiled=[_sds((t, D)), _sds((t, D), MXU_DTYPE), _sds((t, 2 * FF), MXU_DTYPE)],
                    out_acc=[_sds((1, D)), _sds((1, D))], out_exacc=[_sds((b, 1, D))] * 3)


def _mod_bwd(name, dab, dr, x, mod, k0, wts, seq, extra=(), nt=False, **carry):
    t = dr.shape[0]
    tm = min(TM_WIDE, seq)
    nin = 1 + len(extra)
    width = dab.shape[1] + sum(e.shape[1] for e in extra)

    def fn(ctx, tiled, halos, exs, res, outs, accs, exaccs, scr):
        parts = tiled[:nin]
        dr_ref, x_ref = tiled[nin:]
        mod_ref, = exs
        sc = mod_ref[0, k0 + 1:k0 + 2, :]
        if nin == 1:
            dp = parts[0][...]
        else:
            dp = jnp.concatenate([_mx(p[...]) for p in parts], axis=1)
            outs[1][...] = dp
        dh, off = jnp.zeros((tm, D), F32), 0
        for w_ref in res:
            k = w_ref.shape[1 if nt else 0]
            dh = dh + (_dot_nt if nt else _dot)(dp[:, off:off + k], w_ref[...])
            off += k
        outs[0][...] = ALPHA * dr_ref[...] + dh * (1.0 + sc)
        _acc(exaccs[0].at[0], jnp.sum(dh, axis=0, keepdims=True))
        _acc(exaccs[1].at[0], jnp.sum(dh * x_ref[...], axis=0, keepdims=True))

    b = mod.shape[0]
    out_tiled = [_sds((t, D))] + ([_sds((t, width), MXU_DTYPE)] if nin > 1 else [])
    return _rowcall(name, fn, t, tm, seq // tm, tiled=[dab, *extra, dr, x], exs=[mod], res=list(wts),
                    out_tiled=out_tiled, out_exacc=[_sds((b, 1, D)), _sds((b, 1, D))], **carry)


def _tn_matmul(name, a, b, bm, bn, bt=2048, carry=None, a_cols=None):
    t = a.shape[0]
    bt = min(bt, t)
    start, m = a_cols or (0, a.shape[1])
    off = start // bm
    n = b.shape[1]
    grid = (m // bm, n // bn, t // bt)
    n_in, n_out = (len(carry.ins), len(carry.out_shapes)) if carry else (0, 0)

    def body(a_ref, b_ref, *refs):
        o_ref = refs[n_in]

        def work():
            @pl.when(pl.program_id(2) == 0)
            def _():
                o_ref[...] = jnp.zeros(o_ref.shape, F32)
            o_ref[...] += lax.dot_general(a_ref[...], b_ref[...], (((0,), (0,)), ((), ())), preferred_element_type=F32)

        if not carry:
            return work()
        step = (pl.program_id(0) * grid[1] + pl.program_id(1)) * grid[2] + pl.program_id(2)
        carry.run((refs[:n_in], refs[n_in + 1:n_in + 1 + n_out], refs[n_in + 1 + n_out:]), step == 0, step == 0,
                  step == grid[0] * grid[1] * grid[2] - 1, work)

    any_spec = pl.BlockSpec(memory_space=pl.ANY)
    outs = _pcall(body, name=name, grid=grid,
                  in_specs=[pl.BlockSpec((bt, bm), lambda i, j, k: (k, i + off)), pl.BlockSpec((bt, bn), lambda i, j, k: (k, j))]
                  + [any_spec] * n_in,
                  out_specs=[pl.BlockSpec((bm, bn), lambda i, j, k: (i, j))] + [any_spec] * n_out,
                  out_shape=[_sds((m, n))] + (carry.out_shapes if carry else []),
                  input_output_aliases={2 + k: 1 + v for k, v in carry.aliases.items()} if carry else {},
                  scratch_shapes=carry.sem_shapes() if carry else [],
                  compiler_params=_cparams(("arbitrary",) * 3 if carry else ("parallel", "parallel", "arbitrary")))(
                      a, b, *(carry.ins if carry else []))
    return (outs[0], outs[1:]) if carry else outs[0]


S5_BLK = 16


def _tn_diag(name, a, b, bt=1024):
    t = a.shape[0]
    bt = min(bt, t)
    rows, cols = S5_BLK * S5P, S5_BLK * S5H
    nblk = a.shape[1] // rows

    def body(a_ref, b_ref, o_ref):
        @pl.when(pl.program_id(1) == 0)
        def _():
            o_ref[...] = jnp.zeros(o_ref.shape, F32)
        o_ref[0] += lax.dot_general(a_ref[...], b_ref[...], (((0,), (0,)), ((), ())), preferred_element_type=F32)

    return _pcall(body, name=name, grid=(nblk, t // bt),
                  in_specs=[pl.BlockSpec((bt, rows), lambda i, k: (k, i)),
                            pl.BlockSpec((bt, cols), lambda i, k: (k, i % (S5G // S5_BLK)))],
                  out_specs=pl.BlockSpec((1, rows, cols), lambda i, k: (i, 0, 0)), out_shape=_sds((nblk, rows, cols)),
                  compiler_params=_cparams(("parallel", "arbitrary")))(a, b)


def _diag_groups(o):
    o = o.reshape(2, S5G // S5_BLK, S5_BLK, S5P, S5_BLK, S5H)
    return jnp.einsum("rbgpgh->rbgph", o).reshape(2, S5G, S5P, S5H)


def _proj_fwd(name, x, mod, w_in, seq):
    t = x.shape[0]
    tm = min(TM_WIDE, seq)

    def fn(ctx, tiled, halos, exs, res, outs, accs, exaccs, scr):
        mod_ref, = exs
        sh, sc = mod_ref[0, 3:4, :], mod_ref[0, 4:5, :]
        h = _mx(tiled[0][...] * (1.0 + sc) + sh)
        outs[0][...] = h
        outs[1][...] = _dot(h, res[0][...])

    return _rowcall(name, fn, t, tm, seq // tm, tiled=[x], exs=[mod], res=[w_in],
                    out_tiled=[_sds((t, D), MXU_DTYPE), _sds((t, PW))])


def _shift_rows(cur, prev8, j):
    if j == 0:
        return cur
    rolled = pltpu.roll(cur, j, 0)
    top = jnp.where(_iota((8, cur.shape[1]), 0) < j, pltpu.roll(prev8, j, 0), rolled[0:8])
    return jnp.concatenate([top, rolled[8:]], axis=0)


def _shift_rows_up(cur, next8, j):
    if j == 0:
        return cur
    n = cur.shape[0]
    rolled = pltpu.roll(cur, n - j, 0)
    bot = jnp.where(_iota((8, cur.shape[1]), 0) >= 8 - j, pltpu.roll(next8, 8 - j, 0), rolled[n - 8:n])
    return jnp.concatenate([rolled[:n - 8], bot], axis=0)


def _softplus(x):
    return jnp.maximum(x, 0.0) + jnp.log(1.0 + jnp.exp(-jnp.abs(x)))


def _ssd_common(proj_ref, xpre, dtb_ref, alog_ref):
    xbc = xpre * _sigmoid(xpre)
    xs, bm, cm = xbc[:, 0:SW], xbc[:, SW:SW + 256], xbc[:, SW + 256:SW + 512]
    dtraw = proj_ref[:, PW - 128:PW] + dtb_ref[...]
    dt = _softplus(dtraw)
    a = -jnp.exp(alog_ref[...])
    tril = (_iota((CH, CH), 0) >= _iota((CH, CH), 1)).astype(F32)
    acs = _dot_hi(tril, dt * a)
    return xs, bm, cm, dtraw, dt, a, acs, acs.T


def _pair_lane(lo, hi):
    r = lo.shape[0]
    return jnp.where(_iota((r, 128), 1) < HP, lo, hi)


def _ssd_fwd(name, proj, conv_w, conv_b, dt_bias, a_log, d_rep, norm_w, seq):
    t = proj.shape[0]
    nb = t // seq

    def fn(ctx, tiled, halos, exs, res, outs, accs, exaccs, scr):
        @pl.when(ctx.seq_first)
        def _():
            scr[0][...] = jnp.zeros(scr[0].shape, F32)

        for b in range(nb):
            one(ctx, res, [r.at[b] for r in tiled + halos + outs + scr])

    def one(ctx, res, refs):
        proj_ref, halo_ref, yo_ref, xpre_ref, y_ref, sprev_ref, state_ref = refs
        cw_ref, cb_ref, dtb_ref, alog_ref, d_ref, nw_ref = res
        raw = proj_ref[:, SW:SW + D]
        prev8 = halo_ref[:, SW:SW + D] * jnp.where(ctx.seq_first, 0.0, 1.0)
        xpre = cb_ref[...] + cw_ref[3:4, :] * raw
        for j in (1, 2, 3):
            xpre = xpre + cw_ref[3 - j:4 - j, :] * _shift_rows(raw, prev8, j)
        xpre_ref[...] = xpre
        xs, bm, cm, dtraw, dt, a, acs, acst = _ssd_common(proj_ref, xpre, dtb_ref, alog_ref)
        causal = _iota((CH, CH), 0) >= _iota((CH, CH), 1)
        lane_lo = _iota((CH, 128), 1) < HP
        sprev = state_ref[...]
        sprev_ref[...] = sprev
        ys = []
        for g in range(NG):
            bmg, cmg = bm[:, g * NS:(g + 1) * NS], cm[:, g * NS:(g + 1) * NS]
            bmt = bmg.T
            cb = _dot(_mx(cmg), _mx(bmt))
            for q in (2 * g, 2 * g + 1):
                xsq = xs[:, 128 * q:128 * q + 128]
                xd = xsq * _pair_lane(dt[:, 2 * q:2 * q + 1], dt[:, 2 * q + 1:2 * q + 2])
                sp = sprev[:, 128 * q:128 * q + 128]
                ydiag = jnp.zeros((CH, 128), F32)
                snew = jnp.zeros((NS, 128), F32)
                for jj in range(2):
                    h = 2 * q + jj
                    col, row = acs[:, h:h + 1], acst[h:h + 1, :]
                    lm = jnp.where(causal, jnp.exp(jnp.minimum(col - row, 0.0)), 0.0)
                    xm = _mx(jnp.where(lane_lo if jj == 0 else ~lane_lo, xd, 0.0))
                    ydiag = ydiag + _dot(_mx(cb * lm), xm)
                    dec_row = jnp.exp(acst[h:h + 1, CH - 1:CH] - row)
                    snew = snew + _dot(_mx(bmt * dec_row), xm)
                e_pair = jnp.exp(_pair_lane(acs[:, 2 * q:2 * q + 1], acs[:, 2 * q + 1:2 * q + 2]))
                yoff = _dot(_mx(cmg), _mx(sp)) * e_pair
                cd = jnp.exp(_pair_lane(acst[2 * q:2 * q + 1, CH - 1:CH], acst[2 * q + 1:2 * q + 2, CH - 1:CH]))
                state_ref[:, 128 * q:128 * q + 128] = cd * sp + snew
                ys.append(ydiag + yoff + d_ref[:, 128 * q:128 * q + 128] * xsq)
        y = jnp.concatenate(ys, axis=1)
        y_ref[...] = y
        z = proj_ref[:, 0:SW]
        yz = y * (z * _sigmoid(z))
        outp = []
        for g in range(NG):
            seg = yz[:, 256 * g:256 * g + 256]
            rinv = lax.rsqrt(jnp.mean(seg * seg, axis=-1, keepdims=True) + LN_EPS)
            outp.append(seg * rinv * nw_ref[:, 256 * g:256 * g + 256])
        yo_ref[...] = _mx(jnp.concatenate(outp, axis=1))

    return _rowcall(name, fn, t, CH, seq // CH, tiled=[proj], halos=[(proj, 8)],
                    res=[conv_w, conv_b, dt_bias, a_log, d_rep, norm_w],
                    out_tiled=[_sds((t, SW), MXU_DTYPE), _sds((t, D)), _sds((t, SW)), _sds((t, SW))],
                    scratch=[pltpu.VMEM((nb, NS, SW), F32)], batch=nb)


def _ssd_bwd(name, dyo, proj, xpre_all, y_all, sprev_all, conv_w, dt_bias, a_log, d_rep, norm_w, seq, **carry):
    t = proj.shape[0]
    nb = t // seq

    def fn(ctx, tiled, halos, exs, res, outs, accs, exaccs, scr):
        @pl.when(ctx.seq_last)
        def _():
            for r in scr:
                r[...] = jnp.zeros(r.shape, F32)

        for b in range(nb):
            one(ctx, res, accs, [r.at[b] for r in tiled + halos + outs + scr])

    def one(ctx, res, accs, refs):
        dyo_ref, proj_ref, xpre_ref, y_ref, sprev_ref, halo_ref, dzx_ref, ddt_ref, ds_ref, nxt_ref = refs
        cw_ref, dtb_ref, alog_ref, d_ref, nw_ref = res
        dnw_acc, dd_acc, dcw_acc, dcb_acc, ddtb_acc, dalog_acc = accs
        xpre = xpre_ref[...]
        xs, bm, cm, dtraw, dt, a, acs, acst = _ssd_common(proj_ref, xpre, dtb_ref, alog_ref)
        y = y_ref[...]
        z = proj_ref[:, 0:SW]
        sz = _sigmoid(z)
        siluz = z * sz
        yz = y * siluz
        dyo = dyo_ref[...]
        dyz_parts, dnw_parts = [], []
        for g in range(NG):
            seg = yz[:, 256 * g:256 * g + 256]
            rinv = lax.rsqrt(jnp.mean(seg * seg, axis=-1, keepdims=True) + LN_EPS)
            yn = seg * rinv
            dseg = dyo[:, 256 * g:256 * g + 256]
            dnw_parts.append(jnp.sum(dseg * yn, axis=0, keepdims=True))
            dyn = dseg * nw_ref[:, 256 * g:256 * g + 256]
            dyz_parts.append(rinv * (dyn - yn * jnp.mean(dyn * yn, axis=-1, keepdims=True)))
        dyz = jnp.concatenate(dyz_parts, axis=1)
        _acc(dnw_acc, jnp.concatenate(dnw_parts, axis=1))
        dy = dyz * siluz
        dz = dyz * y * (sz * (1.0 + z * (1.0 - sz)))
        _acc(dd_acc, jnp.sum(dy * xs, axis=0, keepdims=True))

        causal = _iota((CH, CH), 0) >= _iota((CH, CH), 1)
        anti = _iota((CH, CH), 0) <= _iota((CH, CH), 1)
        lane_lo = _iota((CH, 128), 1) < HP
        lane_id = _iota((CH, 128), 1)
        last_row = _iota((CH, 128), 0) == CH - 1
        sprev = sprev_ref[...]
        dacs = jnp.zeros((CH, 128), F32)
        ddt_x = jnp.zeros((CH, 128), F32)
        dxs_parts, dbm_parts, dcm_parts = [], [], []
        for g in range(NG):
            bmg, cmg = bm[:, g * NS:(g + 1) * NS], cm[:, g * NS:(g + 1) * NS]
            bmt, cmt = bmg.T, cmg.T
            cb = _dot(_mx(cmg), _mx(bmt))
            cbt = _dot(_mx(bmg), _mx(cmt))
            dcb = jnp.zeros((CH, CH), F32)
            dcbt = jnp.zeros((CH, CH), F32)
            dbmg = jnp.zeros((CH, NS), F32)
            dcmg = jnp.zeros((CH, NS), F32)
            for q in (2 * g, 2 * g + 1):
                sl = slice(128 * q, 128 * q + 128)
                xsq = xs[:, sl]
                dtp = _pair_lane(dt[:, 2 * q:2 * q + 1], dt[:, 2 * q + 1:2 * q + 2])
                xd = xsq * dtp
                dyq = dy[:, sl]
                sp = sprev[:, sl]
                dsn = ds_ref[:, sl]
                e_pair = jnp.exp(_pair_lane(acs[:, 2 * q:2 * q + 1], acs[:, 2 * q + 1:2 * q + 2]))
                cd = jnp.exp(_pair_lane(acst[2 * q:2 * q + 1, CH - 1:CH], acst[2 * q + 1:2 * q + 2, CH - 1:CH]))
                dye = dyq * e_pair
                dcmg = dcmg + _dot(_mx(dye), _mx(sp.T))
                dsp = _dot(_mx(cmt), _mx(dye)) + cd * dsn
                yoff = _dot(_mx(cmg), _mx(sp)) * e_pair
                dacs_lane = dyq * yoff
                dxd = jnp.zeros((CH, 128), F32)
                sds = jnp.sum(dsn * sp, axis=0, keepdims=True) * cd
                for jj in range(2):
                    h = 2 * q + jj
                    hm = lane_lo if jj == 0 else ~lane_lo
                    col, row = acs[:, h:h + 1], acst[h:h + 1, :]
                    lm = jnp.where(causal, jnp.exp(jnp.minimum(col - row, 0.0)), 0.0)
                    lmt = jnp.where(anti, jnp.exp(jnp.minimum(row - col, 0.0)), 0.0)
                    xm = _mx(jnp.where(hm, xd, 0.0))
                    dym = _mx(jnp.where(hm, dyq, 0.0))
                    gm = _dot_nt(dym, xm)
                    gmt = _dot_nt(xm, dym)
                    dcb = dcb + gm * lm
                    dcbt = dcbt + gmt * lmt
                    dxd = dxd + _dot(_mx(cbt * lmt), dym)
                    w = gm * cb * lm
                    wt = gmt * cbt * lmt
                    dacs_h = jnp.sum(w, axis=1, keepdims=True) - jnp.sum(wt, axis=1, keepdims=True)
                    alast = acst[h:h + 1, CH - 1:CH]
                    dec_col = jnp.exp(alast - col)
                    dsm = _mx(jnp.where(hm[0:NS], dsn, 0.0))
                    qh = _dot_nt(xm, dsm)
                    dbmg = dbmg + qh * dec_col
                    ddec = jnp.sum(qh * bmg, axis=1, keepdims=True)
                    dxd = dxd + _dot(_mx(bmg * dec_col), dsm)
                    dacs_h = dacs_h - ddec * dec_col
                    dacs_h = dacs_h + jnp.sum(jnp.where(hm, dacs_lane, 0.0), axis=1, keepdims=True)
                    tail = jnp.sum(ddec * dec_col, axis=0, keepdims=True) + jnp.sum(
                        jnp.where(hm[0:1], sds, 0.0), axis=1, keepdims=True)
                    dacs = dacs + jnp.where(lane_id == h, dacs_h, 0.0) + jnp.where(
                        last_row & (lane_id == h), tail, 0.0)
                ds_ref[:, sl] = dsp
                for jj in range(2):
                    h = 2 * q + jj
                    hm = lane_lo if jj == 0 else ~lane_lo
                    ddt_x = ddt_x + jnp.where(lane_id == h, jnp.sum(jnp.where(hm, dxd * xsq, 0.0), axis=1, keepdims=True), 0.0)
                dxs_parts.append(dxd * dtp + d_ref[:, sl] * dyq)
            dcmg = dcmg + _dot(_mx(dcb), _mx(bmg))
            dbmg = dbmg + _dot(_mx(dcbt), _mx(cmg))
            dbm_parts.append(dbmg)
            dcm_parts.append(dcmg)
        triu = (_iota((CH, CH), 0) <= _iota((CH, CH), 1)).astype(F32)
        dadt = _dot_hi(triu, dacs)
        ddt = dadt * a + ddt_x
        _acc(dalog_acc, jnp.sum(dadt * dt, axis=0, keepdims=True) * a)
        ddtraw = ddt * _sigmoid(dtraw)
        ddt_ref[...] = ddtraw
        _acc(ddtb_acc, jnp.sum(ddtraw, axis=0, keepdims=True))
        dxbc = jnp.concatenate(dxs_parts + dbm_parts + dcm_parts, axis=1)
        sx = _sigmoid(xpre)
        dpre = dxbc * (sx * (1.0 + xpre * (1.0 - sx)))
        _acc(dcb_acc, jnp.sum(dpre, axis=0, keepdims=True))
        raw = proj_ref[:, SW:SW + D]
        prev8 = halo_ref[:, SW:SW + D] * jnp.where(ctx.seq_first, 0.0, 1.0)
        next8 = nxt_ref[...]
        draw = cw_ref[3:4, :] * dpre
        dcw = [None] * 4
        dcw[3] = jnp.sum(dpre * raw, axis=0, keepdims=True)
        for j in (1, 2, 3):
            dcw[3 - j] = jnp.sum(dpre * _shift_rows(raw, prev8, j), axis=0, keepdims=True)
            draw = draw + cw_ref[3 - j:4 - j, :] * _shift_rows_up(dpre, next8, j)
        _acc(dcw_acc, jnp.concatenate(dcw + [jnp.zeros((4, D), F32)], axis=0))
        nxt_ref[...] = dpre[0:8]
        dzx_ref[:, 0:SW] = dz
        dzx_ref[:, SW:SW + D] = draw

    return _rowcall(name, fn, t, CH, seq // CH, tiled=[dyo, proj, xpre_all, y_all, sprev_all], halos=[(proj, 8)],
                    res=[conv_w, dt_bias, a_log, d_rep, norm_w],
                    out_tiled=[_sds((t, SW + D)), _sds((t, 128))],
                    out_acc=[_sds((1, SW)), _sds((1, SW)), _sds((8, D)), _sds((1, D)), _sds((1, 128)), _sds((1, 128))],
                    scratch=[pltpu.VMEM((nb, NS, SW), F32), pltpu.VMEM((nb, 8, D), F32)], reverse=True, batch=nb, **carry)


def _gelu(y):
    k = math.sqrt(2.0 / math.pi)
    return 0.5 * y * (1.0 + jnp.tanh(k * (y + 0.044715 * y * y * y)))


def _gelu_grad(y):
    k = math.sqrt(2.0 / math.pi)
    th = jnp.tanh(k * (y + 0.044715 * y * y * y))
    return 0.5 * (1.0 + th) + 0.5 * y * (1.0 - th * th) * k * (1.0 + 3.0 * 0.044715 * y * y)


S5T = 256


def _cmul_add(xr, xi, ar, ai, sr, si):
    return xr + ar * sr - ai * si, xi + ar * si + ai * sr


def _s5_fwd(name, proj, bbd, cbd, pw, tab, d5, w_glu, b_glu, seq, **carry):
    t = proj.shape[0]
    tm = S5T

    def fn(ctx, tiled, halos, exs, res, outs, accs, exaccs, scr):
        proj_ref, = tiled
        bbd_ref, cbd_ref, pw_ref, tab_ref, d_ref, wg_ref, bg_ref = res
        out_ref, xst_ref, y_ref, xb_ref, ub_ref = outs
        carry_ref, = scr

        @pl.when(ctx.seq_first)
        def _():
            carry_ref[...] = jnp.zeros(carry_ref.shape, F32)

        u = proj_ref[:, 1536:2048]
        bu = _dot(_mx(u), bbd_ref[...])
        xr, xi = bu[:, :S5L].reshape(tm // 8, 8, S5L), bu[:, S5L:].reshape(tm // 8, 8, S5L)
        for k, sh in enumerate((1, 2, 4)):
            xr, xi = _cmul_add(xr, xi, pw_ref[k, :, :S5L], pw_ref[k, :, S5L:], pltpu.roll(xr, sh, 1), pltpu.roll(xi, sh, 1))
        cr, ci = carry_ref[0:1, :S5L], carry_ref[0:1, S5L:]
        for i in range(tm // 8):
            tr, ti = _cmul_add(xr[i], xi[i], tab_ref[:, :S5L], tab_ref[:, S5L:], cr, ci)
            xst_ref[8 * i:8 * i + 8, :S5L] = tr
            xst_ref[8 * i:8 * i + 8, S5L:] = ti
            cr, ci = tr[7:8], ti[7:8]
        carry_ref[0:1, :S5L] = cr
        carry_ref[0:1, S5L:] = ci
        xb = _mx(xst_ref[...])
        xb_ref[...] = xb
        ub_ref[...] = _mx(u)
        y = _dot(xb, cbd_ref[...]) + u * d_ref[...]
        y_ref[...] = y
        g = _gelu(y)
        v = _dot(_mx(g), wg_ref[...]) + bg_ref[...]
        out_ref[...] = _mx(g * _sigmoid(v))

    return _rowcall(name, fn, t, tm, seq // tm, tiled=[proj], res=[bbd, cbd, pw, tab, d5, w_glu, b_glu],
                    out_tiled=[_sds((t, SW), MXU_DTYPE), _sds((t, 2 * S5L)), _sds((t, SW)),
                               _sds((t, 2 * S5L), MXU_DTYPE), _sds((t, SW), MXU_DTYPE)],
                    scratch=[pltpu.VMEM((8, 2 * S5L), F32)], **carry)


def _s5_bwd(name, dout, proj, xst, y_all, bbdt, cbdt, pwc, tabc, d5, w_glu, b_glu, seq, **carry):
    t = proj.shape[0]
    tm = S5T

    def fn(ctx, tiled, halos, exs, res, outs, accs, exaccs, scr):
        dout_ref, proj_ref, xst_ref, y_ref = tiled
        halo_ref, = halos
        bbdt_ref, cbdt_ref, pw_ref, tab_ref, d_ref, wg_ref, bg_ref = res
        du_ref, lam_ref, dyb_ref, gb_ref, dvb_ref = outs
        da_acc, dd_acc, dbg_acc = accs
        carry_ref, lamf_ref = scr

        @pl.when(ctx.seq_last)
        def _():
            carry_ref[...] = jnp.zeros(carry_ref.shape, F32)

        u = proj_ref[:, 1536:2048]
        y = y_ref[...]
        g = _gelu(y)
        v = _dot(_mx(g), wg_ref[...]) + bg_ref[...]
        sg = _sigmoid(v)
        dout = dout_ref[...]
        dv = dout * g * sg * (1.0 - sg)
        dvb = _mx(dv)
        dvb_ref[...] = dvb
        gb_ref[...] = _mx(g)
        _acc(dbg_acc, jnp.sum(dv, axis=0, keepdims=True))
        dg = dout * sg + _dot_nt(dvb, wg_ref[...])
        dy = dg * _gelu_grad(y)
        dyb = _mx(dy)
        dyb_ref[...] = dyb
        _acc(dd_acc, jnp.sum(dy * u, axis=0, keepdims=True))
        dx = _dot(dyb, cbdt_ref[...])
        xr, xi = dx[:, :S5L].reshape(tm // 8, 8, S5L), dx[:, S5L:].reshape(tm // 8, 8, S5L)
        for k, sh in enumerate((1, 2, 4)):
            xr, xi = _cmul_add(xr, xi, pw_ref[k, :, :S5L], pw_ref[k, :, S5L:], pltpu.roll(xr, 8 - sh, 1),
                               pltpu.roll(xi, 8 - sh, 1))
        cr, ci = carry_ref[0:1, :S5L], carry_ref[0:1, S5L:]
        for i in reversed(range(tm // 8)):
            tr, ti = _cmul_add(xr[i], xi[i], tab_ref[:, :S5L], tab_ref[:, S5L:], cr, ci)
            lamf_ref[8 * i:8 * i + 8, :S5L] = tr
            lamf_ref[8 * i:8 * i + 8, S5L:] = ti
            cr, ci = tr[0:1], ti[0:1]
        carry_ref[0:1, :S5L] = cr
        carry_ref[0:1, S5L:] = ci
        lam = lamf_ref[...]
        lamb = _mx(lam)
        lam_ref[...] = lamb
        du_ref[...] = dy * d_ref[...] + _dot(lamb, bbdt_ref[...])
        prev8 = halo_ref[...] * jnp.where(ctx.seq_first, 0.0, 1.0)
        xprev = _shift_rows(xst_ref[...], prev8, 1)
        lr, li = lam[:, :S5L], lam[:, S5L:]
        pr, pi = xprev[:, :S5L], xprev[:, S5L:]
        dar = jnp.sum(lr * pr + li * pi, axis=0, keepdims=True)
        dai = jnp.sum(li * pr - lr * pi, axis=0, keepdims=True)
        _acc(da_acc, jnp.concatenate([dar, dai], axis=1))

    return _rowcall(name, fn, t, tm, seq // tm, tiled=[dout, proj, xst, y_all], halos=[(xst, 8)],
                    res=[bbdt, cbdt, pwc, tabc, d5, w_glu, b_glu],
                    out_tiled=[_sds((t, SW)), _sds((t, 2 * S5L), MXU_DTYPE), _sds((t, SW), MXU_DTYPE),
                               _sds((t, SW), MXU_DTYPE), _sds((t, SW), MXU_DTYPE)],
                    out_acc=[_sds((1, 2 * S5L)), _sds((1, SW)), _sds((1, SW))],
                    scratch=[pltpu.VMEM((8, 2 * S5L), F32), pltpu.VMEM((tm, 2 * S5L), F32)], reverse=True, **carry)


def _out_fwd(name, yssd, ys5, x1, mod, w_out, lng, lnb, seq):
    t = x1.shape[0]
    tm = min(TM_WIDE, seq)

    def fn(ctx, tiled, halos, exs, res, outs, accs, exaccs, scr):
        ya_ref, yb_ref, x_ref = tiled
        mod_ref, = exs
        w_ref, g_ref, b_ref = res
        m = _dot(ya_ref[...], w_ref[0:SW, :]) + _dot(yb_ref[...], w_ref[SW:2 * SW, :])
        r = ALPHA * x_ref[...] + mod_ref[0, 5:6, :] * m
        outs[0][...] = _ln_fwd(r, g_ref[...], b_ref[...])
        outs[1][...] = r
        outs[2][...] = m

    return _rowcall(name, fn, t, tm, seq // tm, tiled=[yssd, ys5, x1], exs=[mod], res=[w_out, lng, lnb],
                    out_tiled=[_sds((t, D)), _sds((t, D)), _sds((t, D))])


def _out_bwd(name, dxo, r, m, mod, lng, w_out, seq, **carry):
    t = dxo.shape[0]
    tm = min(TM_WIDE, seq)

    def fn(ctx, tiled, halos, exs, res, outs, accs, exaccs, scr):
        dxo_ref, r_ref, m_ref = tiled
        mod_ref, = exs
        g_ref, w_ref = res
        dr, dgam, dbet = _ln_bwd(r_ref[...], g_ref[...], dxo_ref[...])
        outs[0][...] = dr
        _acc(accs[0], dgam)
        _acc(accs[1], dbet)
        _acc(exaccs[0].at[0], jnp.sum(dr * m_ref[...], axis=0, keepdims=True))
        dm = _mx(mod_ref[0, 5:6, :] * dr)
        outs[1][...] = dm
        dyc = _dot_nt(dm, w_ref[...])
        outs[2][...] = dyc[:, 0:SW]
        outs[3][...] = dyc[:, SW:2 * SW]

    b = mod.shape[0]
    return _rowcall(name, fn, t, tm, seq // tm, tiled=[dxo, r, m], exs=[mod], res=[lng, w_out],
                    out_tiled=[_sds((t, D)), _sds((t, D), MXU_DTYPE), _sds((t, SW)), _sds((t, SW))],
                    out_acc=[_sds((1, D)), _sds((1, D))], out_exacc=[_sds((b, 1, D))], **carry)


def _s5_discretise(a_re, a_im, log_dt, b_re, b_im):
    dt = jnp.exp(log_dt)[:, None]
    mag = jnp.exp(dt * a_re)
    ab_re, ab_im = mag * jnp.cos(dt * a_im), mag * jnp.sin(dt * a_im)
    den = a_re * a_re + a_im * a_im
    nr, ni = ab_re - 1.0, ab_im
    f_re, f_im = (nr * a_re + ni * a_im) / den, (ni * a_re - nr * a_im) / den
    bb_re = f_re[..., None] * b_re - f_im[..., None] * b_im
    bb_im = f_re[..., None] * b_im + f_im[..., None] * b_re
    return ab_re, ab_im, bb_re, bb_im


def _s5_tables(ab_re, ab_im):
    ar, ai = ab_re.reshape(1, S5L), ab_im.reshape(1, S5L)
    pows = [(ar, ai)]
    for _ in range(7):
        pr, pi = pows[-1]
        pows.append((pr * ar - pi * ai, pr * ai + pi * ar))

    def pack(rows, sign):
        return jnp.concatenate([jnp.concatenate([r for r, _ in rows], axis=0),
                                jnp.concatenate([sign * i for _, i in rows], axis=0)], axis=1)

    row = jnp.arange(8)[:, None]
    pw = jnp.stack([jnp.where(row >= sh, pack([pows[sh - 1]], 1.0), 0.0) for sh in (1, 2, 4)])
    pwc = jnp.stack([jnp.where(row < 8 - sh, pack([pows[sh - 1]], -1.0), 0.0) for sh in (1, 2, 4)])
    tab = pack(pows, 1.0)
    tabc = pack(pows[::-1], -1.0)
    return pw, tab, pwc, tabc


class _GradGroup:
    def __init__(self, tag, grads, place):
        self.tag, self.names, (self.half, self.chip) = tag, list(grads), place
        self.gsh = [_shard_major(g) if n in COL_SHARDED else g.reshape(4, g.shape[0] // 4, g.shape[1])
                    for n, g in grads.items()]

    def sibling(self):
        return _rs_sibling_stage(self.gsh)

    def chips(self, received):
        self.from_sibling = received
        return _rs_chips_stage(_rs_add(self.tag + "_add", self.gsh, received, self.half))

    def join(self, received):
        return _rs_join_stage(_rs_sum(self.tag + "_sum", self.gsh, self.from_sibling, received, self.chip, self.half))

    def result(self, joined):
        return dict(zip(self.names, joined))


def _hid(fn, *args, stage=None, **kw):
    if stage is None:
        return fn(*args, **kw), None
    return fn(*args, carry=stage, **kw)


def _local_step(x, tgt, mod, w, sp, seq, dist=None):
    t = x.shape[0]
    mxu = MXU_DTYPE
    big = {}

    def group(tag, grads):
        if dist is None:
            big.update(grads)
            return None
        return _GradGroup(tag, grads, dist[2])

    mid = {"carry_mid": 0.6} if dist else {}
    (h1, ab1, s1), got = _hid(_ffn_fwd_hidden, "ffn1_fwd_a", x, mod, 0, w["ffn1_w1t"], w["ffn1_w3t"], seq,
                              stage=dist and dist[0][0], **mid)
    if dist:
        w = {**w, **dist[0][1](got)}
    x1, r1, f1 = _ffn_fwd_out("ffn1_fwd_b", s1, x, mod, 0, w["ffn1_w2"], sp["ln1_g"], sp["ln1_b"], seq)
    h2, proj = _proj_fwd("proj_fwd", x1, mod, w["w_in"], seq)
    yssd, xpre, yraw, sprev = _ssd_fwd("ssd_fwd", proj, sp["conv_w"], sp["conv_b"], sp["dt_bias"], sp["a_log"],
                                       sp["d_rep"], sp["ssd_norm_w"], seq)
    (ab_re, ab_im, bb_re, bb_im), disc_vjp = jax.vjp(_s5_discretise, sp["s5_a_re"], sp["s5_a_im"], sp["s5_log_dt"],
                                                     sp["s5_b_re"], sp["s5_b_im"])
    eye = jnp.eye(S5G, dtype=F32)
    bbd = jnp.concatenate([jnp.einsum("gk,gph->ghkp", eye, bb_re).reshape(SW, S5L),
                           jnp.einsum("gk,gph->ghkp", eye, bb_im).reshape(SW, S5L)], axis=1).astype(mxu)
    cbd = jnp.concatenate([jnp.einsum("gk,ghp->gpkh", eye, sp["s5_c_re"]).reshape(S5L, SW),
                           -jnp.einsum("gk,ghp->gpkh", eye, sp["s5_c_im"]).reshape(S5L, SW)], axis=0).astype(mxu)
    bbdt = jnp.concatenate([jnp.einsum("gk,gph->kpgh", eye, bb_re).reshape(S5L, SW),
                            jnp.einsum("gk,gph->kpgh", eye, bb_im).reshape(S5L, SW)], axis=0).astype(mxu)
    cbdt = jnp.concatenate([jnp.einsum("gk,ghp->khgp", eye, sp["s5_c_re"]).reshape(SW, S5L),
                            -jnp.einsum("gk,ghp->khgp", eye, sp["s5_c_im"]).reshape(SW, S5L)], axis=1).astype(mxu)
    pw, tab, pwc, tabc = _s5_tables(lax.stop_gradient(ab_re), lax.stop_gradient(ab_im))
    (ys5, xst, y5, xstb, ub), got = _hid(_s5_fwd, "s5_fwd", proj, bbd, cbd, pw, tab, sp["s5_d"], w["w_glu"], sp["b_glu"],
                                         seq, stage=dist and dist[1][0], **mid)
    if dist:
        w = {**w, **dist[1][1](got)}
    x2, r2, m2 = _out_fwd("out_fwd", yssd, ys5, x1, mod, w["w_out"], sp["ln2_g"], sp["ln2_b"], seq)
    h3, ab3, s3 = _ffn_fwd_hidden("ffn2_fwd_a", x2, mod, 6, w["ffn2_w1t"], w["ffn2_w3t"], seq)
    x3, r3, f3, dy3, loss_acc = _ffn_fwd_out("ffn2_fwd_b", s3, x2, mod, 6, w["ffn2_w2"], sp["ln3_g"], sp["ln3_b"], seq, tgt=tgt)
    dx2, df3, dab3, dg3g, dg3b, dgate3, dsh3, dsc3 = _ffn_bwd("ffn2_bwd", dy3, r3, ab3, f3, x2, mod, 6, sp["ln3_g"],
                                                              w["ffn2_w2"], w["ffn2_w1t"], w["ffn2_w3t"], seq)
    grp = group("rs_ffn2", dict(ffn2_w1=_tn_matmul("ffn2_dw1", dab3, h3, FB, D, a_cols=(0, FF)),
                                ffn2_w3=_tn_matmul("ffn2_dw3", dab3, h3, FB, D, a_cols=(FF, FF)),
                                ffn2_w2=_tn_matmul("ffn2_dw2", s3, df3, FB, D)))
    (dr2, dm2, dyssd, dys5, dg2g, dg2b, dgate2), got = _hid(_out_bwd, "out_bwd", dx2, r2, m2, mod, sp["ln2_g"],
                                                          w["w_out"], seq, stage=grp and grp.sibling())
    g_w_out = jnp.concatenate([_tn_matmul("dw_out_a", yssd, dm2, SW, D), _tn_matmul("dw_out_b", ys5, dm2, SW, D)], axis=0)
    (du, lam, dy5b, g5b, dv5b, da5, dd5, dbglu), got = _hid(
        _s5_bwd, "s5_bwd", dys5, proj, xst, y5, bbdt, cbdt, pwc, tabc, sp["s5_d"], w["w_glu"], sp["b_glu"],
        seq, stage=grp and grp.chips(got))
    g_w_glu = _tn_matmul("dw_glu", g5b, dv5b, SW, SW)
    dbb = _diag_groups(_tn_diag("s5_db", lam, ub))
    dcc = _diag_groups(_tn_diag("s5_dc", xstb, dy5b))
    (dzx, ddt, dnw, ddl, dcw, dcb, ddtb, dalog), got = _hid(
        _ssd_bwd, "ssd_bwd", dyssd, proj, xpre, yraw, sprev, sp["conv_w"], sp["dt_bias"], sp["a_log"], sp["d_rep"],
        sp["ssd_norm_w"], seq, stage=grp and grp.join(got))
    if grp:
        big.update(grp.result(got))
    dx1, dproj, dsh2, dsc2 = _mod_bwd("proj_bwd", dzx, dr2, x1, mod, 3, [w["w_in"]], seq, extra=(du, ddt), nt=True)
    gwi = _tn_matmul("dw_in", h2, dproj, D, PW)
    grp = group("rs_mix", dict(w_in=jnp.concatenate([gwi[:, :1536], gwi[:, 2048:2056], gwi[:, 1536:2048]], axis=1),
                               w_glu=g_w_glu, w_out=g_w_out))
    (dr1, df1, dab1, dg1g, dg1b, dgate1), got = _hid(_ffn_bwd1, "ffn1_bwd1", dx1, r1, ab1, f1, mod, 0, sp["ln1_g"],
                                                          w["ffn1_w2"], seq, stage=grp and grp.sibling())
    g1, got = _hid(_tn_matmul, "ffn1_dw1", dab1, h1, FB, D, a_cols=(0, FF), stage=grp and grp.chips(got))
    g3, got = _hid(_tn_matmul, "ffn1_dw3", dab1, h1, FB, D, a_cols=(FF, FF), stage=grp and grp.join(got))
    if grp:
        big.update(grp.result(got))
    grp = group("rs_ffn1a", dict(ffn1_w1=g1, ffn1_w3=g3))
    g2, got = _hid(_tn_matmul, "ffn1_dw2", s1, df1, FB, D, stage=grp and grp.sibling())
    grp2 = group("rs_ffn1b", dict(ffn1_w2=g2))
    (dx0, dsh1, dsc1), got = _hid(_mod_bwd, "ffn1_bwd2", dab1, dr1, x, mod, 0, [w["ffn1_w1t"], w["ffn1_w3t"]], seq,
                                  stage=grp and _merge_stages(grp.chips(got), grp2.sibling()))
    if grp:
        n = len(grp.names)
        got = _run_stage("rs_ffn1_tail", _merge_stages(grp.join(got[:n]), grp2.chips(got[n:])))
        big.update(grp.result(got[:n]))
        big.update(grp2.result(_run_stage("rs_ffn1b_join", grp2.join(got[n:]))))
    dmod = jnp.concatenate([dsh1, dsc1, dgate1, dsh2, dsc2, dgate2, dsh3, dsc3, dgate3], axis=1)
    dc_re, dc_im = dcc[0].transpose(0, 2, 1), -dcc[1].transpose(0, 2, 1)
    g_a_re, g_a_im, g_log_dt, g_b_re, g_b_im = disc_vjp(
        (da5[:, :S5L].reshape(S5G, S5P), da5[:, S5L:].reshape(S5G, S5P), dbb[0], dbb[1]))
    small = dict(ln1_g=dg1g, ln1_b=dg1b, ln2_g=dg2g, ln2_b=dg2b, ln3_g=dg3g, ln3_b=dg3b, conv_w=dcw[0:4], conv_b=dcb,
                 dt_bias=ddtb[:, :NH], a_log=dalog[:, :NH], d_ssd=jnp.sum(ddl.reshape(NH, HP), axis=1).reshape(1, NH),
                 ssd_norm_w=dnw, s5_a_re=g_a_re, s5_a_im=g_a_im, s5_log_dt=g_log_dt, s5_b_re=g_b_re, s5_b_im=g_b_im,
                 s5_c_re=dc_re, s5_c_im=dc_im, s5_d=dd5, w_glu_b=dbglu)
    return loss_acc[0, 0], dx0, dmod, big, small


def _place():
    return lax.axis_index("x"), lax.axis_index("y"), lax.axis_index("c")


def _other_chips(x, y):
    return [(1 - x, y), (x, 1 - y), (1 - x, 1 - y)]


def _allgather8(name, a):
    r, n = a.shape

    def body(x_ref, out_ref, send_sems, recv_sems, local_sem):
        x, y, c = _place()
        me, sibling = (x, y, c), (x, y, 1 - c)
        chips = _other_chips(x, y)

        def rows(px, py, pc):
            return out_ref.at[pl.ds(pl.multiple_of((4 * px + 2 * py + pc) * r, 8), r), :]

        def copy(k, block, to, src=None):
            return pltpu.make_async_remote_copy(src_ref=rows(*block) if src is None else src, dst_ref=rows(*block),
                                                send_sem=send_sems.at[k], recv_sem=recv_sems.at[k], device_id=to,
                                                device_id_type=MESH_T)

        mine = pltpu.make_async_copy(x_ref, rows(*me), local_sem)
        mine.start()
        first = [copy(0, me, sibling, src=x_ref)]
        first += [copy(1 + j, me, (*chip, c), src=x_ref) for j, chip in enumerate(chips)]
        for cp in first:
            cp.start()
        passed = [copy(4 + j, (*chip, c), sibling) for j, chip in enumerate(chips)]
        for j, chip in enumerate(chips):
            copy(1 + j, (*chip, c), me).wait_recv()
            passed[j].start()
        copy(0, sibling, me).wait_recv()
        for j, chip in enumerate(chips):
            copy(4 + j, (*chip, 1 - c), me).wait_recv()
        for cp in first + passed:
            cp.wait_send()
        mine.wait()

    out = _pcall(body, name=name, out_shape=_sds((8 * r, n), a.dtype),
                 in_specs=[pl.BlockSpec(memory_space=pltpu.VMEM)], out_specs=pl.BlockSpec(memory_space=pltpu.VMEM),
                 scratch_shapes=[pltpu.SemaphoreType.DMA((7,)), pltpu.SemaphoreType.DMA((7,)), pltpu.SemaphoreType.DMA],
                 compiler_params=_cparams())(a)
    return out.reshape(8, r, n)


class _Stage:
    def __init__(self, ins, out_shapes, n_sems, start, finish, mid=None, aliases=None):
        self.ins, self.out_shapes, self.n_sems = list(ins), list(out_shapes), tuple(n_sems)
        self.start, self.mid, self.finish = start, mid, finish
        self.aliases = dict(aliases or {})

    def sem_shapes(self):
        return [pltpu.SemaphoreType.DMA((n,)) for n in self.n_sems]

    def run(self, refs, at_start=None, at_mid=None, at_finish=None, between=None):
        if between is None:
            for part in (self.start, self.mid, self.finish):
                if part is not None:
                    part(*refs)
            return
        pl.when(at_start)(functools.partial(self.start, *refs))
        if self.mid is not None:
            pl.when(at_mid)(functools.partial(self.mid, *refs))
        between()
        pl.when(at_finish)(functools.partial(self.finish, *refs))


def _merge_stages(a, b):
    n_in, n_out, n_sem = len(a.ins), len(a.out_shapes), len(a.n_sems)

    def both(fa, fb):
        def run(ins, outs, sems):
            fa(ins[:n_in], outs[:n_out], sems[:n_sem])
            fb(ins[n_in:], outs[n_out:], sems[n_sem:])
        return run

    aliases = {**a.aliases, **{n_in + k: n_out + v for k, v in b.aliases.items()}}
    return _Stage(a.ins + b.ins, a.out_shapes + b.out_shapes, a.n_sems + b.n_sems, both(a.start, b.start),
                  both(a.finish, b.finish), aliases=aliases)


def _run_stage(name, st):
    n_in, n_out = len(st.ins), len(st.out_shapes)

    def body(*refs):
        st.run((refs[:n_in], refs[n_in:n_in + n_out], refs[n_in + n_out:]))

    any_spec = pl.BlockSpec(memory_space=pl.ANY)
    return _pcall(body, name=name, out_shape=st.out_shapes, in_specs=[any_spec] * n_in, out_specs=[any_spec] * n_out,
                  input_output_aliases=st.aliases, scratch_shapes=st.sem_shapes(), compiler_params=_cparams())(*st.ins)


def _rows(ref_rows, half, align):
    hr = ref_rows // 2
    return pl.ds(pl.multiple_of(half * hr, align), hr)


def _gather_stage(shards):
    n = len(shards)
    pairs = [(i, j) for i in range(n) for j in range(3)]
    align = [32 // s.dtype.itemsize for s in shards]

    def env(ins, outs, sems):
        x, y, c = _place()
        chips = _other_chips(x, y)

        def copy(i, k, chip, half, to, src=None):
            dst = outs[i].at[2 * chip[0] + chip[1], _rows(shards[i].shape[0], half, align[i])]
            return pltpu.make_async_remote_copy(src_ref=dst if src is None else src, dst_ref=dst,
                                                send_sem=sems[0].at[6 * i + k], recv_sem=sems[1].at[6 * i + k],
                                                device_id=to, device_id_type=MESH_T)

        def first(i, j):
            return copy(i, j, (x, y), c, (*chips[j], c), src=ins[i].at[_rows(shards[i].shape[0], c, align[i])])

        def passed(i, j, half):
            return copy(i, 3 + j, chips[j], half, (x, y, 1 - c))

        def landed(i, j):
            return copy(i, j, chips[j], c, (x, y, 1 - c))

        return c, first, passed, landed

    def start(ins, outs, sems):
        c, first, passed, landed = env(ins, outs, sems)
        for i, j in pairs:
            first(i, j).start()

    def mid(ins, outs, sems):
        c, first, passed, landed = env(ins, outs, sems)
        for i, j in pairs:
            landed(i, j).wait_recv()
            passed(i, j, c).start()

    def finish(ins, outs, sems):
        c, first, passed, landed = env(ins, outs, sems)
        for i, j in pairs:
            passed(i, j, 1 - c).wait_recv()
        for i, j in pairs:
            first(i, j).wait_send()
            passed(i, j, c).wait_send()

    return _Stage(shards, [_sds((4,) + s.shape, s.dtype) for s in shards], (6 * n, 6 * n), start, finish, mid)


def _rs_sibling_stage(gs):
    n = len(gs)

    def copies(ins, outs, sems):
        x, y, c = _place()
        return [pltpu.make_async_remote_copy(src_ref=ins[i].at[:, _rows(gs[i].shape[1], 1 - c, 8)], dst_ref=outs[i],
                                             send_sem=sems[0].at[i], recv_sem=sems[1].at[i], device_id=(x, y, 1 - c),
                                             device_id_type=MESH_T) for i in range(n)]

    def start(*refs):
        for cp in copies(*refs):
            cp.start()

    def finish(*refs):
        for cp in copies(*refs):
            cp.wait()

    return _Stage(gs, [_sds((4, g.shape[1] // 2, g.shape[2]), g.dtype) for g in gs], (n, n), start, finish)


def _rs_chips_stage(hs):
    n = len(hs)

    def copies(ins, outs, sems):
        x, y, c = _place()
        return [pltpu.make_async_remote_copy(src_ref=ins[i].at[2 * chip[0] + chip[1]], dst_ref=outs[i].at[j],
                                             send_sem=sems[0].at[3 * i + j], recv_sem=sems[1].at[3 * i + j],
                                             device_id=(*chip, c), device_id_type=MESH_T)
                for i in range(n) for j, chip in enumerate(_other_chips(x, y))]

    def start(*refs):
        for cp in copies(*refs):
            cp.start()

    def finish(*refs):
        for cp in copies(*refs):
            cp.wait()

    return _Stage(hs, [_sds((3,) + h.shape[1:], h.dtype) for h in hs], (3 * n, 3 * n), start, finish)


def _rs_join_stage(fs):
    n = len(fs)

    def copy(outs, sems, i, half):
        x, y, c = _place()
        part = outs[i].at[_rows(fs[i].shape[0], c if half == "mine" else 1 - c, 8)]
        return pltpu.make_async_remote_copy(src_ref=part, dst_ref=part, send_sem=sems[0].at[i], recv_sem=sems[1].at[i],
                                            device_id=(x, y, 1 - c), device_id_type=MESH_T)

    def start(ins, outs, sems):
        for i in range(n):
            copy(outs, sems, i, "mine").start()

    def finish(ins, outs, sems):
        for i in range(n):
            copy(outs, sems, i, "theirs").wait_recv()
        for i in range(n):
            copy(outs, sems, i, "mine").wait_send()

    return _Stage(fs, [_sds(f.shape, f.dtype) for f in fs], (n, n), start, finish, aliases={i: i for i in range(n)})


def _row_block(r, cap=2048):
    b = min(r, cap)
    while r % b or b % 8:
        b -= 8
    return b


RS_SPLIT = 2


def _rs_add(name, gs, r1s, sel):
    n = len(gs)

    def body(sel_ref, *refs):
        g_refs, r_refs, b_refs = (refs[k * n:(k + 1) * n] for k in range(3))
        for i in range(n):
            b_refs[i][...] = (g_refs[i][...] + r_refs[i][...]).astype(BF16)

    def blk(g):
        return (1, g.shape[1] // 2 // RS_SPLIT, g.shape[2])

    here = lambda k, j, s: (k, j, 0)
    in_specs = [pl.BlockSpec(blk(g), lambda k, j, s: (k, s[0] * RS_SPLIT + j, 0)) for g in gs]
    in_specs += [pl.BlockSpec(blk(g), here) for g in gs]
    return _pcall(body, name=name, out_shape=[_sds((4, g.shape[1] // 2, g.shape[2]), BF16) for g in gs],
                  grid_spec=pltpu.PrefetchScalarGridSpec(num_scalar_prefetch=1, grid=(4, RS_SPLIT), in_specs=in_specs,
                                                         out_specs=[pl.BlockSpec(blk(g), here) for g in gs]),
                  compiler_params=_cparams(("parallel", "parallel")))(sel.reshape(1).astype(jnp.int32), *gs, *r1s)


def _rs_sum(name, gs, r1s, r2s, chip, half):
    n = len(gs)

    def body(sel_ref, *refs):
        g_refs, h_refs, r_refs, o_refs = (refs[k * n:(k + 1) * n] for k in range(4))
        for i in range(n):
            r = r_refs[i]
            o_refs[i][...] = (((g_refs[i][0] + h_refs[i][0]) + r[0].astype(F32)) + r[1].astype(F32)) + r[2].astype(F32)

    def rows(g):
        return g.shape[1] // 2 // RS_SPLIT

    in_specs = [pl.BlockSpec((1, rows(g), g.shape[2]), lambda j, s: (s[0], s[1] * RS_SPLIT + j, 0)) for g in gs]
    in_specs += [pl.BlockSpec((1, rows(g), g.shape[2]), lambda j, s: (s[0], j, 0)) for g in gs]
    in_specs += [pl.BlockSpec((3, rows(g), g.shape[2]), lambda j, s: (0, j, 0)) for g in gs]
    sel = jnp.stack([chip, half]).astype(jnp.int32)
    return _pcall(body, name=name, out_shape=[_sds(g.shape[1:]) for g in gs],
                  grid_spec=pltpu.PrefetchScalarGridSpec(
                      num_scalar_prefetch=1, grid=(RS_SPLIT,), in_specs=in_specs,
                      out_specs=[pl.BlockSpec((rows(g), g.shape[2]), lambda j, s: (s[1] * RS_SPLIT + j, 0)) for g in gs]),
                  compiler_params=_cparams(("parallel",)))(sel, *gs, *r1s, *r2s)


def _sum8(name, a):
    _, r, n = a.shape
    br = _row_block(r)

    def body(a_ref, o_ref):
        acc = a_ref[0]
        for k in range(1, 8):
            acc = acc + a_ref[k]
        o_ref[...] = acc

    return _pcall(body, name=name, out_shape=_sds((r, n)), grid=(r // br,),
                  in_specs=[pl.BlockSpec((8, br, n), lambda j: (0, j, 0))], out_specs=pl.BlockSpec((br, n), lambda j: (j, 0)),
                  compiler_params=_cparams(("parallel",)))(a)


def _adamw(name, ws, gs, ms, vs, nblk):
    n = len(ws)

    def body(*refs):
        w_refs, g_refs, m_refs, v_refs, d_refs, nm_refs, nv_refs = (refs[k * n:(k + 1) * n] for k in range(7))
        for i in range(n):
            gv = g_refs[i][...]
            nm = ADAM_B1 * m_refs[i][...] + (1.0 - ADAM_B1) * gv
            nv = ADAM_B2 * v_refs[i][...] + (1.0 - ADAM_B2) * (gv * gv)
            nm_refs[i][...] = nm
            nv_refs[i][...] = nv
            m_hat = nm / (1.0 - ADAM_B1 ** ADAM_STEP)
            v_hat = nv / (1.0 - ADAM_B2 ** ADAM_STEP)
            d_refs[i][...] = -ADAM_LR * (m_hat / (jnp.sqrt(v_hat) + ADAM_EPS) + ADAM_WD * w_refs[i][...])

    specs = [pl.BlockSpec((w.shape[0] // nblk, w.shape[1]), lambda j: (j, 0)) for w in ws]
    outs = _pcall(body, name=name, out_shape=[_sds(w.shape) for w in ws] * 3, grid=(nblk,), in_specs=specs * 4,
                  out_specs=specs * 3, compiler_params=_cparams(("parallel",)))(*ws, *gs, *ms, *vs)
    return outs[:n], outs[n:2 * n], outs[2 * n:]


ADA_COLS = 2304
ADA_BLK = 768


def _ada_fwd(name, c_all, w_shard, b_cols):
    nb = c_all.shape[0]

    def body(c_ref, w_ref, b_ref, o_ref):
        cv = c_ref[...]
        cs = _mx(cv * _sigmoid(cv))
        o_ref[...] = _dot(cs, _mx(w_ref[...])) + b_ref[...]

    return _pcall(body, name=name, out_shape=_sds((nb, ADA_COLS)), grid=(ADA_COLS // ADA_BLK,),
                  in_specs=[pl.BlockSpec((nb, D), lambda j: (0, 0)), pl.BlockSpec((D, ADA_BLK), lambda j: (0, j)),
                            pl.BlockSpec((1, ADA_BLK), lambda j: (0, j))],
                  out_specs=pl.BlockSpec((nb, ADA_BLK), lambda j: (0, j)),
                  compiler_params=_cparams(("parallel",)))(c_all, w_shard, b_cols)


def _ada_bwd(name, c_all, dmod_cols, dmod_all):
    nb = c_all.shape[0]

    def body(c_ref, dc_ref, da_ref, gw_ref, gb_ref):
        cv = c_ref[...]
        cs = _mx(cv * _sigmoid(cv))
        gw_ref[...] = lax.dot_general(cs, _mx(dc_ref[...]), (((0,), (0,)), ((), ())), preferred_element_type=F32)

        @pl.when(pl.program_id(0) == 0)
        def _():
            gb_ref[...] = jnp.sum(da_ref[...], axis=0, keepdims=True)

    return _pcall(body, name=name, out_shape=[_sds((D, ADA_COLS)), _sds((1, 9 * D))], grid=(ADA_COLS // ADA_BLK,),
                  in_specs=[pl.BlockSpec((nb, D), lambda j: (0, 0)), pl.BlockSpec((nb, ADA_BLK), lambda j: (0, j)),
                            pl.BlockSpec((nb, 9 * D), lambda j: (0, 0))],
                  out_specs=[pl.BlockSpec((D, ADA_BLK), lambda j: (0, j)), pl.BlockSpec((1, 9 * D), lambda j: (0, 0))],
                  compiler_params=_cparams(("arbitrary",)))(c_all, dmod_cols, dmod_all)


BIG = ("ffn1_w1", "ffn1_w3", "ffn1_w2", "w_in", "w_glu", "w_out", "ffn2_w1", "ffn2_w3", "ffn2_w2")
COL_SHARDED = ("w_in",)
TRANSPOSED = ("ffn1_w1", "ffn1_w3", "ffn2_w1", "ffn2_w3")
SMALL = ("b_ada", "ln1_g", "ln1_b", "conv_w", "conv_b", "dt_bias", "a_log", "d_ssd", "ssd_norm_w", "s5_a_re", "s5_a_im",
         "s5_log_dt", "s5_b_re", "s5_b_im", "s5_c_re", "s5_c_im", "s5_d", "b_glu", "ln2_g", "ln2_b", "ln3_g", "ln3_b")
WEIGHTS = ("w_ada", "b_ada", "ffn1_w1", "ffn1_w3", "ffn1_w2", "ln1_g", "ln1_b", "w_in", "conv_w", "conv_b", "dt_bias",
           "a_log", "d_ssd", "ssd_norm_w", "s5_a_re", "s5_a_im", "s5_log_dt", "s5_b_re", "s5_b_im", "s5_c_re", "s5_c_im",
           "s5_d", "w_glu", "b_glu", "w_out", "ln2_g", "ln2_b", "ffn2_w1", "ffn2_w3", "ffn2_w2", "ln3_g", "ln3_b")
BIG_PAD = 2 * 1024 * 128


def _pack(arrs, mult, axis_keep=0):
    lead = arrs[0].shape[:axis_keep]
    flat = jnp.concatenate([a.reshape(lead + (-1,)) for a in arrs], axis=axis_keep)
    pad = (-flat.shape[-1]) % mult
    if pad:
        flat = jnp.concatenate([flat, jnp.zeros(lead + (pad,), flat.dtype)], axis=axis_keep)
    return flat


def _unpack(flat, shapes):
    out, off = [], 0
    for s in shapes:
        size = math.prod(s)
        out.append(flat[..., off:off + size].reshape(flat.shape[:-1] + tuple(s)))
        off += size
    return out


def _shard_major(a):
    rows, cols = a.shape
    return a.reshape(rows, 4, cols // 4).transpose(1, 0, 2)


def _from_shard_major(a):
    _, rows, w = a.shape
    return a.transpose(1, 0, 2).reshape(rows, 4 * w)


def kernel(x, c, w_ada, b_ada, ffn1_w1, ffn1_w3, ffn1_w2, ln1_g, ln1_b, w_in, conv_w, conv_b, dt_bias, a_log, d_ssd, ssd_norm_w, s5_a_re, s5_a_im, s5_log_dt, s5_b_re, s5_b_im, s5_c_re, s5_c_im, s5_d, w_glu, b_glu, w_out, ln2_g, ln2_b, ffn2_w1, ffn2_w3, ffn2_w2, ln3_g, ln3_b, loss_target, m_w_ada, m_b_ada, m_ffn1_w1, m_ffn1_w3, m_ffn1_w2, m_ln1_g, m_ln1_b, m_w_in, m_conv_w, m_conv_b, m_dt_bias, m_a_log, m_d_ssd, m_ssd_norm_w, m_s5_a_re, m_s5_a_im, m_s5_log_dt, m_s5_b_re, m_s5_b_im, m_s5_c_re, m_s5_c_im, m_s5_d, m_w_glu, m_b_glu, m_w_out, m_ln2_g, m_ln2_b, m_ffn2_w1, m_ffn2_w3, m_ffn2_w2, m_ln3_g, m_ln3_b, v_w_ada, v_b_ada, v_ffn1_w1, v_ffn1_w3, v_ffn1_w2, v_ln1_g, v_ln1_b, v_w_in, v_conv_w, v_conv_b, v_dt_bias, v_a_log, v_d_ssd, v_ssd_norm_w, v_s5_a_re, v_s5_a_im, v_s5_log_dt, v_s5_b_re, v_s5_b_im, v_s5_c_re, v_s5_c_im, v_s5_d, v_w_glu, v_b_glu, v_w_out, v_ln2_g, v_ln2_b, v_ffn2_w1, v_ffn2_w3, v_ffn2_w2, v_ln3_g, v_ln3_b):
    a = dict(locals())
    xi, yi, ci = _place()
    chip = 2 * xi + yi
    dev = 2 * chip + ci
    nb, seq, _ = x.shape
    t = nb * seq
    ndev = 8

    c_rows = nb * D // 128
    c_cw = _allgather8("gather_c", jnp.concatenate([c.reshape(c_rows, 128), conv_w.reshape(-1, 128)], axis=0))
    c_all = c_cw[:, :c_rows].reshape(ndev * nb, D)
    b_cols = lax.dynamic_slice(b_ada, (0, chip * ADA_COLS), (1, ADA_COLS))
    mod_part = _ada_fwd("ada_fwd", c_all, w_ada[0], b_cols)

    def nat(n):
        return jnp.swapaxes(a[n], 1, 2)[0] if n[-7:] in TRANSPOSED else a[n][0]

    def gather(names, extra=()):
        own = [nat(n).astype(MXU_DTYPE) for n in names]

        def weights(pieces):
            w = {}
            for n, mine, piece in zip(names, own, pieces):
                piece = lax.dynamic_update_slice(piece, mine[None], (chip, 0, 0))
                if n in COL_SHARDED:
                    wi = _from_shard_major(piece)
                    w[n] = jnp.concatenate([wi[:, :1536], wi[:, 1544:2056], wi[:, 1536:1544],
                                            jnp.zeros((D, 120), wi.dtype)], axis=1)
                else:
                    w[n + "t" if n in TRANSPOSED else n] = piece.reshape(-1, piece.shape[-1])
            return w

        return _gather_stage(own + list(extra)), weights

    first_stage, first_weights = gather(("ffn1_w1", "ffn1_w3"), extra=[mod_part])
    got = _run_stage("gather_w_first", first_stage)
    w = first_weights(got[:2])
    mod_all = lax.dynamic_update_slice(got[2], mod_part[None], (chip, 0, 0)).transpose(1, 0, 2).reshape(ndev * nb, 9 * D)
    mod = lax.dynamic_slice(mod_all, (nb * dev, 0), (nb, 9 * D)).reshape(nb, 9, D)

    conv_full = _from_shard_major(c_cw[0::2, c_rows:].reshape(4, 4, 256))
    pad8 = lambda v: jnp.concatenate([v.reshape(1, NH), jnp.zeros((1, 128 - NH), F32)], axis=1)
    sp = dict(ln1_g=ln1_g, ln1_b=ln1_b, ln2_g=ln2_g, ln2_b=ln2_b, ln3_g=ln3_g, ln3_b=ln3_b, conv_w=conv_full,
              conv_b=conv_b, dt_bias=pad8(dt_bias), a_log=pad8(a_log), d_rep=jnp.repeat(d_ssd[0], HP)[None],
              ssd_norm_w=ssd_norm_w, s5_a_re=s5_a_re[0], s5_a_im=s5_a_im[0], s5_log_dt=s5_log_dt[0], s5_b_re=s5_b_re[0],
              s5_b_im=s5_b_im[0], s5_c_re=s5_c_re[0], s5_c_im=s5_c_im[0], s5_d=s5_d, b_glu=b_glu)

    lsum, dx0, dmod, gbig, small = _local_step(x.reshape(t, D), loss_target.reshape(t, D), mod, w, sp, seq,
                                               dist=(gather(("ffn1_w2", "w_in", "w_glu", "w_out")),
                                                     gather(("ffn2_w1", "ffn2_w3", "ffn2_w2")), (ci, chip)))

    snames = [n for n in SMALL if n != "b_ada"]
    sgrad = dict(small)
    sgrad["b_glu"] = small["w_glu_b"]
    dm_rows = nb * 9 * D // 128
    tail = _allgather8("gather_small", jnp.concatenate(
        [dmod.reshape(dm_rows, 128),
         _pack([(lsum * (0.5 / D)).reshape(1)] + [sgrad[n] for n in snames], 1024).reshape(-1, 128)], axis=0))
    dmod_all = tail[:, :dm_rows].reshape(ndev * nb, 9 * D)
    dmod_cols = lax.dynamic_slice(dmod_all, (0, chip * ADA_COLS), (ndev * nb, ADA_COLS))
    g_w_ada, g_b_ada = _ada_bwd("ada_bwd", c_all, dmod_cols, dmod_all)

    gbig["w_ada"] = g_w_ada

    outs = {}
    for call, names in (("adamw_a", ("ffn1_w1", "ffn1_w3", "ffn1_w2", "w_in", "w_glu", "w_out")),
                        ("adamw_b", ("ffn2_w1", "ffn2_w3", "ffn2_w2", "w_ada"))):
        res = _adamw(call, [nat(n) for n in names], [gbig[n] for n in names], [nat("m_" + n) for n in names],
                     [nat("v_" + n) for n in names], 8)
        for kind, arrs in zip(("grad", "delta", "new_m", "new_v"), ([gbig[n] for n in names],) + tuple(res)):
            for n, arr in zip(names, arrs):
                outs[kind, n] = (arr.T if n in TRANSPOSED else arr)[None]

    ssum = _sum8("small_sum", tail)[dm_rows:].reshape(-1)

    def view2d(u):
        s = u.shape[1:]
        return u.reshape((1, s[0]) if len(s) == 1 else (-1, s[-1]))

    vshape = {n: view2d(a[n]).shape for n in SMALL}
    loss, *sums = _unpack(ssum, [()] + [vshape[n] if n != "conv_w" else (4, D) for n in snames])
    gsm = dict(zip(snames, sums))
    gsm["conv_w"] = lax.dynamic_slice(gsm["conv_w"], (0, chip * 256), (4, 256))
    gsm["b_ada"] = g_b_ada
    res = _adamw("adamw_small", [view2d(a[n]) for n in SMALL], [gsm[n] for n in SMALL],
                 [view2d(a["m_" + n]) for n in SMALL], [view2d(a["v_" + n]) for n in SMALL], 1)
    for kind, arrs in zip(("grad", "delta", "new_m", "new_v"), ([gsm[n] for n in SMALL],) + tuple(res)):
        for n, arr in zip(SMALL, arrs):
            outs[kind, n] = arr.reshape(a[n].shape)

    res = [loss, dx0.reshape(nb, seq, D)]
    for kind in ("grad", "delta", "new_m", "new_v"):
        res += [outs[kind, n] for n in WEIGHTS]
    return tuple(res)
```

```python
import functools
import math

import jax
import jax.numpy as jnp
from jax import lax
from jax.experimental import pallas as pl
from jax.experimental.pallas import tpu as pltpu

F32 = jnp.float32
BF16 = jnp.bfloat16
MXU_DTYPE = jnp.bfloat16

D = 1024
FF = 2816
FB = 1408
NH, HP, NS, NG = 8, 64, 128, 2
CH = 128
SW = 512
S5G, S5P, S5H = 32, 64, 16
S5L = S5G * S5P
PW = 2176
ALPHA = 2.0 ** 0.25
LN_EPS = 1e-5
ADAM_LR, ADAM_B1, ADAM_B2, ADAM_EPS, ADAM_WD, ADAM_STEP = 0.001, 0.9, 0.999, 1e-08, 0.01, 10
VMEM_LIMIT = 56 * 1024 * 1024
TM_WIDE = 512
MESH_T = pl.DeviceIdType.MESH


def _pcall(body, **kw):
    return pl.pallas_call(body, **kw)


def _cparams(sem=None, **kw):
    return pltpu.CompilerParams(dimension_semantics=sem, vmem_limit_bytes=VMEM_LIMIT, **kw)


def _dot(a, b):
    return jnp.dot(a, b, preferred_element_type=F32)


def _dot_nt(a, b):
    return lax.dot_general(a, b, (((1,), (1,)), ((), ())), preferred_element_type=F32)


def _dot_hi(a, b):
    return jnp.dot(a, b, preferred_element_type=F32, precision=lax.Precision.HIGHEST)


def _mx(a):
    return a.astype(MXU_DTYPE)


def _sigmoid(x):
    return 1.0 / (1.0 + jnp.exp(-x))


def _iota(shape, axis):
    return lax.broadcasted_iota(jnp.int32, shape, axis)


def _rowcall(name, fn, n_rows, tm, tpe, *, tiled=(), halos=(), exs=(), res=(), out_tiled=(), out_acc=(),
             out_exacc=(), scratch=(), reverse=False, batch=None, carry=None, carry_mid=0.0):
    if batch:
        n_rows //= batch
    nt = n_rows // tm

    def blk(i):
        return (nt - 1 - i) if reverse else i

    in_specs, args = [], []
    for a in tiled:
        if batch:
            in_specs.append(pl.BlockSpec((batch, tm, a.shape[1]), lambda i: (0, blk(i), 0)))
            args.append(a.reshape(batch, n_rows, a.shape[1]))
            continue
        in_specs.append(pl.BlockSpec((tm, a.shape[1]), lambda i: (blk(i), 0)))
        args.append(a)
    for a, rows in halos:
        r = tm // rows
        if batch:
            in_specs.append(pl.BlockSpec((batch, rows, a.shape[1]), lambda i, r=r: (0, jnp.maximum(blk(i) * r - 1, 0), 0)))
            args.append(a.reshape(batch, n_rows, a.shape[1]))
            continue
        in_specs.append(pl.BlockSpec((rows, a.shape[1]), lambda i, r=r: (jnp.maximum(blk(i) * r - 1, 0), 0)))
        args.append(a)
    for a in exs:
        in_specs.append(pl.BlockSpec((1,) + a.shape[1:], lambda i: (blk(i) // tpe, 0, 0)))
        args.append(a)
    for a in res:
        nd = a.ndim
        in_specs.append(pl.BlockSpec(a.shape, lambda i, nd=nd: (0,) * nd, pipeline_mode=pl.Buffered(1)))
        args.append(a)
    any_spec = pl.BlockSpec(memory_space=pl.ANY)
    st_ins = carry.ins if carry else []
    st_outs = carry.out_shapes if carry else []
    st_sems = carry.sem_shapes() if carry else []
    base_in = len(args)
    in_specs += [any_spec] * len(st_ins)
    args += st_ins
    out_specs, out_shape = [], []
    for s in out_tiled:
        if batch:
            out_specs.append(pl.BlockSpec((batch, tm, s.shape[1]), lambda i: (0, blk(i), 0)))
            out_shape.append(_sds((batch, n_rows, s.shape[1]), s.dtype))
            continue
        out_specs.append(pl.BlockSpec((tm, s.shape[1]), lambda i: (blk(i), 0)))
        out_shape.append(s)
    for s in out_acc:
        nd = len(s.shape)
        out_specs.append(pl.BlockSpec(s.shape, lambda i, nd=nd: (0,) * nd))
        out_shape.append(s)
    for s in out_exacc:
        out_specs.append(pl.BlockSpec((1,) + s.shape[1:], lambda i: (blk(i) // tpe, 0, 0)))
        out_shape.append(s)
    base_out = len(out_shape)
    out_specs += [any_spec] * len(st_outs)
    out_shape += st_outs
    aliases = {base_in + k: base_out + v for k, v in carry.aliases.items()} if carry else {}
    n = [len(tiled), len(halos), len(exs), len(res), len(st_ins), len(out_tiled), len(out_acc), len(out_exacc),
         len(st_outs), len(scratch), len(st_sems)]

    def body(*refs):
        groups, k = [], 0
        for m in n:
            groups.append(refs[k:k + m])
            k += m
        i = pl.program_id(0)
        b = blk(i)

        class ctx:
            first = i == 0
            pos = b % tpe
            seq_first = (b % tpe) == 0
            seq_last = (b % tpe) == tpe - 1
            ex_enter = (i % tpe) == 0

        def work():
            for cond, refs_ in ((ctx.first, groups[6]), (ctx.ex_enter, groups[7])):
                if refs_:
                    @pl.when(cond)
                    def _():
                        for r in refs_:
                            r[...] = jnp.zeros(r.shape, r.dtype)
            fn(ctx, *groups[0:4], *groups[5:8], groups[9])

        if carry:
            carry.run((groups[4], groups[8], groups[10]), i == 0, i == min(nt - 1, int(carry_mid * nt)), i == nt - 1, work)
        else:
            work()

    outs = _pcall(body, name=name, grid=(nt,), in_specs=in_specs, out_specs=out_specs, out_shape=out_shape,
                  input_output_aliases=aliases, scratch_shapes=list(scratch) + st_sems,
                  compiler_params=_cparams(("arbitrary",)))(*args)
    if batch:
        outs = [o.reshape(batch * n_rows, o.shape[2]) if k < len(out_tiled) else o for k, o in enumerate(outs)]
    return (outs[:base_out], outs[base_out:]) if carry else outs


def _acc(ref, val):
    ref[...] += val


def _sds(shape, dtype=F32):
    return jax.ShapeDtypeStruct(shape, dtype)


def _ln_fwd(r, g, b):
    mu = jnp.mean(r, axis=-1, keepdims=True)
    rc = r - mu
    var = jnp.mean(rc * rc, axis=-1, keepdims=True)
    return rc * lax.rsqrt(var + LN_EPS) * g + b


def _ln_bwd(r, g, dy):
    mu = jnp.mean(r, axis=-1, keepdims=True)
    rc = r - mu
    var = jnp.mean(rc * rc, axis=-1, keepdims=True)
    rstd = lax.rsqrt(var + LN_EPS)
    xhat = rc * rstd
    dxh = dy * g
    dr = rstd * (dxh - jnp.mean(dxh, axis=-1, keepdims=True) - xhat * jnp.mean(dxh * xhat, axis=-1, keepdims=True))
    return dr, jnp.sum(dy * xhat, axis=0, keepdims=True), jnp.sum(dy, axis=0, keepdims=True)


def _ffn_fwd_hidden(name, x, mod, k0, w1, w3, seq, **carry):
    t = x.shape[0]
    tm = min(TM_WIDE, seq)

    def fn(ctx, tiled, halos, exs, res, outs, accs, exaccs, scr):
        mod_ref, = exs
        w1_ref, w3_ref = res
        h_ref, ab_ref, s_ref = outs
        sh, sc = mod_ref[0, k0:k0 + 1, :], mod_ref[0, k0 + 1:k0 + 2, :]
        h = _mx(tiled[0][...] * (1.0 + sc) + sh)
        h_ref[...] = h
        for j in range(2):
            a = _dot_nt(h, w1_ref[j * FB:(j + 1) * FB, :])
            b = _dot_nt(h, w3_ref[j * FB:(j + 1) * FB, :])
            ab_ref[:, j * FB:(j + 1) * FB] = _mx(a)
            ab_ref[:, FF + j * FB:FF + (j + 1) * FB] = _mx(b)
            s_ref[:, j * FB:(j + 1) * FB] = _mx(a * _sigmoid(a) * b)

    return _rowcall(name, fn, t, tm, seq // tm, tiled=[x], exs=[mod], res=[w1, w3],
                    out_tiled=[_sds((t, D), MXU_DTYPE), _sds((t, 2 * FF), MXU_DTYPE), _sds((t, FF), MXU_DTYPE)], **carry)


def _ffn_fwd_out(name, s, x, mod, k0, w2, lng, lnb, seq, tgt=None, **carry):
    t = x.shape[0]
    tm = min(TM_WIDE, seq)
    with_loss = tgt is not None

    def fn(ctx, tiled, halos, exs, res, outs, accs, exaccs, scr):
        mod_ref, = exs
        w2_ref, g_ref, b_ref = res
        xo_ref, r_ref, f_ref = outs[:3]
        f = _dot(tiled[0][...], w2_ref[...])
        f_ref[...] = f
        r = ALPHA * tiled[1][...] + 0.5 * mod_ref[0, k0 + 2:k0 + 3, :] * f
        r_ref[...] = r
        xo = _ln_fwd(r, g_ref[...], b_ref[...])
        xo_ref[...] = xo
        if with_loss:
            e = xo - tiled[2][...]
            outs[3][...] = e * (1.0 / D)
            _acc(accs[0], jnp.sum(e * e) * jnp.ones((8, 128), F32))

    return _rowcall(name, fn, t, tm, seq // tm, tiled=[s, x] + ([tgt] if with_loss else []), exs=[mod],
                    res=[w2, lng, lnb], out_tiled=[_sds((t, D))] * (4 if with_loss else 3),
                    out_acc=[_sds((8, 128))] if with_loss else [], **carry)


def _ffn_bwd1(name, dxo, r, ab, f, mod, k0, lng, w2, seq, **carry):
    t = dxo.shape[0]
    tm = 256

    def fn(ctx, tiled, halos, exs, res, outs, accs, exaccs, scr):
        dxo_ref, r_ref, ab_ref, f_ref = tiled
        mod_ref, = exs
        g_ref, w2_ref = res
        dr_ref, df_ref, dab_ref = outs
        g = mod_ref[0, k0 + 2:k0 + 3, :]
        dr, dgam, dbet = _ln_bwd(r_ref[...], g_ref[...], dxo_ref[...])
        dr_ref[...] = dr
        _acc(accs[0], dgam)
        _acc(accs[1], dbet)
        _acc(exaccs[0].at[0], jnp.sum(0.5 * f_ref[...] * dr, axis=0, keepdims=True))
        df = _mx(0.5 * g * dr)
        df_ref[...] = df
        for j in range(2):
            ds = _dot_nt(df, w2_ref[j * FB:(j + 1) * FB, :])
            a = ab_ref[:, j * FB:(j + 1) * FB].astype(F32)
            b = ab_ref[:, FF + j * FB:FF + (j + 1) * FB].astype(F32)
            sig = _sigmoid(a)
            dab_ref[:, j * FB:(j + 1) * FB] = _mx(ds * b * (sig * (1.0 + a * (1.0 - sig))))
            dab_ref[:, FF + j * FB:FF + (j + 1) * FB] = _mx(ds * (a * sig))

    b = mod.shape[0]
    return _rowcall(name, fn, t, tm, seq // tm, tiled=[dxo, r, ab, f], exs=[mod], res=[lng, w2],
                    out_tiled=[_sds((t, D)), _sds((t, D), MXU_DTYPE), _sds((t, 2 * FF), MXU_DTYPE)],
                    out_acc=[_sds((1, D)), _sds((1, D))], out_exacc=[_sds((b, 1, D))], **carry)


def _ffn_bwd(name, dxo, r, ab, f, x, mod, k0, lng, w2, w1t, w3t, seq):
    t = dxo.shape[0]
    tm = 256

    def fn(ctx, tiled, halos, exs, res, outs, accs, exaccs, scr):
        dxo_ref, r_ref, ab_ref, f_ref, x_ref = tiled
        mod_ref, = exs
        g_ref, w2_ref, w1t_ref, w3t_ref = res
        dx_ref, df_ref, dab_ref = outs
        sc, g = mod_ref[0, k0 + 1:k0 + 2, :], mod_ref[0, k0 + 2:k0 + 3, :]
        dr, dgam, dbet = _ln_bwd(r_ref[...], g_ref[...], dxo_ref[...])
        _acc(accs[0], dgam)
        _acc(accs[1], dbet)
        _acc(exaccs[0].at[0], jnp.sum(0.5 * f_ref[...] * dr, axis=0, keepdims=True))
        df = _mx(0.5 * g * dr)
        df_ref[...] = df
        dh = jnp.zeros((tm, D), F32)
        for j in range(2):
            blk = slice(j * FB, (j + 1) * FB)
            ds = _dot_nt(df, w2_ref[blk, :])
            a = ab_ref[:, blk].astype(F32)
            b = ab_ref[:, FF + j * FB:FF + (j + 1) * FB].astype(F32)
            sig = _sigmoid(a)
            da = _mx(ds * b * (sig * (1.0 + a * (1.0 - sig))))
            db = _mx(ds * (a * sig))
            dab_ref[:, blk] = da
            dab_ref[:, FF + j * FB:FF + (j + 1) * FB] = db
            dh = dh + _dot(da, w1t_ref[blk, :]) + _dot(db, w3t_ref[blk, :])
        dx_ref[...] = ALPHA * dr + dh * (1.0 + sc)
        _acc(exaccs[1].at[0], jnp.sum(dh, axis=0, keepdims=True))
        _acc(exaccs[2].at[0], jnp.sum(dh * x_ref[...], axis=0, keepdims=True))

    b = mod.shape[0]
    return _rowcall(name, fn, t, tm, seq // tm, tiled=[dxo, r, ab, f, x], exs=[mod], res=[lng, w2, w1t, w3t],
                    out_tiled=[_sds((t, D)), _sds((t, D), MXU_DTYPE), _sds((t, 2 * FF), MXU_DTYPE)],
                    out_acc=[_sds((1, D)), _sds((1, D))], out_exacc=[_sds((b, 1, D))] * 3)


def _mod_bwd(name, dab, dr, x, mod, k0, wts, seq, extra=(), nt=False, **carry):
    t = dr.shape[0]
    tm = min(TM_WIDE, seq)
    nin = 1 + len(extra)
    width = dab.shape[1] + sum(e.shape[1] for e in extra)

    def fn(ctx, tiled, halos, exs, res, outs, accs, exaccs, scr):
        parts = tiled[:nin]
        dr_ref, x_ref = tiled[nin:]
        mod_ref, = exs
        sc = mod_ref[0, k0 + 1:k0 + 2, :]
        if nin == 1:
            dp = parts[0][...]
        else:
            dp = jnp.concatenate([_mx(p[...]) for p in parts], axis=1)
            outs[1][...] = dp
        dh, off = jnp.zeros((tm, D), F32), 0
        for w_ref in res:
            k = w_ref.shape[1 if nt else 0]
            dh = dh + (_dot_nt if nt else _dot)(dp[:, off:off + k], w_ref[...])
            off += k
        outs[0][...] = ALPHA * dr_ref[...] + dh * (1.0 + sc)
        _acc(exaccs[0].at[0], jnp.sum(dh, axis=0, keepdims=True))
        _acc(exaccs[1].at[0], jnp.sum(dh * x_ref[...], axis=0, keepdims=True))

    b = mod.shape[0]
    out_tiled = [_sds((t, D))] + ([_sds((t, width), MXU_DTYPE)] if nin > 1 else [])
    return _rowcall(name, fn, t, tm, seq // tm, tiled=[dab, *extra, dr, x], exs=[mod], res=list(wts),
                    out_tiled=out_tiled, out_exacc=[_sds((b, 1, D)), _sds((b, 1, D))], **carry)


def _tn_matmul(name, a, b, bm, bn, bt=2048, carry=None, a_cols=None):
    t = a.shape[0]
    bt = min(bt, t)
    start, m = a_cols or (0, a.shape[1])
    off = start // bm
    n = b.shape[1]
    grid = (m // bm, n // bn, t // bt)
    n_in, n_out = (len(carry.ins), len(carry.out_shapes)) if carry else (0, 0)

    def body(a_ref, b_ref, *refs):
        o_ref = refs[n_in]

        def work():
            @pl.when(pl.program_id(2) == 0)
            def _():
                o_ref[...] = jnp.zeros(o_ref.shape, F32)
            o_ref[...] += lax.dot_general(a_ref[...], b_ref[...], (((0,), (0,)), ((), ())), preferred_element_type=F32)

        if not carry:
            return work()
        step = (pl.program_id(0) * grid[1] + pl.program_id(1)) * grid[2] + pl.program_id(2)
        carry.run((refs[:n_in], refs[n_in + 1:n_in + 1 + n_out], refs[n_in + 1 + n_out:]), step == 0, step == 0,
                  step == grid[0] * grid[1] * grid[2] - 1, work)

    any_spec = pl.BlockSpec(memory_space=pl.ANY)
    outs = _pcall(body, name=name, grid=grid,
                  in_specs=[pl.BlockSpec((bt, bm), lambda i, j, k: (k, i + off)), pl.BlockSpec((bt, bn), lambda i, j, k: (k, j))]
                  + [any_spec] * n_in,
                  out_specs=[pl.BlockSpec((bm, bn), lambda i, j, k: (i, j))] + [any_spec] * n_out,
                  out_shape=[_sds((m, n))] + (carry.out_shapes if carry else []),
                  input_output_aliases={2 + k: 1 + v for k, v in carry.aliases.items()} if carry else {},
                  scratch_shapes=carry.sem_shapes() if carry else [],
                  compiler_params=_cparams(("arbitrary",) * 3 if carry else ("parallel", "parallel", "arbitrary")))(
                      a, b, *(carry.ins if carry else []))
    return (outs[0], outs[1:]) if carry else outs[0]


S5_BLK = 16


def _tn_diag(name, a, b, bt=1024):
    t = a.shape[0]
    bt = min(bt, t)
    rows, cols = S5_BLK * S5P, S5_BLK * S5H
    nblk = a.shape[1] // rows

    def body(a_ref, b_ref, o_ref):
        @pl.when(pl.program_id(1) == 0)
        def _():
            o_ref[...] = jnp.zeros(o_ref.shape, F32)
        o_ref[0] += lax.dot_general(a_ref[...], b_ref[...], (((0,), (0,)), ((), ())), preferred_element_type=F32)

    return _pcall(body, name=name, grid=(nblk, t // bt),
                  in_specs=[pl.BlockSpec((bt, rows), lambda i, k: (k, i)),
                            pl.BlockSpec((bt, cols), lambda i, k: (k, i % (S5G // S5_BLK)))],
                  out_specs=pl.BlockSpec((1, rows, cols), lambda i, k: (i, 0, 0)), out_shape=_sds((nblk, rows, cols)),
                  compiler_params=_cparams(("parallel", "arbitrary")))(a, b)


def _diag_groups(o):
    o = o.reshape(2, S5G // S5_BLK, S5_BLK, S5P, S5_BLK, S5H)
    return jnp.einsum("rbgpgh->rbgph", o).reshape(2, S5G, S5P, S5H)


def _proj_fwd(name, x, mod, w_in, seq):
    t = x.shape[0]
    tm = min(TM_WIDE, seq)

    def fn(ctx, tiled, halos, exs, res, outs, accs, exaccs, scr):
        mod_ref, = exs
        sh, sc = mod_ref[0, 3:4, :], mod_ref[0, 4:5, :]
        h = _mx(tiled[0][...] * (1.0 + sc) + sh)
        outs[0][...] = h
        outs[1][...] = _dot(h, res[0][...])

    return _rowcall(name, fn, t, tm, seq // tm, tiled=[x], exs=[mod], res=[w_in],
                    out_tiled=[_sds((t, D), MXU_DTYPE), _sds((t, PW))])


def _shift_rows(cur, prev8, j):
    if j == 0:
        return cur
    rolled = pltpu.roll(cur, j, 0)
    top = jnp.where(_iota((8, cur.shape[1]), 0) < j, pltpu.roll(prev8, j, 0), rolled[0:8])
    return jnp.concatenate([top, rolled[8:]], axis=0)


def _shift_rows_up(cur, next8, j):
    if j == 0:
        return cur
    n = cur.shape[0]
    rolled = pltpu.roll(cur, n - j, 0)
    bot = jnp.where(_iota((8, cur.shape[1]), 0) >= 8 - j, pltpu.roll(next8, 8 - j, 0), rolled[n - 8:n])
    return jnp.concatenate([rolled[:n - 8], bot], axis=0)


def _softplus(x):
    return jnp.maximum(x, 0.0) + jnp.log(1.0 + jnp.exp(-jnp.abs(x)))


def _ssd_common(proj_ref, xpre, dtb_ref, alog_ref):
    xs = bm = cm = None
    if xpre is not None:
        xbc = xpre * _sigmoid(xpre)
        xs, bm, cm = xbc[:, 0:SW], xbc[:, SW:SW + 256], xbc[:, SW + 256:SW + 512]
    dtraw = proj_ref[:, PW - 128:PW] + dtb_ref[...]
    dt = _softplus(dtraw)
    a = -jnp.exp(alog_ref[...])
    tril = (_iota((CH, CH), 0) >= _iota((CH, CH), 1)).astype(F32)
    acs = _dot_hi(tril, dt * a)
    return xs, bm, cm, dtraw, dt, a, acs, acs.T


def _pair_lane(lo, hi):
    r = lo.shape[0]
    return jnp.where(_iota((r, 128), 1) < HP, lo, hi)


def _ssd_fwd(name, proj, conv_w, conv_b, dt_bias, a_log, d_rep, norm_w, seq):
    t = proj.shape[0]
    nb = t // seq

    def fn(ctx, tiled, halos, exs, res, outs, accs, exaccs, scr):
        @pl.when(ctx.seq_first)
        def _():
            scr[0][...] = jnp.zeros(scr[0].shape, F32)

        for b in range(nb):
            one(ctx, res, [r.at[b] for r in tiled + halos + outs + scr])

    def one(ctx, res, refs):
        proj_ref, halo_ref, yo_ref, xpre_ref, y_ref, sprev_ref, state_ref = refs
        cw_ref, cb_ref, dtb_ref, alog_ref, d_ref, nw_ref = res
        raw = proj_ref[:, SW:SW + D]
        prev8 = halo_ref[:, SW:SW + D] * jnp.where(ctx.seq_first, 0.0, 1.0)
        xpre = cb_ref[...] + cw_ref[3:4, :] * raw
        for j in (1, 2, 3):
            xpre = xpre + cw_ref[3 - j:4 - j, :] * _shift_rows(raw, prev8, j)
        xpre_ref[...] = xpre
        xs, bm, cm, dtraw, dt, a, acs, acst = _ssd_common(proj_ref, xpre, dtb_ref, alog_ref)
        causal = _iota((CH, CH), 0) >= _iota((CH, CH), 1)
        lane_lo = _iota((CH, 128), 1) < HP
        sprev = state_ref[...]
        sprev_ref[...] = sprev
        ys = []
        for g in range(NG):
            bmg, cmg = bm[:, g * NS:(g + 1) * NS], cm[:, g * NS:(g + 1) * NS]
            bmt = bmg.T
            cb = _dot(_mx(cmg), _mx(bmt))
            for q in (2 * g, 2 * g + 1):
                xsq = xs[:, 128 * q:128 * q + 128]
                xd = xsq * _pair_lane(dt[:, 2 * q:2 * q + 1], dt[:, 2 * q + 1:2 * q + 2])
                sp = sprev[:, 128 * q:128 * q + 128]
                ydiag = jnp.zeros((CH, 128), F32)
                snew = jnp.zeros((NS, 128), F32)
                for jj in range(2):
                    h = 2 * q + jj
                    col, row = acs[:, h:h + 1], acst[h:h + 1, :]
                    lm = jnp.where(causal, jnp.exp(jnp.minimum(col - row, 0.0)), 0.0)
                    xm = _mx(jnp.where(lane_lo if jj == 0 else ~lane_lo, xd, 0.0))
                    ydiag = ydiag + _dot(_mx(cb * lm), xm)
                    dec_row = jnp.exp(acst[h:h + 1, CH - 1:CH] - row)
                    snew = snew + _dot(_mx(bmt * dec_row), xm)
                e_pair = jnp.exp(_pair_lane(acs[:, 2 * q:2 * q + 1], acs[:, 2 * q + 1:2 * q + 2]))
                yoff = _dot(_mx(cmg), _mx(sp)) * e_pair
                cd = jnp.exp(_pair_lane(acst[2 * q:2 * q + 1, CH - 1:CH], acst[2 * q + 1:2 * q + 2, CH - 1:CH]))
                state_ref[:, 128 * q:128 * q + 128] = cd * sp + snew
                ys.append(ydiag + yoff + d_ref[:, 128 * q:128 * q + 128] * xsq)
        y = jnp.concatenate(ys, axis=1)
        y_ref[...] = y
        z = proj_ref[:, 0:SW]
        yz = y * (z * _sigmoid(z))
        outp = []
        for g in range(NG):
            seg = yz[:, 256 * g:256 * g + 256]
            rinv = lax.rsqrt(jnp.mean(seg * seg, axis=-1, keepdims=True) + LN_EPS)
            outp.append(seg * rinv * nw_ref[:, 256 * g:256 * g + 256])
        yo_ref[...] = _mx(jnp.concatenate(outp, axis=1))

    return _rowcall(name, fn, t, CH, seq // CH, tiled=[proj], halos=[(proj, 8)],
                    res=[conv_w, conv_b, dt_bias, a_log, d_rep, norm_w],
                    out_tiled=[_sds((t, SW), MXU_DTYPE), _sds((t, D)), _sds((t, SW)), _sds((t, SW))],
                    scratch=[pltpu.VMEM((nb, NS, SW), F32)], batch=nb)


def _ssd_bwd(name, dyo, proj, xpre_all, y_all, sprev_all, conv_w, dt_bias, a_log, d_rep, norm_w, seq, **carry):
    t = proj.shape[0]
    nb = t // seq

    def fn(ctx, tiled, halos, exs, res, outs, accs, exaccs, scr):
        @pl.when(ctx.seq_last)
        def _():
            for r in scr:
                r[...] = jnp.zeros(r.shape, F32)

        for b in range(nb):
            one(ctx, res, accs, [r.at[b] for r in tiled + halos + outs + scr])

    def one(ctx, res, accs, refs):
        dyo_ref, proj_ref, xpre_ref, y_ref, sprev_ref, halo_ref, dzx_ref, ddt_ref, ds_ref, nxt_ref = refs
        cw_ref, dtb_ref, alog_ref, d_ref, nw_ref = res
        dnw_acc, dd_acc, dcw_acc, dcb_acc, ddtb_acc, dalog_acc = accs
        _, _, _, dtraw, dt, a, acs, acst = _ssd_common(proj_ref, None, dtb_ref, alog_ref)

        def act(lo, hi):
            v = xpre_ref[:, lo:hi]
            return v * _sigmoid(v)

        y = y_ref[...]
        z = proj_ref[:, 0:SW]
        sz = _sigmoid(z)
        siluz = z * sz
        yz = y * siluz
        dyo = dyo_ref[...]
        dyz_parts, dnw_parts = [], []
        for g in range(NG):
            seg = yz[:, 256 * g:256 * g + 256]
            rinv = lax.rsqrt(jnp.mean(seg * seg, axis=-1, keepdims=True) + LN_EPS)
            yn = seg * rinv
            dseg = dyo[:, 256 * g:256 * g + 256]
            dnw_parts.append(jnp.sum(dseg * yn, axis=0, keepdims=True))
            dyn = dseg * nw_ref[:, 256 * g:256 * g + 256]
            dyz_parts.append(rinv * (dyn - yn * jnp.mean(dyn * yn, axis=-1, keepdims=True)))
        dyz = jnp.concatenate(dyz_parts, axis=1)
        _acc(dnw_acc, jnp.concatenate(dnw_parts, axis=1))
        dy = dyz * siluz
        dz = dyz * y * (sz * (1.0 + z * (1.0 - sz)))
        dd_parts = []

        causal = _iota((CH, CH), 0) >= _iota((CH, CH), 1)
        anti = _iota((CH, CH), 0) <= _iota((CH, CH), 1)
        lane_lo = _iota((CH, 128), 1) < HP
        lane_id = _iota((CH, 128), 1)
        last_row = _iota((CH, 128), 0) == CH - 1
        dacs = jnp.zeros((CH, 128), F32)
        ddt_x = jnp.zeros((CH, 128), F32)
        dxs_parts, dbm_parts, dcm_parts = [], [], []
        for g in range(NG):
            bmg, cmg = act(SW + g * NS, SW + (g + 1) * NS), act(SW + 256 + g * NS, SW + 256 + (g + 1) * NS)
            bmt, cmt = bmg.T, cmg.T
            cb = _dot(_mx(cmg), _mx(bmt))
            cbt = _dot(_mx(bmg), _mx(cmt))
            dcb = jnp.zeros((CH, CH), F32)
            dcbt = jnp.zeros((CH, CH), F32)
            dbmg = jnp.zeros((CH, NS), F32)
            dcmg = jnp.zeros((CH, NS), F32)
            for q in (2 * g, 2 * g + 1):
                sl = slice(128 * q, 128 * q + 128)
                xsq = act(128 * q, 128 * q + 128)
                dtp = _pair_lane(dt[:, 2 * q:2 * q + 1], dt[:, 2 * q + 1:2 * q + 2])
                xd = xsq * dtp
                dyq = dy[:, sl]
                dd_parts.append(jnp.sum(dyq * xsq, axis=0, keepdims=True))
                sp = sprev_ref[:, sl]
                dsn = ds_ref[:, sl]
                e_pair = jnp.exp(_pair_lane(acs[:, 2 * q:2 * q + 1], acs[:, 2 * q + 1:2 * q + 2]))
                cd = jnp.exp(_pair_lane(acst[2 * q:2 * q + 1, CH - 1:CH], acst[2 * q + 1:2 * q + 2, CH - 1:CH]))
                dye = dyq * e_pair
                dcmg = dcmg + _dot(_mx(dye), _mx(sp.T))
                dsp = _dot(_mx(cmt), _mx(dye)) + cd * dsn
                yoff = _dot(_mx(cmg), _mx(sp)) * e_pair
                dacs_lane = dyq * yoff
                dxd = jnp.zeros((CH, 128), F32)
                sds = jnp.sum(dsn * sp, axis=0, keepdims=True) * cd
                for jj in range(2):
                    h = 2 * q + jj
                    hm = lane_lo if jj == 0 else ~lane_lo
                    col, row = acs[:, h:h + 1], acst[h:h + 1, :]
                    lm = jnp.where(causal, jnp.exp(jnp.minimum(col - row, 0.0)), 0.0)
                    lmt = jnp.where(anti, jnp.exp(jnp.minimum(row - col, 0.0)), 0.0)
                    xm = _mx(jnp.where(hm, xd, 0.0))
                    dym = _mx(jnp.where(hm, dyq, 0.0))
                    gm = _dot_nt(dym, xm)
                    gmt = _dot_nt(xm, dym)
                    dcb = dcb + gm * lm
                    dcbt = dcbt + gmt * lmt
                    dxd = dxd + _dot(_mx(cbt * lmt), dym)
                    w = gm * cb * lm
                    wt = gmt * cbt * lmt
                    dacs_h = jnp.sum(w, axis=1, keepdims=True) - jnp.sum(wt, axis=1, keepdims=True)
                    alast = acst[h:h + 1, CH - 1:CH]
                    dec_col = jnp.exp(alast - col)
                    dsm = _mx(jnp.where(hm[0:NS], dsn, 0.0))
                    qh = _dot_nt(xm, dsm)
                    dbmg = dbmg + qh * dec_col
                    ddec = jnp.sum(qh * bmg, axis=1, keepdims=True)
                    dxd = dxd + _dot(_mx(bmg * dec_col), dsm)
                    dacs_h = dacs_h - ddec * dec_col
                    dacs_h = dacs_h + jnp.sum(jnp.where(hm, dacs_lane, 0.0), axis=1, keepdims=True)
                    tail = jnp.sum(ddec * dec_col, axis=0, keepdims=True) + jnp.sum(
                        jnp.where(hm[0:1], sds, 0.0), axis=1, keepdims=True)
                    dacs = dacs + jnp.where(lane_id == h, dacs_h, 0.0) + jnp.where(
                        last_row & (lane_id == h), tail, 0.0)
                ds_ref[:, sl] = dsp
                for jj in range(2):
                    h = 2 * q + jj
                    hm = lane_lo if jj == 0 else ~lane_lo
                    ddt_x = ddt_x + jnp.where(lane_id == h, jnp.sum(jnp.where(hm, dxd * xsq, 0.0), axis=1, keepdims=True), 0.0)
                dxs_parts.append(dxd * dtp + d_ref[:, sl] * dyq)
            dcmg = dcmg + _dot(_mx(dcb), _mx(bmg))
            dbmg = dbmg + _dot(_mx(dcbt), _mx(cmg))
            dbm_parts.append(dbmg)
            dcm_parts.append(dcmg)
        triu = (_iota((CH, CH), 0) <= _iota((CH, CH), 1)).astype(F32)
        dadt = _dot_hi(triu, dacs)
        ddt = dadt * a + ddt_x
        _acc(dalog_acc, jnp.sum(dadt * dt, axis=0, keepdims=True) * a)
        ddtraw = ddt * _sigmoid(dtraw)
        ddt_ref[...] = ddtraw
        _acc(ddtb_acc, jnp.sum(ddtraw, axis=0, keepdims=True))
        _acc(dd_acc, jnp.concatenate(dd_parts, axis=1))
        dxbc = jnp.concatenate(dxs_parts + dbm_parts + dcm_parts, axis=1)
        xpre = xpre_ref[...]
        sx = _sigmoid(xpre)
        dpre = dxbc * (sx * (1.0 + xpre * (1.0 - sx)))
        _acc(dcb_acc, jnp.sum(dpre, axis=0, keepdims=True))
        raw = proj_ref[:, SW:SW + D]
        prev8 = halo_ref[:, SW:SW + D] * jnp.where(ctx.seq_first, 0.0, 1.0)
        next8 = nxt_ref[...]
        draw = cw_ref[3:4, :] * dpre
        dcw = [None] * 4
        dcw[3] = jnp.sum(dpre * raw, axis=0, keepdims=True)
        for j in (1, 2, 3):
            dcw[3 - j] = jnp.sum(dpre * _shift_rows(raw, prev8, j), axis=0, keepdims=True)
            draw = draw + cw_ref[3 - j:4 - j, :] * _shift_rows_up(dpre, next8, j)
        _acc(dcw_acc, jnp.concatenate(dcw + [jnp.zeros((4, D), F32)], axis=0))
        nxt_ref[...] = dpre[0:8]
        dzx_ref[:, 0:SW] = dz
        dzx_ref[:, SW:SW + D] = draw

    return _rowcall(name, fn, t, CH, seq // CH, tiled=[dyo, proj, xpre_all, y_all, sprev_all], halos=[(proj, 8)],
                    res=[conv_w, dt_bias, a_log, d_rep, norm_w],
                    out_tiled=[_sds((t, SW + D)), _sds((t, 128))],
                    out_acc=[_sds((1, SW)), _sds((1, SW)), _sds((8, D)), _sds((1, D)), _sds((1, 128)), _sds((1, 128))],
                    scratch=[pltpu.VMEM((nb, NS, SW), F32), pltpu.VMEM((nb, 8, D), F32)], reverse=True, batch=nb, **carry)


def _gelu(y):
    k = math.sqrt(2.0 / math.pi)
    return 0.5 * y * (1.0 + jnp.tanh(k * (y + 0.044715 * y * y * y)))


def _gelu_grad(y):
    k = math.sqrt(2.0 / math.pi)
    th = jnp.tanh(k * (y + 0.044715 * y * y * y))
    return 0.5 * (1.0 + th) + 0.5 * y * (1.0 - th * th) * k * (1.0 + 3.0 * 0.044715 * y * y)


S5T = 256


def _cmul_add(xr, xi, ar, ai, sr, si):
    return xr + ar * sr - ai * si, xi + ar * si + ai * sr


def _s5_fwd(name, proj, bbd, cbd, pw, tab, d5, w_glu, b_glu, seq, **carry):
    t = proj.shape[0]
    tm = S5T

    def fn(ctx, tiled, halos, exs, res, outs, accs, exaccs, scr):
        proj_ref, = tiled
        bbd_ref, cbd_ref, pw_ref, tab_ref, d_ref, wg_ref, bg_ref = res
        out_ref, xst_ref, y_ref, xb_ref, ub_ref = outs
        carry_ref, = scr

        @pl.when(ctx.seq_first)
        def _():
            carry_ref[...] = jnp.zeros(carry_ref.shape, F32)

        u = proj_ref[:, 1536:2048]
        bu = _dot(_mx(u), bbd_ref[...])
        xr, xi = bu[:, :S5L].reshape(tm // 8, 8, S5L), bu[:, S5L:].reshape(tm // 8, 8, S5L)
        for k, sh in enumerate((1, 2, 4)):
            xr, xi = _cmul_add(xr, xi, pw_ref[k, :, :S5L], pw_ref[k, :, S5L:], pltpu.roll(xr, sh, 1), pltpu.roll(xi, sh, 1))
        cr, ci = carry_ref[0:1, :S5L], carry_ref[0:1, S5L:]
        for i in range(tm // 8):
            tr, ti = _cmul_add(xr[i], xi[i], tab_ref[:, :S5L], tab_ref[:, S5L:], cr, ci)
            xst_ref[8 * i:8 * i + 8, :S5L] = tr
            xst_ref[8 * i:8 * i + 8, S5L:] = ti
            cr, ci = tr[7:8], ti[7:8]
        carry_ref[0:1, :S5L] = cr
        carry_ref[0:1, S5L:] = ci
        xb = _mx(xst_ref[...])
        xb_ref[...] = xb
        ub_ref[...] = _mx(u)
        y = _dot(xb, cbd_ref[...]) + u * d_ref[...]
        y_ref[...] = y
        g = _gelu(y)
        v = _dot(_mx(g), wg_ref[...]) + bg_ref[...]
        out_ref[...] = _mx(g * _sigmoid(v))

    return _rowcall(name, fn, t, tm, seq // tm, tiled=[proj], res=[bbd, cbd, pw, tab, d5, w_glu, b_glu],
                    out_tiled=[_sds((t, SW), MXU_DTYPE), _sds((t, 2 * S5L)), _sds((t, SW)),
                               _sds((t, 2 * S5L), MXU_DTYPE), _sds((t, SW), MXU_DTYPE)],
                    scratch=[pltpu.VMEM((8, 2 * S5L), F32)], **carry)


def _s5_bwd(name, dout, proj, xst, y_all, bbdt, cbdt, pwc, tabc, d5, w_glu, b_glu, seq, **carry):
    t = proj.shape[0]
    tm = S5T

    def fn(ctx, tiled, halos, exs, res, outs, accs, exaccs, scr):
        dout_ref, proj_ref, xst_ref, y_ref = tiled
        halo_ref, = halos
        bbdt_ref, cbdt_ref, pw_ref, tab_ref, d_ref, wg_ref, bg_ref = res
        du_ref, lam_ref, dyb_ref, gb_ref, dvb_ref = outs
        da_acc, dd_acc, dbg_acc = accs
        carry_ref, lamf_ref = scr

        @pl.when(ctx.seq_last)
        def _():
            carry_ref[...] = jnp.zeros(carry_ref.shape, F32)

        u = proj_ref[:, 1536:2048]
        y = y_ref[...]
        g = _gelu(y)
        v = _dot(_mx(g), wg_ref[...]) + bg_ref[...]
        sg = _sigmoid(v)
        dout = dout_ref[...]
        dv = dout * g * sg * (1.0 - sg)
        dvb = _mx(dv)
        dvb_ref[...] = dvb
        gb_ref[...] = _mx(g)
        _acc(dbg_acc, jnp.sum(dv, axis=0, keepdims=True))
        dg = dout * sg + _dot_nt(dvb, wg_ref[...])
        dy = dg * _gelu_grad(y)
        dyb = _mx(dy)
        dyb_ref[...] = dyb
        _acc(dd_acc, jnp.sum(dy * u, axis=0, keepdims=True))
        dx = _dot(dyb, cbdt_ref[...])
        xr, xi = dx[:, :S5L].reshape(tm // 8, 8, S5L), dx[:, S5L:].reshape(tm // 8, 8, S5L)
        for k, sh in enumerate((1, 2, 4)):
            xr, xi = _cmul_add(xr, xi, pw_ref[k, :, :S5L], pw_ref[k, :, S5L:], pltpu.roll(xr, 8 - sh, 1),
                               pltpu.roll(xi, 8 - sh, 1))
        cr, ci = carry_ref[0:1, :S5L], carry_ref[0:1, S5L:]
        for i in reversed(range(tm // 8)):
            tr, ti = _cmul_add(xr[i], xi[i], tab_ref[:, :S5L], tab_ref[:, S5L:], cr, ci)
            lamf_ref[8 * i:8 * i + 8, :S5L] = tr
            lamf_ref[8 * i:8 * i + 8, S5L:] = ti
            cr, ci = tr[0:1], ti[0:1]
        carry_ref[0:1, :S5L] = cr
        carry_ref[0:1, S5L:] = ci
        lam = lamf_ref[...]
        lamb = _mx(lam)
        lam_ref[...] = lamb
        du_ref[...] = dy * d_ref[...] + _dot(lamb, bbdt_ref[...])
        prev8 = halo_ref[...] * jnp.where(ctx.seq_first, 0.0, 1.0)
        xprev = _shift_rows(xst_ref[...], prev8, 1)
        lr, li = lam[:, :S5L], lam[:, S5L:]
        pr, pi = xprev[:, :S5L], xprev[:, S5L:]
        dar = jnp.sum(lr * pr + li * pi, axis=0, keepdims=True)
        dai = jnp.sum(li * pr - lr * pi, axis=0, keepdims=True)
        _acc(da_acc, jnp.concatenate([dar, dai], axis=1))

    return _rowcall(name, fn, t, tm, seq // tm, tiled=[dout, proj, xst, y_all], halos=[(xst, 8)],
                    res=[bbdt, cbdt, pwc, tabc, d5, w_glu, b_glu],
                    out_tiled=[_sds((t, SW)), _sds((t, 2 * S5L), MXU_DTYPE), _sds((t, SW), MXU_DTYPE),
                               _sds((t, SW), MXU_DTYPE), _sds((t, SW), MXU_DTYPE)],
                    out_acc=[_sds((1, 2 * S5L)), _sds((1, SW)), _sds((1, SW))],
                    scratch=[pltpu.VMEM((8, 2 * S5L), F32), pltpu.VMEM((tm, 2 * S5L), F32)], reverse=True, **carry)


def _out_fwd(name, yssd, ys5, x1, mod, w_out, lng, lnb, seq):
    t = x1.shape[0]
    tm = min(TM_WIDE, seq)

    def fn(ctx, tiled, halos, exs, res, outs, accs, exaccs, scr):
        ya_ref, yb_ref, x_ref = tiled
        mod_ref, = exs
        w_ref, g_ref, b_ref = res
        m = _dot(ya_ref[...], w_ref[0:SW, :]) + _dot(yb_ref[...], w_ref[SW:2 * SW, :])
        r = ALPHA * x_ref[...] + mod_ref[0, 5:6, :] * m
        outs[0][...] = _ln_fwd(r, g_ref[...], b_ref[...])
        outs[1][...] = r
        outs[2][...] = m

    return _rowcall(name, fn, t, tm, seq // tm, tiled=[yssd, ys5, x1], exs=[mod], res=[w_out, lng, lnb],
                    out_tiled=[_sds((t, D)), _sds((t, D)), _sds((t, D))])


def _out_bwd(name, dxo, r, m, mod, lng, w_out, seq, **carry):
    t = dxo.shape[0]
    tm = min(TM_WIDE, seq)

    def fn(ctx, tiled, halos, exs, res, outs, accs, exaccs, scr):
        dxo_ref, r_ref, m_ref = tiled
        mod_ref, = exs
        g_ref, w_ref = res
        dr, dgam, dbet = _ln_bwd(r_ref[...], g_ref[...], dxo_ref[...])
        outs[0][...] = dr
        _acc(accs[0], dgam)
        _acc(accs[1], dbet)
        _acc(exaccs[0].at[0], jnp.sum(dr * m_ref[...], axis=0, keepdims=True))
        dm = _mx(mod_ref[0, 5:6, :] * dr)
        outs[1][...] = dm
        dyc = _dot_nt(dm, w_ref[...])
        outs[2][...] = dyc[:, 0:SW]
        outs[3][...] = dyc[:, SW:2 * SW]

    b = mod.shape[0]
    return _rowcall(name, fn, t, tm, seq // tm, tiled=[dxo, r, m], exs=[mod], res=[lng, w_out],
                    out_tiled=[_sds((t, D)), _sds((t, D), MXU_DTYPE), _sds((t, SW)), _sds((t, SW))],
                    out_acc=[_sds((1, D)), _sds((1, D))], out_exacc=[_sds((b, 1, D))], **carry)


def _s5_discretise(a_re, a_im, log_dt, b_re, b_im):
    dt = jnp.exp(log_dt)[:, None]
    mag = jnp.exp(dt * a_re)
    ab_re, ab_im = mag * jnp.cos(dt * a_im), mag * jnp.sin(dt * a_im)
    den = a_re * a_re + a_im * a_im
    nr, ni = ab_re - 1.0, ab_im
    f_re, f_im = (nr * a_re + ni * a_im) / den, (ni * a_re - nr * a_im) / den
    bb_re = f_re[..., None] * b_re - f_im[..., None] * b_im
    bb_im = f_re[..., None] * b_im + f_im[..., None] * b_re
    return ab_re, ab_im, bb_re, bb_im


def _s5_tables(ab_re, ab_im):
    ar, ai = ab_re.reshape(1, S5L), ab_im.reshape(1, S5L)
    pows = [(ar, ai)]
    for _ in range(7):
        pr, pi = pows[-1]
        pows.append((pr * ar - pi * ai, pr * ai + pi * ar))

    def pack(rows, sign):
        return jnp.concatenate([jnp.concatenate([r for r, _ in rows], axis=0),
                                jnp.concatenate([sign * i for _, i in rows], axis=0)], axis=1)

    row = jnp.arange(8)[:, None]
    pw = jnp.stack([jnp.where(row >= sh, pack([pows[sh - 1]], 1.0), 0.0) for sh in (1, 2, 4)])
    pwc = jnp.stack([jnp.where(row < 8 - sh, pack([pows[sh - 1]], -1.0), 0.0) for sh in (1, 2, 4)])
    tab = pack(pows, 1.0)
    tabc = pack(pows[::-1], -1.0)
    return pw, tab, pwc, tabc


class _GradGroup:
    def __init__(self, tag, grads, place):
        self.tag, self.names, (self.half, self.chip) = tag, list(grads), place
        self.gsh = [_shard_major(g) if n in COL_SHARDED else g.reshape(4, g.shape[0] // 4, g.shape[1])
                    for n, g in grads.items()]

    def sibling(self):
        return _rs_sibling_stage(self.gsh)

    def chips(self, received):
        self.from_sibling = received
        return _rs_chips_stage(_rs_add(self.tag + "_add", self.gsh, received, self.half))

    def join(self, received):
        return _rs_join_stage(_rs_sum(self.tag + "_sum", self.gsh, self.from_sibling, received, self.chip, self.half))

    def result(self, joined):
        return dict(zip(self.names, joined))


def _hid(fn, *args, stage=None, **kw):
    if stage is None:
        return fn(*args, **kw), None
    return fn(*args, carry=stage, **kw)


def _local_step(x, tgt, mod, w, sp, seq, dist=None):
    t = x.shape[0]
    mxu = MXU_DTYPE
    big = {}

    def group(tag, grads):
        if dist is None:
            big.update(grads)
            return None
        return _GradGroup(tag, grads, dist[2])

    mid = {"carry_mid": 0.6} if dist else {}
    (h1, ab1, s1), got = _hid(_ffn_fwd_hidden, "ffn1_fwd_a", x, mod, 0, w["ffn1_w1t"], w["ffn1_w3t"], seq,
                              stage=dist and dist[0][0], **mid)
    if dist:
        w = {**w, **dist[0][1](got)}
    x1, r1, f1 = _ffn_fwd_out("ffn1_fwd_b", s1, x, mod, 0, w["ffn1_w2"], sp["ln1_g"], sp["ln1_b"], seq)
    h2, proj = _proj_fwd("proj_fwd", x1, mod, w["w_in"], seq)
    yssd, xpre, yraw, sprev = _ssd_fwd("ssd_fwd", proj, sp["conv_w"], sp["conv_b"], sp["dt_bias"], sp["a_log"],
                                       sp["d_rep"], sp["ssd_norm_w"], seq)
    (ab_re, ab_im, bb_re, bb_im), disc_vjp = jax.vjp(_s5_discretise, sp["s5_a_re"], sp["s5_a_im"], sp["s5_log_dt"],
                                                     sp["s5_b_re"], sp["s5_b_im"])
    eye = jnp.eye(S5G, dtype=F32)
    bbd = jnp.concatenate([jnp.einsum("gk,gph->ghkp", eye, bb_re).reshape(SW, S5L),
                           jnp.einsum("gk,gph->ghkp", eye, bb_im).reshape(SW, S5L)], axis=1).astype(mxu)
    cbd = jnp.concatenate([jnp.einsum("gk,ghp->gpkh", eye, sp["s5_c_re"]).reshape(S5L, SW),
                           -jnp.einsum("gk,ghp->gpkh", eye, sp["s5_c_im"]).reshape(S5L, SW)], axis=0).astype(mxu)
    bbdt = jnp.concatenate([jnp.einsum("gk,gph->kpgh", eye, bb_re).reshape(S5L, SW),
                            jnp.einsum("gk,gph->kpgh", eye, bb_im).reshape(S5L, SW)], axis=0).astype(mxu)
    cbdt = jnp.concatenate([jnp.einsum("gk,ghp->khgp", eye, sp["s5_c_re"]).reshape(SW, S5L),
                            -jnp.einsum("gk,ghp->khgp", eye, sp["s5_c_im"]).reshape(SW, S5L)], axis=1).astype(mxu)
    pw, tab, pwc, tabc = _s5_tables(lax.stop_gradient(ab_re), lax.stop_gradient(ab_im))
    (ys5, xst, y5, xstb, ub), got = _hid(_s5_fwd, "s5_fwd", proj, bbd, cbd, pw, tab, sp["s5_d"], w["w_glu"], sp["b_glu"],
                                         seq, stage=dist and dist[1][0], **mid)
    if dist:
        w = {**w, **dist[1][1](got)}
    x2, r2, m2 = _out_fwd("out_fwd", yssd, ys5, x1, mod, w["w_out"], sp["ln2_g"], sp["ln2_b"], seq)
    h3, ab3, s3 = _ffn_fwd_hidden("ffn2_fwd_a", x2, mod, 6, w["ffn2_w1t"], w["ffn2_w3t"], seq)
    x3, r3, f3, dy3, loss_acc = _ffn_fwd_out("ffn2_fwd_b", s3, x2, mod, 6, w["ffn2_w2"], sp["ln3_g"], sp["ln3_b"], seq, tgt=tgt)
    dx2, df3, dab3, dg3g, dg3b, dgate3, dsh3, dsc3 = _ffn_bwd("ffn2_bwd", dy3, r3, ab3, f3, x2, mod, 6, sp["ln3_g"],
                                                              w["ffn2_w2"], w["ffn2_w1t"], w["ffn2_w3t"], seq)
    grp = group("rs_ffn2", dict(ffn2_w1=_tn_matmul("ffn2_dw1", dab3, h3, FB, D, a_cols=(0, FF)),
                                ffn2_w3=_tn_matmul("ffn2_dw3", dab3, h3, FB, D, a_cols=(FF, FF)),
                                ffn2_w2=_tn_matmul("ffn2_dw2", s3, df3, FB, D)))
    (dr2, dm2, dyssd, dys5, dg2g, dg2b, dgate2), got = _hid(_out_bwd, "out_bwd", dx2, r2, m2, mod, sp["ln2_g"],
                                                          w["w_out"], seq, stage=grp and grp.sibling())
    g_w_out = jnp.concatenate([_tn_matmul("dw_out_a", yssd, dm2, SW, D), _tn_matmul("dw_out_b", ys5, dm2, SW, D)], axis=0)
    (du, lam, dy5b, g5b, dv5b, da5, dd5, dbglu), got = _hid(
        _s5_bwd, "s5_bwd", dys5, proj, xst, y5, bbdt, cbdt, pwc, tabc, sp["s5_d"], w["w_glu"], sp["b_glu"],
        seq, stage=grp and grp.chips(got))
    g_w_glu = _tn_matmul("dw_glu", g5b, dv5b, SW, SW)
    dbb = _diag_groups(_tn_diag("s5_db", lam, ub))
    dcc = _diag_groups(_tn_diag("s5_dc", xstb, dy5b))
    (dzx, ddt, dnw, ddl, dcw, dcb, ddtb, dalog), got = _hid(
        _ssd_bwd, "ssd_bwd", dyssd, proj, xpre, yraw, sprev, sp["conv_w"], sp["dt_bias"], sp["a_log"], sp["d_rep"],
        sp["ssd_norm_w"], seq, stage=grp and grp.join(got))
    if grp:
        big.update(grp.result(got))
    dx1, dproj, dsh2, dsc2 = _mod_bwd("proj_bwd", dzx, dr2, x1, mod, 3, [w["w_in"]], seq, extra=(du, ddt), nt=True)
    gwi = _tn_matmul("dw_in", h2, dproj, D, PW)
    grp = group("rs_mix", dict(w_in=jnp.concatenate([gwi[:, :1536], gwi[:, 2048:2056], gwi[:, 1536:2048]], axis=1),
                               w_glu=g_w_glu, w_out=g_w_out))
    (dr1, df1, dab1, dg1g, dg1b, dgate1), got = _hid(_ffn_bwd1, "ffn1_bwd1", dx1, r1, ab1, f1, mod, 0, sp["ln1_g"],
                                                          w["ffn1_w2"], seq, stage=grp and grp.sibling())
    g1, got = _hid(_tn_matmul, "ffn1_dw1", dab1, h1, FB, D, a_cols=(0, FF), stage=grp and grp.chips(got))
    g3, got = _hid(_tn_matmul, "ffn1_dw3", dab1, h1, FB, D, a_cols=(FF, FF), stage=grp and grp.join(got))
    if grp:
        big.update(grp.result(got))
    grp = group("rs_ffn1a", dict(ffn1_w1=g1, ffn1_w3=g3))
    g2, got = _hid(_tn_matmul, "ffn1_dw2", s1, df1, FB, D, stage=grp and grp.sibling())
    grp2 = group("rs_ffn1b", dict(ffn1_w2=g2))
    (dx0, dsh1, dsc1), got = _hid(_mod_bwd, "ffn1_bwd2", dab1, dr1, x, mod, 0, [w["ffn1_w1t"], w["ffn1_w3t"]], seq,
                                  stage=grp and _merge_stages(grp.chips(got), grp2.sibling()))
    if grp:
        n = len(grp.names)
        got = _run_stage("rs_ffn1_tail", _merge_stages(grp.join(got[:n]), grp2.chips(got[n:])))
        big.update(grp.result(got[:n]))
        big.update(grp2.result(_run_stage("rs_ffn1b_join", grp2.join(got[n:]))))
    dmod = jnp.concatenate([dsh1, dsc1, dgate1, dsh2, dsc2, dgate2, dsh3, dsc3, dgate3], axis=1)
    dc_re, dc_im = dcc[0].transpose(0, 2, 1), -dcc[1].transpose(0, 2, 1)
    g_a_re, g_a_im, g_log_dt, g_b_re, g_b_im = disc_vjp(
        (da5[:, :S5L].reshape(S5G, S5P), da5[:, S5L:].reshape(S5G, S5P), dbb[0], dbb[1]))
    small = dict(ln1_g=dg1g, ln1_b=dg1b, ln2_g=dg2g, ln2_b=dg2b, ln3_g=dg3g, ln3_b=dg3b, conv_w=dcw[0:4], conv_b=dcb,
                 dt_bias=ddtb[:, :NH], a_log=dalog[:, :NH], d_ssd=jnp.sum(ddl.reshape(NH, HP), axis=1).reshape(1, NH),
                 ssd_norm_w=dnw, s5_a_re=g_a_re, s5_a_im=g_a_im, s5_log_dt=g_log_dt, s5_b_re=g_b_re, s5_b_im=g_b_im,
                 s5_c_re=dc_re, s5_c_im=dc_im, s5_d=dd5, w_glu_b=dbglu)
    return loss_acc[0, 0], dx0, dmod, big, small


def _place():
    return lax.axis_index("x"), lax.axis_index("y"), lax.axis_index("c")


def _other_chips(x, y):
    return [(1 - x, y), (x, 1 - y), (1 - x, 1 - y)]


def _allgather8(name, a):
    r, n = a.shape

    def body(x_ref, out_ref, send_sems, recv_sems, local_sem):
        x, y, c = _place()
        me, sibling = (x, y, c), (x, y, 1 - c)
        chips = _other_chips(x, y)

        def rows(px, py, pc):
            return out_ref.at[pl.ds(pl.multiple_of((4 * px + 2 * py + pc) * r, 8), r), :]

        def copy(k, block, to, src=None):
            return pltpu.make_async_remote_copy(src_ref=rows(*block) if src is None else src, dst_ref=rows(*block),
                                                send_sem=send_sems.at[k], recv_sem=recv_sems.at[k], device_id=to,
                                                device_id_type=MESH_T)

        mine = pltpu.make_async_copy(x_ref, rows(*me), local_sem)
        mine.start()
        first = [copy(0, me, sibling, src=x_ref)]
        first += [copy(1 + j, me, (*chip, c), src=x_ref) for j, chip in enumerate(chips)]
        for cp in first:
            cp.start()
        passed = [copy(4 + j, (*chip, c), sibling) for j, chip in enumerate(chips)]
        for j, chip in enumerate(chips):
            copy(1 + j, (*chip, c), me).wait_recv()
            passed[j].start()
        copy(0, sibling, me).wait_recv()
        for j, chip in enumerate(chips):
            copy(4 + j, (*chip, 1 - c), me).wait_recv()
        for cp in first + passed:
            cp.wait_send()
        mine.wait()

    out = _pcall(body, name=name, out_shape=_sds((8 * r, n), a.dtype),
                 in_specs=[pl.BlockSpec(memory_space=pltpu.VMEM)], out_specs=pl.BlockSpec(memory_space=pltpu.VMEM),
                 scratch_shapes=[pltpu.SemaphoreType.DMA((7,)), pltpu.SemaphoreType.DMA((7,)), pltpu.SemaphoreType.DMA],
                 compiler_params=_cparams())(a)
    return out.reshape(8, r, n)


class _Stage:
    def __init__(self, ins, out_shapes, n_sems, start, finish, mid=None, aliases=None):
        self.ins, self.out_shapes, self.n_sems = list(ins), list(out_shapes), tuple(n_sems)
        self.start, self.mid, self.finish = start, mid, finish
        self.aliases = dict(aliases or {})

    def sem_shapes(self):
        return [pltpu.SemaphoreType.DMA((n,)) for n in self.n_sems]

    def run(self, refs, at_start=None, at_mid=None, at_finish=None, between=None):
        if between is None:
            for part in (self.start, self.mid, self.finish):
                if part is not None:
                    part(*refs)
            return
        pl.when(at_start)(functools.partial(self.start, *refs))
        if self.mid is not None:
            pl.when(at_mid)(functools.partial(self.mid, *refs))
        between()
        pl.when(at_finish)(functools.partial(self.finish, *refs))


def _merge_stages(a, b):
    n_in, n_out, n_sem = len(a.ins), len(a.out_shapes), len(a.n_sems)

    def both(fa, fb):
        def run(ins, outs, sems):
            fa(ins[:n_in], outs[:n_out], sems[:n_sem])
            fb(ins[n_in:], outs[n_out:], sems[n_sem:])
        return run

    aliases = {**a.aliases, **{n_in + k: n_out + v for k, v in b.aliases.items()}}
    return _Stage(a.ins + b.ins, a.out_shapes + b.out_shapes, a.n_sems + b.n_sems, both(a.start, b.start),
                  both(a.finish, b.finish), aliases=aliases)


def _run_stage(name, st):
    n_in, n_out = len(st.ins), len(st.out_shapes)

    def body(*refs):
        st.run((refs[:n_in], refs[n_in:n_in + n_out], refs[n_in + n_out:]))

    any_spec = pl.BlockSpec(memory_space=pl.ANY)
    return _pcall(body, name=name, out_shape=st.out_shapes, in_specs=[any_spec] * n_in, out_specs=[any_spec] * n_out,
                  input_output_aliases=st.aliases, scratch_shapes=st.sem_shapes(), compiler_params=_cparams())(*st.ins)


def _rows(ref_rows, half, align):
    hr = ref_rows // 2
    return pl.ds(pl.multiple_of(half * hr, align), hr)


def _gather_stage(shards):
    n = len(shards)
    pairs = [(i, j) for i in range(n) for j in range(3)]
    align = [32 // s.dtype.itemsize for s in shards]

    def env(ins, outs, sems):
        x, y, c = _place()
        chips = _other_chips(x, y)

        def copy(i, k, chip, half, to, src=None):
            dst = outs[i].at[2 * chip[0] + chip[1], _rows(shards[i].shape[0], half, align[i])]
            return pltpu.make_async_remote_copy(src_ref=dst if src is None else src, dst_ref=dst,
                                                send_sem=sems[0].at[6 * i + k], recv_sem=sems[1].at[6 * i + k],
                                                device_id=to, device_id_type=MESH_T)

        def first(i, j):
            return copy(i, j, (x, y), c, (*chips[j], c), src=ins[i].at[_rows(shards[i].shape[0], c, align[i])])

        def passed(i, j, half):
            return copy(i, 3 + j, chips[j], half, (x, y, 1 - c))

        def landed(i, j):
            return copy(i, j, chips[j], c, (x, y, 1 - c))

        return c, first, passed, landed

    def start(ins, outs, sems):
        c, first, passed, landed = env(ins, outs, sems)
        for i, j in pairs:
            first(i, j).start()

    def mid(ins, outs, sems):
        c, first, passed, landed = env(ins, outs, sems)
        for i, j in pairs:
            landed(i, j).wait_recv()
            passed(i, j, c).start()

    def finish(ins, outs, sems):
        c, first, passed, landed = env(ins, outs, sems)
        for i, j in pairs:
            passed(i, j, 1 - c).wait_recv()
        for i, j in pairs:
            first(i, j).wait_send()
            passed(i, j, c).wait_send()

    return _Stage(shards, [_sds((4,) + s.shape, s.dtype) for s in shards], (6 * n, 6 * n), start, finish, mid)


def _rs_sibling_stage(gs):
    n = len(gs)

    def copies(ins, outs, sems):
        x, y, c = _place()
        return [pltpu.make_async_remote_copy(src_ref=ins[i].at[:, _rows(gs[i].shape[1], 1 - c, 8)], dst_ref=outs[i],
                                             send_sem=sems[0].at[i], recv_sem=sems[1].at[i], device_id=(x, y, 1 - c),
                                             device_id_type=MESH_T) for i in range(n)]

    def start(*refs):
        for cp in copies(*refs):
            cp.start()

    def finish(*refs):
        for cp in copies(*refs):
            cp.wait()

    return _Stage(gs, [_sds((4, g.shape[1] // 2, g.shape[2]), g.dtype) for g in gs], (n, n), start, finish)


def _rs_chips_stage(hs):
    n = len(hs)

    def copies(ins, outs, sems):
        x, y, c = _place()
        return [pltpu.make_async_remote_copy(src_ref=ins[i].at[2 * chip[0] + chip[1]], dst_ref=outs[i].at[j],
                                             send_sem=sems[0].at[3 * i + j], recv_sem=sems[1].at[3 * i + j],
                                             device_id=(*chip, c), device_id_type=MESH_T)
                for i in range(n) for j, chip in enumerate(_other_chips(x, y))]

    def start(*refs):
        for cp in copies(*refs):
            cp.start()

    def finish(*refs):
        for cp in copies(*refs):
            cp.wait()

    return _Stage(hs, [_sds((3,) + h.shape[1:], h.dtype) for h in hs], (3 * n, 3 * n), start, finish)


def _rs_join_stage(fs):
    n = len(fs)

    def copy(outs, sems, i, half):
        x, y, c = _place()
        part = outs[i].at[_rows(fs[i].shape[0], c if half == "mine" else 1 - c, 8)]
        return pltpu.make_async_remote_copy(src_ref=part, dst_ref=part, send_sem=sems[0].at[i], recv_sem=sems[1].at[i],
                                            device_id=(x, y, 1 - c), device_id_type=MESH_T)

    def start(ins, outs, sems):
        for i in range(n):
            copy(outs, sems, i, "mine").start()

    def finish(ins, outs, sems):
        for i in range(n):
            copy(outs, sems, i, "theirs").wait_recv()
        for i in range(n):
            copy(outs, sems, i, "mine").wait_send()

    return _Stage(fs, [_sds(f.shape, f.dtype) for f in fs], (n, n), start, finish, aliases={i: i for i in range(n)})


def _row_block(r, cap=2048):
    b = min(r, cap)
    while r % b or b % 8:
        b -= 8
    return b


RS_SPLIT = 2


def _rs_add(name, gs, r1s, sel):
    n = len(gs)

    def body(sel_ref, *refs):
        g_refs, r_refs, b_refs = (refs[k * n:(k + 1) * n] for k in range(3))
        for i in range(n):
            b_refs[i][...] = (g_refs[i][...] + r_refs[i][...]).astype(BF16)

    def blk(g):
        return (1, g.shape[1] // 2 // RS_SPLIT, g.shape[2])

    here = lambda k, j, s: (k, j, 0)
    in_specs = [pl.BlockSpec(blk(g), lambda k, j, s: (k, s[0] * RS_SPLIT + j, 0)) for g in gs]
    in_specs += [pl.BlockSpec(blk(g), here) for g in gs]
    return _pcall(body, name=name, out_shape=[_sds((4, g.shape[1] // 2, g.shape[2]), BF16) for g in gs],
                  grid_spec=pltpu.PrefetchScalarGridSpec(num_scalar_prefetch=1, grid=(4, RS_SPLIT), in_specs=in_specs,
                                                         out_specs=[pl.BlockSpec(blk(g), here) for g in gs]),
                  compiler_params=_cparams(("parallel", "parallel")))(sel.reshape(1).astype(jnp.int32), *gs, *r1s)


def _rs_sum(name, gs, r1s, r2s, chip, half):
    n = len(gs)

    def body(sel_ref, *refs):
        g_refs, h_refs, r_refs, o_refs = (refs[k * n:(k + 1) * n] for k in range(4))
        for i in range(n):
            r = r_refs[i]
            o_refs[i][...] = (((g_refs[i][0] + h_refs[i][0]) + r[0].astype(F32)) + r[1].astype(F32)) + r[2].astype(F32)

    def rows(g):
        return g.shape[1] // 2 // RS_SPLIT

    in_specs = [pl.BlockSpec((1, rows(g), g.shape[2]), lambda j, s: (s[0], s[1] * RS_SPLIT + j, 0)) for g in gs]
    in_specs += [pl.BlockSpec((1, rows(g), g.shape[2]), lambda j, s: (s[0], j, 0)) for g in gs]
    in_specs += [pl.BlockSpec((3, rows(g), g.shape[2]), lambda j, s: (0, j, 0)) for g in gs]
    sel = jnp.stack([chip, half]).astype(jnp.int32)
    return _pcall(body, name=name, out_shape=[_sds(g.shape[1:]) for g in gs],
                  grid_spec=pltpu.PrefetchScalarGridSpec(
                      num_scalar_prefetch=1, grid=(RS_SPLIT,), in_specs=in_specs,
                      out_specs=[pl.BlockSpec((rows(g), g.shape[2]), lambda j, s: (s[1] * RS_SPLIT + j, 0)) for g in gs]),
                  compiler_params=_cparams(("parallel",)))(sel, *gs, *r1s, *r2s)


def _sum8(name, a):
    _, r, n = a.shape
    br = _row_block(r)

    def body(a_ref, o_ref):
        acc = a_ref[0]
        for k in range(1, 8):
            acc = acc + a_ref[k]
        o_ref[...] = acc

    return _pcall(body, name=name, out_shape=_sds((r, n)), grid=(r // br,),
                  in_specs=[pl.BlockSpec((8, br, n), lambda j: (0, j, 0))], out_specs=pl.BlockSpec((br, n), lambda j: (j, 0)),
                  compiler_params=_cparams(("parallel",)))(a)


def _adamw(name, ws, gs, ms, vs, nblk):
    n = len(ws)

    def body(*refs):
        w_refs, g_refs, m_refs, v_refs, d_refs, nm_refs, nv_refs = (refs[k * n:(k + 1) * n] for k in range(7))
        for i in range(n):
            gv = g_refs[i][...]
            nm = ADAM_B1 * m_refs[i][...] + (1.0 - ADAM_B1) * gv
            nv = ADAM_B2 * v_refs[i][...] + (1.0 - ADAM_B2) * (gv * gv)
            nm_refs[i][...] = nm
            nv_refs[i][...] = nv
            m_hat = nm / (1.0 - ADAM_B1 ** ADAM_STEP)
            v_hat = nv / (1.0 - ADAM_B2 ** ADAM_STEP)
            d_refs[i][...] = -ADAM_LR * (m_hat / (jnp.sqrt(v_hat) + ADAM_EPS) + ADAM_WD * w_refs[i][...])

    specs = [pl.BlockSpec((w.shape[0] // nblk, w.shape[1]), lambda j: (j, 0)) for w in ws]
    outs = _pcall(body, name=name, out_shape=[_sds(w.shape) for w in ws] * 3, grid=(nblk,), in_specs=specs * 4,
                  out_specs=specs * 3, compiler_params=_cparams(("parallel",)))(*ws, *gs, *ms, *vs)
    return outs[:n], outs[n:2 * n], outs[2 * n:]


ADA_COLS = 2304
ADA_BLK = 768


def _ada_fwd(name, c_all, w_shard, b_cols):
    nb = c_all.shape[0]

    def body(c_ref, w_ref, b_ref, o_ref):
        cv = c_ref[...]
        cs = _mx(cv * _sigmoid(cv))
        o_ref[...] = _dot(cs, _mx(w_ref[...])) + b_ref[...]

    return _pcall(body, name=name, out_shape=_sds((nb, ADA_COLS)), grid=(ADA_COLS // ADA_BLK,),
                  in_specs=[pl.BlockSpec((nb, D), lambda j: (0, 0)), pl.BlockSpec((D, ADA_BLK), lambda j: (0, j)),
                            pl.BlockSpec((1, ADA_BLK), lambda j: (0, j))],
                  out_specs=pl.BlockSpec((nb, ADA_BLK), lambda j: (0, j)),
                  compiler_params=_cparams(("parallel",)))(c_all, w_shard, b_cols)


def _ada_bwd(name, c_all, dmod_cols, dmod_all):
    nb = c_all.shape[0]

    def body(c_ref, dc_ref, da_ref, gw_ref, gb_ref):
        cv = c_ref[...]
        cs = _mx(cv * _sigmoid(cv))
        gw_ref[...] = lax.dot_general(cs, _mx(dc_ref[...]), (((0,), (0,)), ((), ())), preferred_element_type=F32)

        @pl.when(pl.program_id(0) == 0)
        def _():
            gb_ref[...] = jnp.sum(da_ref[...], axis=0, keepdims=True)

    return _pcall(body, name=name, out_shape=[_sds((D, ADA_COLS)), _sds((1, 9 * D))], grid=(ADA_COLS // ADA_BLK,),
                  in_specs=[pl.BlockSpec((nb, D), lambda j: (0, 0)), pl.BlockSpec((nb, ADA_BLK), lambda j: (0, j)),
                            pl.BlockSpec((nb, 9 * D), lambda j: (0, 0))],
                  out_specs=[pl.BlockSpec((D, ADA_BLK), lambda j: (0, j)), pl.BlockSpec((1, 9 * D), lambda j: (0, 0))],
                  compiler_params=_cparams(("arbitrary",)))(c_all, dmod_cols, dmod_all)


BIG = ("ffn1_w1", "ffn1_w3", "ffn1_w2", "w_in", "w_glu", "w_out", "ffn2_w1", "ffn2_w3", "ffn2_w2")
COL_SHARDED = ("w_in",)
TRANSPOSED = ("ffn1_w1", "ffn1_w3", "ffn2_w1", "ffn2_w3")
SMALL = ("b_ada", "ln1_g", "ln1_b", "conv_w", "conv_b", "dt_bias", "a_log", "d_ssd", "ssd_norm_w", "s5_a_re", "s5_a_im",
         "s5_log_dt", "s5_b_re", "s5_b_im", "s5_c_re", "s5_c_im", "s5_d", "b_glu", "ln2_g", "ln2_b", "ln3_g", "ln3_b")
WEIGHTS = ("w_ada", "b_ada", "ffn1_w1", "ffn1_w3", "ffn1_w2", "ln1_g", "ln1_b", "w_in", "conv_w", "conv_b", "dt_bias",
           "a_log", "d_ssd", "ssd_norm_w", "s5_a_re", "s5_a_im", "s5_log_dt", "s5_b_re", "s5_b_im", "s5_c_re", "s5_c_im",
           "s5_d", "w_glu", "b_glu", "w_out", "ln2_g", "ln2_b", "ffn2_w1", "ffn2_w3", "ffn2_w2", "ln3_g", "ln3_b")
BIG_PAD = 2 * 1024 * 128


def _pack(arrs, mult, axis_keep=0):
    lead = arrs[0].shape[:axis_keep]
    flat = jnp.concatenate([a.reshape(lead + (-1,)) for a in arrs], axis=axis_keep)
    pad = (-flat.shape[-1]) % mult
    if pad:
        flat = jnp.concatenate([flat, jnp.zeros(lead + (pad,), flat.dtype)], axis=axis_keep)
    return flat


def _unpack(flat, shapes):
    out, off = [], 0
    for s in shapes:
        size = math.prod(s)
        out.append(flat[..., off:off + size].reshape(flat.shape[:-1] + tuple(s)))
        off += size
    return out


def _shard_major(a):
    rows, cols = a.shape
    return a.reshape(rows, 4, cols // 4).transpose(1, 0, 2)


def _from_shard_major(a):
    _, rows, w = a.shape
    return a.transpose(1, 0, 2).reshape(rows, 4 * w)


def kernel(x, c, w_ada, b_ada, ffn1_w1, ffn1_w3, ffn1_w2, ln1_g, ln1_b, w_in, conv_w, conv_b, dt_bias, a_log, d_ssd, ssd_norm_w, s5_a_re, s5_a_im, s5_log_dt, s5_b_re, s5_b_im, s5_c_re, s5_c_im, s5_d, w_glu, b_glu, w_out, ln2_g, ln2_b, ffn2_w1, ffn2_w3, ffn2_w2, ln3_g, ln3_b, loss_target, m_w_ada, m_b_ada, m_ffn1_w1, m_ffn1_w3, m_ffn1_w2, m_ln1_g, m_ln1_b, m_w_in, m_conv_w, m_conv_b, m_dt_bias, m_a_log, m_d_ssd, m_ssd_norm_w, m_s5_a_re, m_s5_a_im, m_s5_log_dt, m_s5_b_re, m_s5_b_im, m_s5_c_re, m_s5_c_im, m_s5_d, m_w_glu, m_b_glu, m_w_out, m_ln2_g, m_ln2_b, m_ffn2_w1, m_ffn2_w3, m_ffn2_w2, m_ln3_g, m_ln3_b, v_w_ada, v_b_ada, v_ffn1_w1, v_ffn1_w3, v_ffn1_w2, v_ln1_g, v_ln1_b, v_w_in, v_conv_w, v_conv_b, v_dt_bias, v_a_log, v_d_ssd, v_ssd_norm_w, v_s5_a_re, v_s5_a_im, v_s5_log_dt, v_s5_b_re, v_s5_b_im, v_s5_c_re, v_s5_c_im, v_s5_d, v_w_glu, v_b_glu, v_w_out, v_ln2_g, v_ln2_b, v_ffn2_w1, v_ffn2_w3, v_ffn2_w2, v_ln3_g, v_ln3_b):
    a = dict(locals())
    xi, yi, ci = _place()
    chip = 2 * xi + yi
    dev = 2 * chip + ci
    nb, seq, _ = x.shape
    t = nb * seq
    ndev = 8

    c_rows = nb * D // 128
    c_cw = _allgather8("gather_c", jnp.concatenate([c.reshape(c_rows, 128), conv_w.reshape(-1, 128)], axis=0))
    c_all = c_cw[:, :c_rows].reshape(ndev * nb, D)
    b_cols = lax.dynamic_slice(b_ada, (0, chip * ADA_COLS), (1, ADA_COLS))
    mod_part = _ada_fwd("ada_fwd", c_all, w_ada[0], b_cols)

    def nat(n):
        return jnp.swapaxes(a[n], 1, 2)[0] if n[-7:] in TRANSPOSED else a[n][0]

    def gather(names, extra=()):
        own = [nat(n).astype(MXU_DTYPE) for n in names]

        def weights(pieces):
            w = {}
            for n, mine, piece in zip(names, own, pieces):
                piece = lax.dynamic_update_slice(piece, mine[None], (chip, 0, 0))
                if n in COL_SHARDED:
                    wi = _from_shard_major(piece)
                    w[n] = jnp.concatenate([wi[:, :1536], wi[:, 1544:2056], wi[:, 1536:1544],
                                            jnp.zeros((D, 120), wi.dtype)], axis=1)
                else:
                    w[n + "t" if n in TRANSPOSED else n] = piece.reshape(-1, piece.shape[-1])
            return w

        return _gather_stage(own + list(extra)), weights

    first_stage, first_weights = gather(("ffn1_w1", "ffn1_w3"), extra=[mod_part])
    got = _run_stage("gather_w_first", first_stage)
    w = first_weights(got[:2])
    mod_all = lax.dynamic_update_slice(got[2], mod_part[None], (chip, 0, 0)).transpose(1, 0, 2).reshape(ndev * nb, 9 * D)
    mod = lax.dynamic_slice(mod_all, (nb * dev, 0), (nb, 9 * D)).reshape(nb, 9, D)

    conv_full = _from_shard_major(c_cw[0::2, c_rows:].reshape(4, 4, 256))
    pad8 = lambda v: jnp.concatenate([v.reshape(1, NH), jnp.zeros((1, 128 - NH), F32)], axis=1)
    sp = dict(ln1_g=ln1_g, ln1_b=ln1_b, ln2_g=ln2_g, ln2_b=ln2_b, ln3_g=ln3_g, ln3_b=ln3_b, conv_w=conv_full,
              conv_b=conv_b, dt_bias=pad8(dt_bias), a_log=pad8(a_log), d_rep=jnp.repeat(d_ssd[0], HP)[None],
              ssd_norm_w=ssd_norm_w, s5_a_re=s5_a_re[0], s5_a_im=s5_a_im[0], s5_log_dt=s5_log_dt[0], s5_b_re=s5_b_re[0],
              s5_b_im=s5_b_im[0], s5_c_re=s5_c_re[0], s5_c_im=s5_c_im[0], s5_d=s5_d, b_glu=b_glu)

    lsum, dx0, dmod, gbig, small = _local_step(x.reshape(t, D), loss_target.reshape(t, D), mod, w, sp, seq,
                                               dist=(gather(("ffn1_w2", "w_in", "w_glu", "w_out")),
                                                     gather(("ffn2_w1", "ffn2_w3", "ffn2_w2")), (ci, chip)))

    snames = [n for n in SMALL if n != "b_ada"]
    sgrad = dict(small)
    sgrad["b_glu"] = small["w_glu_b"]
    dm_rows = nb * 9 * D // 128
    tail = _allgather8("gather_small", jnp.concatenate(
        [dmod.reshape(dm_rows, 128),
         _pack([(lsum * (0.5 / D)).reshape(1)] + [sgrad[n] for n in snames], 1024).reshape(-1, 128)], axis=0))
    dmod_all = tail[:, :dm_rows].reshape(ndev * nb, 9 * D)
    dmod_cols = lax.dynamic_slice(dmod_all, (0, chip * ADA_COLS), (ndev * nb, ADA_COLS))
    g_w_ada, g_b_ada = _ada_bwd("ada_bwd", c_all, dmod_cols, dmod_all)

    gbig["w_ada"] = g_w_ada

    outs = {}
    for call, names in (("adamw_a", ("ffn1_w1", "ffn1_w3", "ffn1_w2", "w_in", "w_glu", "w_out")),
                        ("adamw_b", ("ffn2_w1", "ffn2_w3", "ffn2_w2", "w_ada"))):
        res = _adamw(call, [nat(n) for n in names], [gbig[n] for n in names], [nat("m_" + n) for n in names],
                     [nat("v_" + n) for n in names], 8)
        for kind, arrs in zip(("grad", "delta", "new_m", "new_v"), ([gbig[n] for n in names],) + tuple(res)):
            for n, arr in zip(names, arrs):
                outs[kind, n] = (arr.T if n in TRANSPOSED else arr)[None]

    ssum = _sum8("small_sum", tail)[dm_rows:].reshape(-1)

    def view2d(u):
        s = u.shape[1:]
        return u.reshape((1, s[0]) if len(s) == 1 else (-1, s[-1]))

    vshape = {n: view2d(a[n]).shape for n in SMALL}
    loss, *sums = _unpack(ssum, [()] + [vshape[n] if n != "conv_w" else (4, D) for n in snames])
    gsm = dict(zip(snames, sums))
    gsm["conv_w"] = lax.dynamic_slice(gsm["conv_w"], (0, chip * 256), (4, 256))
    gsm["b_ada"] = g_b_ada
    res = _adamw("adamw_small", [view2d(a[n]) for n in SMALL], [gsm[n] for n in SMALL],
                 [view2d(a["m_" + n]) for n in SMALL], [view2d(a["v_" + n]) for n in SMALL], 1)
    for kind, arrs in zip(("grad", "delta", "new_m", "new_v"), ([gsm[n] for n in SMALL],) + tuple(res)):
        for n, arr in zip(SMALL, arrs):
            outs[kind, n] = arr.reshape(a[n].shape)

    res = [loss, dx0.reshape(nb, seq, D)]
    for kind in ("grad", "delta", "new_m", "new_v"):
        res += [outs[kind, n] for n in WEIGHTS]
    return tuple(res)
```
